```python
import math
import jax, jax.numpy as jnp
from jax import lax
import numpy as np

D_MODEL = 1024
BATCH = 8
SEQ = 2048
DEPTH = 1
DEC_BATCH = 128
DEC_SEQ = 1
PAST_LEN = 16384
PAGE_SIZE = 128

N_META = 16
MIX_WIDTH = D_MODEL
SSM_WIDTH = MIX_WIDTH // 2
CONV_WIDTH = MIX_WIDTH - SSM_WIDTH
SSM_GROUP_CH = 16
SSM_GROUPS = SSM_WIDTH // SSM_GROUP_CH
SSM_STATE = 64
CONV_HEADS = 8
CONV_HEAD_DIM = CONV_WIDTH // CONV_HEADS
CONV_K = 3
N_EXPERTS = 32
TOP_K = 4
D_FF = D_MODEL
SWIGLU_LIMIT = 7.0
SWIGLU_ALPHA = 1.702
MOE_BLOCK = 128
EPS = 1e-5
DT_MIN = 1e-3
DT_MAX = 1e-1

kernel_name = "hymba_s5_shortconv_moe_step"


def rmsnorm(x, g):
    xf = x.astype(jnp.float32)
    xf = xf * lax.rsqrt(jnp.mean(xf * xf, axis=-1, keepdims=True) + EPS)
    return xf.astype(x.dtype) * g


def _lin_combine(left, right):
    a_l, b_l = left
    a_r, b_r = right
    return a_r * a_l, a_r * b_l + b_r


def ssm_mixer(u, s0_re, s0_im, a_re, a_im, log_dt, b_re, b_im, c_re, c_im, d_skip, w_glu, b_glu):
    bt, L, _ = u.shape
    f32 = jnp.float32
    uf = u.astype(f32).reshape(bt, L, SSM_GROUPS, SSM_GROUP_CH)
    A = lax.complex(a_re.astype(f32), a_im.astype(f32))
    dtA = A * jnp.exp(log_dt.astype(f32))[:, None]
    a_bar = jnp.exp(dtA)
    b_bar = ((a_bar - 1.0) / A)[:, :, None] * lax.complex(b_re.astype(f32), b_im.astype(f32))
    bu = jnp.einsum('gnh,blgh->blgn', b_bar, uf.astype(jnp.complex64))
    decay = jnp.broadcast_to(a_bar, bu.shape)
    _, s = lax.associative_scan(_lin_combine, (decay, bu), axis=1)
    steps = jnp.arange(1, L + 1, dtype=f32)[:, None, None]
    s0 = lax.complex(s0_re.astype(f32), s0_im.astype(f32))
    s = s + jnp.exp(dtA[None] * steps)[None] * s0[:, None]
    Cc = lax.complex(c_re.astype(f32), c_im.astype(f32))
    y = jnp.einsum('ghn,blgn->blgh', Cc, s).real + d_skip.astype(f32).reshape(SSM_GROUPS, SSM_GROUP_CH) * uf
    y = jax.nn.gelu(y.reshape(bt, L, SSM_WIDTH))
    out = y * jax.nn.sigmoid(y @ w_glu.astype(f32) + b_glu.astype(f32))
    last = s[:, -1]
    return out.astype(u.dtype), jnp.real(last), jnp.imag(last)


def causal_conv(z, buf, w):
    L = z.shape[1]
    z_ext = jnp.concatenate([buf.astype(z.dtype), z], axis=1)
    y = w[0] * z_ext[:, 0:L]
    for k in range(1, CONV_K):
        y = y + w[k] * z_ext[:, k:k + L]
    return y, z_ext[:, L:]


def moe(x2, w_router, b_router, w_gate_up, b_gate_up, w_down, b_down):
    T = x2.shape[0]
    logits = (x2 @ w_router + b_router).astype(jnp.float32)
    top_v, top_i = lax.top_k(logits, TOP_K)
    gates = jax.nn.softmax(top_v, axis=-1)
    n_assign = T * TOP_K
    flat_e = top_i.reshape(-1).astype(jnp.int32)
    order = jnp.argsort(flat_e)
    sorted_e = flat_e[order]
    counts = jnp.bincount(flat_e, length=N_EXPERTS)
    start = jnp.cumsum(counts) - counts
    padded = ((counts + MOE_BLOCK - 1) // MOE_BLOCK) * MOE_BLOCK
    pend = jnp.cumsum(padded)
    pstart = pend - padded
    rank = jnp.arange(n_assign, dtype=jnp.int32) - start[sorted_e]
    dest_sorted = (pstart[sorted_e] + rank).astype(jnp.int32)
    n_blocks = (n_assign + N_EXPERTS * (MOE_BLOCK - 1) + MOE_BLOCK - 1) // MOE_BLOCK
    cap = n_blocks * MOE_BLOCK
    tok_sorted = (order // TOP_K).astype(jnp.int32)
    slot_tok = jnp.full((cap,), T, jnp.int32).at[dest_sorted].set(tok_sorted)
    blk_e = jnp.minimum(jnp.searchsorted(pend, jnp.arange(n_blocks) * MOE_BLOCK, side='right'), N_EXPERTS - 1)
    x_pad = jnp.concatenate([x2, jnp.zeros((1, x2.shape[1]), x2.dtype)], axis=0)
    xb = x_pad[slot_tok].reshape(n_blocks, MOE_BLOCK, x2.shape[1])

    def expert_block(args):
        xe, e = args
        gu = xe @ w_gate_up[e] + b_gate_up[e]
        gate, up = gu[:, :D_FF], gu[:, D_FF:]
        gate = jnp.minimum(gate, SWIGLU_LIMIT)
        up = jnp.clip(up, -SWIGLU_LIMIT, SWIGLU_LIMIT)
        h = gate * jax.nn.sigmoid(SWIGLU_ALPHA * gate) * (up + 1.0)
        return h @ w_down[e] + b_down[e]

    yb = lax.map(expert_block, (xb, blk_e)).reshape(cap, x2.shape[1])
    dest = jnp.zeros((n_assign,), jnp.int32).at[order].set(dest_sorted)
    y = yb[dest].reshape(T, TOP_K, x2.shape[1])
    return jnp.sum(y * gates[..., None].astype(y.dtype), axis=1)


def trunk(x, s_re, s_im, conv_buf, norm_mix, w_in, ssm_a_re, ssm_a_im, ssm_log_dt, ssm_b_re, ssm_b_im,
          ssm_c_re, ssm_c_im, ssm_d, w_glu, b_glu, conv_w, norm_out_ssm, norm_out_conv, w_out,
          norm_ffn, w_router, b_router, w_gate_up, b_gate_up, w_down, b_down, norm_final):
    new_re, new_im, new_buf = [], [], []
    for l in range(DEPTH):
        h = rmsnorm(x, norm_mix[l])
        proj = h @ w_in[l]
        u = proj[..., :SSM_WIDTH]
        zc = proj[..., SSM_WIDTH:SSM_WIDTH + CONV_WIDTH]
        gb = proj[..., SSM_WIDTH + CONV_WIDTH:SSM_WIDTH + 2 * CONV_WIDTH]
        gc = proj[..., SSM_WIDTH + 2 * CONV_WIDTH:]
        y_ssm, r_re, r_im = ssm_mixer(u, s_re[l], s_im[l], ssm_a_re[l], ssm_a_im[l], ssm_log_dt[l],
                                      ssm_b_re[l], ssm_b_im[l], ssm_c_re[l], ssm_c_im[l], ssm_d[l],
                                      w_glu[l], b_glu[l])
        y_conv, r_buf = causal_conv(gc * zc, conv_buf[l], conv_w[l])
        y_conv = gb * y_conv
        mix = jnp.concatenate([rmsnorm(y_ssm, norm_out_ssm[l]), rmsnorm(y_conv, norm_out_conv[l])], axis=-1)
        x = x + mix @ w_out[l]
        hf = rmsnorm(x, norm_ffn[l])
        bt, L, D = hf.shape
        x = x + moe(hf.reshape(bt * L, D), w_router[l], b_router[l], w_gate_up[l], b_gate_up[l],
                    w_down[l], b_down[l]).reshape(bt, L, D)
        new_re.append(r_re)
        new_im.append(r_im)
        new_buf.append(r_buf)
    return rmsnorm(x, norm_final), jnp.stack(new_re), jnp.stack(new_im), jnp.stack(new_buf)


def setup_inputs(seed: int = 0) -> dict:
    key = jax.random.key(seed)
    ks = jax.random.split(key, 32)
    f32 = jnp.float32
    nrm = lambda k, shape, s: (jax.random.normal(k, shape, f32) * s)
    n_idx = jnp.arange(SSM_STATE, dtype=f32)
    a_re = -0.5 + nrm(ks[5], (DEPTH, SSM_GROUPS, SSM_STATE), 0.01)
    a_im = math.pi * n_idx[None, None] + nrm(ks[6], (DEPTH, SSM_GROUPS, SSM_STATE), 0.01)
    log_dt = jax.random.uniform(ks[7], (DEPTH, SSM_GROUPS), f32, math.log(DT_MIN), math.log(DT_MAX))
    return {
        "x_prompt": nrm(ks[0], (BATCH, SEQ, D_MODEL), 1.0),
        "x_sample": nrm(ks[1], (DEC_BATCH, DEC_SEQ, D_MODEL), 1.0),
        "state_ssm_re": nrm(ks[2], (DEPTH, DEC_BATCH, SSM_GROUPS, SSM_STATE), 0.5),
        "state_ssm_im": nrm(ks[3], (DEPTH, DEC_BATCH, SSM_GROUPS, SSM_STATE), 0.5),
        "state_conv": nrm(ks[4], (DEPTH, DEC_BATCH, CONV_K - 1, CONV_WIDTH), 1.0),
        "meta_tokens": nrm(ks[8], (N_META, D_MODEL), 1.0),
        "norm_mix": 1.0 + nrm(ks[9], (DEPTH, D_MODEL), 0.02),
        "w_in": nrm(ks[10], (DEPTH, D_MODEL, SSM_WIDTH + 3 * CONV_WIDTH), D_MODEL ** -0.5),
        "ssm_a_re": a_re,
        "ssm_a_im": a_im,
        "ssm_log_dt": log_dt,
        "ssm_b_re": nrm(ks[11], (DEPTH, SSM_GROUPS, SSM_STATE, SSM_GROUP_CH), (2 * SSM_GROUP_CH) ** -0.5),
        "ssm_b_im": nrm(ks[12], (DEPTH, SSM_GROUPS, SSM_STATE, SSM_GROUP_CH), (2 * SSM_GROUP_CH) ** -0.5),
        "ssm_c_re": nrm(ks[13], (DEPTH, SSM_GROUPS, SSM_GROUP_CH, SSM_STATE), (2 * SSM_STATE) ** -0.5),
        "ssm_c_im": nrm(ks[14], (DEPTH, SSM_GROUPS, SSM_GROUP_CH, SSM_STATE), (2 * SSM_STATE) ** -0.5),
        "ssm_d": nrm(ks[15], (DEPTH, SSM_WIDTH), 1.0),
        "w_glu": nrm(ks[16], (DEPTH, SSM_WIDTH, SSM_WIDTH), SSM_WIDTH ** -0.5),
        "b_glu": nrm(ks[17], (DEPTH, SSM_WIDTH), 0.01),
        "conv_w": nrm(ks[18], (DEPTH, CONV_K, CONV_WIDTH), CONV_K ** -0.5),
        "norm_out_ssm": 1.0 + nrm(ks[19], (DEPTH, SSM_WIDTH), 0.02),
        "norm_out_conv": 1.0 + nrm(ks[20], (DEPTH, CONV_WIDTH), 0.02),
        "w_out": nrm(ks[21], (DEPTH, MIX_WIDTH, D_MODEL), MIX_WIDTH ** -0.5),
        "norm_ffn": 1.0 + nrm(ks[22], (DEPTH, D_MODEL), 0.02),
        "w_router": nrm(ks[23], (DEPTH, D_MODEL, N_EXPERTS), D_MODEL ** -0.5),
        "b_router": nrm(ks[24], (DEPTH, N_EXPERTS), 0.01),
        "w_gate_up": nrm(ks[25], (DEPTH, N_EXPERTS, D_MODEL, 2 * D_FF), D_MODEL ** -0.5),
        "b_gate_up": nrm(ks[26], (DEPTH, N_EXPERTS, 2 * D_FF), 0.01),
        "w_down": nrm(ks[27], (DEPTH, N_EXPERTS, D_FF, D_MODEL), D_FF ** -0.5),
        "b_down": nrm(ks[28], (DEPTH, N_EXPERTS, D_MODEL), 0.01),
        "norm_final": 1.0 + nrm(ks[29], (D_MODEL,), 0.02),
    }


def reference(x_prompt, x_sample, state_ssm_re, state_ssm_im, state_conv, meta_tokens, norm_mix, w_in,
              ssm_a_re, ssm_a_im, ssm_log_dt, ssm_b_re, ssm_b_im, ssm_c_re, ssm_c_im, ssm_d, w_glu, b_glu,
              conv_w, norm_out_ssm, norm_out_conv, w_out, norm_ffn, w_router, b_router, w_gate_up,
              b_gate_up, w_down, b_down, norm_final):
    bp = x_prompt.shape[0]
    meta = jnp.broadcast_to(meta_tokens[None].astype(x_prompt.dtype), (bp, N_META, D_MODEL))
    xp = jnp.concatenate([meta, x_prompt], axis=1)
    zeros_re = jnp.zeros((DEPTH, bp, SSM_GROUPS, SSM_STATE), jnp.float32)
    zeros_buf = jnp.zeros((DEPTH, bp, CONV_K - 1, CONV_WIDTH), x_prompt.dtype)
    y_p, p_re, p_im, p_conv = trunk(xp, zeros_re, zeros_re, zeros_buf, norm_mix, w_in, ssm_a_re, ssm_a_im,
                                    ssm_log_dt, ssm_b_re, ssm_b_im, ssm_c_re, ssm_c_im, ssm_d, w_glu, b_glu,
                                    conv_w, norm_out_ssm, norm_out_conv, w_out, norm_ffn, w_router, b_router,
                                    w_gate_up, b_gate_up, w_down, b_down, norm_final)
    y_prompt = y_p[:, N_META:]
    y_sample, s_re, s_im, s_conv = trunk(x_sample, state_ssm_re, state_ssm_im, state_conv, norm_mix, w_in,
                                         ssm_a_re, ssm_a_im, ssm_log_dt, ssm_b_re, ssm_b_im, ssm_c_re,
                                         ssm_c_im, ssm_d, w_glu, b_glu, conv_w, norm_out_ssm, norm_out_conv,
                                         w_out, norm_ffn, w_router, b_router, w_gate_up, b_gate_up, w_down,
                                         b_down, norm_final)
    return (y_prompt, y_sample, p_re, p_im, p_conv, s_re, s_im, s_conv)
```

```python
import functools
import math

import jax
import jax.numpy as jnp
from jax import lax
from jax.experimental import pallas as pl
from jax.experimental.pallas import tpu as pltpu

F32 = jnp.float32
BF16 = jnp.bfloat16
EPS = 1e-5
N_META = 16
CHUNK = 16
LANE = 128
TOP_K = 4
SWIGLU_LIMIT = 7.0
SWIGLU_ALPHA = 1.702
MOE_ROWS = 256
MIB = 1024 * 1024


def _rms(x, g):
    return x * lax.rsqrt(jnp.mean(x * x, axis=-1, keepdims=True) + EPS) * g


def _gelu_tanh(x):
    c = math.sqrt(2.0 / math.pi)
    return 0.5 * x * (1.0 + jnp.tanh(c * (x + 0.044715 * (x * x * x))))


def _params(sem, vmem_mib):
    return pltpu.CompilerParams(dimension_semantics=sem, vmem_limit_bytes=vmem_mib * MIB)


def _small_front_kernel(x_ref, nmix_ref, win_ref, s0r_ref, s0i_ref, b0_ref, b1_ref, cw_ref,
                        bdb_ref, cm_ref, abr_ref, abi_ref, nconv_ref,
                        u_ref, z_ref, y_ref, ycn_ref, sr_ref, si_ref):
    ns, nst = s0r_ref.shape
    cw = u_ref.shape[1]
    h = _rms(x_ref[...], nmix_ref[...]).astype(BF16)
    proj = jnp.dot(h, win_ref[...], preferred_element_type=F32)
    u = proj[:, 0:cw]
    zc = proj[:, cw:2 * cw]
    gb = proj[:, 2 * cw:3 * cw]
    gc = proj[:, 3 * cw:4 * cw]
    z = gc * zc
    u_ref[...] = u
    z_ref[...] = z
    bu = jnp.dot(u[:ns].astype(BF16), bdb_ref[...], preferred_element_type=F32)
    abr = abr_ref[...]
    abi = abi_ref[...]
    s0r = s0r_ref[...]
    s0i = s0i_ref[...]
    sr = abr * s0r - abi * s0i + bu[:, :nst]
    si = abr * s0i + abi * s0r + bu[:, nst:]
    sr_ref[...] = sr
    si_ref[...] = si
    scat = jnp.concatenate([sr, si], axis=-1).astype(BF16)
    y_ref[...] = jnp.dot(scat, cm_ref[...], preferred_element_type=F32)
    conv = cw_ref[0:1, :] * b0_ref[...] + cw_ref[1:2, :] * b1_ref[...] + cw_ref[2:3, :] * z[:ns]
    ycn_ref[...] = _rms(gb[:ns] * conv, nconv_ref[...]).astype(BF16)


def _front_kernel(x_ref, nmix_ref, win_ref, zm_ref, cw_ref, nconv_ref,
                  u4_ref, ycn_ref, zt_ref, zbuf):
    i = pl.program_id(0)
    nb, tt, d = x_ref.shape
    cw = ycn_ref.shape[2]
    rows = nb * tt

    @pl.when(i == 0)
    def _():
        zbuf[:, 0:8, :] = jnp.broadcast_to(zm_ref[...][None], (nb, 8, cw))

    h = _rms(x_ref[...].reshape(rows, d), nmix_ref[...]).astype(BF16)
    u = jnp.dot(h, win_ref[:, 0:cw], preferred_element_type=F32)
    for j in range(cw // LANE):
        u4_ref[j] = u[:, j * LANE:(j + 1) * LANE].astype(BF16).reshape(nb, tt, LANE)
    zc = jnp.dot(h, win_ref[:, cw:2 * cw], preferred_element_type=F32)
    gc = jnp.dot(h, win_ref[:, 3 * cw:4 * cw], preferred_element_type=F32)
    z3 = (gc * zc).reshape(nb, tt, cw)
    zbuf[:, 8:8 + tt, :] = z3
    z1 = zbuf[:, 7:7 + tt, :]
    z2 = zbuf[:, 6:6 + tt, :]
    conv = cw_ref[0:1, :] * z2 + cw_ref[1:2, :] * z1 + cw_ref[2:3, :] * z3
    gb = jnp.dot(h, win_ref[:, 2 * cw:3 * cw], preferred_element_type=F32)
    yc = gb * conv.reshape(rows, cw)
    ycn_ref[...] = _rms(yc, nconv_ref[...]).astype(BF16).reshape(nb, tt, cw)
    tail = zbuf[:, tt:tt + 8, :]
    zt_ref[...] = tail
    zbuf[:, 0:8, :] = tail


def _ssm_kernel(u_ref, um_ref, t_ref, p_ref, r_ref, a16_ref, y_ref, sl_ref,
                s_carry, ds_ref, sp_ref):
    th = pl.program_id(1)
    _, nb, cc, w = u_ref.shape
    nst = p_ref.shape[2]
    half = nst // 2
    rows = nb * cc
    blk = 2 * LANE
    u = u_ref[0].reshape(rows, w)

    @pl.when(th == 0)
    def _():
        ds_ref[:, 0:8, :] = jnp.dot(um_ref[0], p_ref[0], preferred_element_type=F32).reshape(nb, 8, nst)
        s_carry[...] = ds_ref[:, 0:1, :]

    ds_ref[...] = jnp.dot(u, p_ref[0], preferred_element_type=F32).reshape(nb, cc, nst)
    ar = a16_ref[0, 0:1, :].reshape(1, 1, half)
    ai = a16_ref[0, 1:2, :].reshape(1, 1, half)
    sr = s_carry[:, :, 0:half]
    si = s_carry[:, :, half:nst]
    for c in range(cc):
        sp_ref[:, c:c + 1, 0:half] = sr
        sp_ref[:, c:c + 1, half:nst] = si
        dr = ds_ref[:, c:c + 1, 0:half]
        di = ds_ref[:, c:c + 1, half:nst]
        sr, si = ar * sr - ai * si + dr, ar * si + ai * sr + di
    s_carry[:, :, 0:half] = sr
    s_carry[:, :, half:nst] = si
    sl_ref[0, :, :, 0:half] = sr
    sl_ref[0, :, :, half:nst] = si

    sp = sp_ref[...].reshape(rows, nst).astype(BF16)
    for tb in range(w // blk):
        acc = jnp.dot(sp, r_ref[0, :, tb * blk:(tb + 1) * blk], preferred_element_type=F32)
        for sb in range(tb + 1):
            acc = acc + jnp.dot(u[:, sb * blk:(sb + 1) * blk], t_ref[0, tb - sb],
                                preferred_element_type=F32)
        y_ref[0, :, :, tb * blk:(tb + 1) * blk] = acc.reshape(nb, cc, blk)


def _mix_body(x_ref, y4_ref, u4_ref, ycn_ref, dsk_ref, wglu_ref, bglu_ref, nssm_ref, wout_ref,
              nffn_ref, wr_ref, br_ref, x1_ref, hf_ref, idx_ref, gate_ref):
    nb, tt, d = x_ref.shape
    rows = nb * tt
    nj = y4_ref.shape[0]
    ne = br_ref.shape[0]
    ys = []
    for j in range(nj):
        yj = (y4_ref[j].reshape(rows, LANE)
              + dsk_ref[:, j * LANE:(j + 1) * LANE] * u4_ref[j].reshape(rows, LANE).astype(F32))
        ys.append(_gelu_tanh(yj))
    y = jnp.concatenate(ys, axis=-1)
    glu = jnp.dot(y.astype(BF16), wglu_ref[...], preferred_element_type=F32) + bglu_ref[...]
    o = y * jax.nn.sigmoid(glu)
    ysn = _rms(o, nssm_ref[...]).astype(BF16)
    mix = jnp.concatenate([ysn, ycn_ref[...].reshape(rows, ycn_ref.shape[2])], axis=-1)
    x1 = x_ref[...].reshape(rows, d) + jnp.dot(mix, wout_ref[...], preferred_element_type=F32)
    x1_ref[...] = x1
    hf = _rms(x1, nffn_ref[...])
    hf_ref[...] = hf.astype(BF16)
    logits = jnp.dot(hf, wr_ref[...], preferred_element_type=F32, precision=lax.Precision.HIGHEST)
    lt = logits.T[0:ne, :] + br_ref[...]
    iota = lax.broadcasted_iota(jnp.int32, lt.shape, 0)
    vals, idxs = [], []
    for _ in range(TOP_K):
        m = jnp.max(lt, axis=0, keepdims=True)
        ik = jnp.min(jnp.where(lt == m, iota, ne), axis=0, keepdims=True)
        vals.append(m)
        idxs.append(ik)
        lt = jnp.where(iota == ik, -jnp.inf, lt)
    es = [jnp.exp(v - vals[0]) for v in vals]
    tot = es[0] + es[1] + es[2] + es[3]
    idx_ref[...] = jnp.concatenate(idxs, axis=0)
    gate_ref[...] = jnp.concatenate([e / tot for e in es], axis=0)


def _mix_kernel_first(*refs):
    _mix_body(*refs)


def _mix_kernel_alias(*refs):
    _mix_body(*refs[:12], *refs[16:])


def _moe_kernel(be_ref, nu_ref, x_ref, wg_ref, bg_ref, wd_ref, bd_ref, y_ref, wg_bf, wd_bf):
    i = pl.program_id(0)
    e = be_ref[i]
    prev = be_ref[jnp.maximum(i - 1, 0)]
    dff = wd_ref.shape[1]

    @pl.when(jnp.logical_or(i == 0, e != prev))
    def _():
        wg_bf[...] = wg_ref[0].astype(BF16)
        wd_bf[...] = wd_ref[0].astype(BF16)

    @pl.when(i < nu_ref[0])
    def _():
        gu = jnp.dot(x_ref[...], wg_bf[...], preferred_element_type=F32) + bg_ref[0]
        gate = jnp.minimum(gu[:, :dff], SWIGLU_LIMIT)
        up = jnp.clip(gu[:, dff:], -SWIGLU_LIMIT, SWIGLU_LIMIT)
        h = gate * jax.nn.sigmoid(SWIGLU_ALPHA * gate) * (up + 1.0)
        y = jnp.dot(h.astype(BF16), wd_bf[...], preferred_element_type=F32) + bd_ref[0]
        y_ref[...] = y.astype(BF16)

    @pl.when(i >= nu_ref[0])
    def _():
        y_ref[...] = jnp.zeros(y_ref.shape, y_ref.dtype)


def _final_kernel(x1_ref, yk_ref, g_ref, nf_ref, o_ref):
    acc = x1_ref[...]
    for k in range(TOP_K):
        acc = acc + g_ref[:, k:k + 1] * yk_ref[k].astype(F32)
    o_ref[...] = _rms(acc, nf_ref[...]).reshape(o_ref.shape)


def _ssm_matrices(a_re, a_im, log_dt, b_re, b_im, c_re, c_im):
    g, n = a_re.shape
    hch = b_re.shape[2]
    gpt = LANE // hch
    nj = g // gpt
    a = lax.complex(a_re, a_im)
    dta = a * jnp.exp(log_dt)[:, None]
    a_bar = jnp.exp(dta)
    bb = ((a_bar - 1.0) / a)[:, :, None] * lax.complex(b_re, b_im)
    cc = lax.complex(c_re, c_im)
    ks = jnp.arange(CHUNK + 1, dtype=F32)
    pw = jnp.exp(dta[None] * ks[:, None, None])
    eye_t = jnp.eye(gpt, dtype=F32)
    kk = jnp.real(jnp.einsum('gon,kgn,gni->kgio', cc, pw[:CHUNK], bb))
    kk = kk.reshape(CHUNK, nj, gpt, hch, hch)
    bd = jnp.einsum('kjgio,gf->kjgifo', kk, eye_t).reshape(CHUNK, nj, LANE, LANE)
    bd = jnp.concatenate([jnp.zeros_like(bd[:1]), bd], axis=0)
    nd = CHUNK // 2
    tiles = []
    for dlt in range(nd):
        top = jnp.concatenate([bd[2 * dlt + 1], bd[2 * dlt + 2]], axis=-1)
        bot = jnp.concatenate([bd[2 * dlt], bd[2 * dlt + 1]], axis=-1)
        tiles.append(jnp.concatenate([top, bot], axis=-2))
    t_mat = jnp.stack(tiles, axis=1).astype(BF16)
    pc = pw[CHUNK - 1::-1][:CHUNK, :, :, None] * bb[None]
    pc = pc.reshape(CHUNK, nj, gpt, n, hch)

    def _p(part):
        return jnp.einsum('sjgnh,gf->jsghfn', part, eye_t).reshape(nj, CHUNK * LANE, gpt * n)

    p_mat = jnp.concatenate([_p(jnp.real(pc)), _p(jnp.imag(pc))], axis=-1).astype(BF16)
    wc = cc[None] * pw[1:CHUNK + 1][:, :, None, :]
    wc = wc.reshape(CHUNK, nj, gpt, hch, n)

    def _r(part):
        return jnp.einsum('tjgon,gf->jgntfo', part, eye_t).reshape(nj, gpt * n, CHUNK * LANE)

    r_mat = jnp.concatenate([_r(jnp.real(wc)), _r(-jnp.imag(wc))], axis=1).astype(BF16)
    a16 = pw[CHUNK].reshape(nj, 1, gpt * n)
    a16 = jnp.concatenate([jnp.real(a16), jnp.imag(a16)], axis=1)
    eye_g = jnp.eye(g, dtype=F32)

    def _b(part):
        return jnp.einsum('gnh,gf->ghfn', part, eye_g).reshape(g * hch, g * n)

    bdb = jnp.concatenate([_b(jnp.real(bb)), _b(jnp.imag(bb))], axis=-1).astype(BF16)

    def _c(part):
        return jnp.einsum('gon,gf->gnfo', part, eye_g).reshape(g * n, g * hch)

    cm = jnp.concatenate([_c(jnp.real(cc)), _c(-jnp.imag(cc))], axis=0).astype(BF16)
    abr = jnp.real(a_bar).reshape(1, g * n)
    abi = jnp.imag(a_bar).reshape(1, g * n)
    return t_mat, p_mat, r_mat, a16, bdb, cm, abr, abi


def _full(shape):
    return pl.BlockSpec(shape, lambda *_: (0,) * len(shape))


def kernel(x_prompt, x_sample, state_ssm_re, state_ssm_im, state_conv, meta_tokens, norm_mix, w_in,
           ssm_a_re, ssm_a_im, ssm_log_dt, ssm_b_re, ssm_b_im, ssm_c_re, ssm_c_im, ssm_d, w_glu, b_glu,
           conv_w, norm_out_ssm, norm_out_conv, w_out, norm_ffn, w_router, b_router, w_gate_up,
           b_gate_up, w_down, b_down, norm_final):
    nb, seq, d = x_prompt.shape
    ns = x_sample.shape[0]
    depth, _, g, n = state_ssm_re.shape
    assert depth == 1 and x_sample.shape[1] == 1 and meta_tokens.shape[0] == CHUNK
    cw = conv_w.shape[2]
    nj = cw // LANE
    ne = w_router.shape[2]
    dff = w_down.shape[2]
    nst = g * n
    tt = 128
    n_tt = seq // tt
    n_chunks = seq // CHUNK
    tp = nb * seq
    tall = tp + ns

    t_mat, p_mat, r_mat, a16, bdb, cm, abr, abi = _ssm_matrices(
        ssm_a_re[0], ssm_a_im[0], ssm_log_dt[0], ssm_b_re[0], ssm_b_im[0], ssm_c_re[0], ssm_c_im[0])
    win_bf = w_in[0].astype(BF16)
    nmix = norm_mix[0].reshape(1, d)
    nconv = norm_out_conv[0].reshape(1, cw)
    cwt = conv_w[0]

    xsm = jnp.concatenate([x_sample.reshape(ns, d), meta_tokens], axis=0)
    nsm = ns + CHUNK
    s0r = state_ssm_re[0].reshape(ns, nst)
    s0i = state_ssm_im[0].reshape(ns, nst)
    buf0 = state_conv[0, :, 0, :]
    buf1 = state_conv[0, :, 1, :]
    u_sm, z_sm, y_s, ycn_s, sr_s, si_s = pl.pallas_call(
        _small_front_kernel,
        out_shape=(jax.ShapeDtypeStruct((nsm, cw), F32), jax.ShapeDtypeStruct((nsm, cw), F32),
                   jax.ShapeDtypeStruct((ns, cw), F32), jax.ShapeDtypeStruct((ns, cw), BF16),
                   jax.ShapeDtypeStruct((ns, nst), F32), jax.ShapeDtypeStruct((ns, nst), F32)),
        compiler_params=_params(None, 48),
        name="small_front",
    )(xsm, nmix, win_bf, s0r, s0i, buf0, buf1, cwt, bdb, cm, abr, abi, nconv)
    u_meta = u_sm[ns:]
    z_meta8 = z_sm[ns + CHUNK - 8:]
    new_conv_s = jnp.stack([buf1, z_sm[:ns]], axis=1)[None]
    new_re_s = sr_s.reshape(1, ns, g, n)
    new_im_s = si_s.reshape(1, ns, g, n)

    u4, ycn_p, ztail = pl.pallas_call(
        _front_kernel,
        grid=(n_tt,),
        in_specs=[pl.BlockSpec((nb, tt, d), lambda i: (0, i, 0)),
                  _full((1, d)), _full((d, 4 * cw)), _full((8, cw)), _full((3, cw)), _full((1, cw))],
        out_specs=(pl.BlockSpec((nj, nb, tt, LANE), lambda i: (0, 0, i, 0)),
                   pl.BlockSpec((nb, tt, cw), lambda i: (0, i, 0)),
                   pl.BlockSpec((nb, 8, cw), lambda i: (0, 0, 0))),
        out_shape=(jax.ShapeDtypeStruct((nj, nb, seq, LANE), BF16),
                   jax.ShapeDtypeStruct((nb, seq, cw), BF16),
                   jax.ShapeDtypeStruct((nb, 8, cw), F32)),
        scratch_shapes=[pltpu.VMEM((nb, tt + 8, cw), F32)],
        compiler_params=_params(("arbitrary",), 52),
        name="front",
    )(x_prompt, nmix, win_bf, z_meta8, cwt, nconv)
    new_conv_p = ztail[:, 6:8, :][None]

    cc = n_chunks // 2
    wch = CHUNK * LANE
    u4c = u4.reshape(nj, nb, n_chunks, wch)
    um = u_meta.reshape(CHUNK, nj, LANE).transpose(1, 0, 2).reshape(nj, 1, wch)
    um = jnp.broadcast_to(um, (nj, 8 * nb, wch)).astype(BF16)
    nstj = 2 * (LANE // (cw // g)) * n
    y4c, s_last = pl.pallas_call(
        _ssm_kernel,
        grid=(nj, n_chunks // cc),
        in_specs=[pl.BlockSpec((1, nb, cc, wch), lambda j, t: (j, 0, t, 0)),
                  pl.BlockSpec((1, 8 * nb, wch), lambda j, t: (j, 0, 0)),
                  pl.BlockSpec((1, CHUNK // 2, 2 * LANE, 2 * LANE), lambda j, t: (j, 0, 0, 0)),
                  pl.BlockSpec((1, wch, nstj), lambda j, t: (j, 0, 0)),
                  pl.BlockSpec((1, nstj, wch), lambda j, t: (j, 0, 0)),
                  pl.BlockSpec((1, 2, nstj // 2), lambda j, t: (j, 0, 0))],
        out_specs=(pl.BlockSpec((1, nb, cc, wch), lambda j, t: (j, 0, t, 0)),
                   pl.BlockSpec((1, nb, 1, nstj), lambda j, t: (j, 0, 0, 0))),
        out_shape=(jax.ShapeDtypeStruct((nj, nb, n_chunks, wch), F32),
                   jax.ShapeDtypeStruct((nj, nb, 1, nstj), F32)),
        scratch_shapes=[pltpu.VMEM((nb, 1, nstj), F32), pltpu.VMEM((nb, cc, nstj), F32),
                        pltpu.VMEM((nb, cc, nstj), F32)],
        compiler_params=_params(("parallel", "arbitrary"), 56),
        name="ssm",
    )(u4c, um, t_mat, p_mat, r_mat, a16)
    y4 = y4c.reshape(nj, nb, seq, LANE)
    gpt = g // nj
    sl = s_last.reshape(nj, nb, 2, gpt, n)
    new_re_p = sl[:, :, 0].transpose(1, 0, 2, 3).reshape(1, nb, g, n)
    new_im_p = sl[:, :, 1].transpose(1, 0, 2, 3).reshape(1, nb, g, n)

    dsk = ssm_d[0].reshape(1, cw)
    wglu_bf = w_glu[0].astype(BF16)
    bglu = b_glu[0].reshape(1, cw)
    nssm = norm_out_ssm[0].reshape(1, cw)
    wout_bf = w_out[0].astype(BF16)
    nffn = norm_ffn[0].reshape(1, d)
    wr_pad = jnp.zeros((d, LANE), F32).at[:, :ne].set(w_router[0])
    br = b_router[0].reshape(ne, 1)
    mix_w = (dsk, wglu_bf, bglu, nssm, wout_bf, nffn, wr_pad, br)
    mix_w_specs = [_full((1, cw)), _full((cw, cw)), _full((1, cw)), _full((1, cw)), _full((2 * cw, d)),
                   _full((1, d)), _full((d, LANE)), _full((ne, 1))]
    mix_out_shape = (jax.ShapeDtypeStruct((tall, d), F32), jax.ShapeDtypeStruct((tall, d), BF16),
                     jax.ShapeDtypeStruct((TOP_K, tall), jnp.int32),
                     jax.ShapeDtypeStruct((TOP_K, tall), F32))
    tm = 64
    rows_p = nb * tm
    x1_all, hf_all, idx_all, gate_all = pl.pallas_call(
        _mix_kernel_first,
        grid=(seq // tm,),
        in_specs=[pl.BlockSpec((nb, tm, d), lambda i: (0, i, 0)),
                  pl.BlockSpec((nj, nb, tm, LANE), lambda i: (0, 0, i, 0)),
                  pl.BlockSpec((nj, nb, tm, LANE), lambda i: (0, 0, i, 0)),
                  pl.BlockSpec((nb, tm, cw), lambda i: (0, i, 0))] + mix_w_specs,
        out_specs=(pl.BlockSpec((rows_p, d), lambda i: (i, 0)),
                   pl.BlockSpec((rows_p, d), lambda i: (i, 0)),
                   pl.BlockSpec((TOP_K, rows_p), lambda i: (0, i)),
                   pl.BlockSpec((TOP_K, rows_p), lambda i: (0, i))),
        out_shape=mix_out_shape,
        compiler_params=_params(("parallel",), 52),
        name="mix_prompt",
    )(x_prompt, y4, u4, ycn_p, *mix_w)

    y4_s = y_s.reshape(ns, nj, LANE).transpose(1, 0, 2).reshape(nj, 1, ns, LANE)
    u4_s = u_sm[:ns].astype(BF16).reshape(ns, nj, LANE).transpose(1, 0, 2).reshape(nj, 1, ns, LANE)
    sblk = tp // ns
    any_spec = pl.BlockSpec(memory_space=pl.ANY)
    x1_all, hf_all, idx_all, gate_all = pl.pallas_call(
        _mix_kernel_alias,
        grid=(1,),
        in_specs=[_full((1, ns, d)), _full((nj, 1, ns, LANE)), _full((nj, 1, ns, LANE)),
                  _full((1, ns, cw))] + mix_w_specs + [any_spec] * 4,
        out_specs=(pl.BlockSpec((ns, d), lambda i: (sblk, 0)),
                   pl.BlockSpec((ns, d), lambda i: (sblk, 0)),
                   pl.BlockSpec((TOP_K, ns), lambda i: (0, sblk)),
                   pl.BlockSpec((TOP_K, ns), lambda i: (0, sblk))),
        out_shape=mix_out_shape,
        input_output_aliases={12: 0, 13: 1, 14: 2, 15: 3},
        compiler_params=_params(("arbitrary",), 32),
        name="mix_sample",
    )(x_sample.reshape(1, ns, d), y4_s, u4_s, ycn_s.reshape(1, ns, cw), *mix_w,
      x1_all, hf_all, idx_all, gate_all)

    bm = MOE_ROWS
    onehot = (idx_all[:, :, None] == jnp.arange(ne, dtype=jnp.int32)[None, None, :])
    sel = jnp.sum(onehot.astype(jnp.int32), axis=0)
    csum = jnp.cumsum(sel, axis=0)
    counts = csum[-1]
    padded = ((counts + bm - 1) // bm) * bm
    pend = jnp.cumsum(padded)
    pstart = pend - padded
    rank = jnp.take_along_axis(csum, idx_all.T, axis=1) - 1
    dest = (pstart[idx_all.T] + rank).astype(jnp.int32)
    n_blocks = (tall * TOP_K + ne * (bm - 1) + bm - 1) // bm
    cap = n_blocks * bm
    tok = jnp.broadcast_to(jnp.arange(tall, dtype=jnp.int32)[:, None], (tall, TOP_K))
    slot_tok = jnp.zeros((cap,), jnp.int32).at[dest.reshape(-1)].set(tok.reshape(-1))
    blk_e = jnp.minimum(jnp.searchsorted(pend, jnp.arange(n_blocks, dtype=jnp.int32) * bm, side='right'),
                        ne - 1).astype(jnp.int32)
    n_used = (pend[-1] // bm).astype(jnp.int32).reshape(1)
    xs = jnp.take(hf_all, slot_tok, axis=0)

    yb = pl.pallas_call(
        _moe_kernel,
        grid_spec=pltpu.PrefetchScalarGridSpec(
            num_scalar_prefetch=2,
            grid=(n_blocks,),
            in_specs=[pl.BlockSpec((bm, d), lambda i, be, nu: (i, 0)),
                      pl.BlockSpec((1, d, 2 * dff), lambda i, be, nu: (be[i], 0, 0)),
                      pl.BlockSpec((1, 1, 2 * dff), lambda i, be, nu: (be[i], 0, 0)),
                      pl.BlockSpec((1, dff, d), lambda i, be, nu: (be[i], 0, 0)),
                      pl.BlockSpec((1, 1, d), lambda i, be, nu: (be[i], 0, 0))],
            out_specs=pl.BlockSpec((bm, d), lambda i, be, nu: (i, 0)),
            scratch_shapes=[pltpu.VMEM((d, 2 * dff), BF16), pltpu.VMEM((dff, d), BF16)]),
        out_shape=jax.ShapeDtypeStruct((cap, d), BF16),
        compiler_params=_params(("arbitrary",), 56),
        name="moe",
    )(blk_e, n_used, xs, w_gate_up[0], b_gate_up[0].reshape(ne, 1, 2 * dff), w_down[0],
      b_down[0].reshape(ne, 1, d))

    yk = jnp.take(yb, dest.T, axis=0)
    gate_t = gate_all.T
    nfin = norm_final.reshape(1, d)
    tr = rows_p
    y_p = pl.pallas_call(
        _final_kernel,
        grid=(tp // tr,),
        in_specs=[pl.BlockSpec((tr, d), lambda i: (i, 0)),
                  pl.BlockSpec((TOP_K, tr, d), lambda i: (0, i, 0)),
                  pl.BlockSpec((tr, TOP_K), lambda i: (i, 0)), _full((1, d))],
        out_specs=pl.BlockSpec((nb, tm, d), lambda i: (0, i, 0)),
        out_shape=jax.ShapeDtypeStruct((nb, seq, d), F32),
        compiler_params=_params(("parallel",), 40),
        name="final_prompt",
    )(x1_all, yk, gate_t, nfin)
    y_sm = pl.pallas_call(
        _final_kernel,
        grid=(1,),
        in_specs=[pl.BlockSpec((ns, d), lambda i: (sblk, 0)),
                  pl.BlockSpec((TOP_K, ns, d), lambda i: (0, sblk, 0)),
                  pl.BlockSpec((ns, TOP_K), lambda i: (sblk, 0)), _full((1, d))],
        out_specs=pl.BlockSpec((ns, d), lambda i: (0, 0)),
        out_shape=jax.ShapeDtypeStruct((ns, d), F32),
        compiler_params=_params(("arbitrary",), 32),
        name="final_sample",
    )(x1_all, yk, gate_t, nfin)

    return (y_p, y_sm.reshape(ns, 1, d), new_re_p, new_im_p, new_conv_p,
            new_re_s, new_im_s, new_conv_s)
```

```python
import math

import jax
import jax.numpy as jnp
from jax import lax
from jax.experimental import pallas as pl
from jax.experimental.pallas import tpu as pltpu

F32 = jnp.float32
BF16 = jnp.bfloat16
U32 = jnp.uint32
I32 = jnp.int32
EPS = 1e-5
CHUNK = 16
LANE = 128
TOP_K = 4
SWIGLU_LIMIT = 7.0
SWIGLU_ALPHA = 1.702
MOE_ROWS = 256
TOK_TILE = 256
WIN = 32
SUBLANE = 8
GATE_COLS = LANE


def _dispatch_run_rows(ne):
    return TOP_K * TOK_TILE + ne * 2 * (SUBLANE - 1) + WIN


def _combine_run_rows(ne):
    return -(-(TOP_K * TOK_TILE + ne * (SUBLANE - 1 + WIN - 1)) // WIN) * WIN
HI_MASK = 0xFFFF0000
MIB = 1024 * 1024


def _rms(x, g):
    return x * lax.rsqrt(jnp.mean(x * x, axis=-1, keepdims=True) + EPS) * g


def _gelu_tanh(x):
    c = math.sqrt(2.0 / math.pi)
    return 0.5 * x * (1.0 + jnp.tanh(c * (x + 0.044715 * (x * x * x))))


def _params(sem, vmem_mib):
    return pltpu.CompilerParams(dimension_semantics=sem, vmem_limit_bytes=vmem_mib * MIB)


def _pack_pairs(a, b):
    return (pltpu.bitcast(a, U32) >> 16) | (pltpu.bitcast(b, U32) & jnp.uint32(HI_MASK))


def _unpack_pairs(w):
    lo = pltpu.bitcast(w << 16, F32)
    hi = pltpu.bitcast(w & jnp.uint32(HI_MASK), F32)
    return jnp.concatenate([lo, hi], axis=-1).astype(BF16)


def _small_front_kernel(x_ref, nmix_ref, win_ref, s0r_ref, s0i_ref, b0_ref, b1_ref, cw_ref,
                        bdb_ref, cm_ref, abr_ref, abi_ref, nconv_ref,
                        u_ref, z_ref, y_ref, ycn_ref, sr_ref, si_ref):
    ns, nst = s0r_ref.shape
    cw = u_ref.shape[1]
    h = _rms(x_ref[...], nmix_ref[...]).astype(BF16)
    proj = jnp.dot(h, win_ref[...], preferred_element_type=F32)
    u = proj[:, 0:cw]
    zc = proj[:, cw:2 * cw]
    gb = proj[:, 2 * cw:3 * cw]
    gc = proj[:, 3 * cw:4 * cw]
    z = gc * zc
    u_ref[...] = u
    z_ref[...] = z
    bu = jnp.dot(u[:ns].astype(BF16), bdb_ref[...], preferred_element_type=F32)
    abr = abr_ref[...]
    abi = abi_ref[...]
    s0r = s0r_ref[...]
    s0i = s0i_ref[...]
    sr = abr * s0r - abi * s0i + bu[:, :nst]
    si = abr * s0i + abi * s0r + bu[:, nst:]
    sr_ref[...] = sr
    si_ref[...] = si
    scat = jnp.concatenate([sr, si], axis=-1).astype(BF16)
    y_ref[...] = jnp.dot(scat, cm_ref[...], preferred_element_type=F32)
    conv = cw_ref[0:1, :] * b0_ref[...] + cw_ref[1:2, :] * b1_ref[...] + cw_ref[2:3, :] * z[:ns]
    ycn_ref[...] = _rms(gb[:ns] * conv, nconv_ref[...]).astype(BF16)


def _front_kernel(x_ref, nmix_ref, win_ref, zm_ref, cw_ref, nconv_ref,
                  u4_ref, ycn_ref, zt_ref, zbuf):
    i = pl.program_id(0)
    nb, tt, d = x_ref.shape
    cw = ycn_ref.shape[2]
    rows = nb * tt

    @pl.when(i == 0)
    def _():
        zbuf[:, 0:8, :] = jnp.broadcast_to(zm_ref[...][None], (nb, 8, cw))

    h = _rms(x_ref[...].reshape(rows, d), nmix_ref[...]).astype(BF16)
    u = jnp.dot(h, win_ref[:, 0:cw], preferred_element_type=F32)
    for j in range(cw // LANE):
        u4_ref[j] = u[:, j * LANE:(j + 1) * LANE].astype(BF16).reshape(nb, tt, LANE)
    zc = jnp.dot(h, win_ref[:, cw:2 * cw], preferred_element_type=F32)
    gc = jnp.dot(h, win_ref[:, 3 * cw:4 * cw], preferred_element_type=F32)
    z3 = (gc * zc).reshape(nb, tt, cw)
    zbuf[:, 8:8 + tt, :] = z3
    z1 = zbuf[:, 7:7 + tt, :]
    z2 = zbuf[:, 6:6 + tt, :]
    conv = cw_ref[0:1, :] * z2 + cw_ref[1:2, :] * z1 + cw_ref[2:3, :] * z3
    gb = jnp.dot(h, win_ref[:, 2 * cw:3 * cw], preferred_element_type=F32)
    yc = gb * conv.reshape(rows, cw)
    ycn_ref[...] = _rms(yc, nconv_ref[...]).astype(BF16).reshape(nb, tt, cw)
    tail = zbuf[:, tt:tt + 8, :]
    zt_ref[...] = tail
    zbuf[:, 0:8, :] = tail


def _ssm_kernel(u_ref, um_ref, t_ref, p_ref, r_ref, a16_ref, y_ref, sl_ref,
                s_carry, ds_ref, sp_ref):
    th = pl.program_id(1)
    _, nb, cc, w = u_ref.shape
    nst = p_ref.shape[2]
    half = nst // 2
    rows = nb * cc
    blk = 2 * LANE
    u = u_ref[0].reshape(rows, w)

    @pl.when(th == 0)
    def _():
        ds_ref[:, 0:8, :] = jnp.dot(um_ref[0], p_ref[0], preferred_element_type=F32).reshape(nb, 8, nst)
        s_carry[...] = ds_ref[:, 0:1, :]

    ds_ref[...] = jnp.dot(u, p_ref[0], preferred_element_type=F32).reshape(nb, cc, nst)
    ar = a16_ref[0, 0:1, :].reshape(1, 1, half)
    ai = a16_ref[0, 1:2, :].reshape(1, 1, half)
    sr = s_carry[:, :, 0:half]
    si = s_carry[:, :, half:nst]
    for c in range(cc):
        sp_ref[:, c:c + 1, 0:half] = sr
        sp_ref[:, c:c + 1, half:nst] = si
        dr = ds_ref[:, c:c + 1, 0:half]
        di = ds_ref[:, c:c + 1, half:nst]
        sr, si = ar * sr - ai * si + dr, ar * si + ai * sr + di
    s_carry[:, :, 0:half] = sr
    s_carry[:, :, half:nst] = si
    sl_ref[0, :, :, 0:half] = sr
    sl_ref[0, :, :, half:nst] = si

    sp = sp_ref[...].reshape(rows, nst).astype(BF16)
    for tb in range(w // blk):
        acc = jnp.dot(sp, r_ref[0, :, tb * blk:(tb + 1) * blk], preferred_element_type=F32)
        for sb in range(tb + 1):
            acc = acc + jnp.dot(u[:, sb * blk:(sb + 1) * blk], t_ref[0, tb - sb],
                                preferred_element_type=F32)
        y_ref[0, :, :, tb * blk:(tb + 1) * blk] = acc.reshape(nb, cc, blk)


def _mix_compute(x_ref, y4_ref, u4_ref, ycn_ref, dsk_ref, wglu_ref, bglu_ref, nssm_ref, wout_ref,
                 nffn_ref, wrh_ref, wrl_ref, br_ref):
    nb, tt, d = x_ref.shape
    rows = nb * tt
    nj = y4_ref.shape[0]
    ne = br_ref.shape[0]
    ys = []
    for j in range(nj):
        yj = (y4_ref[j].reshape(rows, LANE)
              + dsk_ref[:, j * LANE:(j + 1) * LANE] * u4_ref[j].reshape(rows, LANE).astype(F32))
        ys.append(_gelu_tanh(yj))
    y = jnp.concatenate(ys, axis=-1)
    glu = jnp.dot(y.astype(BF16), wglu_ref[...], preferred_element_type=F32) + bglu_ref[...]
    o = y * jax.nn.sigmoid(glu)
    ysn = _rms(o, nssm_ref[...]).astype(BF16)
    mix = jnp.concatenate([ysn, ycn_ref[...].reshape(rows, ycn_ref.shape[2])], axis=-1)
    x1 = x_ref[...].reshape(rows, d) + jnp.dot(mix, wout_ref[...], preferred_element_type=F32)
    hf = _rms(x1, nffn_ref[...])
    hf_hi = hf.astype(BF16)
    hf_lo = (hf - hf_hi.astype(F32)).astype(BF16)
    logits = (jnp.dot(hf_hi, wrh_ref[...], preferred_element_type=F32)
              + jnp.dot(hf_lo, wrh_ref[...], preferred_element_type=F32)
              + jnp.dot(hf_hi, wrl_ref[...], preferred_element_type=F32))
    lt = logits.T[0:ne, :] + br_ref[...]
    iota = lax.broadcasted_iota(I32, lt.shape, 0)
    vals, idxs = [], []
    sel = jnp.zeros(lt.shape, F32)
    for _ in range(TOP_K):
        m = jnp.max(lt, axis=0, keepdims=True)
        ik = jnp.min(jnp.where(lt == m, iota, ne), axis=0, keepdims=True)
        vals.append(m)
        idxs.append(ik)
        hit = iota == ik
        sel = sel + jnp.where(hit, 1.0, 0.0)
        lt = jnp.where(hit, -jnp.inf, lt)
    es = [jnp.exp(v - vals[0]) for v in vals]
    tot = es[0] + es[1] + es[2] + es[3]
    idx = jnp.concatenate(idxs, axis=0)
    gates = jnp.concatenate([e / tot for e in es], axis=0)
    return x1, hf_hi, idx, gates, sel


def _mix_kernel_prompt(*refs):
    x1, hf, idx, gates, sel = _mix_compute(*refs[:13])
    x1_ref, hf_ref, idx_ref, gate_ref, cnt_ref = refs[13:]
    x1_ref[...] = x1
    hf_ref[...] = hf
    idx_ref[...] = idx
    gate_ref[...] = gates
    for t in range(cnt_ref.shape[0]):
        part = sel[:, t * TOK_TILE:(t + 1) * TOK_TILE]
        cnt_ref[t] = jnp.sum(part, axis=1, keepdims=True).astype(I32)


def _mix_kernel_sample(*refs):
    x1, hf, idx, gates, sel = _mix_compute(*refs[:13])
    x1_ref, hf_ref, idx_ref, gate_ref, cnt_ref = refs[18:]
    ns = x1.shape[0]
    x1_ref[...] = jnp.zeros(x1_ref.shape, x1_ref.dtype)
    hf_ref[...] = jnp.zeros(hf_ref.shape, hf_ref.dtype)
    idx_ref[...] = jnp.zeros(idx_ref.shape, idx_ref.dtype)
    gate_ref[...] = jnp.zeros(gate_ref.shape, gate_ref.dtype)
    x1_ref[0:ns, :] = x1
    hf_ref[0:ns, :] = hf
    idx_ref[:, 0:ns] = idx
    gate_ref[:, 0:ns] = gates
    cnt_ref[0] = jnp.sum(sel, axis=1, keepdims=True).astype(I32)


def _local_rows(idx_ref, loff_ref, tile, n_valid):
    ne = loff_ref.shape[1]
    tt = idx_ref.shape[1]
    e_iota = lax.broadcasted_iota(I32, (ne, tt), 0)
    tok = tile * tt + lax.broadcasted_iota(I32, (1, tt), 1)
    valid = tok < n_valid
    hits = [jnp.logical_and(e_iota == idx_ref[k:k + 1, :], valid) for k in range(TOP_K)]
    sel = jnp.zeros((ne, tt), F32)
    for h in hits:
        sel = sel + jnp.where(h, 1.0, 0.0)
    before = lax.broadcasted_iota(I32, (tt, tt), 0) < lax.broadcasted_iota(I32, (tt, tt), 1)
    tri = jnp.where(before, 1.0, 0.0).astype(BF16)
    base = jnp.dot(sel.astype(BF16), tri, preferred_element_type=F32) + loff_ref[0].astype(F32)
    rows = []
    for h in hits:
        r = jnp.sum(jnp.where(h, base, 0.0), axis=0, keepdims=True)
        rows.append(jnp.where(valid, r, -1.0))
    return rows


def _window_copy(buf, slot, hbm, lo, g, sem, to_hbm):
    src = buf.at[slot, pl.ds(pl.multiple_of(lo, SUBLANE), WIN)]
    dst = hbm.at[pl.ds(pl.multiple_of(g, SUBLANE), WIN)]
    if to_hbm:
        return pltpu.make_async_copy(src, dst, sem.at[slot])
    return pltpu.make_async_copy(dst, src, sem.at[slot])


def _start_windows(gs_ref, nw_ref, lo_ref, tile, ne, buf, slot, hbm, sem, to_hbm):
    def per_expert(e, carry):
        g = gs_ref[tile * ne + e]
        lo = lo_ref[tile * ne + e]

        def per_window(w, c):
            _window_copy(buf, slot, hbm, lo + w * WIN, g + w * WIN, sem, to_hbm).start()
            return c

        return lax.fori_loop(0, nw_ref[tile * ne + e], per_window, carry)

    lax.fori_loop(0, ne, per_expert, 0)


def _wait_windows(count, buf, slot, hbm, sem, to_hbm):
    def per_window(w, c):
        _window_copy(buf, slot, hbm, 0, 0, sem, to_hbm).wait()
        return c

    lax.fori_loop(0, count, per_window, 0)


def _dispatch_kernel(gs_ref, nw_ref, lo_ref, tg_ref, tw_ref, nv_ref,
                     hf_ref, idx_ref, gate_ref, loff_ref, xs_ref, buf, carry, sem):
    i = pl.program_id(0)
    nt = pl.num_programs(0)
    ne = loff_ref.shape[1]
    tt, d = hf_ref.shape
    nrun = buf.shape[1]
    slot = i % 2

    @pl.when(i == 0)
    def _():
        carry[...] = jnp.zeros(carry.shape, carry.dtype)

    rows = _local_rows(idx_ref, loff_ref, i, nv_ref[0])
    r_iota = lax.broadcasted_iota(I32, (nrun, tt), 0).astype(F32)
    smat = jnp.zeros((nrun, tt), F32)
    gmat = jnp.zeros((nrun, tt), F32)
    for k, r in enumerate(rows):
        hit = r_iota == r
        smat = smat + jnp.where(hit, 1.0, 0.0)
        gmat = gmat + jnp.where(hit, gate_ref[k:k + 1, :], 0.0)
    xr = jnp.dot(smat.astype(BF16), hf_ref[...], preferred_element_type=F32)
    gcol = jnp.sum(gmat, axis=1, keepdims=True)
    lane0 = lax.broadcasted_iota(I32, (nrun, GATE_COLS), 1) == 0
    buf[slot, :, 0:d // 2] = _pack_pairs(xr[:, :d // 2], xr[:, d // 2:])
    buf[slot, :, d // 2:] = pltpu.bitcast(jnp.where(lane0, gcol, 0.0), U32)

    def merge(e, c):
        @pl.when(nw_ref[i * ne + e] > 0)
        def _():
            lo = pl.multiple_of(lo_ref[i * ne + e], SUBLANE)
            buf[slot, pl.ds(lo, SUBLANE), :] = buf[slot, pl.ds(lo, SUBLANE), :] | carry[e]
            tg = tg_ref[i * ne + e]

            @pl.when(tg >= 0)
            def _():
                carry[e] = buf[slot, pl.ds(pl.multiple_of(tg, SUBLANE), SUBLANE), :]

            @pl.when(tg < 0)
            def _():
                carry[e] = jnp.zeros(carry.shape[1:], carry.dtype)

        return c

    lax.fori_loop(0, ne, merge, 0)

    @pl.when(i > 0)
    def _():
        _wait_windows(tw_ref[i - 1], buf, 1 - slot, xs_ref, sem, True)

    _start_windows(gs_ref, nw_ref, lo_ref, i, ne, buf, slot, xs_ref, sem, True)

    @pl.when(i == nt - 1)
    def _():
        _wait_windows(tw_ref[i], buf, slot, xs_ref, sem, True)


def _moe_kernel(be_ref, bv_ref, x_ref, wg_ref, bg_ref, wd_ref, bd_ref, y_ref, wg_bf, wd_bf):
    i = pl.program_id(0)
    e = be_ref[i]
    prev = be_ref[jnp.maximum(i - 1, 0)]
    dff = wd_ref.shape[1]
    bm = x_ref.shape[0]

    @pl.when(jnp.logical_or(i == 0, e != prev))
    def _():
        wg_bf[...] = wg_ref[0].astype(BF16)
        wd_bf[...] = wd_ref[0].astype(BF16)

    @pl.when(bv_ref[i] > 0)
    def _():
        live = lax.broadcasted_iota(I32, (bm, 1), 0) < bv_ref[i]
        nw = x_ref.shape[1] - GATE_COLS
        x = jnp.where(live, _unpack_pairs(x_ref[:, 0:nw]), jnp.zeros((), BF16))
        route = jnp.where(live, pltpu.bitcast(x_ref[:, nw:], F32)[:, 0:1], 0.0)
        gu = jnp.dot(x, wg_bf[...], preferred_element_type=F32) + bg_ref[0]
        gate = jnp.minimum(gu[:, :dff], SWIGLU_LIMIT)
        up = jnp.clip(gu[:, dff:], -SWIGLU_LIMIT, SWIGLU_LIMIT)
        h = gate * jax.nn.sigmoid(SWIGLU_ALPHA * gate) * (up + 1.0)
        y = jnp.dot(h.astype(BF16), wd_bf[...], preferred_element_type=F32) + bd_ref[0]
        yr = (route * y).astype(BF16).astype(F32)
        half = yr.shape[1] // 2
        y_ref[...] = _pack_pairs(yr[:, :half], yr[:, half:])

    @pl.when(bv_ref[i] <= 0)
    def _():
        y_ref[...] = jnp.zeros(y_ref.shape, y_ref.dtype)


def _combine_tile(step, nt):
    return (step + nt - 1) % nt


def _combine_kernel(gs_ref, nw_ref, lo_ref, tw_ref, nv_ref,
                    x1_ref, idx_ref, loff_ref, nf_ref, yb_ref,
                    yp_ref, ys_ref, buf, sem):
    s = pl.program_id(0)
    nt = pl.num_programs(0)
    ne = loff_ref.shape[1]
    tt, d = x1_ref.shape
    nrun = buf.shape[1]
    tile = _combine_tile(s, nt)
    slot = s % 2

    @pl.when(s == 0)
    def _():
        buf[...] = jnp.zeros(buf.shape, buf.dtype)
        _start_windows(gs_ref, nw_ref, lo_ref, tile, ne, buf, slot, yb_ref, sem, False)

    @pl.when(s + 1 < nt)
    def _():
        _start_windows(gs_ref, nw_ref, lo_ref, _combine_tile(s + 1, nt), ne, buf, 1 - slot, yb_ref,
                       sem, False)

    rows = _local_rows(idx_ref, loff_ref, tile, nv_ref[0])
    r_iota = lax.broadcasted_iota(I32, (nrun, tt), 0).astype(F32)
    smat = jnp.zeros((nrun, tt), F32)
    for r in rows:
        smat = smat + jnp.where(r_iota == r, 1.0, 0.0)

    _wait_windows(tw_ref[tile], buf, slot, yb_ref, sem, False)
    yrun = _unpack_pairs(buf[slot])
    moe = lax.dot_general(smat.astype(BF16), yrun, (((0,), (0,)), ((), ())),
                          preferred_element_type=F32)
    out = _rms(x1_ref[...] + moe, nf_ref[...])

    @pl.when(s == 0)
    def _():
        ys_ref[...] = out[0:ys_ref.shape[0], :]

    @pl.when(s > 0)
    def _():
        yp_ref[...] = out.reshape(yp_ref.shape)


def _ssm_matrices(a_re, a_im, log_dt, b_re, b_im, c_re, c_im):
    g, n = a_re.shape
    hch = b_re.shape[2]
    gpt = LANE // hch
    nj = g // gpt
    a = lax.complex(a_re, a_im)
    dta = a * jnp.exp(log_dt)[:, None]
    a_bar = jnp.exp(dta)
    bb = ((a_bar - 1.0) / a)[:, :, None] * lax.complex(b_re, b_im)
    cc = lax.complex(c_re, c_im)
    ks = jnp.arange(CHUNK + 1, dtype=F32)
    pw = jnp.exp(dta[None] * ks[:, None, None])
    eye_t = jnp.eye(gpt, dtype=F32)
    kk = jnp.real(jnp.einsum('gon,kgn,gni->kgio', cc, pw[:CHUNK], bb))
    kk = kk.reshape(CHUNK, nj, gpt, hch, hch)
    bd = jnp.einsum('kjgio,gf->kjgifo', kk, eye_t).reshape(CHUNK, nj, LANE, LANE)
    bd = jnp.concatenate([jnp.zeros_like(bd[:1]), bd], axis=0)
    nd = CHUNK // 2
    tiles = []
    for dlt in range(nd):
        top = jnp.concatenate([bd[2 * dlt + 1], bd[2 * dlt + 2]], axis=-1)
        bot = jnp.concatenate([bd[2 * dlt], bd[2 * dlt + 1]], axis=-1)
        tiles.append(jnp.concatenate([top, bot], axis=-2))
    t_mat = jnp.stack(tiles, axis=1).astype(BF16)
    pc = pw[CHUNK - 1::-1][:CHUNK, :, :, None] * bb[None]
    pc = pc.reshape(CHUNK, nj, gpt, n, hch)

    def _p(part):
        return jnp.einsum('sjgnh,gf->jsghfn', part, eye_t).reshape(nj, CHUNK * LANE, gpt * n)

    p_mat = jnp.concatenate([_p(jnp.real(pc)), _p(jnp.imag(pc))], axis=-1).astype(BF16)
    wc = cc[None] * pw[1:CHUNK + 1][:, :, None, :]
    wc = wc.reshape(CHUNK, nj, gpt, hch, n)

    def _r(part):
        return jnp.einsum('tjgon,gf->jgntfo', part, eye_t).reshape(nj, gpt * n, CHUNK * LANE)

    r_mat = jnp.concatenate([_r(jnp.real(wc)), _r(-jnp.imag(wc))], axis=1).astype(BF16)
    a16 = pw[CHUNK].reshape(nj, 1, gpt * n)
    a16 = jnp.concatenate([jnp.real(a16), jnp.imag(a16)], axis=1)
    eye_g = jnp.eye(g, dtype=F32)

    def _b(part):
        return jnp.einsum('gnh,gf->ghfn', part, eye_g).reshape(g * hch, g * n)

    bdb = jnp.concatenate([_b(jnp.real(bb)), _b(jnp.imag(bb))], axis=-1).astype(BF16)

    def _c(part):
        return jnp.einsum('gon,gf->gnfo', part, eye_g).reshape(g * n, g * hch)

    cm = jnp.concatenate([_c(jnp.real(cc)), _c(-jnp.imag(cc))], axis=0).astype(BF16)
    abr = jnp.real(a_bar).reshape(1, g * n)
    abi = jnp.imag(a_bar).reshape(1, g * n)
    return t_mat, p_mat, r_mat, a16, bdb, cm, abr, abi


def _full(shape):
    return pl.BlockSpec(shape, lambda *_: (0,) * len(shape))


def kernel(x_prompt, x_sample, state_ssm_re, state_ssm_im, state_conv, meta_tokens, norm_mix, w_in,
           ssm_a_re, ssm_a_im, ssm_log_dt, ssm_b_re, ssm_b_im, ssm_c_re, ssm_c_im, ssm_d, w_glu, b_glu,
           conv_w, norm_out_ssm, norm_out_conv, w_out, norm_ffn, w_router, b_router, w_gate_up,
           b_gate_up, w_down, b_down, norm_final):
    nb, seq, d = x_prompt.shape
    ns = x_sample.shape[0]
    depth, _, g, n = state_ssm_re.shape
    assert depth == 1 and x_sample.shape[1] == 1 and meta_tokens.shape[0] == CHUNK
    cw = conv_w.shape[2]
    nj = cw // LANE
    ne = w_router.shape[2]
    dff = w_down.shape[2]
    nst = g * n
    tt = 128
    n_tt = seq // tt
    n_chunks = seq // CHUNK
    tp = nb * seq
    tall = tp + ns
    tm = 64
    rows_p = nb * tm
    assert rows_p % TOK_TILE == 0 and TOK_TILE % tm == 0 and ns <= TOK_TILE and d % 2 == 0
    n_tiles = tp // TOK_TILE + 1
    ta = n_tiles * TOK_TILE

    t_mat, p_mat, r_mat, a16, bdb, cm, abr, abi = _ssm_matrices(
        ssm_a_re[0], ssm_a_im[0], ssm_log_dt[0], ssm_b_re[0], ssm_b_im[0], ssm_c_re[0], ssm_c_im[0])
    win_bf = w_in[0].astype(BF16)
    nmix = norm_mix[0].reshape(1, d)
    nconv = norm_out_conv[0].reshape(1, cw)
    cwt = conv_w[0]

    xsm = jnp.concatenate([x_sample.reshape(ns, d), meta_tokens], axis=0)
    nsm = ns + CHUNK
    s0r = state_ssm_re[0].reshape(ns, nst)
    s0i = state_ssm_im[0].reshape(ns, nst)
    buf0 = state_conv[0, :, 0, :]
    buf1 = state_conv[0, :, 1, :]
    u_sm, z_sm, y_s, ycn_s, sr_s, si_s = pl.pallas_call(
        _small_front_kernel,
        out_shape=(jax.ShapeDtypeStruct((nsm, cw), F32), jax.ShapeDtypeStruct((nsm, cw), F32),
                   jax.ShapeDtypeStruct((ns, cw), F32), jax.ShapeDtypeStruct((ns, cw), BF16),
                   jax.ShapeDtypeStruct((ns, nst), F32), jax.ShapeDtypeStruct((ns, nst), F32)),
        compiler_params=_params(None, 48),
        name="small_front",
    )(xsm, nmix, win_bf, s0r, s0i, buf0, buf1, cwt, bdb, cm, abr, abi, nconv)
    u_meta = u_sm[ns:]
    z_meta8 = z_sm[ns + CHUNK - 8:]
    new_conv_s = jnp.stack([buf1, z_sm[:ns]], axis=1)[None]
    new_re_s = sr_s.reshape(1, ns, g, n)
    new_im_s = si_s.reshape(1, ns, g, n)

    u4, ycn_p, ztail = pl.pallas_call(
        _front_kernel,
        grid=(n_tt,),
        in_specs=[pl.BlockSpec((nb, tt, d), lambda i: (0, i, 0)),
                  _full((1, d)), _full((d, 4 * cw)), _full((8, cw)), _full((3, cw)), _full((1, cw))],
        out_specs=(pl.BlockSpec((nj, nb, tt, LANE), lambda i: (0, 0, i, 0)),
                   pl.BlockSpec((nb, tt, cw), lambda i: (0, i, 0)),
                   pl.BlockSpec((nb, 8, cw), lambda i: (0, 0, 0))),
        out_shape=(jax.ShapeDtypeStruct((nj, nb, seq, LANE), BF16),
                   jax.ShapeDtypeStruct((nb, seq, cw), BF16),
                   jax.ShapeDtypeStruct((nb, 8, cw), F32)),
        scratch_shapes=[pltpu.VMEM((nb, tt + 8, cw), F32)],
        compiler_params=_params(("arbitrary",), 52),
        name="front",
    )(x_prompt, nmix, win_bf, z_meta8, cwt, nconv)
    new_conv_p = ztail[:, 6:8, :][None]

    cc = n_chunks // 2
    wch = CHUNK * LANE
    u4c = u4.reshape(nj, nb, n_chunks, wch)
    um = u_meta.reshape(CHUNK, nj, LANE).transpose(1, 0, 2).reshape(nj, 1, wch)
    um = jnp.broadcast_to(um, (nj, 8 * nb, wch)).astype(BF16)
    nstj = 2 * (LANE // (cw // g)) * n
    y4c, s_last = pl.pallas_call(
        _ssm_kernel,
        grid=(nj, n_chunks // cc),
        in_specs=[pl.BlockSpec((1, nb, cc, wch), lambda j, t: (j, 0, t, 0)),
                  pl.BlockSpec((1, 8 * nb, wch), lambda j, t: (j, 0, 0)),
                  pl.BlockSpec((1, CHUNK // 2, 2 * LANE, 2 * LANE), lambda j, t: (j, 0, 0, 0)),
                  pl.BlockSpec((1, wch, nstj), lambda j, t: (j, 0, 0)),
                  pl.BlockSpec((1, nstj, wch), lambda j, t: (j, 0, 0)),
                  pl.BlockSpec((1, 2, nstj // 2), lambda j, t: (j, 0, 0))],
        out_specs=(pl.BlockSpec((1, nb, cc, wch), lambda j, t: (j, 0, t, 0)),
                   pl.BlockSpec((1, nb, 1, nstj), lambda j, t: (j, 0, 0, 0))),
        out_shape=(jax.ShapeDtypeStruct((nj, nb, n_chunks, wch), F32),
                   jax.ShapeDtypeStruct((nj, nb, 1, nstj), F32)),
        scratch_shapes=[pltpu.VMEM((nb, 1, nstj), F32), pltpu.VMEM((nb, cc, nstj), F32),
                        pltpu.VMEM((nb, cc, nstj), F32)],
        compiler_params=_params(("parallel", "arbitrary"), 56),
        name="ssm",
    )(u4c, um, t_mat, p_mat, r_mat, a16)
    y4 = y4c.reshape(nj, nb, seq, LANE)
    gpt = g // nj
    sl = s_last.reshape(nj, nb, 2, gpt, n)
    new_re_p = sl[:, :, 0].transpose(1, 0, 2, 3).reshape(1, nb, g, n)
    new_im_p = sl[:, :, 1].transpose(1, 0, 2, 3).reshape(1, nb, g, n)

    dsk = ssm_d[0].reshape(1, cw)
    wglu_bf = w_glu[0].astype(BF16)
    bglu = b_glu[0].reshape(1, cw)
    nssm = norm_out_ssm[0].reshape(1, cw)
    wout_bf = w_out[0].astype(BF16)
    nffn = norm_ffn[0].reshape(1, d)
    wr_pad = jnp.zeros((d, LANE), F32).at[:, :ne].set(w_router[0])
    wr_hi = wr_pad.astype(BF16)
    wr_lo = (wr_pad - wr_hi.astype(F32)).astype(BF16)
    br = b_router[0].reshape(ne, 1)
    mix_w = (dsk, wglu_bf, bglu, nssm, wout_bf, nffn, wr_hi, wr_lo, br)
    mix_w_specs = [_full((1, cw)), _full((cw, cw)), _full((1, cw)), _full((1, cw)), _full((2 * cw, d)),
                   _full((1, d)), _full((d, LANE)), _full((d, LANE)), _full((ne, 1))]
    mix_out_shape = (jax.ShapeDtypeStruct((ta, d), F32), jax.ShapeDtypeStruct((ta, d), BF16),
                     jax.ShapeDtypeStruct((TOP_K, ta), I32), jax.ShapeDtypeStruct((TOP_K, ta), F32),
                     jax.ShapeDtypeStruct((n_tiles, ne, 1), I32))
    tpm = rows_p // TOK_TILE
    x1_all, hf_all, idx_all, gate_all, cnt = pl.pallas_call(
        _mix_kernel_prompt,
        grid=(seq // tm,),
        in_specs=[pl.BlockSpec((nb, tm, d), lambda i: (0, i, 0)),
                  pl.BlockSpec((nj, nb, tm, LANE), lambda i: (0, 0, i, 0)),
                  pl.BlockSpec((nj, nb, tm, LANE), lambda i: (0, 0, i, 0)),
                  pl.BlockSpec((nb, tm, cw), lambda i: (0, i, 0))] + mix_w_specs,
        out_specs=(pl.BlockSpec((rows_p, d), lambda i: (i, 0)),
                   pl.BlockSpec((rows_p, d), lambda i: (i, 0)),
                   pl.BlockSpec((TOP_K, rows_p), lambda i: (0, i)),
                   pl.BlockSpec((TOP_K, rows_p), lambda i: (0, i)),
                   pl.BlockSpec((tpm, ne, 1), lambda i: (i, 0, 0))),
        out_shape=mix_out_shape,
        compiler_params=_params(("parallel",), 52),
        name="mix_prompt",
    )(x_prompt, y4, u4, ycn_p, *mix_w)

    y4_s = y_s.reshape(ns, nj, LANE).transpose(1, 0, 2).reshape(nj, 1, ns, LANE)
    u4_s = u_sm[:ns].astype(BF16).reshape(ns, nj, LANE).transpose(1, 0, 2).reshape(nj, 1, ns, LANE)
    last = n_tiles - 1
    any_spec = pl.BlockSpec(memory_space=pl.ANY)
    x1_all, hf_all, idx_all, gate_all, cnt = pl.pallas_call(
        _mix_kernel_sample,
        grid=(1,),
        in_specs=[_full((1, ns, d)), _full((nj, 1, ns, LANE)), _full((nj, 1, ns, LANE)),
                  _full((1, ns, cw))] + mix_w_specs + [any_spec] * 5,
        out_specs=(pl.BlockSpec((TOK_TILE, d), lambda i: (last, 0)),
                   pl.BlockSpec((TOK_TILE, d), lambda i: (last, 0)),
                   pl.BlockSpec((TOP_K, TOK_TILE), lambda i: (0, last)),
                   pl.BlockSpec((TOP_K, TOK_TILE), lambda i: (0, last)),
                   pl.BlockSpec((1, ne, 1), lambda i: (last, 0, 0))),
        out_shape=mix_out_shape,
        input_output_aliases={13: 0, 14: 1, 15: 2, 16: 3, 17: 4},
        compiler_params=_params(("arbitrary",), 32),
        name="mix_sample",
    )(x_sample.reshape(1, ns, d), y4_s, u4_s, ycn_s.reshape(1, ns, cw), *mix_w,
      x1_all, hf_all, idx_all, gate_all, cnt)

    bm = MOE_ROWS
    cnt2 = cnt.reshape(n_tiles, ne)
    before = jnp.cumsum(cnt2, axis=0) - cnt2
    count = jnp.sum(cnt2, axis=0)
    padded = ((count + WIN + bm - 1) // bm) * bm
    pend = jnp.cumsum(padded)
    pstart = pend - padded
    phase = before % SUBLANE
    span = jnp.where(cnt2 > 0, phase + cnt2, 0)
    gstart = (pstart[None, :] + before - phase).astype(I32).reshape(-1)
    nwin = ((span + WIN - 1) // WIN).astype(I32)
    reg8 = ((span + SUBLANE - 1) // SUBLANE) * SUBLANE
    loff = (jnp.cumsum(reg8, axis=1) - reg8).astype(I32)
    tail = jnp.where(span % SUBLANE != 0, loff + (span // SUBLANE) * SUBLANE, -1).astype(I32)
    twin = jnp.sum(nwin, axis=1).astype(I32)
    n_blocks = (ta * TOP_K + ne * (WIN + bm - 1) + bm - 1) // bm
    cap = n_blocks * bm
    blk0 = jnp.arange(n_blocks, dtype=I32) * bm
    blk_e = jnp.minimum(jnp.sum((pend[None, :] <= blk0[:, None]).astype(I32), axis=1), ne - 1)
    blk_valid = jnp.clip(count[blk_e] - (blk0 - pstart[blk_e]), 0, bm).astype(I32)
    nvalid = jnp.full((1,), tall, I32)
    loff_al = (WIN * (jnp.cumsum(nwin, axis=1) - nwin)).astype(I32)
    tables_d = (gstart, nwin.reshape(-1), loff.reshape(-1), tail.reshape(-1), twin, nvalid)
    tables_c = (gstart, nwin.reshape(-1), loff_al.reshape(-1), twin, nvalid)
    rowoff_d = (loff + phase).astype(I32).reshape(n_tiles, ne, 1)
    rowoff_c = (loff_al + phase).astype(I32).reshape(n_tiles, ne, 1)
    xw = d // 2 + GATE_COLS

    xs = pl.pallas_call(
        _dispatch_kernel,
        grid_spec=pltpu.PrefetchScalarGridSpec(
            num_scalar_prefetch=6,
            grid=(n_tiles,),
            in_specs=[pl.BlockSpec((TOK_TILE, d), lambda i, *_: (i, 0)),
                      pl.BlockSpec((TOP_K, TOK_TILE), lambda i, *_: (0, i)),
                      pl.BlockSpec((TOP_K, TOK_TILE), lambda i, *_: (0, i)),
                      pl.BlockSpec((1, ne, 1), lambda i, *_: (i, 0, 0))],
            out_specs=pl.BlockSpec(memory_space=pl.ANY),
            scratch_shapes=[pltpu.VMEM((2, _dispatch_run_rows(ne), xw), U32),
                            pltpu.VMEM((ne, SUBLANE, xw), U32), pltpu.SemaphoreType.DMA((2,))]),
        out_shape=jax.ShapeDtypeStruct((cap, xw), U32),
        compiler_params=_params(("arbitrary",), 40),
        name="dispatch",
    )(*tables_d, hf_all, idx_all, gate_all, rowoff_d)

    yb = pl.pallas_call(
        _moe_kernel,
        grid_spec=pltpu.PrefetchScalarGridSpec(
            num_scalar_prefetch=2,
            grid=(n_blocks,),
            in_specs=[pl.BlockSpec((bm, xw), lambda i, be, bv: (i, 0)),
                      pl.BlockSpec((1, d, 2 * dff), lambda i, be, bv: (be[i], 0, 0)),
                      pl.BlockSpec((1, 1, 2 * dff), lambda i, be, bv: (be[i], 0, 0)),
                      pl.BlockSpec((1, dff, d), lambda i, be, bv: (be[i], 0, 0)),
                      pl.BlockSpec((1, 1, d), lambda i, be, bv: (be[i], 0, 0))],
            out_specs=pl.BlockSpec((bm, d // 2), lambda i, be, bv: (i, 0)),
            scratch_shapes=[pltpu.VMEM((d, 2 * dff), BF16), pltpu.VMEM((dff, d), BF16)]),
        out_shape=jax.ShapeDtypeStruct((cap, d // 2), U32),
        compiler_params=_params(("arbitrary",), 56),
        name="moe",
    )(blk_e, blk_valid, xs, w_gate_up[0], b_gate_up[0].reshape(ne, 1, 2 * dff), w_down[0],
      b_down[0].reshape(ne, 1, d))

    nfin = norm_final.reshape(1, d)
    nbh = TOK_TILE // tm

    def _tile_of(s):
        return (s + n_tiles - 1) % n_tiles

    def _yp_index(s, *_):
        t = jnp.maximum(s - 1, 0)
        return (t % tpm, t // tpm, 0)

    y_p, y_sm = pl.pallas_call(
        _combine_kernel,
        grid_spec=pltpu.PrefetchScalarGridSpec(
            num_scalar_prefetch=5,
            grid=(n_tiles,),
            in_specs=[pl.BlockSpec((TOK_TILE, d), lambda s, *_: (_tile_of(s), 0)),
                      pl.BlockSpec((TOP_K, TOK_TILE), lambda s, *_: (0, _tile_of(s))),
                      pl.BlockSpec((1, ne, 1), lambda s, *_: (_tile_of(s), 0, 0)),
                      pl.BlockSpec((1, d), lambda s, *_: (0, 0)),
                      pl.BlockSpec(memory_space=pl.ANY)],
            out_specs=(pl.BlockSpec((nbh, tm, d), _yp_index),
                       pl.BlockSpec((ns, d), lambda s, *_: (0, 0))),
            scratch_shapes=[pltpu.VMEM((2, _combine_run_rows(ne), d // 2), U32),
                            pltpu.SemaphoreType.DMA((2,))]),
        out_shape=(jax.ShapeDtypeStruct((nb, seq, d), F32), jax.ShapeDtypeStruct((ns, d), F32)),
        compiler_params=_params(("arbitrary",), 48),
        name="combine",
    )(*tables_c, x1_all, idx_all, rowoff_c, nfin, yb)

    return (y_p, y_sm.reshape(ns, 1, d), new_re_p, new_im_p, new_conv_p,
            new_re_s, new_im_s, new_conv_s)
```

```python
import math

import jax
import jax.numpy as jnp
from jax import lax
from jax.experimental import pallas as pl
from jax.experimental.pallas import tpu as pltpu

F32 = jnp.float32
BF16 = jnp.bfloat16
U32 = jnp.uint32
I32 = jnp.int32
EPS = 1e-5
CHUNK = 16
LANE = 128
TOP_K = 4
SWIGLU_LIMIT = 7.0
SWIGLU_ALPHA = 1.702
MOE_ROWS = 512
MOE_SUB = 256
TOK_TILE = 256
WIN = 32
SUBLANE = 8
GATE_COLS = LANE


def _dispatch_run_rows(ne):
    return TOP_K * TOK_TILE + ne * 2 * (SUBLANE - 1) + WIN


def _combine_run_rows(ne):
    return -(-(TOP_K * TOK_TILE + ne * (SUBLANE - 1 + WIN - 1)) // WIN) * WIN
HI_MASK = 0xFFFF0000
MIB = 1024 * 1024


def _rms(x, g):
    return x * lax.rsqrt(jnp.mean(x * x, axis=-1, keepdims=True) + EPS) * g


def _gelu_tanh(x):
    c = math.sqrt(2.0 / math.pi)
    return 0.5 * x * (1.0 + jnp.tanh(c * (x + 0.044715 * (x * x * x))))


def _params(sem, vmem_mib):
    return pltpu.CompilerParams(dimension_semantics=sem, vmem_limit_bytes=vmem_mib * MIB)


def _pack_pairs(a, b):
    return (pltpu.bitcast(a, U32) >> 16) | (pltpu.bitcast(b, U32) & jnp.uint32(HI_MASK))


def _unpack_pairs(w):
    lo = pltpu.bitcast(w << 16, F32)
    hi = pltpu.bitcast(w & jnp.uint32(HI_MASK), F32)
    return jnp.concatenate([lo, hi], axis=-1).astype(BF16)


def _small_front_kernel(x_ref, nmix_ref, win_ref, s0r_ref, s0i_ref, b0_ref, b1_ref, cw_ref,
                        bdb_ref, cm_ref, abr_ref, abi_ref, nconv_ref,
                        u_ref, z_ref, y_ref, ycn_ref, sr_ref, si_ref):
    ns, nst = s0r_ref.shape
    cw = u_ref.shape[1]
    h = _rms(x_ref[...], nmix_ref[...]).astype(BF16)
    proj = jnp.dot(h, win_ref[...], preferred_element_type=F32)
    u = proj[:, 0:cw]
    zc = proj[:, cw:2 * cw]
    gb = proj[:, 2 * cw:3 * cw]
    gc = proj[:, 3 * cw:4 * cw]
    z = gc * zc
    u_ref[...] = u
    z_ref[...] = z
    bu = jnp.dot(u[:ns].astype(BF16), bdb_ref[...], preferred_element_type=F32)
    abr = abr_ref[...]
    abi = abi_ref[...]
    s0r = s0r_ref[...]
    s0i = s0i_ref[...]
    sr = abr * s0r - abi * s0i + bu[:, :nst]
    si = abr * s0i + abi * s0r + bu[:, nst:]
    sr_ref[...] = sr
    si_ref[...] = si
    scat = jnp.concatenate([sr, si], axis=-1).astype(BF16)
    y_ref[...] = jnp.dot(scat, cm_ref[...], preferred_element_type=F32)
    conv = cw_ref[0:1, :] * b0_ref[...] + cw_ref[1:2, :] * b1_ref[...] + cw_ref[2:3, :] * z[:ns]
    ycn_ref[...] = _rms(gb[:ns] * conv, nconv_ref[...]).astype(BF16)


def _front_kernel(x_ref, nmix_ref, win_ref, zm_ref, cw_ref, nconv_ref,
                  u4_ref, ycn_ref, zt_ref, zbuf):
    i = pl.program_id(0)
    nb, tt, d = x_ref.shape
    cw = ycn_ref.shape[2]
    rows = nb * tt

    @pl.when(i == 0)
    def _():
        zbuf[:, 0:8, :] = jnp.broadcast_to(zm_ref[...][None], (nb, 8, cw))

    h = _rms(x_ref[...].reshape(rows, d), nmix_ref[...]).astype(BF16)
    u = jnp.dot(h, win_ref[:, 0:cw], preferred_element_type=F32)
    for j in range(cw // LANE):
        u4_ref[j] = u[:, j * LANE:(j + 1) * LANE].astype(BF16).reshape(nb, tt, LANE)
    zc = jnp.dot(h, win_ref[:, cw:2 * cw], preferred_element_type=F32)
    gc = jnp.dot(h, win_ref[:, 3 * cw:4 * cw], preferred_element_type=F32)
    z3 = (gc * zc).reshape(nb, tt, cw)
    zbuf[:, 8:8 + tt, :] = z3
    z1 = zbuf[:, 7:7 + tt, :]
    z2 = zbuf[:, 6:6 + tt, :]
    conv = cw_ref[0:1, :] * z2 + cw_ref[1:2, :] * z1 + cw_ref[2:3, :] * z3
    gb = jnp.dot(h, win_ref[:, 2 * cw:3 * cw], preferred_element_type=F32)
    yc = gb * conv.reshape(rows, cw)
    ycn_ref[...] = _rms(yc, nconv_ref[...]).astype(BF16).reshape(nb, tt, cw)
    tail = zbuf[:, tt:tt + 8, :]
    zt_ref[...] = tail
    zbuf[:, 0:8, :] = tail


def _ssm_kernel(u_ref, um_ref, t_ref, p_ref, r_ref, a16_ref, y_ref, sl_ref,
                s_carry, ds_ref, sp_ref):
    th = pl.program_id(1)
    _, nb, cc, w = u_ref.shape
    nst = p_ref.shape[2]
    half = nst // 2
    rows = nb * cc
    blk = 2 * LANE
    u = u_ref[0].reshape(rows, w)

    @pl.when(th == 0)
    def _():
        ds_ref[:, 0:8, :] = jnp.dot(um_ref[0], p_ref[0], preferred_element_type=F32).reshape(nb, 8, nst)
        s_carry[...] = ds_ref[:, 0:1, :]

    ds_ref[...] = jnp.dot(u, p_ref[0], preferred_element_type=F32).reshape(nb, cc, nst)
    ar = a16_ref[0, 0:1, :].reshape(1, 1, half)
    ai = a16_ref[0, 1:2, :].reshape(1, 1, half)
    sr = s_carry[:, :, 0:half]
    si = s_carry[:, :, half:nst]
    for c in range(cc):
        sp_ref[:, c:c + 1, 0:half] = sr
        sp_ref[:, c:c + 1, half:nst] = si
        dr = ds_ref[:, c:c + 1, 0:half]
        di = ds_ref[:, c:c + 1, half:nst]
        sr, si = ar * sr - ai * si + dr, ar * si + ai * sr + di
    s_carry[:, :, 0:half] = sr
    s_carry[:, :, half:nst] = si
    sl_ref[0, :, :, 0:half] = sr
    sl_ref[0, :, :, half:nst] = si

    sp = sp_ref[...].reshape(rows, nst).astype(BF16)
    for tb in range(w // blk):
        acc = jnp.dot(sp, r_ref[0, :, tb * blk:(tb + 1) * blk], preferred_element_type=F32)
        for sb in range(tb + 1):
            acc = acc + jnp.dot(u[:, sb * blk:(sb + 1) * blk], t_ref[0, tb - sb],
                                preferred_element_type=F32)
        y_ref[0, :, :, tb * blk:(tb + 1) * blk] = acc.reshape(nb, cc, blk)


def _mix_compute(x_ref, y4_ref, u4_ref, ycn_ref, dsk_ref, wglu_ref, bglu_ref, nssm_ref, wout_ref,
                 nffn_ref, wrh_ref, wrl_ref, br_ref):
    nb, tt, d = x_ref.shape
    rows = nb * tt
    nj = y4_ref.shape[0]
    ne = br_ref.shape[0]
    ys = []
    for j in range(nj):
        yj = (y4_ref[j].reshape(rows, LANE)
              + dsk_ref[:, j * LANE:(j + 1) * LANE] * u4_ref[j].reshape(rows, LANE).astype(F32))
        ys.append(_gelu_tanh(yj))
    y = jnp.concatenate(ys, axis=-1)
    glu = jnp.dot(y.astype(BF16), wglu_ref[...], preferred_element_type=F32) + bglu_ref[...]
    o = y * jax.nn.sigmoid(glu)
    ysn = _rms(o, nssm_ref[...]).astype(BF16)
    mix = jnp.concatenate([ysn, ycn_ref[...].reshape(rows, ycn_ref.shape[2])], axis=-1)
    x1 = x_ref[...].reshape(rows, d) + jnp.dot(mix, wout_ref[...], preferred_element_type=F32)
    hf = _rms(x1, nffn_ref[...])
    hf_hi = hf.astype(BF16)
    hf_lo = (hf - hf_hi.astype(F32)).astype(BF16)
    logits = (jnp.dot(hf_hi, wrh_ref[...], preferred_element_type=F32)
              + jnp.dot(hf_lo, wrh_ref[...], preferred_element_type=F32)
              + jnp.dot(hf_hi, wrl_ref[...], preferred_element_type=F32))
    lt = logits.T[0:ne, :] + br_ref[...]
    iota = lax.broadcasted_iota(I32, lt.shape, 0)
    vals, idxs = [], []
    sel = jnp.zeros(lt.shape, F32)
    for _ in range(TOP_K):
        m = jnp.max(lt, axis=0, keepdims=True)
        ik = jnp.min(jnp.where(lt == m, iota, ne), axis=0, keepdims=True)
        vals.append(m)
        idxs.append(ik)
        hit = iota == ik
        sel = sel + jnp.where(hit, 1.0, 0.0)
        lt = jnp.where(hit, -jnp.inf, lt)
    es = [jnp.exp(v - vals[0]) for v in vals]
    tot = es[0] + es[1] + es[2] + es[3]
    idx = jnp.concatenate(idxs, axis=0)
    gates = jnp.concatenate([e / tot for e in es], axis=0)
    return x1, hf_hi, idx, gates, sel


def _mix_kernel_prompt(*refs):
    x1, hf, idx, gates, sel = _mix_compute(*refs[:13])
    x1_ref, hf_ref, idx_ref, gate_ref, cnt_ref = refs[13:]
    x1_ref[...] = x1
    hf_ref[...] = hf
    idx_ref[...] = idx
    gate_ref[...] = gates
    for t in range(cnt_ref.shape[0]):
        part = sel[:, t * TOK_TILE:(t + 1) * TOK_TILE]
        cnt_ref[t] = jnp.sum(part, axis=1, keepdims=True).astype(I32)


def _mix_kernel_sample(*refs):
    x1, hf, idx, gates, sel = _mix_compute(*refs[:13])
    x1_ref, hf_ref, idx_ref, gate_ref, cnt_ref = refs[18:]
    ns = x1.shape[0]
    x1_ref[...] = jnp.zeros(x1_ref.shape, x1_ref.dtype)
    hf_ref[...] = jnp.zeros(hf_ref.shape, hf_ref.dtype)
    idx_ref[...] = jnp.zeros(idx_ref.shape, idx_ref.dtype)
    gate_ref[...] = jnp.zeros(gate_ref.shape, gate_ref.dtype)
    x1_ref[0:ns, :] = x1
    hf_ref[0:ns, :] = hf
    idx_ref[:, 0:ns] = idx
    gate_ref[:, 0:ns] = gates
    cnt_ref[0] = jnp.sum(sel, axis=1, keepdims=True).astype(I32)


def _local_rows(idx_ref, loff_ref, tile, n_valid):
    ne = loff_ref.shape[1]
    tt = idx_ref.shape[1]
    e_iota = lax.broadcasted_iota(I32, (ne, tt), 0)
    tok = tile * tt + lax.broadcasted_iota(I32, (1, tt), 1)
    valid = tok < n_valid
    hits = [jnp.logical_and(e_iota == idx_ref[k:k + 1, :], valid) for k in range(TOP_K)]
    sel = jnp.zeros((ne, tt), F32)
    for h in hits:
        sel = sel + jnp.where(h, 1.0, 0.0)
    before = lax.broadcasted_iota(I32, (tt, tt), 0) < lax.broadcasted_iota(I32, (tt, tt), 1)
    tri = jnp.where(before, 1.0, 0.0).astype(BF16)
    base = jnp.dot(sel.astype(BF16), tri, preferred_element_type=F32) + loff_ref[0].astype(F32)
    rows = []
    for h in hits:
        r = jnp.sum(jnp.where(h, base, 0.0), axis=0, keepdims=True)
        rows.append(jnp.where(valid, r, -1.0))
    return rows


def _window_copy(buf, slot, hbm, lo, g, sem, to_hbm):
    src = buf.at[slot, pl.ds(pl.multiple_of(lo, SUBLANE), WIN)]
    dst = hbm.at[pl.ds(pl.multiple_of(g, SUBLANE), WIN)]
    if to_hbm:
        return pltpu.make_async_copy(src, dst, sem.at[slot])
    return pltpu.make_async_copy(dst, src, sem.at[slot])


def _start_windows(gs_ref, nw_ref, lo_ref, tile, ne, buf, slot, hbm, sem, to_hbm):
    def per_expert(e, carry):
        g = gs_ref[tile * ne + e]
        lo = lo_ref[tile * ne + e]

        def per_window(w, c):
            _window_copy(buf, slot, hbm, lo + w * WIN, g + w * WIN, sem, to_hbm).start()
            return c

        return lax.fori_loop(0, nw_ref[tile * ne + e], per_window, carry)

    lax.fori_loop(0, ne, per_expert, 0)


def _wait_windows(count, buf, slot, hbm, sem, to_hbm):
    def per_window(w, c):
        _window_copy(buf, slot, hbm, 0, 0, sem, to_hbm).wait()
        return c

    lax.fori_loop(0, count, per_window, 0)


def _dispatch_kernel(gs_ref, nw_ref, lo_ref, tg_ref, tw_ref, nv_ref,
                     hf_ref, idx_ref, gate_ref, loff_ref, xs_ref, buf, carry, sem):
    i = pl.program_id(0)
    nt = pl.num_programs(0)
    ne = loff_ref.shape[1]
    tt, d = hf_ref.shape
    nrun = buf.shape[1]
    slot = i % 2

    @pl.when(i == 0)
    def _():
        carry[...] = jnp.zeros(carry.shape, carry.dtype)

    rows = _local_rows(idx_ref, loff_ref, i, nv_ref[0])
    r_iota = lax.broadcasted_iota(I32, (nrun, tt), 0).astype(F32)
    smat = jnp.zeros((nrun, tt), F32)
    gmat = jnp.zeros((nrun, tt), F32)
    for k, r in enumerate(rows):
        hit = r_iota == r
        smat = smat + jnp.where(hit, 1.0, 0.0)
        gmat = gmat + jnp.where(hit, gate_ref[k:k + 1, :], 0.0)
    xr = jnp.dot(smat.astype(BF16), hf_ref[...], preferred_element_type=F32)
    gcol = jnp.sum(gmat, axis=1, keepdims=True)
    lane0 = lax.broadcasted_iota(I32, (nrun, GATE_COLS), 1) == 0
    buf[slot, :, 0:d // 2] = _pack_pairs(xr[:, :d // 2], xr[:, d // 2:])
    buf[slot, :, d // 2:] = pltpu.bitcast(jnp.where(lane0, gcol, 0.0), U32)

    def merge(e, c):
        @pl.when(nw_ref[i * ne + e] > 0)
        def _():
            lo = pl.multiple_of(lo_ref[i * ne + e], SUBLANE)
            buf[slot, pl.ds(lo, SUBLANE), :] = buf[slot, pl.ds(lo, SUBLANE), :] | carry[e]
            tg = tg_ref[i * ne + e]

            @pl.when(tg >= 0)
            def _():
                carry[e] = buf[slot, pl.ds(pl.multiple_of(tg, SUBLANE), SUBLANE), :]

            @pl.when(tg < 0)
            def _():
                carry[e] = jnp.zeros(carry.shape[1:], carry.dtype)

        return c

    lax.fori_loop(0, ne, merge, 0)

    @pl.when(i > 0)
    def _():
        _wait_windows(tw_ref[i - 1], buf, 1 - slot, xs_ref, sem, True)

    _start_windows(gs_ref, nw_ref, lo_ref, i, ne, buf, slot, xs_ref, sem, True)

    @pl.when(i == nt - 1)
    def _():
        _wait_windows(tw_ref[i], buf, slot, xs_ref, sem, True)


def _expert_weight_copies(wg_hbm, wd_hbm, wg_f32, wd_f32, sem, e, slot):
    return (pltpu.make_async_copy(wg_hbm.at[e], wg_f32.at[slot], sem.at[0, slot]),
            pltpu.make_async_copy(wd_hbm.at[e], wd_f32.at[slot], sem.at[1, slot]))


def _moe_kernel(be_ref, bv_ref, first_ref, nxt_ref, slot_ref,
                x_ref, bg_ref, bd_ref, wg_hbm, wd_hbm, y_ref,
                wg_f32, wd_f32, wg_bf, wd_bf, sem):
    i = pl.program_id(0)
    e = be_ref[i]
    dff = wd_bf.shape[0]
    bm = x_ref.shape[0]
    sub = MOE_SUB
    nw = x_ref.shape[1] - GATE_COLS

    @pl.when(first_ref[i] > 0)
    def _():
        slot = slot_ref[i]

        @pl.when(first_ref[i] > 1)
        def _():
            for cp in _expert_weight_copies(wg_hbm, wd_hbm, wg_f32, wd_f32, sem, e, slot):
                cp.start()

        for cp in _expert_weight_copies(wg_hbm, wd_hbm, wg_f32, wd_f32, sem, e, slot):
            cp.wait()

        @pl.when(nxt_ref[i] >= 0)
        def _():
            for cp in _expert_weight_copies(wg_hbm, wd_hbm, wg_f32, wd_f32, sem, nxt_ref[i], 1 - slot):
                cp.start()

        wg_bf[...] = wg_f32[slot].astype(BF16)
        wd_bf[...] = wd_f32[slot].astype(BF16)

    def rows(r0, nrows):
        for c in range(nrows // sub):
            lo = r0 + c * sub
            live = lax.broadcasted_iota(I32, (sub, 1), 0) + lo < bv_ref[i]
            x = jnp.where(live, _unpack_pairs(x_ref[lo:lo + sub, 0:nw]), jnp.zeros((), BF16))
            route = jnp.where(live, pltpu.bitcast(x_ref[lo:lo + sub, nw:], F32)[:, 0:1], 0.0)
            gu = jnp.dot(x, wg_bf[...], preferred_element_type=F32) + bg_ref[0]
            gate = jnp.minimum(gu[:, :dff], SWIGLU_LIMIT)
            up = jnp.clip(gu[:, dff:], -SWIGLU_LIMIT, SWIGLU_LIMIT)
            h = gate * jax.nn.sigmoid(SWIGLU_ALPHA * gate) * (up + 1.0)
            y = jnp.dot(h.astype(BF16), wd_bf[...], preferred_element_type=F32) + bd_ref[0]
            yr = (route * y).astype(BF16).astype(F32)
            half = yr.shape[1] // 2
            y_ref[lo:lo + sub, :] = _pack_pairs(yr[:, :half], yr[:, half:])

    half_rows = bm // 2

    @pl.when(bv_ref[i] > half_rows)
    def _():
        rows(0, bm)

    @pl.when(jnp.logical_and(bv_ref[i] > 0, bv_ref[i] <= half_rows))
    def _():
        rows(0, half_rows)
        y_ref[half_rows:bm, :] = jnp.zeros((bm - half_rows, y_ref.shape[1]), y_ref.dtype)

    @pl.when(bv_ref[i] <= 0)
    def _():
        y_ref[...] = jnp.zeros(y_ref.shape, y_ref.dtype)


def _combine_tile(step, nt):
    return (step + nt - 1) % nt


def _combine_kernel(gs_ref, nw_ref, lo_ref, tw_ref, nv_ref,
                    x1_ref, idx_ref, loff_ref, nf_ref, yb_ref,
                    yp_ref, ys_ref, buf, sem):
    s = pl.program_id(0)
    nt = pl.num_programs(0)
    ne = loff_ref.shape[1]
    tt, d = x1_ref.shape
    nrun = buf.shape[1]
    tile = _combine_tile(s, nt)
    slot = s % 2

    @pl.when(s == 0)
    def _():
        buf[...] = jnp.zeros(buf.shape, buf.dtype)
        _start_windows(gs_ref, nw_ref, lo_ref, tile, ne, buf, slot, yb_ref, sem, False)

    @pl.when(s + 1 < nt)
    def _():
        _start_windows(gs_ref, nw_ref, lo_ref, _combine_tile(s + 1, nt), ne, buf, 1 - slot, yb_ref,
                       sem, False)

    rows = _local_rows(idx_ref, loff_ref, tile, nv_ref[0])
    r_iota = lax.broadcasted_iota(I32, (nrun, tt), 0).astype(F32)
    smat = jnp.zeros((nrun, tt), F32)
    for r in rows:
        smat = smat + jnp.where(r_iota == r, 1.0, 0.0)

    _wait_windows(tw_ref[tile], buf, slot, yb_ref, sem, False)
    yrun = _unpack_pairs(buf[slot])
    moe = lax.dot_general(smat.astype(BF16), yrun, (((0,), (0,)), ((), ())),
                          preferred_element_type=F32)
    out = _rms(x1_ref[...] + moe, nf_ref[...])

    @pl.when(s == 0)
    def _():
        ys_ref[...] = out[0:ys_ref.shape[0], :]

    @pl.when(s > 0)
    def _():
        yp_ref[...] = out.reshape(yp_ref.shape)


def _ssm_matrices(a_re, a_im, log_dt, b_re, b_im, c_re, c_im):
    g, n = a_re.shape
    hch = b_re.shape[2]
    gpt = LANE // hch
    nj = g // gpt
    a = lax.complex(a_re, a_im)
    dta = a * jnp.exp(log_dt)[:, None]
    a_bar = jnp.exp(dta)
    bb = ((a_bar - 1.0) / a)[:, :, None] * lax.complex(b_re, b_im)
    cc = lax.complex(c_re, c_im)
    ks = jnp.arange(CHUNK + 1, dtype=F32)
    pw = jnp.exp(dta[None] * ks[:, None, None])
    eye_t = jnp.eye(gpt, dtype=F32)
    kk = jnp.real(jnp.einsum('gon,kgn,gni->kgio', cc, pw[:CHUNK], bb))
    kk = kk.reshape(CHUNK, nj, gpt, hch, hch)
    bd = jnp.einsum('kjgio,gf->kjgifo', kk, eye_t).reshape(CHUNK, nj, LANE, LANE)
    bd = jnp.concatenate([jnp.zeros_like(bd[:1]), bd], axis=0)
    nd = CHUNK // 2
    tiles = []
    for dlt in range(nd):
        top = jnp.concatenate([bd[2 * dlt + 1], bd[2 * dlt + 2]], axis=-1)
        bot = jnp.concatenate([bd[2 * dlt], bd[2 * dlt + 1]], axis=-1)
        tiles.append(jnp.concatenate([top, bot], axis=-2))
    t_mat = jnp.stack(tiles, axis=1).astype(BF16)
    pc = pw[CHUNK - 1::-1][:CHUNK, :, :, None] * bb[None]
    pc = pc.reshape(CHUNK, nj, gpt, n, hch)

    def _p(part):
        return jnp.einsum('sjgnh,gf->jsghfn', part, eye_t).reshape(nj, CHUNK * LANE, gpt * n)

    p_mat = jnp.concatenate([_p(jnp.real(pc)), _p(jnp.imag(pc))], axis=-1).astype(BF16)
    wc = cc[None] * pw[1:CHUNK + 1][:, :, None, :]
    wc = wc.reshape(CHUNK, nj, gpt, hch, n)

    def _r(part):
        return jnp.einsum('tjgon,gf->jgntfo', part, eye_t).reshape(nj, gpt * n, CHUNK * LANE)

    r_mat = jnp.concatenate([_r(jnp.real(wc)), _r(-jnp.imag(wc))], axis=1).astype(BF16)
    a16 = pw[CHUNK].reshape(nj, 1, gpt * n)
    a16 = jnp.concatenate([jnp.real(a16), jnp.imag(a16)], axis=1)
    eye_g = jnp.eye(g, dtype=F32)

    def _b(part):
        return jnp.einsum('gnh,gf->ghfn', part, eye_g).reshape(g * hch, g * n)

    bdb = jnp.concatenate([_b(jnp.real(bb)), _b(jnp.imag(bb))], axis=-1).astype(BF16)

    def _c(part):
        return jnp.einsum('gon,gf->gnfo', part, eye_g).reshape(g * n, g * hch)

    cm = jnp.concatenate([_c(jnp.real(cc)), _c(-jnp.imag(cc))], axis=0).astype(BF16)
    abr = jnp.real(a_bar).reshape(1, g * n)
    abi = jnp.imag(a_bar).reshape(1, g * n)
    return t_mat, p_mat, r_mat, a16, bdb, cm, abr, abi


def _full(shape):
    return pl.BlockSpec(shape, lambda *_: (0,) * len(shape))


def kernel(x_prompt, x_sample, state_ssm_re, state_ssm_im, state_conv, meta_tokens, norm_mix, w_in,
           ssm_a_re, ssm_a_im, ssm_log_dt, ssm_b_re, ssm_b_im, ssm_c_re, ssm_c_im, ssm_d, w_glu, b_glu,
           conv_w, norm_out_ssm, norm_out_conv, w_out, norm_ffn, w_router, b_router, w_gate_up,
           b_gate_up, w_down, b_down, norm_final):
    nb, seq, d = x_prompt.shape
    ns = x_sample.shape[0]
    depth, _, g, n = state_ssm_re.shape
    assert depth == 1 and x_sample.shape[1] == 1 and meta_tokens.shape[0] == CHUNK
    cw = conv_w.shape[2]
    nj = cw // LANE
    ne = w_router.shape[2]
    dff = w_down.shape[2]
    nst = g * n
    tt = 128
    n_tt = seq // tt
    n_chunks = seq // CHUNK
    tp = nb * seq
    tall = tp + ns
    tm = 64
    rows_p = nb * tm
    assert rows_p % TOK_TILE == 0 and TOK_TILE % tm == 0 and ns <= TOK_TILE and d % 2 == 0
    n_tiles = tp // TOK_TILE + 1
    ta = n_tiles * TOK_TILE

    t_mat, p_mat, r_mat, a16, bdb, cm, abr, abi = _ssm_matrices(
        ssm_a_re[0], ssm_a_im[0], ssm_log_dt[0], ssm_b_re[0], ssm_b_im[0], ssm_c_re[0], ssm_c_im[0])
    win_bf = w_in[0].astype(BF16)
    nmix = norm_mix[0].reshape(1, d)
    nconv = norm_out_conv[0].reshape(1, cw)
    cwt = conv_w[0]

    xsm = jnp.concatenate([x_sample.reshape(ns, d), meta_tokens], axis=0)
    nsm = ns + CHUNK
    s0r = state_ssm_re[0].reshape(ns, nst)
    s0i = state_ssm_im[0].reshape(ns, nst)
    buf0 = state_conv[0, :, 0, :]
    buf1 = state_conv[0, :, 1, :]
    u_sm, z_sm, y_s, ycn_s, sr_s, si_s = pl.pallas_call(
        _small_front_kernel,
        out_shape=(jax.ShapeDtypeStruct((nsm, cw), F32), jax.ShapeDtypeStruct((nsm, cw), F32),
                   jax.ShapeDtypeStruct((ns, cw), F32), jax.ShapeDtypeStruct((ns, cw), BF16),
                   jax.ShapeDtypeStruct((ns, nst), F32), jax.ShapeDtypeStruct((ns, nst), F32)),
        compiler_params=_params(None, 48),
        name="small_front",
    )(xsm, nmix, win_bf, s0r, s0i, buf0, buf1, cwt, bdb, cm, abr, abi, nconv)
    u_meta = u_sm[ns:]
    z_meta8 = z_sm[ns + CHUNK - 8:]
    new_conv_s = jnp.stack([buf1, z_sm[:ns]], axis=1)[None]
    new_re_s = sr_s.reshape(1, ns, g, n)
    new_im_s = si_s.reshape(1, ns, g, n)

    u4, ycn_p, ztail = pl.pallas_call(
        _front_kernel,
        grid=(n_tt,),
        in_specs=[pl.BlockSpec((nb, tt, d), lambda i: (0, i, 0)),
                  _full((1, d)), _full((d, 4 * cw)), _full((8, cw)), _full((3, cw)), _full((1, cw))],
        out_specs=(pl.BlockSpec((nj, nb, tt, LANE), lambda i: (0, 0, i, 0)),
                   pl.BlockSpec((nb, tt, cw), lambda i: (0, i, 0)),
                   pl.BlockSpec((nb, 8, cw), lambda i: (0, 0, 0))),
        out_shape=(jax.ShapeDtypeStruct((nj, nb, seq, LANE), BF16),
                   jax.ShapeDtypeStruct((nb, seq, cw), BF16),
                   jax.ShapeDtypeStruct((nb, 8, cw), F32)),
        scratch_shapes=[pltpu.VMEM((nb, tt + 8, cw), F32)],
        compiler_params=_params(("arbitrary",), 52),
        name="front",
    )(x_prompt, nmix, win_bf, z_meta8, cwt, nconv)
    new_conv_p = ztail[:, 6:8, :][None]

    cc = n_chunks // 2
    wch = CHUNK * LANE
    u4c = u4.reshape(nj, nb, n_chunks, wch)
    um = u_meta.reshape(CHUNK, nj, LANE).transpose(1, 0, 2).reshape(nj, 1, wch)
    um = jnp.broadcast_to(um, (nj, 8 * nb, wch)).astype(BF16)
    nstj = 2 * (LANE // (cw // g)) * n
    y4c, s_last = pl.pallas_call(
        _ssm_kernel,
        grid=(nj, n_chunks // cc),
        in_specs=[pl.BlockSpec((1, nb, cc, wch), lambda j, t: (j, 0, t, 0)),
                  pl.BlockSpec((1, 8 * nb, wch), lambda j, t: (j, 0, 0)),
                  pl.BlockSpec((1, CHUNK // 2, 2 * LANE, 2 * LANE), lambda j, t: (j, 0, 0, 0)),
                  pl.BlockSpec((1, wch, nstj), lambda j, t: (j, 0, 0)),
                  pl.BlockSpec((1, nstj, wch), lambda j, t: (j, 0, 0)),
                  pl.BlockSpec((1, 2, nstj // 2), lambda j, t: (j, 0, 0))],
        out_specs=(pl.BlockSpec((1, nb, cc, wch), lambda j, t: (j, 0, t, 0)),
                   pl.BlockSpec((1, nb, 1, nstj), lambda j, t: (j, 0, 0, 0))),
        out_shape=(jax.ShapeDtypeStruct((nj, nb, n_chunks, wch), F32),
                   jax.ShapeDtypeStruct((nj, nb, 1, nstj), F32)),
        scratch_shapes=[pltpu.VMEM((nb, 1, nstj), F32), pltpu.VMEM((nb, cc, nstj), F32),
                        pltpu.VMEM((nb, cc, nstj), F32)],
        compiler_params=_params(("parallel", "arbitrary"), 56),
        name="ssm",
    )(u4c, um, t_mat, p_mat, r_mat, a16)
    y4 = y4c.reshape(nj, nb, seq, LANE)
    gpt = g // nj
    sl = s_last.reshape(nj, nb, 2, gpt, n)
    new_re_p = sl[:, :, 0].transpose(1, 0, 2, 3).reshape(1, nb, g, n)
    new_im_p = sl[:, :, 1].transpose(1, 0, 2, 3).reshape(1, nb, g, n)

    dsk = ssm_d[0].reshape(1, cw)
    wglu_bf = w_glu[0].astype(BF16)
    bglu = b_glu[0].reshape(1, cw)
    nssm = norm_out_ssm[0].reshape(1, cw)
    wout_bf = w_out[0].astype(BF16)
    nffn = norm_ffn[0].reshape(1, d)
    wr_pad = jnp.zeros((d, LANE), F32).at[:, :ne].set(w_router[0])
    wr_hi = wr_pad.astype(BF16)
    wr_lo = (wr_pad - wr_hi.astype(F32)).astype(BF16)
    br = b_router[0].reshape(ne, 1)
    mix_w = (dsk, wglu_bf, bglu, nssm, wout_bf, nffn, wr_hi, wr_lo, br)
    mix_w_specs = [_full((1, cw)), _full((cw, cw)), _full((1, cw)), _full((1, cw)), _full((2 * cw, d)),
                   _full((1, d)), _full((d, LANE)), _full((d, LANE)), _full((ne, 1))]
    mix_out_shape = (jax.ShapeDtypeStruct((ta, d), F32), jax.ShapeDtypeStruct((ta, d), BF16),
                     jax.ShapeDtypeStruct((TOP_K, ta), I32), jax.ShapeDtypeStruct((TOP_K, ta), F32),
                     jax.ShapeDtypeStruct((n_tiles, ne, 1), I32))
    tpm = rows_p // TOK_TILE
    x1_all, hf_all, idx_all, gate_all, cnt = pl.pallas_call(
        _mix_kernel_prompt,
        grid=(seq // tm,),
        in_specs=[pl.BlockSpec((nb, tm, d), lambda i: (0, i, 0)),
                  pl.BlockSpec((nj, nb, tm, LANE), lambda i: (0, 0, i, 0)),
                  pl.BlockSpec((nj, nb, tm, LANE), lambda i: (0, 0, i, 0)),
                  pl.BlockSpec((nb, tm, cw), lambda i: (0, i, 0))] + mix_w_specs,
        out_specs=(pl.BlockSpec((rows_p, d), lambda i: (i, 0)),
                   pl.BlockSpec((rows_p, d), lambda i: (i, 0)),
                   pl.BlockSpec((TOP_K, rows_p), lambda i: (0, i)),
                   pl.BlockSpec((TOP_K, rows_p), lambda i: (0, i)),
                   pl.BlockSpec((tpm, ne, 1), lambda i: (i, 0, 0))),
        out_shape=mix_out_shape,
        compiler_params=_params(("parallel",), 52),
        name="mix_prompt",
    )(x_prompt, y4, u4, ycn_p, *mix_w)

    y4_s = y_s.reshape(ns, nj, LANE).transpose(1, 0, 2).reshape(nj, 1, ns, LANE)
    u4_s = u_sm[:ns].astype(BF16).reshape(ns, nj, LANE).transpose(1, 0, 2).reshape(nj, 1, ns, LANE)
    last = n_tiles - 1
    any_spec = pl.BlockSpec(memory_space=pl.ANY)
    x1_all, hf_all, idx_all, gate_all, cnt = pl.pallas_call(
        _mix_kernel_sample,
        grid=(1,),
        in_specs=[_full((1, ns, d)), _full((nj, 1, ns, LANE)), _full((nj, 1, ns, LANE)),
                  _full((1, ns, cw))] + mix_w_specs + [any_spec] * 5,
        out_specs=(pl.BlockSpec((TOK_TILE, d), lambda i: (last, 0)),
                   pl.BlockSpec((TOK_TILE, d), lambda i: (last, 0)),
                   pl.BlockSpec((TOP_K, TOK_TILE), lambda i: (0, last)),
                   pl.BlockSpec((TOP_K, TOK_TILE), lambda i: (0, last)),
                   pl.BlockSpec((1, ne, 1), lambda i: (last, 0, 0))),
        out_shape=mix_out_shape,
        input_output_aliases={13: 0, 14: 1, 15: 2, 16: 3, 17: 4},
        compiler_params=_params(("arbitrary",), 32),
        name="mix_sample",
    )(x_sample.reshape(1, ns, d), y4_s, u4_s, ycn_s.reshape(1, ns, cw), *mix_w,
      x1_all, hf_all, idx_all, gate_all, cnt)

    bm = MOE_ROWS
    cnt2 = cnt.reshape(n_tiles, ne)
    before = jnp.cumsum(cnt2, axis=0) - cnt2
    count = jnp.sum(cnt2, axis=0)
    padded = ((count + WIN + bm - 1) // bm) * bm
    pend = jnp.cumsum(padded)
    pstart = pend - padded
    phase = before % SUBLANE
    span = jnp.where(cnt2 > 0, phase + cnt2, 0)
    gstart = (pstart[None, :] + before - phase).astype(I32).reshape(-1)
    nwin = ((span + WIN - 1) // WIN).astype(I32)
    reg8 = ((span + SUBLANE - 1) // SUBLANE) * SUBLANE
    loff = (jnp.cumsum(reg8, axis=1) - reg8).astype(I32)
    tail = jnp.where(span % SUBLANE != 0, loff + (span // SUBLANE) * SUBLANE, -1).astype(I32)
    twin = jnp.sum(nwin, axis=1).astype(I32)
    n_blocks = (ta * TOP_K + ne * (WIN + bm - 1) + bm - 1) // bm
    cap = n_blocks * bm
    blk0 = jnp.arange(n_blocks, dtype=I32) * bm
    blk_e = jnp.minimum(jnp.sum((pend[None, :] <= blk0[:, None]).astype(I32), axis=1), ne - 1)
    e_ar = jnp.arange(ne, dtype=I32)
    blk_hot = blk_e[:, None] == e_ar[None, :]

    def _of_block(per_expert):
        return jnp.sum(jnp.where(blk_hot, per_expert[None, :], 0), axis=1)

    blk_valid = jnp.clip(_of_block(count) - (blk0 - _of_block(pstart)), 0, bm).astype(I32)
    has = count > 0
    later = jnp.logical_and(e_ar[None, :] > e_ar[:, None], has[None, :])
    nxt_e = jnp.min(jnp.where(later, e_ar[None, :], ne), axis=1)
    nxt_e = jnp.where(nxt_e < ne, nxt_e, -1)
    ordinal = jnp.cumsum(has.astype(I32)) - 1
    is_first = jnp.logical_and(blk_valid > 0, blk0 == _of_block(pstart))
    blk_first = jnp.where(is_first, jnp.where(_of_block(ordinal) == 0, 2, 1), 0).astype(I32)
    blk_next = _of_block(nxt_e).astype(I32)
    blk_slot = (_of_block(ordinal) % 2).astype(I32)
    nvalid = jnp.full((1,), tall, I32)
    loff_al = (WIN * (jnp.cumsum(nwin, axis=1) - nwin)).astype(I32)
    tables_d = (gstart, nwin.reshape(-1), loff.reshape(-1), tail.reshape(-1), twin, nvalid)
    tables_c = (gstart, nwin.reshape(-1), loff_al.reshape(-1), twin, nvalid)
    rowoff_d = (loff + phase).astype(I32).reshape(n_tiles, ne, 1)
    rowoff_c = (loff_al + phase).astype(I32).reshape(n_tiles, ne, 1)
    xw = d // 2 + GATE_COLS

    xs = pl.pallas_call(
        _dispatch_kernel,
        grid_spec=pltpu.PrefetchScalarGridSpec(
            num_scalar_prefetch=6,
            grid=(n_tiles,),
            in_specs=[pl.BlockSpec((TOK_TILE, d), lambda i, *_: (i, 0)),
                      pl.BlockSpec((TOP_K, TOK_TILE), lambda i, *_: (0, i)),
                      pl.BlockSpec((TOP_K, TOK_TILE), lambda i, *_: (0, i)),
                      pl.BlockSpec((1, ne, 1), lambda i, *_: (i, 0, 0))],
            out_specs=pl.BlockSpec(memory_space=pl.ANY),
            scratch_shapes=[pltpu.VMEM((2, _dispatch_run_rows(ne), xw), U32),
                            pltpu.VMEM((ne, SUBLANE, xw), U32), pltpu.SemaphoreType.DMA((2,))]),
        out_shape=jax.ShapeDtypeStruct((cap, xw), U32),
        compiler_params=_params(("arbitrary",), 40),
        name="dispatch",
    )(*tables_d, hf_all, idx_all, gate_all, rowoff_d)

    yb = pl.pallas_call(
        _moe_kernel,
        grid_spec=pltpu.PrefetchScalarGridSpec(
            num_scalar_prefetch=5,
            grid=(n_blocks,),
            in_specs=[pl.BlockSpec((bm, xw), lambda i, *_: (i, 0)),
                      pl.BlockSpec((1, 1, 2 * dff), lambda i, be, *_: (be[i], 0, 0)),
                      pl.BlockSpec((1, 1, d), lambda i, be, *_: (be[i], 0, 0)),
                      pl.BlockSpec(memory_space=pl.ANY), pl.BlockSpec(memory_space=pl.ANY)],
            out_specs=pl.BlockSpec((bm, d // 2), lambda i, *_: (i, 0)),
            scratch_shapes=[pltpu.VMEM((2, d, 2 * dff), F32), pltpu.VMEM((2, dff, d), F32),
                            pltpu.VMEM((d, 2 * dff), BF16), pltpu.VMEM((dff, d), BF16),
                            pltpu.SemaphoreType.DMA((2, 2))]),
        out_shape=jax.ShapeDtypeStruct((cap, d // 2), U32),
        compiler_params=_params(("arbitrary",), 58),
        name="moe",
    )(blk_e, blk_valid, blk_first, blk_next, blk_slot, xs, b_gate_up[0].reshape(ne, 1, 2 * dff),
      b_down[0].reshape(ne, 1, d), w_gate_up[0], w_down[0])

    nfin = norm_final.reshape(1, d)
    nbh = TOK_TILE // tm

    def _tile_of(s):
        return (s + n_tiles - 1) % n_tiles

    def _yp_index(s, *_):
        t = jnp.maximum(s - 1, 0)
        return (t % tpm, t // tpm, 0)

    y_p, y_sm = pl.pallas_call(
        _combine_kernel,
        grid_spec=pltpu.PrefetchScalarGridSpec(
            num_scalar_prefetch=5,
            grid=(n_tiles,),
            in_specs=[pl.BlockSpec((TOK_TILE, d), lambda s, *_: (_tile_of(s), 0)),
                      pl.BlockSpec((TOP_K, TOK_TILE), lambda s, *_: (0, _tile_of(s))),
                      pl.BlockSpec((1, ne, 1), lambda s, *_: (_tile_of(s), 0, 0)),
                      pl.BlockSpec((1, d), lambda s, *_: (0, 0)),
                      pl.BlockSpec(memory_space=pl.ANY)],
            out_specs=(pl.BlockSpec((nbh, tm, d), _yp_index),
                       pl.BlockSpec((ns, d), lambda s, *_: (0, 0))),
            scratch_shapes=[pltpu.VMEM((2, _combine_run_rows(ne), d // 2), U32),
                            pltpu.SemaphoreType.DMA((2,))]),
        out_shape=(jax.ShapeDtypeStruct((nb, seq, d), F32), jax.ShapeDtypeStruct((ns, d), F32)),
        compiler_params=_params(("arbitrary",), 48),
        name="combine",
    )(*tables_c, x1_all, idx_all, rowoff_c, nfin, yb)

    return (y_p, y_sm.reshape(ns, 1, d), new_re_p, new_im_p, new_conv_p,
            new_re_s, new_im_s, new_conv_s)
```

```python
import math

import jax
import jax.numpy as jnp
from jax import lax
from jax.experimental import pallas as pl
from jax.experimental.pallas import tpu as pltpu

F32 = jnp.float32
BF16 = jnp.bfloat16
U32 = jnp.uint32
I32 = jnp.int32
EPS = 1e-5
CHUNK = 16
LANE = 128
TOP_K = 4
SWIGLU_LIMIT = 7.0
SWIGLU_ALPHA = 1.702
MOE_ROWS = 512
MOE_SUB = 256
TOK_TILE = 256
WIN = 32
SUBLANE = 8
GATE_COLS = LANE


def _dispatch_run_rows(ne):
    return TOP_K * TOK_TILE + ne * 2 * (SUBLANE - 1) + WIN


def _combine_run_rows(ne):
    return -(-(TOP_K * TOK_TILE + ne * (SUBLANE - 1 + WIN - 1)) // WIN) * WIN
HI_MASK = 0xFFFF0000
MIB = 1024 * 1024


def _rms(x, g):
    return x * lax.rsqrt(jnp.mean(x * x, axis=-1, keepdims=True) + EPS) * g


def _gelu_tanh(x):
    c = math.sqrt(2.0 / math.pi)
    return 0.5 * x * (1.0 + jnp.tanh(c * (x + 0.044715 * (x * x * x))))


def _params(sem, vmem_mib):
    return pltpu.CompilerParams(dimension_semantics=sem, vmem_limit_bytes=vmem_mib * MIB)


def _pack_pairs(a, b):
    return (pltpu.bitcast(a, U32) >> 16) | (pltpu.bitcast(b, U32) & jnp.uint32(HI_MASK))


def _unpack_pairs(w):
    lo = pltpu.bitcast(w << 16, F32)
    hi = pltpu.bitcast(w & jnp.uint32(HI_MASK), F32)
    return jnp.concatenate([lo, hi], axis=-1).astype(BF16)


def _iota2(shape, axis):
    return lax.broadcasted_iota(I32, shape, axis)


def _expand_cols(compact, reps_log2, n_log2):
    q = _iota2((compact.shape[1], 2 << (reps_log2 + n_log2)), 0)
    c = _iota2((compact.shape[1], 2 << (reps_log2 + n_log2)), 1)
    nmask = (1 << n_log2) - 1
    same = jnp.logical_and((q >> n_log2) == (c >> (reps_log2 + n_log2)), (q & nmask) == (c & nmask))
    return jnp.dot(compact, jnp.where(same, 1.0, 0.0).astype(BF16), preferred_element_type=F32)


def _expand_rows(compact, reps_log2, n_log2):
    r = _iota2((2 << (reps_log2 + n_log2), compact.shape[0]), 0)
    q = _iota2((2 << (reps_log2 + n_log2), compact.shape[0]), 1)
    nmask = (1 << n_log2) - 1
    same = jnp.logical_and((r >> (reps_log2 + n_log2)) == (q >> n_log2), (r & nmask) == (q & nmask))
    return jnp.dot(jnp.where(same, 1.0, 0.0).astype(BF16), compact, preferred_element_type=F32)


def _group_mask(shape, row_shift, col_shift, ngroups):
    r = _iota2(shape, 0)
    c = _iota2(shape, 1)
    return ((r >> row_shift) & (ngroups - 1)) == ((c >> col_shift) & (ngroups - 1))


def _small_front_kernel(x_ref, nmix_ref, win_ref, s0r_ref, s0i_ref, b0_ref, b1_ref, cw_ref,
                        bc_ref, cc_ref, abr_ref, abi_ref, nconv_ref,
                        u_ref, z_ref, y_ref, ycn_ref, sr_ref, si_ref, bdb_ref, cm_ref):
    ns, nst = s0r_ref.shape
    cw = u_ref.shape[1]
    n = bc_ref.shape[1] // 2
    nlog = n.bit_length() - 1
    glog = (nst // n).bit_length() - 1
    hlog = (cw >> glog).bit_length() - 1
    bdb_ref[...] = jnp.where(_group_mask(bdb_ref.shape, hlog, nlog, 1 << glog),
                             _expand_cols(bc_ref[...], glog, nlog), 0.0).astype(BF16)
    cm_ref[...] = jnp.where(_group_mask(cm_ref.shape, nlog, hlog, 1 << glog),
                            _expand_rows(cc_ref[...], glog, nlog), 0.0).astype(BF16)
    h = _rms(x_ref[...], nmix_ref[...]).astype(BF16)
    proj = jnp.dot(h, win_ref[...], preferred_element_type=F32)
    u = proj[:, 0:cw]
    zc = proj[:, cw:2 * cw]
    gb = proj[:, 2 * cw:3 * cw]
    gc = proj[:, 3 * cw:4 * cw]
    z = gc * zc
    u_ref[...] = u
    z_ref[...] = z
    bu = jnp.dot(u[:ns].astype(BF16), bdb_ref[...], preferred_element_type=F32)
    abr = abr_ref[...]
    abi = abi_ref[...]
    s0r = s0r_ref[...]
    s0i = s0i_ref[...]
    sr = abr * s0r - abi * s0i + bu[:, :nst]
    si = abr * s0i + abi * s0r + bu[:, nst:]
    sr_ref[...] = sr
    si_ref[...] = si
    scat = jnp.concatenate([sr, si], axis=-1).astype(BF16)
    y_ref[...] = jnp.dot(scat, cm_ref[...], preferred_element_type=F32)
    conv = cw_ref[0:1, :] * b0_ref[...] + cw_ref[1:2, :] * b1_ref[...] + cw_ref[2:3, :] * z[:ns]
    ycn_ref[...] = _rms(gb[:ns] * conv, nconv_ref[...]).astype(BF16)


def _front_kernel(x_ref, nmix_ref, win_ref, zm_ref, cw_ref, nconv_ref,
                  uc_ref, ut_ref, ycn_ref, zt_ref, zbuf, ubuf):
    i = pl.program_id(1)
    nb, tt, d = x_ref.shape
    cw = ycn_ref.shape[2]
    rows = nb * tt
    ncz = tt // CHUNK

    @pl.when(i == 0)
    def _():
        zbuf[:, 0:8, :] = jnp.broadcast_to(zm_ref[...][None], (nb, 8, cw))

    h = _rms(x_ref[...].reshape(rows, d), nmix_ref[...]).astype(BF16)
    u = jnp.dot(h, win_ref[:, 0:cw], preferred_element_type=F32)
    ut_ref[...] = u.astype(BF16).reshape(nb, tt, cw)
    for j in range(cw // LANE):
        ubuf[j] = u[:, j * LANE:(j + 1) * LANE].reshape(nb, tt, LANE)
    for s in range(CHUNK):
        for j in range(cw // LANE):
            piece = ubuf[j, :, pl.ds(s, ncz, stride=CHUNK), :]
            uc_ref[j, :, :, s * LANE:(s + 1) * LANE] = piece.astype(BF16)
    zc = jnp.dot(h, win_ref[:, cw:2 * cw], preferred_element_type=F32)
    gc = jnp.dot(h, win_ref[:, 3 * cw:4 * cw], preferred_element_type=F32)
    z3 = (gc * zc).reshape(nb, tt, cw)
    zbuf[:, 8:8 + tt, :] = z3
    z1 = zbuf[:, 7:7 + tt, :]
    z2 = zbuf[:, 6:6 + tt, :]
    conv = cw_ref[0:1, :] * z2 + cw_ref[1:2, :] * z1 + cw_ref[2:3, :] * z3
    gb = jnp.dot(h, win_ref[:, 2 * cw:3 * cw], preferred_element_type=F32)
    yc = gb * conv.reshape(rows, cw)
    ycn_ref[...] = _rms(yc, nconv_ref[...]).astype(BF16).reshape(nb, tt, cw)
    tail = zbuf[:, tt:tt + 8, :]
    zt_ref[...] = tail
    zbuf[:, 0:8, :] = tail


def _ssm_kernel(u_ref, um_ref, kc_ref, pc_ref, rc_ref, a16_ref, y_ref, sl_ref,
                s_carry, ds_ref, sp_ref, p_s, r_s, t_s):
    th = pl.program_id(1)
    _, nb, cc, w = u_ref.shape
    nst = p_s.shape[1]
    half = nst // 2
    rows = nb * cc
    blk = 2 * LANE
    u = u_ref[0].reshape(rows, w)

    @pl.when(th == 0)
    def _():
        hch = kc_ref.shape[2]
        gpt = LANE // hch
        hlog = hch.bit_length() - 1
        glog = gpt.bit_length() - 1
        nlog = (pc_ref.shape[2] // 2).bit_length() - 1
        p_s[...] = jnp.where(_group_mask(p_s.shape, hlog, nlog, gpt),
                             _expand_cols(pc_ref[0], glog, nlog), 0.0).astype(BF16)
        r_s[...] = jnp.where(_group_mask(r_s.shape, nlog, hlog, gpt),
                             _expand_rows(rc_ref[0], glog, nlog), 0.0).astype(BF16)
        nlag = kc_ref.shape[1] // LANE
        o = _iota2((hch, LANE), 0)
        c = _iota2((hch, LANE), 1)
        spread = jnp.where((c & (hch - 1)) == o, 1.0, 0.0).astype(BF16)
        lagm = jnp.dot(kc_ref[0], spread, preferred_element_type=F32)
        r = _iota2(lagm.shape, 0)
        c = _iota2(lagm.shape, 1)
        lagm = jnp.where(((r >> hlog) & (gpt - 1)) == (c >> hlog), lagm, 0.0).astype(BF16)
        for dlt in range(nlag // 2):
            b0 = lagm[(2 * dlt) * LANE:(2 * dlt + 1) * LANE]
            b1 = lagm[(2 * dlt + 1) * LANE:(2 * dlt + 2) * LANE]
            b2 = lagm[(2 * dlt + 2) * LANE:(2 * dlt + 3) * LANE]
            t_s[dlt, 0:LANE, 0:LANE] = b1
            t_s[dlt, 0:LANE, LANE:blk] = b2
            t_s[dlt, LANE:blk, 0:LANE] = b0
            t_s[dlt, LANE:blk, LANE:blk] = b1
        ds_ref[:, 0:8, :] = jnp.dot(um_ref[0], p_s[...], preferred_element_type=F32).reshape(nb, 8, nst)
        s_carry[...] = ds_ref[:, 0:1, :]

    ds_ref[...] = jnp.dot(u, p_s[...], preferred_element_type=F32).reshape(nb, cc, nst)
    ar = a16_ref[0, 0:1, :].reshape(1, 1, half)
    ai = a16_ref[0, 1:2, :].reshape(1, 1, half)
    sr = s_carry[:, :, 0:half]
    si = s_carry[:, :, half:nst]
    for c in range(cc):
        sp_ref[:, c:c + 1, 0:half] = sr
        sp_ref[:, c:c + 1, half:nst] = si
        dr = ds_ref[:, c:c + 1, 0:half]
        di = ds_ref[:, c:c + 1, half:nst]
        sr, si = ar * sr - ai * si + dr, ar * si + ai * sr + di
    s_carry[:, :, 0:half] = sr
    s_carry[:, :, half:nst] = si
    sl_ref[0, :, :, 0:half] = sr
    sl_ref[0, :, :, half:nst] = si

    sp = sp_ref[...].reshape(rows, nst).astype(BF16)
    for tb in range(w // blk):
        acc = jnp.dot(sp, r_s[:, tb * blk:(tb + 1) * blk], preferred_element_type=F32)
        for sb in range(tb + 1):
            acc = acc + jnp.dot(u[:, sb * blk:(sb + 1) * blk], t_s[tb - sb],
                                preferred_element_type=F32)
        y_ref[0, :, :, tb * blk:(tb + 1) * blk] = acc.reshape(nb, cc, blk)


def _mix_compute(x_ref, yssm, ut_ref, ycn_ref, dsk_ref, wglu_ref, bglu_ref, nssm_ref, wout_ref,
                 nffn_ref, wrh_ref, wrl_ref, br_ref):
    nb, tt, d = x_ref.shape
    rows = nb * tt
    ne = br_ref.shape[0]
    cw = ut_ref.shape[2]
    y = _gelu_tanh(yssm + dsk_ref[...] * ut_ref[...].reshape(rows, cw).astype(F32))
    glu = jnp.dot(y.astype(BF16), wglu_ref[...], preferred_element_type=F32) + bglu_ref[...]
    o = y * jax.nn.sigmoid(glu)
    ysn = _rms(o, nssm_ref[...]).astype(BF16)
    mix = jnp.concatenate([ysn, ycn_ref[...].reshape(rows, ycn_ref.shape[2])], axis=-1)
    x1 = x_ref[...].reshape(rows, d) + jnp.dot(mix, wout_ref[...], preferred_element_type=F32)
    hf = _rms(x1, nffn_ref[...])
    hf_hi = hf.astype(BF16)
    hf_lo = (hf - hf_hi.astype(F32)).astype(BF16)
    logits = (jnp.dot(hf_hi, wrh_ref[...], preferred_element_type=F32)
              + jnp.dot(hf_lo, wrh_ref[...], preferred_element_type=F32)
              + jnp.dot(hf_hi, wrl_ref[...], preferred_element_type=F32))
    lt = logits.T[0:ne, :] + br_ref[...]
    iota = lax.broadcasted_iota(I32, lt.shape, 0)
    vals, idxs = [], []
    sel = jnp.zeros(lt.shape, F32)
    for _ in range(TOP_K):
        m = jnp.max(lt, axis=0, keepdims=True)
        ik = jnp.min(jnp.where(lt == m, iota, ne), axis=0, keepdims=True)
        vals.append(m)
        idxs.append(ik)
        hit = iota == ik
        sel = sel + jnp.where(hit, 1.0, 0.0)
        lt = jnp.where(hit, -jnp.inf, lt)
    es = [jnp.exp(v - vals[0]) for v in vals]
    tot = es[0] + es[1] + es[2] + es[3]
    idx = jnp.concatenate(idxs, axis=0)
    gates = jnp.concatenate([e / tot for e in es], axis=0)
    return x1, hf_hi, idx, gates, sel


def _mix_kernel_prompt(*refs):
    x_ref, yc_ref = refs[0], refs[1]
    x1_ref, hf_ref, idx_ref, gate_ref, cnt_ref, ybuf = refs[13:]
    nj, nb, ncz, _ = yc_ref.shape
    for s in range(CHUNK):
        for j in range(nj):
            ybuf[j, :, pl.ds(s, ncz, stride=CHUNK), :] = yc_ref[j, :, :, s * LANE:(s + 1) * LANE]
    yssm = jnp.concatenate([ybuf[j].reshape(nb * ncz * CHUNK, LANE) for j in range(nj)], axis=-1)
    x1, hf, idx, gates, sel = _mix_compute(x_ref, yssm, *refs[2:13])
    x1_ref[...] = x1
    hf_ref[...] = hf
    idx_ref[...] = idx
    gate_ref[...] = gates
    for t in range(cnt_ref.shape[0]):
        part = sel[:, t * TOK_TILE:(t + 1) * TOK_TILE]
        cnt_ref[t] = jnp.sum(part, axis=1, keepdims=True).astype(I32)


def _mix_kernel_sample(*refs):
    x1, hf, idx, gates, sel = _mix_compute(refs[0], refs[1][...], *refs[2:13])
    x1_ref, hf_ref, idx_ref, gate_ref, cnt_ref = refs[18:]
    ns = x1.shape[0]
    x1_ref[...] = jnp.zeros(x1_ref.shape, x1_ref.dtype)
    hf_ref[...] = jnp.zeros(hf_ref.shape, hf_ref.dtype)
    idx_ref[...] = jnp.zeros(idx_ref.shape, idx_ref.dtype)
    gate_ref[...] = jnp.zeros(gate_ref.shape, gate_ref.dtype)
    x1_ref[0:ns, :] = x1
    hf_ref[0:ns, :] = hf
    idx_ref[:, 0:ns] = idx
    gate_ref[:, 0:ns] = gates
    cnt_ref[0] = jnp.sum(sel, axis=1, keepdims=True).astype(I32)


def _local_rows(idx_ref, loff_ref, tile, n_valid):
    ne = loff_ref.shape[1]
    tt = idx_ref.shape[1]
    e_iota = lax.broadcasted_iota(I32, (ne, tt), 0)
    tok = tile * tt + lax.broadcasted_iota(I32, (1, tt), 1)
    valid = tok < n_valid
    hits = [jnp.logical_and(e_iota == idx_ref[k:k + 1, :], valid) for k in range(TOP_K)]
    sel = jnp.zeros((ne, tt), F32)
    for h in hits:
        sel = sel + jnp.where(h, 1.0, 0.0)
    before = lax.broadcasted_iota(I32, (tt, tt), 0) < lax.broadcasted_iota(I32, (tt, tt), 1)
    tri = jnp.where(before, 1.0, 0.0).astype(BF16)
    base = jnp.dot(sel.astype(BF16), tri, preferred_element_type=F32) + loff_ref[0].astype(F32)
    rows = []
    for h in hits:
        r = jnp.sum(jnp.where(h, base, 0.0), axis=0, keepdims=True)
        rows.append(jnp.where(valid, r, -1.0))
    return rows


def _window_copy(buf, slot, hbm, lo, g, sem, to_hbm):
    src = buf.at[slot, pl.ds(pl.multiple_of(lo, SUBLANE), WIN)]
    dst = hbm.at[pl.ds(pl.multiple_of(g, SUBLANE), WIN)]
    if to_hbm:
        return pltpu.make_async_copy(src, dst, sem.at[slot])
    return pltpu.make_async_copy(dst, src, sem.at[slot])


def _start_windows(gs_ref, nw_ref, lo_ref, tile, ne, buf, slot, hbm, sem, to_hbm):
    def per_expert(e, carry):
        g = gs_ref[tile * ne + e]
        lo = lo_ref[tile * ne + e]

        def per_window(w, c):
            _window_copy(buf, slot, hbm, lo + w * WIN, g + w * WIN, sem, to_hbm).start()
            return c

        return lax.fori_loop(0, nw_ref[tile * ne + e], per_window, carry)

    lax.fori_loop(0, ne, per_expert, 0)


def _wait_windows(count, buf, slot, hbm, sem, to_hbm):
    def per_window(w, c):
        _window_copy(buf, slot, hbm, 0, 0, sem, to_hbm).wait()
        return c

    lax.fori_loop(0, count, per_window, 0)


def _dispatch_kernel(gs_ref, nw_ref, lo_ref, tg_ref, tw_ref, nv_ref,
                     hf_ref, idx_ref, gate_ref, loff_ref, xs_ref, buf, carry, sem):
    i = pl.program_id(0)
    nt = pl.num_programs(0)
    ne = loff_ref.shape[1]
    tt, d = hf_ref.shape
    nrun = buf.shape[1]
    slot = i % 2

    @pl.when(i == 0)
    def _():
        carry[...] = jnp.zeros(carry.shape, carry.dtype)

    rows = _local_rows(idx_ref, loff_ref, i, nv_ref[0])
    r_iota = lax.broadcasted_iota(I32, (nrun, tt), 0).astype(F32)
    smat = jnp.zeros((nrun, tt), F32)
    gmat = jnp.zeros((nrun, tt), F32)
    for k, r in enumerate(rows):
        hit = r_iota == r
        smat = smat + jnp.where(hit, 1.0, 0.0)
        gmat = gmat + jnp.where(hit, gate_ref[k:k + 1, :], 0.0)
    xr = jnp.dot(smat.astype(BF16), hf_ref[...], preferred_element_type=F32)
    gcol = jnp.sum(gmat, axis=1, keepdims=True)
    lane0 = lax.broadcasted_iota(I32, (nrun, GATE_COLS), 1) == 0
    buf[slot, :, 0:d // 2] = _pack_pairs(xr[:, :d // 2], xr[:, d // 2:])
    buf[slot, :, d // 2:] = pltpu.bitcast(jnp.where(lane0, gcol, 0.0), U32)

    def merge(e, c):
        @pl.when(nw_ref[i * ne + e] > 0)
        def _():
            lo = pl.multiple_of(lo_ref[i * ne + e], SUBLANE)
            buf[slot, pl.ds(lo, SUBLANE), :] = buf[slot, pl.ds(lo, SUBLANE), :] | carry[e]
            tg = tg_ref[i * ne + e]

            @pl.when(tg >= 0)
            def _():
                carry[e] = buf[slot, pl.ds(pl.multiple_of(tg, SUBLANE), SUBLANE), :]

            @pl.when(tg < 0)
            def _():
                carry[e] = jnp.zeros(carry.shape[1:], carry.dtype)

        return c

    lax.fori_loop(0, ne, merge, 0)

    @pl.when(i > 0)
    def _():
        _wait_windows(tw_ref[i - 1], buf, 1 - slot, xs_ref, sem, True)

    _start_windows(gs_ref, nw_ref, lo_ref, i, ne, buf, slot, xs_ref, sem, True)

    @pl.when(i == nt - 1)
    def _():
        _wait_windows(tw_ref[i], buf, slot, xs_ref, sem, True)


def _expert_weight_copies(wg_hbm, wd_hbm, wg_f32, wd_f32, sem, e, slot):
    return (pltpu.make_async_copy(wg_hbm.at[e], wg_f32.at[slot], sem.at[0, slot]),
            pltpu.make_async_copy(wd_hbm.at[e], wd_f32.at[slot], sem.at[1, slot]))


def _moe_kernel(be_ref, bv_ref, first_ref, nxt_ref, slot_ref,
                x_ref, bg_ref, bd_ref, wg_hbm, wd_hbm, y_ref,
                wg_f32, wd_f32, wg_bf, wd_bf, sem):
    i = pl.program_id(0)
    e = be_ref[i]
    dff = wd_bf.shape[0]
    bm = x_ref.shape[0]
    sub = MOE_SUB
    nw = x_ref.shape[1] - GATE_COLS

    @pl.when(first_ref[i] > 0)
    def _():
        slot = slot_ref[i]

        @pl.when(first_ref[i] > 1)
        def _():
            for cp in _expert_weight_copies(wg_hbm, wd_hbm, wg_f32, wd_f32, sem, e, slot):
                cp.start()

        for cp in _expert_weight_copies(wg_hbm, wd_hbm, wg_f32, wd_f32, sem, e, slot):
            cp.wait()

        @pl.when(nxt_ref[i] >= 0)
        def _():
            for cp in _expert_weight_copies(wg_hbm, wd_hbm, wg_f32, wd_f32, sem, nxt_ref[i], 1 - slot):
                cp.start()

        wg_bf[...] = wg_f32[slot].astype(BF16)
        wd_bf[...] = wd_f32[slot].astype(BF16)

    def rows(r0, nrows):
        for c in range(nrows // sub):
            lo = r0 + c * sub
            live = lax.broadcasted_iota(I32, (sub, 1), 0) + lo < bv_ref[i]
            x = jnp.where(live, _unpack_pairs(x_ref[lo:lo + sub, 0:nw]), jnp.zeros((), BF16))
            route = jnp.where(live, pltpu.bitcast(x_ref[lo:lo + sub, nw:], F32)[:, 0:1], 0.0)
            gu = jnp.dot(x, wg_bf[...], preferred_element_type=F32) + bg_ref[0]
            gate = jnp.minimum(gu[:, :dff], SWIGLU_LIMIT)
            up = jnp.clip(gu[:, dff:], -SWIGLU_LIMIT, SWIGLU_LIMIT)
            h = gate * jax.nn.sigmoid(SWIGLU_ALPHA * gate) * (up + 1.0)
            y = jnp.dot(h.astype(BF16), wd_bf[...], preferred_element_type=F32) + bd_ref[0]
            yr = (route * y).astype(BF16).astype(F32)
            half = yr.shape[1] // 2
            y_ref[lo:lo + sub, :] = _pack_pairs(yr[:, :half], yr[:, half:])

    half_rows = bm // 2

    @pl.when(bv_ref[i] > half_rows)
    def _():
        rows(0, bm)

    @pl.when(jnp.logical_and(bv_ref[i] > 0, bv_ref[i] <= half_rows))
    def _():
        rows(0, half_rows)
        y_ref[half_rows:bm, :] = jnp.zeros((bm - half_rows, y_ref.shape[1]), y_ref.dtype)

    @pl.when(bv_ref[i] <= 0)
    def _():
        y_ref[...] = jnp.zeros(y_ref.shape, y_ref.dtype)


def _combine_tile(step, nt):
    return (step + nt - 1) % nt


def _combine_kernel(gs_ref, nw_ref, lo_ref, tw_ref, nv_ref,
                    x1_ref, idx_ref, loff_ref, nf_ref, yb_ref,
                    yp_ref, ys_ref, buf, sem):
    s = pl.program_id(0)
    nt = pl.num_programs(0)
    ne = loff_ref.shape[1]
    tt, d = x1_ref.shape
    nrun = buf.shape[1]
    tile = _combine_tile(s, nt)
    slot = s % 2

    @pl.when(s == 0)
    def _():
        buf[...] = jnp.zeros(buf.shape, buf.dtype)
        _start_windows(gs_ref, nw_ref, lo_ref, tile, ne, buf, slot, yb_ref, sem, False)

    @pl.when(s + 1 < nt)
    def _():
        _start_windows(gs_ref, nw_ref, lo_ref, _combine_tile(s + 1, nt), ne, buf, 1 - slot, yb_ref,
                       sem, False)

    rows = _local_rows(idx_ref, loff_ref, tile, nv_ref[0])
    r_iota = lax.broadcasted_iota(I32, (nrun, tt), 0).astype(F32)
    smat = jnp.zeros((nrun, tt), F32)
    for r in rows:
        smat = smat + jnp.where(r_iota == r, 1.0, 0.0)

    _wait_windows(tw_ref[tile], buf, slot, yb_ref, sem, False)
    yrun = _unpack_pairs(buf[slot])
    moe = lax.dot_general(smat.astype(BF16), yrun, (((0,), (0,)), ((), ())),
                          preferred_element_type=F32)
    out = _rms(x1_ref[...] + moe, nf_ref[...])

    @pl.when(s == 0)
    def _():
        ys_ref[...] = out[0:ys_ref.shape[0], :]

    @pl.when(s > 0)
    def _():
        yp_ref[...] = out.reshape(yp_ref.shape)


def _ssm_matrices(a_re, a_im, log_dt, b_re, b_im, c_re, c_im):
    g, n = a_re.shape
    hch = b_re.shape[2]
    gpt = LANE // hch
    nj = g // gpt
    a = lax.complex(a_re, a_im)
    dta = a * jnp.exp(log_dt)[:, None]
    a_bar = jnp.exp(dta)
    bb = ((a_bar - 1.0) / a)[:, :, None] * lax.complex(b_re, b_im)
    cc = lax.complex(c_re, c_im)
    ks = jnp.arange(CHUNK + 1, dtype=F32)
    pw = jnp.exp(dta[None] * ks[:, None, None])
    kk = jnp.real(jnp.einsum('gon,kgn,gni->kgio', cc, pw[:CHUNK], bb))
    kk = jnp.concatenate([jnp.zeros_like(kk[:1]), kk], axis=0)
    kc = kk.reshape(CHUNK + 1, nj, gpt * hch, hch).transpose(1, 0, 2, 3)
    kc = kc.reshape(nj, (CHUNK + 1) * LANE, hch).astype(BF16)
    pc = pw[CHUNK - 1::-1][:CHUNK, :, :, None] * bb[None]
    pc = jnp.stack([jnp.real(pc), jnp.imag(pc)], axis=0)
    pc = pc.reshape(2, CHUNK, nj, gpt, n, hch).transpose(2, 1, 3, 5, 0, 4)
    pc = pc.reshape(nj, CHUNK * LANE, 2 * n).astype(BF16)
    wc = cc[None] * pw[1:CHUNK + 1][:, :, None, :]
    wc = jnp.stack([jnp.real(wc), -jnp.imag(wc)], axis=0)
    rc = wc.reshape(2, CHUNK, nj, gpt, hch, n).transpose(2, 0, 5, 1, 3, 4)
    rc = rc.reshape(nj, 2 * n, CHUNK * LANE).astype(BF16)
    a16 = pw[CHUNK].reshape(nj, 1, gpt * n)
    a16 = jnp.concatenate([jnp.real(a16), jnp.imag(a16)], axis=1)
    bc = jnp.stack([jnp.real(bb), jnp.imag(bb)], axis=0).transpose(1, 3, 0, 2)
    bc = bc.reshape(g * hch, 2 * n).astype(BF16)
    c2 = jnp.stack([jnp.real(cc), -jnp.imag(cc)], axis=0).transpose(0, 3, 1, 2)
    c2 = c2.reshape(2 * n, g * hch).astype(BF16)
    abr = jnp.real(a_bar).reshape(1, g * n)
    abi = jnp.imag(a_bar).reshape(1, g * n)
    return kc, pc, rc, a16, bc, c2, abr, abi


def _full(shape):
    return pl.BlockSpec(shape, lambda *_: (0,) * len(shape))


def kernel(x_prompt, x_sample, state_ssm_re, state_ssm_im, state_conv, meta_tokens, norm_mix, w_in,
           ssm_a_re, ssm_a_im, ssm_log_dt, ssm_b_re, ssm_b_im, ssm_c_re, ssm_c_im, ssm_d, w_glu, b_glu,
           conv_w, norm_out_ssm, norm_out_conv, w_out, norm_ffn, w_router, b_router, w_gate_up,
           b_gate_up, w_down, b_down, norm_final):
    nb, seq, d = x_prompt.shape
    ns = x_sample.shape[0]
    depth, _, g, n = state_ssm_re.shape
    assert depth == 1 and x_sample.shape[1] == 1 and meta_tokens.shape[0] == CHUNK
    cw = conv_w.shape[2]
    nj = cw // LANE
    ne = w_router.shape[2]
    dff = w_down.shape[2]
    nst = g * n
    nbt = nb // 2
    tt = 256
    n_chunks = seq // CHUNK
    tp = nb * seq
    tall = tp + ns
    tm = 128
    rows_p = nbt * tm
    assert rows_p % TOK_TILE == 0 and TOK_TILE % tm == 0 and ns <= TOK_TILE and d % 2 == 0
    n_tiles = tp // TOK_TILE + 1
    ta = n_tiles * TOK_TILE

    kc, pc, rc, a16, bc, c2, abr, abi = _ssm_matrices(
        ssm_a_re[0], ssm_a_im[0], ssm_log_dt[0], ssm_b_re[0], ssm_b_im[0], ssm_c_re[0], ssm_c_im[0])
    win_bf = w_in[0].astype(BF16)
    nmix = norm_mix[0].reshape(1, d)
    nconv = norm_out_conv[0].reshape(1, cw)
    cwt = conv_w[0]

    xsm = jnp.concatenate([x_sample.reshape(ns, d), meta_tokens], axis=0)
    nsm = ns + CHUNK
    s0r = state_ssm_re[0].reshape(ns, nst)
    s0i = state_ssm_im[0].reshape(ns, nst)
    buf0 = state_conv[0, :, 0, :]
    buf1 = state_conv[0, :, 1, :]
    u_sm, z_sm, y_s, ycn_s, sr_s, si_s = pl.pallas_call(
        _small_front_kernel,
        out_shape=(jax.ShapeDtypeStruct((nsm, cw), F32), jax.ShapeDtypeStruct((nsm, cw), F32),
                   jax.ShapeDtypeStruct((ns, cw), F32), jax.ShapeDtypeStruct((ns, cw), BF16),
                   jax.ShapeDtypeStruct((ns, nst), F32), jax.ShapeDtypeStruct((ns, nst), F32)),
        scratch_shapes=[pltpu.VMEM((cw, 2 * nst), BF16), pltpu.VMEM((2 * nst, cw), BF16)],
        compiler_params=_params(None, 56),
        name="small_front",
    )(xsm, nmix, win_bf, s0r, s0i, buf0, buf1, cwt, bc, c2, abr, abi, nconv)
    u_meta = u_sm[ns:]
    z_meta8 = z_sm[ns + CHUNK - 8:]
    new_conv_s = jnp.stack([buf1, z_sm[:ns]], axis=1)[None]
    new_re_s = sr_s.reshape(1, ns, g, n)
    new_im_s = si_s.reshape(1, ns, g, n)

    wch = CHUNK * LANE
    u4c, u_tok, ycn_p, ztail = pl.pallas_call(
        _front_kernel,
        grid=(nb // nbt, seq // tt),
        in_specs=[pl.BlockSpec((nbt, tt, d), lambda b, i: (b, i, 0)),
                  _full((1, d)), _full((d, 4 * cw)), _full((8, cw)), _full((3, cw)), _full((1, cw))],
        out_specs=(pl.BlockSpec((nj, nbt, tt // CHUNK, wch), lambda b, i: (0, b, i, 0)),
                   pl.BlockSpec((nbt, tt, cw), lambda b, i: (b, i, 0)),
                   pl.BlockSpec((nbt, tt, cw), lambda b, i: (b, i, 0)),
                   pl.BlockSpec((nbt, 8, cw), lambda b, i: (b, 0, 0))),
        out_shape=(jax.ShapeDtypeStruct((nj, nb, n_chunks, wch), BF16),
                   jax.ShapeDtypeStruct((nb, seq, cw), BF16),
                   jax.ShapeDtypeStruct((nb, seq, cw), BF16),
                   jax.ShapeDtypeStruct((nb, 8, cw), F32)),
        scratch_shapes=[pltpu.VMEM((nbt, tt + 8, cw), F32), pltpu.VMEM((nj, nbt, tt, LANE), F32)],
        compiler_params=_params(("arbitrary", "arbitrary"), 52),
        name="front",
    )(x_prompt, nmix, win_bf, z_meta8, cwt, nconv)
    new_conv_p = ztail[:, 6:8, :][None]

    cc = n_chunks // 2
    um = u_meta.reshape(CHUNK, nj, LANE).transpose(1, 0, 2).reshape(nj, 1, wch)
    um = jnp.broadcast_to(um, (nj, 8 * nb, wch)).astype(BF16)
    gpt = g // nj
    nstj = 2 * gpt * n
    hch = cw // g
    y4c, s_last = pl.pallas_call(
        _ssm_kernel,
        grid=(nj, n_chunks // cc),
        in_specs=[pl.BlockSpec((1, nb, cc, wch), lambda j, t: (j, 0, t, 0)),
                  pl.BlockSpec((1, 8 * nb, wch), lambda j, t: (j, 0, 0)),
                  pl.BlockSpec((1, (CHUNK + 1) * LANE, hch), lambda j, t: (j, 0, 0)),
                  pl.BlockSpec((1, wch, 2 * n), lambda j, t: (j, 0, 0)),
                  pl.BlockSpec((1, 2 * n, wch), lambda j, t: (j, 0, 0)),
                  pl.BlockSpec((1, 2, nstj // 2), lambda j, t: (j, 0, 0))],
        out_specs=(pl.BlockSpec((1, nb, cc, wch), lambda j, t: (j, 0, t, 0)),
                   pl.BlockSpec((1, nb, 1, nstj), lambda j, t: (j, 0, 0, 0))),
        out_shape=(jax.ShapeDtypeStruct((nj, nb, n_chunks, wch), F32),
                   jax.ShapeDtypeStruct((nj, nb, 1, nstj), F32)),
        scratch_shapes=[pltpu.VMEM((nb, 1, nstj), F32), pltpu.VMEM((nb, cc, nstj), F32),
                        pltpu.VMEM((nb, cc, nstj), F32),
                        pltpu.VMEM((wch, nstj), BF16), pltpu.VMEM((nstj, wch), BF16),
                        pltpu.VMEM((CHUNK // 2, 2 * LANE, 2 * LANE), BF16)],
        compiler_params=_params(("parallel", "arbitrary"), 56),
        name="ssm",
    )(u4c, um, kc, pc, rc, a16)
    sl = s_last.reshape(nj, nb, 2, gpt, n)
    new_re_p = sl[:, :, 0].transpose(1, 0, 2, 3).reshape(1, nb, g, n)
    new_im_p = sl[:, :, 1].transpose(1, 0, 2, 3).reshape(1, nb, g, n)

    dsk = ssm_d[0].reshape(1, cw)
    wglu_bf = w_glu[0].astype(BF16)
    bglu = b_glu[0].reshape(1, cw)
    nssm = norm_out_ssm[0].reshape(1, cw)
    wout_bf = w_out[0].astype(BF16)
    nffn = norm_ffn[0].reshape(1, d)
    wr_pad = jnp.zeros((d, LANE), F32).at[:, :ne].set(w_router[0])
    wr_hi = wr_pad.astype(BF16)
    wr_lo = (wr_pad - wr_hi.astype(F32)).astype(BF16)
    br = b_router[0].reshape(ne, 1)
    mix_w = (dsk, wglu_bf, bglu, nssm, wout_bf, nffn, wr_hi, wr_lo, br)
    mix_w_specs = [_full((1, cw)), _full((cw, cw)), _full((1, cw)), _full((1, cw)), _full((2 * cw, d)),
                   _full((1, d)), _full((d, LANE)), _full((d, LANE)), _full((ne, 1))]
    mix_out_shape = (jax.ShapeDtypeStruct((ta, d), F32), jax.ShapeDtypeStruct((ta, d), BF16),
                     jax.ShapeDtypeStruct((TOP_K, ta), I32), jax.ShapeDtypeStruct((TOP_K, ta), F32),
                     jax.ShapeDtypeStruct((n_tiles, ne, 1), I32))
    tpm = rows_p // TOK_TILE
    nbg = nb // nbt
    x1_all, hf_all, idx_all, gate_all, cnt = pl.pallas_call(
        _mix_kernel_prompt,
        grid=(seq // tm, nbg),
        in_specs=[pl.BlockSpec((nbt, tm, d), lambda i, b: (b, i, 0)),
                  pl.BlockSpec((nj, nbt, tm // CHUNK, wch), lambda i, b: (0, b, i, 0)),
                  pl.BlockSpec((nbt, tm, cw), lambda i, b: (b, i, 0)),
                  pl.BlockSpec((nbt, tm, cw), lambda i, b: (b, i, 0))] + mix_w_specs,
        out_specs=(pl.BlockSpec((rows_p, d), lambda i, b: (i * nbg + b, 0)),
                   pl.BlockSpec((rows_p, d), lambda i, b: (i * nbg + b, 0)),
                   pl.BlockSpec((TOP_K, rows_p), lambda i, b: (0, i * nbg + b)),
                   pl.BlockSpec((TOP_K, rows_p), lambda i, b: (0, i * nbg + b)),
                   pl.BlockSpec((tpm, ne, 1), lambda i, b: (i * nbg + b, 0, 0))),
        out_shape=mix_out_shape,
        scratch_shapes=[pltpu.VMEM((nj, nbt, tm, LANE), F32)],
        compiler_params=_params(("parallel", "parallel"), 52),
        name="mix_prompt",
    )(x_prompt, y4c, u_tok, ycn_p, *mix_w)

    last = n_tiles - 1
    any_spec = pl.BlockSpec(memory_space=pl.ANY)
    x1_all, hf_all, idx_all, gate_all, cnt = pl.pallas_call(
        _mix_kernel_sample,
        grid=(1,),
        in_specs=[_full((1, ns, d)), _full((ns, cw)), _full((1, ns, cw)),
                  _full((1, ns, cw))] + mix_w_specs + [any_spec] * 5,
        out_specs=(pl.BlockSpec((TOK_TILE, d), lambda i: (last, 0)),
                   pl.BlockSpec((TOK_TILE, d), lambda i: (last, 0)),
                   pl.BlockSpec((TOP_K, TOK_TILE), lambda i: (0, last)),
                   pl.BlockSpec((TOP_K, TOK_TILE), lambda i: (0, last)),
                   pl.BlockSpec((1, ne, 1), lambda i: (last, 0, 0))),
        out_shape=mix_out_shape,
        input_output_aliases={13: 0, 14: 1, 15: 2, 16: 3, 17: 4},
        compiler_params=_params(("arbitrary",), 32),
        name="mix_sample",
    )(x_sample.reshape(1, ns, d), y_s, u_sm[:ns].astype(BF16).reshape(1, ns, cw),
      ycn_s.reshape(1, ns, cw), *mix_w, x1_all, hf_all, idx_all, gate_all, cnt)

    bm = MOE_ROWS
    cnt2 = cnt.reshape(n_tiles, ne)
    before = jnp.cumsum(cnt2, axis=0) - cnt2
    count = jnp.sum(cnt2, axis=0)
    padded = ((count + WIN + bm - 1) // bm) * bm
    pend = jnp.cumsum(padded)
    pstart = pend - padded
    phase = before % SUBLANE
    span = jnp.where(cnt2 > 0, phase + cnt2, 0)
    gstart = (pstart[None, :] + before - phase).astype(I32).reshape(-1)
    nwin = ((span + WIN - 1) // WIN).astype(I32)
    reg8 = ((span + SUBLANE - 1) // SUBLANE) * SUBLANE
    loff = (jnp.cumsum(reg8, axis=1) - reg8).astype(I32)
    tail = jnp.where(span % SUBLANE != 0, loff + (span // SUBLANE) * SUBLANE, -1).astype(I32)
    twin = jnp.sum(nwin, axis=1).astype(I32)
    n_blocks = (ta * TOP_K + ne * (WIN + bm - 1) + bm - 1) // bm
    cap = n_blocks * bm
    blk0 = jnp.arange(n_blocks, dtype=I32) * bm
    blk_e = jnp.minimum(jnp.sum((pend[None, :] <= blk0[:, None]).astype(I32), axis=1), ne - 1)
    e_ar = jnp.arange(ne, dtype=I32)
    blk_hot = blk_e[:, None] == e_ar[None, :]

    def _of_block(per_expert):
        return jnp.sum(jnp.where(blk_hot, per_expert[None, :], 0), axis=1)

    blk_valid = jnp.clip(_of_block(count) - (blk0 - _of_block(pstart)), 0, bm).astype(I32)
    has = count > 0
    later = jnp.logical_and(e_ar[None, :] > e_ar[:, None], has[None, :])
    nxt_e = jnp.min(jnp.where(later, e_ar[None, :], ne), axis=1)
    nxt_e = jnp.where(nxt_e < ne, nxt_e, -1)
    ordinal = jnp.cumsum(has.astype(I32)) - 1
    is_first = jnp.logical_and(blk_valid > 0, blk0 == _of_block(pstart))
    blk_first = jnp.where(is_first, jnp.where(_of_block(ordinal) == 0, 2, 1), 0).astype(I32)
    blk_next = _of_block(nxt_e).astype(I32)
    blk_slot = (_of_block(ordinal) % 2).astype(I32)
    nvalid = jnp.full((1,), tall, I32)
    loff_al = (WIN * (jnp.cumsum(nwin, axis=1) - nwin)).astype(I32)
    tables_d = (gstart, nwin.reshape(-1), loff.reshape(-1), tail.reshape(-1), twin, nvalid)
    tables_c = (gstart, nwin.reshape(-1), loff_al.reshape(-1), twin, nvalid)
    rowoff_d = (loff + phase).astype(I32).reshape(n_tiles, ne, 1)
    rowoff_c = (loff_al + phase).astype(I32).reshape(n_tiles, ne, 1)
    xw = d // 2 + GATE_COLS

    xs = pl.pallas_call(
        _dispatch_kernel,
        grid_spec=pltpu.PrefetchScalarGridSpec(
            num_scalar_prefetch=6,
            grid=(n_tiles,),
            in_specs=[pl.BlockSpec((TOK_TILE, d), lambda i, *_: (i, 0)),
                      pl.BlockSpec((TOP_K, TOK_TILE), lambda i, *_: (0, i)),
                      pl.BlockSpec((TOP_K, TOK_TILE), lambda i, *_: (0, i)),
                      pl.BlockSpec((1, ne, 1), lambda i, *_: (i, 0, 0))],
            out_specs=pl.BlockSpec(memory_space=pl.ANY),
            scratch_shapes=[pltpu.VMEM((2, _dispatch_run_rows(ne), xw), U32),
                            pltpu.VMEM((ne, SUBLANE, xw), U32), pltpu.SemaphoreType.DMA((2,))]),
        out_shape=jax.ShapeDtypeStruct((cap, xw), U32),
        compiler_params=_params(("arbitrary",), 40),
        name="dispatch",
    )(*tables_d, hf_all, idx_all, gate_all, rowoff_d)

    yb = pl.pallas_call(
        _moe_kernel,
        grid_spec=pltpu.PrefetchScalarGridSpec(
            num_scalar_prefetch=5,
            grid=(n_blocks,),
            in_specs=[pl.BlockSpec((bm, xw), lambda i, *_: (i, 0)),
                      pl.BlockSpec((1, 1, 2 * dff), lambda i, be, *_: (be[i], 0, 0)),
                      pl.BlockSpec((1, 1, d), lambda i, be, *_: (be[i], 0, 0)),
                      pl.BlockSpec(memory_space=pl.ANY), pl.BlockSpec(memory_space=pl.ANY)],
            out_specs=pl.BlockSpec((bm, d // 2), lambda i, *_: (i, 0)),
            scratch_shapes=[pltpu.VMEM((2, d, 2 * dff), F32), pltpu.VMEM((2, dff, d), F32),
                            pltpu.VMEM((d, 2 * dff), BF16), pltpu.VMEM((dff, d), BF16),
                            pltpu.SemaphoreType.DMA((2, 2))]),
        out_shape=jax.ShapeDtypeStruct((cap, d // 2), U32),
        compiler_params=_params(("arbitrary",), 58),
        name="moe",
    )(blk_e, blk_valid, blk_first, blk_next, blk_slot, xs, b_gate_up[0].reshape(ne, 1, 2 * dff),
      b_down[0].reshape(ne, 1, d), w_gate_up[0], w_down[0])

    nfin = norm_final.reshape(1, d)
    nbh = TOK_TILE // tm

    def _tile_of(s):
        return (s + n_tiles - 1) % n_tiles

    tiles_per_time = nbg * tpm

    def _yp_index(s, *_):
        t = jnp.maximum(s - 1, 0)
        return (t % tiles_per_time, t // tiles_per_time, 0)

    y_p, y_sm = pl.pallas_call(
        _combine_kernel,
        grid_spec=pltpu.PrefetchScalarGridSpec(
            num_scalar_prefetch=5,
            grid=(n_tiles,),
            in_specs=[pl.BlockSpec((TOK_TILE, d), lambda s, *_: (_tile_of(s), 0)),
                      pl.BlockSpec((TOP_K, TOK_TILE), lambda s, *_: (0, _tile_of(s))),
                      pl.BlockSpec((1, ne, 1), lambda s, *_: (_tile_of(s), 0, 0)),
                      pl.BlockSpec((1, d), lambda s, *_: (0, 0)),
                      pl.BlockSpec(memory_space=pl.ANY)],
            out_specs=(pl.BlockSpec((nbh, tm, d), _yp_index),
                       pl.BlockSpec((ns, d), lambda s, *_: (0, 0))),
            scratch_shapes=[pltpu.VMEM((2, _combine_run_rows(ne), d // 2), U32),
                            pltpu.SemaphoreType.DMA((2,))]),
        out_shape=(jax.ShapeDtypeStruct((nb, seq, d), F32), jax.ShapeDtypeStruct((ns, d), F32)),
        compiler_params=_params(("arbitrary",), 48),
        name="combine",
    )(*tables_c, x1_all, idx_all, rowoff_c, nfin, yb)

    return (y_p, y_sm.reshape(ns, 1, d), new_re_p, new_im_p, new_conv_p,
            new_re_s, new_im_s, new_conv_s)
```

```python
import math

import jax
import jax.numpy as jnp
from jax import lax
from jax.experimental import pallas as pl
from jax.experimental.pallas import tpu as pltpu

F32 = jnp.float32
BF16 = jnp.bfloat16
U32 = jnp.uint32
I32 = jnp.int32
EPS = 1e-5
CHUNK = 16
LANE = 128
TOP_K = 4
SWIGLU_LIMIT = 7.0
SWIGLU_ALPHA = 1.702
MOE_ROWS = 512
MOE_SUB = 256
TOK_TILE = 256
WIN = 32
SUBLANE = 8
GATE_COLS = LANE


MAX_WINDOWS = 72


def _dispatch_run_rows(ne):
    return TOP_K * TOK_TILE + ne * 2 * (SUBLANE - 1) + WIN


def _combine_run_rows(ne):
    return -(-(TOP_K * TOK_TILE + ne * (SUBLANE - 1 + WIN - 1)) // WIN) * WIN
HI_MASK = 0xFFFF0000
MIB = 1024 * 1024


def _rms(x, g):
    return x * lax.rsqrt(jnp.mean(x * x, axis=-1, keepdims=True) + EPS) * g


def _gelu_tanh(x):
    c = math.sqrt(2.0 / math.pi)
    return 0.5 * x * (1.0 + jnp.tanh(c * (x + 0.044715 * (x * x * x))))


def _params(sem, vmem_mib):
    return pltpu.CompilerParams(dimension_semantics=sem, vmem_limit_bytes=vmem_mib * MIB)


def _pack_pairs(a, b):
    return (pltpu.bitcast(a, U32) >> 16) | (pltpu.bitcast(b, U32) & jnp.uint32(HI_MASK))


def _unpack_pairs(w):
    lo = pltpu.bitcast(w << 16, F32)
    hi = pltpu.bitcast(w & jnp.uint32(HI_MASK), F32)
    return jnp.concatenate([lo, hi], axis=-1).astype(BF16)


def _iota2(shape, axis):
    return lax.broadcasted_iota(I32, shape, axis)


def _expand_cols(compact, reps_log2, n_log2):
    q = _iota2((compact.shape[1], 2 << (reps_log2 + n_log2)), 0)
    c = _iota2((compact.shape[1], 2 << (reps_log2 + n_log2)), 1)
    nmask = (1 << n_log2) - 1
    same = jnp.logical_and((q >> n_log2) == (c >> (reps_log2 + n_log2)), (q & nmask) == (c & nmask))
    return jnp.dot(compact, jnp.where(same, 1.0, 0.0).astype(BF16), preferred_element_type=F32)


def _expand_rows(compact, reps_log2, n_log2):
    r = _iota2((2 << (reps_log2 + n_log2), compact.shape[0]), 0)
    q = _iota2((2 << (reps_log2 + n_log2), compact.shape[0]), 1)
    nmask = (1 << n_log2) - 1
    same = jnp.logical_and((r >> (reps_log2 + n_log2)) == (q >> n_log2), (r & nmask) == (q & nmask))
    return jnp.dot(jnp.where(same, 1.0, 0.0).astype(BF16), compact, preferred_element_type=F32)


def _group_mask(shape, row_shift, col_shift, ngroups):
    r = _iota2(shape, 0)
    c = _iota2(shape, 1)
    return ((r >> row_shift) & (ngroups - 1)) == ((c >> col_shift) & (ngroups - 1))


def _small_front_kernel(x_ref, nmix_ref, win_ref, s0r_ref, s0i_ref, b0_ref, b1_ref, cw_ref,
                        bc_ref, cc_ref, abr_ref, abi_ref, nconv_ref,
                        u_ref, z_ref, y_ref, ycn_ref, sr_ref, si_ref, bdb_ref, cm_ref):
    ns, nst = s0r_ref.shape
    cw = u_ref.shape[1]
    n = bc_ref.shape[1] // 2
    nlog = n.bit_length() - 1
    glog = (nst // n).bit_length() - 1
    hlog = (cw >> glog).bit_length() - 1
    bdb_ref[...] = jnp.where(_group_mask(bdb_ref.shape, hlog, nlog, 1 << glog),
                             _expand_cols(bc_ref[...], glog, nlog), 0.0).astype(BF16)
    cm_ref[...] = jnp.where(_group_mask(cm_ref.shape, nlog, hlog, 1 << glog),
                            _expand_rows(cc_ref[...], glog, nlog), 0.0).astype(BF16)
    h = _rms(x_ref[...], nmix_ref[...]).astype(BF16)
    proj = jnp.dot(h, win_ref[...], preferred_element_type=F32)
    u = proj[:, 0:cw]
    zc = proj[:, cw:2 * cw]
    gb = proj[:, 2 * cw:3 * cw]
    gc = proj[:, 3 * cw:4 * cw]
    z = gc * zc
    u_ref[...] = u
    z_ref[...] = z
    bu = jnp.dot(u[:ns].astype(BF16), bdb_ref[...], preferred_element_type=F32)
    abr = abr_ref[...]
    abi = abi_ref[...]
    s0r = s0r_ref[...]
    s0i = s0i_ref[...]
    sr = abr * s0r - abi * s0i + bu[:, :nst]
    si = abr * s0i + abi * s0r + bu[:, nst:]
    sr_ref[...] = sr
    si_ref[...] = si
    scat = jnp.concatenate([sr, si], axis=-1).astype(BF16)
    y_ref[...] = jnp.dot(scat, cm_ref[...], preferred_element_type=F32)
    conv = cw_ref[0:1, :] * b0_ref[...] + cw_ref[1:2, :] * b1_ref[...] + cw_ref[2:3, :] * z[:ns]
    ycn_ref[...] = _rms(gb[:ns] * conv, nconv_ref[...]).astype(BF16)


def _front_kernel(x_ref, nmix_ref, win_ref, zm_ref, cw_ref, nconv_ref,
                  uc_ref, ut_ref, ycn_ref, zt_ref, zbuf, ubuf):
    i = pl.program_id(1)
    nb, tt, d = x_ref.shape
    cw = ycn_ref.shape[2]
    rows = nb * tt
    ncz = tt // CHUNK

    @pl.when(i == 0)
    def _():
        zbuf[:, 0:8, :] = jnp.broadcast_to(zm_ref[...][None], (nb, 8, cw))

    h = _rms(x_ref[...].reshape(rows, d), nmix_ref[...]).astype(BF16)
    u = jnp.dot(h, win_ref[:, 0:cw], preferred_element_type=F32)
    ut_ref[...] = u.astype(BF16).reshape(nb, tt, cw)
    for j in range(cw // LANE):
        ubuf[j] = u[:, j * LANE:(j + 1) * LANE].reshape(nb, tt, LANE)
    for s in range(CHUNK):
        for j in range(cw // LANE):
            piece = ubuf[j, :, pl.ds(s, ncz, stride=CHUNK), :]
            uc_ref[j, :, :, s * LANE:(s + 1) * LANE] = piece.astype(BF16)
    zc = jnp.dot(h, win_ref[:, cw:2 * cw], preferred_element_type=F32)
    gc = jnp.dot(h, win_ref[:, 3 * cw:4 * cw], preferred_element_type=F32)
    z3 = (gc * zc).reshape(nb, tt, cw)
    zbuf[:, 8:8 + tt, :] = z3
    z1 = zbuf[:, 7:7 + tt, :]
    z2 = zbuf[:, 6:6 + tt, :]
    conv = cw_ref[0:1, :] * z2 + cw_ref[1:2, :] * z1 + cw_ref[2:3, :] * z3
    gb = jnp.dot(h, win_ref[:, 2 * cw:3 * cw], preferred_element_type=F32)
    yc = gb * conv.reshape(rows, cw)
    ycn_ref[...] = _rms(yc, nconv_ref[...]).astype(BF16).reshape(nb, tt, cw)
    tail = zbuf[:, tt:tt + 8, :]
    zt_ref[...] = tail
    zbuf[:, 0:8, :] = tail


def _ssm_kernel(u_ref, um_ref, kc_ref, pc_ref, rc_ref, a16_ref, y_ref, sl_ref,
                s_carry, ds_ref, sp_ref, p_s, r_s, t_s):
    th = pl.program_id(1)
    _, nb, cc, w = u_ref.shape
    nst = p_s.shape[1]
    half = nst // 2
    rows = nb * cc
    blk = 2 * LANE
    u = u_ref[0].reshape(rows, w)

    @pl.when(th == 0)
    def _():
        hch = kc_ref.shape[2]
        gpt = LANE // hch
        hlog = hch.bit_length() - 1
        glog = gpt.bit_length() - 1
        nlog = (pc_ref.shape[2] // 2).bit_length() - 1
        p_s[...] = jnp.where(_group_mask(p_s.shape, hlog, nlog, gpt),
                             _expand_cols(pc_ref[0], glog, nlog), 0.0).astype(BF16)
        r_s[...] = jnp.where(_group_mask(r_s.shape, nlog, hlog, gpt),
                             _expand_rows(rc_ref[0], glog, nlog), 0.0).astype(BF16)
        nlag = kc_ref.shape[1] // LANE
        o = _iota2((hch, LANE), 0)
        c = _iota2((hch, LANE), 1)
        spread = jnp.where((c & (hch - 1)) == o, 1.0, 0.0).astype(BF16)
        lagm = jnp.dot(kc_ref[0], spread, preferred_element_type=F32)
        r = _iota2(lagm.shape, 0)
        c = _iota2(lagm.shape, 1)
        lagm = jnp.where(((r >> hlog) & (gpt - 1)) == (c >> hlog), lagm, 0.0).astype(BF16)
        for dlt in range(nlag // 2):
            b0 = lagm[(2 * dlt) * LANE:(2 * dlt + 1) * LANE]
            b1 = lagm[(2 * dlt + 1) * LANE:(2 * dlt + 2) * LANE]
            b2 = lagm[(2 * dlt + 2) * LANE:(2 * dlt + 3) * LANE]
            t_s[dlt, 0:LANE, 0:LANE] = b1
            t_s[dlt, 0:LANE, LANE:blk] = b2
            t_s[dlt, LANE:blk, 0:LANE] = b0
            t_s[dlt, LANE:blk, LANE:blk] = b1
        ds_ref[:, 0:8, :] = jnp.dot(um_ref[0], p_s[...], preferred_element_type=F32).reshape(nb, 8, nst)
        s_carry[...] = ds_ref[:, 0:1, :]

    ds_ref[...] = jnp.dot(u, p_s[...], preferred_element_type=F32).reshape(nb, cc, nst)
    ar = a16_ref[0, 0:1, :].reshape(1, 1, half)
    ai = a16_ref[0, 1:2, :].reshape(1, 1, half)
    sr = s_carry[:, :, 0:half]
    si = s_carry[:, :, half:nst]
    for c in range(cc):
        sp_ref[:, c:c + 1, 0:half] = sr
        sp_ref[:, c:c + 1, half:nst] = si
        dr = ds_ref[:, c:c + 1, 0:half]
        di = ds_ref[:, c:c + 1, half:nst]
        sr, si = ar * sr - ai * si + dr, ar * si + ai * sr + di
    s_carry[:, :, 0:half] = sr
    s_carry[:, :, half:nst] = si
    sl_ref[0, :, :, 0:half] = sr
    sl_ref[0, :, :, half:nst] = si

    sp = sp_ref[...].reshape(rows, nst).astype(BF16)
    for tb in range(w // blk):
        acc = jnp.dot(sp, r_s[:, tb * blk:(tb + 1) * blk], preferred_element_type=F32)
        for sb in range(tb + 1):
            acc = acc + jnp.dot(u[:, sb * blk:(sb + 1) * blk], t_s[tb - sb],
                                preferred_element_type=F32)
        y_ref[0, :, :, tb * blk:(tb + 1) * blk] = acc.reshape(nb, cc, blk)


def _mix_rows(x, yssm, ut, ycn, dsk_ref, wglu_ref, bglu_ref, nssm_ref, wout_ref,
              nffn_ref, wrh_ref, wrl_ref, br_ref):
    ne = br_ref.shape[0]
    y = _gelu_tanh(yssm + dsk_ref[...] * ut.astype(F32))
    glu = jnp.dot(y.astype(BF16), wglu_ref[...], preferred_element_type=F32) + bglu_ref[...]
    o = y * jax.nn.sigmoid(glu)
    ysn = _rms(o, nssm_ref[...]).astype(BF16)
    mix = jnp.concatenate([ysn, ycn], axis=-1)
    x1 = x + jnp.dot(mix, wout_ref[...], preferred_element_type=F32)
    hf = _rms(x1, nffn_ref[...])
    hf_hi = hf.astype(BF16)
    hf_lo = (hf - hf_hi.astype(F32)).astype(BF16)
    logits = (jnp.dot(hf_hi, wrh_ref[...], preferred_element_type=F32)
              + jnp.dot(hf_lo, wrh_ref[...], preferred_element_type=F32)
              + jnp.dot(hf_hi, wrl_ref[...], preferred_element_type=F32))
    lt = logits.T[0:ne, :] + br_ref[...]
    iota = lax.broadcasted_iota(I32, lt.shape, 0)
    vals, idxs = [], []
    sel = jnp.zeros(lt.shape, F32)
    for _ in range(TOP_K):
        m = jnp.max(lt, axis=0, keepdims=True)
        ik = jnp.min(jnp.where(lt == m, iota, ne), axis=0, keepdims=True)
        vals.append(m)
        idxs.append(ik)
        hit = iota == ik
        sel = sel + jnp.where(hit, 1.0, 0.0)
        lt = jnp.where(hit, -jnp.inf, lt)
    es = [jnp.exp(v - vals[0]) for v in vals]
    tot = es[0] + es[1] + es[2] + es[3]
    idx = jnp.concatenate(idxs, axis=0)
    gates = jnp.concatenate([e / tot for e in es], axis=0)
    return x1, hf_hi, idx, gates, sel


def _mix_kernel_prompt(*refs):
    x_ref, yc_ref = refs[0], refs[1]
    x1_ref, hf_ref, idx_ref, gate_ref, cnt_ref, ybuf = refs[13:]
    nj, nb, ncz, _ = yc_ref.shape
    for s in range(CHUNK):
        for j in range(nj):
            ybuf[j, :, pl.ds(s, ncz, stride=CHUNK), :] = yc_ref[j, :, :, s * LANE:(s + 1) * LANE]
    ut_ref, ycn_ref = refs[2], refs[3]
    tt, d = x_ref.shape[1], x_ref.shape[2]
    cw = nj * LANE
    nbc = TOK_TILE // tt
    for t in range(cnt_ref.shape[0]):
        b0, r0 = t * nbc, t * TOK_TILE
        yssm = jnp.concatenate([ybuf[j, b0:b0 + nbc].reshape(TOK_TILE, LANE) for j in range(nj)], axis=-1)
        x1, hf, idx, gates, sel = _mix_rows(
            x_ref[b0:b0 + nbc].reshape(TOK_TILE, d), yssm, ut_ref[b0:b0 + nbc].reshape(TOK_TILE, cw),
            ycn_ref[b0:b0 + nbc].reshape(TOK_TILE, cw), *refs[4:13])
        x1_ref[r0:r0 + TOK_TILE, :] = x1
        hf_ref[r0:r0 + TOK_TILE, :] = hf
        idx_ref[:, r0:r0 + TOK_TILE] = idx
        gate_ref[:, r0:r0 + TOK_TILE] = gates
        cnt_ref[t] = jnp.sum(sel, axis=1, keepdims=True).astype(I32)


def _mix_kernel_sample(*refs):
    x1, hf, idx, gates, sel = _mix_rows(refs[0][0], refs[1][...], refs[2][0], refs[3][0], *refs[4:13])
    x1_ref, hf_ref, idx_ref, gate_ref, cnt_ref = refs[18:]
    ns = x1.shape[0]
    x1_ref[...] = jnp.zeros(x1_ref.shape, x1_ref.dtype)
    hf_ref[...] = jnp.zeros(hf_ref.shape, hf_ref.dtype)
    idx_ref[...] = jnp.zeros(idx_ref.shape, idx_ref.dtype)
    gate_ref[...] = jnp.zeros(gate_ref.shape, gate_ref.dtype)
    x1_ref[0:ns, :] = x1
    hf_ref[0:ns, :] = hf
    idx_ref[:, 0:ns] = idx
    gate_ref[:, 0:ns] = gates
    cnt_ref[0] = jnp.sum(sel, axis=1, keepdims=True).astype(I32)


def _local_rows(idx_ref, loff_ref, tile, n_valid):
    ne = loff_ref.shape[1]
    tt = idx_ref.shape[1]
    e_iota = lax.broadcasted_iota(I32, (ne, tt), 0)
    tok = tile * tt + lax.broadcasted_iota(I32, (1, tt), 1)
    valid = tok < n_valid
    hits = [jnp.logical_and(e_iota == idx_ref[k:k + 1, :], valid) for k in range(TOP_K)]
    sel = jnp.zeros((ne, tt), F32)
    for h in hits:
        sel = sel + jnp.where(h, 1.0, 0.0)
    before = lax.broadcasted_iota(I32, (tt, tt), 0) < lax.broadcasted_iota(I32, (tt, tt), 1)
    tri = jnp.where(before, 1.0, 0.0).astype(BF16)
    base = jnp.dot(sel.astype(BF16), tri, preferred_element_type=F32) + loff_ref[0].astype(F32)
    rows = []
    for h in hits:
        r = jnp.sum(jnp.where(h, base, 0.0), axis=0, keepdims=True)
        rows.append(jnp.where(valid, r, -1.0))
    return rows


def _window_copy(buf, slot, hbm, lo, g, sem, to_hbm):
    src = buf.at[slot, pl.ds(pl.multiple_of(lo, SUBLANE), WIN)]
    dst = hbm.at[pl.ds(pl.multiple_of(g, SUBLANE), WIN)]
    if to_hbm:
        return pltpu.make_async_copy(src, dst, sem.at[slot])
    return pltpu.make_async_copy(dst, src, sem.at[slot])


def _start_windows(wg_ref, wl_ref, tw_ref, tile, buf, slot, hbm, sem, to_hbm):
    def per_window(w, c):
        k = tile * MAX_WINDOWS + w
        lo = w * WIN if wl_ref is None else wl_ref[k]
        _window_copy(buf, slot, hbm, lo, wg_ref[k], sem, to_hbm).start()
        return c

    lax.fori_loop(0, tw_ref[tile], per_window, 0)


def _wait_windows(count, buf, slot, hbm, sem, to_hbm):
    def per_window(w, c):
        _window_copy(buf, slot, hbm, 0, 0, sem, to_hbm).wait()
        return c

    lax.fori_loop(0, count, per_window, 0)


def _dispatch_kernel(wg_ref, wl_ref, mlo_ref, mtg_ref, keep_ref, tw_ref, nv_ref,
                     hf_ref, idx_ref, gate_ref, loff_ref, xs_ref, buf, carry, sem):
    i = pl.program_id(0)
    nt = pl.num_programs(0)
    ne = loff_ref.shape[1]
    tt, d = hf_ref.shape
    nrun = buf.shape[1]
    slot = i % 2

    @pl.when(i == 0)
    def _():
        carry[...] = jnp.zeros(carry.shape, carry.dtype)

    rows = _local_rows(idx_ref, loff_ref, i, nv_ref[0])
    r_iota = lax.broadcasted_iota(I32, (nrun, tt), 0).astype(F32)
    smat = jnp.zeros((nrun, tt), F32)
    gmat = jnp.zeros((nrun, tt), F32)
    for k, r in enumerate(rows):
        hit = r_iota == r
        smat = smat + jnp.where(hit, 1.0, 0.0)
        gmat = gmat + jnp.where(hit, gate_ref[k:k + 1, :], 0.0)
    xr = jnp.dot(smat.astype(BF16), hf_ref[...], preferred_element_type=F32)
    gcol = jnp.sum(gmat, axis=1, keepdims=True)
    lane0 = lax.broadcasted_iota(I32, (nrun, GATE_COLS), 1) == 0
    buf[slot, :, 0:d // 2] = _pack_pairs(xr[:, :d // 2], xr[:, d // 2:])
    buf[slot, :, d // 2:] = pltpu.bitcast(jnp.where(lane0, gcol, 0.0), U32)

    def merge(e, c):
        k = i * ne + e
        lo = pl.multiple_of(mlo_ref[k], SUBLANE)
        buf[slot, pl.ds(lo, SUBLANE), :] = buf[slot, pl.ds(lo, SUBLANE), :] | carry[e]
        tg = pl.multiple_of(mtg_ref[k], SUBLANE)
        carry[e] = jnp.where(keep_ref[k] > 0, carry[e], buf[slot, pl.ds(tg, SUBLANE), :])
        return c

    lax.fori_loop(0, ne, merge, 0, unroll=4)

    @pl.when(i > 0)
    def _():
        _wait_windows(tw_ref[i - 1], buf, 1 - slot, xs_ref, sem, True)

    _start_windows(wg_ref, wl_ref, tw_ref, i, buf, slot, xs_ref, sem, True)

    @pl.when(i == nt - 1)
    def _():
        _wait_windows(tw_ref[i], buf, slot, xs_ref, sem, True)


def _expert_weight_copies(wg_hbm, wd_hbm, wg_f32, wd_f32, sem, e, slot):
    return (pltpu.make_async_copy(wg_hbm.at[e], wg_f32.at[slot], sem.at[0, slot]),
            pltpu.make_async_copy(wd_hbm.at[e], wd_f32.at[slot], sem.at[1, slot]))


def _moe_kernel(be_ref, bv_ref, first_ref, nxt_ref, slot_ref,
                x_ref, bg_ref, bd_ref, wg_hbm, wd_hbm, y_ref,
                wg_f32, wd_f32, wg_bf, wd_bf, sem):
    i = pl.program_id(0)
    e = be_ref[i]
    dff = wd_bf.shape[0]
    bm = x_ref.shape[0]
    sub = MOE_SUB
    nw = x_ref.shape[1] - GATE_COLS

    @pl.when(first_ref[i] > 0)
    def _():
        slot = slot_ref[i]

        @pl.when(first_ref[i] > 1)
        def _():
            for cp in _expert_weight_copies(wg_hbm, wd_hbm, wg_f32, wd_f32, sem, e, slot):
                cp.start()

        for cp in _expert_weight_copies(wg_hbm, wd_hbm, wg_f32, wd_f32, sem, e, slot):
            cp.wait()

        @pl.when(nxt_ref[i] >= 0)
        def _():
            for cp in _expert_weight_copies(wg_hbm, wd_hbm, wg_f32, wd_f32, sem, nxt_ref[i], 1 - slot):
                cp.start()

        wg_bf[...] = wg_f32[slot].astype(BF16)
        wd_bf[...] = wd_f32[slot].astype(BF16)

    def rows(r0, nrows):
        for c in range(nrows // sub):
            lo = r0 + c * sub
            live = lax.broadcasted_iota(I32, (sub, 1), 0) + lo < bv_ref[i]
            x = jnp.where(live, _unpack_pairs(x_ref[lo:lo + sub, 0:nw]), jnp.zeros((), BF16))
            route = jnp.where(live, pltpu.bitcast(x_ref[lo:lo + sub, nw:], F32)[:, 0:1], 0.0)
            gu = jnp.dot(x, wg_bf[...], preferred_element_type=F32) + bg_ref[0]
            gate = jnp.minimum(gu[:, :dff], SWIGLU_LIMIT)
            up = jnp.clip(gu[:, dff:], -SWIGLU_LIMIT, SWIGLU_LIMIT)
            h = gate * jax.nn.sigmoid(SWIGLU_ALPHA * gate) * (up + 1.0)
            y = jnp.dot(h.astype(BF16), wd_bf[...], preferred_element_type=F32) + bd_ref[0]
            yr = (route * y).astype(BF16).astype(F32)
            half = yr.shape[1] // 2
            y_ref[lo:lo + sub, :] = _pack_pairs(yr[:, :half], yr[:, half:])

    half_rows = bm // 2

    @pl.when(bv_ref[i] > half_rows)
    def _():
        rows(0, bm)

    @pl.when(jnp.logical_and(bv_ref[i] > 0, bv_ref[i] <= half_rows))
    def _():
        rows(0, half_rows)
        y_ref[half_rows:bm, :] = jnp.zeros((bm - half_rows, y_ref.shape[1]), y_ref.dtype)

    @pl.when(bv_ref[i] <= 0)
    def _():
        y_ref[...] = jnp.zeros(y_ref.shape, y_ref.dtype)


def _combine_tile(step, nt):
    return (step + nt - 1) % nt


def _combine_kernel(wg_ref, tw_ref, nv_ref,
                    x1_ref, idx_ref, loff_ref, nf_ref, yb_ref,
                    yp_ref, ys_ref, buf, sem):
    s = pl.program_id(0)
    nt = pl.num_programs(0)
    tt, d = x1_ref.shape
    nrun = buf.shape[1]
    tile = _combine_tile(s, nt)
    slot = s % 2

    @pl.when(s == 0)
    def _():
        buf[...] = jnp.zeros(buf.shape, buf.dtype)
        _start_windows(wg_ref, None, tw_ref, tile, buf, slot, yb_ref, sem, False)

    @pl.when(s + 1 < nt)
    def _():
        _start_windows(wg_ref, None, tw_ref, _combine_tile(s + 1, nt), buf, 1 - slot, yb_ref, sem, False)

    rows = _local_rows(idx_ref, loff_ref, tile, nv_ref[0])
    r_iota = lax.broadcasted_iota(I32, (nrun, tt), 0).astype(F32)
    smat = jnp.zeros((nrun, tt), F32)
    for r in rows:
        smat = smat + jnp.where(r_iota == r, 1.0, 0.0)

    _wait_windows(tw_ref[tile], buf, slot, yb_ref, sem, False)
    yrun = _unpack_pairs(buf[slot])
    moe = lax.dot_general(smat.astype(BF16), yrun, (((0,), (0,)), ((), ())),
                          preferred_element_type=F32)
    out = _rms(x1_ref[...] + moe, nf_ref[...])

    @pl.when(s == 0)
    def _():
        ys_ref[...] = out[0:ys_ref.shape[0], :]

    @pl.when(s > 0)
    def _():
        yp_ref[...] = out.reshape(yp_ref.shape)


def _ssm_matrices(a_re, a_im, log_dt, b_re, b_im, c_re, c_im):
    g, n = a_re.shape
    hch = b_re.shape[2]
    gpt = LANE // hch
    nj = g // gpt
    a = lax.complex(a_re, a_im)
    dta = a * jnp.exp(log_dt)[:, None]
    a_bar = jnp.exp(dta)
    bb = ((a_bar - 1.0) / a)[:, :, None] * lax.complex(b_re, b_im)
    cc = lax.complex(c_re, c_im)
    ks = jnp.arange(CHUNK + 1, dtype=F32)
    pw = jnp.exp(dta[None] * ks[:, None, None])
    kk = jnp.real(jnp.einsum('gon,kgn,gni->kgio', cc, pw[:CHUNK], bb))
    kk = jnp.concatenate([jnp.zeros_like(kk[:1]), kk], axis=0)
    kc = kk.reshape(CHUNK + 1, nj, gpt * hch, hch).transpose(1, 0, 2, 3)
    kc = kc.reshape(nj, (CHUNK + 1) * LANE, hch).astype(BF16)
    pwr, pwi = jnp.real(pw), jnp.imag(pw)
    bbr = jnp.real(bb).transpose(0, 2, 1)[None]
    bbi = jnp.imag(bb).transpose(0, 2, 1)[None]
    par = pwr[CHUNK - 1::-1][:CHUNK, :, None, :]
    pai = pwi[CHUNK - 1::-1][:CHUNK, :, None, :]
    pc = jnp.stack([par * bbr - pai * bbi, par * bbi + pai * bbr], axis=3)
    pc = pc.reshape(CHUNK, nj, gpt * hch, 2 * n).transpose(1, 0, 2, 3)
    pc = pc.reshape(nj, CHUNK * LANE, 2 * n).astype(BF16)
    ccr = jnp.real(cc).transpose(2, 0, 1)[:, None]
    cci = jnp.imag(cc).transpose(2, 0, 1)[:, None]
    qar = pwr[1:CHUNK + 1].transpose(2, 0, 1)[..., None]
    qai = pwi[1:CHUNK + 1].transpose(2, 0, 1)[..., None]
    rc = jnp.stack([ccr * qar - cci * qai, -(ccr * qai + cci * qar)], axis=0)
    rc = rc.reshape(2 * n, CHUNK, nj, gpt * hch).transpose(2, 0, 1, 3)
    rc = rc.reshape(nj, 2 * n, CHUNK * LANE).astype(BF16)
    a16 = pw[CHUNK].reshape(nj, 1, gpt * n)
    a16 = jnp.concatenate([jnp.real(a16), jnp.imag(a16)], axis=1)
    bc = jnp.stack([jnp.real(bb), jnp.imag(bb)], axis=0).transpose(1, 3, 0, 2)
    bc = bc.reshape(g * hch, 2 * n).astype(BF16)
    c2 = jnp.stack([jnp.real(cc), -jnp.imag(cc)], axis=0).transpose(0, 3, 1, 2)
    c2 = c2.reshape(2 * n, g * hch).astype(BF16)
    abr = jnp.real(a_bar).reshape(1, g * n)
    abi = jnp.imag(a_bar).reshape(1, g * n)
    return kc, pc, rc, a16, bc, c2, abr, abi


def _full(shape):
    return pl.BlockSpec(shape, lambda *_: (0,) * len(shape))


def kernel(x_prompt, x_sample, state_ssm_re, state_ssm_im, state_conv, meta_tokens, norm_mix, w_in,
           ssm_a_re, ssm_a_im, ssm_log_dt, ssm_b_re, ssm_b_im, ssm_c_re, ssm_c_im, ssm_d, w_glu, b_glu,
           conv_w, norm_out_ssm, norm_out_conv, w_out, norm_ffn, w_router, b_router, w_gate_up,
           b_gate_up, w_down, b_down, norm_final):
    nb, seq, d = x_prompt.shape
    ns = x_sample.shape[0]
    depth, _, g, n = state_ssm_re.shape
    assert depth == 1 and x_sample.shape[1] == 1 and meta_tokens.shape[0] == CHUNK
    cw = conv_w.shape[2]
    nj = cw // LANE
    ne = w_router.shape[2]
    dff = w_down.shape[2]
    nst = g * n
    nbt = nb // 2
    tt = 256
    n_chunks = seq // CHUNK
    tp = nb * seq
    tall = tp + ns
    tm = 128
    rows_p = nbt * tm
    assert rows_p % TOK_TILE == 0 and TOK_TILE % tm == 0 and ns <= TOK_TILE and d % 2 == 0
    n_tiles = tp // TOK_TILE + 1
    ta = n_tiles * TOK_TILE

    kc, pc, rc, a16, bc, c2, abr, abi = _ssm_matrices(
        ssm_a_re[0], ssm_a_im[0], ssm_log_dt[0], ssm_b_re[0], ssm_b_im[0], ssm_c_re[0], ssm_c_im[0])
    win_bf = w_in[0].astype(BF16)
    nmix = norm_mix[0].reshape(1, d)
    nconv = norm_out_conv[0].reshape(1, cw)
    cwt = conv_w[0]

    xsm = jnp.concatenate([x_sample.reshape(ns, d), meta_tokens], axis=0)
    nsm = ns + CHUNK
    s0r = state_ssm_re[0].reshape(ns, nst)
    s0i = state_ssm_im[0].reshape(ns, nst)
    buf0 = state_conv[0, :, 0, :]
    buf1 = state_conv[0, :, 1, :]
    u_sm, z_sm, y_s, ycn_s, sr_s, si_s = pl.pallas_call(
        _small_front_kernel,
        out_shape=(jax.ShapeDtypeStruct((nsm, cw), F32), jax.ShapeDtypeStruct((nsm, cw), F32),
                   jax.ShapeDtypeStruct((ns, cw), F32), jax.ShapeDtypeStruct((ns, cw), BF16),
                   jax.ShapeDtypeStruct((ns, nst), F32), jax.ShapeDtypeStruct((ns, nst), F32)),
        scratch_shapes=[pltpu.VMEM((cw, 2 * nst), BF16), pltpu.VMEM((2 * nst, cw), BF16)],
        compiler_params=_params(None, 56),
        name="small_front",
    )(xsm, nmix, win_bf, s0r, s0i, buf0, buf1, cwt, bc, c2, abr, abi, nconv)
    u_meta = u_sm[ns:]
    z_meta8 = z_sm[ns + CHUNK - 8:]
    new_conv_s = jnp.stack([buf1, z_sm[:ns]], axis=1)[None]
    new_re_s = sr_s.reshape(1, ns, g, n)
    new_im_s = si_s.reshape(1, ns, g, n)

    wch = CHUNK * LANE
    u4c, u_tok, ycn_p, ztail = pl.pallas_call(
        _front_kernel,
        grid=(nb // nbt, seq // tt),
        in_specs=[pl.BlockSpec((nbt, tt, d), lambda b, i: (b, i, 0)),
                  _full((1, d)), _full((d, 4 * cw)), _full((8, cw)), _full((3, cw)), _full((1, cw))],
        out_specs=(pl.BlockSpec((nj, nbt, tt // CHUNK, wch), lambda b, i: (0, b, i, 0)),
                   pl.BlockSpec((nbt, tt, cw), lambda b, i: (b, i, 0)),
                   pl.BlockSpec((nbt, tt, cw), lambda b, i: (b, i, 0)),
                   pl.BlockSpec((nbt, 8, cw), lambda b, i: (b, 0, 0))),
        out_shape=(jax.ShapeDtypeStruct((nj, nb, n_chunks, wch), BF16),
                   jax.ShapeDtypeStruct((nb, seq, cw), BF16),
                   jax.ShapeDtypeStruct((nb, seq, cw), BF16),
                   jax.ShapeDtypeStruct((nb, 8, cw), F32)),
        scratch_shapes=[pltpu.VMEM((nbt, tt + 8, cw), F32), pltpu.VMEM((nj, nbt, tt, LANE), F32)],
        compiler_params=_params(("arbitrary", "arbitrary"), 52),
        name="front",
    )(x_prompt, nmix, win_bf, z_meta8, cwt, nconv)
    new_conv_p = ztail[:, 6:8, :][None]

    cc = n_chunks // 2
    um = u_meta.reshape(CHUNK, nj, LANE).transpose(1, 0, 2).reshape(nj, 1, wch)
    um = jnp.broadcast_to(um, (nj, 8 * nb, wch)).astype(BF16)
    gpt = g // nj
    nstj = 2 * gpt * n
    hch = cw // g
    y4c, s_last = pl.pallas_call(
        _ssm_kernel,
        grid=(nj, n_chunks // cc),
        in_specs=[pl.BlockSpec((1, nb, cc, wch), lambda j, t: (j, 0, t, 0)),
                  pl.BlockSpec((1, 8 * nb, wch), lambda j, t: (j, 0, 0)),
                  pl.BlockSpec((1, (CHUNK + 1) * LANE, hch), lambda j, t: (j, 0, 0)),
                  pl.BlockSpec((1, wch, 2 * n), lambda j, t: (j, 0, 0)),
                  pl.BlockSpec((1, 2 * n, wch), lambda j, t: (j, 0, 0)),
                  pl.BlockSpec((1, 2, nstj // 2), lambda j, t: (j, 0, 0))],
        out_specs=(pl.BlockSpec((1, nb, cc, wch), lambda j, t: (j, 0, t, 0)),
                   pl.BlockSpec((1, nb, 1, nstj), lambda j, t: (j, 0, 0, 0))),
        out_shape=(jax.ShapeDtypeStruct((nj, nb, n_chunks, wch), F32),
                   jax.ShapeDtypeStruct((nj, nb, 1, nstj), F32)),
        scratch_shapes=[pltpu.VMEM((nb, 1, nstj), F32), pltpu.VMEM((nb, cc, nstj), F32),
                        pltpu.VMEM((nb, cc, nstj), F32),
                        pltpu.VMEM((wch, nstj), BF16), pltpu.VMEM((nstj, wch), BF16),
                        pltpu.VMEM((CHUNK // 2, 2 * LANE, 2 * LANE), BF16)],
        compiler_params=_params(("parallel", "arbitrary"), 56),
        name="ssm",
    )(u4c, um, kc, pc, rc, a16)
    sl = s_last.reshape(nj, nb, 2, gpt, n)
    new_re_p = sl[:, :, 0].transpose(1, 0, 2, 3).reshape(1, nb, g, n)
    new_im_p = sl[:, :, 1].transpose(1, 0, 2, 3).reshape(1, nb, g, n)

    dsk = ssm_d[0].reshape(1, cw)
    wglu_bf = w_glu[0].astype(BF16)
    bglu = b_glu[0].reshape(1, cw)
    nssm = norm_out_ssm[0].reshape(1, cw)
    wout_bf = w_out[0].astype(BF16)
    nffn = norm_ffn[0].reshape(1, d)
    wr_pad = jnp.zeros((d, LANE), F32).at[:, :ne].set(w_router[0])
    wr_hi = wr_pad.astype(BF16)
    wr_lo = (wr_pad - wr_hi.astype(F32)).astype(BF16)
    br = b_router[0].reshape(ne, 1)
    mix_w = (dsk, wglu_bf, bglu, nssm, wout_bf, nffn, wr_hi, wr_lo, br)
    mix_w_specs = [_full((1, cw)), _full((cw, cw)), _full((1, cw)), _full((1, cw)), _full((2 * cw, d)),
                   _full((1, d)), _full((d, LANE)), _full((d, LANE)), _full((ne, 1))]
    mix_out_shape = (jax.ShapeDtypeStruct((ta, d), F32), jax.ShapeDtypeStruct((ta, d), BF16),
                     jax.ShapeDtypeStruct((TOP_K, ta), I32), jax.ShapeDtypeStruct((TOP_K, ta), F32),
                     jax.ShapeDtypeStruct((n_tiles, ne, 1), I32))
    tpm = rows_p // TOK_TILE
    nbg = nb // nbt
    x1_all, hf_all, idx_all, gate_all, cnt = pl.pallas_call(
        _mix_kernel_prompt,
        grid=(seq // tm, nbg),
        in_specs=[pl.BlockSpec((nbt, tm, d), lambda i, b: (b, i, 0)),
                  pl.BlockSpec((nj, nbt, tm // CHUNK, wch), lambda i, b: (0, b, i, 0)),
                  pl.BlockSpec((nbt, tm, cw), lambda i, b: (b, i, 0)),
                  pl.BlockSpec((nbt, tm, cw), lambda i, b: (b, i, 0))] + mix_w_specs,
        out_specs=(pl.BlockSpec((rows_p, d), lambda i, b: (i * nbg + b, 0)),
                   pl.BlockSpec((rows_p, d), lambda i, b: (i * nbg + b, 0)),
                   pl.BlockSpec((TOP_K, rows_p), lambda i, b: (0, i * nbg + b)),
                   pl.BlockSpec((TOP_K, rows_p), lambda i, b: (0, i * nbg + b)),
                   pl.BlockSpec((tpm, ne, 1), lambda i, b: (i * nbg + b, 0, 0))),
        out_shape=mix_out_shape,
        scratch_shapes=[pltpu.VMEM((nj, nbt, tm, LANE), F32)],
        compiler_params=_params(("parallel", "parallel"), 52),
        name="mix_prompt",
    )(x_prompt, y4c, u_tok, ycn_p, *mix_w)

    last = n_tiles - 1
    any_spec = pl.BlockSpec(memory_space=pl.ANY)
    x1_all, hf_all, idx_all, gate_all, cnt = pl.pallas_call(
        _mix_kernel_sample,
        grid=(1,),
        in_specs=[_full((1, ns, d)), _full((ns, cw)), _full((1, ns, cw)),
                  _full((1, ns, cw))] + mix_w_specs + [any_spec] * 5,
        out_specs=(pl.BlockSpec((TOK_TILE, d), lambda i: (last, 0)),
                   pl.BlockSpec((TOK_TILE, d), lambda i: (last, 0)),
                   pl.BlockSpec((TOP_K, TOK_TILE), lambda i: (0, last)),
                   pl.BlockSpec((TOP_K, TOK_TILE), lambda i: (0, last)),
                   pl.BlockSpec((1, ne, 1), lambda i: (last, 0, 0))),
        out_shape=mix_out_shape,
        input_output_aliases={13: 0, 14: 1, 15: 2, 16: 3, 17: 4},
        compiler_params=_params(("arbitrary",), 32),
        name="mix_sample",
    )(x_sample.reshape(1, ns, d), y_s, u_sm[:ns].astype(BF16).reshape(1, ns, cw),
      ycn_s.reshape(1, ns, cw), *mix_w, x1_all, hf_all, idx_all, gate_all, cnt)

    bm = MOE_ROWS
    cnt2 = cnt.reshape(n_tiles, ne)
    before = jnp.cumsum(cnt2, axis=0) - cnt2
    count = jnp.sum(cnt2, axis=0)
    padded = ((count + WIN + bm - 1) // bm) * bm
    pend = jnp.cumsum(padded)
    pstart = pend - padded
    phase = before % SUBLANE
    span = jnp.where(cnt2 > 0, phase + cnt2, 0)
    gstart = (pstart[None, :] + before - phase).astype(I32).reshape(-1)
    nwin = ((span + WIN - 1) // WIN).astype(I32)
    reg8 = ((span + SUBLANE - 1) // SUBLANE) * SUBLANE
    loff = (jnp.cumsum(reg8, axis=1) - reg8).astype(I32)
    tail = jnp.where(span % SUBLANE != 0, loff + (span // SUBLANE) * SUBLANE, -1).astype(I32)
    twin = jnp.sum(nwin, axis=1).astype(I32)
    n_blocks = (ta * TOP_K + ne * (WIN + bm - 1) + bm - 1) // bm
    cap = n_blocks * bm
    blk0 = jnp.arange(n_blocks, dtype=I32) * bm
    blk_e = jnp.minimum(jnp.sum((pend[None, :] <= blk0[:, None]).astype(I32), axis=1), ne - 1)
    e_ar = jnp.arange(ne, dtype=I32)
    blk_hot = blk_e[:, None] == e_ar[None, :]

    def _of_block(per_expert):
        return jnp.sum(jnp.where(blk_hot, per_expert[None, :], 0), axis=1)

    blk_valid = jnp.clip(_of_block(count) - (blk0 - _of_block(pstart)), 0, bm).astype(I32)
    has = count > 0
    later = jnp.logical_and(e_ar[None, :] > e_ar[:, None], has[None, :])
    nxt_e = jnp.min(jnp.where(later, e_ar[None, :], ne), axis=1)
    nxt_e = jnp.where(nxt_e < ne, nxt_e, -1)
    ordinal = jnp.cumsum(has.astype(I32)) - 1
    is_first = jnp.logical_and(blk_valid > 0, blk0 == _of_block(pstart))
    blk_first = jnp.where(is_first, jnp.where(_of_block(ordinal) == 0, 2, 1), 0).astype(I32)
    blk_next = _of_block(nxt_e).astype(I32)
    blk_slot = (_of_block(ordinal) % 2).astype(I32)
    nvalid = jnp.full((1,), tall, I32)
    loff_al = (WIN * (jnp.cumsum(nwin, axis=1) - nwin)).astype(I32)
    wcum = jnp.cumsum(nwin, axis=1)
    wslot = jnp.arange(MAX_WINDOWS, dtype=I32)
    w_hot = jnp.logical_and(wslot[None, :, None] >= (wcum - nwin)[:, None, :],
                            wslot[None, :, None] < wcum[:, None, :])

    def _of_window(per_tile_expert):
        return jnp.sum(jnp.where(w_hot, per_tile_expert[:, None, :], 0), axis=2)

    w_in_run = wslot[None, :] - _of_window(wcum - nwin)
    w_hbm = (_of_window(gstart.reshape(n_tiles, ne)) + WIN * w_in_run).astype(I32).reshape(-1)
    w_buf = (_of_window(loff) + WIN * w_in_run).astype(I32).reshape(-1)
    nrun_d = _dispatch_run_rows(ne)
    zero_grp, spare_grp = nrun_d - 2 * SUBLANE, nrun_d - SUBLANE
    m_lo = jnp.where(nwin > 0, loff, spare_grp).astype(I32).reshape(-1)
    m_tg = jnp.where(jnp.logical_and(nwin > 0, tail >= 0), tail, zero_grp).astype(I32).reshape(-1)
    m_keep = (nwin == 0).astype(I32).reshape(-1)
    tables_d = (w_hbm, w_buf, m_lo, m_tg, m_keep, twin, nvalid)
    tables_c = (w_hbm, twin, nvalid)
    rowoff_d = (loff + phase).astype(I32).reshape(n_tiles, ne, 1)
    rowoff_c = (loff_al + phase).astype(I32).reshape(n_tiles, ne, 1)
    xw = d // 2 + GATE_COLS

    xs = pl.pallas_call(
        _dispatch_kernel,
        grid_spec=pltpu.PrefetchScalarGridSpec(
            num_scalar_prefetch=len(tables_d),
            grid=(n_tiles,),
            in_specs=[pl.BlockSpec((TOK_TILE, d), lambda i, *_: (i, 0)),
                      pl.BlockSpec((TOP_K, TOK_TILE), lambda i, *_: (0, i)),
                      pl.BlockSpec((TOP_K, TOK_TILE), lambda i, *_: (0, i)),
                      pl.BlockSpec((1, ne, 1), lambda i, *_: (i, 0, 0))],
            out_specs=pl.BlockSpec(memory_space=pl.ANY),
            scratch_shapes=[pltpu.VMEM((2, _dispatch_run_rows(ne), xw), U32),
                            pltpu.VMEM((ne, SUBLANE, xw), U32), pltpu.SemaphoreType.DMA((2,))]),
        out_shape=jax.ShapeDtypeStruct((cap, xw), U32),
        compiler_params=_params(("arbitrary",), 40),
        name="dispatch",
    )(*tables_d, hf_all, idx_all, gate_all, rowoff_d)

    yb = pl.pallas_call(
        _moe_kernel,
        grid_spec=pltpu.PrefetchScalarGridSpec(
            num_scalar_prefetch=5,
            grid=(n_blocks,),
            in_specs=[pl.BlockSpec((bm, xw), lambda i, *_: (i, 0)),
                      pl.BlockSpec((1, 1, 2 * dff), lambda i, be, *_: (be[i], 0, 0)),
                      pl.BlockSpec((1, 1, d), lambda i, be, *_: (be[i], 0, 0)),
                      pl.BlockSpec(memory_space=pl.ANY), pl.BlockSpec(memory_space=pl.ANY)],
            out_specs=pl.BlockSpec((bm, d // 2), lambda i, *_: (i, 0)),
            scratch_shapes=[pltpu.VMEM((2, d, 2 * dff), F32), pltpu.VMEM((2, dff, d), F32),
                            pltpu.VMEM((d, 2 * dff), BF16), pltpu.VMEM((dff, d), BF16),
                            pltpu.SemaphoreType.DMA((2, 2))]),
        out_shape=jax.ShapeDtypeStruct((cap, d // 2), U32),
        compiler_params=_params(("arbitrary",), 58),
        name="moe",
    )(blk_e, blk_valid, blk_first, blk_next, blk_slot, xs, b_gate_up[0].reshape(ne, 1, 2 * dff),
      b_down[0].reshape(ne, 1, d), w_gate_up[0], w_down[0])

    nfin = norm_final.reshape(1, d)
    nbh = TOK_TILE // tm

    def _tile_of(s):
        return (s + n_tiles - 1) % n_tiles

    tiles_per_time = nbg * tpm

    def _yp_index(s, *_):
        t = jnp.maximum(s - 1, 0)
        return (t % tiles_per_time, t // tiles_per_time, 0)

    y_p, y_sm = pl.pallas_call(
        _combine_kernel,
        grid_spec=pltpu.PrefetchScalarGridSpec(
            num_scalar_prefetch=len(tables_c),
            grid=(n_tiles,),
            in_specs=[pl.BlockSpec((TOK_TILE, d), lambda s, *_: (_tile_of(s), 0)),
                      pl.BlockSpec((TOP_K, TOK_TILE), lambda s, *_: (0, _tile_of(s))),
                      pl.BlockSpec((1, ne, 1), lambda s, *_: (_tile_of(s), 0, 0)),
                      pl.BlockSpec((1, d), lambda s, *_: (0, 0)),
                      pl.BlockSpec(memory_space=pl.ANY)],
            out_specs=(pl.BlockSpec((nbh, tm, d), _yp_index),
                       pl.BlockSpec((ns, d), lambda s, *_: (0, 0))),
            scratch_shapes=[pltpu.VMEM((2, _combine_run_rows(ne), d // 2), U32),
                            pltpu.SemaphoreType.DMA((2,))]),
        out_shape=(jax.ShapeDtypeStruct((nb, seq, d), F32), jax.ShapeDtypeStruct((ns, d), F32)),
        compiler_params=_params(("arbitrary",), 48),
        name="combine",
    )(*tables_c, x1_all, idx_all, rowoff_c, nfin, yb)

    return (y_p, y_sm.reshape(ns, 1, d), new_re_p, new_im_p, new_conv_p,
            new_re_s, new_im_s, new_conv_s)
```

```python
import math

import jax
import jax.numpy as jnp
from jax import lax
from jax.experimental import pallas as pl
from jax.experimental.pallas import tpu as pltpu

F32 = jnp.float32
BF16 = jnp.bfloat16
U32 = jnp.uint32
I32 = jnp.int32
EPS = 1e-5
CHUNK = 16
LANE = 128
TOP_K = 4
SWIGLU_LIMIT = 7.0
SWIGLU_ALPHA = 1.702
MOE_ROWS = 512
MOE_SUB = 256
TOK_TILE = 256
WIN = 32
SUBLANE = 8
GATE_COLS = LANE


MAX_WINDOWS = 72


def _dispatch_run_rows(ne):
    return TOP_K * TOK_TILE + ne * 2 * (SUBLANE - 1) + WIN


def _combine_run_rows(ne):
    return -(-(TOP_K * TOK_TILE + ne * (SUBLANE - 1 + WIN - 1)) // WIN) * WIN
HI_MASK = 0xFFFF0000
MIB = 1024 * 1024


def _rms(x, g):
    return x * lax.rsqrt(jnp.mean(x * x, axis=-1, keepdims=True) + EPS) * g


def _gelu_tanh(x):
    c = math.sqrt(2.0 / math.pi)
    return 0.5 * x * (1.0 + jnp.tanh(c * (x + 0.044715 * (x * x * x))))


def _params(sem, vmem_mib):
    return pltpu.CompilerParams(dimension_semantics=sem, vmem_limit_bytes=vmem_mib * MIB)


def _pack_pairs(a, b):
    return (pltpu.bitcast(a, U32) >> 16) | (pltpu.bitcast(b, U32) & jnp.uint32(HI_MASK))


def _unpack_pairs(w):
    lo = pltpu.bitcast(w << 16, F32)
    hi = pltpu.bitcast(w & jnp.uint32(HI_MASK), F32)
    return jnp.concatenate([lo, hi], axis=-1).astype(BF16)


def _iota2(shape, axis):
    return lax.broadcasted_iota(I32, shape, axis)


def _expand_cols(compact, reps_log2, n_log2):
    q = _iota2((compact.shape[1], 2 << (reps_log2 + n_log2)), 0)
    c = _iota2((compact.shape[1], 2 << (reps_log2 + n_log2)), 1)
    nmask = (1 << n_log2) - 1
    same = jnp.logical_and((q >> n_log2) == (c >> (reps_log2 + n_log2)), (q & nmask) == (c & nmask))
    return jnp.dot(compact, jnp.where(same, 1.0, 0.0).astype(BF16), preferred_element_type=F32)


def _expand_rows(compact, reps_log2, n_log2):
    r = _iota2((2 << (reps_log2 + n_log2), compact.shape[0]), 0)
    q = _iota2((2 << (reps_log2 + n_log2), compact.shape[0]), 1)
    nmask = (1 << n_log2) - 1
    same = jnp.logical_and((r >> (reps_log2 + n_log2)) == (q >> n_log2), (r & nmask) == (q & nmask))
    return jnp.dot(jnp.where(same, 1.0, 0.0).astype(BF16), compact, preferred_element_type=F32)


def _group_mask(shape, row_shift, col_shift, ngroups):
    r = _iota2(shape, 0)
    c = _iota2(shape, 1)
    return ((r >> row_shift) & (ngroups - 1)) == ((c >> col_shift) & (ngroups - 1))


def _small_front_kernel(x_ref, nmix_ref, win_ref, s0r_ref, s0i_ref, b0_ref, b1_ref, cw_ref,
                        bc_ref, cc_ref, abr_ref, abi_ref, nconv_ref,
                        u_ref, z_ref, y_ref, ycn_ref, sr_ref, si_ref, bdb_ref, cm_ref):
    ns, nst = s0r_ref.shape
    cw = u_ref.shape[1]
    n = bc_ref.shape[1] // 2
    nlog = n.bit_length() - 1
    glog = (nst // n).bit_length() - 1
    hlog = (cw >> glog).bit_length() - 1
    bdb_ref[...] = jnp.where(_group_mask(bdb_ref.shape, hlog, nlog, 1 << glog),
                             _expand_cols(bc_ref[...], glog, nlog), 0.0).astype(BF16)
    cm_ref[...] = jnp.where(_group_mask(cm_ref.shape, nlog, hlog, 1 << glog),
                            _expand_rows(cc_ref[...], glog, nlog), 0.0).astype(BF16)
    h = _rms(x_ref[...], nmix_ref[...]).astype(BF16)
    proj = jnp.dot(h, win_ref[...], preferred_element_type=F32)
    u = proj[:, 0:cw]
    zc = proj[:, cw:2 * cw]
    gb = proj[:, 2 * cw:3 * cw]
    gc = proj[:, 3 * cw:4 * cw]
    z = gc * zc
    u_ref[...] = u
    z_ref[...] = z
    bu = jnp.dot(u[:ns].astype(BF16), bdb_ref[...], preferred_element_type=F32)
    abr = abr_ref[...]
    abi = abi_ref[...]
    s0r = s0r_ref[...]
    s0i = s0i_ref[...]
    sr = abr * s0r - abi * s0i + bu[:, :nst]
    si = abr * s0i + abi * s0r + bu[:, nst:]
    sr_ref[...] = sr
    si_ref[...] = si
    scat = jnp.concatenate([sr, si], axis=-1).astype(BF16)
    y_ref[...] = jnp.dot(scat, cm_ref[...], preferred_element_type=F32)
    conv = cw_ref[0:1, :] * b0_ref[...] + cw_ref[1:2, :] * b1_ref[...] + cw_ref[2:3, :] * z[:ns]
    ycn_ref[...] = _rms(gb[:ns] * conv, nconv_ref[...]).astype(BF16)


def _front_kernel(x_ref, nmix_ref, win_ref, zm_ref, cw_ref, nconv_ref,
                  uc_ref, ut_ref, ycn_ref, zt_ref, zbuf, ubuf):
    i = pl.program_id(1)
    nb, tt, d = x_ref.shape
    cw = ycn_ref.shape[2]
    rows = nb * tt
    ncz = tt // CHUNK

    @pl.when(i == 0)
    def _():
        zbuf[:, 0:8, :] = jnp.broadcast_to(zm_ref[...][None], (nb, 8, cw))

    h = _rms(x_ref[...].reshape(rows, d), nmix_ref[...]).astype(BF16)
    u = jnp.dot(h, win_ref[:, 0:cw], preferred_element_type=F32)
    ut_ref[...] = u.astype(BF16).reshape(nb, tt, cw)
    for j in range(cw // LANE):
        ubuf[j] = u[:, j * LANE:(j + 1) * LANE].reshape(nb, tt, LANE)
    for s in range(CHUNK):
        for j in range(cw // LANE):
            piece = ubuf[j, :, pl.ds(s, ncz, stride=CHUNK), :]
            uc_ref[j, :, :, s * LANE:(s + 1) * LANE] = piece.astype(BF16)
    zc = jnp.dot(h, win_ref[:, cw:2 * cw], preferred_element_type=F32)
    gc = jnp.dot(h, win_ref[:, 3 * cw:4 * cw], preferred_element_type=F32)
    z3 = (gc * zc).reshape(nb, tt, cw)
    zbuf[:, 8:8 + tt, :] = z3
    z1 = zbuf[:, 7:7 + tt, :]
    z2 = zbuf[:, 6:6 + tt, :]
    conv = cw_ref[0:1, :] * z2 + cw_ref[1:2, :] * z1 + cw_ref[2:3, :] * z3
    gb = jnp.dot(h, win_ref[:, 2 * cw:3 * cw], preferred_element_type=F32)
    yc = gb * conv.reshape(rows, cw)
    ycn_ref[...] = _rms(yc, nconv_ref[...]).astype(BF16).reshape(nb, tt, cw)
    tail = zbuf[:, tt:tt + 8, :]
    zt_ref[...] = tail
    zbuf[:, 0:8, :] = tail


def _ssm_kernel(u_ref, um_ref, kc_ref, pc_ref, rc_ref, a16_ref, y_ref, sl_ref,
                s_carry, ds_ref, sp_ref, p_s, r_s, t_s):
    th = pl.program_id(1)
    _, nb, cc, w = u_ref.shape
    nst = p_s.shape[1]
    half = nst // 2
    rows = nb * cc
    blk = 2 * LANE
    u = u_ref[0].reshape(rows, w)

    @pl.when(th == 0)
    def _():
        hch = kc_ref.shape[2]
        gpt = LANE // hch
        hlog = hch.bit_length() - 1
        glog = gpt.bit_length() - 1
        nlog = (pc_ref.shape[2] // 2).bit_length() - 1
        p_s[...] = jnp.where(_group_mask(p_s.shape, hlog, nlog, gpt),
                             _expand_cols(pc_ref[0], glog, nlog), 0.0).astype(BF16)
        r_s[...] = jnp.where(_group_mask(r_s.shape, nlog, hlog, gpt),
                             _expand_rows(rc_ref[0], glog, nlog), 0.0).astype(BF16)
        nlag = kc_ref.shape[1] // LANE
        o = _iota2((hch, LANE), 0)
        c = _iota2((hch, LANE), 1)
        spread = jnp.where((c & (hch - 1)) == o, 1.0, 0.0).astype(BF16)
        lagm = jnp.dot(kc_ref[0], spread, preferred_element_type=F32)
        r = _iota2(lagm.shape, 0)
        c = _iota2(lagm.shape, 1)
        lagm = jnp.where(((r >> hlog) & (gpt - 1)) == (c >> hlog), lagm, 0.0).astype(BF16)
        for dlt in range(nlag // 2):
            b0 = lagm[(2 * dlt) * LANE:(2 * dlt + 1) * LANE]
            b1 = lagm[(2 * dlt + 1) * LANE:(2 * dlt + 2) * LANE]
            b2 = lagm[(2 * dlt + 2) * LANE:(2 * dlt + 3) * LANE]
            t_s[dlt, 0:LANE, 0:LANE] = b1
            t_s[dlt, 0:LANE, LANE:blk] = b2
            t_s[dlt, LANE:blk, 0:LANE] = b0
            t_s[dlt, LANE:blk, LANE:blk] = b1
        ds_ref[:, 0:8, :] = jnp.dot(um_ref[0], p_s[...], preferred_element_type=F32).reshape(nb, 8, nst)
        s_carry[...] = ds_ref[:, 0:1, :]

    ds_ref[...] = jnp.dot(u, p_s[...], preferred_element_type=F32).reshape(nb, cc, nst)
    ar = a16_ref[0, 0:1, :].reshape(1, 1, half)
    ai = a16_ref[0, 1:2, :].reshape(1, 1, half)
    sr = s_carry[:, :, 0:half]
    si = s_carry[:, :, half:nst]
    for c in range(cc):
        sp_ref[:, c:c + 1, 0:half] = sr
        sp_ref[:, c:c + 1, half:nst] = si
        dr = ds_ref[:, c:c + 1, 0:half]
        di = ds_ref[:, c:c + 1, half:nst]
        sr, si = ar * sr - ai * si + dr, ar * si + ai * sr + di
    s_carry[:, :, 0:half] = sr
    s_carry[:, :, half:nst] = si
    sl_ref[0, :, :, 0:half] = sr
    sl_ref[0, :, :, half:nst] = si

    sp = sp_ref[...].reshape(rows, nst).astype(BF16)
    for tb in range(w // blk):
        acc = jnp.dot(sp, r_s[:, tb * blk:(tb + 1) * blk], preferred_element_type=F32)
        for sb in range(tb + 1):
            acc = acc + jnp.dot(u[:, sb * blk:(sb + 1) * blk], t_s[tb - sb],
                                preferred_element_type=F32)
        y_ref[0, :, :, tb * blk:(tb + 1) * blk] = acc.reshape(nb, cc, blk)


def _mix_rows(x, yssm, ut, ycn, dsk_ref, wglu_ref, bglu_ref, nssm_ref, wout_ref,
              nffn_ref, wrh_ref, wrl_ref, br_ref):
    ne = br_ref.shape[0]
    y = _gelu_tanh(yssm + dsk_ref[...] * ut.astype(F32))
    glu = jnp.dot(y.astype(BF16), wglu_ref[...], preferred_element_type=F32) + bglu_ref[...]
    o = y * jax.nn.sigmoid(glu)
    ysn = _rms(o, nssm_ref[...]).astype(BF16)
    mix = jnp.concatenate([ysn, ycn], axis=-1)
    x1 = x + jnp.dot(mix, wout_ref[...], preferred_element_type=F32)
    hf = _rms(x1, nffn_ref[...])
    hf_hi = hf.astype(BF16)
    hf_lo = (hf - hf_hi.astype(F32)).astype(BF16)
    logits = (jnp.dot(hf_hi, wrh_ref[...], preferred_element_type=F32)
              + jnp.dot(hf_lo, wrh_ref[...], preferred_element_type=F32)
              + jnp.dot(hf_hi, wrl_ref[...], preferred_element_type=F32))
    lt = logits.T[0:ne, :] + br_ref[...]
    iota = lax.broadcasted_iota(I32, lt.shape, 0)
    vals, idxs = [], []
    sel = jnp.zeros(lt.shape, F32)
    for _ in range(TOP_K):
        m = jnp.max(lt, axis=0, keepdims=True)
        ik = jnp.min(jnp.where(lt == m, iota, ne), axis=0, keepdims=True)
        vals.append(m)
        idxs.append(ik)
        hit = iota == ik
        sel = sel + jnp.where(hit, 1.0, 0.0)
        lt = jnp.where(hit, -jnp.inf, lt)
    es = [jnp.exp(v - vals[0]) for v in vals]
    tot = es[0] + es[1] + es[2] + es[3]
    idx = jnp.concatenate(idxs, axis=0)
    gates = jnp.concatenate([e / tot for e in es], axis=0)
    return x1, hf_hi, idx, gates, sel


def _mix_kernel_prompt(*refs):
    x_ref, yc_ref = refs[0], refs[1]
    x1_ref, hf_ref, idx_ref, gate_ref, cnt_ref, ybuf = refs[13:]
    nj, nb, ncz, _ = yc_ref.shape
    for s in range(CHUNK):
        for j in range(nj):
            ybuf[j, :, pl.ds(s, ncz, stride=CHUNK), :] = yc_ref[j, :, :, s * LANE:(s + 1) * LANE]
    ut_ref, ycn_ref = refs[2], refs[3]
    tt, d = x_ref.shape[1], x_ref.shape[2]
    cw = nj * LANE
    nbc = TOK_TILE // tt
    for t in range(cnt_ref.shape[0]):
        b0, r0 = t * nbc, t * TOK_TILE
        yssm = jnp.concatenate([ybuf[j, b0:b0 + nbc].reshape(TOK_TILE, LANE) for j in range(nj)], axis=-1)
        x1, hf, idx, gates, sel = _mix_rows(
            x_ref[b0:b0 + nbc].reshape(TOK_TILE, d), yssm, ut_ref[b0:b0 + nbc].reshape(TOK_TILE, cw),
            ycn_ref[b0:b0 + nbc].reshape(TOK_TILE, cw), *refs[4:13])
        x1_ref[r0:r0 + TOK_TILE, :] = x1
        hf_ref[r0:r0 + TOK_TILE, :] = hf
        idx_ref[:, r0:r0 + TOK_TILE] = idx
        gate_ref[:, r0:r0 + TOK_TILE] = gates
        cnt_ref[t] = jnp.sum(sel, axis=1, keepdims=True).astype(I32)


def _mix_kernel_sample(*refs):
    x1, hf, idx, gates, sel = _mix_rows(refs[0][0], refs[1][...], refs[2][0], refs[3][0], *refs[4:13])
    x1_ref, hf_ref, idx_ref, gate_ref, cnt_ref = refs[18:]
    ns = x1.shape[0]
    x1_ref[...] = jnp.zeros(x1_ref.shape, x1_ref.dtype)
    hf_ref[...] = jnp.zeros(hf_ref.shape, hf_ref.dtype)
    idx_ref[...] = jnp.zeros(idx_ref.shape, idx_ref.dtype)
    gate_ref[...] = jnp.zeros(gate_ref.shape, gate_ref.dtype)
    x1_ref[0:ns, :] = x1
    hf_ref[0:ns, :] = hf
    idx_ref[:, 0:ns] = idx
    gate_ref[:, 0:ns] = gates
    cnt_ref[0] = jnp.sum(sel, axis=1, keepdims=True).astype(I32)


def _split_bf16(x, parts):
    out = []
    for _ in range(parts - 1):
        p = x.astype(BF16)
        out.append(p)
        x = x - p.astype(F32)
    out.append(x.astype(BF16))
    return out


def _run_onehot(idx_ref, loff_ref, rgn_ref, tile, n_valid, nrun, gate_ref=None):
    ne = loff_ref.shape[1]
    tt = idx_ref.shape[1]
    e_iota = lax.broadcasted_iota(I32, (ne, tt), 0)
    tok = tile * tt + lax.broadcasted_iota(I32, (1, tt), 1)
    valid = tok < n_valid
    hits = [jnp.logical_and(e_iota == idx_ref[k:k + 1, :], valid) for k in range(TOP_K)]
    sel = jnp.zeros((ne, tt), F32)
    for h in hits:
        sel = sel + jnp.where(h, 1.0, 0.0)
    before = lax.broadcasted_iota(I32, (tt, tt), 0) < lax.broadcasted_iota(I32, (tt, tt), 1)
    tri = jnp.where(before, 1.0, 0.0).astype(BF16)
    base = jnp.dot(sel.astype(BF16), tri, preferred_element_type=F32) + loff_ref[0].astype(F32)
    base = jnp.where(sel > 0.0, base + 1.0, 0.0)
    b_hi = jnp.floor(base * (1.0 / 256.0))
    b_lo = base - 256.0 * b_hi
    r_i = lax.broadcasted_iota(I32, (nrun, ne), 0)
    own = jnp.logical_and(r_i >= rgn_ref[0, 0:1, :], r_i < rgn_ref[0, 1:2, :])
    own_bf = jnp.where(own, 1.0, 0.0).astype(BF16)
    want = (256.0 * jnp.dot(own_bf, b_hi.astype(BF16), preferred_element_type=F32)
            + jnp.dot(own_bf, b_lo.astype(BF16), preferred_element_type=F32))
    r_f = (lax.broadcasted_iota(I32, (nrun, tt), 0) + 1).astype(F32)
    smat = jnp.where(want == r_f, 1.0, 0.0).astype(BF16)
    if gate_ref is None:
        return smat
    gate_e = jnp.zeros((ne, tt), F32)
    for k, h in enumerate(hits):
        gate_e = gate_e + jnp.where(h, gate_ref[k:k + 1, :], 0.0)
    nparts = 3
    pieces = jnp.concatenate(_split_bf16(gate_e, nparts), axis=0)
    per = lax.dot_general(smat, pieces, (((1,), (1,)), ((), ())), preferred_element_type=F32)
    lo3 = jnp.concatenate([rgn_ref[0, 0:1, :]] * nparts, axis=1)
    hi3 = jnp.concatenate([rgn_ref[0, 1:2, :]] * nparts, axis=1)
    r_i3 = lax.broadcasted_iota(I32, (nrun, nparts * ne), 0)
    own3 = jnp.logical_and(r_i3 >= lo3, r_i3 < hi3)
    gcol = jnp.sum(jnp.where(own3, per, 0.0), axis=1, keepdims=True)
    return smat, gcol


def _window_copy(buf, slot, hbm, lo, g, sem, to_hbm):
    src = buf.at[slot, pl.ds(pl.multiple_of(lo, SUBLANE), WIN)]
    dst = hbm.at[pl.ds(pl.multiple_of(g, SUBLANE), WIN)]
    if to_hbm:
        return pltpu.make_async_copy(src, dst, sem.at[slot])
    return pltpu.make_async_copy(dst, src, sem.at[slot])


def _start_windows(wg_ref, wl_ref, tw_ref, tile, buf, slot, hbm, sem, to_hbm):
    def per_window(w, c):
        k = tile * MAX_WINDOWS + w
        lo = w * WIN if wl_ref is None else wl_ref[k]
        _window_copy(buf, slot, hbm, lo, wg_ref[k], sem, to_hbm).start()
        return c

    lax.fori_loop(0, tw_ref[tile], per_window, 0)


def _wait_windows(count, buf, slot, hbm, sem, to_hbm):
    def per_window(w, c):
        _window_copy(buf, slot, hbm, 0, 0, sem, to_hbm).wait()
        return c

    lax.fori_loop(0, count, per_window, 0)


def _dispatch_kernel(wg_ref, wl_ref, mlo_ref, mtg_ref, keep_ref, tw_ref, nv_ref,
                     hf_ref, idx_ref, gate_ref, loff_ref, rgn_ref, xs_ref, buf, carry, sem):
    i = pl.program_id(0)
    nt = pl.num_programs(0)
    ne = loff_ref.shape[1]
    tt, d = hf_ref.shape
    nrun = buf.shape[1]
    slot = i % 2

    @pl.when(i == 0)
    def _():
        carry[...] = jnp.zeros(carry.shape, carry.dtype)

    smat, gcol = _run_onehot(idx_ref, loff_ref, rgn_ref, i, nv_ref[0], nrun, gate_ref)
    xr = jnp.dot(smat, hf_ref[...], preferred_element_type=F32)
    lane0 = lax.broadcasted_iota(I32, (nrun, GATE_COLS), 1) == 0
    buf[slot, :, 0:d // 2] = _pack_pairs(xr[:, :d // 2], xr[:, d // 2:])
    buf[slot, :, d // 2:] = pltpu.bitcast(jnp.where(lane0, gcol, 0.0), U32)

    def merge(e, c):
        k = i * ne + e
        lo = pl.multiple_of(mlo_ref[k], SUBLANE)
        buf[slot, pl.ds(lo, SUBLANE), :] = buf[slot, pl.ds(lo, SUBLANE), :] | carry[e]
        tg = pl.multiple_of(mtg_ref[k], SUBLANE)
        carry[e] = jnp.where(keep_ref[k] > 0, carry[e], buf[slot, pl.ds(tg, SUBLANE), :])
        return c

    lax.fori_loop(0, ne, merge, 0, unroll=4)

    @pl.when(i > 0)
    def _():
        _wait_windows(tw_ref[i - 1], buf, 1 - slot, xs_ref, sem, True)

    _start_windows(wg_ref, wl_ref, tw_ref, i, buf, slot, xs_ref, sem, True)

    @pl.when(i == nt - 1)
    def _():
        _wait_windows(tw_ref[i], buf, slot, xs_ref, sem, True)


def _expert_weight_copies(wg_hbm, wd_hbm, wg_f32, wd_f32, sem, e, slot):
    return (pltpu.make_async_copy(wg_hbm.at[e], wg_f32.at[slot], sem.at[0, slot]),
            pltpu.make_async_copy(wd_hbm.at[e], wd_f32.at[slot], sem.at[1, slot]))


def _moe_kernel(be_ref, bv_ref, first_ref, nxt_ref, slot_ref,
                x_ref, bg_ref, bd_ref, wg_hbm, wd_hbm, y_ref,
                wg_f32, wd_f32, wg_bf, wd_bf, sem):
    i = pl.program_id(0)
    e = be_ref[i]
    dff = wd_bf.shape[0]
    bm = x_ref.shape[0]
    sub = MOE_SUB
    nw = x_ref.shape[1] - GATE_COLS

    @pl.when(first_ref[i] > 0)
    def _():
        slot = slot_ref[i]

        @pl.when(first_ref[i] > 1)
        def _():
            for cp in _expert_weight_copies(wg_hbm, wd_hbm, wg_f32, wd_f32, sem, e, slot):
                cp.start()

        for cp in _expert_weight_copies(wg_hbm, wd_hbm, wg_f32, wd_f32, sem, e, slot):
            cp.wait()

        @pl.when(nxt_ref[i] >= 0)
        def _():
            for cp in _expert_weight_copies(wg_hbm, wd_hbm, wg_f32, wd_f32, sem, nxt_ref[i], 1 - slot):
                cp.start()

        wg_bf[...] = wg_f32[slot].astype(BF16)
        wd_bf[...] = wd_f32[slot].astype(BF16)

    def rows(r0, nrows):
        for c in range(nrows // sub):
            lo = r0 + c * sub
            live = lax.broadcasted_iota(I32, (sub, 1), 0) + lo < bv_ref[i]
            x = jnp.where(live, _unpack_pairs(x_ref[lo:lo + sub, 0:nw]), jnp.zeros((), BF16))
            route = jnp.where(live, pltpu.bitcast(x_ref[lo:lo + sub, nw:], F32)[:, 0:1], 0.0)
            gu = jnp.dot(x, wg_bf[...], preferred_element_type=F32) + bg_ref[0]
            gate = jnp.minimum(gu[:, :dff], SWIGLU_LIMIT)
            up = jnp.clip(gu[:, dff:], -SWIGLU_LIMIT, SWIGLU_LIMIT)
            h = gate * jax.nn.sigmoid(SWIGLU_ALPHA * gate) * (up + 1.0)
            y = jnp.dot(h.astype(BF16), wd_bf[...], preferred_element_type=F32) + bd_ref[0]
            yr = (route * y).astype(BF16).astype(F32)
            half = yr.shape[1] // 2
            y_ref[lo:lo + sub, :] = _pack_pairs(yr[:, :half], yr[:, half:])

    half_rows = bm // 2

    @pl.when(bv_ref[i] > half_rows)
    def _():
        rows(0, bm)

    @pl.when(jnp.logical_and(bv_ref[i] > 0, bv_ref[i] <= half_rows))
    def _():
        rows(0, half_rows)
        y_ref[half_rows:bm, :] = jnp.zeros((bm - half_rows, y_ref.shape[1]), y_ref.dtype)

    @pl.when(bv_ref[i] <= 0)
    def _():
        y_ref[...] = jnp.zeros(y_ref.shape, y_ref.dtype)


def _combine_tile(step, nt):
    return (step + nt - 1) % nt


def _combine_kernel(wg_ref, tw_ref, nv_ref,
                    x1_ref, idx_ref, loff_ref, rgn_ref, nf_ref, yb_ref,
                    yp_ref, ys_ref, buf, sem):
    s = pl.program_id(0)
    nt = pl.num_programs(0)
    tt, d = x1_ref.shape
    nrun = buf.shape[1]
    tile = _combine_tile(s, nt)
    slot = s % 2

    @pl.when(s == 0)
    def _():
        buf[...] = jnp.zeros(buf.shape, buf.dtype)
        _start_windows(wg_ref, None, tw_ref, tile, buf, slot, yb_ref, sem, False)

    @pl.when(s + 1 < nt)
    def _():
        _start_windows(wg_ref, None, tw_ref, _combine_tile(s + 1, nt), buf, 1 - slot, yb_ref, sem, False)

    smat = _run_onehot(idx_ref, loff_ref, rgn_ref, tile, nv_ref[0], nrun)

    _wait_windows(tw_ref[tile], buf, slot, yb_ref, sem, False)
    yrun = _unpack_pairs(buf[slot])
    moe = lax.dot_general(smat, yrun, (((0,), (0,)), ((), ())), preferred_element_type=F32)
    out = _rms(x1_ref[...] + moe, nf_ref[...])

    @pl.when(s == 0)
    def _():
        ys_ref[...] = out[0:ys_ref.shape[0], :]

    @pl.when(s > 0)
    def _():
        yp_ref[...] = out.reshape(yp_ref.shape)


def _ssm_matrices(a_re, a_im, log_dt, b_re, b_im, c_re, c_im):
    g, n = a_re.shape
    hch = b_re.shape[2]
    gpt = LANE // hch
    nj = g // gpt
    a = lax.complex(a_re, a_im)
    dta = a * jnp.exp(log_dt)[:, None]
    a_bar = jnp.exp(dta)
    bb = ((a_bar - 1.0) / a)[:, :, None] * lax.complex(b_re, b_im)
    cc = lax.complex(c_re, c_im)
    ks = jnp.arange(CHUNK + 1, dtype=F32)
    pw = jnp.exp(dta[None] * ks[:, None, None])
    kk = jnp.real(jnp.einsum('gon,kgn,gni->kgio', cc, pw[:CHUNK], bb))
    kk = jnp.concatenate([jnp.zeros_like(kk[:1]), kk], axis=0)
    kc = kk.reshape(CHUNK + 1, nj, gpt * hch, hch).transpose(1, 0, 2, 3)
    kc = kc.reshape(nj, (CHUNK + 1) * LANE, hch).astype(BF16)
    pwr, pwi = jnp.real(pw), jnp.imag(pw)
    bbr = jnp.real(bb).transpose(0, 2, 1).reshape(1, g * hch, n)
    bbi = jnp.imag(bb).transpose(0, 2, 1).reshape(1, g * hch, n)
    par = jnp.repeat(pwr[CHUNK - 1::-1][:CHUNK], hch, axis=1)
    pai = jnp.repeat(pwi[CHUNK - 1::-1][:CHUNK], hch, axis=1)
    pc = jnp.concatenate([par * bbr - pai * bbi, par * bbi + pai * bbr], axis=-1)
    pc = pc.reshape(CHUNK, nj, gpt * hch, 2 * n).transpose(1, 0, 2, 3)
    pc = pc.reshape(nj, CHUNK * LANE, 2 * n).astype(BF16)
    ccr = jnp.real(cc).transpose(2, 0, 1).reshape(n, 1, g * hch)
    cci = jnp.imag(cc).transpose(2, 0, 1).reshape(n, 1, g * hch)
    qar = jnp.repeat(pwr[1:CHUNK + 1].transpose(2, 0, 1), hch, axis=2)
    qai = jnp.repeat(pwi[1:CHUNK + 1].transpose(2, 0, 1), hch, axis=2)
    rc = jnp.stack([ccr * qar - cci * qai, -(ccr * qai + cci * qar)], axis=0)
    rc = rc.reshape(2 * n, CHUNK, nj, gpt * hch).transpose(2, 0, 1, 3)
    rc = rc.reshape(nj, 2 * n, CHUNK * LANE).astype(BF16)
    a16 = pw[CHUNK].reshape(nj, 1, gpt * n)
    a16 = jnp.concatenate([jnp.real(a16), jnp.imag(a16)], axis=1)
    bc = jnp.stack([jnp.real(bb), jnp.imag(bb)], axis=0).transpose(1, 3, 0, 2)
    bc = bc.reshape(g * hch, 2 * n).astype(BF16)
    c2 = jnp.stack([jnp.real(cc), -jnp.imag(cc)], axis=0).transpose(0, 3, 1, 2)
    c2 = c2.reshape(2 * n, g * hch).astype(BF16)
    abr = jnp.real(a_bar).reshape(1, g * n)
    abi = jnp.imag(a_bar).reshape(1, g * n)
    return kc, pc, rc, a16, bc, c2, abr, abi


def _full(shape):
    return pl.BlockSpec(shape, lambda *_: (0,) * len(shape))


def kernel(x_prompt, x_sample, state_ssm_re, state_ssm_im, state_conv, meta_tokens, norm_mix, w_in,
           ssm_a_re, ssm_a_im, ssm_log_dt, ssm_b_re, ssm_b_im, ssm_c_re, ssm_c_im, ssm_d, w_glu, b_glu,
           conv_w, norm_out_ssm, norm_out_conv, w_out, norm_ffn, w_router, b_router, w_gate_up,
           b_gate_up, w_down, b_down, norm_final):
    nb, seq, d = x_prompt.shape
    ns = x_sample.shape[0]
    depth, _, g, n = state_ssm_re.shape
    assert depth == 1 and x_sample.shape[1] == 1 and meta_tokens.shape[0] == CHUNK
    cw = conv_w.shape[2]
    nj = cw // LANE
    ne = w_router.shape[2]
    dff = w_down.shape[2]
    nst = g * n
    nbt = nb // 2
    tt = 256
    n_chunks = seq // CHUNK
    tp = nb * seq
    tall = tp + ns
    tm = 128
    rows_p = nbt * tm
    assert rows_p % TOK_TILE == 0 and TOK_TILE % tm == 0 and ns <= TOK_TILE and d % 2 == 0
    n_tiles = tp // TOK_TILE + 1
    ta = n_tiles * TOK_TILE

    kc, pc, rc, a16, bc, c2, abr, abi = _ssm_matrices(
        ssm_a_re[0], ssm_a_im[0], ssm_log_dt[0], ssm_b_re[0], ssm_b_im[0], ssm_c_re[0], ssm_c_im[0])
    win_bf = w_in[0].astype(BF16)
    nmix = norm_mix[0].reshape(1, d)
    nconv = norm_out_conv[0].reshape(1, cw)
    cwt = conv_w[0]

    xsm = jnp.concatenate([x_sample.reshape(ns, d), meta_tokens], axis=0)
    nsm = ns + CHUNK
    s0r = state_ssm_re[0].reshape(ns, nst)
    s0i = state_ssm_im[0].reshape(ns, nst)
    buf0 = state_conv[0, :, 0, :]
    buf1 = state_conv[0, :, 1, :]
    u_sm, z_sm, y_s, ycn_s, sr_s, si_s = pl.pallas_call(
        _small_front_kernel,
        out_shape=(jax.ShapeDtypeStruct((nsm, cw), F32), jax.ShapeDtypeStruct((nsm, cw), F32),
                   jax.ShapeDtypeStruct((ns, cw), F32), jax.ShapeDtypeStruct((ns, cw), BF16),
                   jax.ShapeDtypeStruct((ns, nst), F32), jax.ShapeDtypeStruct((ns, nst), F32)),
        scratch_shapes=[pltpu.VMEM((cw, 2 * nst), BF16), pltpu.VMEM((2 * nst, cw), BF16)],
        compiler_params=_params(None, 56),
        name="small_front",
    )(xsm, nmix, win_bf, s0r, s0i, buf0, buf1, cwt, bc, c2, abr, abi, nconv)
    u_meta = u_sm[ns:]
    z_meta8 = z_sm[ns + CHUNK - 8:]
    new_conv_s = jnp.stack([buf1, z_sm[:ns]], axis=1)[None]
    new_re_s = sr_s.reshape(1, ns, g, n)
    new_im_s = si_s.reshape(1, ns, g, n)

    wch = CHUNK * LANE
    u4c, u_tok, ycn_p, ztail = pl.pallas_call(
        _front_kernel,
        grid=(nb // nbt, seq // tt),
        in_specs=[pl.BlockSpec((nbt, tt, d), lambda b, i: (b, i, 0)),
                  _full((1, d)), _full((d, 4 * cw)), _full((8, cw)), _full((3, cw)), _full((1, cw))],
        out_specs=(pl.BlockSpec((nj, nbt, tt // CHUNK, wch), lambda b, i: (0, b, i, 0)),
                   pl.BlockSpec((nbt, tt, cw), lambda b, i: (b, i, 0)),
                   pl.BlockSpec((nbt, tt, cw), lambda b, i: (b, i, 0)),
                   pl.BlockSpec((nbt, 8, cw), lambda b, i: (b, 0, 0))),
        out_shape=(jax.ShapeDtypeStruct((nj, nb, n_chunks, wch), BF16),
                   jax.ShapeDtypeStruct((nb, seq, cw), BF16),
                   jax.ShapeDtypeStruct((nb, seq, cw), BF16),
                   jax.ShapeDtypeStruct((nb, 8, cw), F32)),
        scratch_shapes=[pltpu.VMEM((nbt, tt + 8, cw), F32), pltpu.VMEM((nj, nbt, tt, LANE), F32)],
        compiler_params=_params(("arbitrary", "arbitrary"), 52),
        name="front",
    )(x_prompt, nmix, win_bf, z_meta8, cwt, nconv)
    new_conv_p = ztail[:, 6:8, :][None]

    cc = n_chunks // 2
    um = u_meta.reshape(CHUNK, nj, LANE).transpose(1, 0, 2).reshape(nj, 1, wch)
    um = jnp.broadcast_to(um, (nj, 8 * nb, wch)).astype(BF16)
    gpt = g // nj
    nstj = 2 * gpt * n
    hch = cw // g
    y4c, s_last = pl.pallas_call(
        _ssm_kernel,
        grid=(nj, n_chunks // cc),
        in_specs=[pl.BlockSpec((1, nb, cc, wch), lambda j, t: (j, 0, t, 0)),
                  pl.BlockSpec((1, 8 * nb, wch), lambda j, t: (j, 0, 0)),
                  pl.BlockSpec((1, (CHUNK + 1) * LANE, hch), lambda j, t: (j, 0, 0)),
                  pl.BlockSpec((1, wch, 2 * n), lambda j, t: (j, 0, 0)),
                  pl.BlockSpec((1, 2 * n, wch), lambda j, t: (j, 0, 0)),
                  pl.BlockSpec((1, 2, nstj // 2), lambda j, t: (j, 0, 0))],
        out_specs=(pl.BlockSpec((1, nb, cc, wch), lambda j, t: (j, 0, t, 0)),
                   pl.BlockSpec((1, nb, 1, nstj), lambda j, t: (j, 0, 0, 0))),
        out_shape=(jax.ShapeDtypeStruct((nj, nb, n_chunks, wch), F32),
                   jax.ShapeDtypeStruct((nj, nb, 1, nstj), F32)),
        scratch_shapes=[pltpu.VMEM((nb, 1, nstj), F32), pltpu.VMEM((nb, cc, nstj), F32),
                        pltpu.VMEM((nb, cc, nstj), F32),
                        pltpu.VMEM((wch, nstj), BF16), pltpu.VMEM((nstj, wch), BF16),
                        pltpu.VMEM((CHUNK // 2, 2 * LANE, 2 * LANE), BF16)],
        compiler_params=_params(("parallel", "arbitrary"), 56),
        name="ssm",
    )(u4c, um, kc, pc, rc, a16)
    sl = s_last.reshape(nj, nb, 2, gpt, n)
    new_re_p = sl[:, :, 0].transpose(1, 0, 2, 3).reshape(1, nb, g, n)
    new_im_p = sl[:, :, 1].transpose(1, 0, 2, 3).reshape(1, nb, g, n)

    dsk = ssm_d[0].reshape(1, cw)
    wglu_bf = w_glu[0].astype(BF16)
    bglu = b_glu[0].reshape(1, cw)
    nssm = norm_out_ssm[0].reshape(1, cw)
    wout_bf = w_out[0].astype(BF16)
    nffn = norm_ffn[0].reshape(1, d)
    wr_pad = jnp.zeros((d, LANE), F32).at[:, :ne].set(w_router[0])
    wr_hi = wr_pad.astype(BF16)
    wr_lo = (wr_pad - wr_hi.astype(F32)).astype(BF16)
    br = b_router[0].reshape(ne, 1)
    mix_w = (dsk, wglu_bf, bglu, nssm, wout_bf, nffn, wr_hi, wr_lo, br)
    mix_w_specs = [_full((1, cw)), _full((cw, cw)), _full((1, cw)), _full((1, cw)), _full((2 * cw, d)),
                   _full((1, d)), _full((d, LANE)), _full((d, LANE)), _full((ne, 1))]
    mix_out_shape = (jax.ShapeDtypeStruct((ta, d), F32), jax.ShapeDtypeStruct((ta, d), BF16),
                     jax.ShapeDtypeStruct((TOP_K, ta), I32), jax.ShapeDtypeStruct((TOP_K, ta), F32),
                     jax.ShapeDtypeStruct((n_tiles, ne, 1), I32))
    tpm = rows_p // TOK_TILE
    nbg = nb // nbt
    x1_all, hf_all, idx_all, gate_all, cnt = pl.pallas_call(
        _mix_kernel_prompt,
        grid=(seq // tm, nbg),
        in_specs=[pl.BlockSpec((nbt, tm, d), lambda i, b: (b, i, 0)),
                  pl.BlockSpec((nj, nbt, tm // CHUNK, wch), lambda i, b: (0, b, i, 0)),
                  pl.BlockSpec((nbt, tm, cw), lambda i, b: (b, i, 0)),
                  pl.BlockSpec((nbt, tm, cw), lambda i, b: (b, i, 0))] + mix_w_specs,
        out_specs=(pl.BlockSpec((rows_p, d), lambda i, b: (i * nbg + b, 0)),
                   pl.BlockSpec((rows_p, d), lambda i, b: (i * nbg + b, 0)),
                   pl.BlockSpec((TOP_K, rows_p), lambda i, b: (0, i * nbg + b)),
                   pl.BlockSpec((TOP_K, rows_p), lambda i, b: (0, i * nbg + b)),
                   pl.BlockSpec((tpm, ne, 1), lambda i, b: (i * nbg + b, 0, 0))),
        out_shape=mix_out_shape,
        scratch_shapes=[pltpu.VMEM((nj, nbt, tm, LANE), F32)],
        compiler_params=_params(("parallel", "parallel"), 52),
        name="mix_prompt",
    )(x_prompt, y4c, u_tok, ycn_p, *mix_w)

    last = n_tiles - 1
    any_spec = pl.BlockSpec(memory_space=pl.ANY)
    x1_all, hf_all, idx_all, gate_all, cnt = pl.pallas_call(
        _mix_kernel_sample,
        grid=(1,),
        in_specs=[_full((1, ns, d)), _full((ns, cw)), _full((1, ns, cw)),
                  _full((1, ns, cw))] + mix_w_specs + [any_spec] * 5,
        out_specs=(pl.BlockSpec((TOK_TILE, d), lambda i: (last, 0)),
                   pl.BlockSpec((TOK_TILE, d), lambda i: (last, 0)),
                   pl.BlockSpec((TOP_K, TOK_TILE), lambda i: (0, last)),
                   pl.BlockSpec((TOP_K, TOK_TILE), lambda i: (0, last)),
                   pl.BlockSpec((1, ne, 1), lambda i: (last, 0, 0))),
        out_shape=mix_out_shape,
        input_output_aliases={13: 0, 14: 1, 15: 2, 16: 3, 17: 4},
        compiler_params=_params(("arbitrary",), 32),
        name="mix_sample",
    )(x_sample.reshape(1, ns, d), y_s, u_sm[:ns].astype(BF16).reshape(1, ns, cw),
      ycn_s.reshape(1, ns, cw), *mix_w, x1_all, hf_all, idx_all, gate_all, cnt)

    bm = MOE_ROWS
    cnt2 = cnt.reshape(n_tiles, ne)
    before = jnp.cumsum(cnt2, axis=0) - cnt2
    count = jnp.sum(cnt2, axis=0)
    padded = ((count + WIN + bm - 1) // bm) * bm
    pend = jnp.cumsum(padded)
    pstart = pend - padded
    phase = before % SUBLANE
    span = jnp.where(cnt2 > 0, phase + cnt2, 0)
    gstart = (pstart[None, :] + before - phase).astype(I32).reshape(-1)
    nwin = ((span + WIN - 1) // WIN).astype(I32)
    reg8 = ((span + SUBLANE - 1) // SUBLANE) * SUBLANE
    loff = (jnp.cumsum(reg8, axis=1) - reg8).astype(I32)
    tail = jnp.where(span % SUBLANE != 0, loff + (span // SUBLANE) * SUBLANE, -1).astype(I32)
    twin = jnp.sum(nwin, axis=1).astype(I32)
    n_blocks = (ta * TOP_K + ne * (WIN + bm - 1) + bm - 1) // bm
    cap = n_blocks * bm
    blk0 = jnp.arange(n_blocks, dtype=I32) * bm
    blk_e = jnp.minimum(jnp.sum((pend[None, :] <= blk0[:, None]).astype(I32), axis=1), ne - 1)
    e_ar = jnp.arange(ne, dtype=I32)
    blk_hot = blk_e[:, None] == e_ar[None, :]

    def _of_block(per_expert):
        return jnp.sum(jnp.where(blk_hot, per_expert[None, :], 0), axis=1)

    blk_valid = jnp.clip(_of_block(count) - (blk0 - _of_block(pstart)), 0, bm).astype(I32)
    has = count > 0
    later = jnp.logical_and(e_ar[None, :] > e_ar[:, None], has[None, :])
    nxt_e = jnp.min(jnp.where(later, e_ar[None, :], ne), axis=1)
    nxt_e = jnp.where(nxt_e < ne, nxt_e, -1)
    ordinal = jnp.cumsum(has.astype(I32)) - 1
    is_first = jnp.logical_and(blk_valid > 0, blk0 == _of_block(pstart))
    blk_first = jnp.where(is_first, jnp.where(_of_block(ordinal) == 0, 2, 1), 0).astype(I32)
    blk_next = _of_block(nxt_e).astype(I32)
    blk_slot = (_of_block(ordinal) % 2).astype(I32)
    nvalid = jnp.full((1,), tall, I32)
    loff_al = (WIN * (jnp.cumsum(nwin, axis=1) - nwin)).astype(I32)
    wcum = jnp.cumsum(nwin, axis=1)
    wslot = jnp.arange(MAX_WINDOWS, dtype=I32)
    w_hot = jnp.logical_and(wslot[None, :, None] >= (wcum - nwin)[:, None, :],
                            wslot[None, :, None] < wcum[:, None, :])

    def _of_window(per_tile_expert):
        return jnp.sum(jnp.where(w_hot, per_tile_expert[:, None, :], 0), axis=2)

    w_in_run = wslot[None, :] - _of_window(wcum - nwin)
    w_hbm = (_of_window(gstart.reshape(n_tiles, ne)) + WIN * w_in_run).astype(I32).reshape(-1)
    w_buf = (_of_window(loff) + WIN * w_in_run).astype(I32).reshape(-1)
    nrun_d = _dispatch_run_rows(ne)
    zero_grp, spare_grp = nrun_d - 2 * SUBLANE, nrun_d - SUBLANE
    m_lo = jnp.where(nwin > 0, loff, spare_grp).astype(I32).reshape(-1)
    m_tg = jnp.where(jnp.logical_and(nwin > 0, tail >= 0), tail, zero_grp).astype(I32).reshape(-1)
    m_keep = (nwin == 0).astype(I32).reshape(-1)
    tables_d = (w_hbm, w_buf, m_lo, m_tg, m_keep, twin, nvalid)
    tables_c = (w_hbm, twin, nvalid)
    rowoff_d = (loff + phase).astype(I32).reshape(n_tiles, ne, 1)
    rowoff_c = (loff_al + phase).astype(I32).reshape(n_tiles, ne, 1)
    rgn_d = jnp.stack([loff, loff + reg8], axis=1).astype(I32)
    rgn_c = jnp.stack([loff_al, loff_al + WIN * nwin], axis=1).astype(I32)
    xw = d // 2 + GATE_COLS

    xs = pl.pallas_call(
        _dispatch_kernel,
        grid_spec=pltpu.PrefetchScalarGridSpec(
            num_scalar_prefetch=len(tables_d),
            grid=(n_tiles,),
            in_specs=[pl.BlockSpec((TOK_TILE, d), lambda i, *_: (i, 0)),
                      pl.BlockSpec((TOP_K, TOK_TILE), lambda i, *_: (0, i)),
                      pl.BlockSpec((TOP_K, TOK_TILE), lambda i, *_: (0, i)),
                      pl.BlockSpec((1, ne, 1), lambda i, *_: (i, 0, 0)),
                      pl.BlockSpec((1, 2, ne), lambda i, *_: (i, 0, 0))],
            out_specs=pl.BlockSpec(memory_space=pl.ANY),
            scratch_shapes=[pltpu.VMEM((2, _dispatch_run_rows(ne), xw), U32),
                            pltpu.VMEM((ne, SUBLANE, xw), U32), pltpu.SemaphoreType.DMA((2,))]),
        out_shape=jax.ShapeDtypeStruct((cap, xw), U32),
        compiler_params=_params(("arbitrary",), 40),
        name="dispatch",
    )(*tables_d, hf_all, idx_all, gate_all, rowoff_d, rgn_d)

    yb = pl.pallas_call(
        _moe_kernel,
        grid_spec=pltpu.PrefetchScalarGridSpec(
            num_scalar_prefetch=5,
            grid=(n_blocks,),
            in_specs=[pl.BlockSpec((bm, xw), lambda i, *_: (i, 0)),
                      pl.BlockSpec((1, 1, 2 * dff), lambda i, be, *_: (be[i], 0, 0)),
                      pl.BlockSpec((1, 1, d), lambda i, be, *_: (be[i], 0, 0)),
                      pl.BlockSpec(memory_space=pl.ANY), pl.BlockSpec(memory_space=pl.ANY)],
            out_specs=pl.BlockSpec((bm, d // 2), lambda i, *_: (i, 0)),
            scratch_shapes=[pltpu.VMEM((2, d, 2 * dff), F32), pltpu.VMEM((2, dff, d), F32),
                            pltpu.VMEM((d, 2 * dff), BF16), pltpu.VMEM((dff, d), BF16),
                            pltpu.SemaphoreType.DMA((2, 2))]),
        out_shape=jax.ShapeDtypeStruct((cap, d // 2), U32),
        compiler_params=_params(("arbitrary",), 58),
        name="moe",
    )(blk_e, blk_valid, blk_first, blk_next, blk_slot, xs, b_gate_up[0].reshape(ne, 1, 2 * dff),
      b_down[0].reshape(ne, 1, d), w_gate_up[0], w_down[0])

    nfin = norm_final.reshape(1, d)
    nbh = TOK_TILE // tm

    def _tile_of(s):
        return (s + n_tiles - 1) % n_tiles

    tiles_per_time = nbg * tpm

    def _yp_index(s, *_):
        t = jnp.maximum(s - 1, 0)
        return (t % tiles_per_time, t // tiles_per_time, 0)

    y_p, y_sm = pl.pallas_call(
        _combine_kernel,
        grid_spec=pltpu.PrefetchScalarGridSpec(
            num_scalar_prefetch=len(tables_c),
            grid=(n_tiles,),
            in_specs=[pl.BlockSpec((TOK_TILE, d), lambda s, *_: (_tile_of(s), 0)),
                      pl.BlockSpec((TOP_K, TOK_TILE), lambda s, *_: (0, _tile_of(s))),
                      pl.BlockSpec((1, ne, 1), lambda s, *_: (_tile_of(s), 0, 0)),
                      pl.BlockSpec((1, 2, ne), lambda s, *_: (_tile_of(s), 0, 0)),
                      pl.BlockSpec((1, d), lambda s, *_: (0, 0)),
                      pl.BlockSpec(memory_space=pl.ANY)],
            out_specs=(pl.BlockSpec((nbh, tm, d), _yp_index),
                       pl.BlockSpec((ns, d), lambda s, *_: (0, 0))),
            scratch_shapes=[pltpu.VMEM((2, _combine_run_rows(ne), d // 2), U32),
                            pltpu.SemaphoreType.DMA((2,))]),
        out_shape=(jax.ShapeDtypeStruct((nb, seq, d), F32), jax.ShapeDtypeStruct((ns, d), F32)),
        compiler_params=_params(("arbitrary",), 48),
        name="combine",
    )(*tables_c, x1_all, idx_all, rowoff_c, rgn_c, nfin, yb)

    return (y_p, y_sm.reshape(ns, 1, d), new_re_p, new_im_p, new_conv_p,
            new_re_s, new_im_s, new_conv_s)
```

```python
import math

import jax
import jax.numpy as jnp
from jax import lax
from jax.experimental import pallas as pl
from jax.experimental.pallas import tpu as pltpu

F32 = jnp.float32
BF16 = jnp.bfloat16
U32 = jnp.uint32
I32 = jnp.int32
EPS = 1e-5
CHUNK = 16
LANE = 128
TOP_K = 4
SWIGLU_LIMIT = 7.0
SWIGLU_ALPHA = 1.702
MOE_ROWS = 1024
MOE_SUB = 256
TOK_TILE = 256
WIN = 32
SUBLANE = 8
GATE_COLS = LANE


MAX_WINDOWS = 72


def _dispatch_run_rows(ne):
    return TOP_K * TOK_TILE + ne * 2 * (SUBLANE - 1) + WIN


def _combine_run_rows(ne):
    return -(-(TOP_K * TOK_TILE + ne * (SUBLANE - 1 + WIN - 1)) // WIN) * WIN
HI_MASK = 0xFFFF0000
MIB = 1024 * 1024


def _rms(x, g):
    return x * lax.rsqrt(jnp.mean(x * x, axis=-1, keepdims=True) + EPS) * g


def _gelu_tanh(x):
    c = math.sqrt(2.0 / math.pi)
    return 0.5 * x * (1.0 + jnp.tanh(c * (x + 0.044715 * (x * x * x))))


def _params(sem, vmem_mib):
    return pltpu.CompilerParams(dimension_semantics=sem, vmem_limit_bytes=vmem_mib * MIB)


def _pack_pairs(a, b):
    return (pltpu.bitcast(a, U32) >> 16) | (pltpu.bitcast(b, U32) & jnp.uint32(HI_MASK))


def _unpack_pairs(w):
    lo = pltpu.bitcast(w << 16, F32)
    hi = pltpu.bitcast(w & jnp.uint32(HI_MASK), F32)
    return jnp.concatenate([lo, hi], axis=-1).astype(BF16)


def _iota2(shape, axis):
    return lax.broadcasted_iota(I32, shape, axis)


def _expand_cols(compact, reps_log2, n_log2):
    q = _iota2((compact.shape[1], 2 << (reps_log2 + n_log2)), 0)
    c = _iota2((compact.shape[1], 2 << (reps_log2 + n_log2)), 1)
    nmask = (1 << n_log2) - 1
    same = jnp.logical_and((q >> n_log2) == (c >> (reps_log2 + n_log2)), (q & nmask) == (c & nmask))
    return jnp.dot(compact, jnp.where(same, 1.0, 0.0).astype(BF16), preferred_element_type=F32)


def _expand_rows(compact, reps_log2, n_log2):
    r = _iota2((2 << (reps_log2 + n_log2), compact.shape[0]), 0)
    q = _iota2((2 << (reps_log2 + n_log2), compact.shape[0]), 1)
    nmask = (1 << n_log2) - 1
    same = jnp.logical_and((r >> (reps_log2 + n_log2)) == (q >> n_log2), (r & nmask) == (q & nmask))
    return jnp.dot(jnp.where(same, 1.0, 0.0).astype(BF16), compact, preferred_element_type=F32)


def _group_mask(shape, row_shift, col_shift, ngroups):
    r = _iota2(shape, 0)
    c = _iota2(shape, 1)
    return ((r >> row_shift) & (ngroups - 1)) == ((c >> col_shift) & (ngroups - 1))


def _small_front_kernel(x_ref, nmix_ref, win_ref, s0r_ref, s0i_ref, b0_ref, b1_ref, cw_ref,
                        bc_ref, cc_ref, abr_ref, abi_ref, nconv_ref,
                        u_ref, z_ref, y_ref, ycn_ref, sr_ref, si_ref, bdb_ref, cm_ref):
    ns, nst = s0r_ref.shape
    cw = u_ref.shape[1]
    n = bc_ref.shape[1] // 2
    nlog = n.bit_length() - 1
    glog = (nst // n).bit_length() - 1
    hlog = (cw >> glog).bit_length() - 1
    bdb_ref[...] = jnp.where(_group_mask(bdb_ref.shape, hlog, nlog, 1 << glog),
                             _expand_cols(bc_ref[...], glog, nlog), 0.0).astype(BF16)
    cm_ref[...] = jnp.where(_group_mask(cm_ref.shape, nlog, hlog, 1 << glog),
                            _expand_rows(cc_ref[...], glog, nlog), 0.0).astype(BF16)
    h = _rms(x_ref[...], nmix_ref[...]).astype(BF16)
    proj = jnp.dot(h, win_ref[...], preferred_element_type=F32)
    u = proj[:, 0:cw]
    zc = proj[:, cw:2 * cw]
    gb = proj[:, 2 * cw:3 * cw]
    gc = proj[:, 3 * cw:4 * cw]
    z = gc * zc
    u_ref[...] = u
    z_ref[...] = z
    bu = jnp.dot(u[:ns].astype(BF16), bdb_ref[...], preferred_element_type=F32)
    abr = abr_ref[...]
    abi = abi_ref[...]
    s0r = s0r_ref[...]
    s0i = s0i_ref[...]
    sr = abr * s0r - abi * s0i + bu[:, :nst]
    si = abr * s0i + abi * s0r + bu[:, nst:]
    sr_ref[...] = sr
    si_ref[...] = si
    scat = jnp.concatenate([sr, si], axis=-1).astype(BF16)
    y_ref[...] = jnp.dot(scat, cm_ref[...], preferred_element_type=F32)
    conv = cw_ref[0:1, :] * b0_ref[...] + cw_ref[1:2, :] * b1_ref[...] + cw_ref[2:3, :] * z[:ns]
    ycn_ref[...] = _rms(gb[:ns] * conv, nconv_ref[...]).astype(BF16)


def _front_kernel(x_ref, nmix_ref, win_ref, zm_ref, cw_ref, nconv_ref,
                  uc_ref, ut_ref, ycn_ref, zt_ref, zbuf, ubuf):
    i = pl.program_id(1)
    nb, tt, d = x_ref.shape
    cw = ycn_ref.shape[2]
    rows = nb * tt
    ncz = tt // CHUNK

    @pl.when(i == 0)
    def _():
        zbuf[:, 0:8, :] = jnp.broadcast_to(zm_ref[...][None], (nb, 8, cw))

    h = _rms(x_ref[...].reshape(rows, d), nmix_ref[...]).astype(BF16)
    u = jnp.dot(h, win_ref[:, 0:cw], preferred_element_type=F32)
    ut_ref[...] = u.astype(BF16).reshape(nb, tt, cw)
    for j in range(cw // LANE):
        ubuf[j] = u[:, j * LANE:(j + 1) * LANE].reshape(nb, tt, LANE)
    for s in range(CHUNK):
        for j in range(cw // LANE):
            piece = ubuf[j, :, pl.ds(s, ncz, stride=CHUNK), :]
            uc_ref[j, :, :, s * LANE:(s + 1) * LANE] = piece.astype(BF16)
    zc = jnp.dot(h, win_ref[:, cw:2 * cw], preferred_element_type=F32)
    gc = jnp.dot(h, win_ref[:, 3 * cw:4 * cw], preferred_element_type=F32)
    z3 = (gc * zc).reshape(nb, tt, cw)
    zbuf[:, 8:8 + tt, :] = z3
    z1 = zbuf[:, 7:7 + tt, :]
    z2 = zbuf[:, 6:6 + tt, :]
    conv = cw_ref[0:1, :] * z2 + cw_ref[1:2, :] * z1 + cw_ref[2:3, :] * z3
    gb = jnp.dot(h, win_ref[:, 2 * cw:3 * cw], preferred_element_type=F32)
    yc = gb * conv.reshape(rows, cw)
    ycn_ref[...] = _rms(yc, nconv_ref[...]).astype(BF16).reshape(nb, tt, cw)
    tail = zbuf[:, tt:tt + 8, :]
    zt_ref[...] = tail
    zbuf[:, 0:8, :] = tail


def _ssm_kernel(u_ref, um_ref, kc_ref, pc_ref, rc_ref, a16_ref, y_ref, sl_ref,
                s_carry, ds_ref, sp_ref, p_s, r_s, t_s):
    th = pl.program_id(1)
    _, nb, cc, w = u_ref.shape
    nst = p_s.shape[1]
    half = nst // 2
    rows = nb * cc
    blk = 2 * LANE
    u = u_ref[0].reshape(rows, w)

    @pl.when(th == 0)
    def _():
        hch = kc_ref.shape[2]
        gpt = LANE // hch
        hlog = hch.bit_length() - 1
        glog = gpt.bit_length() - 1
        nlog = (pc_ref.shape[2] // 2).bit_length() - 1
        p_s[...] = jnp.where(_group_mask(p_s.shape, hlog, nlog, gpt),
                             _expand_cols(pc_ref[0], glog, nlog), 0.0).astype(BF16)
        r_s[...] = jnp.where(_group_mask(r_s.shape, nlog, hlog, gpt),
                             _expand_rows(rc_ref[0], glog, nlog), 0.0).astype(BF16)
        nlag = kc_ref.shape[1] // LANE
        o = _iota2((hch, LANE), 0)
        c = _iota2((hch, LANE), 1)
        spread = jnp.where((c & (hch - 1)) == o, 1.0, 0.0).astype(BF16)
        lagm = jnp.dot(kc_ref[0], spread, preferred_element_type=F32)
        r = _iota2(lagm.shape, 0)
        c = _iota2(lagm.shape, 1)
        lagm = jnp.where(((r >> hlog) & (gpt - 1)) == (c >> hlog), lagm, 0.0).astype(BF16)
        for dlt in range(nlag // 2):
            b0 = lagm[(2 * dlt) * LANE:(2 * dlt + 1) * LANE]
            b1 = lagm[(2 * dlt + 1) * LANE:(2 * dlt + 2) * LANE]
            b2 = lagm[(2 * dlt + 2) * LANE:(2 * dlt + 3) * LANE]
            t_s[dlt, 0:LANE, 0:LANE] = b1
            t_s[dlt, 0:LANE, LANE:blk] = b2
            t_s[dlt, LANE:blk, 0:LANE] = b0
            t_s[dlt, LANE:blk, LANE:blk] = b1
        ds_ref[:, 0:8, :] = jnp.dot(um_ref[0], p_s[...], preferred_element_type=F32).reshape(nb, 8, nst)
        s_carry[...] = ds_ref[:, 0:1, :]

    ds_ref[...] = jnp.dot(u, p_s[...], preferred_element_type=F32).reshape(nb, cc, nst)
    ar = a16_ref[0, 0:1, :].reshape(1, 1, half)
    ai = a16_ref[0, 1:2, :].reshape(1, 1, half)
    sr = s_carry[:, :, 0:half]
    si = s_carry[:, :, half:nst]
    for c in range(cc):
        sp_ref[:, c:c + 1, 0:half] = sr
        sp_ref[:, c:c + 1, half:nst] = si
        dr = ds_ref[:, c:c + 1, 0:half]
        di = ds_ref[:, c:c + 1, half:nst]
        sr, si = ar * sr - ai * si + dr, ar * si + ai * sr + di
    s_carry[:, :, 0:half] = sr
    s_carry[:, :, half:nst] = si
    sl_ref[0, :, :, 0:half] = sr
    sl_ref[0, :, :, half:nst] = si

    sp = sp_ref[...].reshape(rows, nst).astype(BF16)
    for tb in range(w // blk):
        acc = jnp.dot(sp, r_s[:, tb * blk:(tb + 1) * blk], preferred_element_type=F32)
        for sb in range(tb + 1):
            acc = acc + jnp.dot(u[:, sb * blk:(sb + 1) * blk], t_s[tb - sb],
                                preferred_element_type=F32)
        y_ref[0, :, :, tb * blk:(tb + 1) * blk] = acc.reshape(nb, cc, blk)


def _mix_rows(x, yssm, ut, ycn, dsk_ref, wglu_ref, bglu_ref, nssm_ref, wout_ref,
              nffn_ref, wrh_ref, wrl_ref, br_ref):
    ne = br_ref.shape[0]
    y = _gelu_tanh(yssm + dsk_ref[...] * ut.astype(F32))
    glu = jnp.dot(y.astype(BF16), wglu_ref[...], preferred_element_type=F32) + bglu_ref[...]
    o = y * jax.nn.sigmoid(glu)
    ysn = _rms(o, nssm_ref[...]).astype(BF16)
    mix = jnp.concatenate([ysn, ycn], axis=-1)
    x1 = x + jnp.dot(mix, wout_ref[...], preferred_element_type=F32)
    hf = _rms(x1, nffn_ref[...])
    hf_hi = hf.astype(BF16)
    hf_lo = (hf - hf_hi.astype(F32)).astype(BF16)
    logits = (jnp.dot(hf_hi, wrh_ref[...], preferred_element_type=F32)
              + jnp.dot(hf_lo, wrh_ref[...], preferred_element_type=F32)
              + jnp.dot(hf_hi, wrl_ref[...], preferred_element_type=F32))
    lt = logits.T[0:ne, :] + br_ref[...]
    iota = lax.broadcasted_iota(I32, lt.shape, 0)
    vals, idxs = [], []
    sel = jnp.zeros(lt.shape, F32)
    for _ in range(TOP_K):
        m = jnp.max(lt, axis=0, keepdims=True)
        ik = jnp.min(jnp.where(lt == m, iota, ne), axis=0, keepdims=True)
        vals.append(m)
        idxs.append(ik)
        hit = iota == ik
        sel = sel + jnp.where(hit, 1.0, 0.0)
        lt = jnp.where(hit, -jnp.inf, lt)
    es = [jnp.exp(v - vals[0]) for v in vals]
    tot = es[0] + es[1] + es[2] + es[3]
    idx = jnp.concatenate(idxs, axis=0)
    gates = jnp.concatenate([e / tot for e in es], axis=0)
    return x1, hf_hi, idx, gates, sel


def _mix_kernel_prompt(*refs):
    x_ref, yc_ref = refs[0], refs[1]
    x1_ref, hf_ref, idx_ref, gate_ref, cnt_ref, ybuf = refs[13:]
    nj, nb, ncz, _ = yc_ref.shape
    for s in range(CHUNK):
        for j in range(nj):
            ybuf[j, :, pl.ds(s, ncz, stride=CHUNK), :] = yc_ref[j, :, :, s * LANE:(s + 1) * LANE]
    ut_ref, ycn_ref = refs[2], refs[3]
    tt, d = x_ref.shape[1], x_ref.shape[2]
    cw = nj * LANE
    nbc = TOK_TILE // tt
    for t in range(cnt_ref.shape[0]):
        b0, r0 = t * nbc, t * TOK_TILE
        yssm = jnp.concatenate([ybuf[j, b0:b0 + nbc].reshape(TOK_TILE, LANE) for j in range(nj)], axis=-1)
        x1, hf, idx, gates, sel = _mix_rows(
            x_ref[b0:b0 + nbc].reshape(TOK_TILE, d), yssm, ut_ref[b0:b0 + nbc].reshape(TOK_TILE, cw),
            ycn_ref[b0:b0 + nbc].reshape(TOK_TILE, cw), *refs[4:13])
        x1_ref[r0:r0 + TOK_TILE, :] = x1
        hf_ref[r0:r0 + TOK_TILE, :] = hf
        idx_ref[:, r0:r0 + TOK_TILE] = idx
        gate_ref[:, r0:r0 + TOK_TILE] = gates
        cnt_ref[t] = jnp.sum(sel, axis=1, keepdims=True).astype(I32)


def _mix_kernel_sample(*refs):
    x1, hf, idx, gates, sel = _mix_rows(refs[0][0], refs[1][...], refs[2][0], refs[3][0], *refs[4:13])
    x1_ref, hf_ref, idx_ref, gate_ref, cnt_ref = refs[18:]
    ns = x1.shape[0]
    x1_ref[...] = jnp.zeros(x1_ref.shape, x1_ref.dtype)
    hf_ref[...] = jnp.zeros(hf_ref.shape, hf_ref.dtype)
    idx_ref[...] = jnp.zeros(idx_ref.shape, idx_ref.dtype)
    gate_ref[...] = jnp.zeros(gate_ref.shape, gate_ref.dtype)
    x1_ref[0:ns, :] = x1
    hf_ref[0:ns, :] = hf
    idx_ref[:, 0:ns] = idx
    gate_ref[:, 0:ns] = gates
    cnt_ref[0] = jnp.sum(sel, axis=1, keepdims=True).astype(I32)


def _split_bf16(x, parts):
    out = []
    for _ in range(parts - 1):
        p = x.astype(BF16)
        out.append(p)
        x = x - p.astype(F32)
    out.append(x.astype(BF16))
    return out


def _run_onehot(idx_ref, loff_ref, rgn_ref, tile, n_valid, nrun, gate_ref=None):
    ne = loff_ref.shape[1]
    tt = idx_ref.shape[1]
    e_iota = lax.broadcasted_iota(I32, (ne, tt), 0)
    tok = tile * tt + lax.broadcasted_iota(I32, (1, tt), 1)
    valid = tok < n_valid
    hits = [jnp.logical_and(e_iota == idx_ref[k:k + 1, :], valid) for k in range(TOP_K)]
    sel = jnp.zeros((ne, tt), F32)
    for h in hits:
        sel = sel + jnp.where(h, 1.0, 0.0)
    before = lax.broadcasted_iota(I32, (tt, tt), 0) < lax.broadcasted_iota(I32, (tt, tt), 1)
    tri = jnp.where(before, 1.0, 0.0).astype(BF16)
    base = jnp.dot(sel.astype(BF16), tri, preferred_element_type=F32) + loff_ref[0].astype(F32)
    base = jnp.where(sel > 0.0, base + 1.0, 0.0)
    b_hi = jnp.floor(base * (1.0 / 256.0))
    b_lo = base - 256.0 * b_hi
    r_i = lax.broadcasted_iota(I32, (nrun, ne), 0)
    own = jnp.logical_and(r_i >= rgn_ref[0, 0:1, :], r_i < rgn_ref[0, 1:2, :])
    own_bf = jnp.where(own, 1.0, 0.0).astype(BF16)
    want = (256.0 * jnp.dot(own_bf, b_hi.astype(BF16), preferred_element_type=F32)
            + jnp.dot(own_bf, b_lo.astype(BF16), preferred_element_type=F32))
    r_f = (lax.broadcasted_iota(I32, (nrun, tt), 0) + 1).astype(F32)
    smat = jnp.where(want == r_f, 1.0, 0.0).astype(BF16)
    if gate_ref is None:
        return smat
    gate_e = jnp.zeros((ne, tt), F32)
    for k, h in enumerate(hits):
        gate_e = gate_e + jnp.where(h, gate_ref[k:k + 1, :], 0.0)
    nparts = 3
    pieces = jnp.concatenate(_split_bf16(gate_e, nparts), axis=0)
    per = lax.dot_general(smat, pieces, (((1,), (1,)), ((), ())), preferred_element_type=F32)
    lo3 = jnp.concatenate([rgn_ref[0, 0:1, :]] * nparts, axis=1)
    hi3 = jnp.concatenate([rgn_ref[0, 1:2, :]] * nparts, axis=1)
    r_i3 = lax.broadcasted_iota(I32, (nrun, nparts * ne), 0)
    own3 = jnp.logical_and(r_i3 >= lo3, r_i3 < hi3)
    gcol = jnp.sum(jnp.where(own3, per, 0.0), axis=1, keepdims=True)
    return smat, gcol


def _window_copy(buf, slot, hbm, lo, g, sem, to_hbm, rows, align):
    src = buf.at[slot, pl.ds(pl.multiple_of(lo, align), rows)]
    dst = hbm.at[pl.ds(pl.multiple_of(g, align), rows)]
    if to_hbm:
        return pltpu.make_async_copy(src, dst, sem.at[slot])
    return pltpu.make_async_copy(dst, src, sem.at[slot])


def _start_windows(wg_ref, wl_ref, tw_ref, tile, slots, buf, slot, hbm, sem, to_hbm, rows, align):
    def per_window(w, c):
        k = tile * slots + w
        lo = w * rows if wl_ref is None else wl_ref[k]
        _window_copy(buf, slot, hbm, lo, wg_ref[k], sem, to_hbm, rows, align).start()
        return c

    lax.fori_loop(0, tw_ref[tile], per_window, 0)


def _wait_windows(count, buf, slot, hbm, sem, to_hbm, rows, align):
    batch = 8

    def many(w, c):
        _window_copy(buf, slot, hbm, 0, 0, sem, to_hbm, batch * rows, align).wait()
        return c

    def one(w, c):
        _window_copy(buf, slot, hbm, 0, 0, sem, to_hbm, rows, align).wait()
        return c

    lax.fori_loop(0, count // batch, many, 0)
    lax.fori_loop(0, count % batch, one, 0)


def _dispatch_kernel(wg_ref, wl_ref, mlo_ref, mtg_ref, keep_ref, tw_ref, nv_ref,
                     hf_ref, idx_ref, gate_ref, loff_ref, rgn_ref, xs_ref, buf, carry, sem):
    i = pl.program_id(0)
    nt = pl.num_programs(0)
    ne = loff_ref.shape[1]
    tt, d = hf_ref.shape
    nrun = buf.shape[1]
    slot = i % 2

    @pl.when(i == 0)
    def _():
        carry[...] = jnp.zeros(carry.shape, carry.dtype)

    smat, gcol = _run_onehot(idx_ref, loff_ref, rgn_ref, i, nv_ref[0], nrun, gate_ref)
    xr = jnp.dot(smat, hf_ref[...], preferred_element_type=F32)
    lane0 = lax.broadcasted_iota(I32, (nrun, GATE_COLS), 1) == 0
    buf[slot, :, 0:d // 2] = _pack_pairs(xr[:, :d // 2], xr[:, d // 2:])
    buf[slot, :, d // 2:] = pltpu.bitcast(jnp.where(lane0, gcol, 0.0), U32)

    def merge(e, c):
        k = i * ne + e
        lo = pl.multiple_of(mlo_ref[k], SUBLANE)
        buf[slot, pl.ds(lo, SUBLANE), :] = buf[slot, pl.ds(lo, SUBLANE), :] | carry[e]
        tg = pl.multiple_of(mtg_ref[k], SUBLANE)
        carry[e] = jnp.where(keep_ref[k] > 0, carry[e], buf[slot, pl.ds(tg, SUBLANE), :])
        return c

    lax.fori_loop(0, ne, merge, 0, unroll=4)

    @pl.when(i > 0)
    def _():
        _wait_windows(tw_ref[i - 1], buf, 1 - slot, xs_ref, sem, True, WIN, SUBLANE)

    _start_windows(wg_ref, wl_ref, tw_ref, i, MAX_WINDOWS, buf, slot, xs_ref, sem, True, WIN, SUBLANE)

    @pl.when(i == nt - 1)
    def _():
        _wait_windows(tw_ref[i], buf, slot, xs_ref, sem, True, WIN, SUBLANE)


def _expert_weight_copies(wg_hbm, wd_hbm, wg_f32, wd_f32, sem, e, slot):
    return (pltpu.make_async_copy(wg_hbm.at[e], wg_f32.at[slot], sem.at[0, slot]),
            pltpu.make_async_copy(wd_hbm.at[e], wd_f32.at[slot], sem.at[1, slot]))


def _moe_kernel(be_ref, bv_ref, first_ref, nxt_ref, slot_ref,
                x_ref, bg_ref, bd_ref, wg_hbm, wd_hbm, y_ref,
                wg_f32, wd_f32, wg_bf, wd_bf, sem):
    i = pl.program_id(0)
    e = be_ref[i]
    dff = wd_bf.shape[0]
    bm = x_ref.shape[0]
    sub = MOE_SUB
    nw = x_ref.shape[1] - GATE_COLS

    @pl.when(first_ref[i] > 0)
    def _():
        slot = slot_ref[i]

        @pl.when(first_ref[i] > 1)
        def _():
            for cp in _expert_weight_copies(wg_hbm, wd_hbm, wg_f32, wd_f32, sem, e, slot):
                cp.start()

        for cp in _expert_weight_copies(wg_hbm, wd_hbm, wg_f32, wd_f32, sem, e, slot):
            cp.wait()

        @pl.when(nxt_ref[i] >= 0)
        def _():
            for cp in _expert_weight_copies(wg_hbm, wd_hbm, wg_f32, wd_f32, sem, nxt_ref[i], 1 - slot):
                cp.start()

        wg_bf[...] = wg_f32[slot].astype(BF16)
        wd_bf[...] = wd_f32[slot].astype(BF16)

    def rows(r0, nrows):
        for c in range(nrows // sub):
            lo = r0 + c * sub
            live = lax.broadcasted_iota(I32, (sub, 1), 0) + lo < bv_ref[i]
            x = jnp.where(live, _unpack_pairs(x_ref[lo:lo + sub, 0:nw]), jnp.zeros((), BF16))
            route = jnp.where(live, pltpu.bitcast(x_ref[lo:lo + sub, nw:], F32)[:, 0:1], 0.0)
            gu = jnp.dot(x, wg_bf[...], preferred_element_type=F32) + bg_ref[0]
            gate = jnp.minimum(gu[:, :dff], SWIGLU_LIMIT)
            up = jnp.clip(gu[:, dff:], -SWIGLU_LIMIT, SWIGLU_LIMIT)
            h = gate * jax.nn.sigmoid(SWIGLU_ALPHA * gate) * (up + 1.0)
            y = jnp.dot(h.astype(BF16), wd_bf[...], preferred_element_type=F32) + bd_ref[0]
            yr = (route * y).astype(BF16).astype(F32)
            half = yr.shape[1] // 2
            y_ref[lo:lo + sub, :] = _pack_pairs(yr[:, :half], yr[:, half:])

    nchains = bm // sub
    for live_chains in range(nchains + 1):
        lo_rows, hi_rows = (live_chains - 1) * sub, live_chains * sub

        @pl.when(jnp.logical_and(bv_ref[i] > lo_rows, bv_ref[i] <= hi_rows) if live_chains
                 else bv_ref[i] <= 0)
        def _(used=hi_rows):
            if used:
                rows(0, used)
            if used < bm:
                y_ref[used:bm, :] = jnp.zeros((bm - used, y_ref.shape[1]), y_ref.dtype)


def _combine_tile(step, nt):
    return (step + nt - 1) % nt


def _combine_kernel(wg_ref, tw_ref, nv_ref,
                    x1_ref, idx_ref, loff_ref, rgn_ref, nf_ref, yb_ref,
                    yp_ref, ys_ref, buf, sem):
    s = pl.program_id(0)
    nt = pl.num_programs(0)
    tt, d = x1_ref.shape
    nrun = buf.shape[1]
    tile = _combine_tile(s, nt)
    slot = s % 2

    @pl.when(s == 0)
    def _():
        buf[...] = jnp.zeros(buf.shape, buf.dtype)
        _start_windows(wg_ref, None, tw_ref, tile, MAX_WINDOWS, buf, slot, yb_ref, sem, False,
                       WIN, SUBLANE)

    @pl.when(s + 1 < nt)
    def _():
        _start_windows(wg_ref, None, tw_ref, _combine_tile(s + 1, nt), MAX_WINDOWS, buf, 1 - slot,
                       yb_ref, sem, False, WIN, SUBLANE)

    smat = _run_onehot(idx_ref, loff_ref, rgn_ref, tile, nv_ref[0], nrun)

    _wait_windows(tw_ref[tile], buf, slot, yb_ref, sem, False, WIN, SUBLANE)
    yrun = _unpack_pairs(buf[slot])
    moe = lax.dot_general(smat, yrun, (((0,), (0,)), ((), ())), preferred_element_type=F32)
    out = _rms(x1_ref[...] + moe, nf_ref[...])

    @pl.when(s == 0)
    def _():
        ys_ref[...] = out[0:ys_ref.shape[0], :]

    @pl.when(s > 0)
    def _():
        yp_ref[...] = out.reshape(yp_ref.shape)


def _ssm_matrices(a_re, a_im, log_dt, b_re, b_im, c_re, c_im):
    g, n = a_re.shape
    hch = b_re.shape[2]
    gpt = LANE // hch
    nj = g // gpt
    a = lax.complex(a_re, a_im)
    dta = a * jnp.exp(log_dt)[:, None]
    a_bar = jnp.exp(dta)
    bb = ((a_bar - 1.0) / a)[:, :, None] * lax.complex(b_re, b_im)
    cc = lax.complex(c_re, c_im)
    ks = jnp.arange(CHUNK + 1, dtype=F32)
    pw = jnp.exp(dta[None] * ks[:, None, None])
    kk = jnp.real(jnp.einsum('gon,kgn,gni->kgio', cc, pw[:CHUNK], bb))
    kk = jnp.concatenate([jnp.zeros_like(kk[:1]), kk], axis=0)
    kc = kk.reshape(CHUNK + 1, nj, gpt * hch, hch).transpose(1, 0, 2, 3)
    kc = kc.reshape(nj, (CHUNK + 1) * LANE, hch).astype(BF16)
    pwr, pwi = jnp.real(pw), jnp.imag(pw)
    bbr = jnp.real(bb).transpose(0, 2, 1).reshape(1, g * hch, n)
    bbi = jnp.imag(bb).transpose(0, 2, 1).reshape(1, g * hch, n)
    par = jnp.repeat(pwr[CHUNK - 1::-1][:CHUNK], hch, axis=1)
    pai = jnp.repeat(pwi[CHUNK - 1::-1][:CHUNK], hch, axis=1)
    pc = jnp.concatenate([par * bbr - pai * bbi, par * bbi + pai * bbr], axis=-1)
    pc = pc.reshape(CHUNK, nj, gpt * hch, 2 * n).transpose(1, 0, 2, 3)
    pc = pc.reshape(nj, CHUNK * LANE, 2 * n).astype(BF16)
    ccr = jnp.real(cc).transpose(2, 0, 1).reshape(n, 1, g * hch)
    cci = jnp.imag(cc).transpose(2, 0, 1).reshape(n, 1, g * hch)
    qar = jnp.repeat(pwr[1:CHUNK + 1].transpose(2, 0, 1), hch, axis=2)
    qai = jnp.repeat(pwi[1:CHUNK + 1].transpose(2, 0, 1), hch, axis=2)
    rc = jnp.stack([ccr * qar - cci * qai, -(ccr * qai + cci * qar)], axis=0)
    rc = rc.reshape(2 * n, CHUNK, nj, gpt * hch).transpose(2, 0, 1, 3)
    rc = rc.reshape(nj, 2 * n, CHUNK * LANE).astype(BF16)
    a16 = pw[CHUNK].reshape(nj, 1, gpt * n)
    a16 = jnp.concatenate([jnp.real(a16), jnp.imag(a16)], axis=1)
    bc = jnp.stack([jnp.real(bb), jnp.imag(bb)], axis=0).transpose(1, 3, 0, 2)
    bc = bc.reshape(g * hch, 2 * n).astype(BF16)
    c2 = jnp.stack([jnp.real(cc), -jnp.imag(cc)], axis=0).transpose(0, 3, 1, 2)
    c2 = c2.reshape(2 * n, g * hch).astype(BF16)
    abr = jnp.real(a_bar).reshape(1, g * n)
    abi = jnp.imag(a_bar).reshape(1, g * n)
    return kc, pc, rc, a16, bc, c2, abr, abi


def _full(shape):
    return pl.BlockSpec(shape, lambda *_: (0,) * len(shape))


def kernel(x_prompt, x_sample, state_ssm_re, state_ssm_im, state_conv, meta_tokens, norm_mix, w_in,
           ssm_a_re, ssm_a_im, ssm_log_dt, ssm_b_re, ssm_b_im, ssm_c_re, ssm_c_im, ssm_d, w_glu, b_glu,
           conv_w, norm_out_ssm, norm_out_conv, w_out, norm_ffn, w_router, b_router, w_gate_up,
           b_gate_up, w_down, b_down, norm_final):
    nb, seq, d = x_prompt.shape
    ns = x_sample.shape[0]
    depth, _, g, n = state_ssm_re.shape
    assert depth == 1 and x_sample.shape[1] == 1 and meta_tokens.shape[0] == CHUNK
    cw = conv_w.shape[2]
    nj = cw // LANE
    ne = w_router.shape[2]
    dff = w_down.shape[2]
    nst = g * n
    nbt = nb // 2
    tt = 256
    n_chunks = seq // CHUNK
    tp = nb * seq
    tall = tp + ns
    tm = 128
    rows_p = nbt * tm
    assert rows_p % TOK_TILE == 0 and TOK_TILE % tm == 0 and ns <= TOK_TILE and d % 2 == 0
    n_tiles = tp // TOK_TILE + 1
    ta = n_tiles * TOK_TILE

    kc, pc, rc, a16, bc, c2, abr, abi = _ssm_matrices(
        ssm_a_re[0], ssm_a_im[0], ssm_log_dt[0], ssm_b_re[0], ssm_b_im[0], ssm_c_re[0], ssm_c_im[0])
    win_bf = w_in[0].astype(BF16)
    nmix = norm_mix[0].reshape(1, d)
    nconv = norm_out_conv[0].reshape(1, cw)
    cwt = conv_w[0]

    xsm = jnp.concatenate([x_sample.reshape(ns, d), meta_tokens], axis=0)
    nsm = ns + CHUNK
    s0r = state_ssm_re[0].reshape(ns, nst)
    s0i = state_ssm_im[0].reshape(ns, nst)
    buf0 = state_conv[0, :, 0, :]
    buf1 = state_conv[0, :, 1, :]
    u_sm, z_sm, y_s, ycn_s, sr_s, si_s = pl.pallas_call(
        _small_front_kernel,
        out_shape=(jax.ShapeDtypeStruct((nsm, cw), F32), jax.ShapeDtypeStruct((nsm, cw), F32),
                   jax.ShapeDtypeStruct((ns, cw), F32), jax.ShapeDtypeStruct((ns, cw), BF16),
                   jax.ShapeDtypeStruct((ns, nst), F32), jax.ShapeDtypeStruct((ns, nst), F32)),
        scratch_shapes=[pltpu.VMEM((cw, 2 * nst), BF16), pltpu.VMEM((2 * nst, cw), BF16)],
        compiler_params=_params(None, 56),
        name="small_front",
    )(xsm, nmix, win_bf, s0r, s0i, buf0, buf1, cwt, bc, c2, abr, abi, nconv)
    u_meta = u_sm[ns:]
    z_meta8 = z_sm[ns + CHUNK - 8:]
    new_conv_s = jnp.stack([buf1, z_sm[:ns]], axis=1)[None]
    new_re_s = sr_s.reshape(1, ns, g, n)
    new_im_s = si_s.reshape(1, ns, g, n)

    wch = CHUNK * LANE
    u4c, u_tok, ycn_p, ztail = pl.pallas_call(
        _front_kernel,
        grid=(nb // nbt, seq // tt),
        in_specs=[pl.BlockSpec((nbt, tt, d), lambda b, i: (b, i, 0)),
                  _full((1, d)), _full((d, 4 * cw)), _full((8, cw)), _full((3, cw)), _full((1, cw))],
        out_specs=(pl.BlockSpec((nj, nbt, tt // CHUNK, wch), lambda b, i: (0, b, i, 0)),
                   pl.BlockSpec((nbt, tt, cw), lambda b, i: (b, i, 0)),
                   pl.BlockSpec((nbt, tt, cw), lambda b, i: (b, i, 0)),
                   pl.BlockSpec((nbt, 8, cw), lambda b, i: (b, 0, 0))),
        out_shape=(jax.ShapeDtypeStruct((nj, nb, n_chunks, wch), BF16),
                   jax.ShapeDtypeStruct((nb, seq, cw), BF16),
                   jax.ShapeDtypeStruct((nb, seq, cw), BF16),
                   jax.ShapeDtypeStruct((nb, 8, cw), F32)),
        scratch_shapes=[pltpu.VMEM((nbt, tt + 8, cw), F32), pltpu.VMEM((nj, nbt, tt, LANE), F32)],
        compiler_params=_params(("arbitrary", "arbitrary"), 52),
        name="front",
    )(x_prompt, nmix, win_bf, z_meta8, cwt, nconv)
    new_conv_p = ztail[:, 6:8, :][None]

    cc = n_chunks // 2
    um = u_meta.reshape(CHUNK, nj, LANE).transpose(1, 0, 2).reshape(nj, 1, wch)
    um = jnp.broadcast_to(um, (nj, 8 * nb, wch)).astype(BF16)
    gpt = g // nj
    nstj = 2 * gpt * n
    hch = cw // g
    y4c, s_last = pl.pallas_call(
        _ssm_kernel,
        grid=(nj, n_chunks // cc),
        in_specs=[pl.BlockSpec((1, nb, cc, wch), lambda j, t: (j, 0, t, 0)),
                  pl.BlockSpec((1, 8 * nb, wch), lambda j, t: (j, 0, 0)),
                  pl.BlockSpec((1, (CHUNK + 1) * LANE, hch), lambda j, t: (j, 0, 0)),
                  pl.BlockSpec((1, wch, 2 * n), lambda j, t: (j, 0, 0)),
                  pl.BlockSpec((1, 2 * n, wch), lambda j, t: (j, 0, 0)),
                  pl.BlockSpec((1, 2, nstj // 2), lambda j, t: (j, 0, 0))],
        out_specs=(pl.BlockSpec((1, nb, cc, wch), lambda j, t: (j, 0, t, 0)),
                   pl.BlockSpec((1, nb, 1, nstj), lambda j, t: (j, 0, 0, 0))),
        out_shape=(jax.ShapeDtypeStruct((nj, nb, n_chunks, wch), F32),
                   jax.ShapeDtypeStruct((nj, nb, 1, nstj), F32)),
        scratch_shapes=[pltpu.VMEM((nb, 1, nstj), F32), pltpu.VMEM((nb, cc, nstj), F32),
                        pltpu.VMEM((nb, cc, nstj), F32),
                        pltpu.VMEM((wch, nstj), BF16), pltpu.VMEM((nstj, wch), BF16),
                        pltpu.VMEM((CHUNK // 2, 2 * LANE, 2 * LANE), BF16)],
        compiler_params=_params(("parallel", "arbitrary"), 56),
        name="ssm",
    )(u4c, um, kc, pc, rc, a16)
    sl = s_last.reshape(nj, nb, 2, gpt, n)
    new_re_p = sl[:, :, 0].transpose(1, 0, 2, 3).reshape(1, nb, g, n)
    new_im_p = sl[:, :, 1].transpose(1, 0, 2, 3).reshape(1, nb, g, n)

    dsk = ssm_d[0].reshape(1, cw)
    wglu_bf = w_glu[0].astype(BF16)
    bglu = b_glu[0].reshape(1, cw)
    nssm = norm_out_ssm[0].reshape(1, cw)
    wout_bf = w_out[0].astype(BF16)
    nffn = norm_ffn[0].reshape(1, d)
    wr_pad = jnp.zeros((d, LANE), F32).at[:, :ne].set(w_router[0])
    wr_hi = wr_pad.astype(BF16)
    wr_lo = (wr_pad - wr_hi.astype(F32)).astype(BF16)
    br = b_router[0].reshape(ne, 1)
    mix_w = (dsk, wglu_bf, bglu, nssm, wout_bf, nffn, wr_hi, wr_lo, br)
    mix_w_specs = [_full((1, cw)), _full((cw, cw)), _full((1, cw)), _full((1, cw)), _full((2 * cw, d)),
                   _full((1, d)), _full((d, LANE)), _full((d, LANE)), _full((ne, 1))]
    mix_out_shape = (jax.ShapeDtypeStruct((ta, d), F32), jax.ShapeDtypeStruct((ta, d), BF16),
                     jax.ShapeDtypeStruct((TOP_K, ta), I32), jax.ShapeDtypeStruct((TOP_K, ta), F32),
                     jax.ShapeDtypeStruct((n_tiles, ne, 1), I32))
    tpm = rows_p // TOK_TILE
    nbg = nb // nbt
    x1_all, hf_all, idx_all, gate_all, cnt = pl.pallas_call(
        _mix_kernel_prompt,
        grid=(seq // tm, nbg),
        in_specs=[pl.BlockSpec((nbt, tm, d), lambda i, b: (b, i, 0)),
                  pl.BlockSpec((nj, nbt, tm // CHUNK, wch), lambda i, b: (0, b, i, 0)),
                  pl.BlockSpec((nbt, tm, cw), lambda i, b: (b, i, 0)),
                  pl.BlockSpec((nbt, tm, cw), lambda i, b: (b, i, 0))] + mix_w_specs,
        out_specs=(pl.BlockSpec((rows_p, d), lambda i, b: (i * nbg + b, 0)),
                   pl.BlockSpec((rows_p, d), lambda i, b: (i * nbg + b, 0)),
                   pl.BlockSpec((TOP_K, rows_p), lambda i, b: (0, i * nbg + b)),
                   pl.BlockSpec((TOP_K, rows_p), lambda i, b: (0, i * nbg + b)),
                   pl.BlockSpec((tpm, ne, 1), lambda i, b: (i * nbg + b, 0, 0))),
        out_shape=mix_out_shape,
        scratch_shapes=[pltpu.VMEM((nj, nbt, tm, LANE), F32)],
        compiler_params=_params(("parallel", "parallel"), 52),
        name="mix_prompt",
    )(x_prompt, y4c, u_tok, ycn_p, *mix_w)

    last = n_tiles - 1
    any_spec = pl.BlockSpec(memory_space=pl.ANY)
    x1_all, hf_all, idx_all, gate_all, cnt = pl.pallas_call(
        _mix_kernel_sample,
        grid=(1,),
        in_specs=[_full((1, ns, d)), _full((ns, cw)), _full((1, ns, cw)),
                  _full((1, ns, cw))] + mix_w_specs + [any_spec] * 5,
        out_specs=(pl.BlockSpec((TOK_TILE, d), lambda i: (last, 0)),
                   pl.BlockSpec((TOK_TILE, d), lambda i: (last, 0)),
                   pl.BlockSpec((TOP_K, TOK_TILE), lambda i: (0, last)),
                   pl.BlockSpec((TOP_K, TOK_TILE), lambda i: (0, last)),
                   pl.BlockSpec((1, ne, 1), lambda i: (last, 0, 0))),
        out_shape=mix_out_shape,
        input_output_aliases={13: 0, 14: 1, 15: 2, 16: 3, 17: 4},
        compiler_params=_params(("arbitrary",), 32),
        name="mix_sample",
    )(x_sample.reshape(1, ns, d), y_s, u_sm[:ns].astype(BF16).reshape(1, ns, cw),
      ycn_s.reshape(1, ns, cw), *mix_w, x1_all, hf_all, idx_all, gate_all, cnt)

    bm = MOE_ROWS
    cnt2 = cnt.reshape(n_tiles, ne)
    before = jnp.cumsum(cnt2, axis=0) - cnt2
    count = jnp.sum(cnt2, axis=0)
    padded = ((count + WIN + bm - 1) // bm) * bm
    pend = jnp.cumsum(padded)
    pstart = pend - padded
    phase = before % SUBLANE
    span = jnp.where(cnt2 > 0, phase + cnt2, 0)
    gstart = (pstart[None, :] + before - phase).astype(I32).reshape(-1)
    nwin = ((span + WIN - 1) // WIN).astype(I32)
    reg8 = ((span + SUBLANE - 1) // SUBLANE) * SUBLANE
    loff = (jnp.cumsum(reg8, axis=1) - reg8).astype(I32)
    tail = jnp.where(span % SUBLANE != 0, loff + (span // SUBLANE) * SUBLANE, -1).astype(I32)
    twin = jnp.sum(nwin, axis=1).astype(I32)
    n_blocks = (ta * TOP_K + ne * (WIN + bm - 1) + bm - 1) // bm
    cap = n_blocks * bm
    blk0 = jnp.arange(n_blocks, dtype=I32) * bm
    blk_e = jnp.minimum(jnp.sum((pend[None, :] <= blk0[:, None]).astype(I32), axis=1), ne - 1)
    e_ar = jnp.arange(ne, dtype=I32)
    blk_hot = blk_e[:, None] == e_ar[None, :]

    def _of_block(per_expert):
        return jnp.sum(jnp.where(blk_hot, per_expert[None, :], 0), axis=1)

    blk_valid = jnp.clip(_of_block(count) - (blk0 - _of_block(pstart)), 0, bm).astype(I32)
    has = count > 0
    later = jnp.logical_and(e_ar[None, :] > e_ar[:, None], has[None, :])
    nxt_e = jnp.min(jnp.where(later, e_ar[None, :], ne), axis=1)
    nxt_e = jnp.where(nxt_e < ne, nxt_e, -1)
    ordinal = jnp.cumsum(has.astype(I32)) - 1
    is_first = jnp.logical_and(blk_valid > 0, blk0 == _of_block(pstart))
    blk_first = jnp.where(is_first, jnp.where(_of_block(ordinal) == 0, 2, 1), 0).astype(I32)
    blk_next = _of_block(nxt_e).astype(I32)
    blk_slot = (_of_block(ordinal) % 2).astype(I32)
    nvalid = jnp.full((1,), tall, I32)
    loff_al = (WIN * (jnp.cumsum(nwin, axis=1) - nwin)).astype(I32)

    def _window_list(nw, slots, rows, first_hbm, first_buf):
        wcum = jnp.cumsum(nw, axis=1)
        wslot = jnp.arange(slots, dtype=I32)
        w_hot = jnp.logical_and(wslot[None, :, None] >= (wcum - nw)[:, None, :],
                                wslot[None, :, None] < wcum[:, None, :])

        def _of_window(per_tile_expert):
            return jnp.sum(jnp.where(w_hot, per_tile_expert[:, None, :], 0), axis=2)

        w_in_run = wslot[None, :] - _of_window(wcum - nw)
        return [(_of_window(f) + rows * w_in_run).astype(I32).reshape(-1) for f in (first_hbm, first_buf)]

    w_hbm, w_buf = _window_list(nwin, MAX_WINDOWS, WIN, gstart.reshape(n_tiles, ne), loff)
    nrun_d = _dispatch_run_rows(ne)
    zero_grp, spare_grp = nrun_d - 2 * SUBLANE, nrun_d - SUBLANE
    m_lo = jnp.where(nwin > 0, loff, spare_grp).astype(I32).reshape(-1)
    m_tg = jnp.where(jnp.logical_and(nwin > 0, tail >= 0), tail, zero_grp).astype(I32).reshape(-1)
    m_keep = (nwin == 0).astype(I32).reshape(-1)
    tables_d = (w_hbm, w_buf, m_lo, m_tg, m_keep, twin, nvalid)
    tables_c = (w_hbm, twin, nvalid)
    rowoff_d = (loff + phase).astype(I32).reshape(n_tiles, ne, 1)
    rowoff_c = (loff_al + phase).astype(I32).reshape(n_tiles, ne, 1)
    rgn_d = jnp.stack([loff, loff + reg8], axis=1).astype(I32)
    rgn_c = jnp.stack([loff_al, loff_al + WIN * nwin], axis=1).astype(I32)
    xw = d // 2 + GATE_COLS

    xs = pl.pallas_call(
        _dispatch_kernel,
        grid_spec=pltpu.PrefetchScalarGridSpec(
            num_scalar_prefetch=len(tables_d),
            grid=(n_tiles,),
            in_specs=[pl.BlockSpec((TOK_TILE, d), lambda i, *_: (i, 0)),
                      pl.BlockSpec((TOP_K, TOK_TILE), lambda i, *_: (0, i)),
                      pl.BlockSpec((TOP_K, TOK_TILE), lambda i, *_: (0, i)),
                      pl.BlockSpec((1, ne, 1), lambda i, *_: (i, 0, 0)),
                      pl.BlockSpec((1, 2, ne), lambda i, *_: (i, 0, 0))],
            out_specs=pl.BlockSpec(memory_space=pl.ANY),
            scratch_shapes=[pltpu.VMEM((2, _dispatch_run_rows(ne), xw), U32),
                            pltpu.VMEM((ne, SUBLANE, xw), U32), pltpu.SemaphoreType.DMA((2,))]),
        out_shape=jax.ShapeDtypeStruct((cap, xw), U32),
        compiler_params=_params(("arbitrary",), 40),
        name="dispatch",
    )(*tables_d, hf_all, idx_all, gate_all, rowoff_d, rgn_d)

    yb = pl.pallas_call(
        _moe_kernel,
        grid_spec=pltpu.PrefetchScalarGridSpec(
            num_scalar_prefetch=5,
            grid=(n_blocks,),
            in_specs=[pl.BlockSpec((bm, xw), lambda i, *_: (i, 0)),
                      pl.BlockSpec((1, 1, 2 * dff), lambda i, be, *_: (be[i], 0, 0)),
                      pl.BlockSpec((1, 1, d), lambda i, be, *_: (be[i], 0, 0)),
                      pl.BlockSpec(memory_space=pl.ANY), pl.BlockSpec(memory_space=pl.ANY)],
            out_specs=pl.BlockSpec((bm, d // 2), lambda i, *_: (i, 0)),
            scratch_shapes=[pltpu.VMEM((2, d, 2 * dff), F32), pltpu.VMEM((2, dff, d), F32),
                            pltpu.VMEM((d, 2 * dff), BF16), pltpu.VMEM((dff, d), BF16),
                            pltpu.SemaphoreType.DMA((2, 2))]),
        out_shape=jax.ShapeDtypeStruct((cap, d // 2), U32),
        compiler_params=_params(("arbitrary",), 58),
        name="moe",
    )(blk_e, blk_valid, blk_first, blk_next, blk_slot, xs, b_gate_up[0].reshape(ne, 1, 2 * dff),
      b_down[0].reshape(ne, 1, d), w_gate_up[0], w_down[0])

    nfin = norm_final.reshape(1, d)
    nbh = TOK_TILE // tm

    def _tile_of(s):
        return (s + n_tiles - 1) % n_tiles

    tiles_per_time = nbg * tpm

    def _yp_index(s, *_):
        t = jnp.maximum(s - 1, 0)
        return (t % tiles_per_time, t // tiles_per_time, 0)

    y_p, y_sm = pl.pallas_call(
        _combine_kernel,
        grid_spec=pltpu.PrefetchScalarGridSpec(
            num_scalar_prefetch=len(tables_c),
            grid=(n_tiles,),
            in_specs=[pl.BlockSpec((TOK_TILE, d), lambda s, *_: (_tile_of(s), 0)),
                      pl.BlockSpec((TOP_K, TOK_TILE), lambda s, *_: (0, _tile_of(s))),
                      pl.BlockSpec((1, ne, 1), lambda s, *_: (_tile_of(s), 0, 0)),
                      pl.BlockSpec((1, 2, ne), lambda s, *_: (_tile_of(s), 0, 0)),
                      pl.BlockSpec((1, d), lambda s, *_: (0, 0)),
                      pl.BlockSpec(memory_space=pl.ANY)],
            out_specs=(pl.BlockSpec((nbh, tm, d), _yp_index),
                       pl.BlockSpec((ns, d), lambda s, *_: (0, 0))),
            scratch_shapes=[pltpu.VMEM((2, _combine_run_rows(ne), d // 2), U32),
                            pltpu.SemaphoreType.DMA((2,))]),
        out_shape=(jax.ShapeDtypeStruct((nb, seq, d), F32), jax.ShapeDtypeStruct((ns, d), F32)),
        compiler_params=_params(("arbitrary",), 48),
        name="combine",
    )(*tables_c, x1_all, idx_all, rowoff_c, rgn_c, nfin, yb)

    return (y_p, y_sm.reshape(ns, 1, d), new_re_p, new_im_p, new_conv_p,
            new_re_s, new_im_s, new_conv_s)
```

```python
import math

import jax
import jax.numpy as jnp
from jax import lax
from jax.experimental import pallas as pl
from jax.experimental.pallas import tpu as pltpu

F32 = jnp.float32
BF16 = jnp.bfloat16
U32 = jnp.uint32
I32 = jnp.int32
EPS = 1e-5
CHUNK = 16
LANE = 128
TOP_K = 4
SWIGLU_LIMIT = 7.0
SWIGLU_ALPHA = 1.702
MOE_ROWS = 1024
MOE_SUB = 256
MOE_CHAIN = 512
TOK_TILE = 256
WIN = 32
SUBLANE = 8
GATE_COLS = LANE


MAX_WINDOWS = 72


def _dispatch_run_rows(ne):
    return TOP_K * TOK_TILE + ne * 2 * (SUBLANE - 1) + WIN


def _combine_run_rows(ne):
    return -(-(TOP_K * TOK_TILE + ne * (SUBLANE - 1 + WIN - 1)) // WIN) * WIN
HI_MASK = 0xFFFF0000
MIB = 1024 * 1024


def _rms(x, g):
    return x * lax.rsqrt(jnp.mean(x * x, axis=-1, keepdims=True) + EPS) * g


def _gelu_tanh(x):
    c = math.sqrt(2.0 / math.pi)
    return 0.5 * x * (1.0 + jnp.tanh(c * (x + 0.044715 * (x * x * x))))


def _params(sem, vmem_mib):
    return pltpu.CompilerParams(dimension_semantics=sem, vmem_limit_bytes=vmem_mib * MIB)


def _pack_pairs(a, b):
    return (pltpu.bitcast(a, U32) >> 16) | (pltpu.bitcast(b, U32) & jnp.uint32(HI_MASK))


def _unpack_pairs(w):
    lo = pltpu.bitcast(w << 16, F32)
    hi = pltpu.bitcast(w & jnp.uint32(HI_MASK), F32)
    return jnp.concatenate([lo, hi], axis=-1).astype(BF16)


def _iota2(shape, axis):
    return lax.broadcasted_iota(I32, shape, axis)


def _expand_cols(compact, reps_log2, n_log2):
    q = _iota2((compact.shape[1], 2 << (reps_log2 + n_log2)), 0)
    c = _iota2((compact.shape[1], 2 << (reps_log2 + n_log2)), 1)
    nmask = (1 << n_log2) - 1
    same = jnp.logical_and((q >> n_log2) == (c >> (reps_log2 + n_log2)), (q & nmask) == (c & nmask))
    return jnp.dot(compact, jnp.where(same, 1.0, 0.0).astype(BF16), preferred_element_type=F32)


def _expand_rows(compact, reps_log2, n_log2):
    r = _iota2((2 << (reps_log2 + n_log2), compact.shape[0]), 0)
    q = _iota2((2 << (reps_log2 + n_log2), compact.shape[0]), 1)
    nmask = (1 << n_log2) - 1
    same = jnp.logical_and((r >> (reps_log2 + n_log2)) == (q >> n_log2), (r & nmask) == (q & nmask))
    return jnp.dot(jnp.where(same, 1.0, 0.0).astype(BF16), compact, preferred_element_type=F32)


def _group_mask(shape, row_shift, col_shift, ngroups):
    r = _iota2(shape, 0)
    c = _iota2(shape, 1)
    return ((r >> row_shift) & (ngroups - 1)) == ((c >> col_shift) & (ngroups - 1))


def _small_front_kernel(x_ref, nmix_ref, win_ref, s0r_ref, s0i_ref, b0_ref, b1_ref, cw_ref,
                        bc_ref, cc_ref, abr_ref, abi_ref, nconv_ref,
                        u_ref, z_ref, y_ref, ycn_ref, sr_ref, si_ref, bdb_ref, cm_ref):
    ns, nst = s0r_ref.shape
    cw = u_ref.shape[1]
    n = bc_ref.shape[1] // 2
    nlog = n.bit_length() - 1
    glog = (nst // n).bit_length() - 1
    hlog = (cw >> glog).bit_length() - 1
    bdb_ref[...] = jnp.where(_group_mask(bdb_ref.shape, hlog, nlog, 1 << glog),
                             _expand_cols(bc_ref[...], glog, nlog), 0.0).astype(BF16)
    cm_ref[...] = jnp.where(_group_mask(cm_ref.shape, nlog, hlog, 1 << glog),
                            _expand_rows(cc_ref[...], glog, nlog), 0.0).astype(BF16)
    h = _rms(x_ref[...], nmix_ref[...]).astype(BF16)
    proj = jnp.dot(h, win_ref[...], preferred_element_type=F32)
    u = proj[:, 0:cw]
    zc = proj[:, cw:2 * cw]
    gb = proj[:, 2 * cw:3 * cw]
    gc = proj[:, 3 * cw:4 * cw]
    z = gc * zc
    u_ref[...] = u
    z_ref[...] = z
    bu = jnp.dot(u[:ns].astype(BF16), bdb_ref[...], preferred_element_type=F32)
    abr = abr_ref[...]
    abi = abi_ref[...]
    s0r = s0r_ref[...]
    s0i = s0i_ref[...]
    sr = abr * s0r - abi * s0i + bu[:, :nst]
    si = abr * s0i + abi * s0r + bu[:, nst:]
    sr_ref[...] = sr
    si_ref[...] = si
    scat = jnp.concatenate([sr, si], axis=-1).astype(BF16)
    y_ref[...] = jnp.dot(scat, cm_ref[...], preferred_element_type=F32)
    conv = cw_ref[0:1, :] * b0_ref[...] + cw_ref[1:2, :] * b1_ref[...] + cw_ref[2:3, :] * z[:ns]
    ycn_ref[...] = _rms(gb[:ns] * conv, nconv_ref[...]).astype(BF16)


def _front_kernel(x_ref, nmix_ref, win_ref, zm_ref, cw_ref, nconv_ref,
                  uc_ref, ut_ref, ycn_ref, zt_ref, zbuf, ubuf):
    i = pl.program_id(1)
    nb, tt, d = x_ref.shape
    cw = ycn_ref.shape[2]
    rows = nb * tt
    ncz = tt // CHUNK

    @pl.when(i == 0)
    def _():
        zbuf[:, 0:8, :] = jnp.broadcast_to(zm_ref[...][None], (nb, 8, cw))

    h = _rms(x_ref[...].reshape(rows, d), nmix_ref[...]).astype(BF16)
    u = jnp.dot(h, win_ref[:, 0:cw], preferred_element_type=F32)
    ut_ref[...] = u.astype(BF16).reshape(nb, tt, cw)
    for j in range(cw // LANE):
        ubuf[j] = u[:, j * LANE:(j + 1) * LANE].reshape(nb, tt, LANE)
    for s in range(CHUNK):
        for j in range(cw // LANE):
            piece = ubuf[j, :, pl.ds(s, ncz, stride=CHUNK), :]
            uc_ref[j, :, :, s * LANE:(s + 1) * LANE] = piece.astype(BF16)
    zc = jnp.dot(h, win_ref[:, cw:2 * cw], preferred_element_type=F32)
    gc = jnp.dot(h, win_ref[:, 3 * cw:4 * cw], preferred_element_type=F32)
    z3 = (gc * zc).reshape(nb, tt, cw)
    zbuf[:, 8:8 + tt, :] = z3
    z1 = zbuf[:, 7:7 + tt, :]
    z2 = zbuf[:, 6:6 + tt, :]
    conv = cw_ref[0:1, :] * z2 + cw_ref[1:2, :] * z1 + cw_ref[2:3, :] * z3
    gb = jnp.dot(h, win_ref[:, 2 * cw:3 * cw], preferred_element_type=F32)
    yc = gb * conv.reshape(rows, cw)
    ycn_ref[...] = _rms(yc, nconv_ref[...]).astype(BF16).reshape(nb, tt, cw)
    tail = zbuf[:, tt:tt + 8, :]
    zt_ref[...] = tail
    zbuf[:, 0:8, :] = tail


def _ssm_kernel(u_ref, um_ref, kc_ref, pc_ref, rc_ref, a16_ref, y_ref, sl_ref,
                s_carry, ds_ref, sp_ref, p_s, r_s, t_s):
    th = pl.program_id(1)
    _, nb, cc, w = u_ref.shape
    nst = p_s.shape[1]
    half = nst // 2
    rows = nb * cc
    blk = 2 * LANE
    u = u_ref[0].reshape(rows, w)

    @pl.when(th == 0)
    def _():
        hch = kc_ref.shape[2]
        gpt = LANE // hch
        hlog = hch.bit_length() - 1
        glog = gpt.bit_length() - 1
        nlog = (pc_ref.shape[2] // 2).bit_length() - 1
        p_s[...] = jnp.where(_group_mask(p_s.shape, hlog, nlog, gpt),
                             _expand_cols(pc_ref[0], glog, nlog), 0.0).astype(BF16)
        r_s[...] = jnp.where(_group_mask(r_s.shape, nlog, hlog, gpt),
                             _expand_rows(rc_ref[0], glog, nlog), 0.0).astype(BF16)
        nlag = kc_ref.shape[1] // LANE
        o = _iota2((hch, LANE), 0)
        c = _iota2((hch, LANE), 1)
        spread = jnp.where((c & (hch - 1)) == o, 1.0, 0.0).astype(BF16)
        lagm = jnp.dot(kc_ref[0], spread, preferred_element_type=F32)
        r = _iota2(lagm.shape, 0)
        c = _iota2(lagm.shape, 1)
        lagm = jnp.where(((r >> hlog) & (gpt - 1)) == (c >> hlog), lagm, 0.0).astype(BF16)
        for dlt in range(nlag // 2):
            b0 = lagm[(2 * dlt) * LANE:(2 * dlt + 1) * LANE]
            b1 = lagm[(2 * dlt + 1) * LANE:(2 * dlt + 2) * LANE]
            b2 = lagm[(2 * dlt + 2) * LANE:(2 * dlt + 3) * LANE]
            t_s[dlt, 0:LANE, 0:LANE] = b1
            t_s[dlt, 0:LANE, LANE:blk] = b2
            t_s[dlt, LANE:blk, 0:LANE] = b0
            t_s[dlt, LANE:blk, LANE:blk] = b1
        ds_ref[:, 0:8, :] = jnp.dot(um_ref[0], p_s[...], preferred_element_type=F32).reshape(nb, 8, nst)
        s_carry[...] = ds_ref[:, 0:1, :]

    ds_ref[...] = jnp.dot(u, p_s[...], preferred_element_type=F32).reshape(nb, cc, nst)
    ar = a16_ref[0, 0:1, :].reshape(1, 1, half)
    ai = a16_ref[0, 1:2, :].reshape(1, 1, half)
    sr = s_carry[:, :, 0:half]
    si = s_carry[:, :, half:nst]
    for c in range(cc):
        sp_ref[:, c:c + 1, 0:half] = sr
        sp_ref[:, c:c + 1, half:nst] = si
        dr = ds_ref[:, c:c + 1, 0:half]
        di = ds_ref[:, c:c + 1, half:nst]
        sr, si = ar * sr - ai * si + dr, ar * si + ai * sr + di
    s_carry[:, :, 0:half] = sr
    s_carry[:, :, half:nst] = si
    sl_ref[0, :, :, 0:half] = sr
    sl_ref[0, :, :, half:nst] = si

    sp = sp_ref[...].reshape(rows, nst).astype(BF16)
    for tb in range(w // blk):
        acc = jnp.dot(sp, r_s[:, tb * blk:(tb + 1) * blk], preferred_element_type=F32)
        for sb in range(tb + 1):
            acc = acc + jnp.dot(u[:, sb * blk:(sb + 1) * blk], t_s[tb - sb],
                                preferred_element_type=F32)
        y_ref[0, :, :, tb * blk:(tb + 1) * blk] = acc.reshape(nb, cc, blk)


def _mix_rows(x, yssm, ut, ycn, dsk_ref, wglu_ref, bglu_ref, nssm_ref, wout_ref,
              nffn_ref, wrh_ref, wrl_ref, br_ref):
    ne = br_ref.shape[0]
    y = _gelu_tanh(yssm + dsk_ref[...] * ut.astype(F32))
    glu = jnp.dot(y.astype(BF16), wglu_ref[...], preferred_element_type=F32) + bglu_ref[...]
    o = y * jax.nn.sigmoid(glu)
    ysn = _rms(o, nssm_ref[...]).astype(BF16)
    mix = jnp.concatenate([ysn, ycn], axis=-1)
    x1 = x + jnp.dot(mix, wout_ref[...], preferred_element_type=F32)
    hf = _rms(x1, nffn_ref[...])
    hf_hi = hf.astype(BF16)
    hf_lo = (hf - hf_hi.astype(F32)).astype(BF16)
    logits = (jnp.dot(hf_hi, wrh_ref[...], preferred_element_type=F32)
              + jnp.dot(hf_lo, wrh_ref[...], preferred_element_type=F32)
              + jnp.dot(hf_hi, wrl_ref[...], preferred_element_type=F32))
    lt = logits.T[0:ne, :] + br_ref[...]
    iota = lax.broadcasted_iota(I32, lt.shape, 0)
    vals, idxs = [], []
    sel = jnp.zeros(lt.shape, F32)
    for _ in range(TOP_K):
        m = jnp.max(lt, axis=0, keepdims=True)
        ik = jnp.min(jnp.where(lt == m, iota, ne), axis=0, keepdims=True)
        vals.append(m)
        idxs.append(ik)
        hit = iota == ik
        sel = sel + jnp.where(hit, 1.0, 0.0)
        lt = jnp.where(hit, -jnp.inf, lt)
    es = [jnp.exp(v - vals[0]) for v in vals]
    tot = es[0] + es[1] + es[2] + es[3]
    idx = jnp.concatenate(idxs, axis=0)
    gates = jnp.concatenate([e / tot for e in es], axis=0)
    return x1, hf_hi, idx, gates, sel


def _mix_kernel_prompt(*refs):
    x_ref, yc_ref = refs[0], refs[1]
    x1_ref, hf_ref, idx_ref, gate_ref, cnt_ref, ybuf = refs[13:]
    nj, nb, ncz, _ = yc_ref.shape
    for s in range(CHUNK):
        for j in range(nj):
            ybuf[j, :, pl.ds(s, ncz, stride=CHUNK), :] = yc_ref[j, :, :, s * LANE:(s + 1) * LANE]
    ut_ref, ycn_ref = refs[2], refs[3]
    tt, d = x_ref.shape[1], x_ref.shape[2]
    cw = nj * LANE
    nbc = TOK_TILE // tt
    for t in range(cnt_ref.shape[0]):
        b0, r0 = t * nbc, t * TOK_TILE
        yssm = jnp.concatenate([ybuf[j, b0:b0 + nbc].reshape(TOK_TILE, LANE) for j in range(nj)], axis=-1)
        x1, hf, idx, gates, sel = _mix_rows(
            x_ref[b0:b0 + nbc].reshape(TOK_TILE, d), yssm, ut_ref[b0:b0 + nbc].reshape(TOK_TILE, cw),
            ycn_ref[b0:b0 + nbc].reshape(TOK_TILE, cw), *refs[4:13])
        x1_ref[r0:r0 + TOK_TILE, :] = x1
        hf_ref[r0:r0 + TOK_TILE, :] = hf
        idx_ref[:, r0:r0 + TOK_TILE] = idx
        gate_ref[:, r0:r0 + TOK_TILE] = gates
        cnt_ref[t] = jnp.sum(sel, axis=1, keepdims=True).astype(I32)


def _mix_kernel_sample(*refs):
    x1, hf, idx, gates, sel = _mix_rows(refs[0][0], refs[1][...], refs[2][0], refs[3][0], *refs[4:13])
    x1_ref, hf_ref, idx_ref, gate_ref, cnt_ref = refs[18:]
    ns = x1.shape[0]
    x1_ref[...] = jnp.zeros(x1_ref.shape, x1_ref.dtype)
    hf_ref[...] = jnp.zeros(hf_ref.shape, hf_ref.dtype)
    idx_ref[...] = jnp.zeros(idx_ref.shape, idx_ref.dtype)
    gate_ref[...] = jnp.zeros(gate_ref.shape, gate_ref.dtype)
    x1_ref[0:ns, :] = x1
    hf_ref[0:ns, :] = hf
    idx_ref[:, 0:ns] = idx
    gate_ref[:, 0:ns] = gates
    cnt_ref[0] = jnp.sum(sel, axis=1, keepdims=True).astype(I32)


def _split_bf16(x, parts):
    out = []
    for _ in range(parts - 1):
        p = x.astype(BF16)
        out.append(p)
        x = x - p.astype(F32)
    out.append(x.astype(BF16))
    return out


def _run_onehot(idx_ref, loff_ref, rgn_ref, tile, n_valid, nrun, gate_ref=None):
    ne = loff_ref.shape[1]
    tt = idx_ref.shape[1]
    e_iota = lax.broadcasted_iota(I32, (ne, tt), 0)
    tok = tile * tt + lax.broadcasted_iota(I32, (1, tt), 1)
    valid = tok < n_valid
    hits = [jnp.logical_and(e_iota == idx_ref[k:k + 1, :], valid) for k in range(TOP_K)]
    sel = jnp.zeros((ne, tt), F32)
    for h in hits:
        sel = sel + jnp.where(h, 1.0, 0.0)
    before = lax.broadcasted_iota(I32, (tt, tt), 0) < lax.broadcasted_iota(I32, (tt, tt), 1)
    tri = jnp.where(before, 1.0, 0.0).astype(BF16)
    base = jnp.dot(sel.astype(BF16), tri, preferred_element_type=F32) + loff_ref[0].astype(F32)
    base = jnp.where(sel > 0.0, base + 1.0, 0.0)
    b_hi = jnp.floor(base * (1.0 / 256.0))
    b_lo = base - 256.0 * b_hi
    r_i = lax.broadcasted_iota(I32, (nrun, ne), 0)
    own = jnp.logical_and(r_i >= rgn_ref[0, 0:1, :], r_i < rgn_ref[0, 1:2, :])
    own_bf = jnp.where(own, 1.0, 0.0).astype(BF16)
    want = (256.0 * jnp.dot(own_bf, b_hi.astype(BF16), preferred_element_type=F32)
            + jnp.dot(own_bf, b_lo.astype(BF16), preferred_element_type=F32))
    r_f = (lax.broadcasted_iota(I32, (nrun, tt), 0) + 1).astype(F32)
    smat = jnp.where(want == r_f, 1.0, 0.0).astype(BF16)
    if gate_ref is None:
        return smat
    gate_e = jnp.zeros((ne, tt), F32)
    for k, h in enumerate(hits):
        gate_e = gate_e + jnp.where(h, gate_ref[k:k + 1, :], 0.0)
    nparts = 3
    pieces = jnp.concatenate(_split_bf16(gate_e, nparts), axis=0)
    per = lax.dot_general(smat, pieces, (((1,), (1,)), ((), ())), preferred_element_type=F32)
    lo3 = jnp.concatenate([rgn_ref[0, 0:1, :]] * nparts, axis=1)
    hi3 = jnp.concatenate([rgn_ref[0, 1:2, :]] * nparts, axis=1)
    r_i3 = lax.broadcasted_iota(I32, (nrun, nparts * ne), 0)
    own3 = jnp.logical_and(r_i3 >= lo3, r_i3 < hi3)
    gcol = jnp.sum(jnp.where(own3, per, 0.0), axis=1, keepdims=True)
    return smat, gcol


def _window_copy(buf, slot, hbm, lo, g, sem, to_hbm, rows, align):
    src = buf.at[slot, pl.ds(pl.multiple_of(lo, align), rows)]
    dst = hbm.at[pl.ds(pl.multiple_of(g, align), rows)]
    if to_hbm:
        return pltpu.make_async_copy(src, dst, sem.at[slot])
    return pltpu.make_async_copy(dst, src, sem.at[slot])


def _start_windows(wg_ref, wl_ref, tw_ref, tile, slots, buf, slot, hbm, sem, to_hbm, rows, align):
    def per_window(w, c):
        k = tile * slots + w
        lo = w * rows if wl_ref is None else wl_ref[k]
        _window_copy(buf, slot, hbm, lo, wg_ref[k], sem, to_hbm, rows, align).start()
        return c

    lax.fori_loop(0, tw_ref[tile], per_window, 0)


def _wait_windows(count, buf, slot, hbm, sem, to_hbm, rows, align):
    batch = 8

    def many(w, c):
        _window_copy(buf, slot, hbm, 0, 0, sem, to_hbm, batch * rows, align).wait()
        return c

    def one(w, c):
        _window_copy(buf, slot, hbm, 0, 0, sem, to_hbm, rows, align).wait()
        return c

    lax.fori_loop(0, count // batch, many, 0)
    lax.fori_loop(0, count % batch, one, 0)


def _dispatch_kernel(wg_ref, wl_ref, mlo_ref, mtg_ref, keep_ref, tw_ref, nv_ref,
                     hf_ref, idx_ref, gate_ref, loff_ref, rgn_ref, xs_ref, buf, carry, sem):
    i = pl.program_id(0)
    nt = pl.num_programs(0)
    ne = loff_ref.shape[1]
    tt, d = hf_ref.shape
    nrun = buf.shape[1]
    slot = i % 2

    @pl.when(i == 0)
    def _():
        carry[...] = jnp.zeros(carry.shape, carry.dtype)

    smat, gcol = _run_onehot(idx_ref, loff_ref, rgn_ref, i, nv_ref[0], nrun, gate_ref)
    xr = jnp.dot(smat, hf_ref[...], preferred_element_type=F32)
    lane0 = lax.broadcasted_iota(I32, (nrun, GATE_COLS), 1) == 0
    buf[slot, :, 0:d // 2] = _pack_pairs(xr[:, :d // 2], xr[:, d // 2:])
    buf[slot, :, d // 2:] = pltpu.bitcast(jnp.where(lane0, gcol, 0.0), U32)

    def merge(e, c):
        k = i * ne + e
        lo = pl.multiple_of(mlo_ref[k], SUBLANE)
        buf[slot, pl.ds(lo, SUBLANE), :] = buf[slot, pl.ds(lo, SUBLANE), :] | carry[e]
        tg = pl.multiple_of(mtg_ref[k], SUBLANE)
        carry[e] = jnp.where(keep_ref[k] > 0, carry[e], buf[slot, pl.ds(tg, SUBLANE), :])
        return c

    lax.fori_loop(0, ne, merge, 0, unroll=4)

    @pl.when(i > 0)
    def _():
        _wait_windows(tw_ref[i - 1], buf, 1 - slot, xs_ref, sem, True, WIN, SUBLANE)

    _start_windows(wg_ref, wl_ref, tw_ref, i, MAX_WINDOWS, buf, slot, xs_ref, sem, True, WIN, SUBLANE)

    @pl.when(i == nt - 1)
    def _():
        _wait_windows(tw_ref[i], buf, slot, xs_ref, sem, True, WIN, SUBLANE)


def _expert_weight_copies(wg_hbm, wd_hbm, wg_f32, wd_f32, sem, e, slot):
    return (pltpu.make_async_copy(wg_hbm.at[e], wg_f32.at[slot], sem.at[0, slot]),
            pltpu.make_async_copy(wd_hbm.at[e], wd_f32.at[slot], sem.at[1, slot]))


def _moe_kernel(be_ref, bv_ref, first_ref, nxt_ref, slot_ref,
                x_ref, bg_ref, bd_ref, wg_hbm, wd_hbm, y_ref,
                wg_f32, wd_f32, wg_bf, wd_bf, sem):
    i = pl.program_id(0)
    e = be_ref[i]
    dff = wd_bf.shape[0]
    bm = x_ref.shape[0]
    sub = MOE_SUB
    nw = x_ref.shape[1] - GATE_COLS

    @pl.when(first_ref[i] > 0)
    def _():
        slot = slot_ref[i]

        @pl.when(first_ref[i] > 1)
        def _():
            for cp in _expert_weight_copies(wg_hbm, wd_hbm, wg_f32, wd_f32, sem, e, slot):
                cp.start()

        for cp in _expert_weight_copies(wg_hbm, wd_hbm, wg_f32, wd_f32, sem, e, slot):
            cp.wait()

        @pl.when(nxt_ref[i] >= 0)
        def _():
            for cp in _expert_weight_copies(wg_hbm, wd_hbm, wg_f32, wd_f32, sem, nxt_ref[i], 1 - slot):
                cp.start()

        wg_bf[...] = wg_f32[slot].astype(BF16)
        wd_bf[...] = wd_f32[slot].astype(BF16)

    def rows(r0, nrows):
        sizes = [MOE_CHAIN] * (nrows // MOE_CHAIN) + ([sub] if nrows % MOE_CHAIN else [])
        lo = r0
        for size in sizes:
            _chain(lo, size)
            lo += size

    def _chain(lo, n):
        live = lax.broadcasted_iota(I32, (n, 1), 0) + lo < bv_ref[i]
        x = jnp.where(live, _unpack_pairs(x_ref[lo:lo + n, 0:nw]), jnp.zeros((), BF16))
        route = jnp.where(live, pltpu.bitcast(x_ref[lo:lo + n, nw:], F32)[:, 0:1], 0.0)
        gu = jnp.dot(x, wg_bf[...], preferred_element_type=F32) + bg_ref[0]
        gate = jnp.minimum(gu[:, :dff], SWIGLU_LIMIT)
        up = jnp.clip(gu[:, dff:], -SWIGLU_LIMIT, SWIGLU_LIMIT)
        h = gate * jax.nn.sigmoid(SWIGLU_ALPHA * gate) * (up + 1.0)
        y = jnp.dot(h.astype(BF16), wd_bf[...], preferred_element_type=F32) + bd_ref[0]
        yr = (route * y).astype(BF16).astype(F32)
        half = yr.shape[1] // 2
        y_ref[lo:lo + n, :] = _pack_pairs(yr[:, :half], yr[:, half:])

    nchains = bm // sub
    for live_chains in range(nchains + 1):
        lo_rows, hi_rows = (live_chains - 1) * sub, live_chains * sub

        @pl.when(jnp.logical_and(bv_ref[i] > lo_rows, bv_ref[i] <= hi_rows) if live_chains
                 else bv_ref[i] <= 0)
        def _(used=hi_rows):
            if used:
                rows(0, used)
            if used < bm:
                y_ref[used:bm, :] = jnp.zeros((bm - used, y_ref.shape[1]), y_ref.dtype)


def _combine_tile(step, nt):
    return (step + nt - 1) % nt


def _combine_kernel(wg_ref, tw_ref, nv_ref,
                    x1_ref, idx_ref, loff_ref, rgn_ref, nf_ref, yb_ref,
                    yp_ref, ys_ref, buf, sem):
    s = pl.program_id(0)
    nt = pl.num_programs(0)
    tt, d = x1_ref.shape
    nrun = buf.shape[1]
    tile = _combine_tile(s, nt)
    slot = s % 2

    @pl.when(s == 0)
    def _():
        buf[...] = jnp.zeros(buf.shape, buf.dtype)
        _start_windows(wg_ref, None, tw_ref, tile, MAX_WINDOWS, buf, slot, yb_ref, sem, False,
                       WIN, SUBLANE)

    @pl.when(s + 1 < nt)
    def _():
        _start_windows(wg_ref, None, tw_ref, _combine_tile(s + 1, nt), MAX_WINDOWS, buf, 1 - slot,
                       yb_ref, sem, False, WIN, SUBLANE)

    smat = _run_onehot(idx_ref, loff_ref, rgn_ref, tile, nv_ref[0], nrun)

    _wait_windows(tw_ref[tile], buf, slot, yb_ref, sem, False, WIN, SUBLANE)
    yrun = _unpack_pairs(buf[slot])
    moe = lax.dot_general(smat, yrun, (((0,), (0,)), ((), ())), preferred_element_type=F32)
    out = _rms(x1_ref[...] + moe, nf_ref[...])

    @pl.when(s == 0)
    def _():
        ys_ref[...] = out[0:ys_ref.shape[0], :]

    @pl.when(s > 0)
    def _():
        yp_ref[...] = out.reshape(yp_ref.shape)


def _ssm_matrices(a_re, a_im, log_dt, b_re, b_im, c_re, c_im):
    g, n = a_re.shape
    hch = b_re.shape[2]
    gpt = LANE // hch
    nj = g // gpt
    a = lax.complex(a_re, a_im)
    dta = a * jnp.exp(log_dt)[:, None]
    a_bar = jnp.exp(dta)
    bb = ((a_bar - 1.0) / a)[:, :, None] * lax.complex(b_re, b_im)
    cc = lax.complex(c_re, c_im)
    ks = jnp.arange(CHUNK + 1, dtype=F32)
    pw = jnp.exp(dta[None] * ks[:, None, None])
    kk = jnp.real(jnp.einsum('gon,kgn,gni->kgio', cc, pw[:CHUNK], bb))
    kk = jnp.concatenate([jnp.zeros_like(kk[:1]), kk], axis=0)
    kc = kk.reshape(CHUNK + 1, nj, gpt * hch, hch).transpose(1, 0, 2, 3)
    kc = kc.reshape(nj, (CHUNK + 1) * LANE, hch).astype(BF16)
    pwr, pwi = jnp.real(pw), jnp.imag(pw)
    bbr = jnp.real(bb).transpose(0, 2, 1).reshape(1, g * hch, n)
    bbi = jnp.imag(bb).transpose(0, 2, 1).reshape(1, g * hch, n)
    par = jnp.repeat(pwr[CHUNK - 1::-1][:CHUNK], hch, axis=1)
    pai = jnp.repeat(pwi[CHUNK - 1::-1][:CHUNK], hch, axis=1)
    pc = jnp.concatenate([par * bbr - pai * bbi, par * bbi + pai * bbr], axis=-1)
    pc = pc.reshape(CHUNK, nj, gpt * hch, 2 * n).transpose(1, 0, 2, 3)
    pc = pc.reshape(nj, CHUNK * LANE, 2 * n).astype(BF16)
    ccr = jnp.real(cc).transpose(2, 0, 1).reshape(n, 1, g * hch)
    cci = jnp.imag(cc).transpose(2, 0, 1).reshape(n, 1, g * hch)
    qar = jnp.repeat(pwr[1:CHUNK + 1].transpose(2, 0, 1), hch, axis=2)
    qai = jnp.repeat(pwi[1:CHUNK + 1].transpose(2, 0, 1), hch, axis=2)
    rc = jnp.stack([ccr * qar - cci * qai, -(ccr * qai + cci * qar)], axis=0)
    rc = rc.reshape(2 * n, CHUNK, nj, gpt * hch).transpose(2, 0, 1, 3)
    rc = rc.reshape(nj, 2 * n, CHUNK * LANE).astype(BF16)
    a16 = pw[CHUNK].reshape(nj, 1, gpt * n)
    a16 = jnp.concatenate([jnp.real(a16), jnp.imag(a16)], axis=1)
    bc = jnp.stack([jnp.real(bb), jnp.imag(bb)], axis=0).transpose(1, 3, 0, 2)
    bc = bc.reshape(g * hch, 2 * n).astype(BF16)
    c2 = jnp.stack([jnp.real(cc), -jnp.imag(cc)], axis=0).transpose(0, 3, 1, 2)
    c2 = c2.reshape(2 * n, g * hch).astype(BF16)
    abr = jnp.real(a_bar).reshape(1, g * n)
    abi = jnp.imag(a_bar).reshape(1, g * n)
    return kc, pc, rc, a16, bc, c2, abr, abi


def _full(shape):
    return pl.BlockSpec(shape, lambda *_: (0,) * len(shape))


def kernel(x_prompt, x_sample, state_ssm_re, state_ssm_im, state_conv, meta_tokens, norm_mix, w_in,
           ssm_a_re, ssm_a_im, ssm_log_dt, ssm_b_re, ssm_b_im, ssm_c_re, ssm_c_im, ssm_d, w_glu, b_glu,
           conv_w, norm_out_ssm, norm_out_conv, w_out, norm_ffn, w_router, b_router, w_gate_up,
           b_gate_up, w_down, b_down, norm_final):
    nb, seq, d = x_prompt.shape
    ns = x_sample.shape[0]
    depth, _, g, n = state_ssm_re.shape
    assert depth == 1 and x_sample.shape[1] == 1 and meta_tokens.shape[0] == CHUNK
    cw = conv_w.shape[2]
    nj = cw // LANE
    ne = w_router.shape[2]
    dff = w_down.shape[2]
    nst = g * n
    nbt = nb // 2
    tt = 256
    n_chunks = seq // CHUNK
    tp = nb * seq
    tall = tp + ns
    tm = 128
    rows_p = nbt * tm
    assert rows_p % TOK_TILE == 0 and TOK_TILE % tm == 0 and ns <= TOK_TILE and d % 2 == 0
    n_tiles = tp // TOK_TILE + 1
    ta = n_tiles * TOK_TILE

    kc, pc, rc, a16, bc, c2, abr, abi = _ssm_matrices(
        ssm_a_re[0], ssm_a_im[0], ssm_log_dt[0], ssm_b_re[0], ssm_b_im[0], ssm_c_re[0], ssm_c_im[0])
    win_bf = w_in[0].astype(BF16)
    nmix = norm_mix[0].reshape(1, d)
    nconv = norm_out_conv[0].reshape(1, cw)
    cwt = conv_w[0]

    xsm = jnp.concatenate([x_sample.reshape(ns, d), meta_tokens], axis=0)
    nsm = ns + CHUNK
    s0r = state_ssm_re[0].reshape(ns, nst)
    s0i = state_ssm_im[0].reshape(ns, nst)
    buf0 = state_conv[0, :, 0, :]
    buf1 = state_conv[0, :, 1, :]
    u_sm, z_sm, y_s, ycn_s, sr_s, si_s = pl.pallas_call(
        _small_front_kernel,
        out_shape=(jax.ShapeDtypeStruct((nsm, cw), F32), jax.ShapeDtypeStruct((nsm, cw), F32),
                   jax.ShapeDtypeStruct((ns, cw), F32), jax.ShapeDtypeStruct((ns, cw), BF16),
                   jax.ShapeDtypeStruct((ns, nst), F32), jax.ShapeDtypeStruct((ns, nst), F32)),
        scratch_shapes=[pltpu.VMEM((cw, 2 * nst), BF16), pltpu.VMEM((2 * nst, cw), BF16)],
        compiler_params=_params(None, 56),
        name="small_front",
    )(xsm, nmix, win_bf, s0r, s0i, buf0, buf1, cwt, bc, c2, abr, abi, nconv)
    u_meta = u_sm[ns:]
    z_meta8 = z_sm[ns + CHUNK - 8:]
    new_conv_s = jnp.stack([buf1, z_sm[:ns]], axis=1)[None]
    new_re_s = sr_s.reshape(1, ns, g, n)
    new_im_s = si_s.reshape(1, ns, g, n)

    wch = CHUNK * LANE
    u4c, u_tok, ycn_p, ztail = pl.pallas_call(
        _front_kernel,
        grid=(nb // nbt, seq // tt),
        in_specs=[pl.BlockSpec((nbt, tt, d), lambda b, i: (b, i, 0)),
                  _full((1, d)), _full((d, 4 * cw)), _full((8, cw)), _full((3, cw)), _full((1, cw))],
        out_specs=(pl.BlockSpec((nj, nbt, tt // CHUNK, wch), lambda b, i: (0, b, i, 0)),
                   pl.BlockSpec((nbt, tt, cw), lambda b, i: (b, i, 0)),
                   pl.BlockSpec((nbt, tt, cw), lambda b, i: (b, i, 0)),
                   pl.BlockSpec((nbt, 8, cw), lambda b, i: (b, 0, 0))),
        out_shape=(jax.ShapeDtypeStruct((nj, nb, n_chunks, wch), BF16),
                   jax.ShapeDtypeStruct((nb, seq, cw), BF16),
                   jax.ShapeDtypeStruct((nb, seq, cw), BF16),
                   jax.ShapeDtypeStruct((nb, 8, cw), F32)),
        scratch_shapes=[pltpu.VMEM((nbt, tt + 8, cw), F32), pltpu.VMEM((nj, nbt, tt, LANE), F32)],
        compiler_params=_params(("arbitrary", "arbitrary"), 52),
        name="front",
    )(x_prompt, nmix, win_bf, z_meta8, cwt, nconv)
    new_conv_p = ztail[:, 6:8, :][None]

    cc = n_chunks // 2
    um = u_meta.reshape(CHUNK, nj, LANE).transpose(1, 0, 2).reshape(nj, 1, wch)
    um = jnp.broadcast_to(um, (nj, 8 * nb, wch)).astype(BF16)
    gpt = g // nj
    nstj = 2 * gpt * n
    hch = cw // g
    y4c, s_last = pl.pallas_call(
        _ssm_kernel,
        grid=(nj, n_chunks // cc),
        in_specs=[pl.BlockSpec((1, nb, cc, wch), lambda j, t: (j, 0, t, 0)),
                  pl.BlockSpec((1, 8 * nb, wch), lambda j, t: (j, 0, 0)),
                  pl.BlockSpec((1, (CHUNK + 1) * LANE, hch), lambda j, t: (j, 0, 0)),
                  pl.BlockSpec((1, wch, 2 * n), lambda j, t: (j, 0, 0)),
                  pl.BlockSpec((1, 2 * n, wch), lambda j, t: (j, 0, 0)),
                  pl.BlockSpec((1, 2, nstj // 2), lambda j, t: (j, 0, 0))],
        out_specs=(pl.BlockSpec((1, nb, cc, wch), lambda j, t: (j, 0, t, 0)),
                   pl.BlockSpec((1, nb, 1, nstj), lambda j, t: (j, 0, 0, 0))),
        out_shape=(jax.ShapeDtypeStruct((nj, nb, n_chunks, wch), F32),
                   jax.ShapeDtypeStruct((nj, nb, 1, nstj), F32)),
        scratch_shapes=[pltpu.VMEM((nb, 1, nstj), F32), pltpu.VMEM((nb, cc, nstj), F32),
                        pltpu.VMEM((nb, cc, nstj), F32),
                        pltpu.VMEM((wch, nstj), BF16), pltpu.VMEM((nstj, wch), BF16),
                        pltpu.VMEM((CHUNK // 2, 2 * LANE, 2 * LANE), BF16)],
        compiler_params=_params(("parallel", "arbitrary"), 56),
        name="ssm",
    )(u4c, um, kc, pc, rc, a16)
    sl = s_last.reshape(nj, nb, 2, gpt, n)
    new_re_p = sl[:, :, 0].transpose(1, 0, 2, 3).reshape(1, nb, g, n)
    new_im_p = sl[:, :, 1].transpose(1, 0, 2, 3).reshape(1, nb, g, n)

    dsk = ssm_d[0].reshape(1, cw)
    wglu_bf = w_glu[0].astype(BF16)
    bglu = b_glu[0].reshape(1, cw)
    nssm = norm_out_ssm[0].reshape(1, cw)
    wout_bf = w_out[0].astype(BF16)
    nffn = norm_ffn[0].reshape(1, d)
    wr_pad = jnp.zeros((d, LANE), F32).at[:, :ne].set(w_router[0])
    wr_hi = wr_pad.astype(BF16)
    wr_lo = (wr_pad - wr_hi.astype(F32)).astype(BF16)
    br = b_router[0].reshape(ne, 1)
    mix_w = (dsk, wglu_bf, bglu, nssm, wout_bf, nffn, wr_hi, wr_lo, br)
    mix_w_specs = [_full((1, cw)), _full((cw, cw)), _full((1, cw)), _full((1, cw)), _full((2 * cw, d)),
                   _full((1, d)), _full((d, LANE)), _full((d, LANE)), _full((ne, 1))]
    mix_out_shape = (jax.ShapeDtypeStruct((ta, d), F32), jax.ShapeDtypeStruct((ta, d), BF16),
                     jax.ShapeDtypeStruct((TOP_K, ta), I32), jax.ShapeDtypeStruct((TOP_K, ta), F32),
                     jax.ShapeDtypeStruct((n_tiles, ne, 1), I32))
    tpm = rows_p // TOK_TILE
    nbg = nb // nbt
    x1_all, hf_all, idx_all, gate_all, cnt = pl.pallas_call(
        _mix_kernel_prompt,
        grid=(seq // tm, nbg),
        in_specs=[pl.BlockSpec((nbt, tm, d), lambda i, b: (b, i, 0)),
                  pl.BlockSpec((nj, nbt, tm // CHUNK, wch), lambda i, b: (0, b, i, 0)),
                  pl.BlockSpec((nbt, tm, cw), lambda i, b: (b, i, 0)),
                  pl.BlockSpec((nbt, tm, cw), lambda i, b: (b, i, 0))] + mix_w_specs,
        out_specs=(pl.BlockSpec((rows_p, d), lambda i, b: (i * nbg + b, 0)),
                   pl.BlockSpec((rows_p, d), lambda i, b: (i * nbg + b, 0)),
                   pl.BlockSpec((TOP_K, rows_p), lambda i, b: (0, i * nbg + b)),
                   pl.BlockSpec((TOP_K, rows_p), lambda i, b: (0, i * nbg + b)),
                   pl.BlockSpec((tpm, ne, 1), lambda i, b: (i * nbg + b, 0, 0))),
        out_shape=mix_out_shape,
        scratch_shapes=[pltpu.VMEM((nj, nbt, tm, LANE), F32)],
        compiler_params=_params(("parallel", "parallel"), 52),
        name="mix_prompt",
    )(x_prompt, y4c, u_tok, ycn_p, *mix_w)

    last = n_tiles - 1
    any_spec = pl.BlockSpec(memory_space=pl.ANY)
    x1_all, hf_all, idx_all, gate_all, cnt = pl.pallas_call(
        _mix_kernel_sample,
        grid=(1,),
        in_specs=[_full((1, ns, d)), _full((ns, cw)), _full((1, ns, cw)),
                  _full((1, ns, cw))] + mix_w_specs + [any_spec] * 5,
        out_specs=(pl.BlockSpec((TOK_TILE, d), lambda i: (last, 0)),
                   pl.BlockSpec((TOK_TILE, d), lambda i: (last, 0)),
                   pl.BlockSpec((TOP_K, TOK_TILE), lambda i: (0, last)),
                   pl.BlockSpec((TOP_K, TOK_TILE), lambda i: (0, last)),
                   pl.BlockSpec((1, ne, 1), lambda i: (last, 0, 0))),
        out_shape=mix_out_shape,
        input_output_aliases={13: 0, 14: 1, 15: 2, 16: 3, 17: 4},
        compiler_params=_params(("arbitrary",), 32),
        name="mix_sample",
    )(x_sample.reshape(1, ns, d), y_s, u_sm[:ns].astype(BF16).reshape(1, ns, cw),
      ycn_s.reshape(1, ns, cw), *mix_w, x1_all, hf_all, idx_all, gate_all, cnt)

    bm = MOE_ROWS
    cnt2 = cnt.reshape(n_tiles, ne)
    before = jnp.cumsum(cnt2, axis=0) - cnt2
    count = jnp.sum(cnt2, axis=0)
    padded = ((count + WIN + bm - 1) // bm) * bm
    pend = jnp.cumsum(padded)
    pstart = pend - padded
    phase = before % SUBLANE
    span = jnp.where(cnt2 > 0, phase + cnt2, 0)
    gstart = (pstart[None, :] + before - phase).astype(I32).reshape(-1)
    nwin = ((span + WIN - 1) // WIN).astype(I32)
    reg8 = ((span + SUBLANE - 1) // SUBLANE) * SUBLANE
    loff = (jnp.cumsum(reg8, axis=1) - reg8).astype(I32)
    tail = jnp.where(span % SUBLANE != 0, loff + (span // SUBLANE) * SUBLANE, -1).astype(I32)
    twin = jnp.sum(nwin, axis=1).astype(I32)
    n_blocks = (ta * TOP_K + ne * (WIN + bm - 1) + bm - 1) // bm
    cap = n_blocks * bm
    blk0 = jnp.arange(n_blocks, dtype=I32) * bm
    blk_e = jnp.minimum(jnp.sum((pend[None, :] <= blk0[:, None]).astype(I32), axis=1), ne - 1)
    e_ar = jnp.arange(ne, dtype=I32)
    blk_hot = blk_e[:, None] == e_ar[None, :]

    def _of_block(per_expert):
        return jnp.sum(jnp.where(blk_hot, per_expert[None, :], 0), axis=1)

    blk_valid = jnp.clip(_of_block(count) - (blk0 - _of_block(pstart)), 0, bm).astype(I32)
    has = count > 0
    later = jnp.logical_and(e_ar[None, :] > e_ar[:, None], has[None, :])
    nxt_e = jnp.min(jnp.where(later, e_ar[None, :], ne), axis=1)
    nxt_e = jnp.where(nxt_e < ne, nxt_e, -1)
    ordinal = jnp.cumsum(has.astype(I32)) - 1
    is_first = jnp.logical_and(blk_valid > 0, blk0 == _of_block(pstart))
    blk_first = jnp.where(is_first, jnp.where(_of_block(ordinal) == 0, 2, 1), 0).astype(I32)
    blk_next = _of_block(nxt_e).astype(I32)
    blk_slot = (_of_block(ordinal) % 2).astype(I32)
    nvalid = jnp.full((1,), tall, I32)
    loff_al = (WIN * (jnp.cumsum(nwin, axis=1) - nwin)).astype(I32)

    def _window_list(nw, slots, rows, first_hbm, first_buf):
        wcum = jnp.cumsum(nw, axis=1)
        wslot = jnp.arange(slots, dtype=I32)
        w_hot = jnp.logical_and(wslot[None, :, None] >= (wcum - nw)[:, None, :],
                                wslot[None, :, None] < wcum[:, None, :])

        def _of_window(per_tile_expert):
            return jnp.sum(jnp.where(w_hot, per_tile_expert[:, None, :], 0), axis=2)

        w_in_run = wslot[None, :] - _of_window(wcum - nw)
        return [(_of_window(f) + rows * w_in_run).astype(I32).reshape(-1) for f in (first_hbm, first_buf)]

    w_hbm, w_buf = _window_list(nwin, MAX_WINDOWS, WIN, gstart.reshape(n_tiles, ne), loff)
    nrun_d = _dispatch_run_rows(ne)
    zero_grp, spare_grp = nrun_d - 2 * SUBLANE, nrun_d - SUBLANE
    m_lo = jnp.where(nwin > 0, loff, spare_grp).astype(I32).reshape(-1)
    m_tg = jnp.where(jnp.logical_and(nwin > 0, tail >= 0), tail, zero_grp).astype(I32).reshape(-1)
    m_keep = (nwin == 0).astype(I32).reshape(-1)
    tables_d = (w_hbm, w_buf, m_lo, m_tg, m_keep, twin, nvalid)
    tables_c = (w_hbm, twin, nvalid)
    rowoff_d = (loff + phase).astype(I32).reshape(n_tiles, ne, 1)
    rowoff_c = (loff_al + phase).astype(I32).reshape(n_tiles, ne, 1)
    rgn_d = jnp.stack([loff, loff + reg8], axis=1).astype(I32)
    rgn_c = jnp.stack([loff_al, loff_al + WIN * nwin], axis=1).astype(I32)
    xw = d // 2 + GATE_COLS

    xs = pl.pallas_call(
        _dispatch_kernel,
        grid_spec=pltpu.PrefetchScalarGridSpec(
            num_scalar_prefetch=len(tables_d),
            grid=(n_tiles,),
            in_specs=[pl.BlockSpec((TOK_TILE, d), lambda i, *_: (i, 0)),
                      pl.BlockSpec((TOP_K, TOK_TILE), lambda i, *_: (0, i)),
                      pl.BlockSpec((TOP_K, TOK_TILE), lambda i, *_: (0, i)),
                      pl.BlockSpec((1, ne, 1), lambda i, *_: (i, 0, 0)),
                      pl.BlockSpec((1, 2, ne), lambda i, *_: (i, 0, 0))],
            out_specs=pl.BlockSpec(memory_space=pl.ANY),
            scratch_shapes=[pltpu.VMEM((2, _dispatch_run_rows(ne), xw), U32),
                            pltpu.VMEM((ne, SUBLANE, xw), U32), pltpu.SemaphoreType.DMA((2,))]),
        out_shape=jax.ShapeDtypeStruct((cap, xw), U32),
        compiler_params=_params(("arbitrary",), 40),
        name="dispatch",
    )(*tables_d, hf_all, idx_all, gate_all, rowoff_d, rgn_d)

    yb = pl.pallas_call(
        _moe_kernel,
        grid_spec=pltpu.PrefetchScalarGridSpec(
            num_scalar_prefetch=5,
            grid=(n_blocks,),
            in_specs=[pl.BlockSpec((bm, xw), lambda i, *_: (i, 0)),
                      pl.BlockSpec((1, 1, 2 * dff), lambda i, be, *_: (be[i], 0, 0)),
                      pl.BlockSpec((1, 1, d), lambda i, be, *_: (be[i], 0, 0)),
                      pl.BlockSpec(memory_space=pl.ANY), pl.BlockSpec(memory_space=pl.ANY)],
            out_specs=pl.BlockSpec((bm, d // 2), lambda i, *_: (i, 0)),
            scratch_shapes=[pltpu.VMEM((2, d, 2 * dff), F32), pltpu.VMEM((2, dff, d), F32),
                            pltpu.VMEM((d, 2 * dff), BF16), pltpu.VMEM((dff, d), BF16),
                            pltpu.SemaphoreType.DMA((2, 2))]),
        out_shape=jax.ShapeDtypeStruct((cap, d // 2), U32),
        compiler_params=_params(("arbitrary",), 58),
        name="moe",
    )(blk_e, blk_valid, blk_first, blk_next, blk_slot, xs, b_gate_up[0].reshape(ne, 1, 2 * dff),
      b_down[0].reshape(ne, 1, d), w_gate_up[0], w_down[0])

    nfin = norm_final.reshape(1, d)
    nbh = TOK_TILE // tm

    def _tile_of(s):
        return (s + n_tiles - 1) % n_tiles

    tiles_per_time = nbg * tpm

    def _yp_index(s, *_):
        t = jnp.maximum(s - 1, 0)
        return (t % tiles_per_time, t // tiles_per_time, 0)

    y_p, y_sm = pl.pallas_call(
        _combine_kernel,
        grid_spec=pltpu.PrefetchScalarGridSpec(
            num_scalar_prefetch=len(tables_c),
            grid=(n_tiles,),
            in_specs=[pl.BlockSpec((TOK_TILE, d), lambda s, *_: (_tile_of(s), 0)),
                      pl.BlockSpec((TOP_K, TOK_TILE), lambda s, *_: (0, _tile_of(s))),
                      pl.BlockSpec((1, ne, 1), lambda s, *_: (_tile_of(s), 0, 0)),
                      pl.BlockSpec((1, 2, ne), lambda s, *_: (_tile_of(s), 0, 0)),
                      pl.BlockSpec((1, d), lambda s, *_: (0, 0)),
                      pl.BlockSpec(memory_space=pl.ANY)],
            out_specs=(pl.BlockSpec((nbh, tm, d), _yp_index),
                       pl.BlockSpec((ns, d), lambda s, *_: (0, 0))),
            scratch_shapes=[pltpu.VMEM((2, _combine_run_rows(ne), d // 2), U32),
                            pltpu.SemaphoreType.DMA((2,))]),
        out_shape=(jax.ShapeDtypeStruct((nb, seq, d), F32), jax.ShapeDtypeStruct((ns, d), F32)),
        compiler_params=_params(("arbitrary",), 48),
        name="combine",
    )(*tables_c, x1_all, idx_all, rowoff_c, rgn_c, nfin, yb)

    return (y_p, y_sm.reshape(ns, 1, d), new_re_p, new_im_p, new_conv_p,
            new_re_s, new_im_s, new_conv_s)
```

```python
import math

import jax
import jax.numpy as jnp
from jax import lax
from jax.experimental import pallas as pl
from jax.experimental.pallas import tpu as pltpu

F32 = jnp.float32
BF16 = jnp.bfloat16
U32 = jnp.uint32
I32 = jnp.int32
EPS = 1e-5
CHUNK = 16
LANE = 128
TOP_K = 4
SWIGLU_LIMIT = 7.0
SWIGLU_ALPHA = 1.702
MOE_ROWS = 1024
MOE_SUB = 256
MOE_CHAIN = 512
TOK_TILE = 256
WIN = 32
SUBLANE = 8
GATE_COLS = LANE


MAX_WINDOWS = 72


def _dispatch_run_rows(ne):
    return TOP_K * TOK_TILE + ne * 2 * (SUBLANE - 1) + WIN


def _combine_run_rows(ne):
    return -(-(TOP_K * TOK_TILE + ne * (SUBLANE - 1 + WIN - 1)) // WIN) * WIN
HI_MASK = 0xFFFF0000
MIB = 1024 * 1024


def _rms(x, g):
    return x * lax.rsqrt(jnp.mean(x * x, axis=-1, keepdims=True) + EPS) * g


def _gelu_tanh(x):
    c = math.sqrt(2.0 / math.pi)
    return 0.5 * x * (1.0 + jnp.tanh(c * (x + 0.044715 * (x * x * x))))


def _params(sem, vmem_mib):
    return pltpu.CompilerParams(dimension_semantics=sem, vmem_limit_bytes=vmem_mib * MIB)


def _pack_pairs(a, b):
    return (pltpu.bitcast(a, U32) >> 16) | (pltpu.bitcast(b, U32) & jnp.uint32(HI_MASK))


def _unpack_pairs(w):
    lo = pltpu.bitcast(w << 16, F32)
    hi = pltpu.bitcast(w & jnp.uint32(HI_MASK), F32)
    return jnp.concatenate([lo, hi], axis=-1).astype(BF16)


def _iota2(shape, axis):
    return lax.broadcasted_iota(I32, shape, axis)


def _expand_cols(compact, reps_log2, n_log2):
    q = _iota2((compact.shape[1], 2 << (reps_log2 + n_log2)), 0)
    c = _iota2((compact.shape[1], 2 << (reps_log2 + n_log2)), 1)
    nmask = (1 << n_log2) - 1
    same = jnp.logical_and((q >> n_log2) == (c >> (reps_log2 + n_log2)), (q & nmask) == (c & nmask))
    return jnp.dot(compact, jnp.where(same, 1.0, 0.0).astype(BF16), preferred_element_type=F32)


def _expand_rows(compact, reps_log2, n_log2):
    r = _iota2((2 << (reps_log2 + n_log2), compact.shape[0]), 0)
    q = _iota2((2 << (reps_log2 + n_log2), compact.shape[0]), 1)
    nmask = (1 << n_log2) - 1
    same = jnp.logical_and((r >> (reps_log2 + n_log2)) == (q >> n_log2), (r & nmask) == (q & nmask))
    return jnp.dot(jnp.where(same, 1.0, 0.0).astype(BF16), compact, preferred_element_type=F32)


def _group_mask(shape, row_shift, col_shift, ngroups):
    r = _iota2(shape, 0)
    c = _iota2(shape, 1)
    return ((r >> row_shift) & (ngroups - 1)) == ((c >> col_shift) & (ngroups - 1))


def _small_front_kernel(x_ref, nmix_ref, win_ref, s0r_ref, s0i_ref, b0_ref, b1_ref, cw_ref,
                        bc_ref, cc_ref, abr_ref, abi_ref, nconv_ref,
                        u_ref, z_ref, y_ref, ycn_ref, sr_ref, si_ref, bdb_ref, cm_ref):
    ns, nst = s0r_ref.shape
    cw = u_ref.shape[1]
    n = bc_ref.shape[1] // 2
    nlog = n.bit_length() - 1
    glog = (nst // n).bit_length() - 1
    hlog = (cw >> glog).bit_length() - 1
    bdb_ref[...] = jnp.where(_group_mask(bdb_ref.shape, hlog, nlog, 1 << glog),
                             _expand_cols(bc_ref[...], glog, nlog), 0.0).astype(BF16)
    cm_ref[...] = jnp.where(_group_mask(cm_ref.shape, nlog, hlog, 1 << glog),
                            _expand_rows(cc_ref[...], glog, nlog), 0.0).astype(BF16)
    h = _rms(x_ref[...], nmix_ref[...]).astype(BF16)
    proj = jnp.dot(h, win_ref[...], preferred_element_type=F32)
    u = proj[:, 0:cw]
    zc = proj[:, cw:2 * cw]
    gb = proj[:, 2 * cw:3 * cw]
    gc = proj[:, 3 * cw:4 * cw]
    z = gc * zc
    u_ref[...] = u
    z_ref[...] = z
    bu = jnp.dot(u[:ns].astype(BF16), bdb_ref[...], preferred_element_type=F32)
    abr = abr_ref[...]
    abi = abi_ref[...]
    s0r = s0r_ref[...]
    s0i = s0i_ref[...]
    sr = abr * s0r - abi * s0i + bu[:, :nst]
    si = abr * s0i + abi * s0r + bu[:, nst:]
    sr_ref[...] = sr
    si_ref[...] = si
    scat = jnp.concatenate([sr, si], axis=-1).astype(BF16)
    y_ref[...] = jnp.dot(scat, cm_ref[...], preferred_element_type=F32)
    conv = cw_ref[0:1, :] * b0_ref[...] + cw_ref[1:2, :] * b1_ref[...] + cw_ref[2:3, :] * z[:ns]
    ycn_ref[...] = _rms(gb[:ns] * conv, nconv_ref[...]).astype(BF16)


def _front_kernel(x_ref, nmix_ref, win_ref, zm_ref, cw_ref, nconv_ref,
                  uc_ref, ut_ref, ycn_ref, zt_ref, zbuf, ubuf):
    i = pl.program_id(1)
    nb, tt, d = x_ref.shape
    cw = ycn_ref.shape[2]
    rows = nb * tt
    ncz = tt // CHUNK

    @pl.when(i == 0)
    def _():
        zbuf[:, 0:8, :] = jnp.broadcast_to(zm_ref[...][None], (nb, 8, cw))

    h = _rms(x_ref[...].reshape(rows, d), nmix_ref[...]).astype(BF16)
    u = jnp.dot(h, win_ref[:, 0:cw], preferred_element_type=F32)
    ut_ref[...] = u.astype(BF16).reshape(nb, tt, cw)
    for j in range(cw // LANE):
        ubuf[j] = u[:, j * LANE:(j + 1) * LANE].reshape(nb, tt, LANE)
    for s in range(CHUNK):
        for j in range(cw // LANE):
            piece = ubuf[j, :, pl.ds(s, ncz, stride=CHUNK), :]
            uc_ref[j, :, :, s * LANE:(s + 1) * LANE] = piece.astype(BF16)
    zc = jnp.dot(h, win_ref[:, cw:2 * cw], preferred_element_type=F32)
    gc = jnp.dot(h, win_ref[:, 3 * cw:4 * cw], preferred_element_type=F32)
    z3 = (gc * zc).reshape(nb, tt, cw)
    zbuf[:, 8:8 + tt, :] = z3
    z1 = zbuf[:, 7:7 + tt, :]
    z2 = zbuf[:, 6:6 + tt, :]
    conv = cw_ref[0:1, :] * z2 + cw_ref[1:2, :] * z1 + cw_ref[2:3, :] * z3
    gb = jnp.dot(h, win_ref[:, 2 * cw:3 * cw], preferred_element_type=F32)
    yc = gb * conv.reshape(rows, cw)
    ycn_ref[...] = _rms(yc, nconv_ref[...]).astype(BF16).reshape(nb, tt, cw)
    tail = zbuf[:, tt:tt + 8, :]
    zt_ref[...] = tail
    zbuf[:, 0:8, :] = tail


def _ssm_kernel(u_ref, um_ref, kc_ref, pc_ref, rc_ref, a16_ref, y_ref, sl_ref,
                s_carry, ds_ref, sp_ref, p_s, r_s, t_s):
    th = pl.program_id(1)
    _, nb, cc, w = u_ref.shape
    nst = p_s.shape[1]
    half = nst // 2
    rows = nb * cc
    blk = 2 * LANE
    u = u_ref[0].reshape(rows, w)

    @pl.when(th == 0)
    def _():
        hch = kc_ref.shape[2]
        gpt = LANE // hch
        hlog = hch.bit_length() - 1
        glog = gpt.bit_length() - 1
        nlog = (pc_ref.shape[2] // 2).bit_length() - 1
        p_s[...] = jnp.where(_group_mask(p_s.shape, hlog, nlog, gpt),
                             _expand_cols(pc_ref[0], glog, nlog), 0.0).astype(BF16)
        r_s[...] = jnp.where(_group_mask(r_s.shape, nlog, hlog, gpt),
                             _expand_rows(rc_ref[0], glog, nlog), 0.0).astype(BF16)
        nlag = kc_ref.shape[1] // LANE
        o = _iota2((hch, LANE), 0)
        c = _iota2((hch, LANE), 1)
        spread = jnp.where((c & (hch - 1)) == o, 1.0, 0.0).astype(BF16)
        lagm = jnp.dot(kc_ref[0], spread, preferred_element_type=F32)
        r = _iota2(lagm.shape, 0)
        c = _iota2(lagm.shape, 1)
        lagm = jnp.where(((r >> hlog) & (gpt - 1)) == (c >> hlog), lagm, 0.0).astype(BF16)
        for dlt in range(nlag // 2):
            b0 = lagm[(2 * dlt) * LANE:(2 * dlt + 1) * LANE]
            b1 = lagm[(2 * dlt + 1) * LANE:(2 * dlt + 2) * LANE]
            b2 = lagm[(2 * dlt + 2) * LANE:(2 * dlt + 3) * LANE]
            t_s[dlt, 0:LANE, 0:LANE] = b1
            t_s[dlt, 0:LANE, LANE:blk] = b2
            t_s[dlt, LANE:blk, 0:LANE] = b0
            t_s[dlt, LANE:blk, LANE:blk] = b1
        ds_ref[:, 0:8, :] = jnp.dot(um_ref[0], p_s[...], preferred_element_type=F32).reshape(nb, 8, nst)
        s_carry[...] = ds_ref[:, 0:1, :]

    ds_ref[...] = jnp.dot(u, p_s[...], preferred_element_type=F32).reshape(nb, cc, nst)
    for tb in range(w // blk):
        acc = jnp.dot(u[:, 0:blk], t_s[tb], preferred_element_type=F32)
        for sb in range(1, tb + 1):
            acc = acc + jnp.dot(u[:, sb * blk:(sb + 1) * blk], t_s[tb - sb],
                                preferred_element_type=F32)
        y_ref[0, :, :, tb * blk:(tb + 1) * blk] = acc.reshape(nb, cc, blk)
    ar = a16_ref[0, 0:1, :].reshape(1, 1, half)
    ai = a16_ref[0, 1:2, :].reshape(1, 1, half)
    sr = s_carry[:, :, 0:half]
    si = s_carry[:, :, half:nst]
    for c in range(cc):
        sp_ref[:, c:c + 1, 0:half] = sr
        sp_ref[:, c:c + 1, half:nst] = si
        dr = ds_ref[:, c:c + 1, 0:half]
        di = ds_ref[:, c:c + 1, half:nst]
        sr, si = ar * sr - ai * si + dr, ar * si + ai * sr + di
    s_carry[:, :, 0:half] = sr
    s_carry[:, :, half:nst] = si
    sl_ref[0, :, :, 0:half] = sr
    sl_ref[0, :, :, half:nst] = si

    sp = sp_ref[...].reshape(rows, nst).astype(BF16)
    for tb in range(w // blk):
        acc = jnp.dot(sp, r_s[:, tb * blk:(tb + 1) * blk], preferred_element_type=F32)
        y_ref[0, :, :, tb * blk:(tb + 1) * blk] += acc.reshape(nb, cc, blk)


N_MIX_IN = 12


def _mix_rows(x, yssm, ut, ycn, dsk_ref, wglu_ref, bglu_ref, nssm_ref, wout_ref,
              nffn_ref, wr_ref, br_ref):
    ne = br_ref.shape[0]
    y = _gelu_tanh(yssm + dsk_ref[...] * ut.astype(F32))
    glu = jnp.dot(y.astype(BF16), wglu_ref[...], preferred_element_type=F32) + bglu_ref[...]
    o = y * jax.nn.sigmoid(glu)
    ysn = _rms(o, nssm_ref[...]).astype(BF16)
    mix = jnp.concatenate([ysn, ycn], axis=-1)
    x1 = x + jnp.dot(mix, wout_ref[...], preferred_element_type=F32)
    hf = _rms(x1, nffn_ref[...])
    hf_hi = hf.astype(BF16)
    hf_lo = (hf - hf_hi.astype(F32)).astype(BF16)
    r = hf.shape[0]
    part = jnp.dot(jnp.concatenate([hf_hi, hf_lo], axis=0), wr_ref[...], preferred_element_type=F32)
    logits = (part[0:r, 0:LANE] + part[0:r, LANE:2 * LANE]) + (part[r:2 * r, 0:LANE] + part[r:2 * r, LANE:2 * LANE])
    lt = logits.T[0:ne, :] + br_ref[...]
    iota = lax.broadcasted_iota(I32, lt.shape, 0)
    vals, idxs = [], []
    sel = jnp.zeros(lt.shape, F32)
    for _ in range(TOP_K):
        m = jnp.max(lt, axis=0, keepdims=True)
        ik = jnp.min(jnp.where(lt == m, iota, ne), axis=0, keepdims=True)
        vals.append(m)
        idxs.append(ik)
        hit = iota == ik
        sel = sel + jnp.where(hit, 1.0, 0.0)
        lt = jnp.where(hit, -jnp.inf, lt)
    es = [jnp.exp(v - vals[0]) for v in vals]
    tot = es[0] + es[1] + es[2] + es[3]
    idx = jnp.concatenate(idxs, axis=0)
    gates = jnp.concatenate([e / tot for e in es], axis=0)
    return x1, hf_hi, idx, gates, sel


def _mix_kernel_prompt(*refs):
    x_ref, yc_ref = refs[0], refs[1]
    x1_ref, hf_ref, idx_ref, gate_ref, cnt_ref, ybuf = refs[N_MIX_IN:]
    nj, nb, ncz, _ = yc_ref.shape
    for s in range(CHUNK):
        for j in range(nj):
            ybuf[j, :, pl.ds(s, ncz, stride=CHUNK), :] = yc_ref[j, :, :, s * LANE:(s + 1) * LANE]
    ut_ref, ycn_ref = refs[2], refs[3]
    tt, d = x_ref.shape[1], x_ref.shape[2]
    cw = nj * LANE
    nbc = TOK_TILE // tt
    for t in range(cnt_ref.shape[0]):
        b0, r0 = t * nbc, t * TOK_TILE
        yssm = jnp.concatenate([ybuf[j, b0:b0 + nbc].reshape(TOK_TILE, LANE) for j in range(nj)], axis=-1)
        x1, hf, idx, gates, sel = _mix_rows(
            x_ref[b0:b0 + nbc].reshape(TOK_TILE, d), yssm, ut_ref[b0:b0 + nbc].reshape(TOK_TILE, cw),
            ycn_ref[b0:b0 + nbc].reshape(TOK_TILE, cw), *refs[4:N_MIX_IN])
        x1_ref[r0:r0 + TOK_TILE, :] = x1
        hf_ref[r0:r0 + TOK_TILE, :] = hf
        idx_ref[:, r0:r0 + TOK_TILE] = idx
        gate_ref[:, r0:r0 + TOK_TILE] = gates
        cnt_ref[t] = jnp.sum(sel, axis=1, keepdims=True).astype(I32)


def _mix_kernel_sample(*refs):
    x1, hf, idx, gates, sel = _mix_rows(refs[0][0], refs[1][...], refs[2][0], refs[3][0],
                                        *refs[4:N_MIX_IN])
    x1_ref, hf_ref, idx_ref, gate_ref, cnt_ref = refs[N_MIX_IN + 5:]
    ns = x1.shape[0]
    x1_ref[...] = jnp.zeros(x1_ref.shape, x1_ref.dtype)
    hf_ref[...] = jnp.zeros(hf_ref.shape, hf_ref.dtype)
    idx_ref[...] = jnp.zeros(idx_ref.shape, idx_ref.dtype)
    gate_ref[...] = jnp.zeros(gate_ref.shape, gate_ref.dtype)
    x1_ref[0:ns, :] = x1
    hf_ref[0:ns, :] = hf
    idx_ref[:, 0:ns] = idx
    gate_ref[:, 0:ns] = gates
    cnt_ref[0] = jnp.sum(sel, axis=1, keepdims=True).astype(I32)


def _split_bf16(x, parts):
    out = []
    for _ in range(parts - 1):
        p = x.astype(BF16)
        out.append(p)
        x = x - p.astype(F32)
    out.append(x.astype(BF16))
    return out


def _run_onehot(idx_ref, loff_ref, rgn_ref, tile, n_valid, nrun, gate_ref=None):
    ne = loff_ref.shape[1]
    tt = idx_ref.shape[1]
    e_iota = lax.broadcasted_iota(I32, (ne, tt), 0)
    tok = tile * tt + lax.broadcasted_iota(I32, (1, tt), 1)
    valid = tok < n_valid
    hits = [jnp.logical_and(e_iota == idx_ref[k:k + 1, :], valid) for k in range(TOP_K)]
    sel = jnp.zeros((ne, tt), F32)
    for h in hits:
        sel = sel + jnp.where(h, 1.0, 0.0)
    before = lax.broadcasted_iota(I32, (tt, tt), 0) < lax.broadcasted_iota(I32, (tt, tt), 1)
    tri = jnp.where(before, 1.0, 0.0).astype(BF16)
    base = jnp.dot(sel.astype(BF16), tri, preferred_element_type=F32) + loff_ref[0].astype(F32)
    base = jnp.where(sel > 0.0, base + 1.0, 0.0)
    b_hi = jnp.floor(base * (1.0 / 256.0))
    b_lo = base - 256.0 * b_hi
    r_i = lax.broadcasted_iota(I32, (nrun, ne), 0)
    own = jnp.logical_and(r_i >= rgn_ref[0, 0:1, :], r_i < rgn_ref[0, 1:2, :])
    own_bf = jnp.where(own, 1.0, 0.0).astype(BF16)
    want = (256.0 * jnp.dot(own_bf, b_hi.astype(BF16), preferred_element_type=F32)
            + jnp.dot(own_bf, b_lo.astype(BF16), preferred_element_type=F32))
    r_f = (lax.broadcasted_iota(I32, (nrun, tt), 0) + 1).astype(F32)
    smat = jnp.where(want == r_f, 1.0, 0.0).astype(BF16)
    if gate_ref is None:
        return smat
    gate_e = jnp.zeros((ne, tt), F32)
    for k, h in enumerate(hits):
        gate_e = gate_e + jnp.where(h, gate_ref[k:k + 1, :], 0.0)
    nparts = 3
    pieces = jnp.concatenate(_split_bf16(gate_e, nparts), axis=0)
    per = lax.dot_general(smat, pieces, (((1,), (1,)), ((), ())), preferred_element_type=F32)
    lo3 = jnp.concatenate([rgn_ref[0, 0:1, :]] * nparts, axis=1)
    hi3 = jnp.concatenate([rgn_ref[0, 1:2, :]] * nparts, axis=1)
    r_i3 = lax.broadcasted_iota(I32, (nrun, nparts * ne), 0)
    own3 = jnp.logical_and(r_i3 >= lo3, r_i3 < hi3)
    gcol = jnp.sum(jnp.where(own3, per, 0.0), axis=1, keepdims=True)
    return smat, gcol


def _window_copy(buf, slot, hbm, lo, g, sem, to_hbm, rows, align):
    src = buf.at[slot, pl.ds(pl.multiple_of(lo, align), rows)]
    dst = hbm.at[pl.ds(pl.multiple_of(g, align), rows)]
    if to_hbm:
        return pltpu.make_async_copy(src, dst, sem.at[slot])
    return pltpu.make_async_copy(dst, src, sem.at[slot])


def _start_windows(wg_ref, wl_ref, tw_ref, tile, slots, buf, slot, hbm, sem, to_hbm, rows, align):
    def per_window(w, c):
        k = tile * slots + w
        lo = w * rows if wl_ref is None else wl_ref[k]
        _window_copy(buf, slot, hbm, lo, wg_ref[k], sem, to_hbm, rows, align).start()
        return c

    lax.fori_loop(0, tw_ref[tile], per_window, 0)


def _wait_windows(count, buf, slot, hbm, sem, to_hbm, rows, align):
    batch = 8

    def many(w, c):
        _window_copy(buf, slot, hbm, 0, 0, sem, to_hbm, batch * rows, align).wait()
        return c

    def one(w, c):
        _window_copy(buf, slot, hbm, 0, 0, sem, to_hbm, rows, align).wait()
        return c

    lax.fori_loop(0, count // batch, many, 0)
    lax.fori_loop(0, count % batch, one, 0)


def _dispatch_kernel(wg_ref, wl_ref, mlo_ref, mtg_ref, keep_ref, tw_ref, nv_ref,
                     hf_ref, idx_ref, gate_ref, loff_ref, rgn_ref, xs_ref, buf, carry, sem):
    i = pl.program_id(0)
    nt = pl.num_programs(0)
    ne = loff_ref.shape[1]
    tt, d = hf_ref.shape
    nrun = buf.shape[1]
    slot = i % 2

    @pl.when(i == 0)
    def _():
        carry[...] = jnp.zeros(carry.shape, carry.dtype)

    smat, gcol = _run_onehot(idx_ref, loff_ref, rgn_ref, i, nv_ref[0], nrun, gate_ref)
    xr = jnp.dot(smat, hf_ref[...], preferred_element_type=F32)
    lane0 = lax.broadcasted_iota(I32, (nrun, GATE_COLS), 1) == 0
    buf[slot, :, 0:d // 2] = _pack_pairs(xr[:, :d // 2], xr[:, d // 2:])
    buf[slot, :, d // 2:] = pltpu.bitcast(jnp.where(lane0, gcol, 0.0), U32)

    def merge(e, c):
        k = i * ne + e
        lo = pl.multiple_of(mlo_ref[k], SUBLANE)
        buf[slot, pl.ds(lo, SUBLANE), :] = buf[slot, pl.ds(lo, SUBLANE), :] | carry[e]
        tg = pl.multiple_of(mtg_ref[k], SUBLANE)
        carry[e] = jnp.where(keep_ref[k] > 0, carry[e], buf[slot, pl.ds(tg, SUBLANE), :])
        return c

    lax.fori_loop(0, ne, merge, 0, unroll=4)

    @pl.when(i > 0)
    def _():
        _wait_windows(tw_ref[i - 1], buf, 1 - slot, xs_ref, sem, True, WIN, SUBLANE)

    _start_windows(wg_ref, wl_ref, tw_ref, i, MAX_WINDOWS, buf, slot, xs_ref, sem, True, WIN, SUBLANE)

    @pl.when(i == nt - 1)
    def _():
        _wait_windows(tw_ref[i], buf, slot, xs_ref, sem, True, WIN, SUBLANE)


def _expert_weight_copies(wg_hbm, wd_hbm, wg_f32, wd_f32, sem, e, slot):
    return (pltpu.make_async_copy(wg_hbm.at[e], wg_f32.at[slot], sem.at[0, slot]),
            pltpu.make_async_copy(wd_hbm.at[e], wd_f32.at[slot], sem.at[1, slot]))


def _moe_kernel(be_ref, bv_ref, first_ref, nxt_ref, slot_ref,
                x_ref, bg_ref, bd_ref, wg_hbm, wd_hbm, y_ref,
                wg_f32, wd_f32, wg_bf, wd_bf, sem):
    i = pl.program_id(0)
    e = be_ref[i]
    dff = wd_bf.shape[0]
    bm = x_ref.shape[0]
    sub = MOE_SUB
    nw = x_ref.shape[1] - GATE_COLS

    @pl.when(first_ref[i] > 0)
    def _():
        slot = slot_ref[i]

        @pl.when(first_ref[i] > 1)
        def _():
            for cp in _expert_weight_copies(wg_hbm, wd_hbm, wg_f32, wd_f32, sem, e, slot):
                cp.start()

        for cp in _expert_weight_copies(wg_hbm, wd_hbm, wg_f32, wd_f32, sem, e, slot):
            cp.wait()

        @pl.when(nxt_ref[i] >= 0)
        def _():
            for cp in _expert_weight_copies(wg_hbm, wd_hbm, wg_f32, wd_f32, sem, nxt_ref[i], 1 - slot):
                cp.start()

        wg_bf[...] = wg_f32[slot].astype(BF16)
        wd_bf[...] = wd_f32[slot].astype(BF16)

    def rows(r0, nrows):
        sizes = [MOE_CHAIN] * (nrows // MOE_CHAIN) + ([sub] if nrows % MOE_CHAIN else [])
        lo = r0
        for size in sizes:
            _chain(lo, size)
            lo += size

    def _chain(lo, n):
        live = lax.broadcasted_iota(I32, (n, 1), 0) + lo < bv_ref[i]
        x = jnp.where(live, _unpack_pairs(x_ref[lo:lo + n, 0:nw]), jnp.zeros((), BF16))
        route = jnp.where(live, pltpu.bitcast(x_ref[lo:lo + n, nw:], F32)[:, 0:1], 0.0)
        gu = jnp.dot(x, wg_bf[...], preferred_element_type=F32) + bg_ref[0]
        gate = jnp.minimum(gu[:, :dff], SWIGLU_LIMIT)
        up = jnp.clip(gu[:, dff:], -SWIGLU_LIMIT, SWIGLU_LIMIT)
        h = gate * jax.nn.sigmoid(SWIGLU_ALPHA * gate) * (up + 1.0)
        y = jnp.dot(h.astype(BF16), wd_bf[...], preferred_element_type=F32) + bd_ref[0]
        yr = (route * y).astype(BF16).astype(F32)
        half = yr.shape[1] // 2
        y_ref[lo:lo + n, :] = _pack_pairs(yr[:, :half], yr[:, half:])

    nchains = bm // sub
    for live_chains in range(nchains + 1):
        lo_rows, hi_rows = (live_chains - 1) * sub, live_chains * sub

        @pl.when(jnp.logical_and(bv_ref[i] > lo_rows, bv_ref[i] <= hi_rows) if live_chains
                 else bv_ref[i] <= 0)
        def _(used=hi_rows):
            if used:
                rows(0, used)
            if used < bm:
                y_ref[used:bm, :] = jnp.zeros((bm - used, y_ref.shape[1]), y_ref.dtype)


def _combine_tile(step, nt):
    return (step + nt - 1) % nt


def _combine_kernel(wg_ref, tw_ref, nv_ref,
                    x1_ref, idx_ref, loff_ref, rgn_ref, nf_ref, yb_ref,
                    yp_ref, ys_ref, buf, sem):
    s = pl.program_id(0)
    nt = pl.num_programs(0)
    tt, d = x1_ref.shape
    nrun = buf.shape[1]
    tile = _combine_tile(s, nt)
    slot = s % 2

    @pl.when(s == 0)
    def _():
        buf[...] = jnp.zeros(buf.shape, buf.dtype)
        _start_windows(wg_ref, None, tw_ref, tile, MAX_WINDOWS, buf, slot, yb_ref, sem, False,
                       WIN, SUBLANE)

    @pl.when(s + 1 < nt)
    def _():
        _start_windows(wg_ref, None, tw_ref, _combine_tile(s + 1, nt), MAX_WINDOWS, buf, 1 - slot,
                       yb_ref, sem, False, WIN, SUBLANE)

    smat = _run_onehot(idx_ref, loff_ref, rgn_ref, tile, nv_ref[0], nrun)

    _wait_windows(tw_ref[tile], buf, slot, yb_ref, sem, False, WIN, SUBLANE)
    yrun = _unpack_pairs(buf[slot])
    moe = lax.dot_general(smat, yrun, (((0,), (0,)), ((), ())), preferred_element_type=F32)
    out = _rms(x1_ref[...] + moe, nf_ref[...])

    @pl.when(s == 0)
    def _():
        ys_ref[...] = out[0:ys_ref.shape[0], :]

    @pl.when(s > 0)
    def _():
        yp_ref[...] = out.reshape(yp_ref.shape)


def _ssm_matrices(a_re, a_im, log_dt, b_re, b_im, c_re, c_im):
    g, n = a_re.shape
    hch = b_re.shape[2]
    gpt = LANE // hch
    nj = g // gpt
    a = lax.complex(a_re, a_im)
    dta = a * jnp.exp(log_dt)[:, None]
    a_bar = jnp.exp(dta)
    bb = ((a_bar - 1.0) / a)[:, :, None] * lax.complex(b_re, b_im)
    cc = lax.complex(c_re, c_im)
    ks = jnp.arange(CHUNK + 1, dtype=F32)
    pw = jnp.exp(dta[None] * ks[:, None, None])
    kk = jnp.real(jnp.einsum('gon,kgn,gni->kgio', cc, pw[:CHUNK], bb))
    kk = jnp.concatenate([jnp.zeros_like(kk[:1]), kk], axis=0)
    kc = kk.reshape(CHUNK + 1, nj, gpt * hch, hch).transpose(1, 0, 2, 3)
    kc = kc.reshape(nj, (CHUNK + 1) * LANE, hch).astype(BF16)
    pwr, pwi = jnp.real(pw), jnp.imag(pw)
    bbr = jnp.real(bb).transpose(0, 2, 1).reshape(1, g * hch, n)
    bbi = jnp.imag(bb).transpose(0, 2, 1).reshape(1, g * hch, n)
    par = jnp.repeat(pwr[CHUNK - 1::-1][:CHUNK], hch, axis=1)
    pai = jnp.repeat(pwi[CHUNK - 1::-1][:CHUNK], hch, axis=1)
    pc = jnp.concatenate([par * bbr - pai * bbi, par * bbi + pai * bbr], axis=-1)
    pc = pc.reshape(CHUNK, nj, gpt * hch, 2 * n).transpose(1, 0, 2, 3)
    pc = pc.reshape(nj, CHUNK * LANE, 2 * n).astype(BF16)
    ccr = jnp.real(cc).transpose(2, 0, 1).reshape(n, 1, g * hch)
    cci = jnp.imag(cc).transpose(2, 0, 1).reshape(n, 1, g * hch)
    qar = jnp.repeat(pwr[1:CHUNK + 1].transpose(2, 0, 1), hch, axis=2)
    qai = jnp.repeat(pwi[1:CHUNK + 1].transpose(2, 0, 1), hch, axis=2)
    rc = jnp.stack([ccr * qar - cci * qai, -(ccr * qai + cci * qar)], axis=0)
    rc = rc.reshape(2 * n, CHUNK, nj, gpt * hch).transpose(2, 0, 1, 3)
    rc = rc.reshape(nj, 2 * n, CHUNK * LANE).astype(BF16)
    a16 = pw[CHUNK].reshape(nj, 1, gpt * n)
    a16 = jnp.concatenate([jnp.real(a16), jnp.imag(a16)], axis=1)
    bc = jnp.stack([jnp.real(bb), jnp.imag(bb)], axis=0).transpose(1, 3, 0, 2)
    bc = bc.reshape(g * hch, 2 * n).astype(BF16)
    c2 = jnp.stack([jnp.real(cc), -jnp.imag(cc)], axis=0).transpose(0, 3, 1, 2)
    c2 = c2.reshape(2 * n, g * hch).astype(BF16)
    abr = jnp.real(a_bar).reshape(1, g * n)
    abi = jnp.imag(a_bar).reshape(1, g * n)
    return kc, pc, rc, a16, bc, c2, abr, abi


def _full(shape):
    return pl.BlockSpec(shape, lambda *_: (0,) * len(shape))


def kernel(x_prompt, x_sample, state_ssm_re, state_ssm_im, state_conv, meta_tokens, norm_mix, w_in,
           ssm_a_re, ssm_a_im, ssm_log_dt, ssm_b_re, ssm_b_im, ssm_c_re, ssm_c_im, ssm_d, w_glu, b_glu,
           conv_w, norm_out_ssm, norm_out_conv, w_out, norm_ffn, w_router, b_router, w_gate_up,
           b_gate_up, w_down, b_down, norm_final):
    nb, seq, d = x_prompt.shape
    ns = x_sample.shape[0]
    depth, _, g, n = state_ssm_re.shape
    assert depth == 1 and x_sample.shape[1] == 1 and meta_tokens.shape[0] == CHUNK
    cw = conv_w.shape[2]
    nj = cw // LANE
    ne = w_router.shape[2]
    dff = w_down.shape[2]
    nst = g * n
    nbt = nb // 2
    tt = 256
    n_chunks = seq // CHUNK
    tp = nb * seq
    tall = tp + ns
    tm = 128
    rows_p = nbt * tm
    assert rows_p % TOK_TILE == 0 and TOK_TILE % tm == 0 and ns <= TOK_TILE and d % 2 == 0
    n_tiles = tp // TOK_TILE + 1
    ta = n_tiles * TOK_TILE

    kc, pc, rc, a16, bc, c2, abr, abi = _ssm_matrices(
        ssm_a_re[0], ssm_a_im[0], ssm_log_dt[0], ssm_b_re[0], ssm_b_im[0], ssm_c_re[0], ssm_c_im[0])
    win_bf = w_in[0].astype(BF16)
    nmix = norm_mix[0].reshape(1, d)
    nconv = norm_out_conv[0].reshape(1, cw)
    cwt = conv_w[0]

    xsm = jnp.concatenate([x_sample.reshape(ns, d), meta_tokens], axis=0)
    nsm = ns + CHUNK
    s0r = state_ssm_re[0].reshape(ns, nst)
    s0i = state_ssm_im[0].reshape(ns, nst)
    buf0 = state_conv[0, :, 0, :]
    buf1 = state_conv[0, :, 1, :]
    u_sm, z_sm, y_s, ycn_s, sr_s, si_s = pl.pallas_call(
        _small_front_kernel,
        out_shape=(jax.ShapeDtypeStruct((nsm, cw), F32), jax.ShapeDtypeStruct((nsm, cw), F32),
                   jax.ShapeDtypeStruct((ns, cw), F32), jax.ShapeDtypeStruct((ns, cw), BF16),
                   jax.ShapeDtypeStruct((ns, nst), F32), jax.ShapeDtypeStruct((ns, nst), F32)),
        scratch_shapes=[pltpu.VMEM((cw, 2 * nst), BF16), pltpu.VMEM((2 * nst, cw), BF16)],
        compiler_params=_params(None, 56),
        name="small_front",
    )(xsm, nmix, win_bf, s0r, s0i, buf0, buf1, cwt, bc, c2, abr, abi, nconv)
    u_meta = u_sm[ns:]
    z_meta8 = z_sm[ns + CHUNK - 8:]
    new_conv_s = jnp.stack([buf1, z_sm[:ns]], axis=1)[None]
    new_re_s = sr_s.reshape(1, ns, g, n)
    new_im_s = si_s.reshape(1, ns, g, n)

    wch = CHUNK * LANE
    u4c, u_tok, ycn_p, ztail = pl.pallas_call(
        _front_kernel,
        grid=(nb // nbt, seq // tt),
        in_specs=[pl.BlockSpec((nbt, tt, d), lambda b, i: (b, i, 0)),
                  _full((1, d)), _full((d, 4 * cw)), _full((8, cw)), _full((3, cw)), _full((1, cw))],
        out_specs=(pl.BlockSpec((nj, nbt, tt // CHUNK, wch), lambda b, i: (0, b, i, 0)),
                   pl.BlockSpec((nbt, tt, cw), lambda b, i: (b, i, 0)),
                   pl.BlockSpec((nbt, tt, cw), lambda b, i: (b, i, 0)),
                   pl.BlockSpec((nbt, 8, cw), lambda b, i: (b, 0, 0))),
        out_shape=(jax.ShapeDtypeStruct((nj, nb, n_chunks, wch), BF16),
                   jax.ShapeDtypeStruct((nb, seq, cw), BF16),
                   jax.ShapeDtypeStruct((nb, seq, cw), BF16),
                   jax.ShapeDtypeStruct((nb, 8, cw), F32)),
        scratch_shapes=[pltpu.VMEM((nbt, tt + 8, cw), F32), pltpu.VMEM((nj, nbt, tt, LANE), F32)],
        compiler_params=_params(("arbitrary", "arbitrary"), 52),
        name="front",
    )(x_prompt, nmix, win_bf, z_meta8, cwt, nconv)
    new_conv_p = ztail[:, 6:8, :][None]

    cc = n_chunks // 2
    um = u_meta.reshape(CHUNK, nj, LANE).transpose(1, 0, 2).reshape(nj, 1, wch)
    um = jnp.broadcast_to(um, (nj, 8 * nb, wch)).astype(BF16)
    gpt = g // nj
    nstj = 2 * gpt * n
    hch = cw // g
    y4c, s_last = pl.pallas_call(
        _ssm_kernel,
        grid=(nj, n_chunks // cc),
        in_specs=[pl.BlockSpec((1, nb, cc, wch), lambda j, t: (j, 0, t, 0)),
                  pl.BlockSpec((1, 8 * nb, wch), lambda j, t: (j, 0, 0)),
                  pl.BlockSpec((1, (CHUNK + 1) * LANE, hch), lambda j, t: (j, 0, 0)),
                  pl.BlockSpec((1, wch, 2 * n), lambda j, t: (j, 0, 0)),
                  pl.BlockSpec((1, 2 * n, wch), lambda j, t: (j, 0, 0)),
                  pl.BlockSpec((1, 2, nstj // 2), lambda j, t: (j, 0, 0))],
        out_specs=(pl.BlockSpec((1, nb, cc, wch), lambda j, t: (j, 0, t, 0)),
                   pl.BlockSpec((1, nb, 1, nstj), lambda j, t: (j, 0, 0, 0))),
        out_shape=(jax.ShapeDtypeStruct((nj, nb, n_chunks, wch), F32),
                   jax.ShapeDtypeStruct((nj, nb, 1, nstj), F32)),
        scratch_shapes=[pltpu.VMEM((nb, 1, nstj), F32), pltpu.VMEM((nb, cc, nstj), F32),
                        pltpu.VMEM((nb, cc, nstj), F32),
                        pltpu.VMEM((wch, nstj), BF16), pltpu.VMEM((nstj, wch), BF16),
                        pltpu.VMEM((CHUNK // 2, 2 * LANE, 2 * LANE), BF16)],
        compiler_params=_params(("parallel", "arbitrary"), 56),
        name="ssm",
    )(u4c, um, kc, pc, rc, a16)
    sl = s_last.reshape(nj, nb, 2, gpt, n)
    new_re_p = sl[:, :, 0].transpose(1, 0, 2, 3).reshape(1, nb, g, n)
    new_im_p = sl[:, :, 1].transpose(1, 0, 2, 3).reshape(1, nb, g, n)

    dsk = ssm_d[0].reshape(1, cw)
    wglu_bf = w_glu[0].astype(BF16)
    bglu = b_glu[0].reshape(1, cw)
    nssm = norm_out_ssm[0].reshape(1, cw)
    wout_bf = w_out[0].astype(BF16)
    nffn = norm_ffn[0].reshape(1, d)
    wr_pad = jnp.zeros((d, LANE), F32).at[:, :ne].set(w_router[0])
    wr_hi = wr_pad.astype(BF16)
    wr_lo = (wr_pad - wr_hi.astype(F32)).astype(BF16)
    br = b_router[0].reshape(ne, 1)
    wr2 = jnp.concatenate([wr_hi, wr_lo], axis=1)
    mix_w = (dsk, wglu_bf, bglu, nssm, wout_bf, nffn, wr2, br)
    mix_w_specs = [_full((1, cw)), _full((cw, cw)), _full((1, cw)), _full((1, cw)), _full((2 * cw, d)),
                   _full((1, d)), _full((d, 2 * LANE)), _full((ne, 1))]
    assert 4 + len(mix_w) == N_MIX_IN
    mix_out_shape = (jax.ShapeDtypeStruct((ta, d), F32), jax.ShapeDtypeStruct((ta, d), BF16),
                     jax.ShapeDtypeStruct((TOP_K, ta), I32), jax.ShapeDtypeStruct((TOP_K, ta), F32),
                     jax.ShapeDtypeStruct((n_tiles, ne, 1), I32))
    tpm = rows_p // TOK_TILE
    nbg = nb // nbt
    x1_all, hf_all, idx_all, gate_all, cnt = pl.pallas_call(
        _mix_kernel_prompt,
        grid=(seq // tm, nbg),
        in_specs=[pl.BlockSpec((nbt, tm, d), lambda i, b: (b, i, 0)),
                  pl.BlockSpec((nj, nbt, tm // CHUNK, wch), lambda i, b: (0, b, i, 0)),
                  pl.BlockSpec((nbt, tm, cw), lambda i, b: (b, i, 0)),
                  pl.BlockSpec((nbt, tm, cw), lambda i, b: (b, i, 0))] + mix_w_specs,
        out_specs=(pl.BlockSpec((rows_p, d), lambda i, b: (i * nbg + b, 0)),
                   pl.BlockSpec((rows_p, d), lambda i, b: (i * nbg + b, 0)),
                   pl.BlockSpec((TOP_K, rows_p), lambda i, b: (0, i * nbg + b)),
                   pl.BlockSpec((TOP_K, rows_p), lambda i, b: (0, i * nbg + b)),
                   pl.BlockSpec((tpm, ne, 1), lambda i, b: (i * nbg + b, 0, 0))),
        out_shape=mix_out_shape,
        scratch_shapes=[pltpu.VMEM((nj, nbt, tm, LANE), F32)],
        compiler_params=_params(("parallel", "parallel"), 52),
        name="mix_prompt",
    )(x_prompt, y4c, u_tok, ycn_p, *mix_w)

    last = n_tiles - 1
    any_spec = pl.BlockSpec(memory_space=pl.ANY)
    x1_all, hf_all, idx_all, gate_all, cnt = pl.pallas_call(
        _mix_kernel_sample,
        grid=(1,),
        in_specs=[_full((1, ns, d)), _full((ns, cw)), _full((1, ns, cw)),
                  _full((1, ns, cw))] + mix_w_specs + [any_spec] * 5,
        out_specs=(pl.BlockSpec((TOK_TILE, d), lambda i: (last, 0)),
                   pl.BlockSpec((TOK_TILE, d), lambda i: (last, 0)),
                   pl.BlockSpec((TOP_K, TOK_TILE), lambda i: (0, last)),
                   pl.BlockSpec((TOP_K, TOK_TILE), lambda i: (0, last)),
                   pl.BlockSpec((1, ne, 1), lambda i: (last, 0, 0))),
        out_shape=mix_out_shape,
        input_output_aliases={N_MIX_IN + k: k for k in range(5)},
        compiler_params=_params(("arbitrary",), 32),
        name="mix_sample",
    )(x_sample.reshape(1, ns, d), y_s, u_sm[:ns].astype(BF16).reshape(1, ns, cw),
      ycn_s.reshape(1, ns, cw), *mix_w, x1_all, hf_all, idx_all, gate_all, cnt)

    bm = MOE_ROWS
    cnt2 = cnt.reshape(n_tiles, ne)
    before = jnp.cumsum(cnt2, axis=0) - cnt2
    count = jnp.sum(cnt2, axis=0)
    padded = ((count + WIN + bm - 1) // bm) * bm
    pend = jnp.cumsum(padded)
    pstart = pend - padded
    phase = before % SUBLANE
    span = jnp.where(cnt2 > 0, phase + cnt2, 0)
    gstart = (pstart[None, :] + before - phase).astype(I32).reshape(-1)
    nwin = ((span + WIN - 1) // WIN).astype(I32)
    reg8 = ((span + SUBLANE - 1) // SUBLANE) * SUBLANE
    loff = (jnp.cumsum(reg8, axis=1) - reg8).astype(I32)
    tail = jnp.where(span % SUBLANE != 0, loff + (span // SUBLANE) * SUBLANE, -1).astype(I32)
    twin = jnp.sum(nwin, axis=1).astype(I32)
    n_blocks = (ta * TOP_K + ne * (WIN + bm - 1) + bm - 1) // bm
    cap = n_blocks * bm
    blk0 = jnp.arange(n_blocks, dtype=I32) * bm
    blk_e = jnp.minimum(jnp.sum((pend[None, :] <= blk0[:, None]).astype(I32), axis=1), ne - 1)
    e_ar = jnp.arange(ne, dtype=I32)
    blk_hot = blk_e[:, None] == e_ar[None, :]

    def _of_block(per_expert):
        return jnp.sum(jnp.where(blk_hot, per_expert[None, :], 0), axis=1)

    blk_valid = jnp.clip(_of_block(count) - (blk0 - _of_block(pstart)), 0, bm).astype(I32)
    has = count > 0
    later = jnp.logical_and(e_ar[None, :] > e_ar[:, None], has[None, :])
    nxt_e = jnp.min(jnp.where(later, e_ar[None, :], ne), axis=1)
    nxt_e = jnp.where(nxt_e < ne, nxt_e, -1)
    ordinal = jnp.cumsum(has.astype(I32)) - 1
    is_first = jnp.logical_and(blk_valid > 0, blk0 == _of_block(pstart))
    blk_first = jnp.where(is_first, jnp.where(_of_block(ordinal) == 0, 2, 1), 0).astype(I32)
    blk_next = _of_block(nxt_e).astype(I32)
    blk_slot = (_of_block(ordinal) % 2).astype(I32)
    nvalid = jnp.full((1,), tall, I32)
    loff_al = (WIN * (jnp.cumsum(nwin, axis=1) - nwin)).astype(I32)

    def _window_list(nw, slots, rows, first_hbm, first_buf):
        wcum = jnp.cumsum(nw, axis=1)
        wslot = jnp.arange(slots, dtype=I32)
        w_hot = jnp.logical_and(wslot[None, :, None] >= (wcum - nw)[:, None, :],
                                wslot[None, :, None] < wcum[:, None, :])

        def _of_window(per_tile_expert):
            return jnp.sum(jnp.where(w_hot, per_tile_expert[:, None, :], 0), axis=2)

        w_in_run = wslot[None, :] - _of_window(wcum - nw)
        return [(_of_window(f) + rows * w_in_run).astype(I32).reshape(-1) for f in (first_hbm, first_buf)]

    w_hbm, w_buf = _window_list(nwin, MAX_WINDOWS, WIN, gstart.reshape(n_tiles, ne), loff)
    nrun_d = _dispatch_run_rows(ne)
    zero_grp, spare_grp = nrun_d - 2 * SUBLANE, nrun_d - SUBLANE
    m_lo = jnp.where(nwin > 0, loff, spare_grp).astype(I32).reshape(-1)
    m_tg = jnp.where(jnp.logical_and(nwin > 0, tail >= 0), tail, zero_grp).astype(I32).reshape(-1)
    m_keep = (nwin == 0).astype(I32).reshape(-1)
    tables_d = (w_hbm, w_buf, m_lo, m_tg, m_keep, twin, nvalid)
    tables_c = (w_hbm, twin, nvalid)
    rowoff_d = (loff + phase).astype(I32).reshape(n_tiles, ne, 1)
    rowoff_c = (loff_al + phase).astype(I32).reshape(n_tiles, ne, 1)
    rgn_d = jnp.stack([loff, loff + reg8], axis=1).astype(I32)
    rgn_c = jnp.stack([loff_al, loff_al + WIN * nwin], axis=1).astype(I32)
    xw = d // 2 + GATE_COLS

    xs = pl.pallas_call(
        _dispatch_kernel,
        grid_spec=pltpu.PrefetchScalarGridSpec(
            num_scalar_prefetch=len(tables_d),
            grid=(n_tiles,),
            in_specs=[pl.BlockSpec((TOK_TILE, d), lambda i, *_: (i, 0)),
                      pl.BlockSpec((TOP_K, TOK_TILE), lambda i, *_: (0, i)),
                      pl.BlockSpec((TOP_K, TOK_TILE), lambda i, *_: (0, i)),
                      pl.BlockSpec((1, ne, 1), lambda i, *_: (i, 0, 0)),
                      pl.BlockSpec((1, 2, ne), lambda i, *_: (i, 0, 0))],
            out_specs=pl.BlockSpec(memory_space=pl.ANY),
            scratch_shapes=[pltpu.VMEM((2, _dispatch_run_rows(ne), xw), U32),
                            pltpu.VMEM((ne, SUBLANE, xw), U32), pltpu.SemaphoreType.DMA((2,))]),
        out_shape=jax.ShapeDtypeStruct((cap, xw), U32),
        compiler_params=_params(("arbitrary",), 40),
        name="dispatch",
    )(*tables_d, hf_all, idx_all, gate_all, rowoff_d, rgn_d)

    yb = pl.pallas_call(
        _moe_kernel,
        grid_spec=pltpu.PrefetchScalarGridSpec(
            num_scalar_prefetch=5,
            grid=(n_blocks,),
            in_specs=[pl.BlockSpec((bm, xw), lambda i, *_: (i, 0)),
                      pl.BlockSpec((1, 1, 2 * dff), lambda i, be, *_: (be[i], 0, 0)),
                      pl.BlockSpec((1, 1, d), lambda i, be, *_: (be[i], 0, 0)),
                      pl.BlockSpec(memory_space=pl.ANY), pl.BlockSpec(memory_space=pl.ANY)],
            out_specs=pl.BlockSpec((bm, d // 2), lambda i, *_: (i, 0)),
            scratch_shapes=[pltpu.VMEM((2, d, 2 * dff), F32), pltpu.VMEM((2, dff, d), F32),
                            pltpu.VMEM((d, 2 * dff), BF16), pltpu.VMEM((dff, d), BF16),
                            pltpu.SemaphoreType.DMA((2, 2))]),
        out_shape=jax.ShapeDtypeStruct((cap, d // 2), U32),
        compiler_params=_params(("arbitrary",), 58),
        name="moe",
    )(blk_e, blk_valid, blk_first, blk_next, blk_slot, xs, b_gate_up[0].reshape(ne, 1, 2 * dff),
      b_down[0].reshape(ne, 1, d), w_gate_up[0], w_down[0])

    nfin = norm_final.reshape(1, d)
    nbh = TOK_TILE // tm

    def _tile_of(s):
        return (s + n_tiles - 1) % n_tiles

    tiles_per_time = nbg * tpm

    def _yp_index(s, *_):
        t = jnp.maximum(s - 1, 0)
        return (t % tiles_per_time, t // tiles_per_time, 0)

    y_p, y_sm = pl.pallas_call(
        _combine_kernel,
        grid_spec=pltpu.PrefetchScalarGridSpec(
            num_scalar_prefetch=len(tables_c),
            grid=(n_tiles,),
            in_specs=[pl.BlockSpec((TOK_TILE, d), lambda s, *_: (_tile_of(s), 0)),
                      pl.BlockSpec((TOP_K, TOK_TILE), lambda s, *_: (0, _tile_of(s))),
                      pl.BlockSpec((1, ne, 1), lambda s, *_: (_tile_of(s), 0, 0)),
                      pl.BlockSpec((1, 2, ne), lambda s, *_: (_tile_of(s), 0, 0)),
                      pl.BlockSpec((1, d), lambda s, *_: (0, 0)),
                      pl.BlockSpec(memory_space=pl.ANY)],
            out_specs=(pl.BlockSpec((nbh, tm, d), _yp_index),
                       pl.BlockSpec((ns, d), lambda s, *_: (0, 0))),
            scratch_shapes=[pltpu.VMEM((2, _combine_run_rows(ne), d // 2), U32),
                            pltpu.SemaphoreType.DMA((2,))]),
        out_shape=(jax.ShapeDtypeStruct((nb, seq, d), F32), jax.ShapeDtypeStruct((ns, d), F32)),
        compiler_params=_params(("arbitrary",), 48),
        name="combine",
    )(*tables_c, x1_all, idx_all, rowoff_c, rgn_c, nfin, yb)

    return (y_p, y_sm.reshape(ns, 1, d), new_re_p, new_im_p, new_conv_p,
            new_re_s, new_im_s, new_conv_s)
```

```python
import math

import jax
import jax.numpy as jnp
from jax import lax
from jax.experimental import pallas as pl
from jax.experimental.pallas import tpu as pltpu

F32 = jnp.float32
BF16 = jnp.bfloat16
U32 = jnp.uint32
I32 = jnp.int32
EPS = 1e-5
CHUNK = 16
LANE = 128
TOP_K = 4
SWIGLU_LIMIT = 7.0
SWIGLU_ALPHA = 1.702
MOE_ROWS = 1024
MOE_SUB = 256
MOE_CHAIN = 512
TOK_TILE = 256
WIN = 32
SUBLANE = 8
GATE_COLS = LANE


MAX_WINDOWS = 72


def _dispatch_run_rows(ne):
    return TOP_K * TOK_TILE + ne * 2 * (SUBLANE - 1) + WIN


def _combine_run_rows(ne):
    return -(-(TOP_K * TOK_TILE + ne * (SUBLANE - 1 + WIN - 1)) // WIN) * WIN
HI_MASK = 0xFFFF0000
MIB = 1024 * 1024


def _rms(x, g):
    return x * lax.rsqrt(jnp.mean(x * x, axis=-1, keepdims=True) + EPS) * g


def _gelu_tanh(x):
    c = math.sqrt(2.0 / math.pi)
    return 0.5 * x * (1.0 + jnp.tanh(c * (x + 0.044715 * (x * x * x))))


def _params(sem, vmem_mib):
    return pltpu.CompilerParams(dimension_semantics=sem, vmem_limit_bytes=vmem_mib * MIB)


def _pack_pairs(a, b):
    return (pltpu.bitcast(a, U32) >> 16) | (pltpu.bitcast(b, U32) & jnp.uint32(HI_MASK))


def _unpack_pairs(w):
    lo = pltpu.bitcast(w << 16, F32)
    hi = pltpu.bitcast(w & jnp.uint32(HI_MASK), F32)
    return jnp.concatenate([lo, hi], axis=-1).astype(BF16)


def _iota2(shape, axis):
    return lax.broadcasted_iota(I32, shape, axis)


def _expand_cols(compact, reps_log2, n_log2):
    q = _iota2((compact.shape[1], 2 << (reps_log2 + n_log2)), 0)
    c = _iota2((compact.shape[1], 2 << (reps_log2 + n_log2)), 1)
    nmask = (1 << n_log2) - 1
    same = jnp.logical_and((q >> n_log2) == (c >> (reps_log2 + n_log2)), (q & nmask) == (c & nmask))
    return jnp.dot(compact, jnp.where(same, 1.0, 0.0).astype(BF16), preferred_element_type=F32)


def _expand_rows(compact, reps_log2, n_log2):
    r = _iota2((2 << (reps_log2 + n_log2), compact.shape[0]), 0)
    q = _iota2((2 << (reps_log2 + n_log2), compact.shape[0]), 1)
    nmask = (1 << n_log2) - 1
    same = jnp.logical_and((r >> (reps_log2 + n_log2)) == (q >> n_log2), (r & nmask) == (q & nmask))
    return jnp.dot(jnp.where(same, 1.0, 0.0).astype(BF16), compact, preferred_element_type=F32)


def _group_mask(shape, row_shift, col_shift, ngroups):
    r = _iota2(shape, 0)
    c = _iota2(shape, 1)
    return ((r >> row_shift) & (ngroups - 1)) == ((c >> col_shift) & (ngroups - 1))


def _small_front_kernel(x_ref, nmix_ref, win_ref, s0r_ref, s0i_ref, b0_ref, b1_ref, cw_ref,
                        bc_ref, cc_ref, abr_ref, abi_ref, nconv_ref,
                        u_ref, z_ref, y_ref, ycn_ref, sr_ref, si_ref, bdb_ref, cm_ref):
    ns, nst = s0r_ref.shape
    cw = u_ref.shape[1]
    n = bc_ref.shape[1] // 2
    nlog = n.bit_length() - 1
    glog = (nst // n).bit_length() - 1
    hlog = (cw >> glog).bit_length() - 1
    bdb_ref[...] = jnp.where(_group_mask(bdb_ref.shape, hlog, nlog, 1 << glog),
                             _expand_cols(bc_ref[...], glog, nlog), 0.0).astype(BF16)
    cm_ref[...] = jnp.where(_group_mask(cm_ref.shape, nlog, hlog, 1 << glog),
                            _expand_rows(cc_ref[...], glog, nlog), 0.0).astype(BF16)
    h = _rms(x_ref[...], nmix_ref[...]).astype(BF16)
    proj = jnp.dot(h, win_ref[...], preferred_element_type=F32)
    u = proj[:, 0:cw]
    zc = proj[:, cw:2 * cw]
    gb = proj[:, 2 * cw:3 * cw]
    gc = proj[:, 3 * cw:4 * cw]
    z = gc * zc
    u_ref[...] = u
    z_ref[...] = z
    bu = jnp.dot(u[:ns].astype(BF16), bdb_ref[...], preferred_element_type=F32)
    abr = abr_ref[...]
    abi = abi_ref[...]
    s0r = s0r_ref[...]
    s0i = s0i_ref[...]
    sr = abr * s0r - abi * s0i + bu[:, :nst]
    si = abr * s0i + abi * s0r + bu[:, nst:]
    sr_ref[...] = sr
    si_ref[...] = si
    scat = jnp.concatenate([sr, si], axis=-1).astype(BF16)
    y_ref[...] = jnp.dot(scat, cm_ref[...], preferred_element_type=F32)
    conv = cw_ref[0:1, :] * b0_ref[...] + cw_ref[1:2, :] * b1_ref[...] + cw_ref[2:3, :] * z[:ns]
    ycn_ref[...] = _rms(gb[:ns] * conv, nconv_ref[...]).astype(BF16)


def _front_kernel(x_ref, nmix_ref, win_ref, zm_ref, cw_ref, nconv_ref,
                  uc_ref, ut_ref, ycn_ref, zt_ref, zbuf, ubuf):
    i = pl.program_id(1)
    nb_all, tt, d = x_ref.shape
    cw = ycn_ref.shape[2]
    ncz = tt // CHUNK

    @pl.when(i == 0)
    def _():
        zbuf[:, 0:8, :] = jnp.broadcast_to(zm_ref[...][None], (nb_all, 8, cw))

    nb = nb_all
    rows = nb * tt
    for b0 in range(0, nb_all, nb):
        sq = slice(b0, b0 + nb)
        h = _rms(x_ref[sq].reshape(rows, d), nmix_ref[...]).astype(BF16)
        u = jnp.dot(h, win_ref[:, 0:cw], preferred_element_type=F32)
        ut_ref[sq] = u.astype(BF16).reshape(nb, tt, cw)
        for j in range(cw // LANE):
            ubuf[j, sq] = u[:, j * LANE:(j + 1) * LANE].reshape(nb, tt, LANE)
        for s in range(CHUNK):
            for j in range(cw // LANE):
                piece = ubuf[j, sq, pl.ds(s, ncz, stride=CHUNK), :]
                uc_ref[j, sq, :, s * LANE:(s + 1) * LANE] = piece.astype(BF16)
        zc = jnp.dot(h, win_ref[:, cw:2 * cw], preferred_element_type=F32)
        gc = jnp.dot(h, win_ref[:, 3 * cw:4 * cw], preferred_element_type=F32)
        z3 = (gc * zc).reshape(nb, tt, cw)
        zbuf[sq, 8:8 + tt, :] = z3
        z1 = zbuf[sq, 7:7 + tt, :]
        z2 = zbuf[sq, 6:6 + tt, :]
        conv = cw_ref[0:1, :] * z2 + cw_ref[1:2, :] * z1 + cw_ref[2:3, :] * z3
        gb = jnp.dot(h, win_ref[:, 2 * cw:3 * cw], preferred_element_type=F32)
        yc = gb * conv.reshape(rows, cw)
        ycn_ref[sq] = _rms(yc, nconv_ref[...]).astype(BF16).reshape(nb, tt, cw)
        tail = zbuf[sq, tt:tt + 8, :]
        zt_ref[sq] = tail
        zbuf[sq, 0:8, :] = tail


def _ssm_kernel(u_ref, um_ref, kc_ref, pc_ref, rc_ref, a16_ref, y_ref, sl_ref,
                s_carry, ds_ref, sp_ref, p_s, r_s, t_s):
    th = pl.program_id(1)
    _, nb, cc, w = u_ref.shape
    nst = p_s.shape[1]
    half = nst // 2
    rows = nb * cc
    blk = 2 * LANE
    u = u_ref[0].reshape(rows, w)

    @pl.when(th == 0)
    def _():
        hch = kc_ref.shape[2]
        gpt = LANE // hch
        hlog = hch.bit_length() - 1
        glog = gpt.bit_length() - 1
        nlog = (pc_ref.shape[2] // 2).bit_length() - 1
        p_s[...] = jnp.where(_group_mask(p_s.shape, hlog, nlog, gpt),
                             _expand_cols(pc_ref[0], glog, nlog), 0.0).astype(BF16)
        r_s[...] = jnp.where(_group_mask(r_s.shape, nlog, hlog, gpt),
                             _expand_rows(rc_ref[0], glog, nlog), 0.0).astype(BF16)
        nlag = kc_ref.shape[1] // LANE
        o = _iota2((hch, LANE), 0)
        c = _iota2((hch, LANE), 1)
        spread = jnp.where((c & (hch - 1)) == o, 1.0, 0.0).astype(BF16)
        lagm = jnp.dot(kc_ref[0], spread, preferred_element_type=F32)
        r = _iota2(lagm.shape, 0)
        c = _iota2(lagm.shape, 1)
        lagm = jnp.where(((r >> hlog) & (gpt - 1)) == (c >> hlog), lagm, 0.0).astype(BF16)
        for dlt in range(nlag // 2):
            b0 = lagm[(2 * dlt) * LANE:(2 * dlt + 1) * LANE]
            b1 = lagm[(2 * dlt + 1) * LANE:(2 * dlt + 2) * LANE]
            b2 = lagm[(2 * dlt + 2) * LANE:(2 * dlt + 3) * LANE]
            t_s[dlt, 0:LANE, 0:LANE] = b1
            t_s[dlt, 0:LANE, LANE:blk] = b2
            t_s[dlt, LANE:blk, 0:LANE] = b0
            t_s[dlt, LANE:blk, LANE:blk] = b1
        ds_ref[:, 0:8, :] = jnp.dot(um_ref[0], p_s[...], preferred_element_type=F32).reshape(nb, 8, nst)
        s_carry[...] = ds_ref[:, 0:1, :]

    ds_ref[...] = jnp.dot(u, p_s[...], preferred_element_type=F32).reshape(nb, cc, nst)
    for tb in range(w // blk):
        acc = jnp.dot(u[:, 0:blk], t_s[tb], preferred_element_type=F32)
        for sb in range(1, tb + 1):
            acc = acc + jnp.dot(u[:, sb * blk:(sb + 1) * blk], t_s[tb - sb],
                                preferred_element_type=F32)
        y_ref[0, :, :, tb * blk:(tb + 1) * blk] = acc.reshape(nb, cc, blk)
    ar = a16_ref[0, 0:1, :].reshape(1, 1, half)
    ai = a16_ref[0, 1:2, :].reshape(1, 1, half)
    sr = s_carry[:, :, 0:half]
    si = s_carry[:, :, half:nst]
    for c in range(cc):
        sp_ref[:, c:c + 1, 0:half] = sr
        sp_ref[:, c:c + 1, half:nst] = si
        dr = ds_ref[:, c:c + 1, 0:half]
        di = ds_ref[:, c:c + 1, half:nst]
        sr, si = ar * sr - ai * si + dr, ar * si + ai * sr + di
    s_carry[:, :, 0:half] = sr
    s_carry[:, :, half:nst] = si
    sl_ref[0, :, :, 0:half] = sr
    sl_ref[0, :, :, half:nst] = si

    sp = sp_ref[...].reshape(rows, nst).astype(BF16)
    for tb in range(w // blk):
        acc = jnp.dot(sp, r_s[:, tb * blk:(tb + 1) * blk], preferred_element_type=F32)
        y_ref[0, :, :, tb * blk:(tb + 1) * blk] += acc.reshape(nb, cc, blk)


N_MIX_IN = 12


def _mix_rows(x, yssm, ut, ycn, dsk_ref, wglu_ref, bglu_ref, nssm_ref, wout_ref,
              nffn_ref, wr_ref, br_ref):
    ne = br_ref.shape[0]
    y = _gelu_tanh(yssm + dsk_ref[...] * ut.astype(F32))
    glu = jnp.dot(y.astype(BF16), wglu_ref[...], preferred_element_type=F32) + bglu_ref[...]
    o = y * jax.nn.sigmoid(glu)
    ysn = _rms(o, nssm_ref[...]).astype(BF16)
    mix = jnp.concatenate([ysn, ycn], axis=-1)
    x1 = x + jnp.dot(mix, wout_ref[...], preferred_element_type=F32)
    hf = _rms(x1, nffn_ref[...])
    hf_hi = hf.astype(BF16)
    hf_lo = (hf - hf_hi.astype(F32)).astype(BF16)
    r = hf.shape[0]
    part = jnp.dot(jnp.concatenate([hf_hi, hf_lo], axis=0), wr_ref[...], preferred_element_type=F32)
    logits = (part[0:r, 0:LANE] + part[0:r, LANE:2 * LANE]) + (part[r:2 * r, 0:LANE] + part[r:2 * r, LANE:2 * LANE])
    lt = logits.T[0:ne, :] + br_ref[...]
    iota = lax.broadcasted_iota(I32, lt.shape, 0)
    vals, idxs = [], []
    sel = jnp.zeros(lt.shape, F32)
    for _ in range(TOP_K):
        m = jnp.max(lt, axis=0, keepdims=True)
        ik = jnp.min(jnp.where(lt == m, iota, ne), axis=0, keepdims=True)
        vals.append(m)
        idxs.append(ik)
        hit = iota == ik
        sel = sel + jnp.where(hit, 1.0, 0.0)
        lt = jnp.where(hit, -jnp.inf, lt)
    es = [jnp.exp(v - vals[0]) for v in vals]
    tot = es[0] + es[1] + es[2] + es[3]
    idx = jnp.concatenate(idxs, axis=0)
    gates = jnp.concatenate([e / tot for e in es], axis=0)
    return x1, hf_hi, idx, gates, sel


def _mix_kernel_prompt(*refs):
    x_ref, yc_ref = refs[0], refs[1]
    x1_ref, hf_ref, idx_ref, gate_ref, cnt_ref, ybuf = refs[N_MIX_IN:]
    nj, nb, ncz, _ = yc_ref.shape
    for s in range(CHUNK):
        for j in range(nj):
            ybuf[j, :, pl.ds(s, ncz, stride=CHUNK), :] = yc_ref[j, :, :, s * LANE:(s + 1) * LANE]
    ut_ref, ycn_ref = refs[2], refs[3]
    tt, d = x_ref.shape[1], x_ref.shape[2]
    cw = nj * LANE
    nbc = TOK_TILE // tt
    for t in range(cnt_ref.shape[0]):
        b0, r0 = t * nbc, t * TOK_TILE
        yssm = jnp.concatenate([ybuf[j, b0:b0 + nbc].reshape(TOK_TILE, LANE) for j in range(nj)], axis=-1)
        x1, hf, idx, gates, sel = _mix_rows(
            x_ref[b0:b0 + nbc].reshape(TOK_TILE, d), yssm, ut_ref[b0:b0 + nbc].reshape(TOK_TILE, cw),
            ycn_ref[b0:b0 + nbc].reshape(TOK_TILE, cw), *refs[4:N_MIX_IN])
        x1_ref[r0:r0 + TOK_TILE, :] = x1
        hf_ref[r0:r0 + TOK_TILE, :] = hf
        idx_ref[:, r0:r0 + TOK_TILE] = idx
        gate_ref[:, r0:r0 + TOK_TILE] = gates
        cnt_ref[t] = jnp.sum(sel, axis=1, keepdims=True).astype(I32)


def _mix_kernel_sample(*refs):
    x1, hf, idx, gates, sel = _mix_rows(refs[0][0], refs[1][...], refs[2][0], refs[3][0],
                                        *refs[4:N_MIX_IN])
    x1_ref, hf_ref, idx_ref, gate_ref, cnt_ref = refs[N_MIX_IN + 5:]
    ns = x1.shape[0]
    x1_ref[...] = jnp.zeros(x1_ref.shape, x1_ref.dtype)
    hf_ref[...] = jnp.zeros(hf_ref.shape, hf_ref.dtype)
    idx_ref[...] = jnp.zeros(idx_ref.shape, idx_ref.dtype)
    gate_ref[...] = jnp.zeros(gate_ref.shape, gate_ref.dtype)
    x1_ref[0:ns, :] = x1
    hf_ref[0:ns, :] = hf
    idx_ref[:, 0:ns] = idx
    gate_ref[:, 0:ns] = gates
    cnt_ref[0] = jnp.sum(sel, axis=1, keepdims=True).astype(I32)


def _split_bf16(x, parts):
    out = []
    for _ in range(parts - 1):
        p = x.astype(BF16)
        out.append(p)
        x = x - p.astype(F32)
    out.append(x.astype(BF16))
    return out


def _run_onehot(idx_ref, loff_ref, rgn_ref, tile, n_valid, nrun, gate_ref=None):
    ne = loff_ref.shape[1]
    tt = idx_ref.shape[1]
    e_iota = lax.broadcasted_iota(I32, (ne, tt), 0)
    tok = tile * tt + lax.broadcasted_iota(I32, (1, tt), 1)
    valid = tok < n_valid
    hits = [jnp.logical_and(e_iota == idx_ref[k:k + 1, :], valid) for k in range(TOP_K)]
    sel = jnp.zeros((ne, tt), F32)
    for h in hits:
        sel = sel + jnp.where(h, 1.0, 0.0)
    before = lax.broadcasted_iota(I32, (tt, tt), 0) < lax.broadcasted_iota(I32, (tt, tt), 1)
    tri = jnp.where(before, 1.0, 0.0).astype(BF16)
    base = jnp.dot(sel.astype(BF16), tri, preferred_element_type=F32) + loff_ref[0].astype(F32)
    base = jnp.where(sel > 0.0, base + 1.0, 0.0)
    b_hi = jnp.floor(base * (1.0 / 256.0))
    b_lo = base - 256.0 * b_hi
    r_i = lax.broadcasted_iota(I32, (nrun, ne), 0)
    own = jnp.logical_and(r_i >= rgn_ref[0, 0:1, :], r_i < rgn_ref[0, 1:2, :])
    own_bf = jnp.where(own, 1.0, 0.0).astype(BF16)
    want = (256.0 * jnp.dot(own_bf, b_hi.astype(BF16), preferred_element_type=F32)
            + jnp.dot(own_bf, b_lo.astype(BF16), preferred_element_type=F32))
    r_f = (lax.broadcasted_iota(I32, (nrun, tt), 0) + 1).astype(F32)
    smat = jnp.where(want == r_f, 1.0, 0.0).astype(BF16)
    if gate_ref is None:
        return smat
    gate_e = jnp.zeros((ne, tt), F32)
    for k, h in enumerate(hits):
        gate_e = gate_e + jnp.where(h, gate_ref[k:k + 1, :], 0.0)
    nparts = 3
    pieces = jnp.concatenate(_split_bf16(gate_e, nparts), axis=0)
    per = lax.dot_general(smat, pieces, (((1,), (1,)), ((), ())), preferred_element_type=F32)
    lo3 = jnp.concatenate([rgn_ref[0, 0:1, :]] * nparts, axis=1)
    hi3 = jnp.concatenate([rgn_ref[0, 1:2, :]] * nparts, axis=1)
    r_i3 = lax.broadcasted_iota(I32, (nrun, nparts * ne), 0)
    own3 = jnp.logical_and(r_i3 >= lo3, r_i3 < hi3)
    gcol = jnp.sum(jnp.where(own3, per, 0.0), axis=1, keepdims=True)
    return smat, gcol


def _window_copy(buf, slot, hbm, lo, g, sem, to_hbm, rows, align):
    src = buf.at[slot, pl.ds(pl.multiple_of(lo, align), rows)]
    dst = hbm.at[pl.ds(pl.multiple_of(g, align), rows)]
    if to_hbm:
        return pltpu.make_async_copy(src, dst, sem.at[slot])
    return pltpu.make_async_copy(dst, src, sem.at[slot])


def _start_windows(wg_ref, wl_ref, tw_ref, tile, slots, buf, slot, hbm, sem, to_hbm, rows, align):
    def per_window(w, c):
        k = tile * slots + w
        lo = w * rows if wl_ref is None else wl_ref[k]
        _window_copy(buf, slot, hbm, lo, wg_ref[k], sem, to_hbm, rows, align).start()
        return c

    lax.fori_loop(0, tw_ref[tile], per_window, 0)


def _wait_windows(count, buf, slot, hbm, sem, to_hbm, rows, align):
    batch = 8

    def many(w, c):
        _window_copy(buf, slot, hbm, 0, 0, sem, to_hbm, batch * rows, align).wait()
        return c

    def one(w, c):
        _window_copy(buf, slot, hbm, 0, 0, sem, to_hbm, rows, align).wait()
        return c

    lax.fori_loop(0, count // batch, many, 0)
    lax.fori_loop(0, count % batch, one, 0)


def _dispatch_kernel(wg_ref, wl_ref, mlo_ref, mtg_ref, keep_ref, tw_ref, nv_ref,
                     hf_ref, idx_ref, gate_ref, loff_ref, rgn_ref, xs_ref, buf, carry, sem):
    i = pl.program_id(0)
    nt = pl.num_programs(0)
    ne = loff_ref.shape[1]
    tt, d = hf_ref.shape
    nrun = buf.shape[1]
    slot = i % 2

    @pl.when(i == 0)
    def _():
        carry[...] = jnp.zeros(carry.shape, carry.dtype)

    smat, gcol = _run_onehot(idx_ref, loff_ref, rgn_ref, i, nv_ref[0], nrun, gate_ref)
    xr = jnp.dot(smat, hf_ref[...], preferred_element_type=F32)
    lane0 = lax.broadcasted_iota(I32, (nrun, GATE_COLS), 1) == 0
    buf[slot, :, 0:d // 2] = _pack_pairs(xr[:, :d // 2], xr[:, d // 2:])
    buf[slot, :, d // 2:] = pltpu.bitcast(jnp.where(lane0, gcol, 0.0), U32)

    def merge(e, c):
        k = i * ne + e
        lo = pl.multiple_of(mlo_ref[k], SUBLANE)
        buf[slot, pl.ds(lo, SUBLANE), :] = buf[slot, pl.ds(lo, SUBLANE), :] | carry[e]
        tg = pl.multiple_of(mtg_ref[k], SUBLANE)
        carry[e] = jnp.where(keep_ref[k] > 0, carry[e], buf[slot, pl.ds(tg, SUBLANE), :])
        return c

    lax.fori_loop(0, ne, merge, 0, unroll=4)

    @pl.when(i > 0)
    def _():
        _wait_windows(tw_ref[i - 1], buf, 1 - slot, xs_ref, sem, True, WIN, SUBLANE)

    _start_windows(wg_ref, wl_ref, tw_ref, i, MAX_WINDOWS, buf, slot, xs_ref, sem, True, WIN, SUBLANE)

    @pl.when(i == nt - 1)
    def _():
        _wait_windows(tw_ref[i], buf, slot, xs_ref, sem, True, WIN, SUBLANE)


def _expert_weight_copies(wg_hbm, wd_hbm, wg_f32, wd_f32, sem, e, slot):
    return (pltpu.make_async_copy(wg_hbm.at[e], wg_f32.at[slot], sem.at[0, slot]),
            pltpu.make_async_copy(wd_hbm.at[e], wd_f32.at[slot], sem.at[1, slot]))


def _moe_kernel(be_ref, bv_ref, first_ref, nxt_ref, slot_ref,
                x_ref, bg_ref, bd_ref, wg_hbm, wd_hbm, y_ref,
                wg_f32, wd_f32, wg_bf, wd_bf, sem):
    i = pl.program_id(0)
    e = be_ref[i]
    dff = wd_bf.shape[0]
    bm = x_ref.shape[0]
    sub = MOE_SUB
    nw = x_ref.shape[1] - GATE_COLS

    @pl.when(first_ref[i] > 0)
    def _():
        slot = slot_ref[i]

        @pl.when(first_ref[i] > 1)
        def _():
            for cp in _expert_weight_copies(wg_hbm, wd_hbm, wg_f32, wd_f32, sem, e, slot):
                cp.start()

        for cp in _expert_weight_copies(wg_hbm, wd_hbm, wg_f32, wd_f32, sem, e, slot):
            cp.wait()

        @pl.when(nxt_ref[i] >= 0)
        def _():
            for cp in _expert_weight_copies(wg_hbm, wd_hbm, wg_f32, wd_f32, sem, nxt_ref[i], 1 - slot):
                cp.start()

        wg_bf[...] = wg_f32[slot].astype(BF16)
        wd_bf[...] = wd_f32[slot].astype(BF16)

    def rows(r0, nrows):
        sizes = [MOE_CHAIN] * (nrows // MOE_CHAIN) + ([sub] if nrows % MOE_CHAIN else [])
        lo = r0
        for size in sizes:
            _chain(lo, size)
            lo += size

    def _chain(lo, n):
        live = lax.broadcasted_iota(I32, (n, 1), 0) + lo < bv_ref[i]
        x = jnp.where(live, _unpack_pairs(x_ref[lo:lo + n, 0:nw]), jnp.zeros((), BF16))
        route = jnp.where(live, pltpu.bitcast(x_ref[lo:lo + n, nw:], F32)[:, 0:1], 0.0)
        gu = jnp.dot(x, wg_bf[...], preferred_element_type=F32) + bg_ref[0]
        gate = jnp.minimum(gu[:, :dff], SWIGLU_LIMIT)
        up = jnp.clip(gu[:, dff:], -SWIGLU_LIMIT, SWIGLU_LIMIT)
        h = gate * jax.nn.sigmoid(SWIGLU_ALPHA * gate) * (up + 1.0)
        y = jnp.dot(h.astype(BF16), wd_bf[...], preferred_element_type=F32) + bd_ref[0]
        yr = (route * y).astype(BF16).astype(F32)
        half = yr.shape[1] // 2
        y_ref[lo:lo + n, :] = _pack_pairs(yr[:, :half], yr[:, half:])

    nchains = bm // sub
    for live_chains in range(nchains + 1):
        lo_rows, hi_rows = (live_chains - 1) * sub, live_chains * sub

        @pl.when(jnp.logical_and(bv_ref[i] > lo_rows, bv_ref[i] <= hi_rows) if live_chains
                 else bv_ref[i] <= 0)
        def _(used=hi_rows):
            if used:
                rows(0, used)
            if used < bm:
                y_ref[used:bm, :] = jnp.zeros((bm - used, y_ref.shape[1]), y_ref.dtype)


def _combine_tile(step, nt):
    return (step + nt - 1) % nt


def _combine_kernel(wg_ref, tw_ref, nv_ref,
                    x1_ref, idx_ref, loff_ref, rgn_ref, nf_ref, yb_ref,
                    yp_ref, ys_ref, buf, sem):
    s = pl.program_id(0)
    nt = pl.num_programs(0)
    tt, d = x1_ref.shape
    nrun = buf.shape[1]
    tile = _combine_tile(s, nt)
    slot = s % 2

    @pl.when(s == 0)
    def _():
        buf[...] = jnp.zeros(buf.shape, buf.dtype)
        _start_windows(wg_ref, None, tw_ref, tile, MAX_WINDOWS, buf, slot, yb_ref, sem, False,
                       WIN, SUBLANE)

    @pl.when(s + 1 < nt)
    def _():
        _start_windows(wg_ref, None, tw_ref, _combine_tile(s + 1, nt), MAX_WINDOWS, buf, 1 - slot,
                       yb_ref, sem, False, WIN, SUBLANE)

    smat = _run_onehot(idx_ref, loff_ref, rgn_ref, tile, nv_ref[0], nrun)

    _wait_windows(tw_ref[tile], buf, slot, yb_ref, sem, False, WIN, SUBLANE)
    yrun = _unpack_pairs(buf[slot])
    moe = lax.dot_general(smat, yrun, (((0,), (0,)), ((), ())), preferred_element_type=F32)
    out = _rms(x1_ref[...] + moe, nf_ref[...])

    @pl.when(s == 0)
    def _():
        ys_ref[...] = out[0:ys_ref.shape[0], :]

    @pl.when(s > 0)
    def _():
        yp_ref[...] = out.reshape(yp_ref.shape)


def _ssm_matrices(a_re, a_im, log_dt, b_re, b_im, c_re, c_im):
    g, n = a_re.shape
    hch = b_re.shape[2]
    gpt = LANE // hch
    nj = g // gpt
    a = lax.complex(a_re, a_im)
    dta = a * jnp.exp(log_dt)[:, None]
    a_bar = jnp.exp(dta)
    bb = ((a_bar - 1.0) / a)[:, :, None] * lax.complex(b_re, b_im)
    cc = lax.complex(c_re, c_im)
    ks = jnp.arange(CHUNK + 1, dtype=F32)
    pw = jnp.exp(dta[None] * ks[:, None, None])
    kk = jnp.real(jnp.einsum('gon,kgn,gni->kgio', cc, pw[:CHUNK], bb))
    kk = jnp.concatenate([jnp.zeros_like(kk[:1]), kk], axis=0)
    kc = kk.reshape(CHUNK + 1, nj, gpt * hch, hch).transpose(1, 0, 2, 3)
    kc = kc.reshape(nj, (CHUNK + 1) * LANE, hch).astype(BF16)
    pwr, pwi = jnp.real(pw), jnp.imag(pw)
    bbr = jnp.real(bb).transpose(0, 2, 1).reshape(1, g * hch, n)
    bbi = jnp.imag(bb).transpose(0, 2, 1).reshape(1, g * hch, n)
    par = jnp.repeat(pwr[CHUNK - 1::-1][:CHUNK], hch, axis=1)
    pai = jnp.repeat(pwi[CHUNK - 1::-1][:CHUNK], hch, axis=1)
    pc = jnp.concatenate([par * bbr - pai * bbi, par * bbi + pai * bbr], axis=-1)
    pc = pc.reshape(CHUNK, nj, gpt * hch, 2 * n).transpose(1, 0, 2, 3)
    pc = pc.reshape(nj, CHUNK * LANE, 2 * n).astype(BF16)
    ccr = jnp.real(cc).transpose(2, 0, 1).reshape(n, 1, g * hch)
    cci = jnp.imag(cc).transpose(2, 0, 1).reshape(n, 1, g * hch)
    qar = jnp.repeat(pwr[1:CHUNK + 1].transpose(2, 0, 1), hch, axis=2)
    qai = jnp.repeat(pwi[1:CHUNK + 1].transpose(2, 0, 1), hch, axis=2)
    rc = jnp.stack([ccr * qar - cci * qai, -(ccr * qai + cci * qar)], axis=0)
    rc = rc.reshape(2 * n, CHUNK, nj, gpt * hch).transpose(2, 0, 1, 3)
    rc = rc.reshape(nj, 2 * n, CHUNK * LANE).astype(BF16)
    a16 = pw[CHUNK].reshape(nj, 1, gpt * n)
    a16 = jnp.concatenate([jnp.real(a16), jnp.imag(a16)], axis=1)
    bc = jnp.stack([jnp.real(bb), jnp.imag(bb)], axis=0).transpose(1, 3, 0, 2)
    bc = bc.reshape(g * hch, 2 * n).astype(BF16)
    c2 = jnp.stack([jnp.real(cc), -jnp.imag(cc)], axis=0).transpose(0, 3, 1, 2)
    c2 = c2.reshape(2 * n, g * hch).astype(BF16)
    abr = jnp.real(a_bar).reshape(1, g * n)
    abi = jnp.imag(a_bar).reshape(1, g * n)
    return kc, pc, rc, a16, bc, c2, abr, abi


def _full(shape):
    return pl.BlockSpec(shape, lambda *_: (0,) * len(shape))


def kernel(x_prompt, x_sample, state_ssm_re, state_ssm_im, state_conv, meta_tokens, norm_mix, w_in,
           ssm_a_re, ssm_a_im, ssm_log_dt, ssm_b_re, ssm_b_im, ssm_c_re, ssm_c_im, ssm_d, w_glu, b_glu,
           conv_w, norm_out_ssm, norm_out_conv, w_out, norm_ffn, w_router, b_router, w_gate_up,
           b_gate_up, w_down, b_down, norm_final):
    nb, seq, d = x_prompt.shape
    ns = x_sample.shape[0]
    depth, _, g, n = state_ssm_re.shape
    assert depth == 1 and x_sample.shape[1] == 1 and meta_tokens.shape[0] == CHUNK
    cw = conv_w.shape[2]
    nj = cw // LANE
    ne = w_router.shape[2]
    dff = w_down.shape[2]
    nst = g * n
    nbt = nb // 2
    tt = 256
    n_chunks = seq // CHUNK
    tp = nb * seq
    tall = tp + ns
    tm = 128
    nbm = nb
    rows_p = nbm * tm
    assert rows_p % TOK_TILE == 0 and TOK_TILE % tm == 0 and ns <= TOK_TILE and d % 2 == 0
    n_tiles = tp // TOK_TILE + 1
    ta = n_tiles * TOK_TILE

    kc, pc, rc, a16, bc, c2, abr, abi = _ssm_matrices(
        ssm_a_re[0], ssm_a_im[0], ssm_log_dt[0], ssm_b_re[0], ssm_b_im[0], ssm_c_re[0], ssm_c_im[0])
    win_bf = w_in[0].astype(BF16)
    nmix = norm_mix[0].reshape(1, d)
    nconv = norm_out_conv[0].reshape(1, cw)
    cwt = conv_w[0]

    xsm = jnp.concatenate([x_sample.reshape(ns, d), meta_tokens], axis=0)
    nsm = ns + CHUNK
    s0r = state_ssm_re[0].reshape(ns, nst)
    s0i = state_ssm_im[0].reshape(ns, nst)
    buf0 = state_conv[0, :, 0, :]
    buf1 = state_conv[0, :, 1, :]
    u_sm, z_sm, y_s, ycn_s, sr_s, si_s = pl.pallas_call(
        _small_front_kernel,
        out_shape=(jax.ShapeDtypeStruct((nsm, cw), F32), jax.ShapeDtypeStruct((nsm, cw), F32),
                   jax.ShapeDtypeStruct((ns, cw), F32), jax.ShapeDtypeStruct((ns, cw), BF16),
                   jax.ShapeDtypeStruct((ns, nst), F32), jax.ShapeDtypeStruct((ns, nst), F32)),
        scratch_shapes=[pltpu.VMEM((cw, 2 * nst), BF16), pltpu.VMEM((2 * nst, cw), BF16)],
        compiler_params=_params(None, 56),
        name="small_front",
    )(xsm, nmix, win_bf, s0r, s0i, buf0, buf1, cwt, bc, c2, abr, abi, nconv)
    u_meta = u_sm[ns:]
    z_meta8 = z_sm[ns + CHUNK - 8:]
    new_conv_s = jnp.stack([buf1, z_sm[:ns]], axis=1)[None]
    new_re_s = sr_s.reshape(1, ns, g, n)
    new_im_s = si_s.reshape(1, ns, g, n)

    wch = CHUNK * LANE
    u4c, u_tok, ycn_p, ztail = pl.pallas_call(
        _front_kernel,
        grid=(nb // nbt, seq // tt),
        in_specs=[pl.BlockSpec((nbt, tt, d), lambda b, i: (b, i, 0)),
                  _full((1, d)), _full((d, 4 * cw)), _full((8, cw)), _full((3, cw)), _full((1, cw))],
        out_specs=(pl.BlockSpec((nj, nbt, tt // CHUNK, wch), lambda b, i: (0, b, i, 0)),
                   pl.BlockSpec((nbt, tt, cw), lambda b, i: (b, i, 0)),
                   pl.BlockSpec((nbt, tt, cw), lambda b, i: (b, i, 0)),
                   pl.BlockSpec((nbt, 8, cw), lambda b, i: (b, 0, 0))),
        out_shape=(jax.ShapeDtypeStruct((nj, nb, n_chunks, wch), BF16),
                   jax.ShapeDtypeStruct((nb, seq, cw), BF16),
                   jax.ShapeDtypeStruct((nb, seq, cw), BF16),
                   jax.ShapeDtypeStruct((nb, 8, cw), F32)),
        scratch_shapes=[pltpu.VMEM((nbt, tt + 8, cw), F32), pltpu.VMEM((nj, nbt, tt, LANE), F32)],
        compiler_params=_params(("arbitrary", "arbitrary"), 52),
        name="front",
    )(x_prompt, nmix, win_bf, z_meta8, cwt, nconv)
    new_conv_p = ztail[:, 6:8, :][None]

    cc = n_chunks // 2
    um = u_meta.reshape(CHUNK, nj, LANE).transpose(1, 0, 2).reshape(nj, 1, wch)
    um = jnp.broadcast_to(um, (nj, 8 * nb, wch)).astype(BF16)
    gpt = g // nj
    nstj = 2 * gpt * n
    hch = cw // g
    y4c, s_last = pl.pallas_call(
        _ssm_kernel,
        grid=(nj, n_chunks // cc),
        in_specs=[pl.BlockSpec((1, nb, cc, wch), lambda j, t: (j, 0, t, 0)),
                  pl.BlockSpec((1, 8 * nb, wch), lambda j, t: (j, 0, 0)),
                  pl.BlockSpec((1, (CHUNK + 1) * LANE, hch), lambda j, t: (j, 0, 0)),
                  pl.BlockSpec((1, wch, 2 * n), lambda j, t: (j, 0, 0)),
                  pl.BlockSpec((1, 2 * n, wch), lambda j, t: (j, 0, 0)),
                  pl.BlockSpec((1, 2, nstj // 2), lambda j, t: (j, 0, 0))],
        out_specs=(pl.BlockSpec((1, nb, cc, wch), lambda j, t: (j, 0, t, 0)),
                   pl.BlockSpec((1, nb, 1, nstj), lambda j, t: (j, 0, 0, 0))),
        out_shape=(jax.ShapeDtypeStruct((nj, nb, n_chunks, wch), F32),
                   jax.ShapeDtypeStruct((nj, nb, 1, nstj), F32)),
        scratch_shapes=[pltpu.VMEM((nb, 1, nstj), F32), pltpu.VMEM((nb, cc, nstj), F32),
                        pltpu.VMEM((nb, cc, nstj), F32),
                        pltpu.VMEM((wch, nstj), BF16), pltpu.VMEM((nstj, wch), BF16),
                        pltpu.VMEM((CHUNK // 2, 2 * LANE, 2 * LANE), BF16)],
        compiler_params=_params(("parallel", "arbitrary"), 56),
        name="ssm",
    )(u4c, um, kc, pc, rc, a16)
    sl = s_last.reshape(nj, nb, 2, gpt, n)
    new_re_p = sl[:, :, 0].transpose(1, 0, 2, 3).reshape(1, nb, g, n)
    new_im_p = sl[:, :, 1].transpose(1, 0, 2, 3).reshape(1, nb, g, n)

    dsk = ssm_d[0].reshape(1, cw)
    wglu_bf = w_glu[0].astype(BF16)
    bglu = b_glu[0].reshape(1, cw)
    nssm = norm_out_ssm[0].reshape(1, cw)
    wout_bf = w_out[0].astype(BF16)
    nffn = norm_ffn[0].reshape(1, d)
    wr_pad = jnp.zeros((d, LANE), F32).at[:, :ne].set(w_router[0])
    wr_hi = wr_pad.astype(BF16)
    wr_lo = (wr_pad - wr_hi.astype(F32)).astype(BF16)
    br = b_router[0].reshape(ne, 1)
    wr2 = jnp.concatenate([wr_hi, wr_lo], axis=1)
    mix_w = (dsk, wglu_bf, bglu, nssm, wout_bf, nffn, wr2, br)
    mix_w_specs = [_full((1, cw)), _full((cw, cw)), _full((1, cw)), _full((1, cw)), _full((2 * cw, d)),
                   _full((1, d)), _full((d, 2 * LANE)), _full((ne, 1))]
    assert 4 + len(mix_w) == N_MIX_IN
    mix_out_shape = (jax.ShapeDtypeStruct((ta, d), F32), jax.ShapeDtypeStruct((ta, d), BF16),
                     jax.ShapeDtypeStruct((TOP_K, ta), I32), jax.ShapeDtypeStruct((TOP_K, ta), F32),
                     jax.ShapeDtypeStruct((n_tiles, ne, 1), I32))
    tpm = rows_p // TOK_TILE
    nbg = nb // nbm
    x1_all, hf_all, idx_all, gate_all, cnt = pl.pallas_call(
        _mix_kernel_prompt,
        grid=(seq // tm, nbg),
        in_specs=[pl.BlockSpec((nbm, tm, d), lambda i, b: (b, i, 0)),
                  pl.BlockSpec((nj, nbm, tm // CHUNK, wch), lambda i, b: (0, b, i, 0)),
                  pl.BlockSpec((nbm, tm, cw), lambda i, b: (b, i, 0)),
                  pl.BlockSpec((nbm, tm, cw), lambda i, b: (b, i, 0))] + mix_w_specs,
        out_specs=(pl.BlockSpec((rows_p, d), lambda i, b: (i * nbg + b, 0)),
                   pl.BlockSpec((rows_p, d), lambda i, b: (i * nbg + b, 0)),
                   pl.BlockSpec((TOP_K, rows_p), lambda i, b: (0, i * nbg + b)),
                   pl.BlockSpec((TOP_K, rows_p), lambda i, b: (0, i * nbg + b)),
                   pl.BlockSpec((tpm, ne, 1), lambda i, b: (i * nbg + b, 0, 0))),
        out_shape=mix_out_shape,
        scratch_shapes=[pltpu.VMEM((nj, nbm, tm, LANE), F32)],
        compiler_params=_params(("parallel", "parallel"), 56),
        name="mix_prompt",
    )(x_prompt, y4c, u_tok, ycn_p, *mix_w)

    last = n_tiles - 1
    any_spec = pl.BlockSpec(memory_space=pl.ANY)
    x1_all, hf_all, idx_all, gate_all, cnt = pl.pallas_call(
        _mix_kernel_sample,
        grid=(1,),
        in_specs=[_full((1, ns, d)), _full((ns, cw)), _full((1, ns, cw)),
                  _full((1, ns, cw))] + mix_w_specs + [any_spec] * 5,
        out_specs=(pl.BlockSpec((TOK_TILE, d), lambda i: (last, 0)),
                   pl.BlockSpec((TOK_TILE, d), lambda i: (last, 0)),
                   pl.BlockSpec((TOP_K, TOK_TILE), lambda i: (0, last)),
                   pl.BlockSpec((TOP_K, TOK_TILE), lambda i: (0, last)),
                   pl.BlockSpec((1, ne, 1), lambda i: (last, 0, 0))),
        out_shape=mix_out_shape,
        input_output_aliases={N_MIX_IN + k: k for k in range(5)},
        compiler_params=_params(("arbitrary",), 32),
        name="mix_sample",
    )(x_sample.reshape(1, ns, d), y_s, u_sm[:ns].astype(BF16).reshape(1, ns, cw),
      ycn_s.reshape(1, ns, cw), *mix_w, x1_all, hf_all, idx_all, gate_all, cnt)

    bm = MOE_ROWS
    cnt2 = cnt.reshape(n_tiles, ne)
    before = jnp.cumsum(cnt2, axis=0) - cnt2
    count = jnp.sum(cnt2, axis=0)
    padded = ((count + WIN + bm - 1) // bm) * bm
    pend = jnp.cumsum(padded)
    pstart = pend - padded
    phase = before % SUBLANE
    span = jnp.where(cnt2 > 0, phase + cnt2, 0)
    gstart = (pstart[None, :] + before - phase).astype(I32).reshape(-1)
    nwin = ((span + WIN - 1) // WIN).astype(I32)
    reg8 = ((span + SUBLANE - 1) // SUBLANE) * SUBLANE
    loff = (jnp.cumsum(reg8, axis=1) - reg8).astype(I32)
    tail = jnp.where(span % SUBLANE != 0, loff + (span // SUBLANE) * SUBLANE, -1).astype(I32)
    twin = jnp.sum(nwin, axis=1).astype(I32)
    n_blocks = (ta * TOP_K + ne * (WIN + bm - 1) + bm - 1) // bm
    cap = n_blocks * bm
    blk0 = jnp.arange(n_blocks, dtype=I32) * bm
    blk_e = jnp.minimum(jnp.sum((pend[None, :] <= blk0[:, None]).astype(I32), axis=1), ne - 1)
    e_ar = jnp.arange(ne, dtype=I32)
    blk_hot = blk_e[:, None] == e_ar[None, :]

    def _of_block(per_expert):
        return jnp.sum(jnp.where(blk_hot, per_expert[None, :], 0), axis=1)

    blk_valid = jnp.clip(_of_block(count) - (blk0 - _of_block(pstart)), 0, bm).astype(I32)
    has = count > 0
    later = jnp.logical_and(e_ar[None, :] > e_ar[:, None], has[None, :])
    nxt_e = jnp.min(jnp.where(later, e_ar[None, :], ne), axis=1)
    nxt_e = jnp.where(nxt_e < ne, nxt_e, -1)
    ordinal = jnp.cumsum(has.astype(I32)) - 1
    is_first = jnp.logical_and(blk_valid > 0, blk0 == _of_block(pstart))
    blk_first = jnp.where(is_first, jnp.where(_of_block(ordinal) == 0, 2, 1), 0).astype(I32)
    blk_next = _of_block(nxt_e).astype(I32)
    blk_slot = (_of_block(ordinal) % 2).astype(I32)
    nvalid = jnp.full((1,), tall, I32)
    loff_al = (WIN * (jnp.cumsum(nwin, axis=1) - nwin)).astype(I32)

    def _window_list(nw, slots, rows, first_hbm, first_buf):
        wcum = jnp.cumsum(nw, axis=1)
        wslot = jnp.arange(slots, dtype=I32)
        w_hot = jnp.logical_and(wslot[None, :, None] >= (wcum - nw)[:, None, :],
                                wslot[None, :, None] < wcum[:, None, :])

        def _of_window(per_tile_expert):
            return jnp.sum(jnp.where(w_hot, per_tile_expert[:, None, :], 0), axis=2)

        w_in_run = wslot[None, :] - _of_window(wcum - nw)
        return [(_of_window(f) + rows * w_in_run).astype(I32).reshape(-1) for f in (first_hbm, first_buf)]

    w_hbm, w_buf = _window_list(nwin, MAX_WINDOWS, WIN, gstart.reshape(n_tiles, ne), loff)
    nrun_d = _dispatch_run_rows(ne)
    zero_grp, spare_grp = nrun_d - 2 * SUBLANE, nrun_d - SUBLANE
    m_lo = jnp.where(nwin > 0, loff, spare_grp).astype(I32).reshape(-1)
    m_tg = jnp.where(jnp.logical_and(nwin > 0, tail >= 0), tail, zero_grp).astype(I32).reshape(-1)
    m_keep = (nwin == 0).astype(I32).reshape(-1)
    tables_d = (w_hbm, w_buf, m_lo, m_tg, m_keep, twin, nvalid)
    tables_c = (w_hbm, twin, nvalid)
    rowoff_d = (loff + phase).astype(I32).reshape(n_tiles, ne, 1)
    rowoff_c = (loff_al + phase).astype(I32).reshape(n_tiles, ne, 1)
    rgn_d = jnp.stack([loff, loff + reg8], axis=1).astype(I32)
    rgn_c = jnp.stack([loff_al, loff_al + WIN * nwin], axis=1).astype(I32)
    xw = d // 2 + GATE_COLS

    xs = pl.pallas_call(
        _dispatch_kernel,
        grid_spec=pltpu.PrefetchScalarGridSpec(
            num_scalar_prefetch=len(tables_d),
            grid=(n_tiles,),
            in_specs=[pl.BlockSpec((TOK_TILE, d), lambda i, *_: (i, 0)),
                      pl.BlockSpec((TOP_K, TOK_TILE), lambda i, *_: (0, i)),
                      pl.BlockSpec((TOP_K, TOK_TILE), lambda i, *_: (0, i)),
                      pl.BlockSpec((1, ne, 1), lambda i, *_: (i, 0, 0)),
                      pl.BlockSpec((1, 2, ne), lambda i, *_: (i, 0, 0))],
            out_specs=pl.BlockSpec(memory_space=pl.ANY),
            scratch_shapes=[pltpu.VMEM((2, _dispatch_run_rows(ne), xw), U32),
                            pltpu.VMEM((ne, SUBLANE, xw), U32), pltpu.SemaphoreType.DMA((2,))]),
        out_shape=jax.ShapeDtypeStruct((cap, xw), U32),
        compiler_params=_params(("arbitrary",), 40),
        name="dispatch",
    )(*tables_d, hf_all, idx_all, gate_all, rowoff_d, rgn_d)

    yb = pl.pallas_call(
        _moe_kernel,
        grid_spec=pltpu.PrefetchScalarGridSpec(
            num_scalar_prefetch=5,
            grid=(n_blocks,),
            in_specs=[pl.BlockSpec((bm, xw), lambda i, *_: (i, 0)),
                      pl.BlockSpec((1, 1, 2 * dff), lambda i, be, *_: (be[i], 0, 0)),
                      pl.BlockSpec((1, 1, d), lambda i, be, *_: (be[i], 0, 0)),
                      pl.BlockSpec(memory_space=pl.ANY), pl.BlockSpec(memory_space=pl.ANY)],
            out_specs=pl.BlockSpec((bm, d // 2), lambda i, *_: (i, 0)),
            scratch_shapes=[pltpu.VMEM((2, d, 2 * dff), F32), pltpu.VMEM((2, dff, d), F32),
                            pltpu.VMEM((d, 2 * dff), BF16), pltpu.VMEM((dff, d), BF16),
                            pltpu.SemaphoreType.DMA((2, 2))]),
        out_shape=jax.ShapeDtypeStruct((cap, d // 2), U32),
        compiler_params=_params(("arbitrary",), 58),
        name="moe",
    )(blk_e, blk_valid, blk_first, blk_next, blk_slot, xs, b_gate_up[0].reshape(ne, 1, 2 * dff),
      b_down[0].reshape(ne, 1, d), w_gate_up[0], w_down[0])

    nfin = norm_final.reshape(1, d)
    nbh = TOK_TILE // tm

    def _tile_of(s):
        return (s + n_tiles - 1) % n_tiles

    tiles_per_time = nbg * tpm

    def _yp_index(s, *_):
        t = jnp.maximum(s - 1, 0)
        return (t % tiles_per_time, t // tiles_per_time, 0)

    y_p, y_sm = pl.pallas_call(
        _combine_kernel,
        grid_spec=pltpu.PrefetchScalarGridSpec(
            num_scalar_prefetch=len(tables_c),
            grid=(n_tiles,),
            in_specs=[pl.BlockSpec((TOK_TILE, d), lambda s, *_: (_tile_of(s), 0)),
                      pl.BlockSpec((TOP_K, TOK_TILE), lambda s, *_: (0, _tile_of(s))),
                      pl.BlockSpec((1, ne, 1), lambda s, *_: (_tile_of(s), 0, 0)),
                      pl.BlockSpec((1, 2, ne), lambda s, *_: (_tile_of(s), 0, 0)),
                      pl.BlockSpec((1, d), lambda s, *_: (0, 0)),
                      pl.BlockSpec(memory_space=pl.ANY)],
            out_specs=(pl.BlockSpec((nbh, tm, d), _yp_index),
                       pl.BlockSpec((ns, d), lambda s, *_: (0, 0))),
            scratch_shapes=[pltpu.VMEM((2, _combine_run_rows(ne), d // 2), U32),
                            pltpu.SemaphoreType.DMA((2,))]),
        out_shape=(jax.ShapeDtypeStruct((nb, seq, d), F32), jax.ShapeDtypeStruct((ns, d), F32)),
        compiler_params=_params(("arbitrary",), 48),
        name="combine",
    )(*tables_c, x1_all, idx_all, rowoff_c, rgn_c, nfin, yb)

    return (y_p, y_sm.reshape(ns, 1, d), new_re_p, new_im_p, new_conv_p,
            new_re_s, new_im_s, new_conv_s)
```

```python
import math

import jax
import jax.numpy as jnp
from jax import lax
from jax.experimental import pallas as pl
from jax.experimental.pallas import tpu as pltpu

F32 = jnp.float32
BF16 = jnp.bfloat16
U32 = jnp.uint32
I32 = jnp.int32
EPS = 1e-5
CHUNK = 16
LANE = 128
TOP_K = 4
SWIGLU_LIMIT = 7.0
SWIGLU_ALPHA = 1.702
MOE_ROWS = 1024
MOE_SUB = 128
MOE_CHAIN = 512
TOK_TILE = 256
WIN = 32
SUBLANE = 8
GATE_COLS = LANE


MAX_WINDOWS = 72


def _dispatch_run_rows(ne):
    return TOP_K * TOK_TILE + ne * 2 * (SUBLANE - 1) + WIN


def _combine_run_rows(ne):
    return -(-(TOP_K * TOK_TILE + ne * (SUBLANE - 1 + WIN - 1)) // WIN) * WIN
HI_MASK = 0xFFFF0000
MIB = 1024 * 1024


def _rms(x, g):
    return x * lax.rsqrt(jnp.mean(x * x, axis=-1, keepdims=True) + EPS) * g


def _gelu_tanh(x):
    c = math.sqrt(2.0 / math.pi)
    return 0.5 * x * (1.0 + jnp.tanh(c * (x + 0.044715 * (x * x * x))))


def _params(sem, vmem_mib, **extra):
    return pltpu.CompilerParams(dimension_semantics=sem, vmem_limit_bytes=vmem_mib * MIB, **extra)


def _pack_pairs(a, b):
    return (pltpu.bitcast(a, U32) >> 16) | (pltpu.bitcast(b, U32) & jnp.uint32(HI_MASK))


def _unpack_pairs(w):
    lo = pltpu.bitcast(w << 16, F32)
    hi = pltpu.bitcast(w & jnp.uint32(HI_MASK), F32)
    return jnp.concatenate([lo, hi], axis=-1).astype(BF16)


def _iota2(shape, axis):
    return lax.broadcasted_iota(I32, shape, axis)


def _expand_cols(compact, reps_log2, n_log2):
    q = _iota2((compact.shape[1], 2 << (reps_log2 + n_log2)), 0)
    c = _iota2((compact.shape[1], 2 << (reps_log2 + n_log2)), 1)
    nmask = (1 << n_log2) - 1
    same = jnp.logical_and((q >> n_log2) == (c >> (reps_log2 + n_log2)), (q & nmask) == (c & nmask))
    return jnp.dot(compact, jnp.where(same, 1.0, 0.0).astype(BF16), preferred_element_type=F32)


def _expand_rows(compact, reps_log2, n_log2):
    r = _iota2((2 << (reps_log2 + n_log2), compact.shape[0]), 0)
    q = _iota2((2 << (reps_log2 + n_log2), compact.shape[0]), 1)
    nmask = (1 << n_log2) - 1
    same = jnp.logical_and((r >> (reps_log2 + n_log2)) == (q >> n_log2), (r & nmask) == (q & nmask))
    return jnp.dot(jnp.where(same, 1.0, 0.0).astype(BF16), compact, preferred_element_type=F32)


def _group_mask(shape, row_shift, col_shift, ngroups):
    r = _iota2(shape, 0)
    c = _iota2(shape, 1)
    return ((r >> row_shift) & (ngroups - 1)) == ((c >> col_shift) & (ngroups - 1))


def _small_front_kernel(x_ref, nmix_ref, win_ref, s0r_ref, s0i_ref, b0_ref, b1_ref, cw_ref,
                        bc_ref, cc_ref, abr_ref, abi_ref, nconv_ref,
                        u_ref, z_ref, y_ref, ycn_ref, sr_ref, si_ref, bdb_ref, cm_ref):
    ns, nst = s0r_ref.shape
    cw = u_ref.shape[1]
    n = bc_ref.shape[1] // 2
    nlog = n.bit_length() - 1
    glog = (nst // n).bit_length() - 1
    hlog = (cw >> glog).bit_length() - 1
    bdb_ref[...] = jnp.where(_group_mask(bdb_ref.shape, hlog, nlog, 1 << glog),
                             _expand_cols(bc_ref[...], glog, nlog), 0.0).astype(BF16)
    cm_ref[...] = jnp.where(_group_mask(cm_ref.shape, nlog, hlog, 1 << glog),
                            _expand_rows(cc_ref[...], glog, nlog), 0.0).astype(BF16)
    h = _rms(x_ref[...], nmix_ref[...]).astype(BF16)
    proj = jnp.dot(h, win_ref[...], preferred_element_type=F32)
    u = proj[:, 0:cw]
    zc = proj[:, cw:2 * cw]
    gb = proj[:, 2 * cw:3 * cw]
    gc = proj[:, 3 * cw:4 * cw]
    z = gc * zc
    u_ref[...] = u
    z_ref[...] = z
    bu = jnp.dot(u[:ns].astype(BF16), bdb_ref[...], preferred_element_type=F32)
    abr = abr_ref[...]
    abi = abi_ref[...]
    s0r = s0r_ref[...]
    s0i = s0i_ref[...]
    sr = abr * s0r - abi * s0i + bu[:, :nst]
    si = abr * s0i + abi * s0r + bu[:, nst:]
    sr_ref[...] = sr
    si_ref[...] = si
    scat = jnp.concatenate([sr, si], axis=-1).astype(BF16)
    y_ref[...] = jnp.dot(scat, cm_ref[...], preferred_element_type=F32)
    conv = cw_ref[0:1, :] * b0_ref[...] + cw_ref[1:2, :] * b1_ref[...] + cw_ref[2:3, :] * z[:ns]
    ycn_ref[...] = _rms(gb[:ns] * conv, nconv_ref[...]).astype(BF16)


def _front_kernel(x_ref, nmix_ref, win_ref, zm_ref, cw_ref, nconv_ref,
                  uc_ref, ut_ref, ycn_ref, zt_ref, zbuf, ubuf):
    i = pl.program_id(1)
    nb_all, tt, d = x_ref.shape
    cw = ycn_ref.shape[2]
    ncz = tt // CHUNK

    @pl.when(i == 0)
    def _():
        zbuf[:, 0:8, :] = jnp.broadcast_to(zm_ref[...][None], (nb_all, 8, cw))

    nb = nb_all
    rows = nb * tt
    for b0 in range(0, nb_all, nb):
        sq = slice(b0, b0 + nb)
        h = _rms(x_ref[sq].reshape(rows, d), nmix_ref[...]).astype(BF16)
        u = jnp.dot(h, win_ref[:, 0:cw], preferred_element_type=F32)
        ut_ref[sq] = u.astype(BF16).reshape(nb, tt, cw)
        for j in range(cw // LANE):
            ubuf[j, sq] = u[:, j * LANE:(j + 1) * LANE].reshape(nb, tt, LANE)
        for s in range(CHUNK):
            for j in range(cw // LANE):
                piece = ubuf[j, sq, pl.ds(s, ncz, stride=CHUNK), :]
                uc_ref[j, sq, :, s * LANE:(s + 1) * LANE] = piece.astype(BF16)
        zc = jnp.dot(h, win_ref[:, cw:2 * cw], preferred_element_type=F32)
        gc = jnp.dot(h, win_ref[:, 3 * cw:4 * cw], preferred_element_type=F32)
        z3 = (gc * zc).reshape(nb, tt, cw)
        zbuf[sq, 8:8 + tt, :] = z3
        z1 = zbuf[sq, 7:7 + tt, :]
        z2 = zbuf[sq, 6:6 + tt, :]
        conv = cw_ref[0:1, :] * z2 + cw_ref[1:2, :] * z1 + cw_ref[2:3, :] * z3
        gb = jnp.dot(h, win_ref[:, 2 * cw:3 * cw], preferred_element_type=F32)
        yc = gb * conv.reshape(rows, cw)
        ycn_ref[sq] = _rms(yc, nconv_ref[...]).astype(BF16).reshape(nb, tt, cw)
        tail = zbuf[sq, tt:tt + 8, :]
        zt_ref[sq] = tail
        zbuf[sq, 0:8, :] = tail


def _ssm_kernel(u_ref, um_ref, kc_ref, pc_ref, rc_ref, a16_ref, y_ref, sl_ref,
                s_carry, ds_ref, sp_ref, p_s, r_s, t_s):
    th = pl.program_id(1)
    _, nb, cc, w = u_ref.shape
    nst = p_s.shape[1]
    half = nst // 2
    rows = nb * cc
    blk = 2 * LANE
    u = u_ref[0].reshape(rows, w)

    @pl.when(th == 0)
    def _():
        hch = kc_ref.shape[2]
        gpt = LANE // hch
        hlog = hch.bit_length() - 1
        glog = gpt.bit_length() - 1
        nlog = (pc_ref.shape[2] // 2).bit_length() - 1
        p_s[...] = jnp.where(_group_mask(p_s.shape, hlog, nlog, gpt),
                             _expand_cols(pc_ref[0], glog, nlog), 0.0).astype(BF16)
        r_s[...] = jnp.where(_group_mask(r_s.shape, nlog, hlog, gpt),
                             _expand_rows(rc_ref[0], glog, nlog), 0.0).astype(BF16)
        nlag = kc_ref.shape[1] // LANE
        o = _iota2((hch, LANE), 0)
        c = _iota2((hch, LANE), 1)
        spread = jnp.where((c & (hch - 1)) == o, 1.0, 0.0).astype(BF16)
        lagm = jnp.dot(kc_ref[0], spread, preferred_element_type=F32)
        r = _iota2(lagm.shape, 0)
        c = _iota2(lagm.shape, 1)
        lagm = jnp.where(((r >> hlog) & (gpt - 1)) == (c >> hlog), lagm, 0.0).astype(BF16)
        for dlt in range(nlag // 2):
            b0 = lagm[(2 * dlt) * LANE:(2 * dlt + 1) * LANE]
            b1 = lagm[(2 * dlt + 1) * LANE:(2 * dlt + 2) * LANE]
            b2 = lagm[(2 * dlt + 2) * LANE:(2 * dlt + 3) * LANE]
            t_s[dlt, 0:LANE, 0:LANE] = b1
            t_s[dlt, 0:LANE, LANE:blk] = b2
            t_s[dlt, LANE:blk, 0:LANE] = b0
            t_s[dlt, LANE:blk, LANE:blk] = b1
        ds_ref[:, 0:8, :] = jnp.dot(um_ref[0], p_s[...], preferred_element_type=F32).reshape(nb, 8, nst)
        s_carry[...] = ds_ref[:, 0:1, :]

    ds_ref[...] = jnp.dot(u, p_s[...], preferred_element_type=F32).reshape(nb, cc, nst)
    for tb in range(w // blk):
        acc = jnp.dot(u[:, 0:blk], t_s[tb], preferred_element_type=F32)
        for sb in range(1, tb + 1):
            acc = acc + jnp.dot(u[:, sb * blk:(sb + 1) * blk], t_s[tb - sb],
                                preferred_element_type=F32)
        y_ref[0, :, :, tb * blk:(tb + 1) * blk] = acc.reshape(nb, cc, blk)
    ar = a16_ref[0, 0:1, :].reshape(1, 1, half)
    ai = a16_ref[0, 1:2, :].reshape(1, 1, half)
    sr = s_carry[:, :, 0:half]
    si = s_carry[:, :, half:nst]
    for c in range(cc):
        sp_ref[:, c:c + 1, 0:half] = sr
        sp_ref[:, c:c + 1, half:nst] = si
        dr = ds_ref[:, c:c + 1, 0:half]
        di = ds_ref[:, c:c + 1, half:nst]
        sr, si = ar * sr - ai * si + dr, ar * si + ai * sr + di
    s_carry[:, :, 0:half] = sr
    s_carry[:, :, half:nst] = si
    sl_ref[0, :, :, 0:half] = sr
    sl_ref[0, :, :, half:nst] = si

    sp = sp_ref[...].reshape(rows, nst).astype(BF16)
    for tb in range(w // blk):
        acc = jnp.dot(sp, r_s[:, tb * blk:(tb + 1) * blk], preferred_element_type=F32)
        y_ref[0, :, :, tb * blk:(tb + 1) * blk] += acc.reshape(nb, cc, blk)


N_MIX_IN = 12


def _mix_rows(x, yssm, ut, ycn, dsk_ref, wglu_ref, bglu_ref, nssm_ref, wout_ref,
              nffn_ref, wr_ref, br_ref):
    ne = br_ref.shape[0]
    y = _gelu_tanh(yssm + dsk_ref[...] * ut.astype(F32))
    glu = jnp.dot(y.astype(BF16), wglu_ref[...], preferred_element_type=F32) + bglu_ref[...]
    o = y * jax.nn.sigmoid(glu)
    ysn = _rms(o, nssm_ref[...]).astype(BF16)
    mix = jnp.concatenate([ysn, ycn], axis=-1)
    x1 = x + jnp.dot(mix, wout_ref[...], preferred_element_type=F32)
    hf = _rms(x1, nffn_ref[...])
    hf_hi = hf.astype(BF16)
    hf_lo = (hf - hf_hi.astype(F32)).astype(BF16)
    r = hf.shape[0]
    part = jnp.dot(jnp.concatenate([hf_hi, hf_lo], axis=0), wr_ref[...], preferred_element_type=F32)
    logits = (part[0:r, 0:LANE] + part[0:r, LANE:2 * LANE]) + (part[r:2 * r, 0:LANE] + part[r:2 * r, LANE:2 * LANE])
    lt = logits.T[0:ne, :] + br_ref[...]
    iota = lax.broadcasted_iota(I32, lt.shape, 0)
    vals, idxs = [], []
    sel = jnp.zeros(lt.shape, F32)
    for _ in range(TOP_K):
        m = jnp.max(lt, axis=0, keepdims=True)
        ik = jnp.min(jnp.where(lt == m, iota, ne), axis=0, keepdims=True)
        vals.append(m)
        idxs.append(ik)
        hit = iota == ik
        sel = sel + jnp.where(hit, 1.0, 0.0)
        lt = jnp.where(hit, -jnp.inf, lt)
    es = [jnp.exp(v - vals[0]) for v in vals]
    tot = es[0] + es[1] + es[2] + es[3]
    idx = jnp.concatenate(idxs, axis=0)
    gates = jnp.concatenate([e / tot for e in es], axis=0)
    return x1, hf_hi, idx, gates, sel


def _mix_kernel_prompt(*refs):
    x_ref, yc_ref = refs[0], refs[1]
    x1_ref, hf_ref, idx_ref, gate_ref, cnt_ref, ybuf = refs[N_MIX_IN:]
    nj, nb, ncz, _ = yc_ref.shape
    for s in range(CHUNK):
        for j in range(nj):
            ybuf[j, :, pl.ds(s, ncz, stride=CHUNK), :] = yc_ref[j, :, :, s * LANE:(s + 1) * LANE]
    ut_ref, ycn_ref = refs[2], refs[3]
    tt, d = x_ref.shape[1], x_ref.shape[2]
    cw = nj * LANE
    nbc = TOK_TILE // tt
    for t in range(cnt_ref.shape[0]):
        b0, r0 = t * nbc, t * TOK_TILE
        yssm = jnp.concatenate([ybuf[j, b0:b0 + nbc].reshape(TOK_TILE, LANE) for j in range(nj)], axis=-1)
        x1, hf, idx, gates, sel = _mix_rows(
            x_ref[b0:b0 + nbc].reshape(TOK_TILE, d), yssm, ut_ref[b0:b0 + nbc].reshape(TOK_TILE, cw),
            ycn_ref[b0:b0 + nbc].reshape(TOK_TILE, cw), *refs[4:N_MIX_IN])
        x1_ref[r0:r0 + TOK_TILE, :] = x1
        hf_ref[r0:r0 + TOK_TILE, :] = hf
        idx_ref[:, r0:r0 + TOK_TILE] = idx
        gate_ref[:, r0:r0 + TOK_TILE] = gates
        cnt_ref[t] = jnp.sum(sel, axis=1, keepdims=True).astype(I32)


def _mix_kernel_sample(*refs):
    x1, hf, idx, gates, sel = _mix_rows(refs[0][0], refs[1][...], refs[2][0], refs[3][0],
                                        *refs[4:N_MIX_IN])
    x1_ref, hf_ref, idx_ref, gate_ref, cnt_ref = refs[N_MIX_IN + 5:]
    ns = x1.shape[0]
    x1_ref[...] = jnp.zeros(x1_ref.shape, x1_ref.dtype)
    hf_ref[...] = jnp.zeros(hf_ref.shape, hf_ref.dtype)
    idx_ref[...] = jnp.zeros(idx_ref.shape, idx_ref.dtype)
    gate_ref[...] = jnp.zeros(gate_ref.shape, gate_ref.dtype)
    x1_ref[0:ns, :] = x1
    hf_ref[0:ns, :] = hf
    idx_ref[:, 0:ns] = idx
    gate_ref[:, 0:ns] = gates
    cnt_ref[0] = jnp.sum(sel, axis=1, keepdims=True).astype(I32)


def _split_bf16(x, parts):
    out = []
    for _ in range(parts - 1):
        p = x.astype(BF16)
        out.append(p)
        x = x - p.astype(F32)
    out.append(x.astype(BF16))
    return out


def _run_onehot(idx_ref, loff_ref, rgn_ref, tile, n_valid, nrun, gate_ref=None):
    ne = loff_ref.shape[1]
    tt = idx_ref.shape[1]
    e_iota = lax.broadcasted_iota(I32, (ne, tt), 0)
    tok = tile * tt + lax.broadcasted_iota(I32, (1, tt), 1)
    valid = tok < n_valid
    hits = [jnp.logical_and(e_iota == idx_ref[k:k + 1, :], valid) for k in range(TOP_K)]
    sel = jnp.zeros((ne, tt), F32)
    for h in hits:
        sel = sel + jnp.where(h, 1.0, 0.0)
    before = lax.broadcasted_iota(I32, (tt, tt), 0) < lax.broadcasted_iota(I32, (tt, tt), 1)
    tri = jnp.where(before, 1.0, 0.0).astype(BF16)
    base = jnp.dot(sel.astype(BF16), tri, preferred_element_type=F32) + loff_ref[0].astype(F32)
    base = jnp.where(sel > 0.0, base + 1.0, 0.0)
    b_hi = 256.0 * jnp.floor(base * (1.0 / 256.0))
    b_lo = base - b_hi
    lo2 = jnp.concatenate([rgn_ref[0, 0:1, :]] * 2, axis=1)
    hi2 = jnp.concatenate([rgn_ref[0, 1:2, :]] * 2, axis=1)
    r_i = lax.broadcasted_iota(I32, (nrun, 2 * ne), 0)
    own2 = jnp.where(jnp.logical_and(r_i >= lo2, r_i < hi2), 1.0, 0.0).astype(BF16)
    halves = jnp.concatenate([b_hi, b_lo], axis=0).astype(BF16)
    want = jnp.dot(own2, halves, preferred_element_type=F32)
    r_f = (lax.broadcasted_iota(I32, (nrun, tt), 0) + 1).astype(F32)
    smat = jnp.where(want == r_f, 1.0, 0.0).astype(BF16)
    if gate_ref is None:
        return smat
    gate_e = jnp.zeros((ne, tt), F32)
    for k, h in enumerate(hits):
        gate_e = gate_e + jnp.where(h, gate_ref[k:k + 1, :], 0.0)
    nparts = 3
    pieces = jnp.concatenate(_split_bf16(gate_e, nparts), axis=0)
    per = lax.dot_general(smat, pieces, (((1,), (1,)), ((), ())), preferred_element_type=F32)
    lo3 = jnp.concatenate([rgn_ref[0, 0:1, :]] * nparts, axis=1)
    hi3 = jnp.concatenate([rgn_ref[0, 1:2, :]] * nparts, axis=1)
    r_i3 = lax.broadcasted_iota(I32, (nrun, nparts * ne), 0)
    own3 = jnp.logical_and(r_i3 >= lo3, r_i3 < hi3)
    gcol = jnp.sum(jnp.where(own3, per, 0.0), axis=1, keepdims=True)
    return smat, gcol


def _window_copy(buf, slot, hbm, lo, g, sem, to_hbm, rows, align):
    src = buf.at[slot, pl.ds(pl.multiple_of(lo, align), rows)]
    dst = hbm.at[pl.ds(pl.multiple_of(g, align), rows)]
    if to_hbm:
        return pltpu.make_async_copy(src, dst, sem.at[slot])
    return pltpu.make_async_copy(dst, src, sem.at[slot])


def _start_windows(wg_ref, wl_ref, tw_ref, tile, slots, buf, slot, hbm, sem, to_hbm, rows, align):
    def per_window(w, c):
        k = tile * slots + w
        lo = w * rows if wl_ref is None else wl_ref[k]
        _window_copy(buf, slot, hbm, lo, wg_ref[k], sem, to_hbm, rows, align).start()
        return c

    lax.fori_loop(0, tw_ref[tile], per_window, 0)


def _wait_windows(count, buf, slot, hbm, sem, to_hbm, rows, align):
    batch = 8

    def many(w, c):
        _window_copy(buf, slot, hbm, 0, 0, sem, to_hbm, batch * rows, align).wait()
        return c

    def one(w, c):
        _window_copy(buf, slot, hbm, 0, 0, sem, to_hbm, rows, align).wait()
        return c

    lax.fori_loop(0, count // batch, many, 0)
    lax.fori_loop(0, count % batch, one, 0)


def _dispatch_kernel(wg_ref, wl_ref, mlo_ref, mtg_ref, keep_ref, tw_ref, nv_ref,
                     hf_ref, idx_ref, gate_ref, loff_ref, rgn_ref, xs_ref, buf, carry, sem):
    i = pl.program_id(0)
    nt = pl.num_programs(0)
    ne = loff_ref.shape[1]
    tt, d = hf_ref.shape
    nrun = buf.shape[1]
    slot = i % 2

    @pl.when(i == 0)
    def _():
        carry[...] = jnp.zeros(carry.shape, carry.dtype)

    smat, gcol = _run_onehot(idx_ref, loff_ref, rgn_ref, i, nv_ref[0], nrun, gate_ref)
    xr = jnp.dot(smat, hf_ref[...], preferred_element_type=F32)
    lane0 = lax.broadcasted_iota(I32, (nrun, GATE_COLS), 1) == 0
    buf[slot, :, 0:d // 2] = _pack_pairs(xr[:, :d // 2], xr[:, d // 2:])
    buf[slot, :, d // 2:] = pltpu.bitcast(jnp.where(lane0, gcol, 0.0), U32)

    def merge(e, c):
        k = i * ne + e
        lo = pl.multiple_of(mlo_ref[k], SUBLANE)
        buf[slot, pl.ds(lo, SUBLANE), :] = buf[slot, pl.ds(lo, SUBLANE), :] | carry[e]
        tg = pl.multiple_of(mtg_ref[k], SUBLANE)
        carry[e] = jnp.where(keep_ref[k] > 0, carry[e], buf[slot, pl.ds(tg, SUBLANE), :])
        return c

    lax.fori_loop(0, ne, merge, 0, unroll=4)

    @pl.when(i > 0)
    def _():
        _wait_windows(tw_ref[i - 1], buf, 1 - slot, xs_ref, sem, True, WIN, SUBLANE)

    _start_windows(wg_ref, wl_ref, tw_ref, i, MAX_WINDOWS, buf, slot, xs_ref, sem, True, WIN, SUBLANE)

    @pl.when(i == nt - 1)
    def _():
        _wait_windows(tw_ref[i], buf, slot, xs_ref, sem, True, WIN, SUBLANE)


def _expert_weight_copies(wg_hbm, wd_hbm, wg_f32, wd_f32, sem, e, slot):
    return (pltpu.make_async_copy(wg_hbm.at[e], wg_f32.at[slot], sem.at[0, slot]),
            pltpu.make_async_copy(wd_hbm.at[e], wd_f32.at[slot], sem.at[1, slot]))


def _moe_kernel(be_ref, bv_ref, first_ref, nxt_ref, slot_ref,
                x_ref, bg_ref, bd_ref, wg_hbm, wd_hbm, y_ref,
                wg_f32, wd_f32, wg_bf, wd_bf, sem):
    i = pl.program_id(0)
    e = be_ref[i]
    dff = wd_bf.shape[0]
    bm = x_ref.shape[0]
    sub = MOE_SUB
    nw = x_ref.shape[1] - GATE_COLS

    @pl.when(first_ref[i] > 0)
    def _():
        slot = slot_ref[i]

        @pl.when(first_ref[i] > 1)
        def _():
            for cp in _expert_weight_copies(wg_hbm, wd_hbm, wg_f32, wd_f32, sem, e, slot):
                cp.start()

        for cp in _expert_weight_copies(wg_hbm, wd_hbm, wg_f32, wd_f32, sem, e, slot):
            cp.wait()

        @pl.when(nxt_ref[i] >= 0)
        def _():
            for cp in _expert_weight_copies(wg_hbm, wd_hbm, wg_f32, wd_f32, sem, nxt_ref[i], 1 - slot):
                cp.start()

        wg_bf[...] = wg_f32[slot].astype(BF16)
        wd_bf[...] = wd_f32[slot].astype(BF16)

    def rows(r0, nrows):
        sizes = [MOE_CHAIN] * (nrows // MOE_CHAIN) + ([nrows % MOE_CHAIN] if nrows % MOE_CHAIN else [])
        lo = r0
        for size in sizes:
            _chain(lo, size)
            lo += size

    def _chain(lo, n):
        live = lax.broadcasted_iota(I32, (n, 1), 0) + lo < bv_ref[i]
        x = jnp.where(live, _unpack_pairs(x_ref[lo:lo + n, 0:nw]), jnp.zeros((), BF16))
        route = jnp.where(live, pltpu.bitcast(x_ref[lo:lo + n, nw:], F32)[:, 0:1], 0.0)
        gu = jnp.dot(x, wg_bf[...], preferred_element_type=F32) + bg_ref[0]
        gate = jnp.minimum(gu[:, :dff], SWIGLU_LIMIT)
        up = jnp.clip(gu[:, dff:], -SWIGLU_LIMIT, SWIGLU_LIMIT)
        h = gate * jax.nn.sigmoid(SWIGLU_ALPHA * gate) * (up + 1.0)
        y = jnp.dot(h.astype(BF16), wd_bf[...], preferred_element_type=F32) + bd_ref[0]
        yr = (route * y).astype(BF16).astype(F32)
        half = yr.shape[1] // 2
        y_ref[lo:lo + n, :] = _pack_pairs(yr[:, :half], yr[:, half:])

    nchains = bm // sub
    for live_chains in range(nchains + 1):
        lo_rows, hi_rows = (live_chains - 1) * sub, live_chains * sub

        @pl.when(jnp.logical_and(bv_ref[i] > lo_rows, bv_ref[i] <= hi_rows) if live_chains
                 else bv_ref[i] <= 0)
        def _(used=hi_rows):
            if used:
                rows(0, used)
            if used < bm:
                y_ref[used:bm, :] = jnp.zeros((bm - used, y_ref.shape[1]), y_ref.dtype)


def _combine_tile(step, nt):
    return (step + nt - 1) % nt


def _combine_kernel(wg_ref, tw_ref, nv_ref,
                    x1_ref, idx_ref, loff_ref, rgn_ref, nf_ref, yb_ref,
                    yp_ref, ys_ref, buf, sem):
    s = pl.program_id(0)
    nt = pl.num_programs(0)
    tt, d = x1_ref.shape
    nrun = buf.shape[1]
    tile = _combine_tile(s, nt)
    slot = s % 2

    @pl.when(s == 0)
    def _():
        buf[...] = jnp.zeros(buf.shape, buf.dtype)
        _start_windows(wg_ref, None, tw_ref, tile, MAX_WINDOWS, buf, slot, yb_ref, sem, False,
                       WIN, SUBLANE)

    @pl.when(s + 1 < nt)
    def _():
        _start_windows(wg_ref, None, tw_ref, _combine_tile(s + 1, nt), MAX_WINDOWS, buf, 1 - slot,
                       yb_ref, sem, False, WIN, SUBLANE)

    smat = _run_onehot(idx_ref, loff_ref, rgn_ref, tile, nv_ref[0], nrun)

    _wait_windows(tw_ref[tile], buf, slot, yb_ref, sem, False, WIN, SUBLANE)
    yrun = _unpack_pairs(buf[slot])
    moe = lax.dot_general(smat, yrun, (((0,), (0,)), ((), ())), preferred_element_type=F32)
    out = _rms(x1_ref[...] + moe, nf_ref[...])

    @pl.when(s == 0)
    def _():
        ys_ref[...] = out[0:ys_ref.shape[0], :]

    @pl.when(s > 0)
    def _():
        yp_ref[...] = out.reshape(yp_ref.shape)


def _ssm_matrices(a_re, a_im, log_dt, b_re, b_im, c_re, c_im):
    g, n = a_re.shape
    hch = b_re.shape[2]
    gpt = LANE // hch
    nj = g // gpt
    a = lax.complex(a_re, a_im)
    dta = a * jnp.exp(log_dt)[:, None]
    a_bar = jnp.exp(dta)
    bb = ((a_bar - 1.0) / a)[:, :, None] * lax.complex(b_re, b_im)
    cc = lax.complex(c_re, c_im)
    ks = jnp.arange(CHUNK + 1, dtype=F32)
    pw = jnp.exp(dta[None] * ks[:, None, None])
    kk = jnp.real(jnp.einsum('gon,kgn,gni->kgio', cc, pw[:CHUNK], bb))
    kk = jnp.concatenate([jnp.zeros_like(kk[:1]), kk], axis=0)
    kc = kk.reshape(CHUNK + 1, nj, gpt * hch, hch).transpose(1, 0, 2, 3)
    kc = kc.reshape(nj, (CHUNK + 1) * LANE, hch).astype(BF16)
    pwr, pwi = jnp.real(pw), jnp.imag(pw)
    bbr = jnp.real(bb).transpose(0, 2, 1).reshape(1, g * hch, n)
    bbi = jnp.imag(bb).transpose(0, 2, 1).reshape(1, g * hch, n)
    par = jnp.repeat(pwr[CHUNK - 1::-1][:CHUNK], hch, axis=1)
    pai = jnp.repeat(pwi[CHUNK - 1::-1][:CHUNK], hch, axis=1)
    pc = jnp.concatenate([par * bbr - pai * bbi, par * bbi + pai * bbr], axis=-1)
    pc = pc.reshape(CHUNK, nj, gpt * hch, 2 * n).transpose(1, 0, 2, 3)
    pc = pc.reshape(nj, CHUNK * LANE, 2 * n).astype(BF16)
    ccr = jnp.real(cc).transpose(2, 0, 1).reshape(n, 1, g * hch)
    cci = jnp.imag(cc).transpose(2, 0, 1).reshape(n, 1, g * hch)
    qar = jnp.repeat(pwr[1:CHUNK + 1].transpose(2, 0, 1), hch, axis=2)
    qai = jnp.repeat(pwi[1:CHUNK + 1].transpose(2, 0, 1), hch, axis=2)
    rc = jnp.stack([ccr * qar - cci * qai, -(ccr * qai + cci * qar)], axis=0)
    rc = rc.reshape(2 * n, CHUNK, nj, gpt * hch).transpose(2, 0, 1, 3)
    rc = rc.reshape(nj, 2 * n, CHUNK * LANE).astype(BF16)
    a16 = pw[CHUNK].reshape(nj, 1, gpt * n)
    a16 = jnp.concatenate([jnp.real(a16), jnp.imag(a16)], axis=1)
    bc = jnp.stack([jnp.real(bb), jnp.imag(bb)], axis=0).transpose(1, 3, 0, 2)
    bc = bc.reshape(g * hch, 2 * n).astype(BF16)
    c2 = jnp.stack([jnp.real(cc), -jnp.imag(cc)], axis=0).transpose(0, 3, 1, 2)
    c2 = c2.reshape(2 * n, g * hch).astype(BF16)
    abr = jnp.real(a_bar).reshape(1, g * n)
    abi = jnp.imag(a_bar).reshape(1, g * n)
    return kc, pc, rc, a16, bc, c2, abr, abi


def _full(shape):
    return pl.BlockSpec(shape, lambda *_: (0,) * len(shape))


def kernel(x_prompt, x_sample, state_ssm_re, state_ssm_im, state_conv, meta_tokens, norm_mix, w_in,
           ssm_a_re, ssm_a_im, ssm_log_dt, ssm_b_re, ssm_b_im, ssm_c_re, ssm_c_im, ssm_d, w_glu, b_glu,
           conv_w, norm_out_ssm, norm_out_conv, w_out, norm_ffn, w_router, b_router, w_gate_up,
           b_gate_up, w_down, b_down, norm_final):
    nb, seq, d = x_prompt.shape
    ns = x_sample.shape[0]
    depth, _, g, n = state_ssm_re.shape
    assert depth == 1 and x_sample.shape[1] == 1 and meta_tokens.shape[0] == CHUNK
    cw = conv_w.shape[2]
    nj = cw // LANE
    ne = w_router.shape[2]
    dff = w_down.shape[2]
    nst = g * n
    nbt = nb // 2
    tt = 256
    n_chunks = seq // CHUNK
    tp = nb * seq
    tall = tp + ns
    tm = 128
    nbm = nb
    rows_p = nbm * tm
    assert rows_p % TOK_TILE == 0 and TOK_TILE % tm == 0 and ns <= TOK_TILE and d % 2 == 0
    n_tiles = tp // TOK_TILE + 1
    ta = n_tiles * TOK_TILE

    kc, pc, rc, a16, bc, c2, abr, abi = _ssm_matrices(
        ssm_a_re[0], ssm_a_im[0], ssm_log_dt[0], ssm_b_re[0], ssm_b_im[0], ssm_c_re[0], ssm_c_im[0])
    win_bf = w_in[0].astype(BF16)
    nmix = norm_mix[0].reshape(1, d)
    nconv = norm_out_conv[0].reshape(1, cw)
    cwt = conv_w[0]

    xsm = jnp.concatenate([x_sample.reshape(ns, d), meta_tokens], axis=0)
    nsm = ns + CHUNK
    s0r = state_ssm_re[0].reshape(ns, nst)
    s0i = state_ssm_im[0].reshape(ns, nst)
    buf0 = state_conv[0, :, 0, :]
    buf1 = state_conv[0, :, 1, :]
    u_sm, z_sm, y_s, ycn_s, sr_s, si_s = pl.pallas_call(
        _small_front_kernel,
        out_shape=(jax.ShapeDtypeStruct((nsm, cw), F32), jax.ShapeDtypeStruct((nsm, cw), F32),
                   jax.ShapeDtypeStruct((ns, cw), F32), jax.ShapeDtypeStruct((ns, cw), BF16),
                   jax.ShapeDtypeStruct((ns, nst), F32), jax.ShapeDtypeStruct((ns, nst), F32)),
        scratch_shapes=[pltpu.VMEM((cw, 2 * nst), BF16), pltpu.VMEM((2 * nst, cw), BF16)],
        compiler_params=_params(None, 56),
        name="small_front",
    )(xsm, nmix, win_bf, s0r, s0i, buf0, buf1, cwt, bc, c2, abr, abi, nconv)
    u_meta = u_sm[ns:]
    z_meta8 = z_sm[ns + CHUNK - 8:]
    new_conv_s = jnp.stack([buf1, z_sm[:ns]], axis=1)[None]
    new_re_s = sr_s.reshape(1, ns, g, n)
    new_im_s = si_s.reshape(1, ns, g, n)

    wch = CHUNK * LANE
    u4c, u_tok, ycn_p, ztail = pl.pallas_call(
        _front_kernel,
        grid=(nb // nbt, seq // tt),
        in_specs=[pl.BlockSpec((nbt, tt, d), lambda b, i: (b, i, 0)),
                  _full((1, d)), _full((d, 4 * cw)), _full((8, cw)), _full((3, cw)), _full((1, cw))],
        out_specs=(pl.BlockSpec((nj, nbt, tt // CHUNK, wch), lambda b, i: (0, b, i, 0)),
                   pl.BlockSpec((nbt, tt, cw), lambda b, i: (b, i, 0)),
                   pl.BlockSpec((nbt, tt, cw), lambda b, i: (b, i, 0)),
                   pl.BlockSpec((nbt, 8, cw), lambda b, i: (b, 0, 0))),
        out_shape=(jax.ShapeDtypeStruct((nj, nb, n_chunks, wch), BF16),
                   jax.ShapeDtypeStruct((nb, seq, cw), BF16),
                   jax.ShapeDtypeStruct((nb, seq, cw), BF16),
                   jax.ShapeDtypeStruct((nb, 8, cw), F32)),
        scratch_shapes=[pltpu.VMEM((nbt, tt + 8, cw), F32), pltpu.VMEM((nj, nbt, tt, LANE), F32)],
        compiler_params=_params(("arbitrary", "arbitrary"), 52),
        name="front",
    )(x_prompt, nmix, win_bf, z_meta8, cwt, nconv)
    new_conv_p = ztail[:, 6:8, :][None]

    cc = n_chunks // 2
    um = u_meta.reshape(CHUNK, nj, LANE).transpose(1, 0, 2).reshape(nj, 1, wch)
    um = jnp.broadcast_to(um, (nj, 8 * nb, wch)).astype(BF16)
    gpt = g // nj
    nstj = 2 * gpt * n
    hch = cw // g
    y4c, s_last = pl.pallas_call(
        _ssm_kernel,
        grid=(nj, n_chunks // cc),
        in_specs=[pl.BlockSpec((1, nb, cc, wch), lambda j, t: (j, 0, t, 0)),
                  pl.BlockSpec((1, 8 * nb, wch), lambda j, t: (j, 0, 0)),
                  pl.BlockSpec((1, (CHUNK + 1) * LANE, hch), lambda j, t: (j, 0, 0)),
                  pl.BlockSpec((1, wch, 2 * n), lambda j, t: (j, 0, 0)),
                  pl.BlockSpec((1, 2 * n, wch), lambda j, t: (j, 0, 0)),
                  pl.BlockSpec((1, 2, nstj // 2), lambda j, t: (j, 0, 0))],
        out_specs=(pl.BlockSpec((1, nb, cc, wch), lambda j, t: (j, 0, t, 0)),
                   pl.BlockSpec((1, nb, 1, nstj), lambda j, t: (j, 0, 0, 0))),
        out_shape=(jax.ShapeDtypeStruct((nj, nb, n_chunks, wch), F32),
                   jax.ShapeDtypeStruct((nj, nb, 1, nstj), F32)),
        scratch_shapes=[pltpu.VMEM((nb, 1, nstj), F32), pltpu.VMEM((nb, cc, nstj), F32),
                        pltpu.VMEM((nb, cc, nstj), F32),
                        pltpu.VMEM((wch, nstj), BF16), pltpu.VMEM((nstj, wch), BF16),
                        pltpu.VMEM((CHUNK // 2, 2 * LANE, 2 * LANE), BF16)],
        compiler_params=_params(("parallel", "arbitrary"), 56),
        name="ssm",
    )(u4c, um, kc, pc, rc, a16)
    sl = s_last.reshape(nj, nb, 2, gpt, n)
    new_re_p = sl[:, :, 0].transpose(1, 0, 2, 3).reshape(1, nb, g, n)
    new_im_p = sl[:, :, 1].transpose(1, 0, 2, 3).reshape(1, nb, g, n)

    dsk = ssm_d[0].reshape(1, cw)
    wglu_bf = w_glu[0].astype(BF16)
    bglu = b_glu[0].reshape(1, cw)
    nssm = norm_out_ssm[0].reshape(1, cw)
    wout_bf = w_out[0].astype(BF16)
    nffn = norm_ffn[0].reshape(1, d)
    wr_pad = jnp.zeros((d, LANE), F32).at[:, :ne].set(w_router[0])
    wr_hi = wr_pad.astype(BF16)
    wr_lo = (wr_pad - wr_hi.astype(F32)).astype(BF16)
    br = b_router[0].reshape(ne, 1)
    wr2 = jnp.concatenate([wr_hi, wr_lo], axis=1)
    mix_w = (dsk, wglu_bf, bglu, nssm, wout_bf, nffn, wr2, br)
    mix_w_specs = [_full((1, cw)), _full((cw, cw)), _full((1, cw)), _full((1, cw)), _full((2 * cw, d)),
                   _full((1, d)), _full((d, 2 * LANE)), _full((ne, 1))]
    assert 4 + len(mix_w) == N_MIX_IN
    mix_out_shape = (jax.ShapeDtypeStruct((ta, d), F32), jax.ShapeDtypeStruct((ta, d), BF16),
                     jax.ShapeDtypeStruct((TOP_K, ta), I32), jax.ShapeDtypeStruct((TOP_K, ta), F32),
                     jax.ShapeDtypeStruct((n_tiles, ne, 1), I32))
    tpm = rows_p // TOK_TILE
    nbg = nb // nbm
    x1_all, hf_all, idx_all, gate_all, cnt = pl.pallas_call(
        _mix_kernel_prompt,
        grid=(seq // tm, nbg),
        in_specs=[pl.BlockSpec((nbm, tm, d), lambda i, b: (b, i, 0)),
                  pl.BlockSpec((nj, nbm, tm // CHUNK, wch), lambda i, b: (0, b, i, 0)),
                  pl.BlockSpec((nbm, tm, cw), lambda i, b: (b, i, 0)),
                  pl.BlockSpec((nbm, tm, cw), lambda i, b: (b, i, 0))] + mix_w_specs,
        out_specs=(pl.BlockSpec((rows_p, d), lambda i, b: (i * nbg + b, 0)),
                   pl.BlockSpec((rows_p, d), lambda i, b: (i * nbg + b, 0)),
                   pl.BlockSpec((TOP_K, rows_p), lambda i, b: (0, i * nbg + b)),
                   pl.BlockSpec((TOP_K, rows_p), lambda i, b: (0, i * nbg + b)),
                   pl.BlockSpec((tpm, ne, 1), lambda i, b: (i * nbg + b, 0, 0))),
        out_shape=mix_out_shape,
        scratch_shapes=[pltpu.VMEM((nj, nbm, tm, LANE), F32)],
        compiler_params=_params(("parallel", "parallel"), 56),
        name="mix_prompt",
    )(x_prompt, y4c, u_tok, ycn_p, *mix_w)

    last = n_tiles - 1
    any_spec = pl.BlockSpec(memory_space=pl.ANY)
    x1_all, hf_all, idx_all, gate_all, cnt = pl.pallas_call(
        _mix_kernel_sample,
        grid=(1,),
        in_specs=[_full((1, ns, d)), _full((ns, cw)), _full((1, ns, cw)),
                  _full((1, ns, cw))] + mix_w_specs + [any_spec] * 5,
        out_specs=(pl.BlockSpec((TOK_TILE, d), lambda i: (last, 0)),
                   pl.BlockSpec((TOK_TILE, d), lambda i: (last, 0)),
                   pl.BlockSpec((TOP_K, TOK_TILE), lambda i: (0, last)),
                   pl.BlockSpec((TOP_K, TOK_TILE), lambda i: (0, last)),
                   pl.BlockSpec((1, ne, 1), lambda i: (last, 0, 0))),
        out_shape=mix_out_shape,
        input_output_aliases={N_MIX_IN + k: k for k in range(5)},
        compiler_params=_params(("arbitrary",), 32),
        name="mix_sample",
    )(x_sample.reshape(1, ns, d), y_s, u_sm[:ns].astype(BF16).reshape(1, ns, cw),
      ycn_s.reshape(1, ns, cw), *mix_w, x1_all, hf_all, idx_all, gate_all, cnt)

    bm = MOE_ROWS
    cnt2 = cnt.reshape(n_tiles, ne)
    before = jnp.cumsum(cnt2, axis=0) - cnt2
    count = jnp.sum(cnt2, axis=0)
    padded = ((count + WIN + bm - 1) // bm) * bm
    pend = jnp.cumsum(padded)
    pstart = pend - padded
    phase = before % SUBLANE
    span = jnp.where(cnt2 > 0, phase + cnt2, 0)
    gstart = (pstart[None, :] + before - phase).astype(I32).reshape(-1)
    nwin = ((span + WIN - 1) // WIN).astype(I32)
    reg8 = ((span + SUBLANE - 1) // SUBLANE) * SUBLANE
    loff = (jnp.cumsum(reg8, axis=1) - reg8).astype(I32)
    tail = jnp.where(span % SUBLANE != 0, loff + (span // SUBLANE) * SUBLANE, -1).astype(I32)
    twin = jnp.sum(nwin, axis=1).astype(I32)
    n_blocks = (ta * TOP_K + ne * (WIN + bm - 1) + bm - 1) // bm
    cap = n_blocks * bm
    blk0 = jnp.arange(n_blocks, dtype=I32) * bm
    blk_e = jnp.minimum(jnp.sum((pend[None, :] <= blk0[:, None]).astype(I32), axis=1), ne - 1)
    e_ar = jnp.arange(ne, dtype=I32)
    blk_hot = blk_e[:, None] == e_ar[None, :]

    def _of_block(per_expert):
        return jnp.sum(jnp.where(blk_hot, per_expert[None, :], 0), axis=1)

    blk_valid = jnp.clip(_of_block(count) - (blk0 - _of_block(pstart)), 0, bm).astype(I32)
    has = count > 0
    later = jnp.logical_and(e_ar[None, :] > e_ar[:, None], has[None, :])
    nxt_e = jnp.min(jnp.where(later, e_ar[None, :], ne), axis=1)
    nxt_e = jnp.where(nxt_e < ne, nxt_e, -1)
    ordinal = jnp.cumsum(has.astype(I32)) - 1
    is_first = jnp.logical_and(blk_valid > 0, blk0 == _of_block(pstart))
    blk_first = jnp.where(is_first, jnp.where(_of_block(ordinal) == 0, 2, 1), 0).astype(I32)
    blk_next = _of_block(nxt_e).astype(I32)
    blk_slot = (_of_block(ordinal) % 2).astype(I32)
    nvalid = jnp.full((1,), tall, I32)
    loff_al = (WIN * (jnp.cumsum(nwin, axis=1) - nwin)).astype(I32)

    def _window_list(nw, slots, rows, first_hbm, first_buf):
        wcum = jnp.cumsum(nw, axis=1)
        wslot = jnp.arange(slots, dtype=I32)
        w_hot = jnp.logical_and(wslot[None, :, None] >= (wcum - nw)[:, None, :],
                                wslot[None, :, None] < wcum[:, None, :])

        def _of_window(per_tile_expert):
            return jnp.sum(jnp.where(w_hot, per_tile_expert[:, None, :], 0), axis=2)

        w_in_run = wslot[None, :] - _of_window(wcum - nw)
        return [(_of_window(f) + rows * w_in_run).astype(I32).reshape(-1) for f in (first_hbm, first_buf)]

    w_hbm, w_buf = _window_list(nwin, MAX_WINDOWS, WIN, gstart.reshape(n_tiles, ne), loff)
    nrun_d = _dispatch_run_rows(ne)
    zero_grp, spare_grp = nrun_d - 2 * SUBLANE, nrun_d - SUBLANE
    m_lo = jnp.where(nwin > 0, loff, spare_grp).astype(I32).reshape(-1)
    m_tg = jnp.where(jnp.logical_and(nwin > 0, tail >= 0), tail, zero_grp).astype(I32).reshape(-1)
    m_keep = (nwin == 0).astype(I32).reshape(-1)
    tables_d = (w_hbm, w_buf, m_lo, m_tg, m_keep, twin, nvalid)
    tables_c = (w_hbm, twin, nvalid)
    rowoff_d = (loff + phase).astype(I32).reshape(n_tiles, ne, 1)
    rowoff_c = (loff_al + phase).astype(I32).reshape(n_tiles, ne, 1)
    rgn_d = jnp.stack([loff, loff + reg8], axis=1).astype(I32)
    rgn_c = jnp.stack([loff_al, loff_al + WIN * nwin], axis=1).astype(I32)
    xw = d // 2 + GATE_COLS

    xs = pl.pallas_call(
        _dispatch_kernel,
        grid_spec=pltpu.PrefetchScalarGridSpec(
            num_scalar_prefetch=len(tables_d),
            grid=(n_tiles,),
            in_specs=[pl.BlockSpec((TOK_TILE, d), lambda i, *_: (i, 0)),
                      pl.BlockSpec((TOP_K, TOK_TILE), lambda i, *_: (0, i)),
                      pl.BlockSpec((TOP_K, TOK_TILE), lambda i, *_: (0, i)),
                      pl.BlockSpec((1, ne, 1), lambda i, *_: (i, 0, 0)),
                      pl.BlockSpec((1, 2, ne), lambda i, *_: (i, 0, 0))],
            out_specs=pl.BlockSpec(memory_space=pl.ANY),
            scratch_shapes=[pltpu.VMEM((2, _dispatch_run_rows(ne), xw), U32),
                            pltpu.VMEM((ne, SUBLANE, xw), U32), pltpu.SemaphoreType.DMA((2,))]),
        out_shape=jax.ShapeDtypeStruct((cap, xw), U32),
        compiler_params=_params(("arbitrary",), 40),
        name="dispatch",
    )(*tables_d, hf_all, idx_all, gate_all, rowoff_d, rgn_d)

    yb = pl.pallas_call(
        _moe_kernel,
        grid_spec=pltpu.PrefetchScalarGridSpec(
            num_scalar_prefetch=5,
            grid=(n_blocks,),
            in_specs=[pl.BlockSpec((bm, xw), lambda i, *_: (i, 0)),
                      pl.BlockSpec((1, 1, 2 * dff), lambda i, be, *_: (be[i], 0, 0)),
                      pl.BlockSpec((1, 1, d), lambda i, be, *_: (be[i], 0, 0)),
                      pl.BlockSpec(memory_space=pl.ANY), pl.BlockSpec(memory_space=pl.ANY)],
            out_specs=pl.BlockSpec((bm, d // 2), lambda i, *_: (i, 0)),
            scratch_shapes=[pltpu.VMEM((2, d, 2 * dff), F32), pltpu.VMEM((2, dff, d), F32),
                            pltpu.VMEM((d, 2 * dff), BF16), pltpu.VMEM((dff, d), BF16),
                            pltpu.SemaphoreType.DMA((2, 2))]),
        out_shape=jax.ShapeDtypeStruct((cap, d // 2), U32),
        compiler_params=_params(("arbitrary",), 58),
        name="moe",
    )(blk_e, blk_valid, blk_first, blk_next, blk_slot, xs, b_gate_up[0].reshape(ne, 1, 2 * dff),
      b_down[0].reshape(ne, 1, d), w_gate_up[0], w_down[0])

    nfin = norm_final.reshape(1, d)
    nbh = TOK_TILE // tm

    def _tile_of(s):
        return (s + n_tiles - 1) % n_tiles

    tiles_per_time = nbg * tpm

    def _yp_index(s, *_):
        t = jnp.maximum(s - 1, 0)
        return (t % tiles_per_time, t // tiles_per_time, 0)

    y_p, y_sm = pl.pallas_call(
        _combine_kernel,
        grid_spec=pltpu.PrefetchScalarGridSpec(
            num_scalar_prefetch=len(tables_c),
            grid=(n_tiles,),
            in_specs=[pl.BlockSpec((TOK_TILE, d), lambda s, *_: (_tile_of(s), 0)),
                      pl.BlockSpec((TOP_K, TOK_TILE), lambda s, *_: (0, _tile_of(s))),
                      pl.BlockSpec((1, ne, 1), lambda s, *_: (_tile_of(s), 0, 0)),
                      pl.BlockSpec((1, 2, ne), lambda s, *_: (_tile_of(s), 0, 0)),
                      pl.BlockSpec((1, d), lambda s, *_: (0, 0)),
                      pl.BlockSpec(memory_space=pl.ANY)],
            out_specs=(pl.BlockSpec((nbh, tm, d), _yp_index),
                       pl.BlockSpec((ns, d), lambda s, *_: (0, 0))),
            scratch_shapes=[pltpu.VMEM((2, _combine_run_rows(ne), d // 2), U32),
                            pltpu.SemaphoreType.DMA((2,))]),
        out_shape=(jax.ShapeDtypeStruct((nb, seq, d), F32), jax.ShapeDtypeStruct((ns, d), F32)),
        compiler_params=_params(("arbitrary",), 48),
        name="combine",
    )(*tables_c, x1_all, idx_all, rowoff_c, rgn_c, nfin, yb)

    return (y_p, y_sm.reshape(ns, 1, d), new_re_p, new_im_p, new_conv_p,
            new_re_s, new_im_s, new_conv_s)
```

```python
import math

import jax
import jax.numpy as jnp
from jax import lax
from jax.experimental import pallas as pl
from jax.experimental.pallas import tpu as pltpu

F32 = jnp.float32
BF16 = jnp.bfloat16
U32 = jnp.uint32
I32 = jnp.int32
EPS = 1e-5
CHUNK = 16
LANE = 128
TOP_K = 4
SWIGLU_LIMIT = 7.0
SWIGLU_ALPHA = 1.702
MOE_ROWS = 1024
MOE_SUB = 256
MOE_CHAIN = 512
TOK_TILE = 256
WIN = 32
SUBLANE = 8
GATE_COLS = LANE


MAX_WINDOWS = 72


def _dispatch_run_rows(ne):
    return TOP_K * TOK_TILE + ne * 2 * (SUBLANE - 1) + WIN


def _combine_run_rows(ne):
    return -(-(TOP_K * TOK_TILE + ne * (SUBLANE - 1 + WIN - 1)) // WIN) * WIN
HI_MASK = 0xFFFF0000
MIB = 1024 * 1024


def _rms(x, g):
    return x * lax.rsqrt(jnp.mean(x * x, axis=-1, keepdims=True) + EPS) * g


def _gelu_tanh(x):
    c = math.sqrt(2.0 / math.pi)
    return 0.5 * x * (1.0 + jnp.tanh(c * (x + 0.044715 * (x * x * x))))


def _params(sem, vmem_mib, **extra):
    return pltpu.CompilerParams(dimension_semantics=sem, vmem_limit_bytes=vmem_mib * MIB, **extra)


def _pack_pairs(a, b):
    return (pltpu.bitcast(a, U32) >> 16) | (pltpu.bitcast(b, U32) & jnp.uint32(HI_MASK))


def _unpack_pairs(w):
    lo = pltpu.bitcast(w << 16, F32)
    hi = pltpu.bitcast(w & jnp.uint32(HI_MASK), F32)
    return jnp.concatenate([lo, hi], axis=-1).astype(BF16)


def _iota2(shape, axis):
    return lax.broadcasted_iota(I32, shape, axis)


def _expand_cols(compact, reps_log2, n_log2):
    q = _iota2((compact.shape[1], 2 << (reps_log2 + n_log2)), 0)
    c = _iota2((compact.shape[1], 2 << (reps_log2 + n_log2)), 1)
    nmask = (1 << n_log2) - 1
    same = jnp.logical_and((q >> n_log2) == (c >> (reps_log2 + n_log2)), (q & nmask) == (c & nmask))
    return jnp.dot(compact, jnp.where(same, 1.0, 0.0).astype(BF16), preferred_element_type=F32)


def _expand_rows(compact, reps_log2, n_log2):
    r = _iota2((2 << (reps_log2 + n_log2), compact.shape[0]), 0)
    q = _iota2((2 << (reps_log2 + n_log2), compact.shape[0]), 1)
    nmask = (1 << n_log2) - 1
    same = jnp.logical_and((r >> (reps_log2 + n_log2)) == (q >> n_log2), (r & nmask) == (q & nmask))
    return jnp.dot(jnp.where(same, 1.0, 0.0).astype(BF16), compact, preferred_element_type=F32)


def _group_mask(shape, row_shift, col_shift, ngroups):
    r = _iota2(shape, 0)
    c = _iota2(shape, 1)
    return ((r >> row_shift) & (ngroups - 1)) == ((c >> col_shift) & (ngroups - 1))


def _small_front_kernel(x_ref, nmix_ref, win_ref, s0r_ref, s0i_ref, b0_ref, b1_ref, cw_ref,
                        bc_ref, cc_ref, abr_ref, abi_ref, nconv_ref,
                        u_ref, z_ref, y_ref, ycn_ref, sr_ref, si_ref, bdb_ref, cm_ref):
    ns, nst = s0r_ref.shape
    cw = u_ref.shape[1]
    n = bc_ref.shape[1] // 2
    nlog = n.bit_length() - 1
    glog = (nst // n).bit_length() - 1
    hlog = (cw >> glog).bit_length() - 1
    bdb_ref[...] = jnp.where(_group_mask(bdb_ref.shape, hlog, nlog, 1 << glog),
                             _expand_cols(bc_ref[...], glog, nlog), 0.0).astype(BF16)
    cm_ref[...] = jnp.where(_group_mask(cm_ref.shape, nlog, hlog, 1 << glog),
                            _expand_rows(cc_ref[...], glog, nlog), 0.0).astype(BF16)
    h = _rms(x_ref[...], nmix_ref[...]).astype(BF16)
    proj = jnp.dot(h, win_ref[...], preferred_element_type=F32)
    u = proj[:, 0:cw]
    zc = proj[:, cw:2 * cw]
    gb = proj[:, 2 * cw:3 * cw]
    gc = proj[:, 3 * cw:4 * cw]
    z = gc * zc
    u_ref[...] = u
    z_ref[...] = z
    bu = jnp.dot(u[:ns].astype(BF16), bdb_ref[...], preferred_element_type=F32)
    abr = abr_ref[...]
    abi = abi_ref[...]
    s0r = s0r_ref[...]
    s0i = s0i_ref[...]
    sr = abr * s0r - abi * s0i + bu[:, :nst]
    si = abr * s0i + abi * s0r + bu[:, nst:]
    sr_ref[...] = sr
    si_ref[...] = si
    scat = jnp.concatenate([sr, si], axis=-1).astype(BF16)
    y_ref[...] = jnp.dot(scat, cm_ref[...], preferred_element_type=F32)
    conv = cw_ref[0:1, :] * b0_ref[...] + cw_ref[1:2, :] * b1_ref[...] + cw_ref[2:3, :] * z[:ns]
    ycn_ref[...] = _rms(gb[:ns] * conv, nconv_ref[...]).astype(BF16)


def _front_kernel(x_ref, nmix_ref, win_ref, zm_ref, cw_ref, nconv_ref,
                  uc_ref, ut_ref, ycn_ref, zt_ref, zbuf, ubuf):
    i = pl.program_id(1)
    nb_all, tt, d = x_ref.shape
    cw = ycn_ref.shape[2]
    ncz = tt // CHUNK

    @pl.when(i == 0)
    def _():
        zbuf[:, 0:8, :] = jnp.broadcast_to(zm_ref[...][None], (nb_all, 8, cw))

    nb = nb_all
    rows = nb * tt
    for b0 in range(0, nb_all, nb):
        sq = slice(b0, b0 + nb)
        h = _rms(x_ref[sq].reshape(rows, d), nmix_ref[...]).astype(BF16)
        u = jnp.dot(h, win_ref[:, 0:cw], preferred_element_type=F32)
        ut_ref[sq] = u.astype(BF16).reshape(nb, tt, cw)
        for j in range(cw // LANE):
            ubuf[j, sq] = u[:, j * LANE:(j + 1) * LANE].reshape(nb, tt, LANE)
        for s in range(CHUNK):
            for j in range(cw // LANE):
                piece = ubuf[j, sq, pl.ds(s, ncz, stride=CHUNK), :]
                uc_ref[j, sq, :, s * LANE:(s + 1) * LANE] = piece.astype(BF16)
        zc = jnp.dot(h, win_ref[:, cw:2 * cw], preferred_element_type=F32)
        gc = jnp.dot(h, win_ref[:, 3 * cw:4 * cw], preferred_element_type=F32)
        z3 = (gc * zc).reshape(nb, tt, cw)
        zbuf[sq, 8:8 + tt, :] = z3
        z1 = zbuf[sq, 7:7 + tt, :]
        z2 = zbuf[sq, 6:6 + tt, :]
        conv = cw_ref[0:1, :] * z2 + cw_ref[1:2, :] * z1 + cw_ref[2:3, :] * z3
        gb = jnp.dot(h, win_ref[:, 2 * cw:3 * cw], preferred_element_type=F32)
        yc = gb * conv.reshape(rows, cw)
        ycn_ref[sq] = _rms(yc, nconv_ref[...]).astype(BF16).reshape(nb, tt, cw)
        tail = zbuf[sq, tt:tt + 8, :]
        zt_ref[sq] = tail
        zbuf[sq, 0:8, :] = tail


def _ssm_kernel(u_ref, um_ref, kc_ref, pc_ref, rc_ref, a16_ref, y_ref, sl_ref,
                s_carry, ds_ref, sp_ref, p_s, r_s, t_s):
    th = pl.program_id(1)
    _, nb, cc, w = u_ref.shape
    nst = p_s.shape[1]
    half = nst // 2
    rows = nb * cc
    blk = 2 * LANE
    u = u_ref[0].reshape(rows, w)

    @pl.when(th == 0)
    def _():
        hch = kc_ref.shape[2]
        gpt = LANE // hch
        hlog = hch.bit_length() - 1
        glog = gpt.bit_length() - 1
        nlog = (pc_ref.shape[2] // 2).bit_length() - 1
        p_s[...] = jnp.where(_group_mask(p_s.shape, hlog, nlog, gpt),
                             _expand_cols(pc_ref[0], glog, nlog), 0.0).astype(BF16)
        r_s[...] = jnp.where(_group_mask(r_s.shape, nlog, hlog, gpt),
                             _expand_rows(rc_ref[0], glog, nlog), 0.0).astype(BF16)
        nlag = kc_ref.shape[1] // LANE
        o = _iota2((hch, LANE), 0)
        c = _iota2((hch, LANE), 1)
        spread = jnp.where((c & (hch - 1)) == o, 1.0, 0.0).astype(BF16)
        lagm = jnp.dot(kc_ref[0], spread, preferred_element_type=F32)
        r = _iota2(lagm.shape, 0)
        c = _iota2(lagm.shape, 1)
        lagm = jnp.where(((r >> hlog) & (gpt - 1)) == (c >> hlog), lagm, 0.0).astype(BF16)
        for dlt in range(nlag // 2):
            b0 = lagm[(2 * dlt) * LANE:(2 * dlt + 1) * LANE]
            b1 = lagm[(2 * dlt + 1) * LANE:(2 * dlt + 2) * LANE]
            b2 = lagm[(2 * dlt + 2) * LANE:(2 * dlt + 3) * LANE]
            t_s[dlt, 0:LANE, 0:LANE] = b1
            t_s[dlt, 0:LANE, LANE:blk] = b2
            t_s[dlt, LANE:blk, 0:LANE] = b0
            t_s[dlt, LANE:blk, LANE:blk] = b1
        ds_ref[:, 0:8, :] = jnp.dot(um_ref[0], p_s[...], preferred_element_type=F32).reshape(nb, 8, nst)
        s_carry[...] = ds_ref[:, 0:1, :]

    ds_ref[...] = jnp.dot(u, p_s[...], preferred_element_type=F32).reshape(nb, cc, nst)
    for tb in range(w // blk):
        acc = jnp.dot(u[:, 0:blk], t_s[tb], preferred_element_type=F32)
        for sb in range(1, tb + 1):
            acc = acc + jnp.dot(u[:, sb * blk:(sb + 1) * blk], t_s[tb - sb],
                                preferred_element_type=F32)
        y_ref[0, :, :, tb * blk:(tb + 1) * blk] = acc.reshape(nb, cc, blk)
    ar = a16_ref[0, 0:1, :].reshape(1, 1, half)
    ai = a16_ref[0, 1:2, :].reshape(1, 1, half)
    sr = s_carry[:, :, 0:half]
    si = s_carry[:, :, half:nst]
    for c in range(cc):
        sp_ref[:, c:c + 1, 0:half] = sr
        sp_ref[:, c:c + 1, half:nst] = si
        dr = ds_ref[:, c:c + 1, 0:half]
        di = ds_ref[:, c:c + 1, half:nst]
        sr, si = ar * sr - ai * si + dr, ar * si + ai * sr + di
    s_carry[:, :, 0:half] = sr
    s_carry[:, :, half:nst] = si
    sl_ref[0, :, :, 0:half] = sr
    sl_ref[0, :, :, half:nst] = si

    sp = sp_ref[...].reshape(rows, nst).astype(BF16)
    for tb in range(w // blk):
        acc = jnp.dot(sp, r_s[:, tb * blk:(tb + 1) * blk], preferred_element_type=F32)
        y_ref[0, :, :, tb * blk:(tb + 1) * blk] += acc.reshape(nb, cc, blk)


N_MIX_IN = 12


def _mix_rows(x, yssm, ut, ycn, dsk_ref, wglu_ref, bglu_ref, nssm_ref, wout_ref,
              nffn_ref, wr_ref, br_ref):
    ne = br_ref.shape[0]
    y = _gelu_tanh(yssm + dsk_ref[...] * ut.astype(F32))
    glu = jnp.dot(y.astype(BF16), wglu_ref[...], preferred_element_type=F32) + bglu_ref[...]
    o = y * jax.nn.sigmoid(glu)
    ysn = _rms(o, nssm_ref[...]).astype(BF16)
    mix = jnp.concatenate([ysn, ycn], axis=-1)
    x1 = x + jnp.dot(mix, wout_ref[...], preferred_element_type=F32)
    hf = _rms(x1, nffn_ref[...])
    hf_hi = hf.astype(BF16)
    hf_lo = (hf - hf_hi.astype(F32)).astype(BF16)
    r = hf.shape[0]
    part = jnp.dot(jnp.concatenate([hf_hi, hf_lo], axis=0), wr_ref[...], preferred_element_type=F32)
    logits = (part[0:r, 0:LANE] + part[0:r, LANE:2 * LANE]) + (part[r:2 * r, 0:LANE] + part[r:2 * r, LANE:2 * LANE])
    lt = logits.T[0:ne, :] + br_ref[...]
    iota = lax.broadcasted_iota(I32, lt.shape, 0)
    vals, idxs = [], []
    sel = jnp.zeros(lt.shape, F32)
    for _ in range(TOP_K):
        m = jnp.max(lt, axis=0, keepdims=True)
        ik = jnp.min(jnp.where(lt == m, iota, ne), axis=0, keepdims=True)
        vals.append(m)
        idxs.append(ik)
        hit = iota == ik
        sel = sel + jnp.where(hit, 1.0, 0.0)
        lt = jnp.where(hit, -jnp.inf, lt)
    es = [jnp.exp(v - vals[0]) for v in vals]
    tot = es[0] + es[1] + es[2] + es[3]
    idx = jnp.concatenate(idxs, axis=0)
    gates = jnp.concatenate([e / tot for e in es], axis=0)
    return x1, hf_hi, idx, gates, sel


def _mix_kernel_prompt(*refs):
    x_ref, yc_ref = refs[0], refs[1]
    x1_ref, hf_ref, idx_ref, gate_ref, cnt_ref, ybuf = refs[N_MIX_IN:]
    nj, nb, ncz, _ = yc_ref.shape
    for s in range(CHUNK):
        for j in range(nj):
            ybuf[j, :, pl.ds(s, ncz, stride=CHUNK), :] = yc_ref[j, :, :, s * LANE:(s + 1) * LANE]
    ut_ref, ycn_ref = refs[2], refs[3]
    tt, d = x_ref.shape[1], x_ref.shape[2]
    cw = nj * LANE
    nbc = TOK_TILE // tt
    for t in range(cnt_ref.shape[0]):
        b0, r0 = t * nbc, t * TOK_TILE
        yssm = jnp.concatenate([ybuf[j, b0:b0 + nbc].reshape(TOK_TILE, LANE) for j in range(nj)], axis=-1)
        x1, hf, idx, gates, sel = _mix_rows(
            x_ref[b0:b0 + nbc].reshape(TOK_TILE, d), yssm, ut_ref[b0:b0 + nbc].reshape(TOK_TILE, cw),
            ycn_ref[b0:b0 + nbc].reshape(TOK_TILE, cw), *refs[4:N_MIX_IN])
        x1_ref[r0:r0 + TOK_TILE, :] = x1
        hf_ref[r0:r0 + TOK_TILE, :] = hf
        idx_ref[:, r0:r0 + TOK_TILE] = idx
        gate_ref[:, r0:r0 + TOK_TILE] = gates
        cnt_ref[t] = jnp.sum(sel, axis=1, keepdims=True).astype(I32)


def _mix_kernel_sample(*refs):
    x1, hf, idx, gates, sel = _mix_rows(refs[0][0], refs[1][...], refs[2][0], refs[3][0],
                                        *refs[4:N_MIX_IN])
    x1_ref, hf_ref, idx_ref, gate_ref, cnt_ref = refs[N_MIX_IN + 5:]
    ns = x1.shape[0]
    x1_ref[...] = jnp.zeros(x1_ref.shape, x1_ref.dtype)
    hf_ref[...] = jnp.zeros(hf_ref.shape, hf_ref.dtype)
    idx_ref[...] = jnp.zeros(idx_ref.shape, idx_ref.dtype)
    gate_ref[...] = jnp.zeros(gate_ref.shape, gate_ref.dtype)
    x1_ref[0:ns, :] = x1
    hf_ref[0:ns, :] = hf
    idx_ref[:, 0:ns] = idx
    gate_ref[:, 0:ns] = gates
    cnt_ref[0] = jnp.sum(sel, axis=1, keepdims=True).astype(I32)


def _split_bf16(x, parts):
    out = []
    for _ in range(parts - 1):
        p = x.astype(BF16)
        out.append(p)
        x = x - p.astype(F32)
    out.append(x.astype(BF16))
    return out


def _run_onehot(idx_ref, loff_ref, rgn_ref, tile, n_valid, nrun, gate_ref=None):
    ne = loff_ref.shape[1]
    tt = idx_ref.shape[1]
    e_iota = lax.broadcasted_iota(I32, (ne, tt), 0)
    tok = tile * tt + lax.broadcasted_iota(I32, (1, tt), 1)
    valid = tok < n_valid
    hits = [jnp.logical_and(e_iota == idx_ref[k:k + 1, :], valid) for k in range(TOP_K)]
    sel = jnp.zeros((ne, tt), F32)
    for h in hits:
        sel = sel + jnp.where(h, 1.0, 0.0)
    before = lax.broadcasted_iota(I32, (tt, tt), 0) < lax.broadcasted_iota(I32, (tt, tt), 1)
    tri = jnp.where(before, 1.0, 0.0).astype(BF16)
    base = jnp.dot(sel.astype(BF16), tri, preferred_element_type=F32) + loff_ref[0].astype(F32)
    base = jnp.where(sel > 0.0, base + 1.0, 0.0)
    b_hi = 256.0 * jnp.floor(base * (1.0 / 256.0))
    b_lo = base - b_hi
    lo2 = jnp.concatenate([rgn_ref[0, 0:1, :]] * 2, axis=1)
    hi2 = jnp.concatenate([rgn_ref[0, 1:2, :]] * 2, axis=1)
    r_i = lax.broadcasted_iota(I32, (nrun, 2 * ne), 0)
    own2 = jnp.where(jnp.logical_and(r_i >= lo2, r_i < hi2), 1.0, 0.0).astype(BF16)
    halves = jnp.concatenate([b_hi, b_lo], axis=0).astype(BF16)
    want = jnp.dot(own2, halves, preferred_element_type=F32)
    r_f = (lax.broadcasted_iota(I32, (nrun, tt), 0) + 1).astype(F32)
    smat = jnp.where(want == r_f, 1.0, 0.0).astype(BF16)
    if gate_ref is None:
        return smat
    gate_e = jnp.zeros((ne, tt), F32)
    for k, h in enumerate(hits):
        gate_e = gate_e + jnp.where(h, gate_ref[k:k + 1, :], 0.0)
    nparts = 3
    pieces = jnp.concatenate(_split_bf16(gate_e, nparts), axis=0)
    per = lax.dot_general(smat, pieces, (((1,), (1,)), ((), ())), preferred_element_type=F32)
    lo3 = jnp.concatenate([rgn_ref[0, 0:1, :]] * nparts, axis=1)
    hi3 = jnp.concatenate([rgn_ref[0, 1:2, :]] * nparts, axis=1)
    r_i3 = lax.broadcasted_iota(I32, (nrun, nparts * ne), 0)
    own3 = jnp.logical_and(r_i3 >= lo3, r_i3 < hi3)
    gcol = jnp.sum(jnp.where(own3, per, 0.0), axis=1, keepdims=True)
    return smat, gcol


def _window_copy(buf, slot, hbm, lo, g, sem, to_hbm, rows, align):
    src = buf.at[slot, pl.ds(pl.multiple_of(lo, align), rows)]
    dst = hbm.at[pl.ds(pl.multiple_of(g, align), rows)]
    if to_hbm:
        return pltpu.make_async_copy(src, dst, sem.at[slot])
    return pltpu.make_async_copy(dst, src, sem.at[slot])


def _start_windows(wg_ref, wl_ref, tw_ref, tile, slots, buf, slot, hbm, sem, to_hbm, rows, align):
    def per_window(w, c):
        k = tile * slots + w
        lo = w * rows if wl_ref is None else wl_ref[k]
        _window_copy(buf, slot, hbm, lo, wg_ref[k], sem, to_hbm, rows, align).start()
        return c

    lax.fori_loop(0, tw_ref[tile], per_window, 0)


def _wait_windows(count, buf, slot, hbm, sem, to_hbm, rows, align):
    batch = 8

    def many(w, c):
        _window_copy(buf, slot, hbm, 0, 0, sem, to_hbm, batch * rows, align).wait()
        return c

    def one(w, c):
        _window_copy(buf, slot, hbm, 0, 0, sem, to_hbm, rows, align).wait()
        return c

    lax.fori_loop(0, count // batch, many, 0)
    lax.fori_loop(0, count % batch, one, 0)


def _dispatch_kernel(wg_ref, wl_ref, mlo_ref, mtg_ref, keep_ref, tw_ref, nv_ref,
                     hf_ref, idx_ref, gate_ref, loff_ref, rgn_ref, xs_ref, buf, carry, sem):
    i = pl.program_id(0)
    nt = pl.num_programs(0)
    ne = loff_ref.shape[1]
    tt, d = hf_ref.shape
    nrun = buf.shape[1]
    slot = i % 2

    @pl.when(i == 0)
    def _():
        carry[...] = jnp.zeros(carry.shape, carry.dtype)

    smat, gcol = _run_onehot(idx_ref, loff_ref, rgn_ref, i, nv_ref[0], nrun, gate_ref)
    xr = jnp.dot(smat, hf_ref[...], preferred_element_type=F32)
    lane0 = lax.broadcasted_iota(I32, (nrun, GATE_COLS), 1) == 0
    buf[slot, :, 0:d // 2] = _pack_pairs(xr[:, :d // 2], xr[:, d // 2:])
    buf[slot, :, d // 2:] = pltpu.bitcast(jnp.where(lane0, gcol, 0.0), U32)

    def merge(e, c):
        k = i * ne + e
        lo = pl.multiple_of(mlo_ref[k], SUBLANE)
        buf[slot, pl.ds(lo, SUBLANE), :] = buf[slot, pl.ds(lo, SUBLANE), :] | carry[e]
        tg = pl.multiple_of(mtg_ref[k], SUBLANE)
        carry[e] = jnp.where(keep_ref[k] > 0, carry[e], buf[slot, pl.ds(tg, SUBLANE), :])
        return c

    lax.fori_loop(0, ne, merge, 0, unroll=4)

    @pl.when(i > 0)
    def _():
        _wait_windows(tw_ref[i - 1], buf, 1 - slot, xs_ref, sem, True, WIN, SUBLANE)

    _start_windows(wg_ref, wl_ref, tw_ref, i, MAX_WINDOWS, buf, slot, xs_ref, sem, True, WIN, SUBLANE)

    @pl.when(i == nt - 1)
    def _():
        _wait_windows(tw_ref[i], buf, slot, xs_ref, sem, True, WIN, SUBLANE)


def _expert_weight_copies(wg_hbm, wd_hbm, wg_f32, wd_f32, sem, e, slot):
    return (pltpu.make_async_copy(wg_hbm.at[e], wg_f32.at[slot], sem.at[0, slot]),
            pltpu.make_async_copy(wd_hbm.at[e], wd_f32.at[slot], sem.at[1, slot]))


def _moe_kernel(be_ref, bv_ref, first_ref, nxt_ref, slot_ref,
                x_ref, bg_ref, bd_ref, wg_hbm, wd_hbm, y_ref,
                wg_f32, wd_f32, wg_bf, wd_bf, sem):
    i = pl.program_id(0)
    e = be_ref[i]
    dff = wd_bf.shape[0]
    bm = x_ref.shape[0]
    sub = MOE_SUB
    nw = x_ref.shape[1] - GATE_COLS

    @pl.when(first_ref[i] > 0)
    def _():
        slot = slot_ref[i]

        @pl.when(first_ref[i] > 1)
        def _():
            for cp in _expert_weight_copies(wg_hbm, wd_hbm, wg_f32, wd_f32, sem, e, slot):
                cp.start()

        for cp in _expert_weight_copies(wg_hbm, wd_hbm, wg_f32, wd_f32, sem, e, slot):
            cp.wait()

        @pl.when(nxt_ref[i] >= 0)
        def _():
            for cp in _expert_weight_copies(wg_hbm, wd_hbm, wg_f32, wd_f32, sem, nxt_ref[i], 1 - slot):
                cp.start()

        wg_bf[...] = wg_f32[slot].astype(BF16)
        wd_bf[...] = wd_f32[slot].astype(BF16)

    def rows(r0, nrows):
        sizes = [MOE_CHAIN] * (nrows // MOE_CHAIN) + ([nrows % MOE_CHAIN] if nrows % MOE_CHAIN else [])
        lo = r0
        for size in sizes:
            _chain(lo, size)
            lo += size

    def _chain(lo, n):
        live = lax.broadcasted_iota(I32, (n, 1), 0) + lo < bv_ref[i]
        x = jnp.where(live, _unpack_pairs(x_ref[lo:lo + n, 0:nw]), jnp.zeros((), BF16))
        route = jnp.where(live, pltpu.bitcast(x_ref[lo:lo + n, nw:], F32)[:, 0:1], 0.0)
        ct = 2 * LANE
        hs = []
        for t in range(dff // ct):
            c0, c1 = t * ct, dff + t * ct
            gate = jnp.dot(x, wg_bf[:, c0:c0 + ct], preferred_element_type=F32) + bg_ref[0, :, c0:c0 + ct]
            up = jnp.dot(x, wg_bf[:, c1:c1 + ct], preferred_element_type=F32) + bg_ref[0, :, c1:c1 + ct]
            gate = jnp.minimum(gate, SWIGLU_LIMIT)
            up = jnp.clip(up, -SWIGLU_LIMIT, SWIGLU_LIMIT)
            hs.append((gate * jax.nn.sigmoid(SWIGLU_ALPHA * gate) * (up + 1.0)).astype(BF16))
        h = jnp.concatenate(hs, axis=1)
        half = y_ref.shape[1]
        for t in range(half // ct):
            c0, c1 = t * ct, half + t * ct
            ya = jnp.dot(h, wd_bf[:, c0:c0 + ct], preferred_element_type=F32) + bd_ref[0, :, c0:c0 + ct]
            yb = jnp.dot(h, wd_bf[:, c1:c1 + ct], preferred_element_type=F32) + bd_ref[0, :, c1:c1 + ct]
            y_ref[lo:lo + n, c0:c0 + ct] = _pack_pairs((route * ya).astype(BF16).astype(F32),
                                                       (route * yb).astype(BF16).astype(F32))

    nchains = bm // sub
    for live_chains in range(nchains + 1):
        lo_rows, hi_rows = (live_chains - 1) * sub, live_chains * sub

        @pl.when(jnp.logical_and(bv_ref[i] > lo_rows, bv_ref[i] <= hi_rows) if live_chains
                 else bv_ref[i] <= 0)
        def _(used=hi_rows):
            if used:
                rows(0, used)
            if used < bm:
                y_ref[used:bm, :] = jnp.zeros((bm - used, y_ref.shape[1]), y_ref.dtype)


def _combine_tile(step, nt):
    return (step + nt - 1) % nt


def _combine_kernel(wg_ref, tw_ref, nv_ref,
                    x1_ref, idx_ref, loff_ref, rgn_ref, nf_ref, yb_ref,
                    yp_ref, ys_ref, buf, sem):
    s = pl.program_id(0)
    nt = pl.num_programs(0)
    tt, d = x1_ref.shape
    nrun = buf.shape[1]
    tile = _combine_tile(s, nt)
    slot = s % 2

    @pl.when(s == 0)
    def _():
        buf[...] = jnp.zeros(buf.shape, buf.dtype)
        _start_windows(wg_ref, None, tw_ref, tile, MAX_WINDOWS, buf, slot, yb_ref, sem, False,
                       WIN, SUBLANE)

    @pl.when(s + 1 < nt)
    def _():
        _start_windows(wg_ref, None, tw_ref, _combine_tile(s + 1, nt), MAX_WINDOWS, buf, 1 - slot,
                       yb_ref, sem, False, WIN, SUBLANE)

    smat = _run_onehot(idx_ref, loff_ref, rgn_ref, tile, nv_ref[0], nrun)

    _wait_windows(tw_ref[tile], buf, slot, yb_ref, sem, False, WIN, SUBLANE)
    yrun = _unpack_pairs(buf[slot])
    moe = lax.dot_general(smat, yrun, (((0,), (0,)), ((), ())), preferred_element_type=F32)
    out = _rms(x1_ref[...] + moe, nf_ref[...])

    @pl.when(s == 0)
    def _():
        ys_ref[...] = out[0:ys_ref.shape[0], :]

    @pl.when(s > 0)
    def _():
        yp_ref[...] = out.reshape(yp_ref.shape)


def _ssm_matrices(a_re, a_im, log_dt, b_re, b_im, c_re, c_im):
    g, n = a_re.shape
    hch = b_re.shape[2]
    gpt = LANE // hch
    nj = g // gpt
    a = lax.complex(a_re, a_im)
    dta = a * jnp.exp(log_dt)[:, None]
    a_bar = jnp.exp(dta)
    bb = ((a_bar - 1.0) / a)[:, :, None] * lax.complex(b_re, b_im)
    cc = lax.complex(c_re, c_im)
    ks = jnp.arange(CHUNK + 1, dtype=F32)
    pw = jnp.exp(dta[None] * ks[:, None, None])
    kk = jnp.real(jnp.einsum('gon,kgn,gni->kgio', cc, pw[:CHUNK], bb))
    kk = jnp.concatenate([jnp.zeros_like(kk[:1]), kk], axis=0)
    kc = kk.reshape(CHUNK + 1, nj, gpt * hch, hch).transpose(1, 0, 2, 3)
    kc = kc.reshape(nj, (CHUNK + 1) * LANE, hch).astype(BF16)
    pwr, pwi = jnp.real(pw), jnp.imag(pw)
    bbr = jnp.real(bb).transpose(0, 2, 1).reshape(1, g * hch, n)
    bbi = jnp.imag(bb).transpose(0, 2, 1).reshape(1, g * hch, n)
    par = jnp.repeat(pwr[CHUNK - 1::-1][:CHUNK], hch, axis=1)
    pai = jnp.repeat(pwi[CHUNK - 1::-1][:CHUNK], hch, axis=1)
    pc = jnp.concatenate([par * bbr - pai * bbi, par * bbi + pai * bbr], axis=-1)
    pc = pc.reshape(CHUNK, nj, gpt * hch, 2 * n).transpose(1, 0, 2, 3)
    pc = pc.reshape(nj, CHUNK * LANE, 2 * n).astype(BF16)
    ccr = jnp.real(cc).transpose(2, 0, 1).reshape(n, 1, g * hch)
    cci = jnp.imag(cc).transpose(2, 0, 1).reshape(n, 1, g * hch)
    qar = jnp.repeat(pwr[1:CHUNK + 1].transpose(2, 0, 1), hch, axis=2)
    qai = jnp.repeat(pwi[1:CHUNK + 1].transpose(2, 0, 1), hch, axis=2)
    rc = jnp.stack([ccr * qar - cci * qai, -(ccr * qai + cci * qar)], axis=0)
    rc = rc.reshape(2 * n, CHUNK, nj, gpt * hch).transpose(2, 0, 1, 3)
    rc = rc.reshape(nj, 2 * n, CHUNK * LANE).astype(BF16)
    a16 = pw[CHUNK].reshape(nj, 1, gpt * n)
    a16 = jnp.concatenate([jnp.real(a16), jnp.imag(a16)], axis=1)
    bc = jnp.stack([jnp.real(bb), jnp.imag(bb)], axis=0).transpose(1, 3, 0, 2)
    bc = bc.reshape(g * hch, 2 * n).astype(BF16)
    c2 = jnp.stack([jnp.real(cc), -jnp.imag(cc)], axis=0).transpose(0, 3, 1, 2)
    c2 = c2.reshape(2 * n, g * hch).astype(BF16)
    abr = jnp.real(a_bar).reshape(1, g * n)
    abi = jnp.imag(a_bar).reshape(1, g * n)
    return kc, pc, rc, a16, bc, c2, abr, abi


def _full(shape):
    return pl.BlockSpec(shape, lambda *_: (0,) * len(shape))


def kernel(x_prompt, x_sample, state_ssm_re, state_ssm_im, state_conv, meta_tokens, norm_mix, w_in,
           ssm_a_re, ssm_a_im, ssm_log_dt, ssm_b_re, ssm_b_im, ssm_c_re, ssm_c_im, ssm_d, w_glu, b_glu,
           conv_w, norm_out_ssm, norm_out_conv, w_out, norm_ffn, w_router, b_router, w_gate_up,
           b_gate_up, w_down, b_down, norm_final):
    nb, seq, d = x_prompt.shape
    ns = x_sample.shape[0]
    depth, _, g, n = state_ssm_re.shape
    assert depth == 1 and x_sample.shape[1] == 1 and meta_tokens.shape[0] == CHUNK
    cw = conv_w.shape[2]
    nj = cw // LANE
    ne = w_router.shape[2]
    dff = w_down.shape[2]
    nst = g * n
    nbt = nb // 2
    tt = 256
    n_chunks = seq // CHUNK
    tp = nb * seq
    tall = tp + ns
    tm = 128
    nbm = nb
    rows_p = nbm * tm
    assert rows_p % TOK_TILE == 0 and TOK_TILE % tm == 0 and ns <= TOK_TILE and d % 2 == 0
    n_tiles = tp // TOK_TILE + 1
    ta = n_tiles * TOK_TILE

    kc, pc, rc, a16, bc, c2, abr, abi = _ssm_matrices(
        ssm_a_re[0], ssm_a_im[0], ssm_log_dt[0], ssm_b_re[0], ssm_b_im[0], ssm_c_re[0], ssm_c_im[0])
    win_bf = w_in[0].astype(BF16)
    nmix = norm_mix[0].reshape(1, d)
    nconv = norm_out_conv[0].reshape(1, cw)
    cwt = conv_w[0]

    xsm = jnp.concatenate([x_sample.reshape(ns, d), meta_tokens], axis=0)
    nsm = ns + CHUNK
    s0r = state_ssm_re[0].reshape(ns, nst)
    s0i = state_ssm_im[0].reshape(ns, nst)
    buf0 = state_conv[0, :, 0, :]
    buf1 = state_conv[0, :, 1, :]
    u_sm, z_sm, y_s, ycn_s, sr_s, si_s = pl.pallas_call(
        _small_front_kernel,
        out_shape=(jax.ShapeDtypeStruct((nsm, cw), F32), jax.ShapeDtypeStruct((nsm, cw), F32),
                   jax.ShapeDtypeStruct((ns, cw), F32), jax.ShapeDtypeStruct((ns, cw), BF16),
                   jax.ShapeDtypeStruct((ns, nst), F32), jax.ShapeDtypeStruct((ns, nst), F32)),
        scratch_shapes=[pltpu.VMEM((cw, 2 * nst), BF16), pltpu.VMEM((2 * nst, cw), BF16)],
        compiler_params=_params(None, 56),
        name="small_front",
    )(xsm, nmix, win_bf, s0r, s0i, buf0, buf1, cwt, bc, c2, abr, abi, nconv)
    u_meta = u_sm[ns:]
    z_meta8 = z_sm[ns + CHUNK - 8:]
    new_conv_s = jnp.stack([buf1, z_sm[:ns]], axis=1)[None]
    new_re_s = sr_s.reshape(1, ns, g, n)
    new_im_s = si_s.reshape(1, ns, g, n)

    wch = CHUNK * LANE
    u4c, u_tok, ycn_p, ztail = pl.pallas_call(
        _front_kernel,
        grid=(nb // nbt, seq // tt),
        in_specs=[pl.BlockSpec((nbt, tt, d), lambda b, i: (b, i, 0)),
                  _full((1, d)), _full((d, 4 * cw)), _full((8, cw)), _full((3, cw)), _full((1, cw))],
        out_specs=(pl.BlockSpec((nj, nbt, tt // CHUNK, wch), lambda b, i: (0, b, i, 0)),
                   pl.BlockSpec((nbt, tt, cw), lambda b, i: (b, i, 0)),
                   pl.BlockSpec((nbt, tt, cw), lambda b, i: (b, i, 0)),
                   pl.BlockSpec((nbt, 8, cw), lambda b, i: (b, 0, 0))),
        out_shape=(jax.ShapeDtypeStruct((nj, nb, n_chunks, wch), BF16),
                   jax.ShapeDtypeStruct((nb, seq, cw), BF16),
                   jax.ShapeDtypeStruct((nb, seq, cw), BF16),
                   jax.ShapeDtypeStruct((nb, 8, cw), F32)),
        scratch_shapes=[pltpu.VMEM((nbt, tt + 8, cw), F32), pltpu.VMEM((nj, nbt, tt, LANE), F32)],
        compiler_params=_params(("arbitrary", "arbitrary"), 52),
        name="front",
    )(x_prompt, nmix, win_bf, z_meta8, cwt, nconv)
    new_conv_p = ztail[:, 6:8, :][None]

    cc = n_chunks // 2
    um = u_meta.reshape(CHUNK, nj, LANE).transpose(1, 0, 2).reshape(nj, 1, wch)
    um = jnp.broadcast_to(um, (nj, 8 * nb, wch)).astype(BF16)
    gpt = g // nj
    nstj = 2 * gpt * n
    hch = cw // g
    y4c, s_last = pl.pallas_call(
        _ssm_kernel,
        grid=(nj, n_chunks // cc),
        in_specs=[pl.BlockSpec((1, nb, cc, wch), lambda j, t: (j, 0, t, 0)),
                  pl.BlockSpec((1, 8 * nb, wch), lambda j, t: (j, 0, 0)),
                  pl.BlockSpec((1, (CHUNK + 1) * LANE, hch), lambda j, t: (j, 0, 0)),
                  pl.BlockSpec((1, wch, 2 * n), lambda j, t: (j, 0, 0)),
                  pl.BlockSpec((1, 2 * n, wch), lambda j, t: (j, 0, 0)),
                  pl.BlockSpec((1, 2, nstj // 2), lambda j, t: (j, 0, 0))],
        out_specs=(pl.BlockSpec((1, nb, cc, wch), lambda j, t: (j, 0, t, 0)),
                   pl.BlockSpec((1, nb, 1, nstj), lambda j, t: (j, 0, 0, 0))),
        out_shape=(jax.ShapeDtypeStruct((nj, nb, n_chunks, wch), F32),
                   jax.ShapeDtypeStruct((nj, nb, 1, nstj), F32)),
        scratch_shapes=[pltpu.VMEM((nb, 1, nstj), F32), pltpu.VMEM((nb, cc, nstj), F32),
                        pltpu.VMEM((nb, cc, nstj), F32),
                        pltpu.VMEM((wch, nstj), BF16), pltpu.VMEM((nstj, wch), BF16),
                        pltpu.VMEM((CHUNK // 2, 2 * LANE, 2 * LANE), BF16)],
        compiler_params=_params(("parallel", "arbitrary"), 56),
        name="ssm",
    )(u4c, um, kc, pc, rc, a16)
    sl = s_last.reshape(nj, nb, 2, gpt, n)
    new_re_p = sl[:, :, 0].transpose(1, 0, 2, 3).reshape(1, nb, g, n)
    new_im_p = sl[:, :, 1].transpose(1, 0, 2, 3).reshape(1, nb, g, n)

    dsk = ssm_d[0].reshape(1, cw)
    wglu_bf = w_glu[0].astype(BF16)
    bglu = b_glu[0].reshape(1, cw)
    nssm = norm_out_ssm[0].reshape(1, cw)
    wout_bf = w_out[0].astype(BF16)
    nffn = norm_ffn[0].reshape(1, d)
    wr_pad = jnp.zeros((d, LANE), F32).at[:, :ne].set(w_router[0])
    wr_hi = wr_pad.astype(BF16)
    wr_lo = (wr_pad - wr_hi.astype(F32)).astype(BF16)
    br = b_router[0].reshape(ne, 1)
    wr2 = jnp.concatenate([wr_hi, wr_lo], axis=1)
    mix_w = (dsk, wglu_bf, bglu, nssm, wout_bf, nffn, wr2, br)
    mix_w_specs = [_full((1, cw)), _full((cw, cw)), _full((1, cw)), _full((1, cw)), _full((2 * cw, d)),
                   _full((1, d)), _full((d, 2 * LANE)), _full((ne, 1))]
    assert 4 + len(mix_w) == N_MIX_IN
    mix_out_shape = (jax.ShapeDtypeStruct((ta, d), F32), jax.ShapeDtypeStruct((ta, d), BF16),
                     jax.ShapeDtypeStruct((TOP_K, ta), I32), jax.ShapeDtypeStruct((TOP_K, ta), F32),
                     jax.ShapeDtypeStruct((n_tiles, ne, 1), I32))
    tpm = rows_p // TOK_TILE
    nbg = nb // nbm
    x1_all, hf_all, idx_all, gate_all, cnt = pl.pallas_call(
        _mix_kernel_prompt,
        grid=(seq // tm, nbg),
        in_specs=[pl.BlockSpec((nbm, tm, d), lambda i, b: (b, i, 0)),
                  pl.BlockSpec((nj, nbm, tm // CHUNK, wch), lambda i, b: (0, b, i, 0)),
                  pl.BlockSpec((nbm, tm, cw), lambda i, b: (b, i, 0)),
                  pl.BlockSpec((nbm, tm, cw), lambda i, b: (b, i, 0))] + mix_w_specs,
        out_specs=(pl.BlockSpec((rows_p, d), lambda i, b: (i * nbg + b, 0)),
                   pl.BlockSpec((rows_p, d), lambda i, b: (i * nbg + b, 0)),
                   pl.BlockSpec((TOP_K, rows_p), lambda i, b: (0, i * nbg + b)),
                   pl.BlockSpec((TOP_K, rows_p), lambda i, b: (0, i * nbg + b)),
                   pl.BlockSpec((tpm, ne, 1), lambda i, b: (i * nbg + b, 0, 0))),
        out_shape=mix_out_shape,
        scratch_shapes=[pltpu.VMEM((nj, nbm, tm, LANE), F32)],
        compiler_params=_params(("parallel", "parallel"), 56),
        name="mix_prompt",
    )(x_prompt, y4c, u_tok, ycn_p, *mix_w)

    last = n_tiles - 1
    any_spec = pl.BlockSpec(memory_space=pl.ANY)
    x1_all, hf_all, idx_all, gate_all, cnt = pl.pallas_call(
        _mix_kernel_sample,
        grid=(1,),
        in_specs=[_full((1, ns, d)), _full((ns, cw)), _full((1, ns, cw)),
                  _full((1, ns, cw))] + mix_w_specs + [any_spec] * 5,
        out_specs=(pl.BlockSpec((TOK_TILE, d), lambda i: (last, 0)),
                   pl.BlockSpec((TOK_TILE, d), lambda i: (last, 0)),
                   pl.BlockSpec((TOP_K, TOK_TILE), lambda i: (0, last)),
                   pl.BlockSpec((TOP_K, TOK_TILE), lambda i: (0, last)),
                   pl.BlockSpec((1, ne, 1), lambda i: (last, 0, 0))),
        out_shape=mix_out_shape,
        input_output_aliases={N_MIX_IN + k: k for k in range(5)},
        compiler_params=_params(("arbitrary",), 32),
        name="mix_sample",
    )(x_sample.reshape(1, ns, d), y_s, u_sm[:ns].astype(BF16).reshape(1, ns, cw),
      ycn_s.reshape(1, ns, cw), *mix_w, x1_all, hf_all, idx_all, gate_all, cnt)

    bm = MOE_ROWS
    cnt2 = cnt.reshape(n_tiles, ne)
    before = jnp.cumsum(cnt2, axis=0) - cnt2
    count = jnp.sum(cnt2, axis=0)
    padded = ((count + WIN + bm - 1) // bm) * bm
    pend = jnp.cumsum(padded)
    pstart = pend - padded
    phase = before % SUBLANE
    span = jnp.where(cnt2 > 0, phase + cnt2, 0)
    gstart = (pstart[None, :] + before - phase).astype(I32).reshape(-1)
    nwin = ((span + WIN - 1) // WIN).astype(I32)
    reg8 = ((span + SUBLANE - 1) // SUBLANE) * SUBLANE
    loff = (jnp.cumsum(reg8, axis=1) - reg8).astype(I32)
    tail = jnp.where(span % SUBLANE != 0, loff + (span // SUBLANE) * SUBLANE, -1).astype(I32)
    twin = jnp.sum(nwin, axis=1).astype(I32)
    n_blocks = (ta * TOP_K + ne * (WIN + bm - 1) + bm - 1) // bm
    cap = n_blocks * bm
    blk0 = jnp.arange(n_blocks, dtype=I32) * bm
    blk_e = jnp.minimum(jnp.sum((pend[None, :] <= blk0[:, None]).astype(I32), axis=1), ne - 1)
    e_ar = jnp.arange(ne, dtype=I32)
    blk_hot = blk_e[:, None] == e_ar[None, :]

    def _of_block(per_expert):
        return jnp.sum(jnp.where(blk_hot, per_expert[None, :], 0), axis=1)

    blk_valid = jnp.clip(_of_block(count) - (blk0 - _of_block(pstart)), 0, bm).astype(I32)
    has = count > 0
    later = jnp.logical_and(e_ar[None, :] > e_ar[:, None], has[None, :])
    nxt_e = jnp.min(jnp.where(later, e_ar[None, :], ne), axis=1)
    nxt_e = jnp.where(nxt_e < ne, nxt_e, -1)
    ordinal = jnp.cumsum(has.astype(I32)) - 1
    is_first = jnp.logical_and(blk_valid > 0, blk0 == _of_block(pstart))
    blk_first = jnp.where(is_first, jnp.where(_of_block(ordinal) == 0, 2, 1), 0).astype(I32)
    blk_next = _of_block(nxt_e).astype(I32)
    blk_slot = (_of_block(ordinal) % 2).astype(I32)
    nvalid = jnp.full((1,), tall, I32)
    loff_al = (WIN * (jnp.cumsum(nwin, axis=1) - nwin)).astype(I32)

    def _window_list(nw, slots, rows, first_hbm, first_buf):
        wcum = jnp.cumsum(nw, axis=1)
        wslot = jnp.arange(slots, dtype=I32)
        w_hot = jnp.logical_and(wslot[None, :, None] >= (wcum - nw)[:, None, :],
                                wslot[None, :, None] < wcum[:, None, :])

        def _of_window(per_tile_expert):
            return jnp.sum(jnp.where(w_hot, per_tile_expert[:, None, :], 0), axis=2)

        w_in_run = wslot[None, :] - _of_window(wcum - nw)
        return [(_of_window(f) + rows * w_in_run).astype(I32).reshape(-1) for f in (first_hbm, first_buf)]

    w_hbm, w_buf = _window_list(nwin, MAX_WINDOWS, WIN, gstart.reshape(n_tiles, ne), loff)
    nrun_d = _dispatch_run_rows(ne)
    zero_grp, spare_grp = nrun_d - 2 * SUBLANE, nrun_d - SUBLANE
    m_lo = jnp.where(nwin > 0, loff, spare_grp).astype(I32).reshape(-1)
    m_tg = jnp.where(jnp.logical_and(nwin > 0, tail >= 0), tail, zero_grp).astype(I32).reshape(-1)
    m_keep = (nwin == 0).astype(I32).reshape(-1)
    tables_d = (w_hbm, w_buf, m_lo, m_tg, m_keep, twin, nvalid)
    tables_c = (w_hbm, twin, nvalid)
    rowoff_d = (loff + phase).astype(I32).reshape(n_tiles, ne, 1)
    rowoff_c = (loff_al + phase).astype(I32).reshape(n_tiles, ne, 1)
    rgn_d = jnp.stack([loff, loff + reg8], axis=1).astype(I32)
    rgn_c = jnp.stack([loff_al, loff_al + WIN * nwin], axis=1).astype(I32)
    xw = d // 2 + GATE_COLS

    xs = pl.pallas_call(
        _dispatch_kernel,
        grid_spec=pltpu.PrefetchScalarGridSpec(
            num_scalar_prefetch=len(tables_d),
            grid=(n_tiles,),
            in_specs=[pl.BlockSpec((TOK_TILE, d), lambda i, *_: (i, 0)),
                      pl.BlockSpec((TOP_K, TOK_TILE), lambda i, *_: (0, i)),
                      pl.BlockSpec((TOP_K, TOK_TILE), lambda i, *_: (0, i)),
                      pl.BlockSpec((1, ne, 1), lambda i, *_: (i, 0, 0)),
                      pl.BlockSpec((1, 2, ne), lambda i, *_: (i, 0, 0))],
            out_specs=pl.BlockSpec(memory_space=pl.ANY),
            scratch_shapes=[pltpu.VMEM((2, _dispatch_run_rows(ne), xw), U32),
                            pltpu.VMEM((ne, SUBLANE, xw), U32), pltpu.SemaphoreType.DMA((2,))]),
        out_shape=jax.ShapeDtypeStruct((cap, xw), U32),
        compiler_params=_params(("arbitrary",), 40),
        name="dispatch",
    )(*tables_d, hf_all, idx_all, gate_all, rowoff_d, rgn_d)

    yb = pl.pallas_call(
        _moe_kernel,
        grid_spec=pltpu.PrefetchScalarGridSpec(
            num_scalar_prefetch=5,
            grid=(n_blocks,),
            in_specs=[pl.BlockSpec((bm, xw), lambda i, *_: (i, 0)),
                      pl.BlockSpec((1, 1, 2 * dff), lambda i, be, *_: (be[i], 0, 0)),
                      pl.BlockSpec((1, 1, d), lambda i, be, *_: (be[i], 0, 0)),
                      pl.BlockSpec(memory_space=pl.ANY), pl.BlockSpec(memory_space=pl.ANY)],
            out_specs=pl.BlockSpec((bm, d // 2), lambda i, *_: (i, 0)),
            scratch_shapes=[pltpu.VMEM((2, d, 2 * dff), F32), pltpu.VMEM((2, dff, d), F32),
                            pltpu.VMEM((d, 2 * dff), BF16), pltpu.VMEM((dff, d), BF16),
                            pltpu.SemaphoreType.DMA((2, 2))]),
        out_shape=jax.ShapeDtypeStruct((cap, d // 2), U32),
        compiler_params=_params(("arbitrary",), 58),
        name="moe",
    )(blk_e, blk_valid, blk_first, blk_next, blk_slot, xs, b_gate_up[0].reshape(ne, 1, 2 * dff),
      b_down[0].reshape(ne, 1, d), w_gate_up[0], w_down[0])

    nfin = norm_final.reshape(1, d)
    nbh = TOK_TILE // tm

    def _tile_of(s):
        return (s + n_tiles - 1) % n_tiles

    tiles_per_time = nbg * tpm

    def _yp_index(s, *_):
        t = jnp.maximum(s - 1, 0)
        return (t % tiles_per_time, t // tiles_per_time, 0)

    y_p, y_sm = pl.pallas_call(
        _combine_kernel,
        grid_spec=pltpu.PrefetchScalarGridSpec(
            num_scalar_prefetch=len(tables_c),
            grid=(n_tiles,),
            in_specs=[pl.BlockSpec((TOK_TILE, d), lambda s, *_: (_tile_of(s), 0)),
                      pl.BlockSpec((TOP_K, TOK_TILE), lambda s, *_: (0, _tile_of(s))),
                      pl.BlockSpec((1, ne, 1), lambda s, *_: (_tile_of(s), 0, 0)),
                      pl.BlockSpec((1, 2, ne), lambda s, *_: (_tile_of(s), 0, 0)),
                      pl.BlockSpec((1, d), lambda s, *_: (0, 0)),
                      pl.BlockSpec(memory_space=pl.ANY)],
            out_specs=(pl.BlockSpec((nbh, tm, d), _yp_index),
                       pl.BlockSpec((ns, d), lambda s, *_: (0, 0))),
            scratch_shapes=[pltpu.VMEM((2, _combine_run_rows(ne), d // 2), U32),
                            pltpu.SemaphoreType.DMA((2,))]),
        out_shape=(jax.ShapeDtypeStruct((nb, seq, d), F32), jax.ShapeDtypeStruct((ns, d), F32)),
        compiler_params=_params(("arbitrary",), 48),
        name="combine",
    )(*tables_c, x1_all, idx_all, rowoff_c, rgn_c, nfin, yb)

    return (y_p, y_sm.reshape(ns, 1, d), new_re_p, new_im_p, new_conv_p,
            new_re_s, new_im_s, new_conv_s)
```

```python
import functools
import math

import jax
import jax.numpy as jnp
from jax import lax
from jax.experimental import pallas as pl
from jax.experimental.pallas import tpu as pltpu

F32 = jnp.float32
BF16 = jnp.bfloat16
U32 = jnp.uint32
I32 = jnp.int32
EPS = 1e-5
CHUNK = 16
LANE = 128
TOP_K = 4
SWIGLU_LIMIT = 7.0
SWIGLU_ALPHA = 1.702
MOE_ROWS = 1024
MOE_SUB = 256
MOE_CHAIN = 512
TOK_TILE = 256
WIN = 32
SUBLANE = 8
GATE_COLS = LANE


MAX_WINDOWS = 72


def _dispatch_run_rows(ne):
    return TOP_K * TOK_TILE + ne * 2 * (SUBLANE - 1) + WIN


def _combine_run_rows(ne):
    return -(-(TOP_K * TOK_TILE + ne * (SUBLANE - 1 + WIN - 1)) // WIN) * WIN


HI_MASK = 0xFFFF0000
BF16_EXACT = 256.0
MIB = 1024 * 1024
VMEM_MIB = {"small_front": 56, "front": 52, "ssm": 56, "mix_prompt": 56, "mix_sample": 32,
            "dispatch": 40, "moe": 58, "combine": 48}


def _rms(x, g):
    return x * lax.rsqrt(jnp.mean(x * x, axis=-1, keepdims=True) + EPS) * g


def _gelu_tanh(x):
    c = math.sqrt(2.0 / math.pi)
    return 0.5 * x * (1.0 + jnp.tanh(c * (x + 0.044715 * (x * x * x))))


def _params(name, sem):
    return dict(name=name, compiler_params=pltpu.CompilerParams(
        dimension_semantics=sem, vmem_limit_bytes=VMEM_MIB[name] * MIB))


def _pack_pairs(a, b):
    return (pltpu.bitcast(a, U32) >> 16) | (pltpu.bitcast(b, U32) & jnp.uint32(HI_MASK))


def _unpack_pairs(w):
    lo = pltpu.bitcast(w << 16, F32)
    hi = pltpu.bitcast(w & jnp.uint32(HI_MASK), F32)
    return jnp.concatenate([lo, hi], axis=-1).astype(BF16)


def _iota2(shape, axis):
    return lax.broadcasted_iota(I32, shape, axis)


def _expand_cols(compact, reps_log2, n_log2):
    q = _iota2((compact.shape[1], 2 << (reps_log2 + n_log2)), 0)
    c = _iota2((compact.shape[1], 2 << (reps_log2 + n_log2)), 1)
    nmask = (1 << n_log2) - 1
    same = jnp.logical_and((q >> n_log2) == (c >> (reps_log2 + n_log2)), (q & nmask) == (c & nmask))
    return jnp.dot(compact, jnp.where(same, 1.0, 0.0).astype(BF16), preferred_element_type=F32)


def _expand_rows(compact, reps_log2, n_log2):
    r = _iota2((2 << (reps_log2 + n_log2), compact.shape[0]), 0)
    q = _iota2((2 << (reps_log2 + n_log2), compact.shape[0]), 1)
    nmask = (1 << n_log2) - 1
    same = jnp.logical_and((r >> (reps_log2 + n_log2)) == (q >> n_log2), (r & nmask) == (q & nmask))
    return jnp.dot(jnp.where(same, 1.0, 0.0).astype(BF16), compact, preferred_element_type=F32)


def _group_mask(shape, row_shift, col_shift, ngroups):
    r = _iota2(shape, 0)
    c = _iota2(shape, 1)
    return ((r >> row_shift) & (ngroups - 1)) == ((c >> col_shift) & (ngroups - 1))


def _small_front_kernel(x_ref, nmix_ref, win_ref, s0r_ref, s0i_ref, b0_ref, b1_ref, cw_ref,
                        bc_ref, cc_ref, abr_ref, abi_ref, nconv_ref,
                        u_ref, z_ref, y_ref, ycn_ref, sr_ref, si_ref, bdb_ref, cm_ref):
    ns, nst = s0r_ref.shape
    cw = u_ref.shape[1]
    n = bc_ref.shape[1] // 2
    nlog = n.bit_length() - 1
    glog = (nst // n).bit_length() - 1
    hlog = (cw >> glog).bit_length() - 1
    bdb_ref[...] = jnp.where(_group_mask(bdb_ref.shape, hlog, nlog, 1 << glog),
                             _expand_cols(bc_ref[...], glog, nlog), 0.0).astype(BF16)
    cm_ref[...] = jnp.where(_group_mask(cm_ref.shape, nlog, hlog, 1 << glog),
                            _expand_rows(cc_ref[...], glog, nlog), 0.0).astype(BF16)
    h = _rms(x_ref[...], nmix_ref[...]).astype(BF16)
    proj = jnp.dot(h, win_ref[...], preferred_element_type=F32)
    u = proj[:, 0:cw]
    zc = proj[:, cw:2 * cw]
    gb = proj[:, 2 * cw:3 * cw]
    gc = proj[:, 3 * cw:4 * cw]
    z = gc * zc
    u_ref[...] = u
    z_ref[...] = z
    bu = jnp.dot(u[:ns].astype(BF16), bdb_ref[...], preferred_element_type=F32)
    abr = abr_ref[...]
    abi = abi_ref[...]
    s0r = s0r_ref[...]
    s0i = s0i_ref[...]
    sr = abr * s0r - abi * s0i + bu[:, :nst]
    si = abr * s0i + abi * s0r + bu[:, nst:]
    sr_ref[...] = sr
    si_ref[...] = si
    scat = jnp.concatenate([sr, si], axis=-1).astype(BF16)
    y_ref[...] = jnp.dot(scat, cm_ref[...], preferred_element_type=F32)
    conv = cw_ref[0:1, :] * b0_ref[...] + cw_ref[1:2, :] * b1_ref[...] + cw_ref[2:3, :] * z[:ns]
    ycn_ref[...] = _rms(gb[:ns] * conv, nconv_ref[...]).astype(BF16)


def _front_kernel(x_ref, nmix_ref, win_ref, zm_ref, cw_ref, nconv_ref,
                  uc_ref, ut_ref, ycn_ref, zt_ref, zbuf, ubuf):
    i = pl.program_id(1)
    nb, tt, d = x_ref.shape
    cw = ycn_ref.shape[2]
    rows = nb * tt
    ncz = tt // CHUNK
    halo = SUBLANE

    @pl.when(i == 0)
    def _():
        zbuf[:, 0:halo, :] = jnp.broadcast_to(zm_ref[...][None], (nb, halo, cw))

    h = _rms(x_ref[...].reshape(rows, d), nmix_ref[...]).astype(BF16)
    u = jnp.dot(h, win_ref[:, 0:cw], preferred_element_type=F32)
    ut_ref[...] = u.astype(BF16).reshape(nb, tt, cw)
    for j in range(cw // LANE):
        ubuf[j] = u[:, j * LANE:(j + 1) * LANE].reshape(nb, tt, LANE)
    for s in range(CHUNK):
        for j in range(cw // LANE):
            piece = ubuf[j, :, pl.ds(s, ncz, stride=CHUNK), :]
            uc_ref[j, :, :, s * LANE:(s + 1) * LANE] = piece.astype(BF16)
    zc = jnp.dot(h, win_ref[:, cw:2 * cw], preferred_element_type=F32)
    gc = jnp.dot(h, win_ref[:, 3 * cw:4 * cw], preferred_element_type=F32)
    z3 = (gc * zc).reshape(nb, tt, cw)
    zbuf[:, halo:halo + tt, :] = z3
    z1 = zbuf[:, halo - 1:halo - 1 + tt, :]
    z2 = zbuf[:, halo - 2:halo - 2 + tt, :]
    conv = cw_ref[0:1, :] * z2 + cw_ref[1:2, :] * z1 + cw_ref[2:3, :] * z3
    gb = jnp.dot(h, win_ref[:, 2 * cw:3 * cw], preferred_element_type=F32)
    yc = gb * conv.reshape(rows, cw)
    ycn_ref[...] = _rms(yc, nconv_ref[...]).astype(BF16).reshape(nb, tt, cw)
    tail = zbuf[:, tt:tt + halo, :]
    zt_ref[...] = tail
    zbuf[:, 0:halo, :] = tail


def _ssm_kernel(u_ref, um_ref, kc_ref, pc_ref, rc_ref, a16_ref, y_ref, sl_ref,
                s_carry, ds_ref, sp_ref, p_s, r_s, t_s):
    th = pl.program_id(1)
    _, nb, cc, w = u_ref.shape
    nst = p_s.shape[1]
    half = nst // 2
    rows = nb * cc
    blk = 2 * LANE
    u = u_ref[0].reshape(rows, w)

    @pl.when(th == 0)
    def _():
        hch = kc_ref.shape[2]
        gpt = LANE // hch
        hlog = hch.bit_length() - 1
        glog = gpt.bit_length() - 1
        nlog = (pc_ref.shape[2] // 2).bit_length() - 1
        p_s[...] = jnp.where(_group_mask(p_s.shape, hlog, nlog, gpt),
                             _expand_cols(pc_ref[0], glog, nlog), 0.0).astype(BF16)
        r_s[...] = jnp.where(_group_mask(r_s.shape, nlog, hlog, gpt),
                             _expand_rows(rc_ref[0], glog, nlog), 0.0).astype(BF16)
        nlag = kc_ref.shape[1] // LANE
        o = _iota2((hch, LANE), 0)
        c = _iota2((hch, LANE), 1)
        spread = jnp.where((c & (hch - 1)) == o, 1.0, 0.0).astype(BF16)
        lagm = jnp.dot(kc_ref[0], spread, preferred_element_type=F32)
        r = _iota2(lagm.shape, 0)
        c = _iota2(lagm.shape, 1)
        lagm = jnp.where(((r >> hlog) & (gpt - 1)) == (c >> hlog), lagm, 0.0).astype(BF16)
        for dlt in range(nlag // 2):
            b0 = lagm[(2 * dlt) * LANE:(2 * dlt + 1) * LANE]
            b1 = lagm[(2 * dlt + 1) * LANE:(2 * dlt + 2) * LANE]
            b2 = lagm[(2 * dlt + 2) * LANE:(2 * dlt + 3) * LANE]
            t_s[dlt, 0:LANE, 0:LANE] = b1
            t_s[dlt, 0:LANE, LANE:blk] = b2
            t_s[dlt, LANE:blk, 0:LANE] = b0
            t_s[dlt, LANE:blk, LANE:blk] = b1
        ds_ref[:, 0:8, :] = jnp.dot(um_ref[0], p_s[...], preferred_element_type=F32).reshape(nb, 8, nst)
        s_carry[...] = ds_ref[:, 0:1, :]

    ds_ref[...] = jnp.dot(u, p_s[...], preferred_element_type=F32).reshape(nb, cc, nst)
    for tb in range(w // blk):
        acc = jnp.dot(u[:, 0:blk], t_s[tb], preferred_element_type=F32)
        for sb in range(1, tb + 1):
            acc = acc + jnp.dot(u[:, sb * blk:(sb + 1) * blk], t_s[tb - sb],
                                preferred_element_type=F32)
        y_ref[0, :, :, tb * blk:(tb + 1) * blk] = acc.reshape(nb, cc, blk)
    ar = a16_ref[0, 0:1, :].reshape(1, 1, half)
    ai = a16_ref[0, 1:2, :].reshape(1, 1, half)
    sr = s_carry[:, :, 0:half]
    si = s_carry[:, :, half:nst]
    for c in range(cc):
        sp_ref[:, c:c + 1, 0:half] = sr
        sp_ref[:, c:c + 1, half:nst] = si
        dr = ds_ref[:, c:c + 1, 0:half]
        di = ds_ref[:, c:c + 1, half:nst]
        sr, si = ar * sr - ai * si + dr, ar * si + ai * sr + di
    s_carry[:, :, 0:half] = sr
    s_carry[:, :, half:nst] = si
    sl_ref[0, :, :, 0:half] = sr
    sl_ref[0, :, :, half:nst] = si

    sp = sp_ref[...].reshape(rows, nst).astype(BF16)
    for tb in range(w // blk):
        acc = jnp.dot(sp, r_s[:, tb * blk:(tb + 1) * blk], preferred_element_type=F32)
        y_ref[0, :, :, tb * blk:(tb + 1) * blk] += acc.reshape(nb, cc, blk)


N_MIX_IN = 12


def _mix_rows(x, yssm, ut, ycn, dsk_ref, wglu_ref, bglu_ref, nssm_ref, wout_ref,
              nffn_ref, wr_ref, br_ref):
    ne = br_ref.shape[0]
    y = _gelu_tanh(yssm + dsk_ref[...] * ut.astype(F32))
    glu = jnp.dot(y.astype(BF16), wglu_ref[...], preferred_element_type=F32) + bglu_ref[...]
    o = y * jax.nn.sigmoid(glu)
    ysn = _rms(o, nssm_ref[...]).astype(BF16)
    mix = jnp.concatenate([ysn, ycn], axis=-1)
    x1 = x + jnp.dot(mix, wout_ref[...], preferred_element_type=F32)
    hf = _rms(x1, nffn_ref[...])
    hf_hi = hf.astype(BF16)
    hf_lo = (hf - hf_hi.astype(F32)).astype(BF16)
    r = hf.shape[0]
    part = jnp.dot(jnp.concatenate([hf_hi, hf_lo], axis=0), wr_ref[...], preferred_element_type=F32)
    logits = (part[0:r, 0:LANE] + part[0:r, LANE:2 * LANE]) + (part[r:2 * r, 0:LANE] + part[r:2 * r, LANE:2 * LANE])
    lt = logits.T[0:ne, :] + br_ref[...]
    iota = lax.broadcasted_iota(I32, lt.shape, 0)
    vals, idxs = [], []
    sel = jnp.zeros(lt.shape, F32)
    for _ in range(TOP_K):
        m = jnp.max(lt, axis=0, keepdims=True)
        ik = jnp.min(jnp.where(lt == m, iota, ne), axis=0, keepdims=True)
        vals.append(m)
        idxs.append(ik)
        hit = iota == ik
        sel = sel + jnp.where(hit, 1.0, 0.0)
        lt = jnp.where(hit, -jnp.inf, lt)
    es = [jnp.exp(v - vals[0]) for v in vals]
    tot = es[0] + es[1] + es[2] + es[3]
    idx = jnp.concatenate(idxs, axis=0)
    gates = jnp.concatenate([e / tot for e in es], axis=0)
    return x1, hf_hi, idx, gates, sel


def _mix_kernel_prompt(*refs):
    x_ref, yc_ref = refs[0], refs[1]
    x1_ref, hf_ref, idx_ref, gate_ref, cnt_ref, ybuf = refs[N_MIX_IN:]
    nj, nb, ncz, _ = yc_ref.shape
    for s in range(CHUNK):
        for j in range(nj):
            ybuf[j, :, pl.ds(s, ncz, stride=CHUNK), :] = yc_ref[j, :, :, s * LANE:(s + 1) * LANE]
    ut_ref, ycn_ref = refs[2], refs[3]
    tt, d = x_ref.shape[1], x_ref.shape[2]
    cw = nj * LANE
    nbc = TOK_TILE // tt
    for t in range(cnt_ref.shape[0]):
        b0, r0 = t * nbc, t * TOK_TILE
        yssm = jnp.concatenate([ybuf[j, b0:b0 + nbc].reshape(TOK_TILE, LANE) for j in range(nj)], axis=-1)
        x1, hf, idx, gates, sel = _mix_rows(
            x_ref[b0:b0 + nbc].reshape(TOK_TILE, d), yssm, ut_ref[b0:b0 + nbc].reshape(TOK_TILE, cw),
            ycn_ref[b0:b0 + nbc].reshape(TOK_TILE, cw), *refs[4:N_MIX_IN])
        x1_ref[r0:r0 + TOK_TILE, :] = x1
        hf_ref[r0:r0 + TOK_TILE, :] = hf
        idx_ref[:, r0:r0 + TOK_TILE] = idx
        gate_ref[:, r0:r0 + TOK_TILE] = gates
        cnt_ref[t] = jnp.sum(sel, axis=1, keepdims=True).astype(I32)


def _mix_kernel_sample(*refs):
    x1, hf, idx, gates, sel = _mix_rows(refs[0][0], refs[1][...], refs[2][0], refs[3][0],
                                        *refs[4:N_MIX_IN])
    x1_ref, hf_ref, idx_ref, gate_ref, cnt_ref = refs[N_MIX_IN + 5:]
    ns = x1.shape[0]
    x1_ref[...] = jnp.zeros(x1_ref.shape, x1_ref.dtype)
    hf_ref[...] = jnp.zeros(hf_ref.shape, hf_ref.dtype)
    idx_ref[...] = jnp.zeros(idx_ref.shape, idx_ref.dtype)
    gate_ref[...] = jnp.zeros(gate_ref.shape, gate_ref.dtype)
    x1_ref[0:ns, :] = x1
    hf_ref[0:ns, :] = hf
    idx_ref[:, 0:ns] = idx
    gate_ref[:, 0:ns] = gates
    cnt_ref[0] = jnp.sum(sel, axis=1, keepdims=True).astype(I32)


class _Table:
    def __init__(self, ref, offset):
        self.ref, self.offset = ref, offset

    def __getitem__(self, k):
        return self.ref[self.offset + k]


def _pack_tables(tables):
    offsets, total = [], 0
    for t in tables:
        offsets.append(total)
        total += t.shape[0]
    return jnp.concatenate(tables), tuple(offsets)


def _split_bf16(x, parts):
    out = []
    for _ in range(parts - 1):
        p = x.astype(BF16)
        out.append(p)
        x = x - p.astype(F32)
    out.append(x.astype(BF16))
    return out


def _run_onehot(idx_ref, loff_ref, rgn_ref, tile, n_valid, nrun, gate_ref=None):
    ne = loff_ref.shape[1]
    tt = idx_ref.shape[1]
    e_iota = lax.broadcasted_iota(I32, (ne, tt), 0)
    tok = tile * tt + lax.broadcasted_iota(I32, (1, tt), 1)
    valid = tok < n_valid
    hits = [jnp.logical_and(e_iota == idx_ref[k:k + 1, :], valid) for k in range(TOP_K)]
    sel = jnp.zeros((ne, tt), F32)
    for h in hits:
        sel = sel + jnp.where(h, 1.0, 0.0)
    before = lax.broadcasted_iota(I32, (tt, tt), 0) < lax.broadcasted_iota(I32, (tt, tt), 1)
    tri = jnp.where(before, 1.0, 0.0).astype(BF16)
    base = jnp.dot(sel.astype(BF16), tri, preferred_element_type=F32) + loff_ref[0].astype(F32)
    base = jnp.where(sel > 0.0, base + 1.0, 0.0)
    b_hi = BF16_EXACT * jnp.floor(base * (1.0 / BF16_EXACT))
    b_lo = base - b_hi
    lo2 = jnp.concatenate([rgn_ref[0, 0:1, :]] * 2, axis=1)
    hi2 = jnp.concatenate([rgn_ref[0, 1:2, :]] * 2, axis=1)
    r_i = lax.broadcasted_iota(I32, (nrun, 2 * ne), 0)
    own2 = jnp.where(jnp.logical_and(r_i >= lo2, r_i < hi2), 1.0, 0.0).astype(BF16)
    halves = jnp.concatenate([b_hi, b_lo], axis=0).astype(BF16)
    want = jnp.dot(own2, halves, preferred_element_type=F32)
    r_f = (lax.broadcasted_iota(I32, (nrun, tt), 0) + 1).astype(F32)
    smat = jnp.where(want == r_f, 1.0, 0.0).astype(BF16)
    if gate_ref is None:
        return smat
    gate_e = jnp.zeros((ne, tt), F32)
    for k, h in enumerate(hits):
        gate_e = gate_e + jnp.where(h, gate_ref[k:k + 1, :], 0.0)
    nparts = 3
    pieces = jnp.concatenate(_split_bf16(gate_e, nparts), axis=0)
    per = lax.dot_general(smat, pieces, (((1,), (1,)), ((), ())), preferred_element_type=F32)
    lo3 = jnp.concatenate([rgn_ref[0, 0:1, :]] * nparts, axis=1)
    hi3 = jnp.concatenate([rgn_ref[0, 1:2, :]] * nparts, axis=1)
    r_i3 = lax.broadcasted_iota(I32, (nrun, nparts * ne), 0)
    own3 = jnp.logical_and(r_i3 >= lo3, r_i3 < hi3)
    gcol = jnp.sum(jnp.where(own3, per, 0.0), axis=1, keepdims=True)
    return smat, gcol


def _window_copy(buf, slot, hbm, lo, g, sem, to_hbm, rows, align):
    src = buf.at[slot, pl.ds(pl.multiple_of(lo, align), rows)]
    dst = hbm.at[pl.ds(pl.multiple_of(g, align), rows)]
    if to_hbm:
        return pltpu.make_async_copy(src, dst, sem.at[slot])
    return pltpu.make_async_copy(dst, src, sem.at[slot])


def _start_windows(wg_ref, wl_ref, tw_ref, tile, slots, buf, slot, hbm, sem, to_hbm, rows, align):
    def per_window(w, c):
        k = tile * slots + w
        lo = w * rows if wl_ref is None else wl_ref[k]
        _window_copy(buf, slot, hbm, lo, wg_ref[k], sem, to_hbm, rows, align).start()
        return c

    lax.fori_loop(0, tw_ref[tile], per_window, 0)


def _wait_windows(count, buf, slot, hbm, sem, to_hbm, rows, align):
    batch = 8

    def many(w, c):
        _window_copy(buf, slot, hbm, 0, 0, sem, to_hbm, batch * rows, align).wait()
        return c

    def one(w, c):
        _window_copy(buf, slot, hbm, 0, 0, sem, to_hbm, rows, align).wait()
        return c

    lax.fori_loop(0, count // batch, many, 0)
    lax.fori_loop(0, count % batch, one, 0)


def _dispatch_kernel(offsets, tab_ref,
                     hf_ref, idx_ref, gate_ref, loff_ref, rgn_ref, xs_ref, buf, carry, sem):
    wg_ref, wl_ref, mlo_ref, mtg_ref, keep_ref, tw_ref, nv_ref = [_Table(tab_ref, o) for o in offsets]
    i = pl.program_id(0)
    nt = pl.num_programs(0)
    ne = loff_ref.shape[1]
    tt, d = hf_ref.shape
    nrun = buf.shape[1]
    slot = i % 2

    @pl.when(i == 0)
    def _():
        carry[...] = jnp.zeros(carry.shape, carry.dtype)

    smat, gcol = _run_onehot(idx_ref, loff_ref, rgn_ref, i, nv_ref[0], nrun, gate_ref)
    xr = jnp.dot(smat, hf_ref[...], preferred_element_type=F32)
    lane0 = lax.broadcasted_iota(I32, (nrun, GATE_COLS), 1) == 0
    buf[slot, :, 0:d // 2] = _pack_pairs(xr[:, :d // 2], xr[:, d // 2:])
    buf[slot, :, d // 2:] = pltpu.bitcast(jnp.where(lane0, gcol, 0.0), U32)

    def merge(e, c):
        k = i * ne + e
        lo = pl.multiple_of(mlo_ref[k], SUBLANE)
        buf[slot, pl.ds(lo, SUBLANE), :] = buf[slot, pl.ds(lo, SUBLANE), :] | carry[e]
        tg = pl.multiple_of(mtg_ref[k], SUBLANE)
        carry[e] = jnp.where(keep_ref[k] > 0, carry[e], buf[slot, pl.ds(tg, SUBLANE), :])
        return c

    lax.fori_loop(0, ne, merge, 0, unroll=4)

    @pl.when(i > 0)
    def _():
        _wait_windows(tw_ref[i - 1], buf, 1 - slot, xs_ref, sem, True, WIN, SUBLANE)

    _start_windows(wg_ref, wl_ref, tw_ref, i, MAX_WINDOWS, buf, slot, xs_ref, sem, True, WIN, SUBLANE)

    @pl.when(i == nt - 1)
    def _():
        _wait_windows(tw_ref[i], buf, slot, xs_ref, sem, True, WIN, SUBLANE)


def _expert_weight_copies(wg_hbm, wd_hbm, wg_f32, wd_f32, sem, e, slot):
    return (pltpu.make_async_copy(wg_hbm.at[e], wg_f32.at[slot], sem.at[0, slot]),
            pltpu.make_async_copy(wd_hbm.at[e], wd_f32.at[slot], sem.at[1, slot]))


def _moe_kernel(offsets, tab_ref,
                x_ref, bg_ref, bd_ref, wg_hbm, wd_hbm, y_ref,
                wg_f32, wd_f32, wg_bf, wd_bf, sem):
    be_ref, bv_ref, first_ref, nxt_ref, slot_ref = [_Table(tab_ref, o) for o in offsets]
    i = pl.program_id(0)
    e = be_ref[i]
    dff = wd_bf.shape[0]
    bm = x_ref.shape[0]
    sub = MOE_SUB
    nw = x_ref.shape[1] - GATE_COLS

    @pl.when(first_ref[i] > 0)
    def _():
        slot = slot_ref[i]

        @pl.when(first_ref[i] > 1)
        def _():
            for cp in _expert_weight_copies(wg_hbm, wd_hbm, wg_f32, wd_f32, sem, e, slot):
                cp.start()

        for cp in _expert_weight_copies(wg_hbm, wd_hbm, wg_f32, wd_f32, sem, e, slot):
            cp.wait()

        @pl.when(nxt_ref[i] >= 0)
        def _():
            for cp in _expert_weight_copies(wg_hbm, wd_hbm, wg_f32, wd_f32, sem, nxt_ref[i], 1 - slot):
                cp.start()

        wg_bf[...] = wg_f32[slot].astype(BF16)
        wd_bf[...] = wd_f32[slot].astype(BF16)

    def rows(r0, nrows):
        sizes = [MOE_CHAIN] * (nrows // MOE_CHAIN) + ([nrows % MOE_CHAIN] if nrows % MOE_CHAIN else [])
        lo = r0
        for size in sizes:
            _chain(lo, size)
            lo += size

    def _chain(lo, n):
        live = lax.broadcasted_iota(I32, (n, 1), 0) + lo < bv_ref[i]
        x = jnp.where(live, _unpack_pairs(x_ref[lo:lo + n, 0:nw]), jnp.zeros((), BF16))
        route = jnp.where(live, pltpu.bitcast(x_ref[lo:lo + n, nw:], F32)[:, 0:1], 0.0)
        gu = jnp.dot(x, wg_bf[...], preferred_element_type=F32) + bg_ref[0]
        gate = jnp.minimum(gu[:, :dff], SWIGLU_LIMIT)
        up = jnp.clip(gu[:, dff:], -SWIGLU_LIMIT, SWIGLU_LIMIT)
        h = gate * jax.nn.sigmoid(SWIGLU_ALPHA * gate) * (up + 1.0)
        y = jnp.dot(h.astype(BF16), wd_bf[...], preferred_element_type=F32) + bd_ref[0]
        yr = (route * y).astype(BF16).astype(F32)
        half = yr.shape[1] // 2
        y_ref[lo:lo + n, :] = _pack_pairs(yr[:, :half], yr[:, half:])

    nchains = bm // sub
    for live_chains in range(nchains + 1):
        lo_rows, hi_rows = (live_chains - 1) * sub, live_chains * sub

        @pl.when(jnp.logical_and(bv_ref[i] > lo_rows, bv_ref[i] <= hi_rows) if live_chains
                 else bv_ref[i] <= 0)
        def _(used=hi_rows):
            if used:
                rows(0, used)
            if used < bm:
                y_ref[used:bm, :] = jnp.zeros((bm - used, y_ref.shape[1]), y_ref.dtype)


def _combine_tile(step, nt):
    return (step + nt - 1) % nt


def _combine_kernel(offsets, tab_ref,
                    x1_ref, idx_ref, loff_ref, rgn_ref, nf_ref, yb_ref,
                    yp_ref, ys_ref, buf, sem):
    wg_ref, tw_ref, nv_ref = [_Table(tab_ref, o) for o in offsets]
    s = pl.program_id(0)
    nt = pl.num_programs(0)
    tt, d = x1_ref.shape
    nrun = buf.shape[1]
    tile = _combine_tile(s, nt)
    slot = s % 2

    @pl.when(s == 0)
    def _():
        buf[...] = jnp.zeros(buf.shape, buf.dtype)
        _start_windows(wg_ref, None, tw_ref, tile, MAX_WINDOWS, buf, slot, yb_ref, sem, False,
                       WIN, SUBLANE)

    @pl.when(s + 1 < nt)
    def _():
        _start_windows(wg_ref, None, tw_ref, _combine_tile(s + 1, nt), MAX_WINDOWS, buf, 1 - slot,
                       yb_ref, sem, False, WIN, SUBLANE)

    smat = _run_onehot(idx_ref, loff_ref, rgn_ref, tile, nv_ref[0], nrun)

    _wait_windows(tw_ref[tile], buf, slot, yb_ref, sem, False, WIN, SUBLANE)
    yrun = _unpack_pairs(buf[slot])
    moe = lax.dot_general(smat, yrun, (((0,), (0,)), ((), ())), preferred_element_type=F32)
    out = _rms(x1_ref[...] + moe, nf_ref[...])

    @pl.when(s == 0)
    def _():
        ys_ref[...] = out[0:ys_ref.shape[0], :]

    @pl.when(s > 0)
    def _():
        yp_ref[...] = out.reshape(yp_ref.shape)


def _ssm_matrices(a_re, a_im, log_dt, b_re, b_im, c_re, c_im):
    g, n = a_re.shape
    hch = b_re.shape[2]
    gpt = LANE // hch
    nj = g // gpt
    a = lax.complex(a_re, a_im)
    dta = a * jnp.exp(log_dt)[:, None]
    a_bar = jnp.exp(dta)
    bb = ((a_bar - 1.0) / a)[:, :, None] * lax.complex(b_re, b_im)
    cc = lax.complex(c_re, c_im)
    ks = jnp.arange(CHUNK + 1, dtype=F32)
    pw = jnp.exp(dta[None] * ks[:, None, None])
    kk = jnp.real(jnp.einsum('gon,kgn,gni->kgio', cc, pw[:CHUNK], bb))
    kk = jnp.concatenate([jnp.zeros_like(kk[:1]), kk], axis=0)
    kc = kk.reshape(CHUNK + 1, nj, gpt * hch, hch).transpose(1, 0, 2, 3)
    kc = kc.reshape(nj, (CHUNK + 1) * LANE, hch).astype(BF16)
    pwr, pwi = jnp.real(pw), jnp.imag(pw)
    bbr = jnp.real(bb).transpose(0, 2, 1).reshape(1, g * hch, n)
    bbi = jnp.imag(bb).transpose(0, 2, 1).reshape(1, g * hch, n)
    par = jnp.repeat(pwr[CHUNK - 1::-1][:CHUNK], hch, axis=1)
    pai = jnp.repeat(pwi[CHUNK - 1::-1][:CHUNK], hch, axis=1)
    pc = jnp.concatenate([par * bbr - pai * bbi, par * bbi + pai * bbr], axis=-1)
    pc = pc.reshape(CHUNK, nj, gpt * hch, 2 * n).transpose(1, 0, 2, 3)
    pc = pc.reshape(nj, CHUNK * LANE, 2 * n).astype(BF16)
    ccr = jnp.real(cc).transpose(2, 0, 1).reshape(n, 1, g * hch)
    cci = jnp.imag(cc).transpose(2, 0, 1).reshape(n, 1, g * hch)
    qar = jnp.repeat(pwr[1:CHUNK + 1].transpose(2, 0, 1), hch, axis=2)
    qai = jnp.repeat(pwi[1:CHUNK + 1].transpose(2, 0, 1), hch, axis=2)
    rc = jnp.stack([ccr * qar - cci * qai, -(ccr * qai + cci * qar)], axis=0)
    rc = rc.reshape(2 * n, CHUNK, nj, gpt * hch).transpose(2, 0, 1, 3)
    rc = rc.reshape(nj, 2 * n, CHUNK * LANE).astype(BF16)
    a16 = pw[CHUNK].reshape(nj, 1, gpt * n)
    a16 = jnp.concatenate([jnp.real(a16), jnp.imag(a16)], axis=1)
    bc = jnp.stack([jnp.real(bb), jnp.imag(bb)], axis=0).transpose(1, 3, 0, 2)
    bc = bc.reshape(g * hch, 2 * n).astype(BF16)
    c2 = jnp.stack([jnp.real(cc), -jnp.imag(cc)], axis=0).transpose(0, 3, 1, 2)
    c2 = c2.reshape(2 * n, g * hch).astype(BF16)
    abr = jnp.real(a_bar).reshape(1, g * n)
    abi = jnp.imag(a_bar).reshape(1, g * n)
    return kc, pc, rc, a16, bc, c2, abr, abi


def _full(shape):
    return pl.BlockSpec(shape, lambda *_: (0,) * len(shape))


def kernel(x_prompt, x_sample, state_ssm_re, state_ssm_im, state_conv, meta_tokens, norm_mix, w_in,
           ssm_a_re, ssm_a_im, ssm_log_dt, ssm_b_re, ssm_b_im, ssm_c_re, ssm_c_im, ssm_d, w_glu, b_glu,
           conv_w, norm_out_ssm, norm_out_conv, w_out, norm_ffn, w_router, b_router, w_gate_up,
           b_gate_up, w_down, b_down, norm_final):
    nb, seq, d = x_prompt.shape
    ns = x_sample.shape[0]
    depth, _, g, n = state_ssm_re.shape
    assert depth == 1 and x_sample.shape[1] == 1 and meta_tokens.shape[0] == CHUNK
    cw = conv_w.shape[2]
    nj = cw // LANE
    ne = w_router.shape[2]
    dff = w_down.shape[2]
    nst = g * n
    nbt = nb // 2
    tt = 256
    n_chunks = seq // CHUNK
    tp = nb * seq
    tall = tp + ns
    tm = 128
    nbm = nb
    rows_p = nbm * tm
    assert rows_p % TOK_TILE == 0 and TOK_TILE % tm == 0 and ns <= TOK_TILE and d % 2 == 0
    n_tiles = tp // TOK_TILE + 1
    ta = n_tiles * TOK_TILE

    kc, pc, rc, a16, bc, c2, abr, abi = _ssm_matrices(
        ssm_a_re[0], ssm_a_im[0], ssm_log_dt[0], ssm_b_re[0], ssm_b_im[0], ssm_c_re[0], ssm_c_im[0])
    win_bf = w_in[0].astype(BF16)
    nmix = norm_mix[0].reshape(1, d)
    nconv = norm_out_conv[0].reshape(1, cw)
    cwt = conv_w[0]

    xsm = jnp.concatenate([x_sample.reshape(ns, d), meta_tokens], axis=0)
    nsm = ns + CHUNK
    s0r = state_ssm_re[0].reshape(ns, nst)
    s0i = state_ssm_im[0].reshape(ns, nst)
    buf0 = state_conv[0, :, 0, :]
    buf1 = state_conv[0, :, 1, :]
    u_sm, z_sm, y_s, ycn_s, sr_s, si_s = pl.pallas_call(
        _small_front_kernel,
        out_shape=(jax.ShapeDtypeStruct((nsm, cw), F32), jax.ShapeDtypeStruct((nsm, cw), F32),
                   jax.ShapeDtypeStruct((ns, cw), F32), jax.ShapeDtypeStruct((ns, cw), BF16),
                   jax.ShapeDtypeStruct((ns, nst), F32), jax.ShapeDtypeStruct((ns, nst), F32)),
        scratch_shapes=[pltpu.VMEM((cw, 2 * nst), BF16), pltpu.VMEM((2 * nst, cw), BF16)],
        **_params("small_front", None),
    )(xsm, nmix, win_bf, s0r, s0i, buf0, buf1, cwt, bc, c2, abr, abi, nconv)
    u_meta = u_sm[ns:]
    z_meta8 = z_sm[ns + CHUNK - 8:]
    new_conv_s = jnp.stack([buf1, z_sm[:ns]], axis=1)[None]
    new_re_s = sr_s.reshape(1, ns, g, n)
    new_im_s = si_s.reshape(1, ns, g, n)

    wch = CHUNK * LANE
    u4c, u_tok, ycn_p, ztail = pl.pallas_call(
        _front_kernel,
        grid=(nb // nbt, seq // tt),
        in_specs=[pl.BlockSpec((nbt, tt, d), lambda b, i: (b, i, 0)),
                  _full((1, d)), _full((d, 4 * cw)), _full((8, cw)), _full((3, cw)), _full((1, cw))],
        out_specs=(pl.BlockSpec((nj, nbt, tt // CHUNK, wch), lambda b, i: (0, b, i, 0)),
                   pl.BlockSpec((nbt, tt, cw), lambda b, i: (b, i, 0)),
                   pl.BlockSpec((nbt, tt, cw), lambda b, i: (b, i, 0)),
                   pl.BlockSpec((nbt, 8, cw), lambda b, i: (b, 0, 0))),
        out_shape=(jax.ShapeDtypeStruct((nj, nb, n_chunks, wch), BF16),
                   jax.ShapeDtypeStruct((nb, seq, cw), BF16),
                   jax.ShapeDtypeStruct((nb, seq, cw), BF16),
                   jax.ShapeDtypeStruct((nb, 8, cw), F32)),
        scratch_shapes=[pltpu.VMEM((nbt, tt + 8, cw), F32), pltpu.VMEM((nj, nbt, tt, LANE), F32)],
        **_params("front", ("arbitrary", "arbitrary")),
    )(x_prompt, nmix, win_bf, z_meta8, cwt, nconv)
    new_conv_p = ztail[:, 6:8, :][None]

    cc = n_chunks // 2
    um = u_meta.reshape(CHUNK, nj, LANE).transpose(1, 0, 2).reshape(nj, 1, wch)
    um = jnp.broadcast_to(um, (nj, 8 * nb, wch)).astype(BF16)
    gpt = g // nj
    nstj = 2 * gpt * n
    hch = cw // g
    y4c, s_last = pl.pallas_call(
        _ssm_kernel,
        grid=(nj, n_chunks // cc),
        in_specs=[pl.BlockSpec((1, nb, cc, wch), lambda j, t: (j, 0, t, 0)),
                  pl.BlockSpec((1, 8 * nb, wch), lambda j, t: (j, 0, 0)),
                  pl.BlockSpec((1, (CHUNK + 1) * LANE, hch), lambda j, t: (j, 0, 0)),
                  pl.BlockSpec((1, wch, 2 * n), lambda j, t: (j, 0, 0)),
                  pl.BlockSpec((1, 2 * n, wch), lambda j, t: (j, 0, 0)),
                  pl.BlockSpec((1, 2, nstj // 2), lambda j, t: (j, 0, 0))],
        out_specs=(pl.BlockSpec((1, nb, cc, wch), lambda j, t: (j, 0, t, 0)),
                   pl.BlockSpec((1, nb, 1, nstj), lambda j, t: (j, 0, 0, 0))),
        out_shape=(jax.ShapeDtypeStruct((nj, nb, n_chunks, wch), F32),
                   jax.ShapeDtypeStruct((nj, nb, 1, nstj), F32)),
        scratch_shapes=[pltpu.VMEM((nb, 1, nstj), F32), pltpu.VMEM((nb, cc, nstj), F32),
                        pltpu.VMEM((nb, cc, nstj), F32),
                        pltpu.VMEM((wch, nstj), BF16), pltpu.VMEM((nstj, wch), BF16),
                        pltpu.VMEM((CHUNK // 2, 2 * LANE, 2 * LANE), BF16)],
        **_params("ssm", ("parallel", "arbitrary")),
    )(u4c, um, kc, pc, rc, a16)
    sl = s_last.reshape(nj, nb, 2, gpt, n)
    new_re_p = sl[:, :, 0].transpose(1, 0, 2, 3).reshape(1, nb, g, n)
    new_im_p = sl[:, :, 1].transpose(1, 0, 2, 3).reshape(1, nb, g, n)

    dsk = ssm_d[0].reshape(1, cw)
    wglu_bf = w_glu[0].astype(BF16)
    bglu = b_glu[0].reshape(1, cw)
    nssm = norm_out_ssm[0].reshape(1, cw)
    wout_bf = w_out[0].astype(BF16)
    nffn = norm_ffn[0].reshape(1, d)
    wr_pad = jnp.zeros((d, LANE), F32).at[:, :ne].set(w_router[0])
    wr_hi = wr_pad.astype(BF16)
    wr_lo = (wr_pad - wr_hi.astype(F32)).astype(BF16)
    br = b_router[0].reshape(ne, 1)
    wr2 = jnp.concatenate([wr_hi, wr_lo], axis=1)
    mix_w = (dsk, wglu_bf, bglu, nssm, wout_bf, nffn, wr2, br)
    mix_w_specs = [_full((1, cw)), _full((cw, cw)), _full((1, cw)), _full((1, cw)), _full((2 * cw, d)),
                   _full((1, d)), _full((d, 2 * LANE)), _full((ne, 1))]
    assert 4 + len(mix_w) == N_MIX_IN
    mix_out_shape = (jax.ShapeDtypeStruct((ta, d), F32), jax.ShapeDtypeStruct((ta, d), BF16),
                     jax.ShapeDtypeStruct((TOP_K, ta), I32), jax.ShapeDtypeStruct((TOP_K, ta), F32),
                     jax.ShapeDtypeStruct((n_tiles, ne, 1), I32))
    tpm = rows_p // TOK_TILE
    nbg = nb // nbm
    x1_all, hf_all, idx_all, gate_all, cnt = pl.pallas_call(
        _mix_kernel_prompt,
        grid=(seq // tm, nbg),
        in_specs=[pl.BlockSpec((nbm, tm, d), lambda i, b: (b, i, 0)),
                  pl.BlockSpec((nj, nbm, tm // CHUNK, wch), lambda i, b: (0, b, i, 0)),
                  pl.BlockSpec((nbm, tm, cw), lambda i, b: (b, i, 0)),
                  pl.BlockSpec((nbm, tm, cw), lambda i, b: (b, i, 0))] + mix_w_specs,
        out_specs=(pl.BlockSpec((rows_p, d), lambda i, b: (i * nbg + b, 0)),
                   pl.BlockSpec((rows_p, d), lambda i, b: (i * nbg + b, 0)),
                   pl.BlockSpec((TOP_K, rows_p), lambda i, b: (0, i * nbg + b)),
                   pl.BlockSpec((TOP_K, rows_p), lambda i, b: (0, i * nbg + b)),
                   pl.BlockSpec((tpm, ne, 1), lambda i, b: (i * nbg + b, 0, 0))),
        out_shape=mix_out_shape,
        scratch_shapes=[pltpu.VMEM((nj, nbm, tm, LANE), F32)],
        **_params("mix_prompt", ("parallel", "parallel")),
    )(x_prompt, y4c, u_tok, ycn_p, *mix_w)

    last = n_tiles - 1
    any_spec = pl.BlockSpec(memory_space=pl.ANY)
    x1_all, hf_all, idx_all, gate_all, cnt = pl.pallas_call(
        _mix_kernel_sample,
        grid=(1,),
        in_specs=[_full((1, ns, d)), _full((ns, cw)), _full((1, ns, cw)),
                  _full((1, ns, cw))] + mix_w_specs + [any_spec] * 5,
        out_specs=(pl.BlockSpec((TOK_TILE, d), lambda i: (last, 0)),
                   pl.BlockSpec((TOK_TILE, d), lambda i: (last, 0)),
                   pl.BlockSpec((TOP_K, TOK_TILE), lambda i: (0, last)),
                   pl.BlockSpec((TOP_K, TOK_TILE), lambda i: (0, last)),
                   pl.BlockSpec((1, ne, 1), lambda i: (last, 0, 0))),
        out_shape=mix_out_shape,
        input_output_aliases={N_MIX_IN + k: k for k in range(5)},
        **_params("mix_sample", ("arbitrary",)),
    )(x_sample.reshape(1, ns, d), y_s, u_sm[:ns].astype(BF16).reshape(1, ns, cw),
      ycn_s.reshape(1, ns, cw), *mix_w, x1_all, hf_all, idx_all, gate_all, cnt)

    bm = MOE_ROWS
    cnt2 = cnt.reshape(n_tiles, ne)
    before = jnp.cumsum(cnt2, axis=0) - cnt2
    count = jnp.sum(cnt2, axis=0)
    padded = ((count + WIN + bm - 1) // bm) * bm
    pend = jnp.cumsum(padded)
    pstart = pend - padded
    phase = before % SUBLANE
    span = jnp.where(cnt2 > 0, phase + cnt2, 0)
    gstart = (pstart[None, :] + before - phase).astype(I32).reshape(-1)
    nwin = ((span + WIN - 1) // WIN).astype(I32)
    reg8 = ((span + SUBLANE - 1) // SUBLANE) * SUBLANE
    loff = (jnp.cumsum(reg8, axis=1) - reg8).astype(I32)
    tail = jnp.where(span % SUBLANE != 0, loff + (span // SUBLANE) * SUBLANE, -1).astype(I32)
    twin = jnp.sum(nwin, axis=1).astype(I32)
    n_blocks = (ta * TOP_K + ne * (WIN + bm - 1) + bm - 1) // bm
    cap = n_blocks * bm
    blk0 = jnp.arange(n_blocks, dtype=I32) * bm
    blk_e = jnp.minimum(jnp.sum((pend[None, :] <= blk0[:, None]).astype(I32), axis=1), ne - 1)
    e_ar = jnp.arange(ne, dtype=I32)
    blk_hot = blk_e[:, None] == e_ar[None, :]

    def _of_block(per_expert):
        return jnp.sum(jnp.where(blk_hot, per_expert[None, :], 0), axis=1)

    blk_valid = jnp.clip(_of_block(count) - (blk0 - _of_block(pstart)), 0, bm).astype(I32)
    has = count > 0
    later = jnp.logical_and(e_ar[None, :] > e_ar[:, None], has[None, :])
    nxt_e = jnp.min(jnp.where(later, e_ar[None, :], ne), axis=1)
    nxt_e = jnp.where(nxt_e < ne, nxt_e, -1)
    ordinal = jnp.cumsum(has.astype(I32)) - 1
    is_first = jnp.logical_and(blk_valid > 0, blk0 == _of_block(pstart))
    blk_first = jnp.where(is_first, jnp.where(_of_block(ordinal) == 0, 2, 1), 0).astype(I32)
    blk_next = _of_block(nxt_e).astype(I32)
    blk_slot = (_of_block(ordinal) % 2).astype(I32)
    nvalid = jnp.full((1,), tall, I32)
    loff_al = (WIN * (jnp.cumsum(nwin, axis=1) - nwin)).astype(I32)

    def _window_list(nw, slots, rows, first_hbm, first_buf):
        wcum = jnp.cumsum(nw, axis=1)
        wslot = jnp.arange(slots, dtype=I32)
        w_hot = jnp.logical_and(wslot[None, :, None] >= (wcum - nw)[:, None, :],
                                wslot[None, :, None] < wcum[:, None, :])

        def _of_window(per_tile_expert):
            return jnp.sum(jnp.where(w_hot, per_tile_expert[:, None, :], 0), axis=2)

        w_in_run = wslot[None, :] - _of_window(wcum - nw)
        return [(_of_window(f) + rows * w_in_run).astype(I32).reshape(-1) for f in (first_hbm, first_buf)]

    w_hbm, w_buf = _window_list(nwin, MAX_WINDOWS, WIN, gstart.reshape(n_tiles, ne), loff)
    nrun_d = _dispatch_run_rows(ne)
    zero_grp, spare_grp = nrun_d - 2 * SUBLANE, nrun_d - SUBLANE
    m_lo = jnp.where(nwin > 0, loff, spare_grp).astype(I32).reshape(-1)
    m_tg = jnp.where(jnp.logical_and(nwin > 0, tail >= 0), tail, zero_grp).astype(I32).reshape(-1)
    m_keep = (nwin == 0).astype(I32).reshape(-1)
    tab_d, off_d = _pack_tables((w_hbm, w_buf, m_lo, m_tg, m_keep, twin, nvalid))
    tab_c, off_c = _pack_tables((w_hbm, twin, nvalid))
    tab_m, off_m = _pack_tables((blk_e, blk_valid, blk_first, blk_next, blk_slot))
    rowoff_d = (loff + phase).astype(I32).reshape(n_tiles, ne, 1)
    rowoff_c = (loff_al + phase).astype(I32).reshape(n_tiles, ne, 1)
    rgn_d = jnp.stack([loff, loff + reg8], axis=1).astype(I32)
    rgn_c = jnp.stack([loff_al, loff_al + WIN * nwin], axis=1).astype(I32)
    xw = d // 2 + GATE_COLS

    xs = pl.pallas_call(
        functools.partial(_dispatch_kernel, off_d),
        grid_spec=pltpu.PrefetchScalarGridSpec(
            num_scalar_prefetch=1,
            grid=(n_tiles,),
            in_specs=[pl.BlockSpec((TOK_TILE, d), lambda i, *_: (i, 0)),
                      pl.BlockSpec((TOP_K, TOK_TILE), lambda i, *_: (0, i)),
                      pl.BlockSpec((TOP_K, TOK_TILE), lambda i, *_: (0, i)),
                      pl.BlockSpec((1, ne, 1), lambda i, *_: (i, 0, 0)),
                      pl.BlockSpec((1, 2, ne), lambda i, *_: (i, 0, 0))],
            out_specs=pl.BlockSpec(memory_space=pl.ANY),
            scratch_shapes=[pltpu.VMEM((2, _dispatch_run_rows(ne), xw), U32),
                            pltpu.VMEM((ne, SUBLANE, xw), U32), pltpu.SemaphoreType.DMA((2,))]),
        out_shape=jax.ShapeDtypeStruct((cap, xw), U32),
        **_params("dispatch", ("arbitrary",)),
    )(tab_d, hf_all, idx_all, gate_all, rowoff_d, rgn_d)

    yb = pl.pallas_call(
        functools.partial(_moe_kernel, off_m),
        grid_spec=pltpu.PrefetchScalarGridSpec(
            num_scalar_prefetch=1,
            grid=(n_blocks,),
            in_specs=[pl.BlockSpec((bm, xw), lambda i, tab: (i, 0)),
                      pl.BlockSpec((1, 1, 2 * dff), lambda i, tab: (tab[off_m[0] + i], 0, 0)),
                      pl.BlockSpec((1, 1, d), lambda i, tab: (tab[off_m[0] + i], 0, 0)),
                      pl.BlockSpec(memory_space=pl.ANY), pl.BlockSpec(memory_space=pl.ANY)],
            out_specs=pl.BlockSpec((bm, d // 2), lambda i, *_: (i, 0)),
            scratch_shapes=[pltpu.VMEM((2, d, 2 * dff), F32), pltpu.VMEM((2, dff, d), F32),
                            pltpu.VMEM((d, 2 * dff), BF16), pltpu.VMEM((dff, d), BF16),
                            pltpu.SemaphoreType.DMA((2, 2))]),
        out_shape=jax.ShapeDtypeStruct((cap, d // 2), U32),
        **_params("moe", ("arbitrary",)),
    )(tab_m, xs, b_gate_up[0].reshape(ne, 1, 2 * dff),
      b_down[0].reshape(ne, 1, d), w_gate_up[0], w_down[0])

    nfin = norm_final.reshape(1, d)
    nbh = TOK_TILE // tm

    def _tile_of(s):
        return (s + n_tiles - 1) % n_tiles

    tiles_per_time = nbg * tpm

    def _yp_index(s, *_):
        t = jnp.maximum(s - 1, 0)
        return (t % tiles_per_time, t // tiles_per_time, 0)

    y_p, y_sm = pl.pallas_call(
        functools.partial(_combine_kernel, off_c),
        grid_spec=pltpu.PrefetchScalarGridSpec(
            num_scalar_prefetch=1,
            grid=(n_tiles,),
            in_specs=[pl.BlockSpec((TOK_TILE, d), lambda s, *_: (_tile_of(s), 0)),
                      pl.BlockSpec((TOP_K, TOK_TILE), lambda s, *_: (0, _tile_of(s))),
                      pl.BlockSpec((1, ne, 1), lambda s, *_: (_tile_of(s), 0, 0)),
                      pl.BlockSpec((1, 2, ne), lambda s, *_: (_tile_of(s), 0, 0)),
                      pl.BlockSpec((1, d), lambda s, *_: (0, 0)),
                      pl.BlockSpec(memory_space=pl.ANY)],
            out_specs=(pl.BlockSpec((nbh, tm, d), _yp_index),
                       pl.BlockSpec((ns, d), lambda s, *_: (0, 0))),
            scratch_shapes=[pltpu.VMEM((2, _combine_run_rows(ne), d // 2), U32),
                            pltpu.SemaphoreType.DMA((2,))]),
        out_shape=(jax.ShapeDtypeStruct((nb, seq, d), F32), jax.ShapeDtypeStruct((ns, d), F32)),
        **_params("combine", ("arbitrary",)),
    )(tab_c, x1_all, idx_all, rowoff_c, rgn_c, nfin, yb)

    return (y_p, y_sm.reshape(ns, 1, d), new_re_p, new_im_p, new_conv_p,
            new_re_s, new_im_s, new_conv_s)
```

```python
import functools
import math

import jax
import jax.numpy as jnp
from jax import lax
from jax.experimental import pallas as pl
from jax.experimental.pallas import tpu as pltpu

F32 = jnp.float32
BF16 = jnp.bfloat16
U32 = jnp.uint32
I32 = jnp.int32
EPS = 1e-5
CHUNK = 16
LANE = 128
TOP_K = 4
SWIGLU_LIMIT = 7.0
SWIGLU_ALPHA = 1.702
MOE_ROWS = 1024
MOE_SUB = 256
MOE_CHAIN = 512
TOK_TILE = 256
WIN = 32
SUBLANE = 8
GATE_COLS = LANE


MAX_WINDOWS = 72


def _dispatch_run_rows(ne):
    return TOP_K * TOK_TILE + ne * 2 * (SUBLANE - 1) + WIN


def _combine_run_rows(ne):
    return -(-(TOP_K * TOK_TILE + ne * (SUBLANE - 1 + WIN - 1)) // WIN) * WIN


DISPATCH_BASE_ROWS = TOP_K * TOK_TILE + 256
COMBINE_BASE_ROWS = TOP_K * TOK_TILE + 768
COMBINE_STEP_ROWS = 256
GATE_PIECES = 3
HI_MASK = 0xFFFF0000
BF16_EXACT = 256.0
MIB = 1024 * 1024
VMEM_MIB = {"small_front": 56, "front": 52, "ssm": 56, "mix_prompt": 56, "mix_sample": 32,
            "dispatch": 40, "moe": 58, "combine": 48}


def _rms(x, g):
    return x * lax.rsqrt(jnp.mean(x * x, axis=-1, keepdims=True) + EPS) * g


def _gelu_tanh(x):
    c = math.sqrt(2.0 / math.pi)
    return 0.5 * x * (1.0 + jnp.tanh(c * (x + 0.044715 * (x * x * x))))


def _params(name, sem):
    return dict(name=name, compiler_params=pltpu.CompilerParams(
        dimension_semantics=sem, vmem_limit_bytes=VMEM_MIB[name] * MIB))


def _pack_pairs(a, b):
    return (pltpu.bitcast(a, U32) >> 16) | (pltpu.bitcast(b, U32) & jnp.uint32(HI_MASK))


def _unpack_pairs(w):
    lo = pltpu.bitcast(w << 16, F32)
    hi = pltpu.bitcast(w & jnp.uint32(HI_MASK), F32)
    return jnp.concatenate([lo, hi], axis=-1).astype(BF16)


def _iota2(shape, axis):
    return lax.broadcasted_iota(I32, shape, axis)


def _expand_cols(compact, reps_log2, n_log2):
    q = _iota2((compact.shape[1], 2 << (reps_log2 + n_log2)), 0)
    c = _iota2((compact.shape[1], 2 << (reps_log2 + n_log2)), 1)
    nmask = (1 << n_log2) - 1
    same = jnp.logical_and((q >> n_log2) == (c >> (reps_log2 + n_log2)), (q & nmask) == (c & nmask))
    return jnp.dot(compact, jnp.where(same, 1.0, 0.0).astype(BF16), preferred_element_type=F32)


def _expand_rows(compact, reps_log2, n_log2):
    r = _iota2((2 << (reps_log2 + n_log2), compact.shape[0]), 0)
    q = _iota2((2 << (reps_log2 + n_log2), compact.shape[0]), 1)
    nmask = (1 << n_log2) - 1
    same = jnp.logical_and((r >> (reps_log2 + n_log2)) == (q >> n_log2), (r & nmask) == (q & nmask))
    return jnp.dot(jnp.where(same, 1.0, 0.0).astype(BF16), compact, preferred_element_type=F32)


def _group_mask(shape, row_shift, col_shift, ngroups):
    r = _iota2(shape, 0)
    c = _iota2(shape, 1)
    return ((r >> row_shift) & (ngroups - 1)) == ((c >> col_shift) & (ngroups - 1))


def _small_front_kernel(x_ref, nmix_ref, win_ref, s0r_ref, s0i_ref, b0_ref, b1_ref, cw_ref,
                        bc_ref, cc_ref, abr_ref, abi_ref, nconv_ref,
                        u_ref, z_ref, y_ref, ycn_ref, sr_ref, si_ref, bdb_ref, cm_ref):
    ns, nst = s0r_ref.shape
    cw = u_ref.shape[1]
    n = bc_ref.shape[1] // 2
    nlog = n.bit_length() - 1
    glog = (nst // n).bit_length() - 1
    hlog = (cw >> glog).bit_length() - 1
    bdb_ref[...] = jnp.where(_group_mask(bdb_ref.shape, hlog, nlog, 1 << glog),
                             _expand_cols(bc_ref[...], glog, nlog), 0.0).astype(BF16)
    cm_ref[...] = jnp.where(_group_mask(cm_ref.shape, nlog, hlog, 1 << glog),
                            _expand_rows(cc_ref[...], glog, nlog), 0.0).astype(BF16)
    h = _rms(x_ref[...], nmix_ref[...]).astype(BF16)
    proj = jnp.dot(h, win_ref[...], preferred_element_type=F32)
    u = proj[:, 0:cw]
    zc = proj[:, cw:2 * cw]
    gb = proj[:, 2 * cw:3 * cw]
    gc = proj[:, 3 * cw:4 * cw]
    z = gc * zc
    u_ref[...] = u
    z_ref[...] = z
    bu = jnp.dot(u[:ns].astype(BF16), bdb_ref[...], preferred_element_type=F32)
    abr = abr_ref[...]
    abi = abi_ref[...]
    s0r = s0r_ref[...]
    s0i = s0i_ref[...]
    sr = abr * s0r - abi * s0i + bu[:, :nst]
    si = abr * s0i + abi * s0r + bu[:, nst:]
    sr_ref[...] = sr
    si_ref[...] = si
    scat = jnp.concatenate([sr, si], axis=-1).astype(BF16)
    y_ref[...] = jnp.dot(scat, cm_ref[...], preferred_element_type=F32)
    conv = cw_ref[0:1, :] * b0_ref[...] + cw_ref[1:2, :] * b1_ref[...] + cw_ref[2:3, :] * z[:ns]
    ycn_ref[...] = _rms(gb[:ns] * conv, nconv_ref[...]).astype(BF16)


def _front_kernel(x_ref, nmix_ref, win_ref, zm_ref, cw_ref, nconv_ref,
                  uc_ref, ut_ref, ycn_ref, zt_ref, zbuf, ubuf):
    i = pl.program_id(1)
    nb, tt, d = x_ref.shape
    cw = ycn_ref.shape[2]
    rows = nb * tt
    ncz = tt // CHUNK
    halo = SUBLANE

    @pl.when(i == 0)
    def _():
        zbuf[:, 0:halo, :] = jnp.broadcast_to(zm_ref[...][None], (nb, halo, cw))

    h = _rms(x_ref[...].reshape(rows, d), nmix_ref[...]).astype(BF16)
    u = jnp.dot(h, win_ref[:, 0:cw], preferred_element_type=F32)
    ut_ref[...] = u.astype(BF16).reshape(nb, tt, cw)
    for j in range(cw // LANE):
        ubuf[j] = u[:, j * LANE:(j + 1) * LANE].reshape(nb, tt, LANE)
    for s in range(CHUNK):
        for j in range(cw // LANE):
            piece = ubuf[j, :, pl.ds(s, ncz, stride=CHUNK), :]
            uc_ref[j, :, :, s * LANE:(s + 1) * LANE] = piece.astype(BF16)
    zc = jnp.dot(h, win_ref[:, cw:2 * cw], preferred_element_type=F32)
    gc = jnp.dot(h, win_ref[:, 3 * cw:4 * cw], preferred_element_type=F32)
    z3 = (gc * zc).reshape(nb, tt, cw)
    zbuf[:, halo:halo + tt, :] = z3
    z1 = zbuf[:, halo - 1:halo - 1 + tt, :]
    z2 = zbuf[:, halo - 2:halo - 2 + tt, :]
    conv = cw_ref[0:1, :] * z2 + cw_ref[1:2, :] * z1 + cw_ref[2:3, :] * z3
    gb = jnp.dot(h, win_ref[:, 2 * cw:3 * cw], preferred_element_type=F32)
    yc = gb * conv.reshape(rows, cw)
    ycn_ref[...] = _rms(yc, nconv_ref[...]).astype(BF16).reshape(nb, tt, cw)
    tail = zbuf[:, tt:tt + halo, :]
    zt_ref[...] = tail
    zbuf[:, 0:halo, :] = tail


def _ssm_kernel(u_ref, um_ref, kc_ref, pc_ref, rc_ref, a16_ref, y_ref, sl_ref,
                s_carry, ds_ref, sp_ref, p_s, r_s, t_s):
    th = pl.program_id(1)
    _, nb, cc, w = u_ref.shape
    nst = p_s.shape[1]
    half = nst // 2
    rows = nb * cc
    blk = 2 * LANE
    u = u_ref[0].reshape(rows, w)

    @pl.when(th == 0)
    def _():
        hch = kc_ref.shape[2]
        gpt = LANE // hch
        hlog = hch.bit_length() - 1
        glog = gpt.bit_length() - 1
        nlog = (pc_ref.shape[2] // 2).bit_length() - 1
        p_s[...] = jnp.where(_group_mask(p_s.shape, hlog, nlog, gpt),
                             _expand_cols(pc_ref[0], glog, nlog), 0.0).astype(BF16)
        r_s[...] = jnp.where(_group_mask(r_s.shape, nlog, hlog, gpt),
                             _expand_rows(rc_ref[0], glog, nlog), 0.0).astype(BF16)
        nlag = kc_ref.shape[1] // LANE
        o = _iota2((hch, LANE), 0)
        c = _iota2((hch, LANE), 1)
        spread = jnp.where((c & (hch - 1)) == o, 1.0, 0.0).astype(BF16)
        lagm = jnp.dot(kc_ref[0], spread, preferred_element_type=F32)
        r = _iota2(lagm.shape, 0)
        c = _iota2(lagm.shape, 1)
        lagm = jnp.where(((r >> hlog) & (gpt - 1)) == (c >> hlog), lagm, 0.0).astype(BF16)
        for dlt in range(nlag // 2):
            b0 = lagm[(2 * dlt) * LANE:(2 * dlt + 1) * LANE]
            b1 = lagm[(2 * dlt + 1) * LANE:(2 * dlt + 2) * LANE]
            b2 = lagm[(2 * dlt + 2) * LANE:(2 * dlt + 3) * LANE]
            t_s[dlt, 0:LANE, 0:LANE] = b1
            t_s[dlt, 0:LANE, LANE:blk] = b2
            t_s[dlt, LANE:blk, 0:LANE] = b0
            t_s[dlt, LANE:blk, LANE:blk] = b1
        ds_ref[:, 0:8, :] = jnp.dot(um_ref[0], p_s[...], preferred_element_type=F32).reshape(nb, 8, nst)
        s_carry[...] = ds_ref[:, 0:1, :]

    ds_ref[...] = jnp.dot(u, p_s[...], preferred_element_type=F32).reshape(nb, cc, nst)
    for tb in range(w // blk):
        acc = jnp.dot(u[:, 0:blk], t_s[tb], preferred_element_type=F32)
        for sb in range(1, tb + 1):
            acc = acc + jnp.dot(u[:, sb * blk:(sb + 1) * blk], t_s[tb - sb],
                                preferred_element_type=F32)
        y_ref[0, :, :, tb * blk:(tb + 1) * blk] = acc.reshape(nb, cc, blk)
    ar = a16_ref[0, 0:1, :].reshape(1, 1, half)
    ai = a16_ref[0, 1:2, :].reshape(1, 1, half)
    sr = s_carry[:, :, 0:half]
    si = s_carry[:, :, half:nst]
    for c in range(cc):
        sp_ref[:, c:c + 1, 0:half] = sr
        sp_ref[:, c:c + 1, half:nst] = si
        dr = ds_ref[:, c:c + 1, 0:half]
        di = ds_ref[:, c:c + 1, half:nst]
        sr, si = ar * sr - ai * si + dr, ar * si + ai * sr + di
    s_carry[:, :, 0:half] = sr
    s_carry[:, :, half:nst] = si
    sl_ref[0, :, :, 0:half] = sr
    sl_ref[0, :, :, half:nst] = si

    sp = sp_ref[...].reshape(rows, nst).astype(BF16)
    for tb in range(w // blk):
        acc = jnp.dot(sp, r_s[:, tb * blk:(tb + 1) * blk], preferred_element_type=F32)
        y_ref[0, :, :, tb * blk:(tb + 1) * blk] += acc.reshape(nb, cc, blk)


N_MIX_IN = 12


def _mix_rows(x, yssm, ut, ycn, dsk_ref, wglu_ref, bglu_ref, nssm_ref, wout_ref,
              nffn_ref, wr_ref, br_ref):
    ne = br_ref.shape[0]
    y = _gelu_tanh(yssm + dsk_ref[...] * ut.astype(F32))
    glu = jnp.dot(y.astype(BF16), wglu_ref[...], preferred_element_type=F32) + bglu_ref[...]
    o = y * jax.nn.sigmoid(glu)
    ysn = _rms(o, nssm_ref[...]).astype(BF16)
    mix = jnp.concatenate([ysn, ycn], axis=-1)
    x1 = x + jnp.dot(mix, wout_ref[...], preferred_element_type=F32)
    hf = _rms(x1, nffn_ref[...])
    hf_hi = hf.astype(BF16)
    hf_lo = (hf - hf_hi.astype(F32)).astype(BF16)
    r = hf.shape[0]
    part = jnp.dot(jnp.concatenate([hf_hi, hf_lo], axis=0), wr_ref[...], preferred_element_type=F32)
    logits = (part[0:r, 0:LANE] + part[0:r, LANE:2 * LANE]) + (part[r:2 * r, 0:LANE] + part[r:2 * r, LANE:2 * LANE])
    lt = logits.T[0:ne, :] + br_ref[...]
    iota = lax.broadcasted_iota(I32, lt.shape, 0)
    vals, idxs = [], []
    sel = jnp.zeros(lt.shape, F32)
    for _ in range(TOP_K):
        m = jnp.max(lt, axis=0, keepdims=True)
        ik = jnp.min(jnp.where(lt == m, iota, ne), axis=0, keepdims=True)
        vals.append(m)
        idxs.append(ik)
        hit = iota == ik
        sel = sel + jnp.where(hit, 1.0, 0.0)
        lt = jnp.where(hit, -jnp.inf, lt)
    es = [jnp.exp(v - vals[0]) for v in vals]
    tot = es[0] + es[1] + es[2] + es[3]
    idx = jnp.concatenate(idxs, axis=0)
    gates = jnp.concatenate([e / tot for e in es], axis=0)
    return x1, hf_hi, idx, gates, sel


def _mix_kernel_prompt(*refs):
    x_ref, yc_ref = refs[0], refs[1]
    x1_ref, hf_ref, idx_ref, gate_ref, cnt_ref, ybuf = refs[N_MIX_IN:]
    nj, nb, ncz, _ = yc_ref.shape
    for s in range(CHUNK):
        for j in range(nj):
            ybuf[j, :, pl.ds(s, ncz, stride=CHUNK), :] = yc_ref[j, :, :, s * LANE:(s + 1) * LANE]
    ut_ref, ycn_ref = refs[2], refs[3]
    tt, d = x_ref.shape[1], x_ref.shape[2]
    cw = nj * LANE
    nbc = TOK_TILE // tt
    for t in range(cnt_ref.shape[0]):
        b0, r0 = t * nbc, t * TOK_TILE
        yssm = jnp.concatenate([ybuf[j, b0:b0 + nbc].reshape(TOK_TILE, LANE) for j in range(nj)], axis=-1)
        x1, hf, idx, gates, sel = _mix_rows(
            x_ref[b0:b0 + nbc].reshape(TOK_TILE, d), yssm, ut_ref[b0:b0 + nbc].reshape(TOK_TILE, cw),
            ycn_ref[b0:b0 + nbc].reshape(TOK_TILE, cw), *refs[4:N_MIX_IN])
        x1_ref[r0:r0 + TOK_TILE, :] = x1
        hf_ref[r0:r0 + TOK_TILE, :] = hf
        idx_ref[:, r0:r0 + TOK_TILE] = idx
        gate_ref[:, r0:r0 + TOK_TILE] = gates
        cnt_ref[t] = jnp.sum(sel, axis=1, keepdims=True).astype(I32)


def _mix_kernel_sample(*refs):
    x1, hf, idx, gates, sel = _mix_rows(refs[0][0], refs[1][...], refs[2][0], refs[3][0],
                                        *refs[4:N_MIX_IN])
    x1_ref, hf_ref, idx_ref, gate_ref, cnt_ref = refs[N_MIX_IN + 5:]
    ns = x1.shape[0]
    x1_ref[...] = jnp.zeros(x1_ref.shape, x1_ref.dtype)
    hf_ref[...] = jnp.zeros(hf_ref.shape, hf_ref.dtype)
    idx_ref[...] = jnp.zeros(idx_ref.shape, idx_ref.dtype)
    gate_ref[...] = jnp.zeros(gate_ref.shape, gate_ref.dtype)
    x1_ref[0:ns, :] = x1
    hf_ref[0:ns, :] = hf
    idx_ref[:, 0:ns] = idx
    gate_ref[:, 0:ns] = gates
    cnt_ref[0] = jnp.sum(sel, axis=1, keepdims=True).astype(I32)


class _Table:
    def __init__(self, ref, offset):
        self.ref, self.offset = ref, offset

    def __getitem__(self, k):
        return self.ref[self.offset + k]


def _pack_tables(tables):
    offsets, total = [], 0
    for t in tables:
        offsets.append(total)
        total += t.shape[0]
    return jnp.concatenate(tables), tuple(offsets)


def _split_bf16(x, parts):
    out = []
    for _ in range(parts - 1):
        p = x.astype(BF16)
        out.append(p)
        x = x - p.astype(F32)
    out.append(x.astype(BF16))
    return out


def _run_tables(idx_ref, loff_ref, tile, n_valid, gate_ref=None):
    ne = loff_ref.shape[1]
    tt = idx_ref.shape[1]
    e_iota = lax.broadcasted_iota(I32, (ne, tt), 0)
    tok = tile * tt + lax.broadcasted_iota(I32, (1, tt), 1)
    valid = tok < n_valid
    hits = [jnp.logical_and(e_iota == idx_ref[k:k + 1, :], valid) for k in range(TOP_K)]
    sel = jnp.zeros((ne, tt), F32)
    for h in hits:
        sel = sel + jnp.where(h, 1.0, 0.0)
    before = lax.broadcasted_iota(I32, (tt, tt), 0) < lax.broadcasted_iota(I32, (tt, tt), 1)
    tri = jnp.where(before, 1.0, 0.0).astype(BF16)
    base = jnp.dot(sel.astype(BF16), tri, preferred_element_type=F32) + loff_ref[0].astype(F32)
    base = jnp.where(sel > 0.0, base + 1.0, 0.0)
    b_hi = BF16_EXACT * jnp.floor(base * (1.0 / BF16_EXACT))
    b_lo = base - b_hi
    halves = jnp.concatenate([b_hi, b_lo], axis=0).astype(BF16)
    if gate_ref is None:
        return halves, None
    gate_e = jnp.zeros((ne, tt), F32)
    for k, h in enumerate(hits):
        gate_e = gate_e + jnp.where(h, gate_ref[k:k + 1, :], 0.0)
    return halves, jnp.concatenate(_split_bf16(gate_e, GATE_PIECES), axis=0)


def _rows_onehot(halves, rgn_ref, row0, nrows, pieces=None):
    ne = rgn_ref.shape[2]
    tt = halves.shape[1]

    def own(copies):
        lo = jnp.concatenate([rgn_ref[0, 0:1, :]] * copies, axis=1)
        hi = jnp.concatenate([rgn_ref[0, 1:2, :]] * copies, axis=1)
        r_i = row0 + lax.broadcasted_iota(I32, (nrows, copies * ne), 0)
        return jnp.logical_and(r_i >= lo, r_i < hi)

    want = jnp.dot(jnp.where(own(2), 1.0, 0.0).astype(BF16), halves, preferred_element_type=F32)
    r_f = (row0 + 1 + lax.broadcasted_iota(I32, (nrows, tt), 0)).astype(F32)
    smat = jnp.where(want == r_f, 1.0, 0.0).astype(BF16)
    if pieces is None:
        return smat
    per = lax.dot_general(smat, pieces, (((1,), (1,)), ((), ())), preferred_element_type=F32)
    gcol = jnp.sum(jnp.where(own(GATE_PIECES), per, 0.0), axis=1, keepdims=True)
    return smat, gcol


def _window_copy(buf, slot, hbm, lo, g, sem, to_hbm, rows, align):
    src = buf.at[slot, pl.ds(pl.multiple_of(lo, align), rows)]
    dst = hbm.at[pl.ds(pl.multiple_of(g, align), rows)]
    if to_hbm:
        return pltpu.make_async_copy(src, dst, sem.at[slot])
    return pltpu.make_async_copy(dst, src, sem.at[slot])


def _start_windows(wg_ref, wl_ref, tw_ref, tile, slots, buf, slot, hbm, sem, to_hbm, rows, align):
    def per_window(w, c):
        k = tile * slots + w
        lo = w * rows if wl_ref is None else wl_ref[k]
        _window_copy(buf, slot, hbm, lo, wg_ref[k], sem, to_hbm, rows, align).start()
        return c

    lax.fori_loop(0, tw_ref[tile], per_window, 0)


def _wait_windows(count, buf, slot, hbm, sem, to_hbm, rows, align):
    batch = 8

    def many(w, c):
        _window_copy(buf, slot, hbm, 0, 0, sem, to_hbm, batch * rows, align).wait()
        return c

    def one(w, c):
        _window_copy(buf, slot, hbm, 0, 0, sem, to_hbm, rows, align).wait()
        return c

    lax.fori_loop(0, count // batch, many, 0)
    lax.fori_loop(0, count % batch, one, 0)


def _dispatch_kernel(offsets, tab_ref,
                     hf_ref, idx_ref, gate_ref, loff_ref, rgn_ref, xs_ref, buf, carry, sem):
    wg_ref, wl_ref, mlo_ref, mtg_ref, keep_ref, tw_ref, used_ref, nv_ref = [
        _Table(tab_ref, o) for o in offsets]
    i = pl.program_id(0)
    nt = pl.num_programs(0)
    ne = loff_ref.shape[1]
    tt, d = hf_ref.shape
    nrun = buf.shape[1]
    slot = i % 2

    @pl.when(i == 0)
    def _():
        carry[...] = jnp.zeros(carry.shape, carry.dtype)
        buf[...] = jnp.zeros(buf.shape, buf.dtype)

    halves, pieces = _run_tables(idx_ref, loff_ref, i, nv_ref[0], gate_ref)

    def emit(row0, nrows):
        smat, gcol = _rows_onehot(halves, rgn_ref, row0, nrows, pieces)
        xr = jnp.dot(smat, hf_ref[...], preferred_element_type=F32)
        lane0 = lax.broadcasted_iota(I32, (nrows, GATE_COLS), 1) == 0
        buf[slot, row0:row0 + nrows, 0:d // 2] = _pack_pairs(xr[:, :d // 2], xr[:, d // 2:])
        buf[slot, row0:row0 + nrows, d // 2:] = pltpu.bitcast(jnp.where(lane0, gcol, 0.0), U32)

    emit(0, DISPATCH_BASE_ROWS)
    past_base = used_ref[i] > DISPATCH_BASE_ROWS

    @pl.when(past_base)
    def _():
        emit(DISPATCH_BASE_ROWS, nrun - DISPATCH_BASE_ROWS)

    @pl.when(jnp.logical_not(past_base))
    def _():
        buf[slot, nrun - 2 * SUBLANE:nrun, :] = jnp.zeros((2 * SUBLANE, buf.shape[2]), buf.dtype)

    def merge(e, c):
        k = i * ne + e
        lo = pl.multiple_of(mlo_ref[k], SUBLANE)
        buf[slot, pl.ds(lo, SUBLANE), :] = buf[slot, pl.ds(lo, SUBLANE), :] | carry[e]
        tg = pl.multiple_of(mtg_ref[k], SUBLANE)
        carry[e] = jnp.where(keep_ref[k] > 0, carry[e], buf[slot, pl.ds(tg, SUBLANE), :])
        return c

    lax.fori_loop(0, ne, merge, 0, unroll=4)

    @pl.when(i > 0)
    def _():
        _wait_windows(tw_ref[i - 1], buf, 1 - slot, xs_ref, sem, True, WIN, SUBLANE)

    _start_windows(wg_ref, wl_ref, tw_ref, i, MAX_WINDOWS, buf, slot, xs_ref, sem, True, WIN, SUBLANE)

    @pl.when(i == nt - 1)
    def _():
        _wait_windows(tw_ref[i], buf, slot, xs_ref, sem, True, WIN, SUBLANE)


def _expert_weight_copies(wg_hbm, wd_hbm, wg_f32, wd_f32, sem, e, slot):
    return (pltpu.make_async_copy(wg_hbm.at[e], wg_f32.at[slot], sem.at[0, slot]),
            pltpu.make_async_copy(wd_hbm.at[e], wd_f32.at[slot], sem.at[1, slot]))


def _moe_kernel(offsets, tab_ref,
                x_ref, bg_ref, bd_ref, wg_hbm, wd_hbm, y_ref,
                wg_f32, wd_f32, wg_bf, wd_bf, sem):
    be_ref, bv_ref, first_ref, nxt_ref, slot_ref = [_Table(tab_ref, o) for o in offsets]
    i = pl.program_id(0)
    e = be_ref[i]
    dff = wd_bf.shape[0]
    bm = x_ref.shape[0]
    sub = MOE_SUB
    nw = x_ref.shape[1] - GATE_COLS

    @pl.when(first_ref[i] > 0)
    def _():
        slot = slot_ref[i]

        @pl.when(first_ref[i] > 1)
        def _():
            for cp in _expert_weight_copies(wg_hbm, wd_hbm, wg_f32, wd_f32, sem, e, slot):
                cp.start()

        for cp in _expert_weight_copies(wg_hbm, wd_hbm, wg_f32, wd_f32, sem, e, slot):
            cp.wait()

        @pl.when(nxt_ref[i] >= 0)
        def _():
            for cp in _expert_weight_copies(wg_hbm, wd_hbm, wg_f32, wd_f32, sem, nxt_ref[i], 1 - slot):
                cp.start()

        wg_bf[...] = wg_f32[slot].astype(BF16)
        wd_bf[...] = wd_f32[slot].astype(BF16)

    def rows(r0, nrows):
        sizes = [MOE_CHAIN] * (nrows // MOE_CHAIN) + ([nrows % MOE_CHAIN] if nrows % MOE_CHAIN else [])
        lo = r0
        for size in sizes:
            _chain(lo, size)
            lo += size

    def _chain(lo, n):
        live = lax.broadcasted_iota(I32, (n, 1), 0) + lo < bv_ref[i]
        x = jnp.where(live, _unpack_pairs(x_ref[lo:lo + n, 0:nw]), jnp.zeros((), BF16))
        route = jnp.where(live, pltpu.bitcast(x_ref[lo:lo + n, nw:], F32)[:, 0:1], 0.0)
        gu = jnp.dot(x, wg_bf[...], preferred_element_type=F32) + bg_ref[0]
        gate = jnp.minimum(gu[:, :dff], SWIGLU_LIMIT)
        up = jnp.clip(gu[:, dff:], -SWIGLU_LIMIT, SWIGLU_LIMIT)
        h = gate * jax.nn.sigmoid(SWIGLU_ALPHA * gate) * (up + 1.0)
        y = jnp.dot(h.astype(BF16), wd_bf[...], preferred_element_type=F32) + bd_ref[0]
        yr = (route * y).astype(BF16).astype(F32)
        half = yr.shape[1] // 2
        y_ref[lo:lo + n, :] = _pack_pairs(yr[:, :half], yr[:, half:])

    nchains = bm // sub
    for live_chains in range(nchains + 1):
        lo_rows, hi_rows = (live_chains - 1) * sub, live_chains * sub

        @pl.when(jnp.logical_and(bv_ref[i] > lo_rows, bv_ref[i] <= hi_rows) if live_chains
                 else bv_ref[i] <= 0)
        def _(used=hi_rows):
            if used:
                rows(0, used)
            if used < bm:
                y_ref[used:bm, :] = jnp.zeros((bm - used, y_ref.shape[1]), y_ref.dtype)


def _combine_tile(step, nt):
    return (step + nt - 1) % nt


def _combine_kernel(offsets, tab_ref,
                    x1_ref, idx_ref, loff_ref, rgn_ref, nf_ref, yb_ref,
                    yp_ref, ys_ref, buf, acc, sem):
    wg_ref, tw_ref, nv_ref = [_Table(tab_ref, o) for o in offsets]
    s = pl.program_id(0)
    nt = pl.num_programs(0)
    tt, d = x1_ref.shape
    nrun = buf.shape[1]
    tile = _combine_tile(s, nt)
    slot = s % 2

    @pl.when(s == 0)
    def _():
        buf[...] = jnp.zeros(buf.shape, buf.dtype)
        _start_windows(wg_ref, None, tw_ref, tile, MAX_WINDOWS, buf, slot, yb_ref, sem, False,
                       WIN, SUBLANE)

    @pl.when(s + 1 < nt)
    def _():
        _start_windows(wg_ref, None, tw_ref, _combine_tile(s + 1, nt), MAX_WINDOWS, buf, 1 - slot,
                       yb_ref, sem, False, WIN, SUBLANE)

    halves, _ = _run_tables(idx_ref, loff_ref, tile, nv_ref[0])

    def gathered(row0, nrows):
        smat = _rows_onehot(halves, rgn_ref, row0, nrows)
        yrun = _unpack_pairs(buf[slot, row0:row0 + nrows])
        return lax.dot_general(smat, yrun, (((0,), (0,)), ((), ())), preferred_element_type=F32)

    _wait_windows(tw_ref[tile], buf, slot, yb_ref, sem, False, WIN, SUBLANE)
    acc[...] = gathered(0, COMBINE_BASE_ROWS)
    for row0 in range(COMBINE_BASE_ROWS, nrun, COMBINE_STEP_ROWS):
        @pl.when(WIN * tw_ref[tile] > row0)
        def _(row0=row0):
            acc[...] += gathered(row0, min(COMBINE_STEP_ROWS, nrun - row0))

    out = _rms(x1_ref[...] + acc[...], nf_ref[...])

    @pl.when(s == 0)
    def _():
        ys_ref[...] = out[0:ys_ref.shape[0], :]

    @pl.when(s > 0)
    def _():
        yp_ref[...] = out.reshape(yp_ref.shape)


def _ssm_matrices(a_re, a_im, log_dt, b_re, b_im, c_re, c_im):
    g, n = a_re.shape
    hch = b_re.shape[2]
    gpt = LANE // hch
    nj = g // gpt
    a = lax.complex(a_re, a_im)
    dta = a * jnp.exp(log_dt)[:, None]
    a_bar = jnp.exp(dta)
    bb = ((a_bar - 1.0) / a)[:, :, None] * lax.complex(b_re, b_im)
    cc = lax.complex(c_re, c_im)
    ks = jnp.arange(CHUNK + 1, dtype=F32)
    pw = jnp.exp(dta[None] * ks[:, None, None])
    kk = jnp.real(jnp.einsum('gon,kgn,gni->kgio', cc, pw[:CHUNK], bb))
    kk = jnp.concatenate([jnp.zeros_like(kk[:1]), kk], axis=0)
    kc = kk.reshape(CHUNK + 1, nj, gpt * hch, hch).transpose(1, 0, 2, 3)
    kc = kc.reshape(nj, (CHUNK + 1) * LANE, hch).astype(BF16)
    pwr, pwi = jnp.real(pw), jnp.imag(pw)
    bbr = jnp.real(bb).transpose(0, 2, 1).reshape(1, g * hch, n)
    bbi = jnp.imag(bb).transpose(0, 2, 1).reshape(1, g * hch, n)
    par = jnp.repeat(pwr[CHUNK - 1::-1][:CHUNK], hch, axis=1)
    pai = jnp.repeat(pwi[CHUNK - 1::-1][:CHUNK], hch, axis=1)
    pc = jnp.concatenate([par * bbr - pai * bbi, par * bbi + pai * bbr], axis=-1)
    pc = pc.reshape(CHUNK, nj, gpt * hch, 2 * n).transpose(1, 0, 2, 3)
    pc = pc.reshape(nj, CHUNK * LANE, 2 * n).astype(BF16)
    ccr = jnp.real(cc).transpose(2, 0, 1).reshape(n, 1, g * hch)
    cci = jnp.imag(cc).transpose(2, 0, 1).reshape(n, 1, g * hch)
    qar = jnp.repeat(pwr[1:CHUNK + 1].transpose(2, 0, 1), hch, axis=2)
    qai = jnp.repeat(pwi[1:CHUNK + 1].transpose(2, 0, 1), hch, axis=2)
    rc = jnp.stack([ccr * qar - cci * qai, -(ccr * qai + cci * qar)], axis=0)
    rc = rc.reshape(2 * n, CHUNK, nj, gpt * hch).transpose(2, 0, 1, 3)
    rc = rc.reshape(nj, 2 * n, CHUNK * LANE).astype(BF16)
    a16 = pw[CHUNK].reshape(nj, 1, gpt * n)
    a16 = jnp.concatenate([jnp.real(a16), jnp.imag(a16)], axis=1)
    bc = jnp.stack([jnp.real(bb), jnp.imag(bb)], axis=0).transpose(1, 3, 0, 2)
    bc = bc.reshape(g * hch, 2 * n).astype(BF16)
    c2 = jnp.stack([jnp.real(cc), -jnp.imag(cc)], axis=0).transpose(0, 3, 1, 2)
    c2 = c2.reshape(2 * n, g * hch).astype(BF16)
    abr = jnp.real(a_bar).reshape(1, g * n)
    abi = jnp.imag(a_bar).reshape(1, g * n)
    return kc, pc, rc, a16, bc, c2, abr, abi


def _full(shape):
    return pl.BlockSpec(shape, lambda *_: (0,) * len(shape))


def kernel(x_prompt, x_sample, state_ssm_re, state_ssm_im, state_conv, meta_tokens, norm_mix, w_in,
           ssm_a_re, ssm_a_im, ssm_log_dt, ssm_b_re, ssm_b_im, ssm_c_re, ssm_c_im, ssm_d, w_glu, b_glu,
           conv_w, norm_out_ssm, norm_out_conv, w_out, norm_ffn, w_router, b_router, w_gate_up,
           b_gate_up, w_down, b_down, norm_final):
    nb, seq, d = x_prompt.shape
    ns = x_sample.shape[0]
    depth, _, g, n = state_ssm_re.shape
    assert depth == 1 and x_sample.shape[1] == 1 and meta_tokens.shape[0] == CHUNK
    cw = conv_w.shape[2]
    nj = cw // LANE
    ne = w_router.shape[2]
    dff = w_down.shape[2]
    nst = g * n
    nbt = nb // 2
    tt = 256
    n_chunks = seq // CHUNK
    tp = nb * seq
    tall = tp + ns
    tm = 128
    nbm = nb
    rows_p = nbm * tm
    assert rows_p % TOK_TILE == 0 and TOK_TILE % tm == 0 and ns <= TOK_TILE and d % 2 == 0
    n_tiles = tp // TOK_TILE + 1
    ta = n_tiles * TOK_TILE

    kc, pc, rc, a16, bc, c2, abr, abi = _ssm_matrices(
        ssm_a_re[0], ssm_a_im[0], ssm_log_dt[0], ssm_b_re[0], ssm_b_im[0], ssm_c_re[0], ssm_c_im[0])
    win_bf = w_in[0].astype(BF16)
    nmix = norm_mix[0].reshape(1, d)
    nconv = norm_out_conv[0].reshape(1, cw)
    cwt = conv_w[0]

    xsm = jnp.concatenate([x_sample.reshape(ns, d), meta_tokens], axis=0)
    nsm = ns + CHUNK
    s0r = state_ssm_re[0].reshape(ns, nst)
    s0i = state_ssm_im[0].reshape(ns, nst)
    buf0 = state_conv[0, :, 0, :]
    buf1 = state_conv[0, :, 1, :]
    u_sm, z_sm, y_s, ycn_s, sr_s, si_s = pl.pallas_call(
        _small_front_kernel,
        out_shape=(jax.ShapeDtypeStruct((nsm, cw), F32), jax.ShapeDtypeStruct((nsm, cw), F32),
                   jax.ShapeDtypeStruct((ns, cw), F32), jax.ShapeDtypeStruct((ns, cw), BF16),
                   jax.ShapeDtypeStruct((ns, nst), F32), jax.ShapeDtypeStruct((ns, nst), F32)),
        scratch_shapes=[pltpu.VMEM((cw, 2 * nst), BF16), pltpu.VMEM((2 * nst, cw), BF16)],
        **_params("small_front", None),
    )(xsm, nmix, win_bf, s0r, s0i, buf0, buf1, cwt, bc, c2, abr, abi, nconv)
    u_meta = u_sm[ns:]
    z_meta8 = z_sm[ns + CHUNK - 8:]
    new_conv_s = jnp.stack([buf1, z_sm[:ns]], axis=1)[None]
    new_re_s = sr_s.reshape(1, ns, g, n)
    new_im_s = si_s.reshape(1, ns, g, n)

    wch = CHUNK * LANE
    u4c, u_tok, ycn_p, ztail = pl.pallas_call(
        _front_kernel,
        grid=(nb // nbt, seq // tt),
        in_specs=[pl.BlockSpec((nbt, tt, d), lambda b, i: (b, i, 0)),
                  _full((1, d)), _full((d, 4 * cw)), _full((8, cw)), _full((3, cw)), _full((1, cw))],
        out_specs=(pl.BlockSpec((nj, nbt, tt // CHUNK, wch), lambda b, i: (0, b, i, 0)),
                   pl.BlockSpec((nbt, tt, cw), lambda b, i: (b, i, 0)),
                   pl.BlockSpec((nbt, tt, cw), lambda b, i: (b, i, 0)),
                   pl.BlockSpec((nbt, 8, cw), lambda b, i: (b, 0, 0))),
        out_shape=(jax.ShapeDtypeStruct((nj, nb, n_chunks, wch), BF16),
                   jax.ShapeDtypeStruct((nb, seq, cw), BF16),
                   jax.ShapeDtypeStruct((nb, seq, cw), BF16),
                   jax.ShapeDtypeStruct((nb, 8, cw), F32)),
        scratch_shapes=[pltpu.VMEM((nbt, tt + 8, cw), F32), pltpu.VMEM((nj, nbt, tt, LANE), F32)],
        **_params("front", ("arbitrary", "arbitrary")),
    )(x_prompt, nmix, win_bf, z_meta8, cwt, nconv)
    new_conv_p = ztail[:, 6:8, :][None]

    cc = n_chunks // 2
    um = u_meta.reshape(CHUNK, nj, LANE).transpose(1, 0, 2).reshape(nj, 1, wch)
    um = jnp.broadcast_to(um, (nj, 8 * nb, wch)).astype(BF16)
    gpt = g // nj
    nstj = 2 * gpt * n
    hch = cw // g
    y4c, s_last = pl.pallas_call(
        _ssm_kernel,
        grid=(nj, n_chunks // cc),
        in_specs=[pl.BlockSpec((1, nb, cc, wch), lambda j, t: (j, 0, t, 0)),
                  pl.BlockSpec((1, 8 * nb, wch), lambda j, t: (j, 0, 0)),
                  pl.BlockSpec((1, (CHUNK + 1) * LANE, hch), lambda j, t: (j, 0, 0)),
                  pl.BlockSpec((1, wch, 2 * n), lambda j, t: (j, 0, 0)),
                  pl.BlockSpec((1, 2 * n, wch), lambda j, t: (j, 0, 0)),
                  pl.BlockSpec((1, 2, nstj // 2), lambda j, t: (j, 0, 0))],
        out_specs=(pl.BlockSpec((1, nb, cc, wch), lambda j, t: (j, 0, t, 0)),
                   pl.BlockSpec((1, nb, 1, nstj), lambda j, t: (j, 0, 0, 0))),
        out_shape=(jax.ShapeDtypeStruct((nj, nb, n_chunks, wch), F32),
                   jax.ShapeDtypeStruct((nj, nb, 1, nstj), F32)),
        scratch_shapes=[pltpu.VMEM((nb, 1, nstj), F32), pltpu.VMEM((nb, cc, nstj), F32),
                        pltpu.VMEM((nb, cc, nstj), F32),
                        pltpu.VMEM((wch, nstj), BF16), pltpu.VMEM((nstj, wch), BF16),
                        pltpu.VMEM((CHUNK // 2, 2 * LANE, 2 * LANE), BF16)],
        **_params("ssm", ("parallel", "arbitrary")),
    )(u4c, um, kc, pc, rc, a16)
    sl = s_last.reshape(nj, nb, 2, gpt, n)
    new_re_p = sl[:, :, 0].transpose(1, 0, 2, 3).reshape(1, nb, g, n)
    new_im_p = sl[:, :, 1].transpose(1, 0, 2, 3).reshape(1, nb, g, n)

    dsk = ssm_d[0].reshape(1, cw)
    wglu_bf = w_glu[0].astype(BF16)
    bglu = b_glu[0].reshape(1, cw)
    nssm = norm_out_ssm[0].reshape(1, cw)
    wout_bf = w_out[0].astype(BF16)
    nffn = norm_ffn[0].reshape(1, d)
    wr_pad = jnp.zeros((d, LANE), F32).at[:, :ne].set(w_router[0])
    wr_hi = wr_pad.astype(BF16)
    wr_lo = (wr_pad - wr_hi.astype(F32)).astype(BF16)
    br = b_router[0].reshape(ne, 1)
    wr2 = jnp.concatenate([wr_hi, wr_lo], axis=1)
    mix_w = (dsk, wglu_bf, bglu, nssm, wout_bf, nffn, wr2, br)
    mix_w_specs = [_full((1, cw)), _full((cw, cw)), _full((1, cw)), _full((1, cw)), _full((2 * cw, d)),
                   _full((1, d)), _full((d, 2 * LANE)), _full((ne, 1))]
    assert 4 + len(mix_w) == N_MIX_IN
    mix_out_shape = (jax.ShapeDtypeStruct((ta, d), F32), jax.ShapeDtypeStruct((ta, d), BF16),
                     jax.ShapeDtypeStruct((TOP_K, ta), I32), jax.ShapeDtypeStruct((TOP_K, ta), F32),
                     jax.ShapeDtypeStruct((n_tiles, ne, 1), I32))
    tpm = rows_p // TOK_TILE
    nbg = nb // nbm
    x1_all, hf_all, idx_all, gate_all, cnt = pl.pallas_call(
        _mix_kernel_prompt,
        grid=(seq // tm, nbg),
        in_specs=[pl.BlockSpec((nbm, tm, d), lambda i, b: (b, i, 0)),
                  pl.BlockSpec((nj, nbm, tm // CHUNK, wch), lambda i, b: (0, b, i, 0)),
                  pl.BlockSpec((nbm, tm, cw), lambda i, b: (b, i, 0)),
                  pl.BlockSpec((nbm, tm, cw), lambda i, b: (b, i, 0))] + mix_w_specs,
        out_specs=(pl.BlockSpec((rows_p, d), lambda i, b: (i * nbg + b, 0)),
                   pl.BlockSpec((rows_p, d), lambda i, b: (i * nbg + b, 0)),
                   pl.BlockSpec((TOP_K, rows_p), lambda i, b: (0, i * nbg + b)),
                   pl.BlockSpec((TOP_K, rows_p), lambda i, b: (0, i * nbg + b)),
                   pl.BlockSpec((tpm, ne, 1), lambda i, b: (i * nbg + b, 0, 0))),
        out_shape=mix_out_shape,
        scratch_shapes=[pltpu.VMEM((nj, nbm, tm, LANE), F32)],
        **_params("mix_prompt", ("parallel", "parallel")),
    )(x_prompt, y4c, u_tok, ycn_p, *mix_w)

    last = n_tiles - 1
    any_spec = pl.BlockSpec(memory_space=pl.ANY)
    x1_all, hf_all, idx_all, gate_all, cnt = pl.pallas_call(
        _mix_kernel_sample,
        grid=(1,),
        in_specs=[_full((1, ns, d)), _full((ns, cw)), _full((1, ns, cw)),
                  _full((1, ns, cw))] + mix_w_specs + [any_spec] * 5,
        out_specs=(pl.BlockSpec((TOK_TILE, d), lambda i: (last, 0)),
                   pl.BlockSpec((TOK_TILE, d), lambda i: (last, 0)),
                   pl.BlockSpec((TOP_K, TOK_TILE), lambda i: (0, last)),
                   pl.BlockSpec((TOP_K, TOK_TILE), lambda i: (0, last)),
                   pl.BlockSpec((1, ne, 1), lambda i: (last, 0, 0))),
        out_shape=mix_out_shape,
        input_output_aliases={N_MIX_IN + k: k for k in range(5)},
        **_params("mix_sample", ("arbitrary",)),
    )(x_sample.reshape(1, ns, d), y_s, u_sm[:ns].astype(BF16).reshape(1, ns, cw),
      ycn_s.reshape(1, ns, cw), *mix_w, x1_all, hf_all, idx_all, gate_all, cnt)

    bm = MOE_ROWS
    cnt2 = cnt.reshape(n_tiles, ne)
    before = jnp.cumsum(cnt2, axis=0) - cnt2
    count = jnp.sum(cnt2, axis=0)
    padded = ((count + WIN + bm - 1) // bm) * bm
    pend = jnp.cumsum(padded)
    pstart = pend - padded
    phase = before % SUBLANE
    span = jnp.where(cnt2 > 0, phase + cnt2, 0)
    gstart = (pstart[None, :] + before - phase).astype(I32).reshape(-1)
    nwin = ((span + WIN - 1) // WIN).astype(I32)
    reg8 = ((span + SUBLANE - 1) // SUBLANE) * SUBLANE
    loff = (jnp.cumsum(reg8, axis=1) - reg8).astype(I32)
    tail = jnp.where(span % SUBLANE != 0, loff + (span // SUBLANE) * SUBLANE, -1).astype(I32)
    twin = jnp.sum(nwin, axis=1).astype(I32)
    n_blocks = (ta * TOP_K + ne * (WIN + bm - 1) + bm - 1) // bm
    cap = n_blocks * bm
    blk0 = jnp.arange(n_blocks, dtype=I32) * bm
    blk_e = jnp.minimum(jnp.sum((pend[None, :] <= blk0[:, None]).astype(I32), axis=1), ne - 1)
    e_ar = jnp.arange(ne, dtype=I32)
    blk_hot = blk_e[:, None] == e_ar[None, :]

    def _of_block(per_expert):
        return jnp.sum(jnp.where(blk_hot, per_expert[None, :], 0), axis=1)

    blk_valid = jnp.clip(_of_block(count) - (blk0 - _of_block(pstart)), 0, bm).astype(I32)
    has = count > 0
    later = jnp.logical_and(e_ar[None, :] > e_ar[:, None], has[None, :])
    nxt_e = jnp.min(jnp.where(later, e_ar[None, :], ne), axis=1)
    nxt_e = jnp.where(nxt_e < ne, nxt_e, -1)
    ordinal = jnp.cumsum(has.astype(I32)) - 1
    is_first = jnp.logical_and(blk_valid > 0, blk0 == _of_block(pstart))
    blk_first = jnp.where(is_first, jnp.where(_of_block(ordinal) == 0, 2, 1), 0).astype(I32)
    blk_next = _of_block(nxt_e).astype(I32)
    blk_slot = (_of_block(ordinal) % 2).astype(I32)
    nvalid = jnp.full((1,), tall, I32)
    loff_al = (WIN * (jnp.cumsum(nwin, axis=1) - nwin)).astype(I32)

    def _window_list(nw, slots, rows, first_hbm, first_buf):
        wcum = jnp.cumsum(nw, axis=1)
        wslot = jnp.arange(slots, dtype=I32)
        w_hot = jnp.logical_and(wslot[None, :, None] >= (wcum - nw)[:, None, :],
                                wslot[None, :, None] < wcum[:, None, :])

        def _of_window(per_tile_expert):
            return jnp.sum(jnp.where(w_hot, per_tile_expert[:, None, :], 0), axis=2)

        w_in_run = wslot[None, :] - _of_window(wcum - nw)
        return [(_of_window(f) + rows * w_in_run).astype(I32).reshape(-1) for f in (first_hbm, first_buf)]

    w_hbm, w_buf = _window_list(nwin, MAX_WINDOWS, WIN, gstart.reshape(n_tiles, ne), loff)
    nrun_d = _dispatch_run_rows(ne)
    zero_grp, spare_grp = nrun_d - 2 * SUBLANE, nrun_d - SUBLANE
    m_lo = jnp.where(nwin > 0, loff, spare_grp).astype(I32).reshape(-1)
    m_tg = jnp.where(jnp.logical_and(nwin > 0, tail >= 0), tail, zero_grp).astype(I32).reshape(-1)
    m_keep = (nwin == 0).astype(I32).reshape(-1)
    used_d = jnp.sum(reg8, axis=1).astype(I32)
    tab_d, off_d = _pack_tables((w_hbm, w_buf, m_lo, m_tg, m_keep, twin, used_d, nvalid))
    tab_c, off_c = _pack_tables((w_hbm, twin, nvalid))
    tab_m, off_m = _pack_tables((blk_e, blk_valid, blk_first, blk_next, blk_slot))
    rowoff_d = (loff + phase).astype(I32).reshape(n_tiles, ne, 1)
    rowoff_c = (loff_al + phase).astype(I32).reshape(n_tiles, ne, 1)
    rgn_d = jnp.stack([loff, loff + reg8], axis=1).astype(I32)
    rgn_c = jnp.stack([loff_al, loff_al + WIN * nwin], axis=1).astype(I32)
    xw = d // 2 + GATE_COLS

    xs = pl.pallas_call(
        functools.partial(_dispatch_kernel, off_d),
        grid_spec=pltpu.PrefetchScalarGridSpec(
            num_scalar_prefetch=1,
            grid=(n_tiles,),
            in_specs=[pl.BlockSpec((TOK_TILE, d), lambda i, *_: (i, 0)),
                      pl.BlockSpec((TOP_K, TOK_TILE), lambda i, *_: (0, i)),
                      pl.BlockSpec((TOP_K, TOK_TILE), lambda i, *_: (0, i)),
                      pl.BlockSpec((1, ne, 1), lambda i, *_: (i, 0, 0)),
                      pl.BlockSpec((1, 2, ne), lambda i, *_: (i, 0, 0))],
            out_specs=pl.BlockSpec(memory_space=pl.ANY),
            scratch_shapes=[pltpu.VMEM((2, _dispatch_run_rows(ne), xw), U32),
                            pltpu.VMEM((ne, SUBLANE, xw), U32), pltpu.SemaphoreType.DMA((2,))]),
        out_shape=jax.ShapeDtypeStruct((cap, xw), U32),
        **_params("dispatch", ("arbitrary",)),
    )(tab_d, hf_all, idx_all, gate_all, rowoff_d, rgn_d)

    yb = pl.pallas_call(
        functools.partial(_moe_kernel, off_m),
        grid_spec=pltpu.PrefetchScalarGridSpec(
            num_scalar_prefetch=1,
            grid=(n_blocks,),
            in_specs=[pl.BlockSpec((bm, xw), lambda i, tab: (i, 0)),
                      pl.BlockSpec((1, 1, 2 * dff), lambda i, tab: (tab[off_m[0] + i], 0, 0)),
                      pl.BlockSpec((1, 1, d), lambda i, tab: (tab[off_m[0] + i], 0, 0)),
                      pl.BlockSpec(memory_space=pl.ANY), pl.BlockSpec(memory_space=pl.ANY)],
            out_specs=pl.BlockSpec((bm, d // 2), lambda i, *_: (i, 0)),
            scratch_shapes=[pltpu.VMEM((2, d, 2 * dff), F32), pltpu.VMEM((2, dff, d), F32),
                            pltpu.VMEM((d, 2 * dff), BF16), pltpu.VMEM((dff, d), BF16),
                            pltpu.SemaphoreType.DMA((2, 2))]),
        out_shape=jax.ShapeDtypeStruct((cap, d // 2), U32),
        **_params("moe", ("arbitrary",)),
    )(tab_m, xs, b_gate_up[0].reshape(ne, 1, 2 * dff),
      b_down[0].reshape(ne, 1, d), w_gate_up[0], w_down[0])

    nfin = norm_final.reshape(1, d)
    nbh = TOK_TILE // tm

    def _tile_of(s):
        return (s + n_tiles - 1) % n_tiles

    tiles_per_time = nbg * tpm

    def _yp_index(s, *_):
        t = jnp.maximum(s - 1, 0)
        return (t % tiles_per_time, t // tiles_per_time, 0)

    y_p, y_sm = pl.pallas_call(
        functools.partial(_combine_kernel, off_c),
        grid_spec=pltpu.PrefetchScalarGridSpec(
            num_scalar_prefetch=1,
            grid=(n_tiles,),
            in_specs=[pl.BlockSpec((TOK_TILE, d), lambda s, *_: (_tile_of(s), 0)),
                      pl.BlockSpec((TOP_K, TOK_TILE), lambda s, *_: (0, _tile_of(s))),
                      pl.BlockSpec((1, ne, 1), lambda s, *_: (_tile_of(s), 0, 0)),
                      pl.BlockSpec((1, 2, ne), lambda s, *_: (_tile_of(s), 0, 0)),
                      pl.BlockSpec((1, d), lambda s, *_: (0, 0)),
                      pl.BlockSpec(memory_space=pl.ANY)],
            out_specs=(pl.BlockSpec((nbh, tm, d), _yp_index),
                       pl.BlockSpec((ns, d), lambda s, *_: (0, 0))),
            scratch_shapes=[pltpu.VMEM((2, _combine_run_rows(ne), d // 2), U32),
                            pltpu.VMEM((TOK_TILE, d), F32), pltpu.SemaphoreType.DMA((2,))]),
        out_shape=(jax.ShapeDtypeStruct((nb, seq, d), F32), jax.ShapeDtypeStruct((ns, d), F32)),
        **_params("combine", ("arbitrary",)),
    )(tab_c, x1_all, idx_all, rowoff_c, rgn_c, nfin, yb)

    return (y_p, y_sm.reshape(ns, 1, d), new_re_p, new_im_p, new_conv_p,
            new_re_s, new_im_s, new_conv_s)
```

```python
import functools
import math

import jax
import jax.numpy as jnp
from jax import lax
from jax.experimental import pallas as pl
from jax.experimental.pallas import tpu as pltpu

F32 = jnp.float32
BF16 = jnp.bfloat16
U32 = jnp.uint32
I32 = jnp.int32
EPS = 1e-5
CHUNK = 16
LANE = 128
TOP_K = 4
SWIGLU_LIMIT = 7.0
SWIGLU_ALPHA = 1.702
MOE_ROWS = 1024
MOE_SUB = 256
MOE_CHAIN = 512
TOK_TILE = 256
WIN = 16
SUBLANE = 8
GATE_COLS = LANE


MAX_WINDOWS = 110


def _dispatch_run_rows(ne):
    return TOP_K * TOK_TILE + ne * 2 * (SUBLANE - 1) + WIN


def _combine_run_rows(ne):
    return -(-(TOP_K * TOK_TILE + ne * (SUBLANE - 1 + WIN - 1)) // WIN) * WIN


DISPATCH_BASE_ROWS = TOP_K * TOK_TILE + 256
COMBINE_BASE_ROWS = TOP_K * TOK_TILE + 512
COMBINE_STEP_ROWS = 256
GATE_PIECES = 3
HI_MASK = 0xFFFF0000
BF16_EXACT = 256.0
MIB = 1024 * 1024
VMEM_MIB = {"small_front": 56, "front": 52, "ssm": 56, "mix_prompt": 56, "mix_sample": 32,
            "dispatch": 40, "moe": 58, "combine": 48}


def _rms(x, g):
    return x * lax.rsqrt(jnp.mean(x * x, axis=-1, keepdims=True) + EPS) * g


def _gelu_tanh(x):
    c = math.sqrt(2.0 / math.pi)
    return 0.5 * x * (1.0 + jnp.tanh(c * (x + 0.044715 * (x * x * x))))


def _params(name, sem):
    return dict(name=name, compiler_params=pltpu.CompilerParams(
        dimension_semantics=sem, vmem_limit_bytes=VMEM_MIB[name] * MIB))


def _pack_pairs(a, b):
    return (pltpu.bitcast(a, U32) >> 16) | (pltpu.bitcast(b, U32) & jnp.uint32(HI_MASK))


def _unpack_pairs(w):
    lo = pltpu.bitcast(w << 16, F32)
    hi = pltpu.bitcast(w & jnp.uint32(HI_MASK), F32)
    return jnp.concatenate([lo, hi], axis=-1).astype(BF16)


def _iota2(shape, axis):
    return lax.broadcasted_iota(I32, shape, axis)


def _expand_cols(compact, reps_log2, n_log2):
    q = _iota2((compact.shape[1], 2 << (reps_log2 + n_log2)), 0)
    c = _iota2((compact.shape[1], 2 << (reps_log2 + n_log2)), 1)
    nmask = (1 << n_log2) - 1
    same = jnp.logical_and((q >> n_log2) == (c >> (reps_log2 + n_log2)), (q & nmask) == (c & nmask))
    return jnp.dot(compact, jnp.where(same, 1.0, 0.0).astype(BF16), preferred_element_type=F32)


def _expand_rows(compact, reps_log2, n_log2):
    r = _iota2((2 << (reps_log2 + n_log2), compact.shape[0]), 0)
    q = _iota2((2 << (reps_log2 + n_log2), compact.shape[0]), 1)
    nmask = (1 << n_log2) - 1
    same = jnp.logical_and((r >> (reps_log2 + n_log2)) == (q >> n_log2), (r & nmask) == (q & nmask))
    return jnp.dot(jnp.where(same, 1.0, 0.0).astype(BF16), compact, preferred_element_type=F32)


def _group_mask(shape, row_shift, col_shift, ngroups):
    r = _iota2(shape, 0)
    c = _iota2(shape, 1)
    return ((r >> row_shift) & (ngroups - 1)) == ((c >> col_shift) & (ngroups - 1))


def _small_front_kernel(x_ref, nmix_ref, win_ref, s0r_ref, s0i_ref, b0_ref, b1_ref, cw_ref,
                        bc_ref, cc_ref, abr_ref, abi_ref, nconv_ref,
                        u_ref, z_ref, y_ref, ycn_ref, sr_ref, si_ref, bdb_ref, cm_ref):
    ns, nst = s0r_ref.shape
    cw = u_ref.shape[1]
    n = bc_ref.shape[1] // 2
    nlog = n.bit_length() - 1
    glog = (nst // n).bit_length() - 1
    hlog = (cw >> glog).bit_length() - 1
    bdb_ref[...] = jnp.where(_group_mask(bdb_ref.shape, hlog, nlog, 1 << glog),
                             _expand_cols(bc_ref[...], glog, nlog), 0.0).astype(BF16)
    cm_ref[...] = jnp.where(_group_mask(cm_ref.shape, nlog, hlog, 1 << glog),
                            _expand_rows(cc_ref[...], glog, nlog), 0.0).astype(BF16)
    h = _rms(x_ref[...], nmix_ref[...]).astype(BF16)
    proj = jnp.dot(h, win_ref[...], preferred_element_type=F32)
    u = proj[:, 0:cw]
    zc = proj[:, cw:2 * cw]
    gb = proj[:, 2 * cw:3 * cw]
    gc = proj[:, 3 * cw:4 * cw]
    z = gc * zc
    u_ref[...] = u
    z_ref[...] = z
    bu = jnp.dot(u[:ns].astype(BF16), bdb_ref[...], preferred_element_type=F32)
    abr = abr_ref[...]
    abi = abi_ref[...]
    s0r = s0r_ref[...]
    s0i = s0i_ref[...]
    sr = abr * s0r - abi * s0i + bu[:, :nst]
    si = abr * s0i + abi * s0r + bu[:, nst:]
    sr_ref[...] = sr
    si_ref[...] = si
    scat = jnp.concatenate([sr, si], axis=-1).astype(BF16)
    y_ref[...] = jnp.dot(scat, cm_ref[...], preferred_element_type=F32)
    conv = cw_ref[0:1, :] * b0_ref[...] + cw_ref[1:2, :] * b1_ref[...] + cw_ref[2:3, :] * z[:ns]
    ycn_ref[...] = _rms(gb[:ns] * conv, nconv_ref[...]).astype(BF16)


def _front_kernel(x_ref, nmix_ref, win_ref, zm_ref, cw_ref, nconv_ref,
                  uc_ref, ut_ref, ycn_ref, zt_ref, zbuf, ubuf):
    i = pl.program_id(1)
    nb, tt, d = x_ref.shape
    cw = ycn_ref.shape[2]
    rows = nb * tt
    ncz = tt // CHUNK
    halo = SUBLANE

    @pl.when(i == 0)
    def _():
        zbuf[:, 0:halo, :] = jnp.broadcast_to(zm_ref[...][None], (nb, halo, cw))

    h = _rms(x_ref[...].reshape(rows, d), nmix_ref[...]).astype(BF16)
    u = jnp.dot(h, win_ref[:, 0:cw], preferred_element_type=F32)
    ut_ref[...] = u.astype(BF16).reshape(nb, tt, cw)
    for j in range(cw // LANE):
        ubuf[j] = u[:, j * LANE:(j + 1) * LANE].reshape(nb, tt, LANE)
    for s in range(CHUNK):
        for j in range(cw // LANE):
            piece = ubuf[j, :, pl.ds(s, ncz, stride=CHUNK), :]
            uc_ref[j, :, :, s * LANE:(s + 1) * LANE] = piece.astype(BF16)
    zc = jnp.dot(h, win_ref[:, cw:2 * cw], preferred_element_type=F32)
    gc = jnp.dot(h, win_ref[:, 3 * cw:4 * cw], preferred_element_type=F32)
    z3 = (gc * zc).reshape(nb, tt, cw)
    zbuf[:, halo:halo + tt, :] = z3
    z1 = zbuf[:, halo - 1:halo - 1 + tt, :]
    z2 = zbuf[:, halo - 2:halo - 2 + tt, :]
    conv = cw_ref[0:1, :] * z2 + cw_ref[1:2, :] * z1 + cw_ref[2:3, :] * z3
    gb = jnp.dot(h, win_ref[:, 2 * cw:3 * cw], preferred_element_type=F32)
    yc = gb * conv.reshape(rows, cw)
    ycn_ref[...] = _rms(yc, nconv_ref[...]).astype(BF16).reshape(nb, tt, cw)
    tail = zbuf[:, tt:tt + halo, :]
    zt_ref[...] = tail
    zbuf[:, 0:halo, :] = tail


def _ssm_kernel(u_ref, um_ref, kc_ref, pc_ref, rc_ref, a16_ref, y_ref, sl_ref,
                s_carry, ds_ref, sp_ref, p_s, r_s, t_s):
    th = pl.program_id(1)
    _, nb, cc, w = u_ref.shape
    nst = p_s.shape[1]
    half = nst // 2
    rows = nb * cc
    blk = 2 * LANE
    u = u_ref[0].reshape(rows, w)

    @pl.when(th == 0)
    def _():
        hch = kc_ref.shape[2]
        gpt = LANE // hch
        hlog = hch.bit_length() - 1
        glog = gpt.bit_length() - 1
        nlog = (pc_ref.shape[2] // 2).bit_length() - 1
        p_s[...] = jnp.where(_group_mask(p_s.shape, hlog, nlog, gpt),
                             _expand_cols(pc_ref[0], glog, nlog), 0.0).astype(BF16)
        r_s[...] = jnp.where(_group_mask(r_s.shape, nlog, hlog, gpt),
                             _expand_rows(rc_ref[0], glog, nlog), 0.0).astype(BF16)
        nlag = kc_ref.shape[1] // LANE
        o = _iota2((hch, LANE), 0)
        c = _iota2((hch, LANE), 1)
        spread = jnp.where((c & (hch - 1)) == o, 1.0, 0.0).astype(BF16)
        lagm = jnp.dot(kc_ref[0], spread, preferred_element_type=F32)
        r = _iota2(lagm.shape, 0)
        c = _iota2(lagm.shape, 1)
        lagm = jnp.where(((r >> hlog) & (gpt - 1)) == (c >> hlog), lagm, 0.0).astype(BF16)
        for dlt in range(nlag // 2):
            b0 = lagm[(2 * dlt) * LANE:(2 * dlt + 1) * LANE]
            b1 = lagm[(2 * dlt + 1) * LANE:(2 * dlt + 2) * LANE]
            b2 = lagm[(2 * dlt + 2) * LANE:(2 * dlt + 3) * LANE]
            t_s[dlt, 0:LANE, 0:LANE] = b1
            t_s[dlt, 0:LANE, LANE:blk] = b2
            t_s[dlt, LANE:blk, 0:LANE] = b0
            t_s[dlt, LANE:blk, LANE:blk] = b1
        ds_ref[:, 0:8, :] = jnp.dot(um_ref[0], p_s[...], preferred_element_type=F32).reshape(nb, 8, nst)
        s_carry[...] = ds_ref[:, 0:1, :]

    ds_ref[...] = jnp.dot(u, p_s[...], preferred_element_type=F32).reshape(nb, cc, nst)
    for tb in range(w // blk):
        acc = jnp.dot(u[:, 0:blk], t_s[tb], preferred_element_type=F32)
        for sb in range(1, tb + 1):
            acc = acc + jnp.dot(u[:, sb * blk:(sb + 1) * blk], t_s[tb - sb],
                                preferred_element_type=F32)
        y_ref[0, :, :, tb * blk:(tb + 1) * blk] = acc.reshape(nb, cc, blk)
    ar = a16_ref[0, 0:1, :].reshape(1, 1, half)
    ai = a16_ref[0, 1:2, :].reshape(1, 1, half)
    sr = s_carry[:, :, 0:half]
    si = s_carry[:, :, half:nst]
    for c in range(cc):
        sp_ref[:, c:c + 1, 0:half] = sr
        sp_ref[:, c:c + 1, half:nst] = si
        dr = ds_ref[:, c:c + 1, 0:half]
        di = ds_ref[:, c:c + 1, half:nst]
        sr, si = ar * sr - ai * si + dr, ar * si + ai * sr + di
    s_carry[:, :, 0:half] = sr
    s_carry[:, :, half:nst] = si
    sl_ref[0, :, :, 0:half] = sr
    sl_ref[0, :, :, half:nst] = si

    sp = sp_ref[...].reshape(rows, nst).astype(BF16)
    for tb in range(w // blk):
        acc = jnp.dot(sp, r_s[:, tb * blk:(tb + 1) * blk], preferred_element_type=F32)
        y_ref[0, :, :, tb * blk:(tb + 1) * blk] += acc.reshape(nb, cc, blk)


N_MIX_IN = 12


def _mix_rows(x, yssm, ut, ycn, dsk_ref, wglu_ref, bglu_ref, nssm_ref, wout_ref,
              nffn_ref, wr_ref, br_ref):
    ne = br_ref.shape[0]
    y = _gelu_tanh(yssm + dsk_ref[...] * ut.astype(F32))
    glu = jnp.dot(y.astype(BF16), wglu_ref[...], preferred_element_type=F32) + bglu_ref[...]
    o = y * jax.nn.sigmoid(glu)
    ysn = _rms(o, nssm_ref[...]).astype(BF16)
    mix = jnp.concatenate([ysn, ycn], axis=-1)
    x1 = x + jnp.dot(mix, wout_ref[...], preferred_element_type=F32)
    hf = _rms(x1, nffn_ref[...])
    hf_hi = hf.astype(BF16)
    hf_lo = (hf - hf_hi.astype(F32)).astype(BF16)
    r = hf.shape[0]
    part = jnp.dot(jnp.concatenate([hf_hi, hf_lo], axis=0), wr_ref[...], preferred_element_type=F32)
    logits = (part[0:r, 0:LANE] + part[0:r, LANE:2 * LANE]) + (part[r:2 * r, 0:LANE] + part[r:2 * r, LANE:2 * LANE])
    lt = logits.T[0:ne, :] + br_ref[...]
    iota = lax.broadcasted_iota(I32, lt.shape, 0)
    vals, idxs = [], []
    sel = jnp.zeros(lt.shape, F32)
    for _ in range(TOP_K):
        m = jnp.max(lt, axis=0, keepdims=True)
        ik = jnp.min(jnp.where(lt == m, iota, ne), axis=0, keepdims=True)
        vals.append(m)
        idxs.append(ik)
        hit = iota == ik
        sel = sel + jnp.where(hit, 1.0, 0.0)
        lt = jnp.where(hit, -jnp.inf, lt)
    es = [jnp.exp(v - vals[0]) for v in vals]
    tot = es[0] + es[1] + es[2] + es[3]
    idx = jnp.concatenate(idxs, axis=0)
    gates = jnp.concatenate([e / tot for e in es], axis=0)
    return x1, hf_hi, idx, gates, sel


def _mix_kernel_prompt(*refs):
    x_ref, yc_ref = refs[0], refs[1]
    x1_ref, hf_ref, idx_ref, gate_ref, cnt_ref, ybuf = refs[N_MIX_IN:]
    nj, nb, ncz, _ = yc_ref.shape
    for s in range(CHUNK):
        for j in range(nj):
            ybuf[j, :, pl.ds(s, ncz, stride=CHUNK), :] = yc_ref[j, :, :, s * LANE:(s + 1) * LANE]
    ut_ref, ycn_ref = refs[2], refs[3]
    tt, d = x_ref.shape[1], x_ref.shape[2]
    cw = nj * LANE
    nbc = TOK_TILE // tt
    for t in range(cnt_ref.shape[0]):
        b0, r0 = t * nbc, t * TOK_TILE
        yssm = jnp.concatenate([ybuf[j, b0:b0 + nbc].reshape(TOK_TILE, LANE) for j in range(nj)], axis=-1)
        x1, hf, idx, gates, sel = _mix_rows(
            x_ref[b0:b0 + nbc].reshape(TOK_TILE, d), yssm, ut_ref[b0:b0 + nbc].reshape(TOK_TILE, cw),
            ycn_ref[b0:b0 + nbc].reshape(TOK_TILE, cw), *refs[4:N_MIX_IN])
        x1_ref[r0:r0 + TOK_TILE, :] = x1
        hf_ref[r0:r0 + TOK_TILE, :] = hf
        idx_ref[:, r0:r0 + TOK_TILE] = idx
        gate_ref[:, r0:r0 + TOK_TILE] = gates
        cnt_ref[t] = jnp.sum(sel, axis=1, keepdims=True).astype(I32)


def _mix_kernel_sample(*refs):
    x1, hf, idx, gates, sel = _mix_rows(refs[0][0], refs[1][...], refs[2][0], refs[3][0],
                                        *refs[4:N_MIX_IN])
    x1_ref, hf_ref, idx_ref, gate_ref, cnt_ref = refs[N_MIX_IN + 5:]
    ns = x1.shape[0]
    x1_ref[...] = jnp.zeros(x1_ref.shape, x1_ref.dtype)
    hf_ref[...] = jnp.zeros(hf_ref.shape, hf_ref.dtype)
    idx_ref[...] = jnp.zeros(idx_ref.shape, idx_ref.dtype)
    gate_ref[...] = jnp.zeros(gate_ref.shape, gate_ref.dtype)
    x1_ref[0:ns, :] = x1
    hf_ref[0:ns, :] = hf
    idx_ref[:, 0:ns] = idx
    gate_ref[:, 0:ns] = gates
    cnt_ref[0] = jnp.sum(sel, axis=1, keepdims=True).astype(I32)


class _Table:
    def __init__(self, ref, offset):
        self.ref, self.offset = ref, offset

    def __getitem__(self, k):
        return self.ref[self.offset + k]


def _pack_tables(tables):
    offsets, total = [], 0
    for t in tables:
        offsets.append(total)
        total += t.shape[0]
    return jnp.concatenate(tables), tuple(offsets)


def _split_bf16(x, parts):
    out = []
    for _ in range(parts - 1):
        p = x.astype(BF16)
        out.append(p)
        x = x - p.astype(F32)
    out.append(x.astype(BF16))
    return out


def _run_tables(idx_ref, loff_ref, tile, n_valid, gate_ref=None):
    ne = loff_ref.shape[1]
    tt = idx_ref.shape[1]
    e_iota = lax.broadcasted_iota(I32, (ne, tt), 0)
    tok = tile * tt + lax.broadcasted_iota(I32, (1, tt), 1)
    valid = tok < n_valid
    hits = [jnp.logical_and(e_iota == idx_ref[k:k + 1, :], valid) for k in range(TOP_K)]
    sel = jnp.zeros((ne, tt), F32)
    for h in hits:
        sel = sel + jnp.where(h, 1.0, 0.0)
    before = lax.broadcasted_iota(I32, (tt, tt), 0) < lax.broadcasted_iota(I32, (tt, tt), 1)
    tri = jnp.where(before, 1.0, 0.0).astype(BF16)
    base = jnp.dot(sel.astype(BF16), tri, preferred_element_type=F32) + loff_ref[0].astype(F32)
    base = jnp.where(sel > 0.0, base + 1.0, 0.0)
    b_hi = BF16_EXACT * jnp.floor(base * (1.0 / BF16_EXACT))
    b_lo = base - b_hi
    halves = jnp.concatenate([b_hi, b_lo], axis=0).astype(BF16)
    if gate_ref is None:
        return halves, None
    gate_e = jnp.zeros((ne, tt), F32)
    for k, h in enumerate(hits):
        gate_e = gate_e + jnp.where(h, gate_ref[k:k + 1, :], 0.0)
    return halves, jnp.concatenate(_split_bf16(gate_e, GATE_PIECES), axis=0)


def _rows_onehot(halves, rgn_ref, row0, nrows, pieces=None):
    ne = rgn_ref.shape[2]
    tt = halves.shape[1]

    def own(copies):
        lo = jnp.concatenate([rgn_ref[0, 0:1, :]] * copies, axis=1)
        hi = jnp.concatenate([rgn_ref[0, 1:2, :]] * copies, axis=1)
        r_i = row0 + lax.broadcasted_iota(I32, (nrows, copies * ne), 0)
        return jnp.logical_and(r_i >= lo, r_i < hi)

    want = jnp.dot(jnp.where(own(2), 1.0, 0.0).astype(BF16), halves, preferred_element_type=F32)
    r_f = (row0 + 1 + lax.broadcasted_iota(I32, (nrows, tt), 0)).astype(F32)
    smat = jnp.where(want == r_f, 1.0, 0.0).astype(BF16)
    if pieces is None:
        return smat
    per = lax.dot_general(smat, pieces, (((1,), (1,)), ((), ())), preferred_element_type=F32)
    gcol = jnp.sum(jnp.where(own(GATE_PIECES), per, 0.0), axis=1, keepdims=True)
    return smat, gcol


def _window_copy(buf, slot, hbm, lo, g, sem, to_hbm, rows, align):
    src = buf.at[slot, pl.ds(pl.multiple_of(lo, align), rows)]
    dst = hbm.at[pl.ds(pl.multiple_of(g, align), rows)]
    if to_hbm:
        return pltpu.make_async_copy(src, dst, sem.at[slot])
    return pltpu.make_async_copy(dst, src, sem.at[slot])


def _start_windows(wg_ref, wl_ref, tw_ref, tile, slots, buf, slot, hbm, sem, to_hbm, rows, align):
    def per_window(w, c):
        k = tile * slots + w
        lo = w * rows if wl_ref is None else wl_ref[k]
        _window_copy(buf, slot, hbm, lo, wg_ref[k], sem, to_hbm, rows, align).start()
        return c

    lax.fori_loop(0, tw_ref[tile], per_window, 0)


def _wait_windows(count, buf, slot, hbm, sem, to_hbm, rows, align):
    batch = 8

    def many(w, c):
        _window_copy(buf, slot, hbm, 0, 0, sem, to_hbm, batch * rows, align).wait()
        return c

    def one(w, c):
        _window_copy(buf, slot, hbm, 0, 0, sem, to_hbm, rows, align).wait()
        return c

    lax.fori_loop(0, count // batch, many, 0)
    lax.fori_loop(0, count % batch, one, 0)


def _dispatch_kernel(offsets, tab_ref,
                     hf_ref, idx_ref, gate_ref, loff_ref, rgn_ref, xs_ref, buf, carry, sem):
    wg_ref, wl_ref, mlo_ref, mtg_ref, keep_ref, tw_ref, used_ref, nv_ref = [
        _Table(tab_ref, o) for o in offsets]
    i = pl.program_id(0)
    nt = pl.num_programs(0)
    ne = loff_ref.shape[1]
    tt, d = hf_ref.shape
    nrun = buf.shape[1]
    slot = i % 2

    @pl.when(i == 0)
    def _():
        carry[...] = jnp.zeros(carry.shape, carry.dtype)
        buf[...] = jnp.zeros(buf.shape, buf.dtype)

    halves, pieces = _run_tables(idx_ref, loff_ref, i, nv_ref[0], gate_ref)

    def emit(row0, nrows):
        smat, gcol = _rows_onehot(halves, rgn_ref, row0, nrows, pieces)
        xr = jnp.dot(smat, hf_ref[...], preferred_element_type=F32)
        lane0 = lax.broadcasted_iota(I32, (nrows, GATE_COLS), 1) == 0
        buf[slot, row0:row0 + nrows, 0:d // 2] = _pack_pairs(xr[:, :d // 2], xr[:, d // 2:])
        buf[slot, row0:row0 + nrows, d // 2:] = pltpu.bitcast(jnp.where(lane0, gcol, 0.0), U32)

    emit(0, DISPATCH_BASE_ROWS)
    past_base = used_ref[i] > DISPATCH_BASE_ROWS

    @pl.when(past_base)
    def _():
        emit(DISPATCH_BASE_ROWS, nrun - DISPATCH_BASE_ROWS)

    @pl.when(jnp.logical_not(past_base))
    def _():
        buf[slot, nrun - 2 * SUBLANE:nrun, :] = jnp.zeros((2 * SUBLANE, buf.shape[2]), buf.dtype)

    def merge(e, c):
        k = i * ne + e
        lo = pl.multiple_of(mlo_ref[k], SUBLANE)
        buf[slot, pl.ds(lo, SUBLANE), :] = buf[slot, pl.ds(lo, SUBLANE), :] | carry[e]
        tg = pl.multiple_of(mtg_ref[k], SUBLANE)
        carry[e] = jnp.where(keep_ref[k] > 0, carry[e], buf[slot, pl.ds(tg, SUBLANE), :])
        return c

    lax.fori_loop(0, ne, merge, 0, unroll=4)

    @pl.when(i > 0)
    def _():
        _wait_windows(tw_ref[i - 1], buf, 1 - slot, xs_ref, sem, True, WIN, SUBLANE)

    _start_windows(wg_ref, wl_ref, tw_ref, i, MAX_WINDOWS, buf, slot, xs_ref, sem, True, WIN, SUBLANE)

    @pl.when(i == nt - 1)
    def _():
        _wait_windows(tw_ref[i], buf, slot, xs_ref, sem, True, WIN, SUBLANE)


def _expert_weight_copies(wg_hbm, wd_hbm, wg_f32, wd_f32, sem, e, slot):
    return (pltpu.make_async_copy(wg_hbm.at[e], wg_f32.at[slot], sem.at[0, slot]),
            pltpu.make_async_copy(wd_hbm.at[e], wd_f32.at[slot], sem.at[1, slot]))


def _moe_kernel(offsets, tab_ref,
                x_ref, bg_ref, bd_ref, wg_hbm, wd_hbm, y_ref,
                wg_f32, wd_f32, wg_bf, wd_bf, sem):
    be_ref, bv_ref, first_ref, nxt_ref, slot_ref = [_Table(tab_ref, o) for o in offsets]
    i = pl.program_id(0)
    e = be_ref[i]
    dff = wd_bf.shape[0]
    bm = x_ref.shape[0]
    sub = MOE_SUB
    nw = x_ref.shape[1] - GATE_COLS

    @pl.when(first_ref[i] > 0)
    def _():
        slot = slot_ref[i]

        @pl.when(first_ref[i] > 1)
        def _():
            for cp in _expert_weight_copies(wg_hbm, wd_hbm, wg_f32, wd_f32, sem, e, slot):
                cp.start()

        for cp in _expert_weight_copies(wg_hbm, wd_hbm, wg_f32, wd_f32, sem, e, slot):
            cp.wait()

        @pl.when(nxt_ref[i] >= 0)
        def _():
            for cp in _expert_weight_copies(wg_hbm, wd_hbm, wg_f32, wd_f32, sem, nxt_ref[i], 1 - slot):
                cp.start()

        wg_bf[...] = wg_f32[slot].astype(BF16)
        wd_bf[...] = wd_f32[slot].astype(BF16)

    def rows(r0, nrows):
        sizes = [MOE_CHAIN] * (nrows // MOE_CHAIN) + ([nrows % MOE_CHAIN] if nrows % MOE_CHAIN else [])
        lo = r0
        for size in sizes:
            _chain(lo, size)
            lo += size

    def _chain(lo, n):
        live = lax.broadcasted_iota(I32, (n, 1), 0) + lo < bv_ref[i]
        x = jnp.where(live, _unpack_pairs(x_ref[lo:lo + n, 0:nw]), jnp.zeros((), BF16))
        route = jnp.where(live, pltpu.bitcast(x_ref[lo:lo + n, nw:], F32)[:, 0:1], 0.0)
        gu = jnp.dot(x, wg_bf[...], preferred_element_type=F32) + bg_ref[0]
        gate = jnp.minimum(gu[:, :dff], SWIGLU_LIMIT)
        up = jnp.clip(gu[:, dff:], -SWIGLU_LIMIT, SWIGLU_LIMIT)
        h = gate * jax.nn.sigmoid(SWIGLU_ALPHA * gate) * (up + 1.0)
        y = jnp.dot(h.astype(BF16), wd_bf[...], preferred_element_type=F32) + bd_ref[0]
        yr = (route * y).astype(BF16).astype(F32)
        half = yr.shape[1] // 2
        y_ref[lo:lo + n, :] = _pack_pairs(yr[:, :half], yr[:, half:])

    nchains = bm // sub
    for live_chains in range(nchains + 1):
        lo_rows, hi_rows = (live_chains - 1) * sub, live_chains * sub

        @pl.when(jnp.logical_and(bv_ref[i] > lo_rows, bv_ref[i] <= hi_rows) if live_chains
                 else bv_ref[i] <= 0)
        def _(used=hi_rows):
            if used:
                rows(0, used)
            if used < bm:
                y_ref[used:bm, :] = jnp.zeros((bm - used, y_ref.shape[1]), y_ref.dtype)


def _combine_tile(step, nt):
    return (step + nt - 1) % nt


def _combine_kernel(offsets, tab_ref,
                    x1_ref, idx_ref, loff_ref, rgn_ref, nf_ref, yb_ref,
                    yp_ref, ys_ref, buf, acc, sem):
    wg_ref, tw_ref, nv_ref = [_Table(tab_ref, o) for o in offsets]
    s = pl.program_id(0)
    nt = pl.num_programs(0)
    tt, d = x1_ref.shape
    nrun = buf.shape[1]
    tile = _combine_tile(s, nt)
    slot = s % 2

    @pl.when(s == 0)
    def _():
        buf[...] = jnp.zeros(buf.shape, buf.dtype)
        _start_windows(wg_ref, None, tw_ref, tile, MAX_WINDOWS, buf, slot, yb_ref, sem, False,
                       WIN, SUBLANE)

    @pl.when(s + 1 < nt)
    def _():
        _start_windows(wg_ref, None, tw_ref, _combine_tile(s + 1, nt), MAX_WINDOWS, buf, 1 - slot,
                       yb_ref, sem, False, WIN, SUBLANE)

    halves, _ = _run_tables(idx_ref, loff_ref, tile, nv_ref[0])

    def gathered(row0, nrows):
        smat = _rows_onehot(halves, rgn_ref, row0, nrows)
        yrun = _unpack_pairs(buf[slot, row0:row0 + nrows])
        return lax.dot_general(smat, yrun, (((0,), (0,)), ((), ())), preferred_element_type=F32)

    _wait_windows(tw_ref[tile], buf, slot, yb_ref, sem, False, WIN, SUBLANE)
    acc[...] = gathered(0, COMBINE_BASE_ROWS)
    for row0 in range(COMBINE_BASE_ROWS, nrun, COMBINE_STEP_ROWS):
        @pl.when(WIN * tw_ref[tile] > row0)
        def _(row0=row0):
            acc[...] += gathered(row0, min(COMBINE_STEP_ROWS, nrun - row0))

    out = _rms(x1_ref[...] + acc[...], nf_ref[...])

    @pl.when(s == 0)
    def _():
        ys_ref[...] = out[0:ys_ref.shape[0], :]

    @pl.when(s > 0)
    def _():
        yp_ref[...] = out.reshape(yp_ref.shape)


def _ssm_matrices(a_re, a_im, log_dt, b_re, b_im, c_re, c_im):
    g, n = a_re.shape
    hch = b_re.shape[2]
    gpt = LANE // hch
    nj = g // gpt
    a = lax.complex(a_re, a_im)
    dta = a * jnp.exp(log_dt)[:, None]
    a_bar = jnp.exp(dta)
    bb = ((a_bar - 1.0) / a)[:, :, None] * lax.complex(b_re, b_im)
    cc = lax.complex(c_re, c_im)
    ks = jnp.arange(CHUNK + 1, dtype=F32)
    pw = jnp.exp(dta[None] * ks[:, None, None])
    kk = jnp.real(jnp.einsum('gon,kgn,gni->kgio', cc, pw[:CHUNK], bb))
    kk = jnp.concatenate([jnp.zeros_like(kk[:1]), kk], axis=0)
    kc = kk.reshape(CHUNK + 1, nj, gpt * hch, hch).transpose(1, 0, 2, 3)
    kc = kc.reshape(nj, (CHUNK + 1) * LANE, hch).astype(BF16)
    pwr, pwi = jnp.real(pw), jnp.imag(pw)
    bbr = jnp.real(bb).transpose(0, 2, 1).reshape(1, g * hch, n)
    bbi = jnp.imag(bb).transpose(0, 2, 1).reshape(1, g * hch, n)
    par = jnp.repeat(pwr[CHUNK - 1::-1][:CHUNK], hch, axis=1)
    pai = jnp.repeat(pwi[CHUNK - 1::-1][:CHUNK], hch, axis=1)
    pc = jnp.concatenate([par * bbr - pai * bbi, par * bbi + pai * bbr], axis=-1)
    pc = pc.reshape(CHUNK, nj, gpt * hch, 2 * n).transpose(1, 0, 2, 3)
    pc = pc.reshape(nj, CHUNK * LANE, 2 * n).astype(BF16)
    ccr = jnp.real(cc).transpose(2, 0, 1).reshape(n, 1, g * hch)
    cci = jnp.imag(cc).transpose(2, 0, 1).reshape(n, 1, g * hch)
    qar = jnp.repeat(pwr[1:CHUNK + 1].transpose(2, 0, 1), hch, axis=2)
    qai = jnp.repeat(pwi[1:CHUNK + 1].transpose(2, 0, 1), hch, axis=2)
    rc = jnp.stack([ccr * qar - cci * qai, -(ccr * qai + cci * qar)], axis=0)
    rc = rc.reshape(2 * n, CHUNK, nj, gpt * hch).transpose(2, 0, 1, 3)
    rc = rc.reshape(nj, 2 * n, CHUNK * LANE).astype(BF16)
    a16 = pw[CHUNK].reshape(nj, 1, gpt * n)
    a16 = jnp.concatenate([jnp.real(a16), jnp.imag(a16)], axis=1)
    bc = jnp.stack([jnp.real(bb), jnp.imag(bb)], axis=0).transpose(1, 3, 0, 2)
    bc = bc.reshape(g * hch, 2 * n).astype(BF16)
    c2 = jnp.stack([jnp.real(cc), -jnp.imag(cc)], axis=0).transpose(0, 3, 1, 2)
    c2 = c2.reshape(2 * n, g * hch).astype(BF16)
    abr = jnp.real(a_bar).reshape(1, g * n)
    abi = jnp.imag(a_bar).reshape(1, g * n)
    return kc, pc, rc, a16, bc, c2, abr, abi


def _full(shape):
    return pl.BlockSpec(shape, lambda *_: (0,) * len(shape))


def kernel(x_prompt, x_sample, state_ssm_re, state_ssm_im, state_conv, meta_tokens, norm_mix, w_in,
           ssm_a_re, ssm_a_im, ssm_log_dt, ssm_b_re, ssm_b_im, ssm_c_re, ssm_c_im, ssm_d, w_glu, b_glu,
           conv_w, norm_out_ssm, norm_out_conv, w_out, norm_ffn, w_router, b_router, w_gate_up,
           b_gate_up, w_down, b_down, norm_final):
    nb, seq, d = x_prompt.shape
    ns = x_sample.shape[0]
    depth, _, g, n = state_ssm_re.shape
    assert depth == 1 and x_sample.shape[1] == 1 and meta_tokens.shape[0] == CHUNK
    cw = conv_w.shape[2]
    nj = cw // LANE
    ne = w_router.shape[2]
    dff = w_down.shape[2]
    nst = g * n
    nbt = nb // 2
    tt = 256
    n_chunks = seq // CHUNK
    tp = nb * seq
    tall = tp + ns
    tm = 128
    nbm = nb
    rows_p = nbm * tm
    assert rows_p % TOK_TILE == 0 and TOK_TILE % tm == 0 and ns <= TOK_TILE and d % 2 == 0
    n_tiles = tp // TOK_TILE + 1
    ta = n_tiles * TOK_TILE

    kc, pc, rc, a16, bc, c2, abr, abi = _ssm_matrices(
        ssm_a_re[0], ssm_a_im[0], ssm_log_dt[0], ssm_b_re[0], ssm_b_im[0], ssm_c_re[0], ssm_c_im[0])
    win_bf = w_in[0].astype(BF16)
    nmix = norm_mix[0].reshape(1, d)
    nconv = norm_out_conv[0].reshape(1, cw)
    cwt = conv_w[0]

    xsm = jnp.concatenate([x_sample.reshape(ns, d), meta_tokens], axis=0)
    nsm = ns + CHUNK
    s0r = state_ssm_re[0].reshape(ns, nst)
    s0i = state_ssm_im[0].reshape(ns, nst)
    buf0 = state_conv[0, :, 0, :]
    buf1 = state_conv[0, :, 1, :]
    u_sm, z_sm, y_s, ycn_s, sr_s, si_s = pl.pallas_call(
        _small_front_kernel,
        out_shape=(jax.ShapeDtypeStruct((nsm, cw), F32), jax.ShapeDtypeStruct((nsm, cw), F32),
                   jax.ShapeDtypeStruct((ns, cw), F32), jax.ShapeDtypeStruct((ns, cw), BF16),
                   jax.ShapeDtypeStruct((ns, nst), F32), jax.ShapeDtypeStruct((ns, nst), F32)),
        scratch_shapes=[pltpu.VMEM((cw, 2 * nst), BF16), pltpu.VMEM((2 * nst, cw), BF16)],
        **_params("small_front", None),
    )(xsm, nmix, win_bf, s0r, s0i, buf0, buf1, cwt, bc, c2, abr, abi, nconv)
    u_meta = u_sm[ns:]
    z_meta8 = z_sm[ns + CHUNK - 8:]
    new_conv_s = jnp.stack([buf1, z_sm[:ns]], axis=1)[None]
    new_re_s = sr_s.reshape(1, ns, g, n)
    new_im_s = si_s.reshape(1, ns, g, n)

    wch = CHUNK * LANE
    u4c, u_tok, ycn_p, ztail = pl.pallas_call(
        _front_kernel,
        grid=(nb // nbt, seq // tt),
        in_specs=[pl.BlockSpec((nbt, tt, d), lambda b, i: (b, i, 0)),
                  _full((1, d)), _full((d, 4 * cw)), _full((8, cw)), _full((3, cw)), _full((1, cw))],
        out_specs=(pl.BlockSpec((nj, nbt, tt // CHUNK, wch), lambda b, i: (0, b, i, 0)),
                   pl.BlockSpec((nbt, tt, cw), lambda b, i: (b, i, 0)),
                   pl.BlockSpec((nbt, tt, cw), lambda b, i: (b, i, 0)),
                   pl.BlockSpec((nbt, 8, cw), lambda b, i: (b, 0, 0))),
        out_shape=(jax.ShapeDtypeStruct((nj, nb, n_chunks, wch), BF16),
                   jax.ShapeDtypeStruct((nb, seq, cw), BF16),
                   jax.ShapeDtypeStruct((nb, seq, cw), BF16),
                   jax.ShapeDtypeStruct((nb, 8, cw), F32)),
        scratch_shapes=[pltpu.VMEM((nbt, tt + 8, cw), F32), pltpu.VMEM((nj, nbt, tt, LANE), F32)],
        **_params("front", ("arbitrary", "arbitrary")),
    )(x_prompt, nmix, win_bf, z_meta8, cwt, nconv)
    new_conv_p = ztail[:, 6:8, :][None]

    cc = n_chunks // 2
    um = u_meta.reshape(CHUNK, nj, LANE).transpose(1, 0, 2).reshape(nj, 1, wch)
    um = jnp.broadcast_to(um, (nj, 8 * nb, wch)).astype(BF16)
    gpt = g // nj
    nstj = 2 * gpt * n
    hch = cw // g
    y4c, s_last = pl.pallas_call(
        _ssm_kernel,
        grid=(nj, n_chunks // cc),
        in_specs=[pl.BlockSpec((1, nb, cc, wch), lambda j, t: (j, 0, t, 0)),
                  pl.BlockSpec((1, 8 * nb, wch), lambda j, t: (j, 0, 0)),
                  pl.BlockSpec((1, (CHUNK + 1) * LANE, hch), lambda j, t: (j, 0, 0)),
                  pl.BlockSpec((1, wch, 2 * n), lambda j, t: (j, 0, 0)),
                  pl.BlockSpec((1, 2 * n, wch), lambda j, t: (j, 0, 0)),
                  pl.BlockSpec((1, 2, nstj // 2), lambda j, t: (j, 0, 0))],
        out_specs=(pl.BlockSpec((1, nb, cc, wch), lambda j, t: (j, 0, t, 0)),
                   pl.BlockSpec((1, nb, 1, nstj), lambda j, t: (j, 0, 0, 0))),
        out_shape=(jax.ShapeDtypeStruct((nj, nb, n_chunks, wch), F32),
                   jax.ShapeDtypeStruct((nj, nb, 1, nstj), F32)),
        scratch_shapes=[pltpu.VMEM((nb, 1, nstj), F32), pltpu.VMEM((nb, cc, nstj), F32),
                        pltpu.VMEM((nb, cc, nstj), F32),
                        pltpu.VMEM((wch, nstj), BF16), pltpu.VMEM((nstj, wch), BF16),
                        pltpu.VMEM((CHUNK // 2, 2 * LANE, 2 * LANE), BF16)],
        **_params("ssm", ("parallel", "arbitrary")),
    )(u4c, um, kc, pc, rc, a16)
    sl = s_last.reshape(nj, nb, 2, gpt, n)
    new_re_p = sl[:, :, 0].transpose(1, 0, 2, 3).reshape(1, nb, g, n)
    new_im_p = sl[:, :, 1].transpose(1, 0, 2, 3).reshape(1, nb, g, n)

    dsk = ssm_d[0].reshape(1, cw)
    wglu_bf = w_glu[0].astype(BF16)
    bglu = b_glu[0].reshape(1, cw)
    nssm = norm_out_ssm[0].reshape(1, cw)
    wout_bf = w_out[0].astype(BF16)
    nffn = norm_ffn[0].reshape(1, d)
    wr_pad = jnp.zeros((d, LANE), F32).at[:, :ne].set(w_router[0])
    wr_hi = wr_pad.astype(BF16)
    wr_lo = (wr_pad - wr_hi.astype(F32)).astype(BF16)
    br = b_router[0].reshape(ne, 1)
    wr2 = jnp.concatenate([wr_hi, wr_lo], axis=1)
    mix_w = (dsk, wglu_bf, bglu, nssm, wout_bf, nffn, wr2, br)
    mix_w_specs = [_full((1, cw)), _full((cw, cw)), _full((1, cw)), _full((1, cw)), _full((2 * cw, d)),
                   _full((1, d)), _full((d, 2 * LANE)), _full((ne, 1))]
    assert 4 + len(mix_w) == N_MIX_IN
    mix_out_shape = (jax.ShapeDtypeStruct((ta, d), F32), jax.ShapeDtypeStruct((ta, d), BF16),
                     jax.ShapeDtypeStruct((TOP_K, ta), I32), jax.ShapeDtypeStruct((TOP_K, ta), F32),
                     jax.ShapeDtypeStruct((n_tiles, ne, 1), I32))
    tpm = rows_p // TOK_TILE
    nbg = nb // nbm
    x1_all, hf_all, idx_all, gate_all, cnt = pl.pallas_call(
        _mix_kernel_prompt,
        grid=(seq // tm, nbg),
        in_specs=[pl.BlockSpec((nbm, tm, d), lambda i, b: (b, i, 0)),
                  pl.BlockSpec((nj, nbm, tm // CHUNK, wch), lambda i, b: (0, b, i, 0)),
                  pl.BlockSpec((nbm, tm, cw), lambda i, b: (b, i, 0)),
                  pl.BlockSpec((nbm, tm, cw), lambda i, b: (b, i, 0))] + mix_w_specs,
        out_specs=(pl.BlockSpec((rows_p, d), lambda i, b: (i * nbg + b, 0)),
                   pl.BlockSpec((rows_p, d), lambda i, b: (i * nbg + b, 0)),
                   pl.BlockSpec((TOP_K, rows_p), lambda i, b: (0, i * nbg + b)),
                   pl.BlockSpec((TOP_K, rows_p), lambda i, b: (0, i * nbg + b)),
                   pl.BlockSpec((tpm, ne, 1), lambda i, b: (i * nbg + b, 0, 0))),
        out_shape=mix_out_shape,
        scratch_shapes=[pltpu.VMEM((nj, nbm, tm, LANE), F32)],
        **_params("mix_prompt", ("parallel", "parallel")),
    )(x_prompt, y4c, u_tok, ycn_p, *mix_w)

    last = n_tiles - 1
    any_spec = pl.BlockSpec(memory_space=pl.ANY)
    x1_all, hf_all, idx_all, gate_all, cnt = pl.pallas_call(
        _mix_kernel_sample,
        grid=(1,),
        in_specs=[_full((1, ns, d)), _full((ns, cw)), _full((1, ns, cw)),
                  _full((1, ns, cw))] + mix_w_specs + [any_spec] * 5,
        out_specs=(pl.BlockSpec((TOK_TILE, d), lambda i: (last, 0)),
                   pl.BlockSpec((TOK_TILE, d), lambda i: (last, 0)),
                   pl.BlockSpec((TOP_K, TOK_TILE), lambda i: (0, last)),
                   pl.BlockSpec((TOP_K, TOK_TILE), lambda i: (0, last)),
                   pl.BlockSpec((1, ne, 1), lambda i: (last, 0, 0))),
        out_shape=mix_out_shape,
        input_output_aliases={N_MIX_IN + k: k for k in range(5)},
        **_params("mix_sample", ("arbitrary",)),
    )(x_sample.reshape(1, ns, d), y_s, u_sm[:ns].astype(BF16).reshape(1, ns, cw),
      ycn_s.reshape(1, ns, cw), *mix_w, x1_all, hf_all, idx_all, gate_all, cnt)

    bm = MOE_ROWS
    cnt2 = cnt.reshape(n_tiles, ne)
    before = jnp.cumsum(cnt2, axis=0) - cnt2
    count = jnp.sum(cnt2, axis=0)
    padded = ((count + WIN + bm - 1) // bm) * bm
    pend = jnp.cumsum(padded)
    pstart = pend - padded
    phase = before % SUBLANE
    span = jnp.where(cnt2 > 0, phase + cnt2, 0)
    gstart = (pstart[None, :] + before - phase).astype(I32).reshape(-1)
    nwin = ((span + WIN - 1) // WIN).astype(I32)
    reg8 = ((span + SUBLANE - 1) // SUBLANE) * SUBLANE
    loff = (jnp.cumsum(reg8, axis=1) - reg8).astype(I32)
    tail = jnp.where(span % SUBLANE != 0, loff + (span // SUBLANE) * SUBLANE, -1).astype(I32)
    twin = jnp.sum(nwin, axis=1).astype(I32)
    n_blocks = (ta * TOP_K + ne * (WIN + bm - 1) + bm - 1) // bm
    cap = n_blocks * bm
    blk0 = jnp.arange(n_blocks, dtype=I32) * bm
    blk_e = jnp.minimum(jnp.sum((pend[None, :] <= blk0[:, None]).astype(I32), axis=1), ne - 1)
    e_ar = jnp.arange(ne, dtype=I32)
    blk_hot = blk_e[:, None] == e_ar[None, :]

    def _of_block(per_expert):
        return jnp.sum(jnp.where(blk_hot, per_expert[None, :], 0), axis=1)

    blk_valid = jnp.clip(_of_block(count) - (blk0 - _of_block(pstart)), 0, bm).astype(I32)
    has = count > 0
    later = jnp.logical_and(e_ar[None, :] > e_ar[:, None], has[None, :])
    nxt_e = jnp.min(jnp.where(later, e_ar[None, :], ne), axis=1)
    nxt_e = jnp.where(nxt_e < ne, nxt_e, -1)
    ordinal = jnp.cumsum(has.astype(I32)) - 1
    is_first = jnp.logical_and(blk_valid > 0, blk0 == _of_block(pstart))
    blk_first = jnp.where(is_first, jnp.where(_of_block(ordinal) == 0, 2, 1), 0).astype(I32)
    blk_next = _of_block(nxt_e).astype(I32)
    blk_slot = (_of_block(ordinal) % 2).astype(I32)
    nvalid = jnp.full((1,), tall, I32)
    loff_al = (WIN * (jnp.cumsum(nwin, axis=1) - nwin)).astype(I32)

    def _window_list(nw, slots, rows, first_hbm, first_buf):
        wcum = jnp.cumsum(nw, axis=1)
        wslot = jnp.arange(slots, dtype=I32)
        w_hot = jnp.logical_and(wslot[None, :, None] >= (wcum - nw)[:, None, :],
                                wslot[None, :, None] < wcum[:, None, :])

        def _of_window(per_tile_expert):
            return jnp.sum(jnp.where(w_hot, per_tile_expert[:, None, :], 0), axis=2)

        w_in_run = wslot[None, :] - _of_window(wcum - nw)
        return [(_of_window(f) + rows * w_in_run).astype(I32).reshape(-1) for f in (first_hbm, first_buf)]

    w_hbm, w_buf = _window_list(nwin, MAX_WINDOWS, WIN, gstart.reshape(n_tiles, ne), loff)
    nrun_d = _dispatch_run_rows(ne)
    zero_grp, spare_grp = nrun_d - 2 * SUBLANE, nrun_d - SUBLANE
    m_lo = jnp.where(nwin > 0, loff, spare_grp).astype(I32).reshape(-1)
    m_tg = jnp.where(jnp.logical_and(nwin > 0, tail >= 0), tail, zero_grp).astype(I32).reshape(-1)
    m_keep = (nwin == 0).astype(I32).reshape(-1)
    used_d = jnp.sum(reg8, axis=1).astype(I32)
    tab_d, off_d = _pack_tables((w_hbm, w_buf, m_lo, m_tg, m_keep, twin, used_d, nvalid))
    tab_c, off_c = _pack_tables((w_hbm, twin, nvalid))
    tab_m, off_m = _pack_tables((blk_e, blk_valid, blk_first, blk_next, blk_slot))
    rowoff_d = (loff + phase).astype(I32).reshape(n_tiles, ne, 1)
    rowoff_c = (loff_al + phase).astype(I32).reshape(n_tiles, ne, 1)
    rgn_d = jnp.stack([loff, loff + reg8], axis=1).astype(I32)
    rgn_c = jnp.stack([loff_al, loff_al + WIN * nwin], axis=1).astype(I32)
    xw = d // 2 + GATE_COLS

    xs = pl.pallas_call(
        functools.partial(_dispatch_kernel, off_d),
        grid_spec=pltpu.PrefetchScalarGridSpec(
            num_scalar_prefetch=1,
            grid=(n_tiles,),
            in_specs=[pl.BlockSpec((TOK_TILE, d), lambda i, *_: (i, 0)),
                      pl.BlockSpec((TOP_K, TOK_TILE), lambda i, *_: (0, i)),
                      pl.BlockSpec((TOP_K, TOK_TILE), lambda i, *_: (0, i)),
                      pl.BlockSpec((1, ne, 1), lambda i, *_: (i, 0, 0)),
                      pl.BlockSpec((1, 2, ne), lambda i, *_: (i, 0, 0))],
            out_specs=pl.BlockSpec(memory_space=pl.ANY),
            scratch_shapes=[pltpu.VMEM((2, _dispatch_run_rows(ne), xw), U32),
                            pltpu.VMEM((ne, SUBLANE, xw), U32), pltpu.SemaphoreType.DMA((2,))]),
        out_shape=jax.ShapeDtypeStruct((cap, xw), U32),
        **_params("dispatch", ("arbitrary",)),
    )(tab_d, hf_all, idx_all, gate_all, rowoff_d, rgn_d)

    yb = pl.pallas_call(
        functools.partial(_moe_kernel, off_m),
        grid_spec=pltpu.PrefetchScalarGridSpec(
            num_scalar_prefetch=1,
            grid=(n_blocks,),
            in_specs=[pl.BlockSpec((bm, xw), lambda i, tab: (i, 0)),
                      pl.BlockSpec((1, 1, 2 * dff), lambda i, tab: (tab[off_m[0] + i], 0, 0)),
                      pl.BlockSpec((1, 1, d), lambda i, tab: (tab[off_m[0] + i], 0, 0)),
                      pl.BlockSpec(memory_space=pl.ANY), pl.BlockSpec(memory_space=pl.ANY)],
            out_specs=pl.BlockSpec((bm, d // 2), lambda i, *_: (i, 0)),
            scratch_shapes=[pltpu.VMEM((2, d, 2 * dff), F32), pltpu.VMEM((2, dff, d), F32),
                            pltpu.VMEM((d, 2 * dff), BF16), pltpu.VMEM((dff, d), BF16),
                            pltpu.SemaphoreType.DMA((2, 2))]),
        out_shape=jax.ShapeDtypeStruct((cap, d // 2), U32),
        **_params("moe", ("arbitrary",)),
    )(tab_m, xs, b_gate_up[0].reshape(ne, 1, 2 * dff),
      b_down[0].reshape(ne, 1, d), w_gate_up[0], w_down[0])

    nfin = norm_final.reshape(1, d)
    nbh = TOK_TILE // tm

    def _tile_of(s):
        return (s + n_tiles - 1) % n_tiles

    tiles_per_time = nbg * tpm

    def _yp_index(s, *_):
        t = jnp.maximum(s - 1, 0)
        return (t % tiles_per_time, t // tiles_per_time, 0)

    y_p, y_sm = pl.pallas_call(
        functools.partial(_combine_kernel, off_c),
        grid_spec=pltpu.PrefetchScalarGridSpec(
            num_scalar_prefetch=1,
            grid=(n_tiles,),
            in_specs=[pl.BlockSpec((TOK_TILE, d), lambda s, *_: (_tile_of(s), 0)),
                      pl.BlockSpec((TOP_K, TOK_TILE), lambda s, *_: (0, _tile_of(s))),
                      pl.BlockSpec((1, ne, 1), lambda s, *_: (_tile_of(s), 0, 0)),
                      pl.BlockSpec((1, 2, ne), lambda s, *_: (_tile_of(s), 0, 0)),
                      pl.BlockSpec((1, d), lambda s, *_: (0, 0)),
                      pl.BlockSpec(memory_space=pl.ANY)],
            out_specs=(pl.BlockSpec((nbh, tm, d), _yp_index),
                       pl.BlockSpec((ns, d), lambda s, *_: (0, 0))),
            scratch_shapes=[pltpu.VMEM((2, _combine_run_rows(ne), d // 2), U32),
                            pltpu.VMEM((TOK_TILE, d), F32), pltpu.SemaphoreType.DMA((2,))]),
        out_shape=(jax.ShapeDtypeStruct((nb, seq, d), F32), jax.ShapeDtypeStruct((ns, d), F32)),
        **_params("combine", ("arbitrary",)),
    )(tab_c, x1_all, idx_all, rowoff_c, rgn_c, nfin, yb)

    return (y_p, y_sm.reshape(ns, 1, d), new_re_p, new_im_p, new_conv_p,
            new_re_s, new_im_s, new_conv_s)
```

```python
import functools
import math

import jax
import jax.numpy as jnp
from jax import lax
from jax.experimental import pallas as pl
from jax.experimental.pallas import tpu as pltpu

F32 = jnp.float32
BF16 = jnp.bfloat16
U32 = jnp.uint32
I32 = jnp.int32
EPS = 1e-5
CHUNK = 16
LANE = 128
TOP_K = 4
SWIGLU_LIMIT = 7.0
SWIGLU_ALPHA = 1.702
MOE_ROWS = 1024
MOE_SUB = 256
MOE_CHAIN = 512
TOK_TILE = 256
WIN = 48
SUBLANE = 8
GATE_COLS = LANE


MAX_WINDOWS = 58


def _dispatch_run_rows(ne):
    return TOP_K * TOK_TILE + ne * 2 * (SUBLANE - 1) + WIN


def _combine_run_rows(ne):
    return -(-(TOP_K * TOK_TILE + ne * (SUBLANE - 1 + WIN - 1)) // WIN) * WIN


DISPATCH_BASE_ROWS = TOP_K * TOK_TILE + 256
COMBINE_BASE_ROWS = TOP_K * TOK_TILE + 768
COMBINE_STEP_ROWS = 256
GATE_PIECES = 3
HI_MASK = 0xFFFF0000
BF16_EXACT = 256.0
MIB = 1024 * 1024
VMEM_MIB = {"small_front": 56, "front": 52, "ssm": 56, "mix_prompt": 56, "mix_sample": 32,
            "dispatch": 40, "moe": 58, "combine": 48}


def _rms(x, g):
    return x * lax.rsqrt(jnp.mean(x * x, axis=-1, keepdims=True) + EPS) * g


def _gelu_tanh(x):
    c = math.sqrt(2.0 / math.pi)
    return 0.5 * x * (1.0 + jnp.tanh(c * (x + 0.044715 * (x * x * x))))


def _params(name, sem):
    return dict(name=name, compiler_params=pltpu.CompilerParams(
        dimension_semantics=sem, vmem_limit_bytes=VMEM_MIB[name] * MIB))


def _pack_pairs(a, b):
    return (pltpu.bitcast(a, U32) >> 16) | (pltpu.bitcast(b, U32) & jnp.uint32(HI_MASK))


def _unpack_pairs(w):
    lo = pltpu.bitcast(w << 16, F32)
    hi = pltpu.bitcast(w & jnp.uint32(HI_MASK), F32)
    return jnp.concatenate([lo, hi], axis=-1).astype(BF16)


def _iota2(shape, axis):
    return lax.broadcasted_iota(I32, shape, axis)


def _expand_cols(compact, reps_log2, n_log2):
    q = _iota2((compact.shape[1], 2 << (reps_log2 + n_log2)), 0)
    c = _iota2((compact.shape[1], 2 << (reps_log2 + n_log2)), 1)
    nmask = (1 << n_log2) - 1
    same = jnp.logical_and((q >> n_log2) == (c >> (reps_log2 + n_log2)), (q & nmask) == (c & nmask))
    return jnp.dot(compact, jnp.where(same, 1.0, 0.0).astype(BF16), preferred_element_type=F32)


def _expand_rows(compact, reps_log2, n_log2):
    r = _iota2((2 << (reps_log2 + n_log2), compact.shape[0]), 0)
    q = _iota2((2 << (reps_log2 + n_log2), compact.shape[0]), 1)
    nmask = (1 << n_log2) - 1
    same = jnp.logical_and((r >> (reps_log2 + n_log2)) == (q >> n_log2), (r & nmask) == (q & nmask))
    return jnp.dot(jnp.where(same, 1.0, 0.0).astype(BF16), compact, preferred_element_type=F32)


def _group_mask(shape, row_shift, col_shift, ngroups):
    r = _iota2(shape, 0)
    c = _iota2(shape, 1)
    return ((r >> row_shift) & (ngroups - 1)) == ((c >> col_shift) & (ngroups - 1))


def _small_front_kernel(x_ref, nmix_ref, win_ref, s0r_ref, s0i_ref, b0_ref, b1_ref, cw_ref,
                        bc_ref, cc_ref, abr_ref, abi_ref, nconv_ref,
                        u_ref, z_ref, y_ref, ycn_ref, sr_ref, si_ref, bdb_ref, cm_ref):
    ns, nst = s0r_ref.shape
    cw = u_ref.shape[1]
    n = bc_ref.shape[1] // 2
    nlog = n.bit_length() - 1
    glog = (nst // n).bit_length() - 1
    hlog = (cw >> glog).bit_length() - 1
    bdb_ref[...] = jnp.where(_group_mask(bdb_ref.shape, hlog, nlog, 1 << glog),
                             _expand_cols(bc_ref[...], glog, nlog), 0.0).astype(BF16)
    cm_ref[...] = jnp.where(_group_mask(cm_ref.shape, nlog, hlog, 1 << glog),
                            _expand_rows(cc_ref[...], glog, nlog), 0.0).astype(BF16)
    h = _rms(x_ref[...], nmix_ref[...]).astype(BF16)
    proj = jnp.dot(h, win_ref[...], preferred_element_type=F32)
    u = proj[:, 0:cw]
    zc = proj[:, cw:2 * cw]
    gb = proj[:, 2 * cw:3 * cw]
    gc = proj[:, 3 * cw:4 * cw]
    z = gc * zc
    u_ref[...] = u
    z_ref[...] = z
    bu = jnp.dot(u[:ns].astype(BF16), bdb_ref[...], preferred_element_type=F32)
    abr = abr_ref[...]
    abi = abi_ref[...]
    s0r = s0r_ref[...]
    s0i = s0i_ref[...]
    sr = abr * s0r - abi * s0i + bu[:, :nst]
    si = abr * s0i + abi * s0r + bu[:, nst:]
    sr_ref[...] = sr
    si_ref[...] = si
    scat = jnp.concatenate([sr, si], axis=-1).astype(BF16)
    y_ref[...] = jnp.dot(scat, cm_ref[...], preferred_element_type=F32)
    conv = cw_ref[0:1, :] * b0_ref[...] + cw_ref[1:2, :] * b1_ref[...] + cw_ref[2:3, :] * z[:ns]
    ycn_ref[...] = _rms(gb[:ns] * conv, nconv_ref[...]).astype(BF16)


def _front_kernel(x_ref, nmix_ref, win_ref, zm_ref, cw_ref, nconv_ref,
                  uc_ref, ut_ref, ycn_ref, zt_ref, zbuf, ubuf):
    i = pl.program_id(1)
    nb, tt, d = x_ref.shape
    cw = ycn_ref.shape[2]
    rows = nb * tt
    ncz = tt // CHUNK
    halo = SUBLANE

    @pl.when(i == 0)
    def _():
        zbuf[:, 0:halo, :] = jnp.broadcast_to(zm_ref[...][None], (nb, halo, cw))

    h = _rms(x_ref[...].reshape(rows, d), nmix_ref[...]).astype(BF16)
    u = jnp.dot(h, win_ref[:, 0:cw], preferred_element_type=F32)
    ut_ref[...] = u.astype(BF16).reshape(nb, tt, cw)
    for j in range(cw // LANE):
        ubuf[j] = u[:, j * LANE:(j + 1) * LANE].reshape(nb, tt, LANE)
    for s in range(CHUNK):
        for j in range(cw // LANE):
            piece = ubuf[j, :, pl.ds(s, ncz, stride=CHUNK), :]
            uc_ref[j, :, :, s * LANE:(s + 1) * LANE] = piece.astype(BF16)
    zc = jnp.dot(h, win_ref[:, cw:2 * cw], preferred_element_type=F32)
    gc = jnp.dot(h, win_ref[:, 3 * cw:4 * cw], preferred_element_type=F32)
    z3 = (gc * zc).reshape(nb, tt, cw)
    zbuf[:, halo:halo + tt, :] = z3
    z1 = zbuf[:, halo - 1:halo - 1 + tt, :]
    z2 = zbuf[:, halo - 2:halo - 2 + tt, :]
    conv = cw_ref[0:1, :] * z2 + cw_ref[1:2, :] * z1 + cw_ref[2:3, :] * z3
    gb = jnp.dot(h, win_ref[:, 2 * cw:3 * cw], preferred_element_type=F32)
    yc = gb * conv.reshape(rows, cw)
    ycn_ref[...] = _rms(yc, nconv_ref[...]).astype(BF16).reshape(nb, tt, cw)
    tail = zbuf[:, tt:tt + halo, :]
    zt_ref[...] = tail
    zbuf[:, 0:halo, :] = tail


def _ssm_kernel(u_ref, um_ref, kc_ref, pc_ref, rc_ref, a16_ref, y_ref, sl_ref,
                s_carry, ds_ref, sp_ref, p_s, r_s, t_s):
    th = pl.program_id(1)
    _, nb, cc, w = u_ref.shape
    nst = p_s.shape[1]
    half = nst // 2
    rows = nb * cc
    blk = 2 * LANE
    u = u_ref[0].reshape(rows, w)

    @pl.when(th == 0)
    def _():
        hch = kc_ref.shape[2]
        gpt = LANE // hch
        hlog = hch.bit_length() - 1
        glog = gpt.bit_length() - 1
        nlog = (pc_ref.shape[2] // 2).bit_length() - 1
        p_s[...] = jnp.where(_group_mask(p_s.shape, hlog, nlog, gpt),
                             _expand_cols(pc_ref[0], glog, nlog), 0.0).astype(BF16)
        r_s[...] = jnp.where(_group_mask(r_s.shape, nlog, hlog, gpt),
                             _expand_rows(rc_ref[0], glog, nlog), 0.0).astype(BF16)
        nlag = kc_ref.shape[1] // LANE
        o = _iota2((hch, LANE), 0)
        c = _iota2((hch, LANE), 1)
        spread = jnp.where((c & (hch - 1)) == o, 1.0, 0.0).astype(BF16)
        lagm = jnp.dot(kc_ref[0], spread, preferred_element_type=F32)
        r = _iota2(lagm.shape, 0)
        c = _iota2(lagm.shape, 1)
        lagm = jnp.where(((r >> hlog) & (gpt - 1)) == (c >> hlog), lagm, 0.0).astype(BF16)
        for dlt in range(nlag // 2):
            b0 = lagm[(2 * dlt) * LANE:(2 * dlt + 1) * LANE]
            b1 = lagm[(2 * dlt + 1) * LANE:(2 * dlt + 2) * LANE]
            b2 = lagm[(2 * dlt + 2) * LANE:(2 * dlt + 3) * LANE]
            t_s[dlt, 0:LANE, 0:LANE] = b1
            t_s[dlt, 0:LANE, LANE:blk] = b2
            t_s[dlt, LANE:blk, 0:LANE] = b0
            t_s[dlt, LANE:blk, LANE:blk] = b1
        ds_ref[:, 0:8, :] = jnp.dot(um_ref[0], p_s[...], preferred_element_type=F32).reshape(nb, 8, nst)
        s_carry[...] = ds_ref[:, 0:1, :]

    ds_ref[...] = jnp.dot(u, p_s[...], preferred_element_type=F32).reshape(nb, cc, nst)
    for tb in range(w // blk):
        acc = jnp.dot(u[:, 0:blk], t_s[tb], preferred_element_type=F32)
        for sb in range(1, tb + 1):
            acc = acc + jnp.dot(u[:, sb * blk:(sb + 1) * blk], t_s[tb - sb],
                                preferred_element_type=F32)
        y_ref[0, :, :, tb * blk:(tb + 1) * blk] = acc.reshape(nb, cc, blk)
    ar = a16_ref[0, 0:1, :].reshape(1, 1, half)
    ai = a16_ref[0, 1:2, :].reshape(1, 1, half)
    sr = s_carry[:, :, 0:half]
    si = s_carry[:, :, half:nst]
    for c in range(cc):
        sp_ref[:, c:c + 1, 0:half] = sr
        sp_ref[:, c:c + 1, half:nst] = si
        dr = ds_ref[:, c:c + 1, 0:half]
        di = ds_ref[:, c:c + 1, half:nst]
        sr, si = ar * sr - ai * si + dr, ar * si + ai * sr + di
    s_carry[:, :, 0:half] = sr
    s_carry[:, :, half:nst] = si
    sl_ref[0, :, :, 0:half] = sr
    sl_ref[0, :, :, half:nst] = si

    sp = sp_ref[...].reshape(rows, nst).astype(BF16)
    for tb in range(w // blk):
        acc = jnp.dot(sp, r_s[:, tb * blk:(tb + 1) * blk], preferred_element_type=F32)
        y_ref[0, :, :, tb * blk:(tb + 1) * blk] += acc.reshape(nb, cc, blk)


N_MIX_IN = 12


def _mix_rows(x, yssm, ut, ycn, dsk_ref, wglu_ref, bglu_ref, nssm_ref, wout_ref,
              nffn_ref, wr_ref, br_ref):
    ne = br_ref.shape[0]
    y = _gelu_tanh(yssm + dsk_ref[...] * ut.astype(F32))
    glu = jnp.dot(y.astype(BF16), wglu_ref[...], preferred_element_type=F32) + bglu_ref[...]
    o = y * jax.nn.sigmoid(glu)
    ysn = _rms(o, nssm_ref[...]).astype(BF16)
    mix = jnp.concatenate([ysn, ycn], axis=-1)
    x1 = x + jnp.dot(mix, wout_ref[...], preferred_element_type=F32)
    hf = _rms(x1, nffn_ref[...])
    hf_hi = hf.astype(BF16)
    hf_lo = (hf - hf_hi.astype(F32)).astype(BF16)
    r = hf.shape[0]
    part = jnp.dot(jnp.concatenate([hf_hi, hf_lo], axis=0), wr_ref[...], preferred_element_type=F32)
    logits = (part[0:r, 0:LANE] + part[0:r, LANE:2 * LANE]) + (part[r:2 * r, 0:LANE] + part[r:2 * r, LANE:2 * LANE])
    lt = logits.T[0:ne, :] + br_ref[...]
    iota = lax.broadcasted_iota(I32, lt.shape, 0)
    vals, idxs = [], []
    sel = jnp.zeros(lt.shape, F32)
    for _ in range(TOP_K):
        m = jnp.max(lt, axis=0, keepdims=True)
        ik = jnp.min(jnp.where(lt == m, iota, ne), axis=0, keepdims=True)
        vals.append(m)
        idxs.append(ik)
        hit = iota == ik
        sel = sel + jnp.where(hit, 1.0, 0.0)
        lt = jnp.where(hit, -jnp.inf, lt)
    es = [jnp.exp(v - vals[0]) for v in vals]
    tot = es[0] + es[1] + es[2] + es[3]
    idx = jnp.concatenate(idxs, axis=0)
    gates = jnp.concatenate([e / tot for e in es], axis=0)
    return x1, hf_hi, idx, gates, sel


def _mix_kernel_prompt(*refs):
    x_ref, yc_ref = refs[0], refs[1]
    x1_ref, hf_ref, idx_ref, gate_ref, cnt_ref, ybuf = refs[N_MIX_IN:]
    nj, nb, ncz, _ = yc_ref.shape
    for s in range(CHUNK):
        for j in range(nj):
            ybuf[j, :, pl.ds(s, ncz, stride=CHUNK), :] = yc_ref[j, :, :, s * LANE:(s + 1) * LANE]
    ut_ref, ycn_ref = refs[2], refs[3]
    tt, d = x_ref.shape[1], x_ref.shape[2]
    cw = nj * LANE
    nbc = TOK_TILE // tt
    for t in range(cnt_ref.shape[0]):
        b0, r0 = t * nbc, t * TOK_TILE
        yssm = jnp.concatenate([ybuf[j, b0:b0 + nbc].reshape(TOK_TILE, LANE) for j in range(nj)], axis=-1)
        x1, hf, idx, gates, sel = _mix_rows(
            x_ref[b0:b0 + nbc].reshape(TOK_TILE, d), yssm, ut_ref[b0:b0 + nbc].reshape(TOK_TILE, cw),
            ycn_ref[b0:b0 + nbc].reshape(TOK_TILE, cw), *refs[4:N_MIX_IN])
        x1_ref[r0:r0 + TOK_TILE, :] = x1
        hf_ref[r0:r0 + TOK_TILE, :] = hf
        idx_ref[:, r0:r0 + TOK_TILE] = idx
        gate_ref[:, r0:r0 + TOK_TILE] = gates
        cnt_ref[t] = jnp.sum(sel, axis=1, keepdims=True).astype(I32)


def _mix_kernel_sample(*refs):
    x1, hf, idx, gates, sel = _mix_rows(refs[0][0], refs[1][...], refs[2][0], refs[3][0],
                                        *refs[4:N_MIX_IN])
    x1_ref, hf_ref, idx_ref, gate_ref, cnt_ref = refs[N_MIX_IN + 5:]
    ns = x1.shape[0]
    x1_ref[...] = jnp.zeros(x1_ref.shape, x1_ref.dtype)
    hf_ref[...] = jnp.zeros(hf_ref.shape, hf_ref.dtype)
    idx_ref[...] = jnp.zeros(idx_ref.shape, idx_ref.dtype)
    gate_ref[...] = jnp.zeros(gate_ref.shape, gate_ref.dtype)
    x1_ref[0:ns, :] = x1
    hf_ref[0:ns, :] = hf
    idx_ref[:, 0:ns] = idx
    gate_ref[:, 0:ns] = gates
    cnt_ref[0] = jnp.sum(sel, axis=1, keepdims=True).astype(I32)


class _Table:
    def __init__(self, ref, offset):
        self.ref, self.offset = ref, offset

    def __getitem__(self, k):
        return self.ref[self.offset + k]


def _pack_tables(tables):
    offsets, total = [], 0
    for t in tables:
        offsets.append(total)
        total += t.shape[0]
    return jnp.concatenate(tables), tuple(offsets)


def _split_bf16(x, parts):
    out = []
    for _ in range(parts - 1):
        p = x.astype(BF16)
        out.append(p)
        x = x - p.astype(F32)
    out.append(x.astype(BF16))
    return out


def _run_tables(idx_ref, loff_ref, tile, n_valid, gate_ref=None):
    ne = loff_ref.shape[1]
    tt = idx_ref.shape[1]
    e_iota = lax.broadcasted_iota(I32, (ne, tt), 0)
    tok = tile * tt + lax.broadcasted_iota(I32, (1, tt), 1)
    valid = tok < n_valid
    hits = [jnp.logical_and(e_iota == idx_ref[k:k + 1, :], valid) for k in range(TOP_K)]
    sel = jnp.zeros((ne, tt), F32)
    for h in hits:
        sel = sel + jnp.where(h, 1.0, 0.0)
    before = lax.broadcasted_iota(I32, (tt, tt), 0) < lax.broadcasted_iota(I32, (tt, tt), 1)
    tri = jnp.where(before, 1.0, 0.0).astype(BF16)
    base = jnp.dot(sel.astype(BF16), tri, preferred_element_type=F32) + loff_ref[0].astype(F32)
    base = jnp.where(sel > 0.0, base + 1.0, 0.0)
    b_hi = BF16_EXACT * jnp.floor(base * (1.0 / BF16_EXACT))
    b_lo = base - b_hi
    halves = jnp.concatenate([b_hi, b_lo], axis=0).astype(BF16)
    if gate_ref is None:
        return halves, None
    gate_e = jnp.zeros((ne, tt), F32)
    for k, h in enumerate(hits):
        gate_e = gate_e + jnp.where(h, gate_ref[k:k + 1, :], 0.0)
    return halves, jnp.concatenate(_split_bf16(gate_e, GATE_PIECES), axis=0)


def _rows_onehot(halves, rgn_ref, row0, nrows, pieces=None):
    ne = rgn_ref.shape[2]
    tt = halves.shape[1]

    def own(copies):
        lo = jnp.concatenate([rgn_ref[0, 0:1, :]] * copies, axis=1)
        hi = jnp.concatenate([rgn_ref[0, 1:2, :]] * copies, axis=1)
        r_i = row0 + lax.broadcasted_iota(I32, (nrows, copies * ne), 0)
        return jnp.logical_and(r_i >= lo, r_i < hi)

    want = jnp.dot(jnp.where(own(2), 1.0, 0.0).astype(BF16), halves, preferred_element_type=F32)
    r_f = (row0 + 1 + lax.broadcasted_iota(I32, (nrows, tt), 0)).astype(F32)
    smat = jnp.where(want == r_f, 1.0, 0.0).astype(BF16)
    if pieces is None:
        return smat
    per = lax.dot_general(smat, pieces, (((1,), (1,)), ((), ())), preferred_element_type=F32)
    gcol = jnp.sum(jnp.where(own(GATE_PIECES), per, 0.0), axis=1, keepdims=True)
    return smat, gcol


def _window_copy(buf, slot, hbm, lo, g, sem, to_hbm, rows, align):
    src = buf.at[slot, pl.ds(pl.multiple_of(lo, align), rows)]
    dst = hbm.at[pl.ds(pl.multiple_of(g, align), rows)]
    if to_hbm:
        return pltpu.make_async_copy(src, dst, sem.at[slot])
    return pltpu.make_async_copy(dst, src, sem.at[slot])


def _start_windows(wg_ref, wl_ref, tw_ref, tile, slots, buf, slot, hbm, sem, to_hbm, rows, align):
    def per_window(w, c):
        k = tile * slots + w
        lo = w * rows if wl_ref is None else wl_ref[k]
        _window_copy(buf, slot, hbm, lo, wg_ref[k], sem, to_hbm, rows, align).start()
        return c

    lax.fori_loop(0, tw_ref[tile], per_window, 0)


def _wait_windows(count, buf, slot, hbm, sem, to_hbm, rows, align):
    batch = 8

    def many(w, c):
        _window_copy(buf, slot, hbm, 0, 0, sem, to_hbm, batch * rows, align).wait()
        return c

    def one(w, c):
        _window_copy(buf, slot, hbm, 0, 0, sem, to_hbm, rows, align).wait()
        return c

    lax.fori_loop(0, count // batch, many, 0)
    lax.fori_loop(0, count % batch, one, 0)


def _dispatch_kernel(offsets, tab_ref,
                     hf_ref, idx_ref, gate_ref, loff_ref, rgn_ref, xs_ref, buf, carry, sem):
    wg_ref, wl_ref, mlo_ref, mtg_ref, keep_ref, tw_ref, used_ref, nv_ref = [
        _Table(tab_ref, o) for o in offsets]
    i = pl.program_id(0)
    nt = pl.num_programs(0)
    ne = loff_ref.shape[1]
    tt, d = hf_ref.shape
    nrun = buf.shape[1]
    slot = i % 2

    @pl.when(i == 0)
    def _():
        carry[...] = jnp.zeros(carry.shape, carry.dtype)
        buf[...] = jnp.zeros(buf.shape, buf.dtype)

    halves, pieces = _run_tables(idx_ref, loff_ref, i, nv_ref[0], gate_ref)

    def emit(row0, nrows):
        smat, gcol = _rows_onehot(halves, rgn_ref, row0, nrows, pieces)
        xr = jnp.dot(smat, hf_ref[...], preferred_element_type=F32)
        lane0 = lax.broadcasted_iota(I32, (nrows, GATE_COLS), 1) == 0
        buf[slot, row0:row0 + nrows, 0:d // 2] = _pack_pairs(xr[:, :d // 2], xr[:, d // 2:])
        buf[slot, row0:row0 + nrows, d // 2:] = pltpu.bitcast(jnp.where(lane0, gcol, 0.0), U32)

    emit(0, DISPATCH_BASE_ROWS)
    past_base = used_ref[i] > DISPATCH_BASE_ROWS

    @pl.when(past_base)
    def _():
        emit(DISPATCH_BASE_ROWS, nrun - DISPATCH_BASE_ROWS)

    @pl.when(jnp.logical_not(past_base))
    def _():
        buf[slot, nrun - 2 * SUBLANE:nrun, :] = jnp.zeros((2 * SUBLANE, buf.shape[2]), buf.dtype)

    def merge(e, c):
        k = i * ne + e
        lo = pl.multiple_of(mlo_ref[k], SUBLANE)
        buf[slot, pl.ds(lo, SUBLANE), :] = buf[slot, pl.ds(lo, SUBLANE), :] | carry[e]
        tg = pl.multiple_of(mtg_ref[k], SUBLANE)
        carry[e] = jnp.where(keep_ref[k] > 0, carry[e], buf[slot, pl.ds(tg, SUBLANE), :])
        return c

    lax.fori_loop(0, ne, merge, 0, unroll=4)

    @pl.when(i > 0)
    def _():
        _wait_windows(tw_ref[i - 1], buf, 1 - slot, xs_ref, sem, True, WIN, SUBLANE)

    _start_windows(wg_ref, wl_ref, tw_ref, i, MAX_WINDOWS, buf, slot, xs_ref, sem, True, WIN, SUBLANE)

    @pl.when(i == nt - 1)
    def _():
        _wait_windows(tw_ref[i], buf, slot, xs_ref, sem, True, WIN, SUBLANE)


def _expert_weight_copies(wg_hbm, wd_hbm, wg_f32, wd_f32, sem, e, slot):
    return (pltpu.make_async_copy(wg_hbm.at[e], wg_f32.at[slot], sem.at[0, slot]),
            pltpu.make_async_copy(wd_hbm.at[e], wd_f32.at[slot], sem.at[1, slot]))


def _moe_kernel(offsets, tab_ref,
                x_ref, bg_ref, bd_ref, wg_hbm, wd_hbm, y_ref,
                wg_f32, wd_f32, wg_bf, wd_bf, sem):
    be_ref, bv_ref, first_ref, nxt_ref, slot_ref = [_Table(tab_ref, o) for o in offsets]
    i = pl.program_id(0)
    e = be_ref[i]
    dff = wd_bf.shape[0]
    bm = x_ref.shape[0]
    sub = MOE_SUB
    nw = x_ref.shape[1] - GATE_COLS

    @pl.when(first_ref[i] > 0)
    def _():
        slot = slot_ref[i]

        @pl.when(first_ref[i] > 1)
        def _():
            for cp in _expert_weight_copies(wg_hbm, wd_hbm, wg_f32, wd_f32, sem, e, slot):
                cp.start()

        for cp in _expert_weight_copies(wg_hbm, wd_hbm, wg_f32, wd_f32, sem, e, slot):
            cp.wait()

        @pl.when(nxt_ref[i] >= 0)
        def _():
            for cp in _expert_weight_copies(wg_hbm, wd_hbm, wg_f32, wd_f32, sem, nxt_ref[i], 1 - slot):
                cp.start()

        wg_bf[...] = wg_f32[slot].astype(BF16)
        wd_bf[...] = wd_f32[slot].astype(BF16)

    def rows(r0, nrows):
        sizes = [MOE_CHAIN] * (nrows // MOE_CHAIN) + ([nrows % MOE_CHAIN] if nrows % MOE_CHAIN else [])
        lo = r0
        for size in sizes:
            _chain(lo, size)
            lo += size

    def _chain(lo, n):
        live = lax.broadcasted_iota(I32, (n, 1), 0) + lo < bv_ref[i]
        x = jnp.where(live, _unpack_pairs(x_ref[lo:lo + n, 0:nw]), jnp.zeros((), BF16))
        route = jnp.where(live, pltpu.bitcast(x_ref[lo:lo + n, nw:], F32)[:, 0:1], 0.0)
        gu = jnp.dot(x, wg_bf[...], preferred_element_type=F32) + bg_ref[0]
        gate = jnp.minimum(gu[:, :dff], SWIGLU_LIMIT)
        up = jnp.clip(gu[:, dff:], -SWIGLU_LIMIT, SWIGLU_LIMIT)
        h = gate * jax.nn.sigmoid(SWIGLU_ALPHA * gate) * (up + 1.0)
        y = jnp.dot(h.astype(BF16), wd_bf[...], preferred_element_type=F32) + bd_ref[0]
        yr = (route * y).astype(BF16).astype(F32)
        half = yr.shape[1] // 2
        y_ref[lo:lo + n, :] = _pack_pairs(yr[:, :half], yr[:, half:])

    nchains = bm // sub
    for live_chains in range(nchains + 1):
        lo_rows, hi_rows = (live_chains - 1) * sub, live_chains * sub

        @pl.when(jnp.logical_and(bv_ref[i] > lo_rows, bv_ref[i] <= hi_rows) if live_chains
                 else bv_ref[i] <= 0)
        def _(used=hi_rows):
            if used:
                rows(0, used)
            if used < bm:
                y_ref[used:bm, :] = jnp.zeros((bm - used, y_ref.shape[1]), y_ref.dtype)


def _combine_tile(step, nt):
    return (step + nt - 1) % nt


def _combine_kernel(offsets, tab_ref,
                    x1_ref, idx_ref, loff_ref, rgn_ref, nf_ref, yb_ref,
                    yp_ref, ys_ref, buf, acc, sem):
    wg_ref, tw_ref, nv_ref = [_Table(tab_ref, o) for o in offsets]
    s = pl.program_id(0)
    nt = pl.num_programs(0)
    tt, d = x1_ref.shape
    nrun = buf.shape[1]
    tile = _combine_tile(s, nt)
    slot = s % 2

    @pl.when(s == 0)
    def _():
        buf[...] = jnp.zeros(buf.shape, buf.dtype)
        _start_windows(wg_ref, None, tw_ref, tile, MAX_WINDOWS, buf, slot, yb_ref, sem, False,
                       WIN, SUBLANE)

    @pl.when(s + 1 < nt)
    def _():
        _start_windows(wg_ref, None, tw_ref, _combine_tile(s + 1, nt), MAX_WINDOWS, buf, 1 - slot,
                       yb_ref, sem, False, WIN, SUBLANE)

    halves, _ = _run_tables(idx_ref, loff_ref, tile, nv_ref[0])

    def gathered(row0, nrows):
        smat = _rows_onehot(halves, rgn_ref, row0, nrows)
        yrun = _unpack_pairs(buf[slot, row0:row0 + nrows])
        return lax.dot_general(smat, yrun, (((0,), (0,)), ((), ())), preferred_element_type=F32)

    _wait_windows(tw_ref[tile], buf, slot, yb_ref, sem, False, WIN, SUBLANE)
    acc[...] = gathered(0, COMBINE_BASE_ROWS)
    for row0 in range(COMBINE_BASE_ROWS, nrun, COMBINE_STEP_ROWS):
        @pl.when(WIN * tw_ref[tile] > row0)
        def _(row0=row0):
            acc[...] += gathered(row0, min(COMBINE_STEP_ROWS, nrun - row0))

    out = _rms(x1_ref[...] + acc[...], nf_ref[...])

    @pl.when(s == 0)
    def _():
        ys_ref[...] = out[0:ys_ref.shape[0], :]

    @pl.when(s > 0)
    def _():
        yp_ref[...] = out.reshape(yp_ref.shape)


def _ssm_matrices(a_re, a_im, log_dt, b_re, b_im, c_re, c_im):
    g, n = a_re.shape
    hch = b_re.shape[2]
    gpt = LANE // hch
    nj = g // gpt
    a = lax.complex(a_re, a_im)
    dta = a * jnp.exp(log_dt)[:, None]
    a_bar = jnp.exp(dta)
    bb = ((a_bar - 1.0) / a)[:, :, None] * lax.complex(b_re, b_im)
    cc = lax.complex(c_re, c_im)
    ks = jnp.arange(CHUNK + 1, dtype=F32)
    pw = jnp.exp(dta[None] * ks[:, None, None])
    kk = jnp.real(jnp.einsum('gon,kgn,gni->kgio', cc, pw[:CHUNK], bb))
    kk = jnp.concatenate([jnp.zeros_like(kk[:1]), kk], axis=0)
    kc = kk.reshape(CHUNK + 1, nj, gpt * hch, hch).transpose(1, 0, 2, 3)
    kc = kc.reshape(nj, (CHUNK + 1) * LANE, hch).astype(BF16)
    pwr, pwi = jnp.real(pw), jnp.imag(pw)
    bbr = jnp.real(bb).transpose(0, 2, 1).reshape(1, g * hch, n)
    bbi = jnp.imag(bb).transpose(0, 2, 1).reshape(1, g * hch, n)
    par = jnp.repeat(pwr[CHUNK - 1::-1][:CHUNK], hch, axis=1)
    pai = jnp.repeat(pwi[CHUNK - 1::-1][:CHUNK], hch, axis=1)
    pc = jnp.concatenate([par * bbr - pai * bbi, par * bbi + pai * bbr], axis=-1)
    pc = pc.reshape(CHUNK, nj, gpt * hch, 2 * n).transpose(1, 0, 2, 3)
    pc = pc.reshape(nj, CHUNK * LANE, 2 * n).astype(BF16)
    ccr = jnp.real(cc).transpose(2, 0, 1).reshape(n, 1, g * hch)
    cci = jnp.imag(cc).transpose(2, 0, 1).reshape(n, 1, g * hch)
    qar = jnp.repeat(pwr[1:CHUNK + 1].transpose(2, 0, 1), hch, axis=2)
    qai = jnp.repeat(pwi[1:CHUNK + 1].transpose(2, 0, 1), hch, axis=2)
    rc = jnp.stack([ccr * qar - cci * qai, -(ccr * qai + cci * qar)], axis=0)
    rc = rc.reshape(2 * n, CHUNK, nj, gpt * hch).transpose(2, 0, 1, 3)
    rc = rc.reshape(nj, 2 * n, CHUNK * LANE).astype(BF16)
    a16 = pw[CHUNK].reshape(nj, 1, gpt * n)
    a16 = jnp.concatenate([jnp.real(a16), jnp.imag(a16)], axis=1)
    bc = jnp.stack([jnp.real(bb), jnp.imag(bb)], axis=0).transpose(1, 3, 0, 2)
    bc = bc.reshape(g * hch, 2 * n).astype(BF16)
    c2 = jnp.stack([jnp.real(cc), -jnp.imag(cc)], axis=0).transpose(0, 3, 1, 2)
    c2 = c2.reshape(2 * n, g * hch).astype(BF16)
    abr = jnp.real(a_bar).reshape(1, g * n)
    abi = jnp.imag(a_bar).reshape(1, g * n)
    return kc, pc, rc, a16, bc, c2, abr, abi


def _full(shape):
    return pl.BlockSpec(shape, lambda *_: (0,) * len(shape))


def kernel(x_prompt, x_sample, state_ssm_re, state_ssm_im, state_conv, meta_tokens, norm_mix, w_in,
           ssm_a_re, ssm_a_im, ssm_log_dt, ssm_b_re, ssm_b_im, ssm_c_re, ssm_c_im, ssm_d, w_glu, b_glu,
           conv_w, norm_out_ssm, norm_out_conv, w_out, norm_ffn, w_router, b_router, w_gate_up,
           b_gate_up, w_down, b_down, norm_final):
    nb, seq, d = x_prompt.shape
    ns = x_sample.shape[0]
    depth, _, g, n = state_ssm_re.shape
    assert depth == 1 and x_sample.shape[1] == 1 and meta_tokens.shape[0] == CHUNK
    cw = conv_w.shape[2]
    nj = cw // LANE
    ne = w_router.shape[2]
    dff = w_down.shape[2]
    nst = g * n
    nbt = nb // 2
    tt = 256
    n_chunks = seq // CHUNK
    tp = nb * seq
    tall = tp + ns
    tm = 128
    nbm = nb
    rows_p = nbm * tm
    assert rows_p % TOK_TILE == 0 and TOK_TILE % tm == 0 and ns <= TOK_TILE and d % 2 == 0
    n_tiles = tp // TOK_TILE + 1
    ta = n_tiles * TOK_TILE

    kc, pc, rc, a16, bc, c2, abr, abi = _ssm_matrices(
        ssm_a_re[0], ssm_a_im[0], ssm_log_dt[0], ssm_b_re[0], ssm_b_im[0], ssm_c_re[0], ssm_c_im[0])
    win_bf = w_in[0].astype(BF16)
    nmix = norm_mix[0].reshape(1, d)
    nconv = norm_out_conv[0].reshape(1, cw)
    cwt = conv_w[0]

    xsm = jnp.concatenate([x_sample.reshape(ns, d), meta_tokens], axis=0)
    nsm = ns + CHUNK
    s0r = state_ssm_re[0].reshape(ns, nst)
    s0i = state_ssm_im[0].reshape(ns, nst)
    buf0 = state_conv[0, :, 0, :]
    buf1 = state_conv[0, :, 1, :]
    u_sm, z_sm, y_s, ycn_s, sr_s, si_s = pl.pallas_call(
        _small_front_kernel,
        out_shape=(jax.ShapeDtypeStruct((nsm, cw), F32), jax.ShapeDtypeStruct((nsm, cw), F32),
                   jax.ShapeDtypeStruct((ns, cw), F32), jax.ShapeDtypeStruct((ns, cw), BF16),
                   jax.ShapeDtypeStruct((ns, nst), F32), jax.ShapeDtypeStruct((ns, nst), F32)),
        scratch_shapes=[pltpu.VMEM((cw, 2 * nst), BF16), pltpu.VMEM((2 * nst, cw), BF16)],
        **_params("small_front", None),
    )(xsm, nmix, win_bf, s0r, s0i, buf0, buf1, cwt, bc, c2, abr, abi, nconv)
    u_meta = u_sm[ns:]
    z_meta8 = z_sm[ns + CHUNK - 8:]
    new_conv_s = jnp.stack([buf1, z_sm[:ns]], axis=1)[None]
    new_re_s = sr_s.reshape(1, ns, g, n)
    new_im_s = si_s.reshape(1, ns, g, n)

    wch = CHUNK * LANE
    u4c, u_tok, ycn_p, ztail = pl.pallas_call(
        _front_kernel,
        grid=(nb // nbt, seq // tt),
        in_specs=[pl.BlockSpec((nbt, tt, d), lambda b, i: (b, i, 0)),
                  _full((1, d)), _full((d, 4 * cw)), _full((8, cw)), _full((3, cw)), _full((1, cw))],
        out_specs=(pl.BlockSpec((nj, nbt, tt // CHUNK, wch), lambda b, i: (0, b, i, 0)),
                   pl.BlockSpec((nbt, tt, cw), lambda b, i: (b, i, 0)),
                   pl.BlockSpec((nbt, tt, cw), lambda b, i: (b, i, 0)),
                   pl.BlockSpec((nbt, 8, cw), lambda b, i: (b, 0, 0))),
        out_shape=(jax.ShapeDtypeStruct((nj, nb, n_chunks, wch), BF16),
                   jax.ShapeDtypeStruct((nb, seq, cw), BF16),
                   jax.ShapeDtypeStruct((nb, seq, cw), BF16),
                   jax.ShapeDtypeStruct((nb, 8, cw), F32)),
        scratch_shapes=[pltpu.VMEM((nbt, tt + 8, cw), F32), pltpu.VMEM((nj, nbt, tt, LANE), F32)],
        **_params("front", ("arbitrary", "arbitrary")),
    )(x_prompt, nmix, win_bf, z_meta8, cwt, nconv)
    new_conv_p = ztail[:, 6:8, :][None]

    cc = n_chunks // 2
    um = u_meta.reshape(CHUNK, nj, LANE).transpose(1, 0, 2).reshape(nj, 1, wch)
    um = jnp.broadcast_to(um, (nj, 8 * nb, wch)).astype(BF16)
    gpt = g // nj
    nstj = 2 * gpt * n
    hch = cw // g
    y4c, s_last = pl.pallas_call(
        _ssm_kernel,
        grid=(nj, n_chunks // cc),
        in_specs=[pl.BlockSpec((1, nb, cc, wch), lambda j, t: (j, 0, t, 0)),
                  pl.BlockSpec((1, 8 * nb, wch), lambda j, t: (j, 0, 0)),
                  pl.BlockSpec((1, (CHUNK + 1) * LANE, hch), lambda j, t: (j, 0, 0)),
                  pl.BlockSpec((1, wch, 2 * n), lambda j, t: (j, 0, 0)),
                  pl.BlockSpec((1, 2 * n, wch), lambda j, t: (j, 0, 0)),
                  pl.BlockSpec((1, 2, nstj // 2), lambda j, t: (j, 0, 0))],
        out_specs=(pl.BlockSpec((1, nb, cc, wch), lambda j, t: (j, 0, t, 0)),
                   pl.BlockSpec((1, nb, 1, nstj), lambda j, t: (j, 0, 0, 0))),
        out_shape=(jax.ShapeDtypeStruct((nj, nb, n_chunks, wch), F32),
                   jax.ShapeDtypeStruct((nj, nb, 1, nstj), F32)),
        scratch_shapes=[pltpu.VMEM((nb, 1, nstj), F32), pltpu.VMEM((nb, cc, nstj), F32),
                        pltpu.VMEM((nb, cc, nstj), F32),
                        pltpu.VMEM((wch, nstj), BF16), pltpu.VMEM((nstj, wch), BF16),
                        pltpu.VMEM((CHUNK // 2, 2 * LANE, 2 * LANE), BF16)],
        **_params("ssm", ("parallel", "arbitrary")),
    )(u4c, um, kc, pc, rc, a16)
    sl = s_last.reshape(nj, nb, 2, gpt, n)
    new_re_p = sl[:, :, 0].transpose(1, 0, 2, 3).reshape(1, nb, g, n)
    new_im_p = sl[:, :, 1].transpose(1, 0, 2, 3).reshape(1, nb, g, n)

    dsk = ssm_d[0].reshape(1, cw)
    wglu_bf = w_glu[0].astype(BF16)
    bglu = b_glu[0].reshape(1, cw)
    nssm = norm_out_ssm[0].reshape(1, cw)
    wout_bf = w_out[0].astype(BF16)
    nffn = norm_ffn[0].reshape(1, d)
    wr_pad = jnp.zeros((d, LANE), F32).at[:, :ne].set(w_router[0])
    wr_hi = wr_pad.astype(BF16)
    wr_lo = (wr_pad - wr_hi.astype(F32)).astype(BF16)
    br = b_router[0].reshape(ne, 1)
    wr2 = jnp.concatenate([wr_hi, wr_lo], axis=1)
    mix_w = (dsk, wglu_bf, bglu, nssm, wout_bf, nffn, wr2, br)
    mix_w_specs = [_full((1, cw)), _full((cw, cw)), _full((1, cw)), _full((1, cw)), _full((2 * cw, d)),
                   _full((1, d)), _full((d, 2 * LANE)), _full((ne, 1))]
    assert 4 + len(mix_w) == N_MIX_IN
    mix_out_shape = (jax.ShapeDtypeStruct((ta, d), F32), jax.ShapeDtypeStruct((ta, d), BF16),
                     jax.ShapeDtypeStruct((TOP_K, ta), I32), jax.ShapeDtypeStruct((TOP_K, ta), F32),
                     jax.ShapeDtypeStruct((n_tiles, ne, 1), I32))
    tpm = rows_p // TOK_TILE
    nbg = nb // nbm
    x1_all, hf_all, idx_all, gate_all, cnt = pl.pallas_call(
        _mix_kernel_prompt,
        grid=(seq // tm, nbg),
        in_specs=[pl.BlockSpec((nbm, tm, d), lambda i, b: (b, i, 0)),
                  pl.BlockSpec((nj, nbm, tm // CHUNK, wch), lambda i, b: (0, b, i, 0)),
                  pl.BlockSpec((nbm, tm, cw), lambda i, b: (b, i, 0)),
                  pl.BlockSpec((nbm, tm, cw), lambda i, b: (b, i, 0))] + mix_w_specs,
        out_specs=(pl.BlockSpec((rows_p, d), lambda i, b: (i * nbg + b, 0)),
                   pl.BlockSpec((rows_p, d), lambda i, b: (i * nbg + b, 0)),
                   pl.BlockSpec((TOP_K, rows_p), lambda i, b: (0, i * nbg + b)),
                   pl.BlockSpec((TOP_K, rows_p), lambda i, b: (0, i * nbg + b)),
                   pl.BlockSpec((tpm, ne, 1), lambda i, b: (i * nbg + b, 0, 0))),
        out_shape=mix_out_shape,
        scratch_shapes=[pltpu.VMEM((nj, nbm, tm, LANE), F32)],
        **_params("mix_prompt", ("parallel", "parallel")),
    )(x_prompt, y4c, u_tok, ycn_p, *mix_w)

    last = n_tiles - 1
    any_spec = pl.BlockSpec(memory_space=pl.ANY)
    x1_all, hf_all, idx_all, gate_all, cnt = pl.pallas_call(
        _mix_kernel_sample,
        grid=(1,),
        in_specs=[_full((1, ns, d)), _full((ns, cw)), _full((1, ns, cw)),
                  _full((1, ns, cw))] + mix_w_specs + [any_spec] * 5,
        out_specs=(pl.BlockSpec((TOK_TILE, d), lambda i: (last, 0)),
                   pl.BlockSpec((TOK_TILE, d), lambda i: (last, 0)),
                   pl.BlockSpec((TOP_K, TOK_TILE), lambda i: (0, last)),
                   pl.BlockSpec((TOP_K, TOK_TILE), lambda i: (0, last)),
                   pl.BlockSpec((1, ne, 1), lambda i: (last, 0, 0))),
        out_shape=mix_out_shape,
        input_output_aliases={N_MIX_IN + k: k for k in range(5)},
        **_params("mix_sample", ("arbitrary",)),
    )(x_sample.reshape(1, ns, d), y_s, u_sm[:ns].astype(BF16).reshape(1, ns, cw),
      ycn_s.reshape(1, ns, cw), *mix_w, x1_all, hf_all, idx_all, gate_all, cnt)

    bm = MOE_ROWS
    cnt2 = cnt.reshape(n_tiles, ne)
    before = jnp.cumsum(cnt2, axis=0) - cnt2
    count = jnp.sum(cnt2, axis=0)
    padded = ((count + WIN + bm - 1) // bm) * bm
    pend = jnp.cumsum(padded)
    pstart = pend - padded
    phase = before % SUBLANE
    span = jnp.where(cnt2 > 0, phase + cnt2, 0)
    gstart = (pstart[None, :] + before - phase).astype(I32).reshape(-1)
    nwin = ((span + WIN - 1) // WIN).astype(I32)
    reg8 = ((span + SUBLANE - 1) // SUBLANE) * SUBLANE
    loff = (jnp.cumsum(reg8, axis=1) - reg8).astype(I32)
    tail = jnp.where(span % SUBLANE != 0, loff + (span // SUBLANE) * SUBLANE, -1).astype(I32)
    twin = jnp.sum(nwin, axis=1).astype(I32)
    n_blocks = (ta * TOP_K + ne * (WIN + bm - 1) + bm - 1) // bm
    cap = n_blocks * bm
    blk0 = jnp.arange(n_blocks, dtype=I32) * bm
    blk_e = jnp.minimum(jnp.sum((pend[None, :] <= blk0[:, None]).astype(I32), axis=1), ne - 1)
    e_ar = jnp.arange(ne, dtype=I32)
    blk_hot = blk_e[:, None] == e_ar[None, :]

    def _of_block(per_expert):
        return jnp.sum(jnp.where(blk_hot, per_expert[None, :], 0), axis=1)

    blk_valid = jnp.clip(_of_block(count) - (blk0 - _of_block(pstart)), 0, bm).astype(I32)
    has = count > 0
    later = jnp.logical_and(e_ar[None, :] > e_ar[:, None], has[None, :])
    nxt_e = jnp.min(jnp.where(later, e_ar[None, :], ne), axis=1)
    nxt_e = jnp.where(nxt_e < ne, nxt_e, -1)
    ordinal = jnp.cumsum(has.astype(I32)) - 1
    is_first = jnp.logical_and(blk_valid > 0, blk0 == _of_block(pstart))
    blk_first = jnp.where(is_first, jnp.where(_of_block(ordinal) == 0, 2, 1), 0).astype(I32)
    blk_next = _of_block(nxt_e).astype(I32)
    blk_slot = (_of_block(ordinal) % 2).astype(I32)
    nvalid = jnp.full((1,), tall, I32)
    loff_al = (WIN * (jnp.cumsum(nwin, axis=1) - nwin)).astype(I32)

    def _window_list(nw, slots, rows, first_hbm, first_buf):
        wcum = jnp.cumsum(nw, axis=1)
        wslot = jnp.arange(slots, dtype=I32)
        w_hot = jnp.logical_and(wslot[None, :, None] >= (wcum - nw)[:, None, :],
                                wslot[None, :, None] < wcum[:, None, :])

        def _of_window(per_tile_expert):
            return jnp.sum(jnp.where(w_hot, per_tile_expert[:, None, :], 0), axis=2)

        w_in_run = wslot[None, :] - _of_window(wcum - nw)
        return [(_of_window(f) + rows * w_in_run).astype(I32).reshape(-1) for f in (first_hbm, first_buf)]

    w_hbm, w_buf = _window_list(nwin, MAX_WINDOWS, WIN, gstart.reshape(n_tiles, ne), loff)
    nrun_d = _dispatch_run_rows(ne)
    zero_grp, spare_grp = nrun_d - 2 * SUBLANE, nrun_d - SUBLANE
    m_lo = jnp.where(nwin > 0, loff, spare_grp).astype(I32).reshape(-1)
    m_tg = jnp.where(jnp.logical_and(nwin > 0, tail >= 0), tail, zero_grp).astype(I32).reshape(-1)
    m_keep = (nwin == 0).astype(I32).reshape(-1)
    used_d = jnp.sum(reg8, axis=1).astype(I32)
    tab_d, off_d = _pack_tables((w_hbm, w_buf, m_lo, m_tg, m_keep, twin, used_d, nvalid))
    tab_c, off_c = _pack_tables((w_hbm, twin, nvalid))
    tab_m, off_m = _pack_tables((blk_e, blk_valid, blk_first, blk_next, blk_slot))
    rowoff_d = (loff + phase).astype(I32).reshape(n_tiles, ne, 1)
    rowoff_c = (loff_al + phase).astype(I32).reshape(n_tiles, ne, 1)
    rgn_d = jnp.stack([loff, loff + reg8], axis=1).astype(I32)
    rgn_c = jnp.stack([loff_al, loff_al + WIN * nwin], axis=1).astype(I32)
    xw = d // 2 + GATE_COLS

    xs = pl.pallas_call(
        functools.partial(_dispatch_kernel, off_d),
        grid_spec=pltpu.PrefetchScalarGridSpec(
            num_scalar_prefetch=1,
            grid=(n_tiles,),
            in_specs=[pl.BlockSpec((TOK_TILE, d), lambda i, *_: (i, 0)),
                      pl.BlockSpec((TOP_K, TOK_TILE), lambda i, *_: (0, i)),
                      pl.BlockSpec((TOP_K, TOK_TILE), lambda i, *_: (0, i)),
                      pl.BlockSpec((1, ne, 1), lambda i, *_: (i, 0, 0)),
                      pl.BlockSpec((1, 2, ne), lambda i, *_: (i, 0, 0))],
            out_specs=pl.BlockSpec(memory_space=pl.ANY),
            scratch_shapes=[pltpu.VMEM((2, _dispatch_run_rows(ne), xw), U32),
                            pltpu.VMEM((ne, SUBLANE, xw), U32), pltpu.SemaphoreType.DMA((2,))]),
        out_shape=jax.ShapeDtypeStruct((cap, xw), U32),
        **_params("dispatch", ("arbitrary",)),
    )(tab_d, hf_all, idx_all, gate_all, rowoff_d, rgn_d)

    yb = pl.pallas_call(
        functools.partial(_moe_kernel, off_m),
        grid_spec=pltpu.PrefetchScalarGridSpec(
            num_scalar_prefetch=1,
            grid=(n_blocks,),
            in_specs=[pl.BlockSpec((bm, xw), lambda i, tab: (i, 0)),
                      pl.BlockSpec((1, 1, 2 * dff), lambda i, tab: (tab[off_m[0] + i], 0, 0)),
                      pl.BlockSpec((1, 1, d), lambda i, tab: (tab[off_m[0] + i], 0, 0)),
                      pl.BlockSpec(memory_space=pl.ANY), pl.BlockSpec(memory_space=pl.ANY)],
            out_specs=pl.BlockSpec((bm, d // 2), lambda i, *_: (i, 0)),
            scratch_shapes=[pltpu.VMEM((2, d, 2 * dff), F32), pltpu.VMEM((2, dff, d), F32),
                            pltpu.VMEM((d, 2 * dff), BF16), pltpu.VMEM((dff, d), BF16),
                            pltpu.SemaphoreType.DMA((2, 2))]),
        out_shape=jax.ShapeDtypeStruct((cap, d // 2), U32),
        **_params("moe", ("arbitrary",)),
    )(tab_m, xs, b_gate_up[0].reshape(ne, 1, 2 * dff),
      b_down[0].reshape(ne, 1, d), w_gate_up[0], w_down[0])

    nfin = norm_final.reshape(1, d)
    nbh = TOK_TILE // tm

    def _tile_of(s):
        return (s + n_tiles - 1) % n_tiles

    tiles_per_time = nbg * tpm

    def _yp_index(s, *_):
        t = jnp.maximum(s - 1, 0)
        return (t % tiles_per_time, t // tiles_per_time, 0)

    y_p, y_sm = pl.pallas_call(
        functools.partial(_combine_kernel, off_c),
        grid_spec=pltpu.PrefetchScalarGridSpec(
            num_scalar_prefetch=1,
            grid=(n_tiles,),
            in_specs=[pl.BlockSpec((TOK_TILE, d), lambda s, *_: (_tile_of(s), 0)),
                      pl.BlockSpec((TOP_K, TOK_TILE), lambda s, *_: (0, _tile_of(s))),
                      pl.BlockSpec((1, ne, 1), lambda s, *_: (_tile_of(s), 0, 0)),
                      pl.BlockSpec((1, 2, ne), lambda s, *_: (_tile_of(s), 0, 0)),
                      pl.BlockSpec((1, d), lambda s, *_: (0, 0)),
                      pl.BlockSpec(memory_space=pl.ANY)],
            out_specs=(pl.BlockSpec((nbh, tm, d), _yp_index),
                       pl.BlockSpec((ns, d), lambda s, *_: (0, 0))),
            scratch_shapes=[pltpu.VMEM((2, _combine_run_rows(ne), d // 2), U32),
                            pltpu.VMEM((TOK_TILE, d), F32), pltpu.SemaphoreType.DMA((2,))]),
        out_shape=(jax.ShapeDtypeStruct((nb, seq, d), F32), jax.ShapeDtypeStruct((ns, d), F32)),
        **_params("combine", ("arbitrary",)),
    )(tab_c, x1_all, idx_all, rowoff_c, rgn_c, nfin, yb)

    return (y_p, y_sm.reshape(ns, 1, d), new_re_p, new_im_p, new_conv_p,
            new_re_s, new_im_s, new_conv_s)
```

```python
import functools
import math

import jax
import jax.numpy as jnp
from jax import lax
from jax.experimental import pallas as pl
from jax.experimental.pallas import tpu as pltpu

F32 = jnp.float32
BF16 = jnp.bfloat16
U32 = jnp.uint32
I32 = jnp.int32
EPS = 1e-5
CHUNK = 16
LANE = 128
TOP_K = 4
SWIGLU_LIMIT = 7.0
SWIGLU_ALPHA = 1.702
MOE_ROWS = 1024
MOE_SUB = 256
MOE_CHAIN = 512
MOE_TAIL = 64
TOK_TILE = 256
WIN = 32
SUBLANE = 8
GATE_COLS = LANE


MAX_WINDOWS = 72


def _dispatch_run_rows(ne):
    return TOP_K * TOK_TILE + ne * 2 * (SUBLANE - 1) + WIN


def _combine_run_rows(ne):
    return -(-(TOP_K * TOK_TILE + ne * (SUBLANE - 1 + WIN - 1)) // WIN) * WIN


DISPATCH_BASE_ROWS = TOP_K * TOK_TILE + 256
COMBINE_BASE_ROWS = TOP_K * TOK_TILE + 768
COMBINE_STEP_ROWS = 256
GATE_PIECES = 3
HI_MASK = 0xFFFF0000
BF16_EXACT = 256.0
MIB = 1024 * 1024
VMEM_MIB = {"small_front": 56, "front": 52, "ssm": 56, "mix_prompt": 56, "mix_sample": 32,
            "dispatch": 40, "moe": 58, "combine": 48}


def _rms(x, g):
    return x * lax.rsqrt(jnp.mean(x * x, axis=-1, keepdims=True) + EPS) * g


def _gelu_tanh(x):
    c = math.sqrt(2.0 / math.pi)
    return 0.5 * x * (1.0 + jnp.tanh(c * (x + 0.044715 * (x * x * x))))


def _params(name, sem):
    return dict(name=name, compiler_params=pltpu.CompilerParams(
        dimension_semantics=sem, vmem_limit_bytes=VMEM_MIB[name] * MIB))


def _pack_pairs(a, b):
    return (pltpu.bitcast(a, U32) >> 16) | (pltpu.bitcast(b, U32) & jnp.uint32(HI_MASK))


def _unpack_pairs(w):
    lo = pltpu.bitcast(w << 16, F32)
    hi = pltpu.bitcast(w & jnp.uint32(HI_MASK), F32)
    return jnp.concatenate([lo, hi], axis=-1).astype(BF16)


def _iota2(shape, axis):
    return lax.broadcasted_iota(I32, shape, axis)


def _expand_cols(compact, reps_log2, n_log2):
    q = _iota2((compact.shape[1], 2 << (reps_log2 + n_log2)), 0)
    c = _iota2((compact.shape[1], 2 << (reps_log2 + n_log2)), 1)
    nmask = (1 << n_log2) - 1
    same = jnp.logical_and((q >> n_log2) == (c >> (reps_log2 + n_log2)), (q & nmask) == (c & nmask))
    return jnp.dot(compact, jnp.where(same, 1.0, 0.0).astype(BF16), preferred_element_type=F32)


def _expand_rows(compact, reps_log2, n_log2):
    r = _iota2((2 << (reps_log2 + n_log2), compact.shape[0]), 0)
    q = _iota2((2 << (reps_log2 + n_log2), compact.shape[0]), 1)
    nmask = (1 << n_log2) - 1
    same = jnp.logical_and((r >> (reps_log2 + n_log2)) == (q >> n_log2), (r & nmask) == (q & nmask))
    return jnp.dot(jnp.where(same, 1.0, 0.0).astype(BF16), compact, preferred_element_type=F32)


def _group_mask(shape, row_shift, col_shift, ngroups):
    r = _iota2(shape, 0)
    c = _iota2(shape, 1)
    return ((r >> row_shift) & (ngroups - 1)) == ((c >> col_shift) & (ngroups - 1))


def _small_front_kernel(x_ref, nmix_ref, win_ref, s0r_ref, s0i_ref, b0_ref, b1_ref, cw_ref,
                        bc_ref, cc_ref, abr_ref, abi_ref, nconv_ref,
                        u_ref, z_ref, y_ref, ycn_ref, sr_ref, si_ref, bdb_ref, cm_ref):
    ns, nst = s0r_ref.shape
    cw = u_ref.shape[1]
    n = bc_ref.shape[1] // 2
    nlog = n.bit_length() - 1
    glog = (nst // n).bit_length() - 1
    hlog = (cw >> glog).bit_length() - 1
    bdb_ref[...] = jnp.where(_group_mask(bdb_ref.shape, hlog, nlog, 1 << glog),
                             _expand_cols(bc_ref[...], glog, nlog), 0.0).astype(BF16)
    cm_ref[...] = jnp.where(_group_mask(cm_ref.shape, nlog, hlog, 1 << glog),
                            _expand_rows(cc_ref[...], glog, nlog), 0.0).astype(BF16)
    h = _rms(x_ref[...], nmix_ref[...]).astype(BF16)
    proj = jnp.dot(h, win_ref[...], preferred_element_type=F32)
    u = proj[:, 0:cw]
    zc = proj[:, cw:2 * cw]
    gb = proj[:, 2 * cw:3 * cw]
    gc = proj[:, 3 * cw:4 * cw]
    z = gc * zc
    u_ref[...] = u
    z_ref[...] = z
    bu = jnp.dot(u[:ns].astype(BF16), bdb_ref[...], preferred_element_type=F32)
    abr = abr_ref[...]
    abi = abi_ref[...]
    s0r = s0r_ref[...]
    s0i = s0i_ref[...]
    sr = abr * s0r - abi * s0i + bu[:, :nst]
    si = abr * s0i + abi * s0r + bu[:, nst:]
    sr_ref[...] = sr
    si_ref[...] = si
    scat = jnp.concatenate([sr, si], axis=-1).astype(BF16)
    y_ref[...] = jnp.dot(scat, cm_ref[...], preferred_element_type=F32)
    conv = cw_ref[0:1, :] * b0_ref[...] + cw_ref[1:2, :] * b1_ref[...] + cw_ref[2:3, :] * z[:ns]
    ycn_ref[...] = _rms(gb[:ns] * conv, nconv_ref[...]).astype(BF16)


def _front_kernel(x_ref, nmix_ref, win_ref, zm_ref, cw_ref, nconv_ref,
                  uc_ref, ut_ref, ycn_ref, zt_ref, zbuf, ubuf):
    i = pl.program_id(1)
    nb, tt, d = x_ref.shape
    cw = ycn_ref.shape[2]
    rows = nb * tt
    ncz = tt // CHUNK
    halo = SUBLANE

    @pl.when(i == 0)
    def _():
        zbuf[:, 0:halo, :] = jnp.broadcast_to(zm_ref[...][None], (nb, halo, cw))

    h = _rms(x_ref[...].reshape(rows, d), nmix_ref[...]).astype(BF16)
    u = jnp.dot(h, win_ref[:, 0:cw], preferred_element_type=F32)
    ut_ref[...] = u.astype(BF16).reshape(nb, tt, cw)
    for j in range(cw // LANE):
        ubuf[j] = u[:, j * LANE:(j + 1) * LANE].reshape(nb, tt, LANE)
    for s in range(CHUNK):
        for j in range(cw // LANE):
            piece = ubuf[j, :, pl.ds(s, ncz, stride=CHUNK), :]
            uc_ref[j, :, :, s * LANE:(s + 1) * LANE] = piece.astype(BF16)
    zc = jnp.dot(h, win_ref[:, cw:2 * cw], preferred_element_type=F32)
    gc = jnp.dot(h, win_ref[:, 3 * cw:4 * cw], preferred_element_type=F32)
    z3 = (gc * zc).reshape(nb, tt, cw)
    zbuf[:, halo:halo + tt, :] = z3
    z1 = zbuf[:, halo - 1:halo - 1 + tt, :]
    z2 = zbuf[:, halo - 2:halo - 2 + tt, :]
    conv = cw_ref[0:1, :] * z2 + cw_ref[1:2, :] * z1 + cw_ref[2:3, :] * z3
    gb = jnp.dot(h, win_ref[:, 2 * cw:3 * cw], preferred_element_type=F32)
    yc = gb * conv.reshape(rows, cw)
    ycn_ref[...] = _rms(yc, nconv_ref[...]).astype(BF16).reshape(nb, tt, cw)
    tail = zbuf[:, tt:tt + halo, :]
    zt_ref[...] = tail
    zbuf[:, 0:halo, :] = tail


def _ssm_kernel(u_ref, um_ref, kc_ref, pc_ref, rc_ref, a16_ref, y_ref, sl_ref,
                s_carry, ds_ref, sp_ref, p_s, r_s, t_s):
    th = pl.program_id(1)
    _, nb, cc, w = u_ref.shape
    nst = p_s.shape[1]
    half = nst // 2
    rows = nb * cc
    blk = 2 * LANE
    u = u_ref[0].reshape(rows, w)

    @pl.when(th == 0)
    def _():
        hch = kc_ref.shape[2]
        gpt = LANE // hch
        hlog = hch.bit_length() - 1
        glog = gpt.bit_length() - 1
        nlog = (pc_ref.shape[2] // 2).bit_length() - 1
        p_s[...] = jnp.where(_group_mask(p_s.shape, hlog, nlog, gpt),
                             _expand_cols(pc_ref[0], glog, nlog), 0.0).astype(BF16)
        r_s[...] = jnp.where(_group_mask(r_s.shape, nlog, hlog, gpt),
                             _expand_rows(rc_ref[0], glog, nlog), 0.0).astype(BF16)
        nlag = kc_ref.shape[1] // LANE
        o = _iota2((hch, LANE), 0)
        c = _iota2((hch, LANE), 1)
        spread = jnp.where((c & (hch - 1)) == o, 1.0, 0.0).astype(BF16)
        lagm = jnp.dot(kc_ref[0], spread, preferred_element_type=F32)
        r = _iota2(lagm.shape, 0)
        c = _iota2(lagm.shape, 1)
        lagm = jnp.where(((r >> hlog) & (gpt - 1)) == (c >> hlog), lagm, 0.0).astype(BF16)
        for dlt in range(nlag // 2):
            b0 = lagm[(2 * dlt) * LANE:(2 * dlt + 1) * LANE]
            b1 = lagm[(2 * dlt + 1) * LANE:(2 * dlt + 2) * LANE]
            b2 = lagm[(2 * dlt + 2) * LANE:(2 * dlt + 3) * LANE]
            t_s[dlt, 0:LANE, 0:LANE] = b1
            t_s[dlt, 0:LANE, LANE:blk] = b2
            t_s[dlt, LANE:blk, 0:LANE] = b0
            t_s[dlt, LANE:blk, LANE:blk] = b1
        ds_ref[:, 0:8, :] = jnp.dot(um_ref[0], p_s[...], preferred_element_type=F32).reshape(nb, 8, nst)
        s_carry[...] = ds_ref[:, 0:1, :]

    ds_ref[...] = jnp.dot(u, p_s[...], preferred_element_type=F32).reshape(nb, cc, nst)
    for tb in range(w // blk):
        acc = jnp.dot(u[:, 0:blk], t_s[tb], preferred_element_type=F32)
        for sb in range(1, tb + 1):
            acc = acc + jnp.dot(u[:, sb * blk:(sb + 1) * blk], t_s[tb - sb],
                                preferred_element_type=F32)
        y_ref[0, :, :, tb * blk:(tb + 1) * blk] = acc.reshape(nb, cc, blk)
    ar = a16_ref[0, 0:1, :].reshape(1, 1, half)
    ai = a16_ref[0, 1:2, :].reshape(1, 1, half)
    sr = s_carry[:, :, 0:half]
    si = s_carry[:, :, half:nst]
    for c in range(cc):
        sp_ref[:, c:c + 1, 0:half] = sr
        sp_ref[:, c:c + 1, half:nst] = si
        dr = ds_ref[:, c:c + 1, 0:half]
        di = ds_ref[:, c:c + 1, half:nst]
        sr, si = ar * sr - ai * si + dr, ar * si + ai * sr + di
    s_carry[:, :, 0:half] = sr
    s_carry[:, :, half:nst] = si
    sl_ref[0, :, :, 0:half] = sr
    sl_ref[0, :, :, half:nst] = si

    sp = sp_ref[...].reshape(rows, nst).astype(BF16)
    for tb in range(w // blk):
        acc = jnp.dot(sp, r_s[:, tb * blk:(tb + 1) * blk], preferred_element_type=F32)
        y_ref[0, :, :, tb * blk:(tb + 1) * blk] += acc.reshape(nb, cc, blk)


N_MIX_IN = 12


def _mix_rows(x, yssm, ut, ycn, dsk_ref, wglu_ref, bglu_ref, nssm_ref, wout_ref,
              nffn_ref, wr_ref, br_ref):
    ne = br_ref.shape[0]
    y = _gelu_tanh(yssm + dsk_ref[...] * ut.astype(F32))
    glu = jnp.dot(y.astype(BF16), wglu_ref[...], preferred_element_type=F32) + bglu_ref[...]
    o = y * jax.nn.sigmoid(glu)
    ysn = _rms(o, nssm_ref[...]).astype(BF16)
    mix = jnp.concatenate([ysn, ycn], axis=-1)
    x1 = x + jnp.dot(mix, wout_ref[...], preferred_element_type=F32)
    hf = _rms(x1, nffn_ref[...])
    hf_hi = hf.astype(BF16)
    hf_lo = (hf - hf_hi.astype(F32)).astype(BF16)
    r = hf.shape[0]
    part = jnp.dot(jnp.concatenate([hf_hi, hf_lo], axis=0), wr_ref[...], preferred_element_type=F32)
    logits = (part[0:r, 0:LANE] + part[0:r, LANE:2 * LANE]) + (part[r:2 * r, 0:LANE] + part[r:2 * r, LANE:2 * LANE])
    lt = logits.T[0:ne, :] + br_ref[...]
    iota = lax.broadcasted_iota(I32, lt.shape, 0)
    vals, idxs = [], []
    sel = jnp.zeros(lt.shape, F32)
    for _ in range(TOP_K):
        m = jnp.max(lt, axis=0, keepdims=True)
        ik = jnp.min(jnp.where(lt == m, iota, ne), axis=0, keepdims=True)
        vals.append(m)
        idxs.append(ik)
        hit = iota == ik
        sel = sel + jnp.where(hit, 1.0, 0.0)
        lt = jnp.where(hit, -jnp.inf, lt)
    es = [jnp.exp(v - vals[0]) for v in vals]
    tot = es[0] + es[1] + es[2] + es[3]
    idx = jnp.concatenate(idxs, axis=0)
    gates = jnp.concatenate([e / tot for e in es], axis=0)
    return x1, hf_hi, idx, gates, sel


def _mix_kernel_prompt(*refs):
    x_ref, yc_ref = refs[0], refs[1]
    x1_ref, hf_ref, idx_ref, gate_ref, cnt_ref, ybuf = refs[N_MIX_IN:]
    nj, nb, ncz, _ = yc_ref.shape
    for s in range(CHUNK):
        for j in range(nj):
            ybuf[j, :, pl.ds(s, ncz, stride=CHUNK), :] = yc_ref[j, :, :, s * LANE:(s + 1) * LANE]
    ut_ref, ycn_ref = refs[2], refs[3]
    tt, d = x_ref.shape[1], x_ref.shape[2]
    cw = nj * LANE
    nbc = TOK_TILE // tt
    for t in range(cnt_ref.shape[0]):
        b0, r0 = t * nbc, t * TOK_TILE
        yssm = jnp.concatenate([ybuf[j, b0:b0 + nbc].reshape(TOK_TILE, LANE) for j in range(nj)], axis=-1)
        x1, hf, idx, gates, sel = _mix_rows(
            x_ref[b0:b0 + nbc].reshape(TOK_TILE, d), yssm, ut_ref[b0:b0 + nbc].reshape(TOK_TILE, cw),
            ycn_ref[b0:b0 + nbc].reshape(TOK_TILE, cw), *refs[4:N_MIX_IN])
        x1_ref[r0:r0 + TOK_TILE, :] = x1
        hf_ref[r0:r0 + TOK_TILE, :] = hf
        idx_ref[:, r0:r0 + TOK_TILE] = idx
        gate_ref[:, r0:r0 + TOK_TILE] = gates
        cnt_ref[t] = jnp.sum(sel, axis=1, keepdims=True).astype(I32)


def _mix_kernel_sample(*refs):
    x1, hf, idx, gates, sel = _mix_rows(refs[0][0], refs[1][...], refs[2][0], refs[3][0],
                                        *refs[4:N_MIX_IN])
    x1_ref, hf_ref, idx_ref, gate_ref, cnt_ref = refs[N_MIX_IN + 5:]
    ns = x1.shape[0]
    x1_ref[...] = jnp.zeros(x1_ref.shape, x1_ref.dtype)
    hf_ref[...] = jnp.zeros(hf_ref.shape, hf_ref.dtype)
    idx_ref[...] = jnp.zeros(idx_ref.shape, idx_ref.dtype)
    gate_ref[...] = jnp.zeros(gate_ref.shape, gate_ref.dtype)
    x1_ref[0:ns, :] = x1
    hf_ref[0:ns, :] = hf
    idx_ref[:, 0:ns] = idx
    gate_ref[:, 0:ns] = gates
    cnt_ref[0] = jnp.sum(sel, axis=1, keepdims=True).astype(I32)


class _Table:
    def __init__(self, ref, offset):
        self.ref, self.offset = ref, offset

    def __getitem__(self, k):
        return self.ref[self.offset + k]


def _pack_tables(tables):
    offsets, total = [], 0
    for t in tables:
        offsets.append(total)
        total += t.shape[0]
    return jnp.concatenate(tables), tuple(offsets)


def _split_bf16(x, parts):
    out = []
    for _ in range(parts - 1):
        p = x.astype(BF16)
        out.append(p)
        x = x - p.astype(F32)
    out.append(x.astype(BF16))
    return out


def _run_tables(idx_ref, loff_ref, tile, n_valid, gate_ref=None):
    ne = loff_ref.shape[1]
    tt = idx_ref.shape[1]
    e_iota = lax.broadcasted_iota(I32, (ne, tt), 0)
    tok = tile * tt + lax.broadcasted_iota(I32, (1, tt), 1)
    valid = tok < n_valid
    hits = [jnp.logical_and(e_iota == idx_ref[k:k + 1, :], valid) for k in range(TOP_K)]
    sel = jnp.zeros((ne, tt), F32)
    for h in hits:
        sel = sel + jnp.where(h, 1.0, 0.0)
    before = lax.broadcasted_iota(I32, (tt, tt), 0) < lax.broadcasted_iota(I32, (tt, tt), 1)
    tri = jnp.where(before, 1.0, 0.0).astype(BF16)
    base = jnp.dot(sel.astype(BF16), tri, preferred_element_type=F32) + loff_ref[0].astype(F32)
    base = jnp.where(sel > 0.0, base + 1.0, 0.0)
    b_hi = BF16_EXACT * jnp.floor(base * (1.0 / BF16_EXACT))
    b_lo = base - b_hi
    halves = jnp.concatenate([b_hi, b_lo], axis=0).astype(BF16)
    if gate_ref is None:
        return halves, None
    gate_e = jnp.zeros((ne, tt), F32)
    for k, h in enumerate(hits):
        gate_e = gate_e + jnp.where(h, gate_ref[k:k + 1, :], 0.0)
    return halves, jnp.concatenate(_split_bf16(gate_e, GATE_PIECES), axis=0)


def _rows_onehot(halves, rgn_ref, row0, nrows, pieces=None):
    ne = rgn_ref.shape[2]
    tt = halves.shape[1]

    def own(copies):
        lo = jnp.concatenate([rgn_ref[0, 0:1, :]] * copies, axis=1)
        hi = jnp.concatenate([rgn_ref[0, 1:2, :]] * copies, axis=1)
        r_i = row0 + lax.broadcasted_iota(I32, (nrows, copies * ne), 0)
        return jnp.logical_and(r_i >= lo, r_i < hi)

    want = jnp.dot(jnp.where(own(2), 1.0, 0.0).astype(BF16), halves, preferred_element_type=F32)
    r_f = (row0 + 1 + lax.broadcasted_iota(I32, (nrows, tt), 0)).astype(F32)
    smat = jnp.where(want == r_f, 1.0, 0.0).astype(BF16)
    if pieces is None:
        return smat
    per = lax.dot_general(smat, pieces, (((1,), (1,)), ((), ())), preferred_element_type=F32)
    gcol = jnp.sum(jnp.where(own(GATE_PIECES), per, 0.0), axis=1, keepdims=True)
    return smat, gcol


def _window_copy(buf, slot, hbm, lo, g, sem, to_hbm, rows, align):
    src = buf.at[slot, pl.ds(pl.multiple_of(lo, align), rows)]
    dst = hbm.at[pl.ds(pl.multiple_of(g, align), rows)]
    if to_hbm:
        return pltpu.make_async_copy(src, dst, sem.at[slot])
    return pltpu.make_async_copy(dst, src, sem.at[slot])


def _start_windows(wg_ref, wl_ref, tw_ref, tile, slots, buf, slot, hbm, sem, to_hbm, rows, align):
    def per_window(w, c):
        k = tile * slots + w
        lo = w * rows if wl_ref is None else wl_ref[k]
        _window_copy(buf, slot, hbm, lo, wg_ref[k], sem, to_hbm, rows, align).start()
        return c

    lax.fori_loop(0, tw_ref[tile], per_window, 0)


def _wait_windows(count, buf, slot, hbm, sem, to_hbm, rows, align):
    batch = 8

    def many(w, c):
        _window_copy(buf, slot, hbm, 0, 0, sem, to_hbm, batch * rows, align).wait()
        return c

    def one(w, c):
        _window_copy(buf, slot, hbm, 0, 0, sem, to_hbm, rows, align).wait()
        return c

    lax.fori_loop(0, count // batch, many, 0)
    lax.fori_loop(0, count % batch, one, 0)


def _dispatch_kernel(offsets, tab_ref,
                     hf_ref, idx_ref, gate_ref, loff_ref, rgn_ref, xs_ref, buf, carry, sem):
    wg_ref, wl_ref, mlo_ref, mtg_ref, keep_ref, tw_ref, used_ref, nv_ref = [
        _Table(tab_ref, o) for o in offsets]
    i = pl.program_id(0)
    nt = pl.num_programs(0)
    ne = loff_ref.shape[1]
    tt, d = hf_ref.shape
    nrun = buf.shape[1]
    slot = i % 2

    @pl.when(i == 0)
    def _():
        carry[...] = jnp.zeros(carry.shape, carry.dtype)
        buf[...] = jnp.zeros(buf.shape, buf.dtype)

    halves, pieces = _run_tables(idx_ref, loff_ref, i, nv_ref[0], gate_ref)

    def emit(row0, nrows):
        smat, gcol = _rows_onehot(halves, rgn_ref, row0, nrows, pieces)
        xr = jnp.dot(smat, hf_ref[...], preferred_element_type=F32)
        lane0 = lax.broadcasted_iota(I32, (nrows, GATE_COLS), 1) == 0
        buf[slot, row0:row0 + nrows, 0:d // 2] = _pack_pairs(xr[:, :d // 2], xr[:, d // 2:])
        buf[slot, row0:row0 + nrows, d // 2:] = pltpu.bitcast(jnp.where(lane0, gcol, 0.0), U32)

    emit(0, DISPATCH_BASE_ROWS)
    past_base = used_ref[i] > DISPATCH_BASE_ROWS

    @pl.when(past_base)
    def _():
        emit(DISPATCH_BASE_ROWS, nrun - DISPATCH_BASE_ROWS)

    @pl.when(jnp.logical_not(past_base))
    def _():
        buf[slot, nrun - 2 * SUBLANE:nrun, :] = jnp.zeros((2 * SUBLANE, buf.shape[2]), buf.dtype)

    def merge(e, c):
        k = i * ne + e
        lo = pl.multiple_of(mlo_ref[k], SUBLANE)
        buf[slot, pl.ds(lo, SUBLANE), :] = buf[slot, pl.ds(lo, SUBLANE), :] | carry[e]
        tg = pl.multiple_of(mtg_ref[k], SUBLANE)
        carry[e] = jnp.where(keep_ref[k] > 0, carry[e], buf[slot, pl.ds(tg, SUBLANE), :])
        return c

    lax.fori_loop(0, ne, merge, 0, unroll=4)

    @pl.when(i > 0)
    def _():
        _wait_windows(tw_ref[i - 1], buf, 1 - slot, xs_ref, sem, True, WIN, SUBLANE)

    _start_windows(wg_ref, wl_ref, tw_ref, i, MAX_WINDOWS, buf, slot, xs_ref, sem, True, WIN, SUBLANE)

    @pl.when(i == nt - 1)
    def _():
        _wait_windows(tw_ref[i], buf, slot, xs_ref, sem, True, WIN, SUBLANE)


def _expert_weight_copies(wg_hbm, wd_hbm, wg_f32, wd_f32, sem, e, slot):
    return (pltpu.make_async_copy(wg_hbm.at[e], wg_f32.at[slot], sem.at[0, slot]),
            pltpu.make_async_copy(wd_hbm.at[e], wd_f32.at[slot], sem.at[1, slot]))


def _moe_kernel(offsets, tab_ref,
                x_ref, bg_ref, bd_ref, wg_hbm, wd_hbm, y_ref,
                wg_f32, wd_f32, wg_bf, wd_bf, sem):
    be_ref, bv_ref, first_ref, nxt_ref, slot_ref = [_Table(tab_ref, o) for o in offsets]
    i = pl.program_id(0)
    e = be_ref[i]
    dff = wd_bf.shape[0]
    bm = x_ref.shape[0]
    sub = MOE_SUB
    nw = x_ref.shape[1] - GATE_COLS

    @pl.when(first_ref[i] > 0)
    def _():
        slot = slot_ref[i]

        @pl.when(first_ref[i] > 1)
        def _():
            for cp in _expert_weight_copies(wg_hbm, wd_hbm, wg_f32, wd_f32, sem, e, slot):
                cp.start()

        for cp in _expert_weight_copies(wg_hbm, wd_hbm, wg_f32, wd_f32, sem, e, slot):
            cp.wait()

        @pl.when(nxt_ref[i] >= 0)
        def _():
            for cp in _expert_weight_copies(wg_hbm, wd_hbm, wg_f32, wd_f32, sem, nxt_ref[i], 1 - slot):
                cp.start()

        wg_bf[...] = wg_f32[slot].astype(BF16)
        wd_bf[...] = wd_f32[slot].astype(BF16)

    def rows(r0, nrows):
        sizes = [MOE_CHAIN] * (nrows // MOE_CHAIN) + ([nrows % MOE_CHAIN] if nrows % MOE_CHAIN else [])
        lo = r0
        for size in sizes:
            _chain(lo, size)
            lo += size

    def _chain(lo, n):
        live = lax.broadcasted_iota(I32, (n, 1), 0) + lo < bv_ref[i]
        x = jnp.where(live, _unpack_pairs(x_ref[lo:lo + n, 0:nw]), jnp.zeros((), BF16))
        route = jnp.where(live, pltpu.bitcast(x_ref[lo:lo + n, nw:], F32)[:, 0:1], 0.0)
        gu = jnp.dot(x, wg_bf[...], preferred_element_type=F32) + bg_ref[0]
        gate = jnp.minimum(gu[:, :dff], SWIGLU_LIMIT)
        up = jnp.clip(gu[:, dff:], -SWIGLU_LIMIT, SWIGLU_LIMIT)
        h = gate * jax.nn.sigmoid(SWIGLU_ALPHA * gate) * (up + 1.0)
        y = jnp.dot(h.astype(BF16), wd_bf[...], preferred_element_type=F32) + bd_ref[0]
        yr = (route * y).astype(BF16).astype(F32)
        half = yr.shape[1] // 2
        y_ref[lo:lo + n, :] = _pack_pairs(yr[:, :half], yr[:, half:])

    steps = [0, MOE_TAIL] + list(range(sub, bm + 1, sub))
    for lo_rows, used in zip([None] + steps[:-1], steps):

        @pl.when(jnp.logical_and(bv_ref[i] > lo_rows, bv_ref[i] <= used) if used else bv_ref[i] <= 0)
        def _(used=used):
            if used:
                rows(0, used)
            if used < bm:
                y_ref[used:bm, :] = jnp.zeros((bm - used, y_ref.shape[1]), y_ref.dtype)


def _combine_tile(step, nt):
    return (step + nt - 1) % nt


def _combine_kernel(offsets, tab_ref,
                    x1_ref, idx_ref, loff_ref, rgn_ref, nf_ref, yb_ref,
                    yp_ref, ys_ref, buf, acc, sem):
    wg_ref, tw_ref, nv_ref = [_Table(tab_ref, o) for o in offsets]
    s = pl.program_id(0)
    nt = pl.num_programs(0)
    tt, d = x1_ref.shape
    nrun = buf.shape[1]
    tile = _combine_tile(s, nt)
    slot = s % 2

    @pl.when(s == 0)
    def _():
        buf[...] = jnp.zeros(buf.shape, buf.dtype)
        _start_windows(wg_ref, None, tw_ref, tile, MAX_WINDOWS, buf, slot, yb_ref, sem, False,
                       WIN, SUBLANE)

    @pl.when(s + 1 < nt)
    def _():
        _start_windows(wg_ref, None, tw_ref, _combine_tile(s + 1, nt), MAX_WINDOWS, buf, 1 - slot,
                       yb_ref, sem, False, WIN, SUBLANE)

    halves, _ = _run_tables(idx_ref, loff_ref, tile, nv_ref[0])

    def gathered(row0, nrows):
        smat = _rows_onehot(halves, rgn_ref, row0, nrows)
        yrun = _unpack_pairs(buf[slot, row0:row0 + nrows])
        return lax.dot_general(smat, yrun, (((0,), (0,)), ((), ())), preferred_element_type=F32)

    _wait_windows(tw_ref[tile], buf, slot, yb_ref, sem, False, WIN, SUBLANE)
    acc[...] = gathered(0, COMBINE_BASE_ROWS)
    for row0 in range(COMBINE_BASE_ROWS, nrun, COMBINE_STEP_ROWS):
        @pl.when(WIN * tw_ref[tile] > row0)
        def _(row0=row0):
            acc[...] += gathered(row0, min(COMBINE_STEP_ROWS, nrun - row0))

    out = _rms(x1_ref[...] + acc[...], nf_ref[...])

    @pl.when(s == 0)
    def _():
        ys_ref[...] = out[0:ys_ref.shape[0], :]

    @pl.when(s > 0)
    def _():
        yp_ref[...] = out.reshape(yp_ref.shape)


def _ssm_matrices(a_re, a_im, log_dt, b_re, b_im, c_re, c_im):
    g, n = a_re.shape
    hch = b_re.shape[2]
    gpt = LANE // hch
    nj = g // gpt
    a = lax.complex(a_re, a_im)
    dta = a * jnp.exp(log_dt)[:, None]
    a_bar = jnp.exp(dta)
    bb = ((a_bar - 1.0) / a)[:, :, None] * lax.complex(b_re, b_im)
    cc = lax.complex(c_re, c_im)
    ks = jnp.arange(CHUNK + 1, dtype=F32)
    pw = jnp.exp(dta[None] * ks[:, None, None])
    kk = jnp.real(jnp.einsum('gon,kgn,gni->kgio', cc, pw[:CHUNK], bb))
    kk = jnp.concatenate([jnp.zeros_like(kk[:1]), kk], axis=0)
    kc = kk.reshape(CHUNK + 1, nj, gpt * hch, hch).transpose(1, 0, 2, 3)
    kc = kc.reshape(nj, (CHUNK + 1) * LANE, hch).astype(BF16)
    pwr, pwi = jnp.real(pw), jnp.imag(pw)
    bbr = jnp.real(bb).transpose(0, 2, 1).reshape(1, g * hch, n)
    bbi = jnp.imag(bb).transpose(0, 2, 1).reshape(1, g * hch, n)
    par = jnp.repeat(pwr[CHUNK - 1::-1][:CHUNK], hch, axis=1)
    pai = jnp.repeat(pwi[CHUNK - 1::-1][:CHUNK], hch, axis=1)
    pc = jnp.concatenate([par * bbr - pai * bbi, par * bbi + pai * bbr], axis=-1)
    pc = pc.reshape(CHUNK, nj, gpt * hch, 2 * n).transpose(1, 0, 2, 3)
    pc = pc.reshape(nj, CHUNK * LANE, 2 * n).astype(BF16)
    ccr = jnp.real(cc).transpose(2, 0, 1).reshape(n, 1, g * hch)
    cci = jnp.imag(cc).transpose(2, 0, 1).reshape(n, 1, g * hch)
    qar = jnp.repeat(pwr[1:CHUNK + 1].transpose(2, 0, 1), hch, axis=2)
    qai = jnp.repeat(pwi[1:CHUNK + 1].transpose(2, 0, 1), hch, axis=2)
    rc = jnp.stack([ccr * qar - cci * qai, -(ccr * qai + cci * qar)], axis=0)
    rc = rc.reshape(2 * n, CHUNK, nj, gpt * hch).transpose(2, 0, 1, 3)
    rc = rc.reshape(nj, 2 * n, CHUNK * LANE).astype(BF16)
    a16 = pw[CHUNK].reshape(nj, 1, gpt * n)
    a16 = jnp.concatenate([jnp.real(a16), jnp.imag(a16)], axis=1)
    bc = jnp.stack([jnp.real(bb), jnp.imag(bb)], axis=0).transpose(1, 3, 0, 2)
    bc = bc.reshape(g * hch, 2 * n).astype(BF16)
    c2 = jnp.stack([jnp.real(cc), -jnp.imag(cc)], axis=0).transpose(0, 3, 1, 2)
    c2 = c2.reshape(2 * n, g * hch).astype(BF16)
    abr = jnp.real(a_bar).reshape(1, g * n)
    abi = jnp.imag(a_bar).reshape(1, g * n)
    return kc, pc, rc, a16, bc, c2, abr, abi


def _full(shape):
    return pl.BlockSpec(shape, lambda *_: (0,) * len(shape))


def kernel(x_prompt, x_sample, state_ssm_re, state_ssm_im, state_conv, meta_tokens, norm_mix, w_in,
           ssm_a_re, ssm_a_im, ssm_log_dt, ssm_b_re, ssm_b_im, ssm_c_re, ssm_c_im, ssm_d, w_glu, b_glu,
           conv_w, norm_out_ssm, norm_out_conv, w_out, norm_ffn, w_router, b_router, w_gate_up,
           b_gate_up, w_down, b_down, norm_final):
    nb, seq, d = x_prompt.shape
    ns = x_sample.shape[0]
    depth, _, g, n = state_ssm_re.shape
    assert depth == 1 and x_sample.shape[1] == 1 and meta_tokens.shape[0] == CHUNK
    cw = conv_w.shape[2]
    nj = cw // LANE
    ne = w_router.shape[2]
    dff = w_down.shape[2]
    nst = g * n
    nbt = nb // 2
    tt = 256
    n_chunks = seq // CHUNK
    tp = nb * seq
    tall = tp + ns
    tm = 128
    nbm = nb
    rows_p = nbm * tm
    assert rows_p % TOK_TILE == 0 and TOK_TILE % tm == 0 and ns <= TOK_TILE and d % 2 == 0
    n_tiles = tp // TOK_TILE + 1
    ta = n_tiles * TOK_TILE

    kc, pc, rc, a16, bc, c2, abr, abi = _ssm_matrices(
        ssm_a_re[0], ssm_a_im[0], ssm_log_dt[0], ssm_b_re[0], ssm_b_im[0], ssm_c_re[0], ssm_c_im[0])
    win_bf = w_in[0].astype(BF16)
    nmix = norm_mix[0].reshape(1, d)
    nconv = norm_out_conv[0].reshape(1, cw)
    cwt = conv_w[0]

    xsm = jnp.concatenate([x_sample.reshape(ns, d), meta_tokens], axis=0)
    nsm = ns + CHUNK
    s0r = state_ssm_re[0].reshape(ns, nst)
    s0i = state_ssm_im[0].reshape(ns, nst)
    buf0 = state_conv[0, :, 0, :]
    buf1 = state_conv[0, :, 1, :]
    u_sm, z_sm, y_s, ycn_s, sr_s, si_s = pl.pallas_call(
        _small_front_kernel,
        out_shape=(jax.ShapeDtypeStruct((nsm, cw), F32), jax.ShapeDtypeStruct((nsm, cw), F32),
                   jax.ShapeDtypeStruct((ns, cw), F32), jax.ShapeDtypeStruct((ns, cw), BF16),
                   jax.ShapeDtypeStruct((ns, nst), F32), jax.ShapeDtypeStruct((ns, nst), F32)),
        scratch_shapes=[pltpu.VMEM((cw, 2 * nst), BF16), pltpu.VMEM((2 * nst, cw), BF16)],
        **_params("small_front", None),
    )(xsm, nmix, win_bf, s0r, s0i, buf0, buf1, cwt, bc, c2, abr, abi, nconv)
    u_meta = u_sm[ns:]
    z_meta8 = z_sm[ns + CHUNK - 8:]
    new_conv_s = jnp.stack([buf1, z_sm[:ns]], axis=1)[None]
    new_re_s = sr_s.reshape(1, ns, g, n)
    new_im_s = si_s.reshape(1, ns, g, n)

    wch = CHUNK * LANE
    u4c, u_tok, ycn_p, ztail = pl.pallas_call(
        _front_kernel,
        grid=(nb // nbt, seq // tt),
        in_specs=[pl.BlockSpec((nbt, tt, d), lambda b, i: (b, i, 0)),
                  _full((1, d)), _full((d, 4 * cw)), _full((8, cw)), _full((3, cw)), _full((1, cw))],
        out_specs=(pl.BlockSpec((nj, nbt, tt // CHUNK, wch), lambda b, i: (0, b, i, 0)),
                   pl.BlockSpec((nbt, tt, cw), lambda b, i: (b, i, 0)),
                   pl.BlockSpec((nbt, tt, cw), lambda b, i: (b, i, 0)),
                   pl.BlockSpec((nbt, 8, cw), lambda b, i: (b, 0, 0))),
        out_shape=(jax.ShapeDtypeStruct((nj, nb, n_chunks, wch), BF16),
                   jax.ShapeDtypeStruct((nb, seq, cw), BF16),
                   jax.ShapeDtypeStruct((nb, seq, cw), BF16),
                   jax.ShapeDtypeStruct((nb, 8, cw), F32)),
        scratch_shapes=[pltpu.VMEM((nbt, tt + 8, cw), F32), pltpu.VMEM((nj, nbt, tt, LANE), F32)],
        **_params("front", ("arbitrary", "arbitrary")),
    )(x_prompt, nmix, win_bf, z_meta8, cwt, nconv)
    new_conv_p = ztail[:, 6:8, :][None]

    cc = n_chunks // 2
    um = u_meta.reshape(CHUNK, nj, LANE).transpose(1, 0, 2).reshape(nj, 1, wch)
    um = jnp.broadcast_to(um, (nj, 8 * nb, wch)).astype(BF16)
    gpt = g // nj
    nstj = 2 * gpt * n
    hch = cw // g
    y4c, s_last = pl.pallas_call(
        _ssm_kernel,
        grid=(nj, n_chunks // cc),
        in_specs=[pl.BlockSpec((1, nb, cc, wch), lambda j, t: (j, 0, t, 0)),
                  pl.BlockSpec((1, 8 * nb, wch), lambda j, t: (j, 0, 0)),
                  pl.BlockSpec((1, (CHUNK + 1) * LANE, hch), lambda j, t: (j, 0, 0)),
                  pl.BlockSpec((1, wch, 2 * n), lambda j, t: (j, 0, 0)),
                  pl.BlockSpec((1, 2 * n, wch), lambda j, t: (j, 0, 0)),
                  pl.BlockSpec((1, 2, nstj // 2), lambda j, t: (j, 0, 0))],
        out_specs=(pl.BlockSpec((1, nb, cc, wch), lambda j, t: (j, 0, t, 0)),
                   pl.BlockSpec((1, nb, 1, nstj), lambda j, t: (j, 0, 0, 0))),
        out_shape=(jax.ShapeDtypeStruct((nj, nb, n_chunks, wch), F32),
                   jax.ShapeDtypeStruct((nj, nb, 1, nstj), F32)),
        scratch_shapes=[pltpu.VMEM((nb, 1, nstj), F32), pltpu.VMEM((nb, cc, nstj), F32),
                        pltpu.VMEM((nb, cc, nstj), F32),
                        pltpu.VMEM((wch, nstj), BF16), pltpu.VMEM((nstj, wch), BF16),
                        pltpu.VMEM((CHUNK // 2, 2 * LANE, 2 * LANE), BF16)],
        **_params("ssm", ("parallel", "arbitrary")),
    )(u4c, um, kc, pc, rc, a16)
    sl = s_last.reshape(nj, nb, 2, gpt, n)
    new_re_p = sl[:, :, 0].transpose(1, 0, 2, 3).reshape(1, nb, g, n)
    new_im_p = sl[:, :, 1].transpose(1, 0, 2, 3).reshape(1, nb, g, n)

    dsk = ssm_d[0].reshape(1, cw)
    wglu_bf = w_glu[0].astype(BF16)
    bglu = b_glu[0].reshape(1, cw)
    nssm = norm_out_ssm[0].reshape(1, cw)
    wout_bf = w_out[0].astype(BF16)
    nffn = norm_ffn[0].reshape(1, d)
    wr_pad = jnp.zeros((d, LANE), F32).at[:, :ne].set(w_router[0])
    wr_hi = wr_pad.astype(BF16)
    wr_lo = (wr_pad - wr_hi.astype(F32)).astype(BF16)
    br = b_router[0].reshape(ne, 1)
    wr2 = jnp.concatenate([wr_hi, wr_lo], axis=1)
    mix_w = (dsk, wglu_bf, bglu, nssm, wout_bf, nffn, wr2, br)
    mix_w_specs = [_full((1, cw)), _full((cw, cw)), _full((1, cw)), _full((1, cw)), _full((2 * cw, d)),
                   _full((1, d)), _full((d, 2 * LANE)), _full((ne, 1))]
    assert 4 + len(mix_w) == N_MIX_IN
    mix_out_shape = (jax.ShapeDtypeStruct((ta, d), F32), jax.ShapeDtypeStruct((ta, d), BF16),
                     jax.ShapeDtypeStruct((TOP_K, ta), I32), jax.ShapeDtypeStruct((TOP_K, ta), F32),
                     jax.ShapeDtypeStruct((n_tiles, ne, 1), I32))
    tpm = rows_p // TOK_TILE
    nbg = nb // nbm
    x1_all, hf_all, idx_all, gate_all, cnt = pl.pallas_call(
        _mix_kernel_prompt,
        grid=(seq // tm, nbg),
        in_specs=[pl.BlockSpec((nbm, tm, d), lambda i, b: (b, i, 0)),
                  pl.BlockSpec((nj, nbm, tm // CHUNK, wch), lambda i, b: (0, b, i, 0)),
                  pl.BlockSpec((nbm, tm, cw), lambda i, b: (b, i, 0)),
                  pl.BlockSpec((nbm, tm, cw), lambda i, b: (b, i, 0))] + mix_w_specs,
        out_specs=(pl.BlockSpec((rows_p, d), lambda i, b: (i * nbg + b, 0)),
                   pl.BlockSpec((rows_p, d), lambda i, b: (i * nbg + b, 0)),
                   pl.BlockSpec((TOP_K, rows_p), lambda i, b: (0, i * nbg + b)),
                   pl.BlockSpec((TOP_K, rows_p), lambda i, b: (0, i * nbg + b)),
                   pl.BlockSpec((tpm, ne, 1), lambda i, b: (i * nbg + b, 0, 0))),
        out_shape=mix_out_shape,
        scratch_shapes=[pltpu.VMEM((nj, nbm, tm, LANE), F32)],
        **_params("mix_prompt", ("parallel", "parallel")),
    )(x_prompt, y4c, u_tok, ycn_p, *mix_w)

    last = n_tiles - 1
    any_spec = pl.BlockSpec(memory_space=pl.ANY)
    x1_all, hf_all, idx_all, gate_all, cnt = pl.pallas_call(
        _mix_kernel_sample,
        grid=(1,),
        in_specs=[_full((1, ns, d)), _full((ns, cw)), _full((1, ns, cw)),
                  _full((1, ns, cw))] + mix_w_specs + [any_spec] * 5,
        out_specs=(pl.BlockSpec((TOK_TILE, d), lambda i: (last, 0)),
                   pl.BlockSpec((TOK_TILE, d), lambda i: (last, 0)),
                   pl.BlockSpec((TOP_K, TOK_TILE), lambda i: (0, last)),
                   pl.BlockSpec((TOP_K, TOK_TILE), lambda i: (0, last)),
                   pl.BlockSpec((1, ne, 1), lambda i: (last, 0, 0))),
        out_shape=mix_out_shape,
        input_output_aliases={N_MIX_IN + k: k for k in range(5)},
        **_params("mix_sample", ("arbitrary",)),
    )(x_sample.reshape(1, ns, d), y_s, u_sm[:ns].astype(BF16).reshape(1, ns, cw),
      ycn_s.reshape(1, ns, cw), *mix_w, x1_all, hf_all, idx_all, gate_all, cnt)

    bm = MOE_ROWS
    cnt2 = cnt.reshape(n_tiles, ne)
    before = jnp.cumsum(cnt2, axis=0) - cnt2
    count = jnp.sum(cnt2, axis=0)
    padded = ((count + WIN + bm - 1) // bm) * bm
    pend = jnp.cumsum(padded)
    pstart = pend - padded
    phase = before % SUBLANE
    span = jnp.where(cnt2 > 0, phase + cnt2, 0)
    gstart = (pstart[None, :] + before - phase).astype(I32).reshape(-1)
    nwin = ((span + WIN - 1) // WIN).astype(I32)
    reg8 = ((span + SUBLANE - 1) // SUBLANE) * SUBLANE
    loff = (jnp.cumsum(reg8, axis=1) - reg8).astype(I32)
    tail = jnp.where(span % SUBLANE != 0, loff + (span // SUBLANE) * SUBLANE, -1).astype(I32)
    twin = jnp.sum(nwin, axis=1).astype(I32)
    n_blocks = (ta * TOP_K + ne * (WIN + bm - 1) + bm - 1) // bm
    cap = n_blocks * bm
    blk0 = jnp.arange(n_blocks, dtype=I32) * bm
    blk_e = jnp.minimum(jnp.sum((pend[None, :] <= blk0[:, None]).astype(I32), axis=1), ne - 1)
    e_ar = jnp.arange(ne, dtype=I32)
    blk_hot = blk_e[:, None] == e_ar[None, :]

    def _of_block(per_expert):
        return jnp.sum(jnp.where(blk_hot, per_expert[None, :], 0), axis=1)

    blk_valid = jnp.clip(_of_block(count) - (blk0 - _of_block(pstart)), 0, bm).astype(I32)
    has = count > 0
    later = jnp.logical_and(e_ar[None, :] > e_ar[:, None], has[None, :])
    nxt_e = jnp.min(jnp.where(later, e_ar[None, :], ne), axis=1)
    nxt_e = jnp.where(nxt_e < ne, nxt_e, -1)
    ordinal = jnp.cumsum(has.astype(I32)) - 1
    is_first = jnp.logical_and(blk_valid > 0, blk0 == _of_block(pstart))
    blk_first = jnp.where(is_first, jnp.where(_of_block(ordinal) == 0, 2, 1), 0).astype(I32)
    blk_next = _of_block(nxt_e).astype(I32)
    blk_slot = (_of_block(ordinal) % 2).astype(I32)
    nvalid = jnp.full((1,), tall, I32)
    loff_al = (WIN * (jnp.cumsum(nwin, axis=1) - nwin)).astype(I32)

    def _window_list(nw, slots, rows, first_hbm, first_buf):
        wcum = jnp.cumsum(nw, axis=1)
        wslot = jnp.arange(slots, dtype=I32)
        w_hot = jnp.logical_and(wslot[None, :, None] >= (wcum - nw)[:, None, :],
                                wslot[None, :, None] < wcum[:, None, :])

        def _of_window(per_tile_expert):
            return jnp.sum(jnp.where(w_hot, per_tile_expert[:, None, :], 0), axis=2)

        w_in_run = wslot[None, :] - _of_window(wcum - nw)
        return [(_of_window(f) + rows * w_in_run).astype(I32).reshape(-1) for f in (first_hbm, first_buf)]

    w_hbm, w_buf = _window_list(nwin, MAX_WINDOWS, WIN, gstart.reshape(n_tiles, ne), loff)
    nrun_d = _dispatch_run_rows(ne)
    zero_grp, spare_grp = nrun_d - 2 * SUBLANE, nrun_d - SUBLANE
    m_lo = jnp.where(nwin > 0, loff, spare_grp).astype(I32).reshape(-1)
    m_tg = jnp.where(jnp.logical_and(nwin > 0, tail >= 0), tail, zero_grp).astype(I32).reshape(-1)
    m_keep = (nwin == 0).astype(I32).reshape(-1)
    used_d = jnp.sum(reg8, axis=1).astype(I32)
    tab_d, off_d = _pack_tables((w_hbm, w_buf, m_lo, m_tg, m_keep, twin, used_d, nvalid))
    tab_c, off_c = _pack_tables((w_hbm, twin, nvalid))
    tab_m, off_m = _pack_tables((blk_e, blk_valid, blk_first, blk_next, blk_slot))
    rowoff_d = (loff + phase).astype(I32).reshape(n_tiles, ne, 1)
    rowoff_c = (loff_al + phase).astype(I32).reshape(n_tiles, ne, 1)
    rgn_d = jnp.stack([loff, loff + reg8], axis=1).astype(I32)
    rgn_c = jnp.stack([loff_al, loff_al + WIN * nwin], axis=1).astype(I32)
    xw = d // 2 + GATE_COLS

    xs = pl.pallas_call(
        functools.partial(_dispatch_kernel, off_d),
        grid_spec=pltpu.PrefetchScalarGridSpec(
            num_scalar_prefetch=1,
            grid=(n_tiles,),
            in_specs=[pl.BlockSpec((TOK_TILE, d), lambda i, *_: (i, 0)),
                      pl.BlockSpec((TOP_K, TOK_TILE), lambda i, *_: (0, i)),
                      pl.BlockSpec((TOP_K, TOK_TILE), lambda i, *_: (0, i)),
                      pl.BlockSpec((1, ne, 1), lambda i, *_: (i, 0, 0)),
                      pl.BlockSpec((1, 2, ne), lambda i, *_: (i, 0, 0))],
            out_specs=pl.BlockSpec(memory_space=pl.ANY),
            scratch_shapes=[pltpu.VMEM((2, _dispatch_run_rows(ne), xw), U32),
                            pltpu.VMEM((ne, SUBLANE, xw), U32), pltpu.SemaphoreType.DMA((2,))]),
        out_shape=jax.ShapeDtypeStruct((cap, xw), U32),
        **_params("dispatch", ("arbitrary",)),
    )(tab_d, hf_all, idx_all, gate_all, rowoff_d, rgn_d)

    yb = pl.pallas_call(
        functools.partial(_moe_kernel, off_m),
        grid_spec=pltpu.PrefetchScalarGridSpec(
            num_scalar_prefetch=1,
            grid=(n_blocks,),
            in_specs=[pl.BlockSpec((bm, xw), lambda i, tab: (i, 0)),
                      pl.BlockSpec((1, 1, 2 * dff), lambda i, tab: (tab[off_m[0] + i], 0, 0)),
                      pl.BlockSpec((1, 1, d), lambda i, tab: (tab[off_m[0] + i], 0, 0)),
                      pl.BlockSpec(memory_space=pl.ANY), pl.BlockSpec(memory_space=pl.ANY)],
            out_specs=pl.BlockSpec((bm, d // 2), lambda i, *_: (i, 0)),
            scratch_shapes=[pltpu.VMEM((2, d, 2 * dff), F32), pltpu.VMEM((2, dff, d), F32),
                            pltpu.VMEM((d, 2 * dff), BF16), pltpu.VMEM((dff, d), BF16),
                            pltpu.SemaphoreType.DMA((2, 2))]),
        out_shape=jax.ShapeDtypeStruct((cap, d // 2), U32),
        **_params("moe", ("arbitrary",)),
    )(tab_m, xs, b_gate_up[0].reshape(ne, 1, 2 * dff),
      b_down[0].reshape(ne, 1, d), w_gate_up[0], w_down[0])

    nfin = norm_final.reshape(1, d)
    nbh = TOK_TILE // tm

    def _tile_of(s):
        return (s + n_tiles - 1) % n_tiles

    tiles_per_time = nbg * tpm

    def _yp_index(s, *_):
        t = jnp.maximum(s - 1, 0)
        return (t % tiles_per_time, t // tiles_per_time, 0)

    y_p, y_sm = pl.pallas_call(
        functools.partial(_combine_kernel, off_c),
        grid_spec=pltpu.PrefetchScalarGridSpec(
            num_scalar_prefetch=1,
            grid=(n_tiles,),
            in_specs=[pl.BlockSpec((TOK_TILE, d), lambda s, *_: (_tile_of(s), 0)),
                      pl.BlockSpec((TOP_K, TOK_TILE), lambda s, *_: (0, _tile_of(s))),
                      pl.BlockSpec((1, ne, 1), lambda s, *_: (_tile_of(s), 0, 0)),
                      pl.BlockSpec((1, 2, ne), lambda s, *_: (_tile_of(s), 0, 0)),
                      pl.BlockSpec((1, d), lambda s, *_: (0, 0)),
                      pl.BlockSpec(memory_space=pl.ANY)],
            out_specs=(pl.BlockSpec((nbh, tm, d), _yp_index),
                       pl.BlockSpec((ns, d), lambda s, *_: (0, 0))),
            scratch_shapes=[pltpu.VMEM((2, _combine_run_rows(ne), d // 2), U32),
                            pltpu.VMEM((TOK_TILE, d), F32), pltpu.SemaphoreType.DMA((2,))]),
        out_shape=(jax.ShapeDtypeStruct((nb, seq, d), F32), jax.ShapeDtypeStruct((ns, d), F32)),
        **_params("combine", ("arbitrary",)),
    )(tab_c, x1_all, idx_all, rowoff_c, rgn_c, nfin, yb)

    return (y_p, y_sm.reshape(ns, 1, d), new_re_p, new_im_p, new_conv_p,
            new_re_s, new_im_s, new_conv_s)
```

```python
import functools
import math

import jax
import jax.numpy as jnp
from jax import lax
from jax.experimental import pallas as pl
from jax.experimental.pallas import tpu as pltpu

F32 = jnp.float32
BF16 = jnp.bfloat16
U32 = jnp.uint32
I32 = jnp.int32
EPS = 1e-5
CHUNK = 16
LANE = 128
TOP_K = 4
SWIGLU_LIMIT = 7.0
SWIGLU_ALPHA = 1.702
MOE_ROWS = 1024
MOE_SUB = 256
MOE_CHAIN = 512
MOE_X_SLOTS = 3
TOK_TILE = 256
WIN = 32
SUBLANE = 8
GATE_COLS = LANE


MAX_WINDOWS = 72


def _dispatch_run_rows(ne):
    return TOP_K * TOK_TILE + ne * 2 * (SUBLANE - 1) + WIN


def _combine_run_rows(ne):
    return -(-(TOP_K * TOK_TILE + ne * (SUBLANE - 1 + WIN - 1)) // WIN) * WIN


DISPATCH_BASE_ROWS = TOP_K * TOK_TILE + 256
COMBINE_BASE_ROWS = TOP_K * TOK_TILE + 768
COMBINE_STEP_ROWS = 256
GATE_PIECES = 3
HI_MASK = 0xFFFF0000
BF16_EXACT = 256.0
MIB = 1024 * 1024
VMEM_MIB = {"small_front": 56, "front": 52, "ssm": 56, "mix_prompt": 56, "mix_sample": 32,
            "dispatch": 40, "moe": 58, "combine": 48}


def _rms(x, g):
    return x * lax.rsqrt(jnp.mean(x * x, axis=-1, keepdims=True) + EPS) * g


def _gelu_tanh(x):
    c = math.sqrt(2.0 / math.pi)
    return 0.5 * x * (1.0 + jnp.tanh(c * (x + 0.044715 * (x * x * x))))


def _params(name, sem):
    return dict(name=name, compiler_params=pltpu.CompilerParams(
        dimension_semantics=sem, vmem_limit_bytes=VMEM_MIB[name] * MIB))


def _pack_pairs(a, b):
    return (pltpu.bitcast(a, U32) >> 16) | (pltpu.bitcast(b, U32) & jnp.uint32(HI_MASK))


def _unpack_pairs(w):
    lo = pltpu.bitcast(w << 16, F32)
    hi = pltpu.bitcast(w & jnp.uint32(HI_MASK), F32)
    return jnp.concatenate([lo, hi], axis=-1).astype(BF16)


def _iota2(shape, axis):
    return lax.broadcasted_iota(I32, shape, axis)


def _expand_cols(compact, reps_log2, n_log2):
    q = _iota2((compact.shape[1], 2 << (reps_log2 + n_log2)), 0)
    c = _iota2((compact.shape[1], 2 << (reps_log2 + n_log2)), 1)
    nmask = (1 << n_log2) - 1
    same = jnp.logical_and((q >> n_log2) == (c >> (reps_log2 + n_log2)), (q & nmask) == (c & nmask))
    return jnp.dot(compact, jnp.where(same, 1.0, 0.0).astype(BF16), preferred_element_type=F32)


def _expand_rows(compact, reps_log2, n_log2):
    r = _iota2((2 << (reps_log2 + n_log2), compact.shape[0]), 0)
    q = _iota2((2 << (reps_log2 + n_log2), compact.shape[0]), 1)
    nmask = (1 << n_log2) - 1
    same = jnp.logical_and((r >> (reps_log2 + n_log2)) == (q >> n_log2), (r & nmask) == (q & nmask))
    return jnp.dot(jnp.where(same, 1.0, 0.0).astype(BF16), compact, preferred_element_type=F32)


def _group_mask(shape, row_shift, col_shift, ngroups):
    r = _iota2(shape, 0)
    c = _iota2(shape, 1)
    return ((r >> row_shift) & (ngroups - 1)) == ((c >> col_shift) & (ngroups - 1))


def _small_front_kernel(x_ref, nmix_ref, win_ref, s0r_ref, s0i_ref, b0_ref, b1_ref, cw_ref,
                        bc_ref, cc_ref, abr_ref, abi_ref, nconv_ref,
                        u_ref, z_ref, y_ref, ycn_ref, sr_ref, si_ref, bdb_ref, cm_ref):
    ns, nst = s0r_ref.shape
    cw = u_ref.shape[1]
    n = bc_ref.shape[1] // 2
    nlog = n.bit_length() - 1
    glog = (nst // n).bit_length() - 1
    hlog = (cw >> glog).bit_length() - 1
    bdb_ref[...] = jnp.where(_group_mask(bdb_ref.shape, hlog, nlog, 1 << glog),
                             _expand_cols(bc_ref[...], glog, nlog), 0.0).astype(BF16)
    cm_ref[...] = jnp.where(_group_mask(cm_ref.shape, nlog, hlog, 1 << glog),
                            _expand_rows(cc_ref[...], glog, nlog), 0.0).astype(BF16)
    h = _rms(x_ref[...], nmix_ref[...]).astype(BF16)
    proj = jnp.dot(h, win_ref[...], preferred_element_type=F32)
    u = proj[:, 0:cw]
    zc = proj[:, cw:2 * cw]
    gb = proj[:, 2 * cw:3 * cw]
    gc = proj[:, 3 * cw:4 * cw]
    z = gc * zc
    u_ref[...] = u
    z_ref[...] = z
    bu = jnp.dot(u[:ns].astype(BF16), bdb_ref[...], preferred_element_type=F32)
    abr = abr_ref[...]
    abi = abi_ref[...]
    s0r = s0r_ref[...]
    s0i = s0i_ref[...]
    sr = abr * s0r - abi * s0i + bu[:, :nst]
    si = abr * s0i + abi * s0r + bu[:, nst:]
    sr_ref[...] = sr
    si_ref[...] = si
    scat = jnp.concatenate([sr, si], axis=-1).astype(BF16)
    y_ref[...] = jnp.dot(scat, cm_ref[...], preferred_element_type=F32)
    conv = cw_ref[0:1, :] * b0_ref[...] + cw_ref[1:2, :] * b1_ref[...] + cw_ref[2:3, :] * z[:ns]
    ycn_ref[...] = _rms(gb[:ns] * conv, nconv_ref[...]).astype(BF16)


def _front_kernel(x_ref, nmix_ref, win_ref, zm_ref, cw_ref, nconv_ref,
                  uc_ref, ut_ref, ycn_ref, zt_ref, zbuf, ubuf):
    i = pl.program_id(1)
    nb, tt, d = x_ref.shape
    cw = ycn_ref.shape[2]
    rows = nb * tt
    ncz = tt // CHUNK
    halo = SUBLANE

    @pl.when(i == 0)
    def _():
        zbuf[:, 0:halo, :] = jnp.broadcast_to(zm_ref[...][None], (nb, halo, cw))

    h = _rms(x_ref[...].reshape(rows, d), nmix_ref[...]).astype(BF16)
    u = jnp.dot(h, win_ref[:, 0:cw], preferred_element_type=F32)
    ut_ref[...] = u.astype(BF16).reshape(nb, tt, cw)
    for j in range(cw // LANE):
        ubuf[j] = u[:, j * LANE:(j + 1) * LANE].reshape(nb, tt, LANE)
    for s in range(CHUNK):
        for j in range(cw // LANE):
            piece = ubuf[j, :, pl.ds(s, ncz, stride=CHUNK), :]
            uc_ref[j, :, :, s * LANE:(s + 1) * LANE] = piece.astype(BF16)
    zc = jnp.dot(h, win_ref[:, cw:2 * cw], preferred_element_type=F32)
    gc = jnp.dot(h, win_ref[:, 3 * cw:4 * cw], preferred_element_type=F32)
    z3 = (gc * zc).reshape(nb, tt, cw)
    zbuf[:, halo:halo + tt, :] = z3
    z1 = zbuf[:, halo - 1:halo - 1 + tt, :]
    z2 = zbuf[:, halo - 2:halo - 2 + tt, :]
    conv = cw_ref[0:1, :] * z2 + cw_ref[1:2, :] * z1 + cw_ref[2:3, :] * z3
    gb = jnp.dot(h, win_ref[:, 2 * cw:3 * cw], preferred_element_type=F32)
    yc = gb * conv.reshape(rows, cw)
    ycn_ref[...] = _rms(yc, nconv_ref[...]).astype(BF16).reshape(nb, tt, cw)
    tail = zbuf[:, tt:tt + halo, :]
    zt_ref[...] = tail
    zbuf[:, 0:halo, :] = tail


def _ssm_kernel(u_ref, um_ref, kc_ref, pc_ref, rc_ref, a16_ref, y_ref, sl_ref,
                s_carry, ds_ref, sp_ref, p_s, r_s, t_s):
    th = pl.program_id(1)
    _, nb, cc, w = u_ref.shape
    nst = p_s.shape[1]
    half = nst // 2
    rows = nb * cc
    blk = 2 * LANE
    u = u_ref[0].reshape(rows, w)

    @pl.when(th == 0)
    def _():
        hch = kc_ref.shape[2]
        gpt = LANE // hch
        hlog = hch.bit_length() - 1
        glog = gpt.bit_length() - 1
        nlog = (pc_ref.shape[2] // 2).bit_length() - 1
        p_s[...] = jnp.where(_group_mask(p_s.shape, hlog, nlog, gpt),
                             _expand_cols(pc_ref[0], glog, nlog), 0.0).astype(BF16)
        r_s[...] = jnp.where(_group_mask(r_s.shape, nlog, hlog, gpt),
                             _expand_rows(rc_ref[0], glog, nlog), 0.0).astype(BF16)
        nlag = kc_ref.shape[1] // LANE
        o = _iota2((hch, LANE), 0)
        c = _iota2((hch, LANE), 1)
        spread = jnp.where((c & (hch - 1)) == o, 1.0, 0.0).astype(BF16)
        lagm = jnp.dot(kc_ref[0], spread, preferred_element_type=F32)
        r = _iota2(lagm.shape, 0)
        c = _iota2(lagm.shape, 1)
        lagm = jnp.where(((r >> hlog) & (gpt - 1)) == (c >> hlog), lagm, 0.0).astype(BF16)
        for dlt in range(nlag // 2):
            b0 = lagm[(2 * dlt) * LANE:(2 * dlt + 1) * LANE]
            b1 = lagm[(2 * dlt + 1) * LANE:(2 * dlt + 2) * LANE]
            b2 = lagm[(2 * dlt + 2) * LANE:(2 * dlt + 3) * LANE]
            t_s[dlt, 0:LANE, 0:LANE] = b1
            t_s[dlt, 0:LANE, LANE:blk] = b2
            t_s[dlt, LANE:blk, 0:LANE] = b0
            t_s[dlt, LANE:blk, LANE:blk] = b1
        ds_ref[:, 0:8, :] = jnp.dot(um_ref[0], p_s[...], preferred_element_type=F32).reshape(nb, 8, nst)
        s_carry[...] = ds_ref[:, 0:1, :]

    ds_ref[...] = jnp.dot(u, p_s[...], preferred_element_type=F32).reshape(nb, cc, nst)
    for tb in range(w // blk):
        acc = jnp.dot(u[:, 0:blk], t_s[tb], preferred_element_type=F32)
        for sb in range(1, tb + 1):
            acc = acc + jnp.dot(u[:, sb * blk:(sb + 1) * blk], t_s[tb - sb],
                                preferred_element_type=F32)
        y_ref[0, :, :, tb * blk:(tb + 1) * blk] = acc.reshape(nb, cc, blk)
    ar = a16_ref[0, 0:1, :].reshape(1, 1, half)
    ai = a16_ref[0, 1:2, :].reshape(1, 1, half)
    sr = s_carry[:, :, 0:half]
    si = s_carry[:, :, half:nst]
    for c in range(cc):
        sp_ref[:, c:c + 1, 0:half] = sr
        sp_ref[:, c:c + 1, half:nst] = si
        dr = ds_ref[:, c:c + 1, 0:half]
        di = ds_ref[:, c:c + 1, half:nst]
        sr, si = ar * sr - ai * si + dr, ar * si + ai * sr + di
    s_carry[:, :, 0:half] = sr
    s_carry[:, :, half:nst] = si
    sl_ref[0, :, :, 0:half] = sr
    sl_ref[0, :, :, half:nst] = si

    sp = sp_ref[...].reshape(rows, nst).astype(BF16)
    for tb in range(w // blk):
        acc = jnp.dot(sp, r_s[:, tb * blk:(tb + 1) * blk], preferred_element_type=F32)
        y_ref[0, :, :, tb * blk:(tb + 1) * blk] += acc.reshape(nb, cc, blk)


N_MIX_IN = 12


def _mix_rows(x, yssm, ut, ycn, dsk_ref, wglu_ref, bglu_ref, nssm_ref, wout_ref,
              nffn_ref, wr_ref, br_ref):
    ne = br_ref.shape[0]
    y = _gelu_tanh(yssm + dsk_ref[...] * ut.astype(F32))
    glu = jnp.dot(y.astype(BF16), wglu_ref[...], preferred_element_type=F32) + bglu_ref[...]
    o = y * jax.nn.sigmoid(glu)
    ysn = _rms(o, nssm_ref[...]).astype(BF16)
    mix = jnp.concatenate([ysn, ycn], axis=-1)
    x1 = x + jnp.dot(mix, wout_ref[...], preferred_element_type=F32)
    hf = _rms(x1, nffn_ref[...])
    hf_hi = hf.astype(BF16)
    hf_lo = (hf - hf_hi.astype(F32)).astype(BF16)
    r = hf.shape[0]
    part = jnp.dot(jnp.concatenate([hf_hi, hf_lo], axis=0), wr_ref[...], preferred_element_type=F32)
    logits = (part[0:r, 0:LANE] + part[0:r, LANE:2 * LANE]) + (part[r:2 * r, 0:LANE] + part[r:2 * r, LANE:2 * LANE])
    lt = logits.T[0:ne, :] + br_ref[...]
    iota = lax.broadcasted_iota(I32, lt.shape, 0)
    vals, idxs = [], []
    sel = jnp.zeros(lt.shape, F32)
    for _ in range(TOP_K):
        m = jnp.max(lt, axis=0, keepdims=True)
        ik = jnp.min(jnp.where(lt == m, iota, ne), axis=0, keepdims=True)
        vals.append(m)
        idxs.append(ik)
        hit = iota == ik
        sel = sel + jnp.where(hit, 1.0, 0.0)
        lt = jnp.where(hit, -jnp.inf, lt)
    es = [jnp.exp(v - vals[0]) for v in vals]
    tot = es[0] + es[1] + es[2] + es[3]
    idx = jnp.concatenate(idxs, axis=0)
    gates = jnp.concatenate([e / tot for e in es], axis=0)
    return x1, hf_hi, idx, gates, sel


def _mix_kernel_prompt(*refs):
    x_ref, yc_ref = refs[0], refs[1]
    x1_ref, hf_ref, idx_ref, gate_ref, cnt_ref, ybuf = refs[N_MIX_IN:]
    nj, nb, ncz, _ = yc_ref.shape
    for s in range(CHUNK):
        for j in range(nj):
            ybuf[j, :, pl.ds(s, ncz, stride=CHUNK), :] = yc_ref[j, :, :, s * LANE:(s + 1) * LANE]
    ut_ref, ycn_ref = refs[2], refs[3]
    tt, d = x_ref.shape[1], x_ref.shape[2]
    cw = nj * LANE
    nbc = TOK_TILE // tt
    for t in range(cnt_ref.shape[0]):
        b0, r0 = t * nbc, t * TOK_TILE
        yssm = jnp.concatenate([ybuf[j, b0:b0 + nbc].reshape(TOK_TILE, LANE) for j in range(nj)], axis=-1)
        x1, hf, idx, gates, sel = _mix_rows(
            x_ref[b0:b0 + nbc].reshape(TOK_TILE, d), yssm, ut_ref[b0:b0 + nbc].reshape(TOK_TILE, cw),
            ycn_ref[b0:b0 + nbc].reshape(TOK_TILE, cw), *refs[4:N_MIX_IN])
        x1_ref[r0:r0 + TOK_TILE, :] = x1
        hf_ref[r0:r0 + TOK_TILE, :] = hf
        idx_ref[:, r0:r0 + TOK_TILE] = idx
        gate_ref[:, r0:r0 + TOK_TILE] = gates
        cnt_ref[t] = jnp.sum(sel, axis=1, keepdims=True).astype(I32)


def _mix_kernel_sample(*refs):
    x1, hf, idx, gates, sel = _mix_rows(refs[0][0], refs[1][...], refs[2][0], refs[3][0],
                                        *refs[4:N_MIX_IN])
    x1_ref, hf_ref, idx_ref, gate_ref, cnt_ref = refs[N_MIX_IN + 5:]
    ns = x1.shape[0]
    x1_ref[...] = jnp.zeros(x1_ref.shape, x1_ref.dtype)
    hf_ref[...] = jnp.zeros(hf_ref.shape, hf_ref.dtype)
    idx_ref[...] = jnp.zeros(idx_ref.shape, idx_ref.dtype)
    gate_ref[...] = jnp.zeros(gate_ref.shape, gate_ref.dtype)
    x1_ref[0:ns, :] = x1
    hf_ref[0:ns, :] = hf
    idx_ref[:, 0:ns] = idx
    gate_ref[:, 0:ns] = gates
    cnt_ref[0] = jnp.sum(sel, axis=1, keepdims=True).astype(I32)


class _Table:
    def __init__(self, ref, offset):
        self.ref, self.offset = ref, offset

    def __getitem__(self, k):
        return self.ref[self.offset + k]


def _pack_tables(tables):
    offsets, total = [], 0
    for t in tables:
        offsets.append(total)
        total += t.shape[0]
    return jnp.concatenate(tables), tuple(offsets)


def _split_bf16(x, parts):
    out = []
    for _ in range(parts - 1):
        p = x.astype(BF16)
        out.append(p)
        x = x - p.astype(F32)
    out.append(x.astype(BF16))
    return out


def _run_tables(idx_ref, loff_ref, tile, n_valid, gate_ref=None):
    ne = loff_ref.shape[1]
    tt = idx_ref.shape[1]
    e_iota = lax.broadcasted_iota(I32, (ne, tt), 0)
    tok = tile * tt + lax.broadcasted_iota(I32, (1, tt), 1)
    valid = tok < n_valid
    hits = [jnp.logical_and(e_iota == idx_ref[k:k + 1, :], valid) for k in range(TOP_K)]
    sel = jnp.zeros((ne, tt), F32)
    for h in hits:
        sel = sel + jnp.where(h, 1.0, 0.0)
    before = lax.broadcasted_iota(I32, (tt, tt), 0) < lax.broadcasted_iota(I32, (tt, tt), 1)
    tri = jnp.where(before, 1.0, 0.0).astype(BF16)
    base = jnp.dot(sel.astype(BF16), tri, preferred_element_type=F32) + loff_ref[0].astype(F32)
    base = jnp.where(sel > 0.0, base + 1.0, 0.0)
    b_hi = BF16_EXACT * jnp.floor(base * (1.0 / BF16_EXACT))
    b_lo = base - b_hi
    halves = jnp.concatenate([b_hi, b_lo], axis=0).astype(BF16)
    if gate_ref is None:
        return halves, None
    gate_e = jnp.zeros((ne, tt), F32)
    for k, h in enumerate(hits):
        gate_e = gate_e + jnp.where(h, gate_ref[k:k + 1, :], 0.0)
    return halves, jnp.concatenate(_split_bf16(gate_e, GATE_PIECES), axis=0)


def _rows_onehot(halves, rgn_ref, row0, nrows, pieces=None):
    ne = rgn_ref.shape[2]
    tt = halves.shape[1]

    def own(copies):
        lo = jnp.concatenate([rgn_ref[0, 0:1, :]] * copies, axis=1)
        hi = jnp.concatenate([rgn_ref[0, 1:2, :]] * copies, axis=1)
        r_i = row0 + lax.broadcasted_iota(I32, (nrows, copies * ne), 0)
        return jnp.logical_and(r_i >= lo, r_i < hi)

    want = jnp.dot(jnp.where(own(2), 1.0, 0.0).astype(BF16), halves, preferred_element_type=F32)
    r_f = (row0 + 1 + lax.broadcasted_iota(I32, (nrows, tt), 0)).astype(F32)
    smat = jnp.where(want == r_f, 1.0, 0.0).astype(BF16)
    if pieces is None:
        return smat
    per = lax.dot_general(smat, pieces, (((1,), (1,)), ((), ())), preferred_element_type=F32)
    gcol = jnp.sum(jnp.where(own(GATE_PIECES), per, 0.0), axis=1, keepdims=True)
    return smat, gcol


def _window_copy(buf, slot, hbm, lo, g, sem, to_hbm, rows, align):
    src = buf.at[slot, pl.ds(pl.multiple_of(lo, align), rows)]
    dst = hbm.at[pl.ds(pl.multiple_of(g, align), rows)]
    if to_hbm:
        return pltpu.make_async_copy(src, dst, sem.at[slot])
    return pltpu.make_async_copy(dst, src, sem.at[slot])


def _start_windows(wg_ref, wl_ref, tw_ref, tile, slots, buf, slot, hbm, sem, to_hbm, rows, align):
    def per_window(w, c):
        k = tile * slots + w
        lo = w * rows if wl_ref is None else wl_ref[k]
        _window_copy(buf, slot, hbm, lo, wg_ref[k], sem, to_hbm, rows, align).start()
        return c

    lax.fori_loop(0, tw_ref[tile], per_window, 0)


def _wait_windows(count, buf, slot, hbm, sem, to_hbm, rows, align):
    batch = 8

    def many(w, c):
        _window_copy(buf, slot, hbm, 0, 0, sem, to_hbm, batch * rows, align).wait()
        return c

    def one(w, c):
        _window_copy(buf, slot, hbm, 0, 0, sem, to_hbm, rows, align).wait()
        return c

    lax.fori_loop(0, count // batch, many, 0)
    lax.fori_loop(0, count % batch, one, 0)


def _dispatch_kernel(offsets, tab_ref,
                     hf_ref, idx_ref, gate_ref, loff_ref, rgn_ref, xs_ref, buf, carry, sem):
    wg_ref, wl_ref, mlo_ref, mtg_ref, keep_ref, tw_ref, used_ref, nv_ref = [
        _Table(tab_ref, o) for o in offsets]
    i = pl.program_id(0)
    nt = pl.num_programs(0)
    ne = loff_ref.shape[1]
    tt, d = hf_ref.shape
    nrun = buf.shape[1]
    slot = i % 2

    @pl.when(i == 0)
    def _():
        carry[...] = jnp.zeros(carry.shape, carry.dtype)
        buf[...] = jnp.zeros(buf.shape, buf.dtype)

    halves, pieces = _run_tables(idx_ref, loff_ref, i, nv_ref[0], gate_ref)

    def emit(row0, nrows):
        smat, gcol = _rows_onehot(halves, rgn_ref, row0, nrows, pieces)
        xr = jnp.dot(smat, hf_ref[...], preferred_element_type=F32)
        lane0 = lax.broadcasted_iota(I32, (nrows, GATE_COLS), 1) == 0
        buf[slot, row0:row0 + nrows, 0:d // 2] = _pack_pairs(xr[:, :d // 2], xr[:, d // 2:])
        buf[slot, row0:row0 + nrows, d // 2:] = pltpu.bitcast(jnp.where(lane0, gcol, 0.0), U32)

    emit(0, DISPATCH_BASE_ROWS)
    past_base = used_ref[i] > DISPATCH_BASE_ROWS

    @pl.when(past_base)
    def _():
        emit(DISPATCH_BASE_ROWS, nrun - DISPATCH_BASE_ROWS)

    @pl.when(jnp.logical_not(past_base))
    def _():
        buf[slot, nrun - 2 * SUBLANE:nrun, :] = jnp.zeros((2 * SUBLANE, buf.shape[2]), buf.dtype)

    def merge(e, c):
        k = i * ne + e
        lo = pl.multiple_of(mlo_ref[k], SUBLANE)
        buf[slot, pl.ds(lo, SUBLANE), :] = buf[slot, pl.ds(lo, SUBLANE), :] | carry[e]
        tg = pl.multiple_of(mtg_ref[k], SUBLANE)
        carry[e] = jnp.where(keep_ref[k] > 0, carry[e], buf[slot, pl.ds(tg, SUBLANE), :])
        return c

    lax.fori_loop(0, ne, merge, 0, unroll=4)

    @pl.when(i > 0)
    def _():
        _wait_windows(tw_ref[i - 1], buf, 1 - slot, xs_ref, sem, True, WIN, SUBLANE)

    _start_windows(wg_ref, wl_ref, tw_ref, i, MAX_WINDOWS, buf, slot, xs_ref, sem, True, WIN, SUBLANE)

    @pl.when(i == nt - 1)
    def _():
        _wait_windows(tw_ref[i], buf, slot, xs_ref, sem, True, WIN, SUBLANE)


def _expert_weight_copies(wg_hbm, wd_hbm, wg_f32, wd_f32, sem, e, slot):
    return (pltpu.make_async_copy(wg_hbm.at[e], wg_f32.at[slot], sem.at[0, slot]),
            pltpu.make_async_copy(wd_hbm.at[e], wd_f32.at[slot], sem.at[1, slot]))


def _moe_kernel(offsets, tab_ref,
                x_hbm, bg_ref, bd_ref, wg_hbm, wd_hbm, y_ref,
                wg_f32, wd_f32, wg_bf, wd_bf, sem, xbuf, xsem):
    be_ref, bv_ref, first_ref, nxt_ref, slot_ref = [_Table(tab_ref, o) for o in offsets]
    i = pl.program_id(0)
    nblk = pl.num_programs(0)
    e = be_ref[i]
    dff = wd_bf.shape[0]
    nslots, bm = xbuf.shape[0], xbuf.shape[1]
    sub = MOE_SUB
    nw = xbuf.shape[2] - GATE_COLS

    def x_copy(j):
        return pltpu.make_async_copy(x_hbm.at[pl.ds(pl.multiple_of(j * bm, bm), bm)],
                                     xbuf.at[j % nslots], xsem.at[j % nslots])

    def x_start(j):
        @pl.when(jnp.logical_and(j < nblk, bv_ref[jnp.minimum(j, nblk - 1)] > 0))
        def _():
            x_copy(j).start()

    @pl.when(i == 0)
    def _():
        for j in range(nslots - 1):
            x_start(i + j)

    x_start(i + nslots - 1)

    @pl.when(bv_ref[i] > 0)
    def _():
        x_copy(i).wait()

    x_ref = xbuf.at[i % nslots]

    @pl.when(first_ref[i] > 0)
    def _():
        slot = slot_ref[i]

        @pl.when(first_ref[i] > 1)
        def _():
            for cp in _expert_weight_copies(wg_hbm, wd_hbm, wg_f32, wd_f32, sem, e, slot):
                cp.start()

        for cp in _expert_weight_copies(wg_hbm, wd_hbm, wg_f32, wd_f32, sem, e, slot):
            cp.wait()

        @pl.when(nxt_ref[i] >= 0)
        def _():
            for cp in _expert_weight_copies(wg_hbm, wd_hbm, wg_f32, wd_f32, sem, nxt_ref[i], 1 - slot):
                cp.start()

        wg_bf[...] = wg_f32[slot].astype(BF16)
        wd_bf[...] = wd_f32[slot].astype(BF16)

    def rows(r0, nrows):
        sizes = [MOE_CHAIN] * (nrows // MOE_CHAIN) + ([nrows % MOE_CHAIN] if nrows % MOE_CHAIN else [])
        lo = r0
        for size in sizes:
            _chain(lo, size)
            lo += size

    def _chain(lo, n):
        live = lax.broadcasted_iota(I32, (n, 1), 0) + lo < bv_ref[i]
        x = jnp.where(live, _unpack_pairs(x_ref[lo:lo + n, 0:nw]), jnp.zeros((), BF16))
        route = jnp.where(live, pltpu.bitcast(x_ref[lo:lo + n, nw:], F32)[:, 0:1], 0.0)
        gu = jnp.dot(x, wg_bf[...], preferred_element_type=F32) + bg_ref[0]
        gate = jnp.minimum(gu[:, :dff], SWIGLU_LIMIT)
        up = jnp.clip(gu[:, dff:], -SWIGLU_LIMIT, SWIGLU_LIMIT)
        h = gate * jax.nn.sigmoid(SWIGLU_ALPHA * gate) * (up + 1.0)
        y = jnp.dot(h.astype(BF16), wd_bf[...], preferred_element_type=F32) + bd_ref[0]
        yr = (route * y).astype(BF16).astype(F32)
        half = yr.shape[1] // 2
        y_ref[lo:lo + n, :] = _pack_pairs(yr[:, :half], yr[:, half:])

    nchains = bm // sub
    for live_chains in range(nchains + 1):
        lo_rows, hi_rows = (live_chains - 1) * sub, live_chains * sub

        @pl.when(jnp.logical_and(bv_ref[i] > lo_rows, bv_ref[i] <= hi_rows) if live_chains
                 else bv_ref[i] <= 0)
        def _(used=hi_rows):
            if used:
                rows(0, used)
            if used < bm:
                y_ref[used:bm, :] = jnp.zeros((bm - used, y_ref.shape[1]), y_ref.dtype)


def _combine_tile(step, nt):
    return (step + nt - 1) % nt


def _combine_kernel(offsets, tab_ref,
                    x1_ref, idx_ref, loff_ref, rgn_ref, nf_ref, yb_ref,
                    yp_ref, ys_ref, buf, acc, sem):
    wg_ref, tw_ref, nv_ref = [_Table(tab_ref, o) for o in offsets]
    s = pl.program_id(0)
    nt = pl.num_programs(0)
    tt, d = x1_ref.shape
    nrun = buf.shape[1]
    tile = _combine_tile(s, nt)
    slot = s % 2

    @pl.when(s == 0)
    def _():
        buf[...] = jnp.zeros(buf.shape, buf.dtype)
        _start_windows(wg_ref, None, tw_ref, tile, MAX_WINDOWS, buf, slot, yb_ref, sem, False,
                       WIN, SUBLANE)

    @pl.when(s + 1 < nt)
    def _():
        _start_windows(wg_ref, None, tw_ref, _combine_tile(s + 1, nt), MAX_WINDOWS, buf, 1 - slot,
                       yb_ref, sem, False, WIN, SUBLANE)

    halves, _ = _run_tables(idx_ref, loff_ref, tile, nv_ref[0])

    def gathered(row0, nrows):
        smat = _rows_onehot(halves, rgn_ref, row0, nrows)
        yrun = _unpack_pairs(buf[slot, row0:row0 + nrows])
        return lax.dot_general(smat, yrun, (((0,), (0,)), ((), ())), preferred_element_type=F32)

    _wait_windows(tw_ref[tile], buf, slot, yb_ref, sem, False, WIN, SUBLANE)
    acc[...] = gathered(0, COMBINE_BASE_ROWS)
    for row0 in range(COMBINE_BASE_ROWS, nrun, COMBINE_STEP_ROWS):
        @pl.when(WIN * tw_ref[tile] > row0)
        def _(row0=row0):
            acc[...] += gathered(row0, min(COMBINE_STEP_ROWS, nrun - row0))

    out = _rms(x1_ref[...] + acc[...], nf_ref[...])

    @pl.when(s == 0)
    def _():
        ys_ref[...] = out[0:ys_ref.shape[0], :]

    @pl.when(s > 0)
    def _():
        yp_ref[...] = out.reshape(yp_ref.shape)


def _ssm_matrices(a_re, a_im, log_dt, b_re, b_im, c_re, c_im):
    g, n = a_re.shape
    hch = b_re.shape[2]
    gpt = LANE // hch
    nj = g // gpt
    a = lax.complex(a_re, a_im)
    dta = a * jnp.exp(log_dt)[:, None]
    a_bar = jnp.exp(dta)
    bb = ((a_bar - 1.0) / a)[:, :, None] * lax.complex(b_re, b_im)
    cc = lax.complex(c_re, c_im)
    ks = jnp.arange(CHUNK + 1, dtype=F32)
    pw = jnp.exp(dta[None] * ks[:, None, None])
    kk = jnp.real(jnp.einsum('gon,kgn,gni->kgio', cc, pw[:CHUNK], bb))
    kk = jnp.concatenate([jnp.zeros_like(kk[:1]), kk], axis=0)
    kc = kk.reshape(CHUNK + 1, nj, gpt * hch, hch).transpose(1, 0, 2, 3)
    kc = kc.reshape(nj, (CHUNK + 1) * LANE, hch).astype(BF16)
    pwr, pwi = jnp.real(pw), jnp.imag(pw)
    bbr = jnp.real(bb).transpose(0, 2, 1).reshape(1, g * hch, n)
    bbi = jnp.imag(bb).transpose(0, 2, 1).reshape(1, g * hch, n)
    par = jnp.repeat(pwr[CHUNK - 1::-1][:CHUNK], hch, axis=1)
    pai = jnp.repeat(pwi[CHUNK - 1::-1][:CHUNK], hch, axis=1)
    pc = jnp.concatenate([par * bbr - pai * bbi, par * bbi + pai * bbr], axis=-1)
    pc = pc.reshape(CHUNK, nj, gpt * hch, 2 * n).transpose(1, 0, 2, 3)
    pc = pc.reshape(nj, CHUNK * LANE, 2 * n).astype(BF16)
    ccr = jnp.real(cc).transpose(2, 0, 1).reshape(n, 1, g * hch)
    cci = jnp.imag(cc).transpose(2, 0, 1).reshape(n, 1, g * hch)
    qar = jnp.repeat(pwr[1:CHUNK + 1].transpose(2, 0, 1), hch, axis=2)
    qai = jnp.repeat(pwi[1:CHUNK + 1].transpose(2, 0, 1), hch, axis=2)
    rc = jnp.stack([ccr * qar - cci * qai, -(ccr * qai + cci * qar)], axis=0)
    rc = rc.reshape(2 * n, CHUNK, nj, gpt * hch).transpose(2, 0, 1, 3)
    rc = rc.reshape(nj, 2 * n, CHUNK * LANE).astype(BF16)
    a16 = pw[CHUNK].reshape(nj, 1, gpt * n)
    a16 = jnp.concatenate([jnp.real(a16), jnp.imag(a16)], axis=1)
    bc = jnp.stack([jnp.real(bb), jnp.imag(bb)], axis=0).transpose(1, 3, 0, 2)
    bc = bc.reshape(g * hch, 2 * n).astype(BF16)
    c2 = jnp.stack([jnp.real(cc), -jnp.imag(cc)], axis=0).transpose(0, 3, 1, 2)
    c2 = c2.reshape(2 * n, g * hch).astype(BF16)
    abr = jnp.real(a_bar).reshape(1, g * n)
    abi = jnp.imag(a_bar).reshape(1, g * n)
    return kc, pc, rc, a16, bc, c2, abr, abi


def _full(shape):
    return pl.BlockSpec(shape, lambda *_: (0,) * len(shape))


def kernel(x_prompt, x_sample, state_ssm_re, state_ssm_im, state_conv, meta_tokens, norm_mix, w_in,
           ssm_a_re, ssm_a_im, ssm_log_dt, ssm_b_re, ssm_b_im, ssm_c_re, ssm_c_im, ssm_d, w_glu, b_glu,
           conv_w, norm_out_ssm, norm_out_conv, w_out, norm_ffn, w_router, b_router, w_gate_up,
           b_gate_up, w_down, b_down, norm_final):
    nb, seq, d = x_prompt.shape
    ns = x_sample.shape[0]
    depth, _, g, n = state_ssm_re.shape
    assert depth == 1 and x_sample.shape[1] == 1 and meta_tokens.shape[0] == CHUNK
    cw = conv_w.shape[2]
    nj = cw // LANE
    ne = w_router.shape[2]
    dff = w_down.shape[2]
    nst = g * n
    nbt = nb // 2
    tt = 256
    n_chunks = seq // CHUNK
    tp = nb * seq
    tall = tp + ns
    tm = 128
    nbm = nb
    rows_p = nbm * tm
    assert rows_p % TOK_TILE == 0 and TOK_TILE % tm == 0 and ns <= TOK_TILE and d % 2 == 0
    n_tiles = tp // TOK_TILE + 1
    ta = n_tiles * TOK_TILE

    kc, pc, rc, a16, bc, c2, abr, abi = _ssm_matrices(
        ssm_a_re[0], ssm_a_im[0], ssm_log_dt[0], ssm_b_re[0], ssm_b_im[0], ssm_c_re[0], ssm_c_im[0])
    win_bf = w_in[0].astype(BF16)
    nmix = norm_mix[0].reshape(1, d)
    nconv = norm_out_conv[0].reshape(1, cw)
    cwt = conv_w[0]

    xsm = jnp.concatenate([x_sample.reshape(ns, d), meta_tokens], axis=0)
    nsm = ns + CHUNK
    s0r = state_ssm_re[0].reshape(ns, nst)
    s0i = state_ssm_im[0].reshape(ns, nst)
    buf0 = state_conv[0, :, 0, :]
    buf1 = state_conv[0, :, 1, :]
    u_sm, z_sm, y_s, ycn_s, sr_s, si_s = pl.pallas_call(
        _small_front_kernel,
        out_shape=(jax.ShapeDtypeStruct((nsm, cw), F32), jax.ShapeDtypeStruct((nsm, cw), F32),
                   jax.ShapeDtypeStruct((ns, cw), F32), jax.ShapeDtypeStruct((ns, cw), BF16),
                   jax.ShapeDtypeStruct((ns, nst), F32), jax.ShapeDtypeStruct((ns, nst), F32)),
        scratch_shapes=[pltpu.VMEM((cw, 2 * nst), BF16), pltpu.VMEM((2 * nst, cw), BF16)],
        **_params("small_front", None),
    )(xsm, nmix, win_bf, s0r, s0i, buf0, buf1, cwt, bc, c2, abr, abi, nconv)
    u_meta = u_sm[ns:]
    z_meta8 = z_sm[ns + CHUNK - 8:]
    new_conv_s = jnp.stack([buf1, z_sm[:ns]], axis=1)[None]
    new_re_s = sr_s.reshape(1, ns, g, n)
    new_im_s = si_s.reshape(1, ns, g, n)

    wch = CHUNK * LANE
    u4c, u_tok, ycn_p, ztail = pl.pallas_call(
        _front_kernel,
        grid=(nb // nbt, seq // tt),
        in_specs=[pl.BlockSpec((nbt, tt, d), lambda b, i: (b, i, 0)),
                  _full((1, d)), _full((d, 4 * cw)), _full((8, cw)), _full((3, cw)), _full((1, cw))],
        out_specs=(pl.BlockSpec((nj, nbt, tt // CHUNK, wch), lambda b, i: (0, b, i, 0)),
                   pl.BlockSpec((nbt, tt, cw), lambda b, i: (b, i, 0)),
                   pl.BlockSpec((nbt, tt, cw), lambda b, i: (b, i, 0)),
                   pl.BlockSpec((nbt, 8, cw), lambda b, i: (b, 0, 0))),
        out_shape=(jax.ShapeDtypeStruct((nj, nb, n_chunks, wch), BF16),
                   jax.ShapeDtypeStruct((nb, seq, cw), BF16),
                   jax.ShapeDtypeStruct((nb, seq, cw), BF16),
                   jax.ShapeDtypeStruct((nb, 8, cw), F32)),
        scratch_shapes=[pltpu.VMEM((nbt, tt + 8, cw), F32), pltpu.VMEM((nj, nbt, tt, LANE), F32)],
        **_params("front", ("arbitrary", "arbitrary")),
    )(x_prompt, nmix, win_bf, z_meta8, cwt, nconv)
    new_conv_p = ztail[:, 6:8, :][None]

    cc = n_chunks // 2
    um = u_meta.reshape(CHUNK, nj, LANE).transpose(1, 0, 2).reshape(nj, 1, wch)
    um = jnp.broadcast_to(um, (nj, 8 * nb, wch)).astype(BF16)
    gpt = g // nj
    nstj = 2 * gpt * n
    hch = cw // g
    y4c, s_last = pl.pallas_call(
        _ssm_kernel,
        grid=(nj, n_chunks // cc),
        in_specs=[pl.BlockSpec((1, nb, cc, wch), lambda j, t: (j, 0, t, 0)),
                  pl.BlockSpec((1, 8 * nb, wch), lambda j, t: (j, 0, 0)),
                  pl.BlockSpec((1, (CHUNK + 1) * LANE, hch), lambda j, t: (j, 0, 0)),
                  pl.BlockSpec((1, wch, 2 * n), lambda j, t: (j, 0, 0)),
                  pl.BlockSpec((1, 2 * n, wch), lambda j, t: (j, 0, 0)),
                  pl.BlockSpec((1, 2, nstj // 2), lambda j, t: (j, 0, 0))],
        out_specs=(pl.BlockSpec((1, nb, cc, wch), lambda j, t: (j, 0, t, 0)),
                   pl.BlockSpec((1, nb, 1, nstj), lambda j, t: (j, 0, 0, 0))),
        out_shape=(jax.ShapeDtypeStruct((nj, nb, n_chunks, wch), F32),
                   jax.ShapeDtypeStruct((nj, nb, 1, nstj), F32)),
        scratch_shapes=[pltpu.VMEM((nb, 1, nstj), F32), pltpu.VMEM((nb, cc, nstj), F32),
                        pltpu.VMEM((nb, cc, nstj), F32),
                        pltpu.VMEM((wch, nstj), BF16), pltpu.VMEM((nstj, wch), BF16),
                        pltpu.VMEM((CHUNK // 2, 2 * LANE, 2 * LANE), BF16)],
        **_params("ssm", ("parallel", "arbitrary")),
    )(u4c, um, kc, pc, rc, a16)
    sl = s_last.reshape(nj, nb, 2, gpt, n)
    new_re_p = sl[:, :, 0].transpose(1, 0, 2, 3).reshape(1, nb, g, n)
    new_im_p = sl[:, :, 1].transpose(1, 0, 2, 3).reshape(1, nb, g, n)

    dsk = ssm_d[0].reshape(1, cw)
    wglu_bf = w_glu[0].astype(BF16)
    bglu = b_glu[0].reshape(1, cw)
    nssm = norm_out_ssm[0].reshape(1, cw)
    wout_bf = w_out[0].astype(BF16)
    nffn = norm_ffn[0].reshape(1, d)
    wr_pad = jnp.zeros((d, LANE), F32).at[:, :ne].set(w_router[0])
    wr_hi = wr_pad.astype(BF16)
    wr_lo = (wr_pad - wr_hi.astype(F32)).astype(BF16)
    br = b_router[0].reshape(ne, 1)
    wr2 = jnp.concatenate([wr_hi, wr_lo], axis=1)
    mix_w = (dsk, wglu_bf, bglu, nssm, wout_bf, nffn, wr2, br)
    mix_w_specs = [_full((1, cw)), _full((cw, cw)), _full((1, cw)), _full((1, cw)), _full((2 * cw, d)),
                   _full((1, d)), _full((d, 2 * LANE)), _full((ne, 1))]
    assert 4 + len(mix_w) == N_MIX_IN
    mix_out_shape = (jax.ShapeDtypeStruct((ta, d), F32), jax.ShapeDtypeStruct((ta, d), BF16),
                     jax.ShapeDtypeStruct((TOP_K, ta), I32), jax.ShapeDtypeStruct((TOP_K, ta), F32),
                     jax.ShapeDtypeStruct((n_tiles, ne, 1), I32))
    tpm = rows_p // TOK_TILE
    nbg = nb // nbm
    x1_all, hf_all, idx_all, gate_all, cnt = pl.pallas_call(
        _mix_kernel_prompt,
        grid=(seq // tm, nbg),
        in_specs=[pl.BlockSpec((nbm, tm, d), lambda i, b: (b, i, 0)),
                  pl.BlockSpec((nj, nbm, tm // CHUNK, wch), lambda i, b: (0, b, i, 0)),
                  pl.BlockSpec((nbm, tm, cw), lambda i, b: (b, i, 0)),
                  pl.BlockSpec((nbm, tm, cw), lambda i, b: (b, i, 0))] + mix_w_specs,
        out_specs=(pl.BlockSpec((rows_p, d), lambda i, b: (i * nbg + b, 0)),
                   pl.BlockSpec((rows_p, d), lambda i, b: (i * nbg + b, 0)),
                   pl.BlockSpec((TOP_K, rows_p), lambda i, b: (0, i * nbg + b)),
                   pl.BlockSpec((TOP_K, rows_p), lambda i, b: (0, i * nbg + b)),
                   pl.BlockSpec((tpm, ne, 1), lambda i, b: (i * nbg + b, 0, 0))),
        out_shape=mix_out_shape,
        scratch_shapes=[pltpu.VMEM((nj, nbm, tm, LANE), F32)],
        **_params("mix_prompt", ("parallel", "parallel")),
    )(x_prompt, y4c, u_tok, ycn_p, *mix_w)

    last = n_tiles - 1
    any_spec = pl.BlockSpec(memory_space=pl.ANY)
    x1_all, hf_all, idx_all, gate_all, cnt = pl.pallas_call(
        _mix_kernel_sample,
        grid=(1,),
        in_specs=[_full((1, ns, d)), _full((ns, cw)), _full((1, ns, cw)),
                  _full((1, ns, cw))] + mix_w_specs + [any_spec] * 5,
        out_specs=(pl.BlockSpec((TOK_TILE, d), lambda i: (last, 0)),
                   pl.BlockSpec((TOK_TILE, d), lambda i: (last, 0)),
                   pl.BlockSpec((TOP_K, TOK_TILE), lambda i: (0, last)),
                   pl.BlockSpec((TOP_K, TOK_TILE), lambda i: (0, last)),
                   pl.BlockSpec((1, ne, 1), lambda i: (last, 0, 0))),
        out_shape=mix_out_shape,
        input_output_aliases={N_MIX_IN + k: k for k in range(5)},
        **_params("mix_sample", ("arbitrary",)),
    )(x_sample.reshape(1, ns, d), y_s, u_sm[:ns].astype(BF16).reshape(1, ns, cw),
      ycn_s.reshape(1, ns, cw), *mix_w, x1_all, hf_all, idx_all, gate_all, cnt)

    bm = MOE_ROWS
    cnt2 = cnt.reshape(n_tiles, ne)
    before = jnp.cumsum(cnt2, axis=0) - cnt2
    count = jnp.sum(cnt2, axis=0)
    padded = ((count + WIN + bm - 1) // bm) * bm
    pend = jnp.cumsum(padded)
    pstart = pend - padded
    phase = before % SUBLANE
    span = jnp.where(cnt2 > 0, phase + cnt2, 0)
    gstart = (pstart[None, :] + before - phase).astype(I32).reshape(-1)
    nwin = ((span + WIN - 1) // WIN).astype(I32)
    reg8 = ((span + SUBLANE - 1) // SUBLANE) * SUBLANE
    loff = (jnp.cumsum(reg8, axis=1) - reg8).astype(I32)
    tail = jnp.where(span % SUBLANE != 0, loff + (span // SUBLANE) * SUBLANE, -1).astype(I32)
    twin = jnp.sum(nwin, axis=1).astype(I32)
    n_blocks = (ta * TOP_K + ne * (WIN + bm - 1) + bm - 1) // bm
    cap = n_blocks * bm
    blk0 = jnp.arange(n_blocks, dtype=I32) * bm
    blk_e = jnp.minimum(jnp.sum((pend[None, :] <= blk0[:, None]).astype(I32), axis=1), ne - 1)
    e_ar = jnp.arange(ne, dtype=I32)
    blk_hot = blk_e[:, None] == e_ar[None, :]

    def _of_block(per_expert):
        return jnp.sum(jnp.where(blk_hot, per_expert[None, :], 0), axis=1)

    blk_valid = jnp.clip(_of_block(count) - (blk0 - _of_block(pstart)), 0, bm).astype(I32)
    has = count > 0
    later = jnp.logical_and(e_ar[None, :] > e_ar[:, None], has[None, :])
    nxt_e = jnp.min(jnp.where(later, e_ar[None, :], ne), axis=1)
    nxt_e = jnp.where(nxt_e < ne, nxt_e, -1)
    ordinal = jnp.cumsum(has.astype(I32)) - 1
    is_first = jnp.logical_and(blk_valid > 0, blk0 == _of_block(pstart))
    blk_first = jnp.where(is_first, jnp.where(_of_block(ordinal) == 0, 2, 1), 0).astype(I32)
    blk_next = _of_block(nxt_e).astype(I32)
    blk_slot = (_of_block(ordinal) % 2).astype(I32)
    nvalid = jnp.full((1,), tall, I32)
    loff_al = (WIN * (jnp.cumsum(nwin, axis=1) - nwin)).astype(I32)

    def _window_list(nw, slots, rows, first_hbm, first_buf):
        wcum = jnp.cumsum(nw, axis=1)
        wslot = jnp.arange(slots, dtype=I32)
        w_hot = jnp.logical_and(wslot[None, :, None] >= (wcum - nw)[:, None, :],
                                wslot[None, :, None] < wcum[:, None, :])

        def _of_window(per_tile_expert):
            return jnp.sum(jnp.where(w_hot, per_tile_expert[:, None, :], 0), axis=2)

        w_in_run = wslot[None, :] - _of_window(wcum - nw)
        return [(_of_window(f) + rows * w_in_run).astype(I32).reshape(-1) for f in (first_hbm, first_buf)]

    w_hbm, w_buf = _window_list(nwin, MAX_WINDOWS, WIN, gstart.reshape(n_tiles, ne), loff)
    nrun_d = _dispatch_run_rows(ne)
    zero_grp, spare_grp = nrun_d - 2 * SUBLANE, nrun_d - SUBLANE
    m_lo = jnp.where(nwin > 0, loff, spare_grp).astype(I32).reshape(-1)
    m_tg = jnp.where(jnp.logical_and(nwin > 0, tail >= 0), tail, zero_grp).astype(I32).reshape(-1)
    m_keep = (nwin == 0).astype(I32).reshape(-1)
    used_d = jnp.sum(reg8, axis=1).astype(I32)
    tab_d, off_d = _pack_tables((w_hbm, w_buf, m_lo, m_tg, m_keep, twin, used_d, nvalid))
    tab_c, off_c = _pack_tables((w_hbm, twin, nvalid))
    tab_m, off_m = _pack_tables((blk_e, blk_valid, blk_first, blk_next, blk_slot))
    rowoff_d = (loff + phase).astype(I32).reshape(n_tiles, ne, 1)
    rowoff_c = (loff_al + phase).astype(I32).reshape(n_tiles, ne, 1)
    rgn_d = jnp.stack([loff, loff + reg8], axis=1).astype(I32)
    rgn_c = jnp.stack([loff_al, loff_al + WIN * nwin], axis=1).astype(I32)
    xw = d // 2 + GATE_COLS

    xs = pl.pallas_call(
        functools.partial(_dispatch_kernel, off_d),
        grid_spec=pltpu.PrefetchScalarGridSpec(
            num_scalar_prefetch=1,
            grid=(n_tiles,),
            in_specs=[pl.BlockSpec((TOK_TILE, d), lambda i, *_: (i, 0)),
                      pl.BlockSpec((TOP_K, TOK_TILE), lambda i, *_: (0, i)),
                      pl.BlockSpec((TOP_K, TOK_TILE), lambda i, *_: (0, i)),
                      pl.BlockSpec((1, ne, 1), lambda i, *_: (i, 0, 0)),
                      pl.BlockSpec((1, 2, ne), lambda i, *_: (i, 0, 0))],
            out_specs=pl.BlockSpec(memory_space=pl.ANY),
            scratch_shapes=[pltpu.VMEM((2, _dispatch_run_rows(ne), xw), U32),
                            pltpu.VMEM((ne, SUBLANE, xw), U32), pltpu.SemaphoreType.DMA((2,))]),
        out_shape=jax.ShapeDtypeStruct((cap, xw), U32),
        **_params("dispatch", ("arbitrary",)),
    )(tab_d, hf_all, idx_all, gate_all, rowoff_d, rgn_d)

    yb = pl.pallas_call(
        functools.partial(_moe_kernel, off_m),
        grid_spec=pltpu.PrefetchScalarGridSpec(
            num_scalar_prefetch=1,
            grid=(n_blocks,),
            in_specs=[pl.BlockSpec(memory_space=pl.ANY),
                      pl.BlockSpec((1, 1, 2 * dff), lambda i, tab: (tab[off_m[0] + i], 0, 0)),
                      pl.BlockSpec((1, 1, d), lambda i, tab: (tab[off_m[0] + i], 0, 0)),
                      pl.BlockSpec(memory_space=pl.ANY), pl.BlockSpec(memory_space=pl.ANY)],
            out_specs=pl.BlockSpec((bm, d // 2), lambda i, *_: (i, 0)),
            scratch_shapes=[pltpu.VMEM((2, d, 2 * dff), F32), pltpu.VMEM((2, dff, d), F32),
                            pltpu.VMEM((d, 2 * dff), BF16), pltpu.VMEM((dff, d), BF16),
                            pltpu.SemaphoreType.DMA((2, 2)),
                            pltpu.VMEM((MOE_X_SLOTS, bm, xw), U32), pltpu.SemaphoreType.DMA((MOE_X_SLOTS,))]),
        out_shape=jax.ShapeDtypeStruct((cap, d // 2), U32),
        **_params("moe", ("arbitrary",)),
    )(tab_m, xs, b_gate_up[0].reshape(ne, 1, 2 * dff),
      b_down[0].reshape(ne, 1, d), w_gate_up[0], w_down[0])

    nfin = norm_final.reshape(1, d)
    nbh = TOK_TILE // tm

    def _tile_of(s):
        return (s + n_tiles - 1) % n_tiles

    tiles_per_time = nbg * tpm

    def _yp_index(s, *_):
        t = jnp.maximum(s - 1, 0)
        return (t % tiles_per_time, t // tiles_per_time, 0)

    y_p, y_sm = pl.pallas_call(
        functools.partial(_combine_kernel, off_c),
        grid_spec=pltpu.PrefetchScalarGridSpec(
            num_scalar_prefetch=1,
            grid=(n_tiles,),
            in_specs=[pl.BlockSpec((TOK_TILE, d), lambda s, *_: (_tile_of(s), 0)),
                      pl.BlockSpec((TOP_K, TOK_TILE), lambda s, *_: (0, _tile_of(s))),
                      pl.BlockSpec((1, ne, 1), lambda s, *_: (_tile_of(s), 0, 0)),
                      pl.BlockSpec((1, 2, ne), lambda s, *_: (_tile_of(s), 0, 0)),
                      pl.BlockSpec((1, d), lambda s, *_: (0, 0)),
                      pl.BlockSpec(memory_space=pl.ANY)],
            out_specs=(pl.BlockSpec((nbh, tm, d), _yp_index),
                       pl.BlockSpec((ns, d), lambda s, *_: (0, 0))),
            scratch_shapes=[pltpu.VMEM((2, _combine_run_rows(ne), d // 2), U32),
                            pltpu.VMEM((TOK_TILE, d), F32), pltpu.SemaphoreType.DMA((2,))]),
        out_shape=(jax.ShapeDtypeStruct((nb, seq, d), F32), jax.ShapeDtypeStruct((ns, d), F32)),
        **_params("combine", ("arbitrary",)),
    )(tab_c, x1_all, idx_all, rowoff_c, rgn_c, nfin, yb)

    return (y_p, y_sm.reshape(ns, 1, d), new_re_p, new_im_p, new_conv_p,
            new_re_s, new_im_s, new_conv_s)
```

```python
import functools
import math

import jax
import jax.numpy as jnp
from jax import lax
from jax.experimental import pallas as pl
from jax.experimental.pallas import tpu as pltpu

F32 = jnp.float32
BF16 = jnp.bfloat16
U32 = jnp.uint32
I32 = jnp.int32
EPS = 1e-5
CHUNK = 16
LANE = 128
TOP_K = 4
SWIGLU_LIMIT = 7.0
SWIGLU_ALPHA = 1.702
MOE_ROWS = 1024
MOE_SUB = 256
MOE_CHAIN = 512
MOE_X_SLOTS = 3
TOK_TILE = 256
WIN = 32
SUBLANE = 8
GATE_COLS = LANE


MAX_WINDOWS = 72


def _dispatch_run_rows(ne):
    return TOP_K * TOK_TILE + ne * 2 * (SUBLANE - 1) + WIN


def _combine_run_rows(ne):
    return -(-(TOP_K * TOK_TILE + ne * (SUBLANE - 1 + WIN - 1)) // WIN) * WIN


DISPATCH_BASE_ROWS = TOP_K * TOK_TILE + 256
COMBINE_BASE_ROWS = TOP_K * TOK_TILE + 768
COMBINE_STEP_ROWS = 256
GATE_PIECES = 3
HI_MASK = 0xFFFF0000
BF16_EXACT = 256.0
MIB = 1024 * 1024
VMEM_MIB = {"small_front": 56, "front": 52, "ssm": 56, "mix_prompt": 56, "mix_sample": 32,
            "dispatch": 40, "moe": 58, "combine": 48}


def _rms(x, g):
    return x * lax.rsqrt(jnp.mean(x * x, axis=-1, keepdims=True) + EPS) * g


def _gelu_tanh(x):
    c = math.sqrt(2.0 / math.pi)
    return 0.5 * x * (1.0 + jnp.tanh(c * (x + 0.044715 * (x * x * x))))


def _params(name, sem):
    return dict(name=name, compiler_params=pltpu.CompilerParams(
        dimension_semantics=sem, vmem_limit_bytes=VMEM_MIB[name] * MIB))


def _pack_pairs(a, b):
    return (pltpu.bitcast(a, U32) >> 16) | (pltpu.bitcast(b, U32) & jnp.uint32(HI_MASK))


def _unpack_pairs(w):
    lo = pltpu.bitcast(w << 16, F32)
    hi = pltpu.bitcast(w & jnp.uint32(HI_MASK), F32)
    return jnp.concatenate([lo, hi], axis=-1).astype(BF16)


def _iota2(shape, axis):
    return lax.broadcasted_iota(I32, shape, axis)


def _expand_cols(compact, reps_log2, n_log2):
    q = _iota2((compact.shape[1], 2 << (reps_log2 + n_log2)), 0)
    c = _iota2((compact.shape[1], 2 << (reps_log2 + n_log2)), 1)
    nmask = (1 << n_log2) - 1
    same = jnp.logical_and((q >> n_log2) == (c >> (reps_log2 + n_log2)), (q & nmask) == (c & nmask))
    return jnp.dot(compact, jnp.where(same, 1.0, 0.0).astype(BF16), preferred_element_type=F32)


def _expand_rows(compact, reps_log2, n_log2):
    r = _iota2((2 << (reps_log2 + n_log2), compact.shape[0]), 0)
    q = _iota2((2 << (reps_log2 + n_log2), compact.shape[0]), 1)
    nmask = (1 << n_log2) - 1
    same = jnp.logical_and((r >> (reps_log2 + n_log2)) == (q >> n_log2), (r & nmask) == (q & nmask))
    return jnp.dot(jnp.where(same, 1.0, 0.0).astype(BF16), compact, preferred_element_type=F32)


def _group_mask(shape, row_shift, col_shift, ngroups):
    r = _iota2(shape, 0)
    c = _iota2(shape, 1)
    return ((r >> row_shift) & (ngroups - 1)) == ((c >> col_shift) & (ngroups - 1))


def _small_front_kernel(x_ref, nmix_ref, win_ref, s0r_ref, s0i_ref, b0_ref, b1_ref, cw_ref,
                        bc_ref, cc_ref, abr_ref, abi_ref, nconv_ref,
                        u_ref, z_ref, y_ref, ycn_ref, sr_ref, si_ref, bdb_ref, cm_ref):
    ns, nst = s0r_ref.shape
    cw = u_ref.shape[1]
    n = bc_ref.shape[1] // 2
    nlog = n.bit_length() - 1
    glog = (nst // n).bit_length() - 1
    hlog = (cw >> glog).bit_length() - 1
    bdb_ref[...] = jnp.where(_group_mask(bdb_ref.shape, hlog, nlog, 1 << glog),
                             _expand_cols(bc_ref[...], glog, nlog), 0.0).astype(BF16)
    cm_ref[...] = jnp.where(_group_mask(cm_ref.shape, nlog, hlog, 1 << glog),
                            _expand_rows(cc_ref[...], glog, nlog), 0.0).astype(BF16)
    h = _rms(x_ref[...], nmix_ref[...]).astype(BF16)
    proj = jnp.dot(h, win_ref[...], preferred_element_type=F32)
    u = proj[:, 0:cw]
    zc = proj[:, cw:2 * cw]
    gb = proj[:, 2 * cw:3 * cw]
    gc = proj[:, 3 * cw:4 * cw]
    z = gc * zc
    u_ref[...] = u
    z_ref[...] = z
    bu = jnp.dot(u[:ns].astype(BF16), bdb_ref[...], preferred_element_type=F32)
    abr = abr_ref[...]
    abi = abi_ref[...]
    s0r = s0r_ref[...]
    s0i = s0i_ref[...]
    sr = abr * s0r - abi * s0i + bu[:, :nst]
    si = abr * s0i + abi * s0r + bu[:, nst:]
    sr_ref[...] = sr
    si_ref[...] = si
    scat = jnp.concatenate([sr, si], axis=-1).astype(BF16)
    y_ref[...] = jnp.dot(scat, cm_ref[...], preferred_element_type=F32)
    conv = cw_ref[0:1, :] * b0_ref[...] + cw_ref[1:2, :] * b1_ref[...] + cw_ref[2:3, :] * z[:ns]
    ycn_ref[...] = _rms(gb[:ns] * conv, nconv_ref[...]).astype(BF16)


def _front_kernel(x_ref, nmix_ref, win_ref, zm_ref, cw_ref, nconv_ref,
                  uc_ref, ut_ref, ycn_ref, zt_ref, zbuf, ubuf):
    i = pl.program_id(1)
    nb, tt, d = x_ref.shape
    cw = ycn_ref.shape[2]
    rows = nb * tt
    ncz = tt // CHUNK
    halo = SUBLANE

    @pl.when(i == 0)
    def _():
        zbuf[:, 0:halo, :] = jnp.broadcast_to(zm_ref[...][None], (nb, halo, cw))

    h = _rms(x_ref[...].reshape(rows, d), nmix_ref[...]).astype(BF16)
    u = jnp.dot(h, win_ref[:, 0:cw], preferred_element_type=F32)
    ut_ref[...] = u.astype(BF16).reshape(nb, tt, cw)
    for j in range(cw // LANE):
        ubuf[j] = u[:, j * LANE:(j + 1) * LANE].reshape(nb, tt, LANE)
    for s in range(CHUNK):
        for j in range(cw // LANE):
            piece = ubuf[j, :, pl.ds(s, ncz, stride=CHUNK), :]
            uc_ref[j, :, :, s * LANE:(s + 1) * LANE] = piece.astype(BF16)
    zc = jnp.dot(h, win_ref[:, cw:2 * cw], preferred_element_type=F32)
    gc = jnp.dot(h, win_ref[:, 3 * cw:4 * cw], preferred_element_type=F32)
    z3 = (gc * zc).reshape(nb, tt, cw)
    zbuf[:, halo:halo + tt, :] = z3
    z1 = zbuf[:, halo - 1:halo - 1 + tt, :]
    z2 = zbuf[:, halo - 2:halo - 2 + tt, :]
    conv = cw_ref[0:1, :] * z2 + cw_ref[1:2, :] * z1 + cw_ref[2:3, :] * z3
    gb = jnp.dot(h, win_ref[:, 2 * cw:3 * cw], preferred_element_type=F32)
    yc = gb * conv.reshape(rows, cw)
    ycn_ref[...] = _rms(yc, nconv_ref[...]).astype(BF16).reshape(nb, tt, cw)
    tail = zbuf[:, tt:tt + halo, :]
    zt_ref[...] = tail
    zbuf[:, 0:halo, :] = tail


def _ssm_kernel(u_ref, um_ref, kc_ref, pc_ref, rc_ref, a16_ref, y_ref, sl_ref,
                s_carry, ds_ref, sp_ref, p_s, r_s, t_s):
    th = pl.program_id(1)
    _, nb, cc, w = u_ref.shape
    nst = p_s.shape[1]
    half = nst // 2
    rows = nb * cc
    blk = 2 * LANE
    u = u_ref[0].reshape(rows, w)

    @pl.when(th == 0)
    def _():
        hch = kc_ref.shape[2]
        gpt = LANE // hch
        hlog = hch.bit_length() - 1
        glog = gpt.bit_length() - 1
        nlog = (pc_ref.shape[2] // 2).bit_length() - 1
        p_s[...] = jnp.where(_group_mask(p_s.shape, hlog, nlog, gpt),
                             _expand_cols(pc_ref[0], glog, nlog), 0.0).astype(BF16)
        r_s[...] = jnp.where(_group_mask(r_s.shape, nlog, hlog, gpt),
                             _expand_rows(rc_ref[0], glog, nlog), 0.0).astype(BF16)
        nlag = kc_ref.shape[1] // LANE
        o = _iota2((hch, LANE), 0)
        c = _iota2((hch, LANE), 1)
        spread = jnp.where((c & (hch - 1)) == o, 1.0, 0.0).astype(BF16)
        lagm = jnp.dot(kc_ref[0], spread, preferred_element_type=F32)
        r = _iota2(lagm.shape, 0)
        c = _iota2(lagm.shape, 1)
        lagm = jnp.where(((r >> hlog) & (gpt - 1)) == (c >> hlog), lagm, 0.0).astype(BF16)
        for dlt in range(nlag // 2):
            b0 = lagm[(2 * dlt) * LANE:(2 * dlt + 1) * LANE]
            b1 = lagm[(2 * dlt + 1) * LANE:(2 * dlt + 2) * LANE]
            b2 = lagm[(2 * dlt + 2) * LANE:(2 * dlt + 3) * LANE]
            t_s[dlt, 0:LANE, 0:LANE] = b1
            t_s[dlt, 0:LANE, LANE:blk] = b2
            t_s[dlt, LANE:blk, 0:LANE] = b0
            t_s[dlt, LANE:blk, LANE:blk] = b1
        ds_ref[:, 0:8, :] = jnp.dot(um_ref[0], p_s[...], preferred_element_type=F32).reshape(nb, 8, nst)
        s_carry[...] = ds_ref[:, 0:1, :]

    ds_ref[...] = jnp.dot(u, p_s[...], preferred_element_type=F32).reshape(nb, cc, nst)
    for tb in range(w // blk):
        acc = jnp.dot(u[:, 0:blk], t_s[tb], preferred_element_type=F32)
        for sb in range(1, tb + 1):
            acc = acc + jnp.dot(u[:, sb * blk:(sb + 1) * blk], t_s[tb - sb],
                                preferred_element_type=F32)
        y_ref[0, :, :, tb * blk:(tb + 1) * blk] = acc.reshape(nb, cc, blk)
    ar = a16_ref[0, 0:1, :].reshape(1, 1, half)
    ai = a16_ref[0, 1:2, :].reshape(1, 1, half)
    sr = s_carry[:, :, 0:half]
    si = s_carry[:, :, half:nst]
    for c in range(cc):
        sp_ref[:, c:c + 1, 0:half] = sr
        sp_ref[:, c:c + 1, half:nst] = si
        dr = ds_ref[:, c:c + 1, 0:half]
        di = ds_ref[:, c:c + 1, half:nst]
        sr, si = ar * sr - ai * si + dr, ar * si + ai * sr + di
    s_carry[:, :, 0:half] = sr
    s_carry[:, :, half:nst] = si
    sl_ref[0, :, :, 0:half] = sr
    sl_ref[0, :, :, half:nst] = si

    sp = sp_ref[...].reshape(rows, nst).astype(BF16)
    for tb in range(w // blk):
        acc = jnp.dot(sp, r_s[:, tb * blk:(tb + 1) * blk], preferred_element_type=F32)
        y_ref[0, :, :, tb * blk:(tb + 1) * blk] += acc.reshape(nb, cc, blk)


N_MIX_IN = 12


def _mix_rows(x, yssm, ut, ycn, dsk_ref, wglu_ref, bglu_ref, nssm_ref, wout_ref,
              nffn_ref, wr_ref, br_ref):
    ne = br_ref.shape[0]
    y = _gelu_tanh(yssm + dsk_ref[...] * ut.astype(F32))
    glu = jnp.dot(y.astype(BF16), wglu_ref[...], preferred_element_type=F32) + bglu_ref[...]
    o = y * jax.nn.sigmoid(glu)
    ysn = _rms(o, nssm_ref[...]).astype(BF16)
    mix = jnp.concatenate([ysn, ycn], axis=-1)
    x1 = x + jnp.dot(mix, wout_ref[...], preferred_element_type=F32)
    hf = _rms(x1, nffn_ref[...])
    hf_hi = hf.astype(BF16)
    hf_lo = (hf - hf_hi.astype(F32)).astype(BF16)
    r = hf.shape[0]
    part = jnp.dot(jnp.concatenate([hf_hi, hf_lo], axis=0), wr_ref[...], preferred_element_type=F32)
    logits = (part[0:r, 0:LANE] + part[0:r, LANE:2 * LANE]) + (part[r:2 * r, 0:LANE] + part[r:2 * r, LANE:2 * LANE])
    lt = logits.T[0:ne, :] + br_ref[...]
    iota = lax.broadcasted_iota(I32, lt.shape, 0)
    vals, idxs = [], []
    sel = jnp.zeros(lt.shape, F32)
    for _ in range(TOP_K):
        m = jnp.max(lt, axis=0, keepdims=True)
        ik = jnp.min(jnp.where(lt == m, iota, ne), axis=0, keepdims=True)
        vals.append(m)
        idxs.append(ik)
        hit = iota == ik
        sel = sel + jnp.where(hit, 1.0, 0.0)
        lt = jnp.where(hit, -jnp.inf, lt)
    es = [jnp.exp(v - vals[0]) for v in vals]
    tot = es[0] + es[1] + es[2] + es[3]
    idx = jnp.concatenate(idxs, axis=0)
    gates = jnp.concatenate([e / tot for e in es], axis=0)
    return x1, hf_hi, idx, gates, sel


def _mix_kernel_prompt(*refs):
    x_ref, yc_ref = refs[0], refs[1]
    x1_ref, hf_ref, idx_ref, gate_ref, cnt_ref, ybuf = refs[N_MIX_IN:]
    nj, nb, ncz, _ = yc_ref.shape
    for s in range(CHUNK):
        for j in range(nj):
            ybuf[j, :, pl.ds(s, ncz, stride=CHUNK), :] = yc_ref[j, :, :, s * LANE:(s + 1) * LANE]
    ut_ref, ycn_ref = refs[2], refs[3]
    tt, d = x_ref.shape[1], x_ref.shape[2]
    cw = nj * LANE
    nbc = TOK_TILE // tt
    for t in range(cnt_ref.shape[0]):
        b0, r0 = t * nbc, t * TOK_TILE
        yssm = jnp.concatenate([ybuf[j, b0:b0 + nbc].reshape(TOK_TILE, LANE) for j in range(nj)], axis=-1)
        x1, hf, idx, gates, sel = _mix_rows(
            x_ref[b0:b0 + nbc].reshape(TOK_TILE, d), yssm, ut_ref[b0:b0 + nbc].reshape(TOK_TILE, cw),
            ycn_ref[b0:b0 + nbc].reshape(TOK_TILE, cw), *refs[4:N_MIX_IN])
        x1_ref[r0:r0 + TOK_TILE, :] = x1
        hf_ref[r0:r0 + TOK_TILE, :] = hf
        idx_ref[:, r0:r0 + TOK_TILE] = idx
        gate_ref[:, r0:r0 + TOK_TILE] = gates
        cnt_ref[t] = jnp.sum(sel, axis=1, keepdims=True).astype(I32)


def _mix_kernel_sample(*refs):
    x1, hf, idx, gates, sel = _mix_rows(refs[0][0], refs[1][...], refs[2][0], refs[3][0],
                                        *refs[4:N_MIX_IN])
    x1_ref, hf_ref, idx_ref, gate_ref, cnt_ref = refs[N_MIX_IN + 5:]
    ns = x1.shape[0]
    x1_ref[...] = jnp.zeros(x1_ref.shape, x1_ref.dtype)
    hf_ref[...] = jnp.zeros(hf_ref.shape, hf_ref.dtype)
    idx_ref[...] = jnp.zeros(idx_ref.shape, idx_ref.dtype)
    gate_ref[...] = jnp.zeros(gate_ref.shape, gate_ref.dtype)
    x1_ref[0:ns, :] = x1
    hf_ref[0:ns, :] = hf
    idx_ref[:, 0:ns] = idx
    gate_ref[:, 0:ns] = gates
    cnt_ref[0] = jnp.sum(sel, axis=1, keepdims=True).astype(I32)


class _Table:
    def __init__(self, ref, offset):
        self.ref, self.offset = ref, offset

    def __getitem__(self, k):
        return self.ref[self.offset + k]


def _pack_tables(tables):
    offsets, total = [], 0
    for t in tables:
        offsets.append(total)
        total += t.shape[0]
    return jnp.concatenate(tables), tuple(offsets)


def _split_bf16(x, parts):
    out = []
    for _ in range(parts - 1):
        p = x.astype(BF16)
        out.append(p)
        x = x - p.astype(F32)
    out.append(x.astype(BF16))
    return out


def _run_tables(idx_ref, loff_ref, tile, n_valid, gate_ref=None):
    ne = loff_ref.shape[1]
    tt = idx_ref.shape[1]
    e_iota = lax.broadcasted_iota(I32, (ne, tt), 0)
    tok = tile * tt + lax.broadcasted_iota(I32, (1, tt), 1)
    valid = tok < n_valid
    hits = [jnp.logical_and(e_iota == idx_ref[k:k + 1, :], valid) for k in range(TOP_K)]
    sel = jnp.zeros((ne, tt), F32)
    for h in hits:
        sel = sel + jnp.where(h, 1.0, 0.0)
    before = lax.broadcasted_iota(I32, (tt, tt), 0) < lax.broadcasted_iota(I32, (tt, tt), 1)
    tri = jnp.where(before, 1.0, 0.0).astype(BF16)
    base = jnp.dot(sel.astype(BF16), tri, preferred_element_type=F32) + loff_ref[0].astype(F32)
    base = jnp.where(sel > 0.0, base + 1.0, 0.0)
    b_hi = BF16_EXACT * jnp.floor(base * (1.0 / BF16_EXACT))
    b_lo = base - b_hi
    halves = jnp.concatenate([b_hi, b_lo], axis=0).astype(BF16)
    if gate_ref is None:
        return halves, None
    gate_e = jnp.zeros((ne, tt), F32)
    for k, h in enumerate(hits):
        gate_e = gate_e + jnp.where(h, gate_ref[k:k + 1, :], 0.0)
    return halves, jnp.concatenate(_split_bf16(gate_e, GATE_PIECES), axis=0)


def _rows_onehot(halves, rgn_ref, row0, nrows, pieces=None):
    ne = rgn_ref.shape[2]
    tt = halves.shape[1]

    def own(copies):
        lo = jnp.concatenate([rgn_ref[0, 0:1, :]] * copies, axis=1)
        hi = jnp.concatenate([rgn_ref[0, 1:2, :]] * copies, axis=1)
        r_i = row0 + lax.broadcasted_iota(I32, (nrows, copies * ne), 0)
        return jnp.logical_and(r_i >= lo, r_i < hi)

    want = jnp.dot(jnp.where(own(2), 1.0, 0.0).astype(BF16), halves, preferred_element_type=F32)
    r_f = (row0 + 1 + lax.broadcasted_iota(I32, (nrows, tt), 0)).astype(F32)
    smat = jnp.where(want == r_f, 1.0, 0.0).astype(BF16)
    if pieces is None:
        return smat
    per = lax.dot_general(smat, pieces, (((1,), (1,)), ((), ())), preferred_element_type=F32)
    gcol = jnp.sum(jnp.where(own(GATE_PIECES), per, 0.0), axis=1, keepdims=True)
    return smat, gcol


def _window_copy(buf, slot, hbm, lo, g, sem, to_hbm, rows, align):
    src = buf.at[slot, pl.ds(pl.multiple_of(lo, align), rows)]
    dst = hbm.at[pl.ds(pl.multiple_of(g, align), rows)]
    if to_hbm:
        return pltpu.make_async_copy(src, dst, sem.at[slot])
    return pltpu.make_async_copy(dst, src, sem.at[slot])


def _start_windows(wg_ref, wl_ref, tw_ref, tile, slots, buf, slot, hbm, sem, to_hbm, rows, align):
    def per_window(w, c):
        k = tile * slots + w
        lo = w * rows if wl_ref is None else wl_ref[k]
        _window_copy(buf, slot, hbm, lo, wg_ref[k], sem, to_hbm, rows, align).start()
        return c

    lax.fori_loop(0, tw_ref[tile], per_window, 0)


def _wait_windows(count, buf, slot, hbm, sem, to_hbm, rows, align):
    batch = 8

    def many(w, c):
        _window_copy(buf, slot, hbm, 0, 0, sem, to_hbm, batch * rows, align).wait()
        return c

    def one(w, c):
        _window_copy(buf, slot, hbm, 0, 0, sem, to_hbm, rows, align).wait()
        return c

    lax.fori_loop(0, count // batch, many, 0)
    lax.fori_loop(0, count % batch, one, 0)


def _dispatch_kernel(offsets, tab_ref,
                     hf_ref, idx_ref, gate_ref, loff_ref, rgn_ref, xs_ref, buf, carry, sem):
    wg_ref, wl_ref, mlo_ref, mtg_ref, keep_ref, tw_ref, used_ref, nv_ref = [
        _Table(tab_ref, o) for o in offsets]
    i = pl.program_id(0)
    nt = pl.num_programs(0)
    ne = loff_ref.shape[1]
    tt, d = hf_ref.shape
    nrun = buf.shape[1]
    slot = i % 2

    @pl.when(i == 0)
    def _():
        carry[...] = jnp.zeros(carry.shape, carry.dtype)
        buf[...] = jnp.zeros(buf.shape, buf.dtype)

    halves, pieces = _run_tables(idx_ref, loff_ref, i, nv_ref[0], gate_ref)

    def emit(row0, nrows):
        smat, gcol = _rows_onehot(halves, rgn_ref, row0, nrows, pieces)
        xr = jnp.dot(smat, hf_ref[...], preferred_element_type=F32)
        lane0 = lax.broadcasted_iota(I32, (nrows, GATE_COLS), 1) == 0
        buf[slot, row0:row0 + nrows, 0:d // 2] = _pack_pairs(xr[:, :d // 2], xr[:, d // 2:])
        buf[slot, row0:row0 + nrows, d // 2:] = pltpu.bitcast(jnp.where(lane0, gcol, 0.0), U32)

    emit(0, DISPATCH_BASE_ROWS)
    past_base = used_ref[i] > DISPATCH_BASE_ROWS

    @pl.when(past_base)
    def _():
        emit(DISPATCH_BASE_ROWS, nrun - DISPATCH_BASE_ROWS)

    @pl.when(jnp.logical_not(past_base))
    def _():
        buf[slot, nrun - 2 * SUBLANE:nrun, :] = jnp.zeros((2 * SUBLANE, buf.shape[2]), buf.dtype)

    def merge(e, c):
        k = i * ne + e
        lo = pl.multiple_of(mlo_ref[k], SUBLANE)
        buf[slot, pl.ds(lo, SUBLANE), :] = buf[slot, pl.ds(lo, SUBLANE), :] | carry[e]
        tg = pl.multiple_of(mtg_ref[k], SUBLANE)
        carry[e] = jnp.where(keep_ref[k] > 0, carry[e], buf[slot, pl.ds(tg, SUBLANE), :])
        return c

    lax.fori_loop(0, ne, merge, 0, unroll=4)

    @pl.when(i > 0)
    def _():
        _wait_windows(tw_ref[i - 1], buf, 1 - slot, xs_ref, sem, True, WIN, SUBLANE)

    _start_windows(wg_ref, wl_ref, tw_ref, i, MAX_WINDOWS, buf, slot, xs_ref, sem, True, WIN, SUBLANE)

    @pl.when(i == nt - 1)
    def _():
        _wait_windows(tw_ref[i], buf, slot, xs_ref, sem, True, WIN, SUBLANE)


def _expert_weight_copies(wg_hbm, wd_hbm, wg_f32, wd_f32, sem, e, slot):
    return (pltpu.make_async_copy(wg_hbm.at[e], wg_f32.at[slot], sem.at[0, slot]),
            pltpu.make_async_copy(wd_hbm.at[e], wd_f32.at[slot], sem.at[1, slot]))


def _moe_kernel(offsets, tab_ref,
                x_hbm, bg_ref, bd_ref, wg_hbm, wd_hbm, y_hbm,
                wg_f32, wd_f32, wg_bf, wd_bf, sem, xbuf, xsem, ybuf, ysem):
    be_ref, bv_ref, first_ref, nxt_ref, slot_ref, wr_ref = [_Table(tab_ref, o) for o in offsets]
    i = pl.program_id(0)
    nblk = pl.num_programs(0)
    e = be_ref[i]
    dff = wd_bf.shape[0]
    nslots, bm = xbuf.shape[0], xbuf.shape[1]
    sub = MOE_SUB
    nw = xbuf.shape[2] - GATE_COLS

    def x_copy(j):
        return pltpu.make_async_copy(x_hbm.at[pl.ds(pl.multiple_of(j * bm, bm), bm)],
                                     xbuf.at[j % nslots], xsem.at[j % nslots])

    def x_start(j):
        @pl.when(jnp.logical_and(j < nblk, bv_ref[jnp.minimum(j, nblk - 1)] > 0))
        def _():
            x_copy(j).start()

    @pl.when(i == 0)
    def _():
        for j in range(nslots - 1):
            x_start(i + j)

    x_start(i + nslots - 1)

    @pl.when(bv_ref[i] > 0)
    def _():
        x_copy(i).wait()

    x_ref = xbuf.at[i % nslots]

    nchains = bm // sub
    y_ref = ybuf.at[i % 2]

    def y_each(j, act):
        for chunks in range(1, nchains + 1):
            @pl.when(wr_ref[j] == chunks)
            def _(chunks=chunks):
                act(pltpu.make_async_copy(ybuf.at[j % 2, pl.ds(0, chunks * sub)],
                                          y_hbm.at[pl.ds(pl.multiple_of(j * bm, bm), chunks * sub)],
                                          ysem.at[j % 2]))

    @pl.when(i >= 2)
    def _():
        y_each(i - 2, lambda cp: cp.wait())

    @pl.when(first_ref[i] > 0)
    def _():
        slot = slot_ref[i]

        @pl.when(first_ref[i] > 1)
        def _():
            for cp in _expert_weight_copies(wg_hbm, wd_hbm, wg_f32, wd_f32, sem, e, slot):
                cp.start()

        for cp in _expert_weight_copies(wg_hbm, wd_hbm, wg_f32, wd_f32, sem, e, slot):
            cp.wait()

        @pl.when(nxt_ref[i] >= 0)
        def _():
            for cp in _expert_weight_copies(wg_hbm, wd_hbm, wg_f32, wd_f32, sem, nxt_ref[i], 1 - slot):
                cp.start()

        wg_bf[...] = wg_f32[slot].astype(BF16)
        wd_bf[...] = wd_f32[slot].astype(BF16)

    def rows(r0, nrows):
        sizes = [MOE_CHAIN] * (nrows // MOE_CHAIN) + ([nrows % MOE_CHAIN] if nrows % MOE_CHAIN else [])
        lo = r0
        for size in sizes:
            _chain(lo, size)
            lo += size

    def _chain(lo, n):
        live = lax.broadcasted_iota(I32, (n, 1), 0) + lo < bv_ref[i]
        x = jnp.where(live, _unpack_pairs(x_ref[lo:lo + n, 0:nw]), jnp.zeros((), BF16))
        route = jnp.where(live, pltpu.bitcast(x_ref[lo:lo + n, nw:], F32)[:, 0:1], 0.0)
        gu = jnp.dot(x, wg_bf[...], preferred_element_type=F32) + bg_ref[0]
        gate = jnp.minimum(gu[:, :dff], SWIGLU_LIMIT)
        up = jnp.clip(gu[:, dff:], -SWIGLU_LIMIT, SWIGLU_LIMIT)
        h = gate * jax.nn.sigmoid(SWIGLU_ALPHA * gate) * (up + 1.0)
        y = jnp.dot(h.astype(BF16), wd_bf[...], preferred_element_type=F32) + bd_ref[0]
        yr = (route * y).astype(BF16).astype(F32)
        half = yr.shape[1] // 2
        y_ref[lo:lo + n, :] = _pack_pairs(yr[:, :half], yr[:, half:])

    for live_chains in range(nchains + 1):
        lo_rows, hi_rows = (live_chains - 1) * sub, live_chains * sub

        @pl.when(jnp.logical_and(bv_ref[i] > lo_rows, bv_ref[i] <= hi_rows) if live_chains
                 else bv_ref[i] <= 0)
        def _(used=hi_rows):
            if used:
                rows(0, used)
            if used < bm:
                y_ref[used:bm, :] = jnp.zeros((bm - used, y_ref.shape[1]), y_ref.dtype)

    y_each(i, lambda cp: cp.start())

    @pl.when(i == nblk - 1)
    def _():
        @pl.when(i >= 1)
        def _():
            y_each(i - 1, lambda cp: cp.wait())

        y_each(i, lambda cp: cp.wait())


def _combine_tile(step, nt):
    return (step + nt - 1) % nt


def _combine_kernel(offsets, tab_ref,
                    x1_ref, idx_ref, loff_ref, rgn_ref, nf_ref, yb_ref,
                    yp_ref, ys_ref, buf, acc, sem):
    wg_ref, tw_ref, nv_ref = [_Table(tab_ref, o) for o in offsets]
    s = pl.program_id(0)
    nt = pl.num_programs(0)
    tt, d = x1_ref.shape
    nrun = buf.shape[1]
    tile = _combine_tile(s, nt)
    slot = s % 2

    @pl.when(s == 0)
    def _():
        buf[...] = jnp.zeros(buf.shape, buf.dtype)
        _start_windows(wg_ref, None, tw_ref, tile, MAX_WINDOWS, buf, slot, yb_ref, sem, False,
                       WIN, SUBLANE)

    @pl.when(s + 1 < nt)
    def _():
        _start_windows(wg_ref, None, tw_ref, _combine_tile(s + 1, nt), MAX_WINDOWS, buf, 1 - slot,
                       yb_ref, sem, False, WIN, SUBLANE)

    halves, _ = _run_tables(idx_ref, loff_ref, tile, nv_ref[0])

    def gathered(row0, nrows):
        smat = _rows_onehot(halves, rgn_ref, row0, nrows)
        yrun = _unpack_pairs(buf[slot, row0:row0 + nrows])
        return lax.dot_general(smat, yrun, (((0,), (0,)), ((), ())), preferred_element_type=F32)

    _wait_windows(tw_ref[tile], buf, slot, yb_ref, sem, False, WIN, SUBLANE)
    acc[...] = gathered(0, COMBINE_BASE_ROWS)
    for row0 in range(COMBINE_BASE_ROWS, nrun, COMBINE_STEP_ROWS):
        @pl.when(WIN * tw_ref[tile] > row0)
        def _(row0=row0):
            acc[...] += gathered(row0, min(COMBINE_STEP_ROWS, nrun - row0))

    out = _rms(x1_ref[...] + acc[...], nf_ref[...])

    @pl.when(s == 0)
    def _():
        ys_ref[...] = out[0:ys_ref.shape[0], :]

    @pl.when(s > 0)
    def _():
        yp_ref[...] = out.reshape(yp_ref.shape)


def _ssm_matrices(a_re, a_im, log_dt, b_re, b_im, c_re, c_im):
    g, n = a_re.shape
    hch = b_re.shape[2]
    gpt = LANE // hch
    nj = g // gpt
    a = lax.complex(a_re, a_im)
    dta = a * jnp.exp(log_dt)[:, None]
    a_bar = jnp.exp(dta)
    bb = ((a_bar - 1.0) / a)[:, :, None] * lax.complex(b_re, b_im)
    cc = lax.complex(c_re, c_im)
    ks = jnp.arange(CHUNK + 1, dtype=F32)
    pw = jnp.exp(dta[None] * ks[:, None, None])
    kk = jnp.real(jnp.einsum('gon,kgn,gni->kgio', cc, pw[:CHUNK], bb))
    kk = jnp.concatenate([jnp.zeros_like(kk[:1]), kk], axis=0)
    kc = kk.reshape(CHUNK + 1, nj, gpt * hch, hch).transpose(1, 0, 2, 3)
    kc = kc.reshape(nj, (CHUNK + 1) * LANE, hch).astype(BF16)
    pwr, pwi = jnp.real(pw), jnp.imag(pw)
    bbr = jnp.real(bb).transpose(0, 2, 1).reshape(1, g * hch, n)
    bbi = jnp.imag(bb).transpose(0, 2, 1).reshape(1, g * hch, n)
    par = jnp.repeat(pwr[CHUNK - 1::-1][:CHUNK], hch, axis=1)
    pai = jnp.repeat(pwi[CHUNK - 1::-1][:CHUNK], hch, axis=1)
    pc = jnp.concatenate([par * bbr - pai * bbi, par * bbi + pai * bbr], axis=-1)
    pc = pc.reshape(CHUNK, nj, gpt * hch, 2 * n).transpose(1, 0, 2, 3)
    pc = pc.reshape(nj, CHUNK * LANE, 2 * n).astype(BF16)
    ccr = jnp.real(cc).transpose(2, 0, 1).reshape(n, 1, g * hch)
    cci = jnp.imag(cc).transpose(2, 0, 1).reshape(n, 1, g * hch)
    qar = jnp.repeat(pwr[1:CHUNK + 1].transpose(2, 0, 1), hch, axis=2)
    qai = jnp.repeat(pwi[1:CHUNK + 1].transpose(2, 0, 1), hch, axis=2)
    rc = jnp.stack([ccr * qar - cci * qai, -(ccr * qai + cci * qar)], axis=0)
    rc = rc.reshape(2 * n, CHUNK, nj, gpt * hch).transpose(2, 0, 1, 3)
    rc = rc.reshape(nj, 2 * n, CHUNK * LANE).astype(BF16)
    a16 = pw[CHUNK].reshape(nj, 1, gpt * n)
    a16 = jnp.concatenate([jnp.real(a16), jnp.imag(a16)], axis=1)
    bc = jnp.stack([jnp.real(bb), jnp.imag(bb)], axis=0).transpose(1, 3, 0, 2)
    bc = bc.reshape(g * hch, 2 * n).astype(BF16)
    c2 = jnp.stack([jnp.real(cc), -jnp.imag(cc)], axis=0).transpose(0, 3, 1, 2)
    c2 = c2.reshape(2 * n, g * hch).astype(BF16)
    abr = jnp.real(a_bar).reshape(1, g * n)
    abi = jnp.imag(a_bar).reshape(1, g * n)
    return kc, pc, rc, a16, bc, c2, abr, abi


def _full(shape):
    return pl.BlockSpec(shape, lambda *_: (0,) * len(shape))


def kernel(x_prompt, x_sample, state_ssm_re, state_ssm_im, state_conv, meta_tokens, norm_mix, w_in,
           ssm_a_re, ssm_a_im, ssm_log_dt, ssm_b_re, ssm_b_im, ssm_c_re, ssm_c_im, ssm_d, w_glu, b_glu,
           conv_w, norm_out_ssm, norm_out_conv, w_out, norm_ffn, w_router, b_router, w_gate_up,
           b_gate_up, w_down, b_down, norm_final):
    nb, seq, d = x_prompt.shape
    ns = x_sample.shape[0]
    depth, _, g, n = state_ssm_re.shape
    assert depth == 1 and x_sample.shape[1] == 1 and meta_tokens.shape[0] == CHUNK
    cw = conv_w.shape[2]
    nj = cw // LANE
    ne = w_router.shape[2]
    dff = w_down.shape[2]
    nst = g * n
    nbt = nb // 2
    tt = 256
    n_chunks = seq // CHUNK
    tp = nb * seq
    tall = tp + ns
    tm = 128
    nbm = nb
    rows_p = nbm * tm
    assert rows_p % TOK_TILE == 0 and TOK_TILE % tm == 0 and ns <= TOK_TILE and d % 2 == 0
    n_tiles = tp // TOK_TILE + 1
    ta = n_tiles * TOK_TILE

    kc, pc, rc, a16, bc, c2, abr, abi = _ssm_matrices(
        ssm_a_re[0], ssm_a_im[0], ssm_log_dt[0], ssm_b_re[0], ssm_b_im[0], ssm_c_re[0], ssm_c_im[0])
    win_bf = w_in[0].astype(BF16)
    nmix = norm_mix[0].reshape(1, d)
    nconv = norm_out_conv[0].reshape(1, cw)
    cwt = conv_w[0]

    xsm = jnp.concatenate([x_sample.reshape(ns, d), meta_tokens], axis=0)
    nsm = ns + CHUNK
    s0r = state_ssm_re[0].reshape(ns, nst)
    s0i = state_ssm_im[0].reshape(ns, nst)
    buf0 = state_conv[0, :, 0, :]
    buf1 = state_conv[0, :, 1, :]
    u_sm, z_sm, y_s, ycn_s, sr_s, si_s = pl.pallas_call(
        _small_front_kernel,
        out_shape=(jax.ShapeDtypeStruct((nsm, cw), F32), jax.ShapeDtypeStruct((nsm, cw), F32),
                   jax.ShapeDtypeStruct((ns, cw), F32), jax.ShapeDtypeStruct((ns, cw), BF16),
                   jax.ShapeDtypeStruct((ns, nst), F32), jax.ShapeDtypeStruct((ns, nst), F32)),
        scratch_shapes=[pltpu.VMEM((cw, 2 * nst), BF16), pltpu.VMEM((2 * nst, cw), BF16)],
        **_params("small_front", None),
    )(xsm, nmix, win_bf, s0r, s0i, buf0, buf1, cwt, bc, c2, abr, abi, nconv)
    u_meta = u_sm[ns:]
    z_meta8 = z_sm[ns + CHUNK - 8:]
    new_conv_s = jnp.stack([buf1, z_sm[:ns]], axis=1)[None]
    new_re_s = sr_s.reshape(1, ns, g, n)
    new_im_s = si_s.reshape(1, ns, g, n)

    wch = CHUNK * LANE
    u4c, u_tok, ycn_p, ztail = pl.pallas_call(
        _front_kernel,
        grid=(nb // nbt, seq // tt),
        in_specs=[pl.BlockSpec((nbt, tt, d), lambda b, i: (b, i, 0)),
                  _full((1, d)), _full((d, 4 * cw)), _full((8, cw)), _full((3, cw)), _full((1, cw))],
        out_specs=(pl.BlockSpec((nj, nbt, tt // CHUNK, wch), lambda b, i: (0, b, i, 0)),
                   pl.BlockSpec((nbt, tt, cw), lambda b, i: (b, i, 0)),
                   pl.BlockSpec((nbt, tt, cw), lambda b, i: (b, i, 0)),
                   pl.BlockSpec((nbt, 8, cw), lambda b, i: (b, 0, 0))),
        out_shape=(jax.ShapeDtypeStruct((nj, nb, n_chunks, wch), BF16),
                   jax.ShapeDtypeStruct((nb, seq, cw), BF16),
                   jax.ShapeDtypeStruct((nb, seq, cw), BF16),
                   jax.ShapeDtypeStruct((nb, 8, cw), F32)),
        scratch_shapes=[pltpu.VMEM((nbt, tt + 8, cw), F32), pltpu.VMEM((nj, nbt, tt, LANE), F32)],
        **_params("front", ("arbitrary", "arbitrary")),
    )(x_prompt, nmix, win_bf, z_meta8, cwt, nconv)
    new_conv_p = ztail[:, 6:8, :][None]

    cc = n_chunks // 2
    um = u_meta.reshape(CHUNK, nj, LANE).transpose(1, 0, 2).reshape(nj, 1, wch)
    um = jnp.broadcast_to(um, (nj, 8 * nb, wch)).astype(BF16)
    gpt = g // nj
    nstj = 2 * gpt * n
    hch = cw // g
    y4c, s_last = pl.pallas_call(
        _ssm_kernel,
        grid=(nj, n_chunks // cc),
        in_specs=[pl.BlockSpec((1, nb, cc, wch), lambda j, t: (j, 0, t, 0)),
                  pl.BlockSpec((1, 8 * nb, wch), lambda j, t: (j, 0, 0)),
                  pl.BlockSpec((1, (CHUNK + 1) * LANE, hch), lambda j, t: (j, 0, 0)),
                  pl.BlockSpec((1, wch, 2 * n), lambda j, t: (j, 0, 0)),
                  pl.BlockSpec((1, 2 * n, wch), lambda j, t: (j, 0, 0)),
                  pl.BlockSpec((1, 2, nstj // 2), lambda j, t: (j, 0, 0))],
        out_specs=(pl.BlockSpec((1, nb, cc, wch), lambda j, t: (j, 0, t, 0)),
                   pl.BlockSpec((1, nb, 1, nstj), lambda j, t: (j, 0, 0, 0))),
        out_shape=(jax.ShapeDtypeStruct((nj, nb, n_chunks, wch), F32),
                   jax.ShapeDtypeStruct((nj, nb, 1, nstj), F32)),
        scratch_shapes=[pltpu.VMEM((nb, 1, nstj), F32), pltpu.VMEM((nb, cc, nstj), F32),
                        pltpu.VMEM((nb, cc, nstj), F32),
                        pltpu.VMEM((wch, nstj), BF16), pltpu.VMEM((nstj, wch), BF16),
                        pltpu.VMEM((CHUNK // 2, 2 * LANE, 2 * LANE), BF16)],
        **_params("ssm", ("parallel", "arbitrary")),
    )(u4c, um, kc, pc, rc, a16)
    sl = s_last.reshape(nj, nb, 2, gpt, n)
    new_re_p = sl[:, :, 0].transpose(1, 0, 2, 3).reshape(1, nb, g, n)
    new_im_p = sl[:, :, 1].transpose(1, 0, 2, 3).reshape(1, nb, g, n)

    dsk = ssm_d[0].reshape(1, cw)
    wglu_bf = w_glu[0].astype(BF16)
    bglu = b_glu[0].reshape(1, cw)
    nssm = norm_out_ssm[0].reshape(1, cw)
    wout_bf = w_out[0].astype(BF16)
    nffn = norm_ffn[0].reshape(1, d)
    wr_pad = jnp.zeros((d, LANE), F32).at[:, :ne].set(w_router[0])
    wr_hi = wr_pad.astype(BF16)
    wr_lo = (wr_pad - wr_hi.astype(F32)).astype(BF16)
    br = b_router[0].reshape(ne, 1)
    wr2 = jnp.concatenate([wr_hi, wr_lo], axis=1)
    mix_w = (dsk, wglu_bf, bglu, nssm, wout_bf, nffn, wr2, br)
    mix_w_specs = [_full((1, cw)), _full((cw, cw)), _full((1, cw)), _full((1, cw)), _full((2 * cw, d)),
                   _full((1, d)), _full((d, 2 * LANE)), _full((ne, 1))]
    assert 4 + len(mix_w) == N_MIX_IN
    mix_out_shape = (jax.ShapeDtypeStruct((ta, d), F32), jax.ShapeDtypeStruct((ta, d), BF16),
                     jax.ShapeDtypeStruct((TOP_K, ta), I32), jax.ShapeDtypeStruct((TOP_K, ta), F32),
                     jax.ShapeDtypeStruct((n_tiles, ne, 1), I32))
    tpm = rows_p // TOK_TILE
    nbg = nb // nbm
    x1_all, hf_all, idx_all, gate_all, cnt = pl.pallas_call(
        _mix_kernel_prompt,
        grid=(seq // tm, nbg),
        in_specs=[pl.BlockSpec((nbm, tm, d), lambda i, b: (b, i, 0)),
                  pl.BlockSpec((nj, nbm, tm // CHUNK, wch), lambda i, b: (0, b, i, 0)),
                  pl.BlockSpec((nbm, tm, cw), lambda i, b: (b, i, 0)),
                  pl.BlockSpec((nbm, tm, cw), lambda i, b: (b, i, 0))] + mix_w_specs,
        out_specs=(pl.BlockSpec((rows_p, d), lambda i, b: (i * nbg + b, 0)),
                   pl.BlockSpec((rows_p, d), lambda i, b: (i * nbg + b, 0)),
                   pl.BlockSpec((TOP_K, rows_p), lambda i, b: (0, i * nbg + b)),
                   pl.BlockSpec((TOP_K, rows_p), lambda i, b: (0, i * nbg + b)),
                   pl.BlockSpec((tpm, ne, 1), lambda i, b: (i * nbg + b, 0, 0))),
        out_shape=mix_out_shape,
        scratch_shapes=[pltpu.VMEM((nj, nbm, tm, LANE), F32)],
        **_params("mix_prompt", ("parallel", "parallel")),
    )(x_prompt, y4c, u_tok, ycn_p, *mix_w)

    last = n_tiles - 1
    any_spec = pl.BlockSpec(memory_space=pl.ANY)
    x1_all, hf_all, idx_all, gate_all, cnt = pl.pallas_call(
        _mix_kernel_sample,
        grid=(1,),
        in_specs=[_full((1, ns, d)), _full((ns, cw)), _full((1, ns, cw)),
                  _full((1, ns, cw))] + mix_w_specs + [any_spec] * 5,
        out_specs=(pl.BlockSpec((TOK_TILE, d), lambda i: (last, 0)),
                   pl.BlockSpec((TOK_TILE, d), lambda i: (last, 0)),
                   pl.BlockSpec((TOP_K, TOK_TILE), lambda i: (0, last)),
                   pl.BlockSpec((TOP_K, TOK_TILE), lambda i: (0, last)),
                   pl.BlockSpec((1, ne, 1), lambda i: (last, 0, 0))),
        out_shape=mix_out_shape,
        input_output_aliases={N_MIX_IN + k: k for k in range(5)},
        **_params("mix_sample", ("arbitrary",)),
    )(x_sample.reshape(1, ns, d), y_s, u_sm[:ns].astype(BF16).reshape(1, ns, cw),
      ycn_s.reshape(1, ns, cw), *mix_w, x1_all, hf_all, idx_all, gate_all, cnt)

    bm = MOE_ROWS
    cnt2 = cnt.reshape(n_tiles, ne)
    before = jnp.cumsum(cnt2, axis=0) - cnt2
    count = jnp.sum(cnt2, axis=0)
    padded = ((count + WIN + bm - 1) // bm) * bm
    pend = jnp.cumsum(padded)
    pstart = pend - padded
    phase = before % SUBLANE
    span = jnp.where(cnt2 > 0, phase + cnt2, 0)
    gstart = (pstart[None, :] + before - phase).astype(I32).reshape(-1)
    nwin = ((span + WIN - 1) // WIN).astype(I32)
    reg8 = ((span + SUBLANE - 1) // SUBLANE) * SUBLANE
    loff = (jnp.cumsum(reg8, axis=1) - reg8).astype(I32)
    tail = jnp.where(span % SUBLANE != 0, loff + (span // SUBLANE) * SUBLANE, -1).astype(I32)
    twin = jnp.sum(nwin, axis=1).astype(I32)
    n_blocks = (ta * TOP_K + ne * (WIN + bm - 1) + bm - 1) // bm
    cap = n_blocks * bm
    blk0 = jnp.arange(n_blocks, dtype=I32) * bm
    blk_e = jnp.minimum(jnp.sum((pend[None, :] <= blk0[:, None]).astype(I32), axis=1), ne - 1)
    e_ar = jnp.arange(ne, dtype=I32)
    blk_hot = blk_e[:, None] == e_ar[None, :]

    def _of_block(per_expert):
        return jnp.sum(jnp.where(blk_hot, per_expert[None, :], 0), axis=1)

    blk_valid = jnp.clip(_of_block(count) - (blk0 - _of_block(pstart)), 0, bm).astype(I32)
    has = count > 0
    later = jnp.logical_and(e_ar[None, :] > e_ar[:, None], has[None, :])
    nxt_e = jnp.min(jnp.where(later, e_ar[None, :], ne), axis=1)
    nxt_e = jnp.where(nxt_e < ne, nxt_e, -1)
    ordinal = jnp.cumsum(has.astype(I32)) - 1
    is_first = jnp.logical_and(blk_valid > 0, blk0 == _of_block(pstart))
    blk_first = jnp.where(is_first, jnp.where(_of_block(ordinal) == 0, 2, 1), 0).astype(I32)
    blk_next = _of_block(nxt_e).astype(I32)
    blk_slot = (_of_block(ordinal) % 2).astype(I32)
    nvalid = jnp.full((1,), tall, I32)
    loff_al = (WIN * (jnp.cumsum(nwin, axis=1) - nwin)).astype(I32)

    def _window_list(nw, slots, rows, first_hbm, first_buf):
        wcum = jnp.cumsum(nw, axis=1)
        wslot = jnp.arange(slots, dtype=I32)
        w_hot = jnp.logical_and(wslot[None, :, None] >= (wcum - nw)[:, None, :],
                                wslot[None, :, None] < wcum[:, None, :])

        def _of_window(per_tile_expert):
            return jnp.sum(jnp.where(w_hot, per_tile_expert[:, None, :], 0), axis=2)

        w_in_run = wslot[None, :] - _of_window(wcum - nw)
        return [(_of_window(f) + rows * w_in_run).astype(I32).reshape(-1) for f in (first_hbm, first_buf)]

    w_hbm, w_buf = _window_list(nwin, MAX_WINDOWS, WIN, gstart.reshape(n_tiles, ne), loff)
    nrun_d = _dispatch_run_rows(ne)
    zero_grp, spare_grp = nrun_d - 2 * SUBLANE, nrun_d - SUBLANE
    m_lo = jnp.where(nwin > 0, loff, spare_grp).astype(I32).reshape(-1)
    m_tg = jnp.where(jnp.logical_and(nwin > 0, tail >= 0), tail, zero_grp).astype(I32).reshape(-1)
    m_keep = (nwin == 0).astype(I32).reshape(-1)
    used_d = jnp.sum(reg8, axis=1).astype(I32)
    tab_d, off_d = _pack_tables((w_hbm, w_buf, m_lo, m_tg, m_keep, twin, used_d, nvalid))
    tab_c, off_c = _pack_tables((w_hbm, twin, nvalid))
    blk_wr = jnp.clip((blk_valid + WIN + MOE_SUB - 1) // MOE_SUB, 1, bm // MOE_SUB).astype(I32)
    tab_m, off_m = _pack_tables((blk_e, blk_valid, blk_first, blk_next, blk_slot, blk_wr))
    rowoff_d = (loff + phase).astype(I32).reshape(n_tiles, ne, 1)
    rowoff_c = (loff_al + phase).astype(I32).reshape(n_tiles, ne, 1)
    rgn_d = jnp.stack([loff, loff + reg8], axis=1).astype(I32)
    rgn_c = jnp.stack([loff_al, loff_al + WIN * nwin], axis=1).astype(I32)
    xw = d // 2 + GATE_COLS

    xs = pl.pallas_call(
        functools.partial(_dispatch_kernel, off_d),
        grid_spec=pltpu.PrefetchScalarGridSpec(
            num_scalar_prefetch=1,
            grid=(n_tiles,),
            in_specs=[pl.BlockSpec((TOK_TILE, d), lambda i, *_: (i, 0)),
                      pl.BlockSpec((TOP_K, TOK_TILE), lambda i, *_: (0, i)),
                      pl.BlockSpec((TOP_K, TOK_TILE), lambda i, *_: (0, i)),
                      pl.BlockSpec((1, ne, 1), lambda i, *_: (i, 0, 0)),
                      pl.BlockSpec((1, 2, ne), lambda i, *_: (i, 0, 0))],
            out_specs=pl.BlockSpec(memory_space=pl.ANY),
            scratch_shapes=[pltpu.VMEM((2, _dispatch_run_rows(ne), xw), U32),
                            pltpu.VMEM((ne, SUBLANE, xw), U32), pltpu.SemaphoreType.DMA((2,))]),
        out_shape=jax.ShapeDtypeStruct((cap, xw), U32),
        **_params("dispatch", ("arbitrary",)),
    )(tab_d, hf_all, idx_all, gate_all, rowoff_d, rgn_d)

    yb = pl.pallas_call(
        functools.partial(_moe_kernel, off_m),
        grid_spec=pltpu.PrefetchScalarGridSpec(
            num_scalar_prefetch=1,
            grid=(n_blocks,),
            in_specs=[pl.BlockSpec(memory_space=pl.ANY),
                      pl.BlockSpec((1, 1, 2 * dff), lambda i, tab: (tab[off_m[0] + i], 0, 0)),
                      pl.BlockSpec((1, 1, d), lambda i, tab: (tab[off_m[0] + i], 0, 0)),
                      pl.BlockSpec(memory_space=pl.ANY), pl.BlockSpec(memory_space=pl.ANY)],
            out_specs=pl.BlockSpec(memory_space=pl.ANY),
            scratch_shapes=[pltpu.VMEM((2, d, 2 * dff), F32), pltpu.VMEM((2, dff, d), F32),
                            pltpu.VMEM((d, 2 * dff), BF16), pltpu.VMEM((dff, d), BF16),
                            pltpu.SemaphoreType.DMA((2, 2)),
                            pltpu.VMEM((MOE_X_SLOTS, bm, xw), U32), pltpu.SemaphoreType.DMA((MOE_X_SLOTS,)),
                            pltpu.VMEM((2, bm, d // 2), U32), pltpu.SemaphoreType.DMA((2,))]),
        out_shape=jax.ShapeDtypeStruct((cap, d // 2), U32),
        **_params("moe", ("arbitrary",)),
    )(tab_m, xs, b_gate_up[0].reshape(ne, 1, 2 * dff),
      b_down[0].reshape(ne, 1, d), w_gate_up[0], w_down[0])

    nfin = norm_final.reshape(1, d)
    nbh = TOK_TILE // tm

    def _tile_of(s):
        return (s + n_tiles - 1) % n_tiles

    tiles_per_time = nbg * tpm

    def _yp_index(s, *_):
        t = jnp.maximum(s - 1, 0)
        return (t % tiles_per_time, t // tiles_per_time, 0)

    y_p, y_sm = pl.pallas_call(
        functools.partial(_combine_kernel, off_c),
        grid_spec=pltpu.PrefetchScalarGridSpec(
            num_scalar_prefetch=1,
            grid=(n_tiles,),
            in_specs=[pl.BlockSpec((TOK_TILE, d), lambda s, *_: (_tile_of(s), 0)),
                      pl.BlockSpec((TOP_K, TOK_TILE), lambda s, *_: (0, _tile_of(s))),
                      pl.BlockSpec((1, ne, 1), lambda s, *_: (_tile_of(s), 0, 0)),
                      pl.BlockSpec((1, 2, ne), lambda s, *_: (_tile_of(s), 0, 0)),
                      pl.BlockSpec((1, d), lambda s, *_: (0, 0)),
                      pl.BlockSpec(memory_space=pl.ANY)],
            out_specs=(pl.BlockSpec((nbh, tm, d), _yp_index),
                       pl.BlockSpec((ns, d), lambda s, *_: (0, 0))),
            scratch_shapes=[pltpu.VMEM((2, _combine_run_rows(ne), d // 2), U32),
                            pltpu.VMEM((TOK_TILE, d), F32), pltpu.SemaphoreType.DMA((2,))]),
        out_shape=(jax.ShapeDtypeStruct((nb, seq, d), F32), jax.ShapeDtypeStruct((ns, d), F32)),
        **_params("combine", ("arbitrary",)),
    )(tab_c, x1_all, idx_all, rowoff_c, rgn_c, nfin, yb)

    return (y_p, y_sm.reshape(ns, 1, d), new_re_p, new_im_p, new_conv_p,
            new_re_s, new_im_s, new_conv_s)
```

```python
import functools
import math

import jax
import jax.numpy as jnp
from jax import lax
from jax.experimental import pallas as pl
from jax.experimental.pallas import tpu as pltpu

F32 = jnp.float32
BF16 = jnp.bfloat16
U32 = jnp.uint32
I32 = jnp.int32
EPS = 1e-5
CHUNK = 16
LANE = 128
TOP_K = 4
SWIGLU_LIMIT = 7.0
SWIGLU_ALPHA = 1.702
MOE_ROWS = 1024
MOE_SUB = 256
MOE_CHAIN = 512
MOE_X_SLOTS = 3
WEIGHT_DMA_PRIORITY = 1
TOK_TILE = 256
WIN = 32
SUBLANE = 8
GATE_COLS = LANE


MAX_WINDOWS = 72


def _dispatch_run_rows(ne):
    return TOP_K * TOK_TILE + ne * 2 * (SUBLANE - 1) + WIN


def _combine_run_rows(ne):
    return -(-(TOP_K * TOK_TILE + ne * (SUBLANE - 1 + WIN - 1)) // WIN) * WIN


DISPATCH_BASE_ROWS = TOP_K * TOK_TILE + 256
COMBINE_BASE_ROWS = TOP_K * TOK_TILE + 768
COMBINE_STEP_ROWS = 256
GATE_PIECES = 3
HI_MASK = 0xFFFF0000
BF16_EXACT = 256.0
MIB = 1024 * 1024
VMEM_MIB = {"small_front": 56, "front": 52, "ssm": 56, "mix_prompt": 56, "mix_sample": 32,
            "dispatch": 40, "moe": 58, "combine": 48}


def _rms(x, g):
    return x * lax.rsqrt(jnp.mean(x * x, axis=-1, keepdims=True) + EPS) * g


def _gelu_tanh(x):
    c = math.sqrt(2.0 / math.pi)
    return 0.5 * x * (1.0 + jnp.tanh(c * (x + 0.044715 * (x * x * x))))


def _params(name, sem):
    return dict(name=name, compiler_params=pltpu.CompilerParams(
        dimension_semantics=sem, vmem_limit_bytes=VMEM_MIB[name] * MIB))


def _pack_pairs(a, b):
    return (pltpu.bitcast(a, U32) >> 16) | (pltpu.bitcast(b, U32) & jnp.uint32(HI_MASK))


def _unpack_pairs(w):
    lo = pltpu.bitcast(w << 16, F32)
    hi = pltpu.bitcast(w & jnp.uint32(HI_MASK), F32)
    return jnp.concatenate([lo, hi], axis=-1).astype(BF16)


def _iota2(shape, axis):
    return lax.broadcasted_iota(I32, shape, axis)


def _expand_cols(compact, reps_log2, n_log2):
    q = _iota2((compact.shape[1], 2 << (reps_log2 + n_log2)), 0)
    c = _iota2((compact.shape[1], 2 << (reps_log2 + n_log2)), 1)
    nmask = (1 << n_log2) - 1
    same = jnp.logical_and((q >> n_log2) == (c >> (reps_log2 + n_log2)), (q & nmask) == (c & nmask))
    return jnp.dot(compact, jnp.where(same, 1.0, 0.0).astype(BF16), preferred_element_type=F32)


def _expand_rows(compact, reps_log2, n_log2):
    r = _iota2((2 << (reps_log2 + n_log2), compact.shape[0]), 0)
    q = _iota2((2 << (reps_log2 + n_log2), compact.shape[0]), 1)
    nmask = (1 << n_log2) - 1
    same = jnp.logical_and((r >> (reps_log2 + n_log2)) == (q >> n_log2), (r & nmask) == (q & nmask))
    return jnp.dot(jnp.where(same, 1.0, 0.0).astype(BF16), compact, preferred_element_type=F32)


def _group_mask(shape, row_shift, col_shift, ngroups):
    r = _iota2(shape, 0)
    c = _iota2(shape, 1)
    return ((r >> row_shift) & (ngroups - 1)) == ((c >> col_shift) & (ngroups - 1))


def _small_front_kernel(x_ref, nmix_ref, win_ref, s0r_ref, s0i_ref, b0_ref, b1_ref, cw_ref,
                        bc_ref, cc_ref, abr_ref, abi_ref, nconv_ref,
                        u_ref, z_ref, y_ref, ycn_ref, sr_ref, si_ref, bdb_ref, cm_ref):
    ns, nst = s0r_ref.shape
    cw = u_ref.shape[1]
    n = bc_ref.shape[1] // 2
    nlog = n.bit_length() - 1
    glog = (nst // n).bit_length() - 1
    hlog = (cw >> glog).bit_length() - 1
    bdb_ref[...] = jnp.where(_group_mask(bdb_ref.shape, hlog, nlog, 1 << glog),
                             _expand_cols(bc_ref[...], glog, nlog), 0.0).astype(BF16)
    cm_ref[...] = jnp.where(_group_mask(cm_ref.shape, nlog, hlog, 1 << glog),
                            _expand_rows(cc_ref[...], glog, nlog), 0.0).astype(BF16)
    h = _rms(x_ref[...], nmix_ref[...]).astype(BF16)
    proj = jnp.dot(h, win_ref[...], preferred_element_type=F32)
    u = proj[:, 0:cw]
    zc = proj[:, cw:2 * cw]
    gb = proj[:, 2 * cw:3 * cw]
    gc = proj[:, 3 * cw:4 * cw]
    z = gc * zc
    u_ref[...] = u
    z_ref[...] = z
    bu = jnp.dot(u[:ns].astype(BF16), bdb_ref[...], preferred_element_type=F32)
    abr = abr_ref[...]
    abi = abi_ref[...]
    s0r = s0r_ref[...]
    s0i = s0i_ref[...]
    sr = abr * s0r - abi * s0i + bu[:, :nst]
    si = abr * s0i + abi * s0r + bu[:, nst:]
    sr_ref[...] = sr
    si_ref[...] = si
    scat = jnp.concatenate([sr, si], axis=-1).astype(BF16)
    y_ref[...] = jnp.dot(scat, cm_ref[...], preferred_element_type=F32)
    conv = cw_ref[0:1, :] * b0_ref[...] + cw_ref[1:2, :] * b1_ref[...] + cw_ref[2:3, :] * z[:ns]
    ycn_ref[...] = _rms(gb[:ns] * conv, nconv_ref[...]).astype(BF16)


def _front_kernel(x_ref, nmix_ref, win_ref, zm_ref, cw_ref, nconv_ref,
                  uc_ref, ut_ref, ycn_ref, zt_ref, zbuf, ubuf):
    i = pl.program_id(1)
    nb, tt, d = x_ref.shape
    cw = ycn_ref.shape[2]
    rows = nb * tt
    ncz = tt // CHUNK
    halo = SUBLANE

    @pl.when(i == 0)
    def _():
        zbuf[:, 0:halo, :] = jnp.broadcast_to(zm_ref[...][None], (nb, halo, cw))

    h = _rms(x_ref[...].reshape(rows, d), nmix_ref[...]).astype(BF16)
    u = jnp.dot(h, win_ref[:, 0:cw], preferred_element_type=F32)
    ut_ref[...] = u.astype(BF16).reshape(nb, tt, cw)
    for j in range(cw // LANE):
        ubuf[j] = u[:, j * LANE:(j + 1) * LANE].reshape(nb, tt, LANE)
    for s in range(CHUNK):
        for j in range(cw // LANE):
            piece = ubuf[j, :, pl.ds(s, ncz, stride=CHUNK), :]
            uc_ref[j, :, :, s * LANE:(s + 1) * LANE] = piece.astype(BF16)
    zc = jnp.dot(h, win_ref[:, cw:2 * cw], preferred_element_type=F32)
    gc = jnp.dot(h, win_ref[:, 3 * cw:4 * cw], preferred_element_type=F32)
    z3 = (gc * zc).reshape(nb, tt, cw)
    zbuf[:, halo:halo + tt, :] = z3
    z1 = zbuf[:, halo - 1:halo - 1 + tt, :]
    z2 = zbuf[:, halo - 2:halo - 2 + tt, :]
    conv = cw_ref[0:1, :] * z2 + cw_ref[1:2, :] * z1 + cw_ref[2:3, :] * z3
    gb = jnp.dot(h, win_ref[:, 2 * cw:3 * cw], preferred_element_type=F32)
    yc = gb * conv.reshape(rows, cw)
    ycn_ref[...] = _rms(yc, nconv_ref[...]).astype(BF16).reshape(nb, tt, cw)
    tail = zbuf[:, tt:tt + halo, :]
    zt_ref[...] = tail
    zbuf[:, 0:halo, :] = tail


def _ssm_kernel(u_ref, um_ref, kc_ref, pc_ref, rc_ref, a16_ref, y_ref, sl_ref,
                s_carry, ds_ref, sp_ref, p_s, r_s, t_s):
    th = pl.program_id(1)
    _, nb, cc, w = u_ref.shape
    nst = p_s.shape[1]
    half = nst // 2
    rows = nb * cc
    blk = 2 * LANE
    u = u_ref[0].reshape(rows, w)

    @pl.when(th == 0)
    def _():
        hch = kc_ref.shape[2]
        gpt = LANE // hch
        hlog = hch.bit_length() - 1
        glog = gpt.bit_length() - 1
        nlog = (pc_ref.shape[2] // 2).bit_length() - 1
        p_s[...] = jnp.where(_group_mask(p_s.shape, hlog, nlog, gpt),
                             _expand_cols(pc_ref[0], glog, nlog), 0.0).astype(BF16)
        r_s[...] = jnp.where(_group_mask(r_s.shape, nlog, hlog, gpt),
                             _expand_rows(rc_ref[0], glog, nlog), 0.0).astype(BF16)
        nlag = kc_ref.shape[1] // LANE
        o = _iota2((hch, LANE), 0)
        c = _iota2((hch, LANE), 1)
        spread = jnp.where((c & (hch - 1)) == o, 1.0, 0.0).astype(BF16)
        lagm = jnp.dot(kc_ref[0], spread, preferred_element_type=F32)
        r = _iota2(lagm.shape, 0)
        c = _iota2(lagm.shape, 1)
        lagm = jnp.where(((r >> hlog) & (gpt - 1)) == (c >> hlog), lagm, 0.0).astype(BF16)
        for dlt in range(nlag // 2):
            b0 = lagm[(2 * dlt) * LANE:(2 * dlt + 1) * LANE]
            b1 = lagm[(2 * dlt + 1) * LANE:(2 * dlt + 2) * LANE]
            b2 = lagm[(2 * dlt + 2) * LANE:(2 * dlt + 3) * LANE]
            t_s[dlt, 0:LANE, 0:LANE] = b1
            t_s[dlt, 0:LANE, LANE:blk] = b2
            t_s[dlt, LANE:blk, 0:LANE] = b0
            t_s[dlt, LANE:blk, LANE:blk] = b1
        ds_ref[:, 0:8, :] = jnp.dot(um_ref[0], p_s[...], preferred_element_type=F32).reshape(nb, 8, nst)
        s_carry[...] = ds_ref[:, 0:1, :]

    ds_ref[...] = jnp.dot(u, p_s[...], preferred_element_type=F32).reshape(nb, cc, nst)
    for tb in range(w // blk):
        acc = jnp.dot(u[:, 0:blk], t_s[tb], preferred_element_type=F32)
        for sb in range(1, tb + 1):
            acc = acc + jnp.dot(u[:, sb * blk:(sb + 1) * blk], t_s[tb - sb],
                                preferred_element_type=F32)
        y_ref[0, :, :, tb * blk:(tb + 1) * blk] = acc.reshape(nb, cc, blk)
    ar = a16_ref[0, 0:1, :].reshape(1, 1, half)
    ai = a16_ref[0, 1:2, :].reshape(1, 1, half)
    sr = s_carry[:, :, 0:half]
    si = s_carry[:, :, half:nst]
    for c in range(cc):
        sp_ref[:, c:c + 1, 0:half] = sr
        sp_ref[:, c:c + 1, half:nst] = si
        dr = ds_ref[:, c:c + 1, 0:half]
        di = ds_ref[:, c:c + 1, half:nst]
        sr, si = ar * sr - ai * si + dr, ar * si + ai * sr + di
    s_carry[:, :, 0:half] = sr
    s_carry[:, :, half:nst] = si
    sl_ref[0, :, :, 0:half] = sr
    sl_ref[0, :, :, half:nst] = si

    sp = sp_ref[...].reshape(rows, nst).astype(BF16)
    for tb in range(w // blk):
        acc = jnp.dot(sp, r_s[:, tb * blk:(tb + 1) * blk], preferred_element_type=F32)
        y_ref[0, :, :, tb * blk:(tb + 1) * blk] += acc.reshape(nb, cc, blk)


N_MIX_IN = 12


def _mix_rows(x, yssm, ut, ycn, dsk_ref, wglu_ref, bglu_ref, nssm_ref, wout_ref,
              nffn_ref, wr_ref, br_ref):
    ne = br_ref.shape[0]
    y = _gelu_tanh(yssm + dsk_ref[...] * ut.astype(F32))
    glu = jnp.dot(y.astype(BF16), wglu_ref[...], preferred_element_type=F32) + bglu_ref[...]
    o = y * jax.nn.sigmoid(glu)
    ysn = _rms(o, nssm_ref[...]).astype(BF16)
    mix = jnp.concatenate([ysn, ycn], axis=-1)
    x1 = x + jnp.dot(mix, wout_ref[...], preferred_element_type=F32)
    hf = _rms(x1, nffn_ref[...])
    hf_hi = hf.astype(BF16)
    hf_lo = (hf - hf_hi.astype(F32)).astype(BF16)
    r = hf.shape[0]
    part = jnp.dot(jnp.concatenate([hf_hi, hf_lo], axis=0), wr_ref[...], preferred_element_type=F32)
    logits = (part[0:r, 0:LANE] + part[0:r, LANE:2 * LANE]) + (part[r:2 * r, 0:LANE] + part[r:2 * r, LANE:2 * LANE])
    lt = logits.T[0:ne, :] + br_ref[...]
    iota = lax.broadcasted_iota(I32, lt.shape, 0)
    vals, idxs = [], []
    sel = jnp.zeros(lt.shape, F32)
    for _ in range(TOP_K):
        m = jnp.max(lt, axis=0, keepdims=True)
        ik = jnp.min(jnp.where(lt == m, iota, ne), axis=0, keepdims=True)
        vals.append(m)
        idxs.append(ik)
        hit = iota == ik
        sel = sel + jnp.where(hit, 1.0, 0.0)
        lt = jnp.where(hit, -jnp.inf, lt)
    es = [jnp.exp(v - vals[0]) for v in vals]
    tot = es[0] + es[1] + es[2] + es[3]
    idx = jnp.concatenate(idxs, axis=0)
    gates = jnp.concatenate([e / tot for e in es], axis=0)
    return x1, hf_hi, idx, gates, sel


def _mix_kernel_prompt(*refs):
    x_ref, yc_ref = refs[0], refs[1]
    x1_ref, hf_ref, idx_ref, gate_ref, cnt_ref, ybuf = refs[N_MIX_IN:]
    nj, nb, ncz, _ = yc_ref.shape
    for s in range(CHUNK):
        for j in range(nj):
            ybuf[j, :, pl.ds(s, ncz, stride=CHUNK), :] = yc_ref[j, :, :, s * LANE:(s + 1) * LANE]
    ut_ref, ycn_ref = refs[2], refs[3]
    tt, d = x_ref.shape[1], x_ref.shape[2]
    cw = nj * LANE
    nbc = TOK_TILE // tt
    for t in range(cnt_ref.shape[0]):
        b0, r0 = t * nbc, t * TOK_TILE
        yssm = jnp.concatenate([ybuf[j, b0:b0 + nbc].reshape(TOK_TILE, LANE) for j in range(nj)], axis=-1)
        x1, hf, idx, gates, sel = _mix_rows(
            x_ref[b0:b0 + nbc].reshape(TOK_TILE, d), yssm, ut_ref[b0:b0 + nbc].reshape(TOK_TILE, cw),
            ycn_ref[b0:b0 + nbc].reshape(TOK_TILE, cw), *refs[4:N_MIX_IN])
        x1_ref[r0:r0 + TOK_TILE, :] = x1
        hf_ref[r0:r0 + TOK_TILE, :] = hf
        idx_ref[:, r0:r0 + TOK_TILE] = idx
        gate_ref[:, r0:r0 + TOK_TILE] = gates
        cnt_ref[t] = jnp.sum(sel, axis=1, keepdims=True).astype(I32)


def _mix_kernel_sample(*refs):
    x1, hf, idx, gates, sel = _mix_rows(refs[0][0], refs[1][...], refs[2][0], refs[3][0],
                                        *refs[4:N_MIX_IN])
    x1_ref, hf_ref, idx_ref, gate_ref, cnt_ref = refs[N_MIX_IN + 5:]
    ns = x1.shape[0]
    x1_ref[...] = jnp.zeros(x1_ref.shape, x1_ref.dtype)
    hf_ref[...] = jnp.zeros(hf_ref.shape, hf_ref.dtype)
    idx_ref[...] = jnp.zeros(idx_ref.shape, idx_ref.dtype)
    gate_ref[...] = jnp.zeros(gate_ref.shape, gate_ref.dtype)
    x1_ref[0:ns, :] = x1
    hf_ref[0:ns, :] = hf
    idx_ref[:, 0:ns] = idx
    gate_ref[:, 0:ns] = gates
    cnt_ref[0] = jnp.sum(sel, axis=1, keepdims=True).astype(I32)


class _Table:
    def __init__(self, ref, offset):
        self.ref, self.offset = ref, offset

    def __getitem__(self, k):
        return self.ref[self.offset + k]


def _pack_tables(tables):
    offsets, total = [], 0
    for t in tables:
        offsets.append(total)
        total += t.shape[0]
    return jnp.concatenate(tables), tuple(offsets)


def _split_bf16(x, parts):
    out = []
    for _ in range(parts - 1):
        p = x.astype(BF16)
        out.append(p)
        x = x - p.astype(F32)
    out.append(x.astype(BF16))
    return out


def _run_tables(idx_ref, loff_ref, tile, n_valid, gate_ref=None):
    ne = loff_ref.shape[1]
    tt = idx_ref.shape[1]
    e_iota = lax.broadcasted_iota(I32, (ne, tt), 0)
    tok = tile * tt + lax.broadcasted_iota(I32, (1, tt), 1)
    valid = tok < n_valid
    hits = [jnp.logical_and(e_iota == idx_ref[k:k + 1, :], valid) for k in range(TOP_K)]
    sel = jnp.zeros((ne, tt), F32)
    for h in hits:
        sel = sel + jnp.where(h, 1.0, 0.0)
    before = lax.broadcasted_iota(I32, (tt, tt), 0) < lax.broadcasted_iota(I32, (tt, tt), 1)
    tri = jnp.where(before, 1.0, 0.0).astype(BF16)
    base = jnp.dot(sel.astype(BF16), tri, preferred_element_type=F32) + loff_ref[0].astype(F32)
    base = jnp.where(sel > 0.0, base + 1.0, 0.0)
    b_hi = BF16_EXACT * jnp.floor(base * (1.0 / BF16_EXACT))
    b_lo = base - b_hi
    halves = jnp.concatenate([b_hi, b_lo], axis=0).astype(BF16)
    if gate_ref is None:
        return halves, None
    gate_e = jnp.zeros((ne, tt), F32)
    for k, h in enumerate(hits):
        gate_e = gate_e + jnp.where(h, gate_ref[k:k + 1, :], 0.0)
    return halves, jnp.concatenate(_split_bf16(gate_e, GATE_PIECES), axis=0)


def _rows_onehot(halves, rgn_ref, row0, nrows, pieces=None):
    ne = rgn_ref.shape[2]
    tt = halves.shape[1]

    def own(copies):
        lo = jnp.concatenate([rgn_ref[0, 0:1, :]] * copies, axis=1)
        hi = jnp.concatenate([rgn_ref[0, 1:2, :]] * copies, axis=1)
        r_i = row0 + lax.broadcasted_iota(I32, (nrows, copies * ne), 0)
        return jnp.logical_and(r_i >= lo, r_i < hi)

    want = jnp.dot(jnp.where(own(2), 1.0, 0.0).astype(BF16), halves, preferred_element_type=F32)
    r_f = (row0 + 1 + lax.broadcasted_iota(I32, (nrows, tt), 0)).astype(F32)
    smat = jnp.where(want == r_f, 1.0, 0.0).astype(BF16)
    if pieces is None:
        return smat
    per = lax.dot_general(smat, pieces, (((1,), (1,)), ((), ())), preferred_element_type=F32)
    gcol = jnp.sum(jnp.where(own(GATE_PIECES), per, 0.0), axis=1, keepdims=True)
    return smat, gcol


def _window_copy(buf, slot, hbm, lo, g, sem, to_hbm, rows, align):
    src = buf.at[slot, pl.ds(pl.multiple_of(lo, align), rows)]
    dst = hbm.at[pl.ds(pl.multiple_of(g, align), rows)]
    if to_hbm:
        return pltpu.make_async_copy(src, dst, sem.at[slot])
    return pltpu.make_async_copy(dst, src, sem.at[slot])


def _start_windows(wg_ref, wl_ref, tw_ref, tile, slots, buf, slot, hbm, sem, to_hbm, rows, align):
    def per_window(w, c):
        k = tile * slots + w
        lo = w * rows if wl_ref is None else wl_ref[k]
        _window_copy(buf, slot, hbm, lo, wg_ref[k], sem, to_hbm, rows, align).start()
        return c

    lax.fori_loop(0, tw_ref[tile], per_window, 0)


def _wait_windows(count, buf, slot, hbm, sem, to_hbm, rows, align):
    batch = 8

    def many(w, c):
        _window_copy(buf, slot, hbm, 0, 0, sem, to_hbm, batch * rows, align).wait()
        return c

    def one(w, c):
        _window_copy(buf, slot, hbm, 0, 0, sem, to_hbm, rows, align).wait()
        return c

    lax.fori_loop(0, count // batch, many, 0)
    lax.fori_loop(0, count % batch, one, 0)


def _dispatch_kernel(offsets, tab_ref,
                     hf_ref, idx_ref, gate_ref, loff_ref, rgn_ref, xs_ref, buf, carry, sem):
    wg_ref, wl_ref, mlo_ref, mtg_ref, keep_ref, tw_ref, used_ref, nv_ref = [
        _Table(tab_ref, o) for o in offsets]
    i = pl.program_id(0)
    nt = pl.num_programs(0)
    ne = loff_ref.shape[1]
    tt, d = hf_ref.shape
    nrun = buf.shape[1]
    slot = i % 2

    @pl.when(i == 0)
    def _():
        carry[...] = jnp.zeros(carry.shape, carry.dtype)
        buf[...] = jnp.zeros(buf.shape, buf.dtype)

    halves, pieces = _run_tables(idx_ref, loff_ref, i, nv_ref[0], gate_ref)

    def emit(row0, nrows):
        smat, gcol = _rows_onehot(halves, rgn_ref, row0, nrows, pieces)
        xr = jnp.dot(smat, hf_ref[...], preferred_element_type=F32)
        lane0 = lax.broadcasted_iota(I32, (nrows, GATE_COLS), 1) == 0
        buf[slot, row0:row0 + nrows, 0:d // 2] = _pack_pairs(xr[:, :d // 2], xr[:, d // 2:])
        buf[slot, row0:row0 + nrows, d // 2:] = pltpu.bitcast(jnp.where(lane0, gcol, 0.0), U32)

    emit(0, DISPATCH_BASE_ROWS)
    past_base = used_ref[i] > DISPATCH_BASE_ROWS

    @pl.when(past_base)
    def _():
        emit(DISPATCH_BASE_ROWS, nrun - DISPATCH_BASE_ROWS)

    @pl.when(jnp.logical_not(past_base))
    def _():
        buf[slot, nrun - 2 * SUBLANE:nrun, :] = jnp.zeros((2 * SUBLANE, buf.shape[2]), buf.dtype)

    def merge(e, c):
        k = i * ne + e
        lo = pl.multiple_of(mlo_ref[k], SUBLANE)
        buf[slot, pl.ds(lo, SUBLANE), :] = buf[slot, pl.ds(lo, SUBLANE), :] | carry[e]
        tg = pl.multiple_of(mtg_ref[k], SUBLANE)
        carry[e] = jnp.where(keep_ref[k] > 0, carry[e], buf[slot, pl.ds(tg, SUBLANE), :])
        return c

    lax.fori_loop(0, ne, merge, 0, unroll=4)

    @pl.when(i > 0)
    def _():
        _wait_windows(tw_ref[i - 1], buf, 1 - slot, xs_ref, sem, True, WIN, SUBLANE)

    _start_windows(wg_ref, wl_ref, tw_ref, i, MAX_WINDOWS, buf, slot, xs_ref, sem, True, WIN, SUBLANE)

    @pl.when(i == nt - 1)
    def _():
        _wait_windows(tw_ref[i], buf, slot, xs_ref, sem, True, WIN, SUBLANE)


def _expert_weight_copies(wg_hbm, wd_hbm, wg_f32, wd_f32, sem, e, slot):
    return (pltpu.make_async_copy(wg_hbm.at[e], wg_f32.at[slot], sem.at[0, slot]),
            pltpu.make_async_copy(wd_hbm.at[e], wd_f32.at[slot], sem.at[1, slot]))


def _moe_kernel(offsets, tab_ref,
                x_hbm, bg_ref, bd_ref, wg_hbm, wd_hbm, y_ref,
                wg_f32, wd_f32, wg_bf, wd_bf, sem, xbuf, xsem):
    be_ref, bv_ref, first_ref, nxt_ref, slot_ref = [_Table(tab_ref, o) for o in offsets]
    i = pl.program_id(0)
    nblk = pl.num_programs(0)
    e = be_ref[i]
    dff = wd_bf.shape[0]
    nslots, bm = xbuf.shape[0], xbuf.shape[1]
    sub = MOE_SUB
    nw = xbuf.shape[2] - GATE_COLS

    def x_copy(j):
        return pltpu.make_async_copy(x_hbm.at[pl.ds(pl.multiple_of(j * bm, bm), bm)],
                                     xbuf.at[j % nslots], xsem.at[j % nslots])

    def x_start(j):
        @pl.when(jnp.logical_and(j < nblk, bv_ref[jnp.minimum(j, nblk - 1)] > 0))
        def _():
            x_copy(j).start()

    @pl.when(i == 0)
    def _():
        for j in range(nslots - 1):
            x_start(i + j)

    x_start(i + nslots - 1)

    @pl.when(bv_ref[i] > 0)
    def _():
        x_copy(i).wait()

    x_ref = xbuf.at[i % nslots]

    @pl.when(first_ref[i] > 0)
    def _():
        slot = slot_ref[i]

        @pl.when(first_ref[i] > 1)
        def _():
            for cp in _expert_weight_copies(wg_hbm, wd_hbm, wg_f32, wd_f32, sem, e, slot):
                cp.start(priority=WEIGHT_DMA_PRIORITY)

        for cp in _expert_weight_copies(wg_hbm, wd_hbm, wg_f32, wd_f32, sem, e, slot):
            cp.wait()

        @pl.when(nxt_ref[i] >= 0)
        def _():
            for cp in _expert_weight_copies(wg_hbm, wd_hbm, wg_f32, wd_f32, sem, nxt_ref[i], 1 - slot):
                cp.start(priority=WEIGHT_DMA_PRIORITY)

        wg_bf[...] = wg_f32[slot].astype(BF16)
        wd_bf[...] = wd_f32[slot].astype(BF16)

    def rows(r0, nrows):
        sizes = [MOE_CHAIN] * (nrows // MOE_CHAIN) + ([nrows % MOE_CHAIN] if nrows % MOE_CHAIN else [])
        lo = r0
        for size in sizes:
            _chain(lo, size)
            lo += size

    def _chain(lo, n):
        live = lax.broadcasted_iota(I32, (n, 1), 0) + lo < bv_ref[i]
        x = jnp.where(live, _unpack_pairs(x_ref[lo:lo + n, 0:nw]), jnp.zeros((), BF16))
        route = jnp.where(live, pltpu.bitcast(x_ref[lo:lo + n, nw:], F32)[:, 0:1], 0.0)
        gu = jnp.dot(x, wg_bf[...], preferred_element_type=F32) + bg_ref[0]
        gate = jnp.minimum(gu[:, :dff], SWIGLU_LIMIT)
        up = jnp.clip(gu[:, dff:], -SWIGLU_LIMIT, SWIGLU_LIMIT)
        h = gate * jax.nn.sigmoid(SWIGLU_ALPHA * gate) * (up + 1.0)
        y = jnp.dot(h.astype(BF16), wd_bf[...], preferred_element_type=F32) + bd_ref[0]
        yr = (route * y).astype(BF16).astype(F32)
        half = yr.shape[1] // 2
        y_ref[lo:lo + n, :] = _pack_pairs(yr[:, :half], yr[:, half:])

    nchains = bm // sub
    for live_chains in range(nchains + 1):
        lo_rows, hi_rows = (live_chains - 1) * sub, live_chains * sub

        @pl.when(jnp.logical_and(bv_ref[i] > lo_rows, bv_ref[i] <= hi_rows) if live_chains
                 else bv_ref[i] <= 0)
        def _(used=hi_rows):
            if used:
                rows(0, used)
            if used < bm:
                y_ref[used:bm, :] = jnp.zeros((bm - used, y_ref.shape[1]), y_ref.dtype)


def _combine_tile(step, nt):
    return (step + nt - 1) % nt


def _combine_kernel(offsets, tab_ref,
                    x1_ref, idx_ref, loff_ref, rgn_ref, nf_ref, yb_ref,
                    yp_ref, ys_ref, buf, acc, sem):
    wg_ref, tw_ref, nv_ref = [_Table(tab_ref, o) for o in offsets]
    s = pl.program_id(0)
    nt = pl.num_programs(0)
    tt, d = x1_ref.shape
    nrun = buf.shape[1]
    tile = _combine_tile(s, nt)
    slot = s % 2

    @pl.when(s == 0)
    def _():
        buf[...] = jnp.zeros(buf.shape, buf.dtype)
        _start_windows(wg_ref, None, tw_ref, tile, MAX_WINDOWS, buf, slot, yb_ref, sem, False,
                       WIN, SUBLANE)

    @pl.when(s + 1 < nt)
    def _():
        _start_windows(wg_ref, None, tw_ref, _combine_tile(s + 1, nt), MAX_WINDOWS, buf, 1 - slot,
                       yb_ref, sem, False, WIN, SUBLANE)

    halves, _ = _run_tables(idx_ref, loff_ref, tile, nv_ref[0])

    def gathered(row0, nrows):
        smat = _rows_onehot(halves, rgn_ref, row0, nrows)
        yrun = _unpack_pairs(buf[slot, row0:row0 + nrows])
        return lax.dot_general(smat, yrun, (((0,), (0,)), ((), ())), preferred_element_type=F32)

    _wait_windows(tw_ref[tile], buf, slot, yb_ref, sem, False, WIN, SUBLANE)
    acc[...] = gathered(0, COMBINE_BASE_ROWS)
    for row0 in range(COMBINE_BASE_ROWS, nrun, COMBINE_STEP_ROWS):
        @pl.when(WIN * tw_ref[tile] > row0)
        def _(row0=row0):
            acc[...] += gathered(row0, min(COMBINE_STEP_ROWS, nrun - row0))

    out = _rms(x1_ref[...] + acc[...], nf_ref[...])

    @pl.when(s == 0)
    def _():
        ys_ref[...] = out[0:ys_ref.shape[0], :]

    @pl.when(s > 0)
    def _():
        yp_ref[...] = out.reshape(yp_ref.shape)


def _ssm_matrices(a_re, a_im, log_dt, b_re, b_im, c_re, c_im):
    g, n = a_re.shape
    hch = b_re.shape[2]
    gpt = LANE // hch
    nj = g // gpt
    a = lax.complex(a_re, a_im)
    dta = a * jnp.exp(log_dt)[:, None]
    a_bar = jnp.exp(dta)
    bb = ((a_bar - 1.0) / a)[:, :, None] * lax.complex(b_re, b_im)
    cc = lax.complex(c_re, c_im)
    ks = jnp.arange(CHUNK + 1, dtype=F32)
    pw = jnp.exp(dta[None] * ks[:, None, None])
    kk = jnp.real(jnp.einsum('gon,kgn,gni->kgio', cc, pw[:CHUNK], bb))
    kk = jnp.concatenate([jnp.zeros_like(kk[:1]), kk], axis=0)
    kc = kk.reshape(CHUNK + 1, nj, gpt * hch, hch).transpose(1, 0, 2, 3)
    kc = kc.reshape(nj, (CHUNK + 1) * LANE, hch).astype(BF16)
    pwr, pwi = jnp.real(pw), jnp.imag(pw)
    bbr = jnp.real(bb).transpose(0, 2, 1).reshape(1, g * hch, n)
    bbi = jnp.imag(bb).transpose(0, 2, 1).reshape(1, g * hch, n)
    par = jnp.repeat(pwr[CHUNK - 1::-1][:CHUNK], hch, axis=1)
    pai = jnp.repeat(pwi[CHUNK - 1::-1][:CHUNK], hch, axis=1)
    pc = jnp.concatenate([par * bbr - pai * bbi, par * bbi + pai * bbr], axis=-1)
    pc = pc.reshape(CHUNK, nj, gpt * hch, 2 * n).transpose(1, 0, 2, 3)
    pc = pc.reshape(nj, CHUNK * LANE, 2 * n).astype(BF16)
    ccr = jnp.real(cc).transpose(2, 0, 1).reshape(n, 1, g * hch)
    cci = jnp.imag(cc).transpose(2, 0, 1).reshape(n, 1, g * hch)
    qar = jnp.repeat(pwr[1:CHUNK + 1].transpose(2, 0, 1), hch, axis=2)
    qai = jnp.repeat(pwi[1:CHUNK + 1].transpose(2, 0, 1), hch, axis=2)
    rc = jnp.stack([ccr * qar - cci * qai, -(ccr * qai + cci * qar)], axis=0)
    rc = rc.reshape(2 * n, CHUNK, nj, gpt * hch).transpose(2, 0, 1, 3)
    rc = rc.reshape(nj, 2 * n, CHUNK * LANE).astype(BF16)
    a16 = pw[CHUNK].reshape(nj, 1, gpt * n)
    a16 = jnp.concatenate([jnp.real(a16), jnp.imag(a16)], axis=1)
    bc = jnp.stack([jnp.real(bb), jnp.imag(bb)], axis=0).transpose(1, 3, 0, 2)
    bc = bc.reshape(g * hch, 2 * n).astype(BF16)
    c2 = jnp.stack([jnp.real(cc), -jnp.imag(cc)], axis=0).transpose(0, 3, 1, 2)
    c2 = c2.reshape(2 * n, g * hch).astype(BF16)
    abr = jnp.real(a_bar).reshape(1, g * n)
    abi = jnp.imag(a_bar).reshape(1, g * n)
    return kc, pc, rc, a16, bc, c2, abr, abi


def _full(shape):
    return pl.BlockSpec(shape, lambda *_: (0,) * len(shape))


def kernel(x_prompt, x_sample, state_ssm_re, state_ssm_im, state_conv, meta_tokens, norm_mix, w_in,
           ssm_a_re, ssm_a_im, ssm_log_dt, ssm_b_re, ssm_b_im, ssm_c_re, ssm_c_im, ssm_d, w_glu, b_glu,
           conv_w, norm_out_ssm, norm_out_conv, w_out, norm_ffn, w_router, b_router, w_gate_up,
           b_gate_up, w_down, b_down, norm_final):
    nb, seq, d = x_prompt.shape
    ns = x_sample.shape[0]
    depth, _, g, n = state_ssm_re.shape
    assert depth == 1 and x_sample.shape[1] == 1 and meta_tokens.shape[0] == CHUNK
    cw = conv_w.shape[2]
    nj = cw // LANE
    ne = w_router.shape[2]
    dff = w_down.shape[2]
    nst = g * n
    nbt = nb // 2
    tt = 256
    n_chunks = seq // CHUNK
    tp = nb * seq
    tall = tp + ns
    tm = 128
    nbm = nb
    rows_p = nbm * tm
    assert rows_p % TOK_TILE == 0 and TOK_TILE % tm == 0 and ns <= TOK_TILE and d % 2 == 0
    n_tiles = tp // TOK_TILE + 1
    ta = n_tiles * TOK_TILE

    kc, pc, rc, a16, bc, c2, abr, abi = _ssm_matrices(
        ssm_a_re[0], ssm_a_im[0], ssm_log_dt[0], ssm_b_re[0], ssm_b_im[0], ssm_c_re[0], ssm_c_im[0])
    win_bf = w_in[0].astype(BF16)
    nmix = norm_mix[0].reshape(1, d)
    nconv = norm_out_conv[0].reshape(1, cw)
    cwt = conv_w[0]

    xsm = jnp.concatenate([x_sample.reshape(ns, d), meta_tokens], axis=0)
    nsm = ns + CHUNK
    s0r = state_ssm_re[0].reshape(ns, nst)
    s0i = state_ssm_im[0].reshape(ns, nst)
    buf0 = state_conv[0, :, 0, :]
    buf1 = state_conv[0, :, 1, :]
    u_sm, z_sm, y_s, ycn_s, sr_s, si_s = pl.pallas_call(
        _small_front_kernel,
        out_shape=(jax.ShapeDtypeStruct((nsm, cw), F32), jax.ShapeDtypeStruct((nsm, cw), F32),
                   jax.ShapeDtypeStruct((ns, cw), F32), jax.ShapeDtypeStruct((ns, cw), BF16),
                   jax.ShapeDtypeStruct((ns, nst), F32), jax.ShapeDtypeStruct((ns, nst), F32)),
        scratch_shapes=[pltpu.VMEM((cw, 2 * nst), BF16), pltpu.VMEM((2 * nst, cw), BF16)],
        **_params("small_front", None),
    )(xsm, nmix, win_bf, s0r, s0i, buf0, buf1, cwt, bc, c2, abr, abi, nconv)
    u_meta = u_sm[ns:]
    z_meta8 = z_sm[ns + CHUNK - 8:]
    new_conv_s = jnp.stack([buf1, z_sm[:ns]], axis=1)[None]
    new_re_s = sr_s.reshape(1, ns, g, n)
    new_im_s = si_s.reshape(1, ns, g, n)

    wch = CHUNK * LANE
    u4c, u_tok, ycn_p, ztail = pl.pallas_call(
        _front_kernel,
        grid=(nb // nbt, seq // tt),
        in_specs=[pl.BlockSpec((nbt, tt, d), lambda b, i: (b, i, 0)),
                  _full((1, d)), _full((d, 4 * cw)), _full((8, cw)), _full((3, cw)), _full((1, cw))],
        out_specs=(pl.BlockSpec((nj, nbt, tt // CHUNK, wch), lambda b, i: (0, b, i, 0)),
                   pl.BlockSpec((nbt, tt, cw), lambda b, i: (b, i, 0)),
                   pl.BlockSpec((nbt, tt, cw), lambda b, i: (b, i, 0)),
                   pl.BlockSpec((nbt, 8, cw), lambda b, i: (b, 0, 0))),
        out_shape=(jax.ShapeDtypeStruct((nj, nb, n_chunks, wch), BF16),
                   jax.ShapeDtypeStruct((nb, seq, cw), BF16),
                   jax.ShapeDtypeStruct((nb, seq, cw), BF16),
                   jax.ShapeDtypeStruct((nb, 8, cw), F32)),
        scratch_shapes=[pltpu.VMEM((nbt, tt + 8, cw), F32), pltpu.VMEM((nj, nbt, tt, LANE), F32)],
        **_params("front", ("arbitrary", "arbitrary")),
    )(x_prompt, nmix, win_bf, z_meta8, cwt, nconv)
    new_conv_p = ztail[:, 6:8, :][None]

    cc = n_chunks // 2
    um = u_meta.reshape(CHUNK, nj, LANE).transpose(1, 0, 2).reshape(nj, 1, wch)
    um = jnp.broadcast_to(um, (nj, 8 * nb, wch)).astype(BF16)
    gpt = g // nj
    nstj = 2 * gpt * n
    hch = cw // g
    y4c, s_last = pl.pallas_call(
        _ssm_kernel,
        grid=(nj, n_chunks // cc),
        in_specs=[pl.BlockSpec((1, nb, cc, wch), lambda j, t: (j, 0, t, 0)),
                  pl.BlockSpec((1, 8 * nb, wch), lambda j, t: (j, 0, 0)),
                  pl.BlockSpec((1, (CHUNK + 1) * LANE, hch), lambda j, t: (j, 0, 0)),
                  pl.BlockSpec((1, wch, 2 * n), lambda j, t: (j, 0, 0)),
                  pl.BlockSpec((1, 2 * n, wch), lambda j, t: (j, 0, 0)),
                  pl.BlockSpec((1, 2, nstj // 2), lambda j, t: (j, 0, 0))],
        out_specs=(pl.BlockSpec((1, nb, cc, wch), lambda j, t: (j, 0, t, 0)),
                   pl.BlockSpec((1, nb, 1, nstj), lambda j, t: (j, 0, 0, 0))),
        out_shape=(jax.ShapeDtypeStruct((nj, nb, n_chunks, wch), F32),
                   jax.ShapeDtypeStruct((nj, nb, 1, nstj), F32)),
        scratch_shapes=[pltpu.VMEM((nb, 1, nstj), F32), pltpu.VMEM((nb, cc, nstj), F32),
                        pltpu.VMEM((nb, cc, nstj), F32),
                        pltpu.VMEM((wch, nstj), BF16), pltpu.VMEM((nstj, wch), BF16),
                        pltpu.VMEM((CHUNK // 2, 2 * LANE, 2 * LANE), BF16)],
        **_params("ssm", ("parallel", "arbitrary")),
    )(u4c, um, kc, pc, rc, a16)
    sl = s_last.reshape(nj, nb, 2, gpt, n)
    new_re_p = sl[:, :, 0].transpose(1, 0, 2, 3).reshape(1, nb, g, n)
    new_im_p = sl[:, :, 1].transpose(1, 0, 2, 3).reshape(1, nb, g, n)

    dsk = ssm_d[0].reshape(1, cw)
    wglu_bf = w_glu[0].astype(BF16)
    bglu = b_glu[0].reshape(1, cw)
    nssm = norm_out_ssm[0].reshape(1, cw)
    wout_bf = w_out[0].astype(BF16)
    nffn = norm_ffn[0].reshape(1, d)
    wr_pad = jnp.zeros((d, LANE), F32).at[:, :ne].set(w_router[0])
    wr_hi = wr_pad.astype(BF16)
    wr_lo = (wr_pad - wr_hi.astype(F32)).astype(BF16)
    br = b_router[0].reshape(ne, 1)
    wr2 = jnp.concatenate([wr_hi, wr_lo], axis=1)
    mix_w = (dsk, wglu_bf, bglu, nssm, wout_bf, nffn, wr2, br)
    mix_w_specs = [_full((1, cw)), _full((cw, cw)), _full((1, cw)), _full((1, cw)), _full((2 * cw, d)),
                   _full((1, d)), _full((d, 2 * LANE)), _full((ne, 1))]
    assert 4 + len(mix_w) == N_MIX_IN
    mix_out_shape = (jax.ShapeDtypeStruct((ta, d), F32), jax.ShapeDtypeStruct((ta, d), BF16),
                     jax.ShapeDtypeStruct((TOP_K, ta), I32), jax.ShapeDtypeStruct((TOP_K, ta), F32),
                     jax.ShapeDtypeStruct((n_tiles, ne, 1), I32))
    tpm = rows_p // TOK_TILE
    nbg = nb // nbm
    x1_all, hf_all, idx_all, gate_all, cnt = pl.pallas_call(
        _mix_kernel_prompt,
        grid=(seq // tm, nbg),
        in_specs=[pl.BlockSpec((nbm, tm, d), lambda i, b: (b, i, 0)),
                  pl.BlockSpec((nj, nbm, tm // CHUNK, wch), lambda i, b: (0, b, i, 0)),
                  pl.BlockSpec((nbm, tm, cw), lambda i, b: (b, i, 0)),
                  pl.BlockSpec((nbm, tm, cw), lambda i, b: (b, i, 0))] + mix_w_specs,
        out_specs=(pl.BlockSpec((rows_p, d), lambda i, b: (i * nbg + b, 0)),
                   pl.BlockSpec((rows_p, d), lambda i, b: (i * nbg + b, 0)),
                   pl.BlockSpec((TOP_K, rows_p), lambda i, b: (0, i * nbg + b)),
                   pl.BlockSpec((TOP_K, rows_p), lambda i, b: (0, i * nbg + b)),
                   pl.BlockSpec((tpm, ne, 1), lambda i, b: (i * nbg + b, 0, 0))),
        out_shape=mix_out_shape,
        scratch_shapes=[pltpu.VMEM((nj, nbm, tm, LANE), F32)],
        **_params("mix_prompt", ("parallel", "parallel")),
    )(x_prompt, y4c, u_tok, ycn_p, *mix_w)

    last = n_tiles - 1
    any_spec = pl.BlockSpec(memory_space=pl.ANY)
    x1_all, hf_all, idx_all, gate_all, cnt = pl.pallas_call(
        _mix_kernel_sample,
        grid=(1,),
        in_specs=[_full((1, ns, d)), _full((ns, cw)), _full((1, ns, cw)),
                  _full((1, ns, cw))] + mix_w_specs + [any_spec] * 5,
        out_specs=(pl.BlockSpec((TOK_TILE, d), lambda i: (last, 0)),
                   pl.BlockSpec((TOK_TILE, d), lambda i: (last, 0)),
                   pl.BlockSpec((TOP_K, TOK_TILE), lambda i: (0, last)),
                   pl.BlockSpec((TOP_K, TOK_TILE), lambda i: (0, last)),
                   pl.BlockSpec((1, ne, 1), lambda i: (last, 0, 0))),
        out_shape=mix_out_shape,
        input_output_aliases={N_MIX_IN + k: k for k in range(5)},
        **_params("mix_sample", ("arbitrary",)),
    )(x_sample.reshape(1, ns, d), y_s, u_sm[:ns].astype(BF16).reshape(1, ns, cw),
      ycn_s.reshape(1, ns, cw), *mix_w, x1_all, hf_all, idx_all, gate_all, cnt)

    bm = MOE_ROWS
    cnt2 = cnt.reshape(n_tiles, ne)
    before = jnp.cumsum(cnt2, axis=0) - cnt2
    count = jnp.sum(cnt2, axis=0)
    padded = ((count + WIN + bm - 1) // bm) * bm
    pend = jnp.cumsum(padded)
    pstart = pend - padded
    phase = before % SUBLANE
    span = jnp.where(cnt2 > 0, phase + cnt2, 0)
    gstart = (pstart[None, :] + before - phase).astype(I32).reshape(-1)
    nwin = ((span + WIN - 1) // WIN).astype(I32)
    reg8 = ((span + SUBLANE - 1) // SUBLANE) * SUBLANE
    loff = (jnp.cumsum(reg8, axis=1) - reg8).astype(I32)
    tail = jnp.where(span % SUBLANE != 0, loff + (span // SUBLANE) * SUBLANE, -1).astype(I32)
    twin = jnp.sum(nwin, axis=1).astype(I32)
    n_blocks = (ta * TOP_K + ne * (WIN + bm - 1) + bm - 1) // bm
    cap = n_blocks * bm
    blk0 = jnp.arange(n_blocks, dtype=I32) * bm
    blk_e = jnp.minimum(jnp.sum((pend[None, :] <= blk0[:, None]).astype(I32), axis=1), ne - 1)
    e_ar = jnp.arange(ne, dtype=I32)
    blk_hot = blk_e[:, None] == e_ar[None, :]

    def _of_block(per_expert):
        return jnp.sum(jnp.where(blk_hot, per_expert[None, :], 0), axis=1)

    blk_valid = jnp.clip(_of_block(count) - (blk0 - _of_block(pstart)), 0, bm).astype(I32)
    has = count > 0
    later = jnp.logical_and(e_ar[None, :] > e_ar[:, None], has[None, :])
    nxt_e = jnp.min(jnp.where(later, e_ar[None, :], ne), axis=1)
    nxt_e = jnp.where(nxt_e < ne, nxt_e, -1)
    ordinal = jnp.cumsum(has.astype(I32)) - 1
    is_first = jnp.logical_and(blk_valid > 0, blk0 == _of_block(pstart))
    blk_first = jnp.where(is_first, jnp.where(_of_block(ordinal) == 0, 2, 1), 0).astype(I32)
    blk_next = _of_block(nxt_e).astype(I32)
    blk_slot = (_of_block(ordinal) % 2).astype(I32)
    nvalid = jnp.full((1,), tall, I32)
    loff_al = (WIN * (jnp.cumsum(nwin, axis=1) - nwin)).astype(I32)

    def _window_list(nw, slots, rows, first_hbm, first_buf):
        wcum = jnp.cumsum(nw, axis=1)
        wslot = jnp.arange(slots, dtype=I32)
        w_hot = jnp.logical_and(wslot[None, :, None] >= (wcum - nw)[:, None, :],
                                wslot[None, :, None] < wcum[:, None, :])

        def _of_window(per_tile_expert):
            return jnp.sum(jnp.where(w_hot, per_tile_expert[:, None, :], 0), axis=2)

        w_in_run = wslot[None, :] - _of_window(wcum - nw)
        return [(_of_window(f) + rows * w_in_run).astype(I32).reshape(-1) for f in (first_hbm, first_buf)]

    w_hbm, w_buf = _window_list(nwin, MAX_WINDOWS, WIN, gstart.reshape(n_tiles, ne), loff)
    nrun_d = _dispatch_run_rows(ne)
    zero_grp, spare_grp = nrun_d - 2 * SUBLANE, nrun_d - SUBLANE
    m_lo = jnp.where(nwin > 0, loff, spare_grp).astype(I32).reshape(-1)
    m_tg = jnp.where(jnp.logical_and(nwin > 0, tail >= 0), tail, zero_grp).astype(I32).reshape(-1)
    m_keep = (nwin == 0).astype(I32).reshape(-1)
    used_d = jnp.sum(reg8, axis=1).astype(I32)
    tab_d, off_d = _pack_tables((w_hbm, w_buf, m_lo, m_tg, m_keep, twin, used_d, nvalid))
    tab_c, off_c = _pack_tables((w_hbm, twin, nvalid))
    tab_m, off_m = _pack_tables((blk_e, blk_valid, blk_first, blk_next, blk_slot))
    rowoff_d = (loff + phase).astype(I32).reshape(n_tiles, ne, 1)
    rowoff_c = (loff_al + phase).astype(I32).reshape(n_tiles, ne, 1)
    rgn_d = jnp.stack([loff, loff + reg8], axis=1).astype(I32)
    rgn_c = jnp.stack([loff_al, loff_al + WIN * nwin], axis=1).astype(I32)
    xw = d // 2 + GATE_COLS

    xs = pl.pallas_call(
        functools.partial(_dispatch_kernel, off_d),
        grid_spec=pltpu.PrefetchScalarGridSpec(
            num_scalar_prefetch=1,
            grid=(n_tiles,),
            in_specs=[pl.BlockSpec((TOK_TILE, d), lambda i, *_: (i, 0)),
                      pl.BlockSpec((TOP_K, TOK_TILE), lambda i, *_: (0, i)),
                      pl.BlockSpec((TOP_K, TOK_TILE), lambda i, *_: (0, i)),
                      pl.BlockSpec((1, ne, 1), lambda i, *_: (i, 0, 0)),
                      pl.BlockSpec((1, 2, ne), lambda i, *_: (i, 0, 0))],
            out_specs=pl.BlockSpec(memory_space=pl.ANY),
            scratch_shapes=[pltpu.VMEM((2, _dispatch_run_rows(ne), xw), U32),
                            pltpu.VMEM((ne, SUBLANE, xw), U32), pltpu.SemaphoreType.DMA((2,))]),
        out_shape=jax.ShapeDtypeStruct((cap, xw), U32),
        **_params("dispatch", ("arbitrary",)),
    )(tab_d, hf_all, idx_all, gate_all, rowoff_d, rgn_d)

    yb = pl.pallas_call(
        functools.partial(_moe_kernel, off_m),
        grid_spec=pltpu.PrefetchScalarGridSpec(
            num_scalar_prefetch=1,
            grid=(n_blocks,),
            in_specs=[pl.BlockSpec(memory_space=pl.ANY),
                      pl.BlockSpec((1, 1, 2 * dff), lambda i, tab: (tab[off_m[0] + i], 0, 0)),
                      pl.BlockSpec((1, 1, d), lambda i, tab: (tab[off_m[0] + i], 0, 0)),
                      pl.BlockSpec(memory_space=pl.ANY), pl.BlockSpec(memory_space=pl.ANY)],
            out_specs=pl.BlockSpec((bm, d // 2), lambda i, *_: (i, 0)),
            scratch_shapes=[pltpu.VMEM((2, d, 2 * dff), F32), pltpu.VMEM((2, dff, d), F32),
                            pltpu.VMEM((d, 2 * dff), BF16), pltpu.VMEM((dff, d), BF16),
                            pltpu.SemaphoreType.DMA((2, 2)),
                            pltpu.VMEM((MOE_X_SLOTS, bm, xw), U32), pltpu.SemaphoreType.DMA((MOE_X_SLOTS,))]),
        out_shape=jax.ShapeDtypeStruct((cap, d // 2), U32),
        **_params("moe", ("arbitrary",)),
    )(tab_m, xs, b_gate_up[0].reshape(ne, 1, 2 * dff),
      b_down[0].reshape(ne, 1, d), w_gate_up[0], w_down[0])

    nfin = norm_final.reshape(1, d)
    nbh = TOK_TILE // tm

    def _tile_of(s):
        return (s + n_tiles - 1) % n_tiles

    tiles_per_time = nbg * tpm

    def _yp_index(s, *_):
        t = jnp.maximum(s - 1, 0)
        return (t % tiles_per_time, t // tiles_per_time, 0)

    y_p, y_sm = pl.pallas_call(
        functools.partial(_combine_kernel, off_c),
        grid_spec=pltpu.PrefetchScalarGridSpec(
            num_scalar_prefetch=1,
            grid=(n_tiles,),
            in_specs=[pl.BlockSpec((TOK_TILE, d), lambda s, *_: (_tile_of(s), 0)),
                      pl.BlockSpec((TOP_K, TOK_TILE), lambda s, *_: (0, _tile_of(s))),
                      pl.BlockSpec((1, ne, 1), lambda s, *_: (_tile_of(s), 0, 0)),
                      pl.BlockSpec((1, 2, ne), lambda s, *_: (_tile_of(s), 0, 0)),
                      pl.BlockSpec((1, d), lambda s, *_: (0, 0)),
                      pl.BlockSpec(memory_space=pl.ANY)],
            out_specs=(pl.BlockSpec((nbh, tm, d), _yp_index),
                       pl.BlockSpec((ns, d), lambda s, *_: (0, 0))),
            scratch_shapes=[pltpu.VMEM((2, _combine_run_rows(ne), d // 2), U32),
                            pltpu.VMEM((TOK_TILE, d), F32), pltpu.SemaphoreType.DMA((2,))]),
        out_shape=(jax.ShapeDtypeStruct((nb, seq, d), F32), jax.ShapeDtypeStruct((ns, d), F32)),
        **_params("combine", ("arbitrary",)),
    )(tab_c, x1_all, idx_all, rowoff_c, rgn_c, nfin, yb)

    return (y_p, y_sm.reshape(ns, 1, d), new_re_p, new_im_p, new_conv_p,
            new_re_s, new_im_s, new_conv_s)
```
